```python
import math
import jax
import jax.numpy as jnp
from jax import lax
import numpy as np

D_MODEL = 1024
BATCH = 8
SEQ = 2048
DEPTH = 1

CTX_LEN = 256
GRID_W = 64
D_MIX = D_MODEL
CHUNK = 128
EPS = 1e-6
RET_HEADS = 4
RET_WIDTH = D_MIX // 2
RET_DK = RET_WIDTH // RET_HEADS
RET_DV = RET_WIDTH // RET_HEADS
ROPE_BASE = 10000.0
SSD_WIDTH = D_MIX - RET_WIDTH
SSD_HEADDIM = 64
SSD_HEADS = SSD_WIDTH // SSD_HEADDIM
SSD_GROUPS = 2
SSD_HPG = SSD_HEADS // SSD_GROUPS
SSD_STATE = 128
SSD_CONV = 5
SSD_CONV_CH = SSD_WIDTH + 2 * SSD_GROUPS * SSD_STATE
D_PROJ = 4 * RET_WIDTH + SSD_WIDTH + SSD_CONV_CH + 2 * SSD_HEADS
MOE_GROUPS = 4
EXPERTS_PER_GROUP = 8
N_EXPERTS = MOE_GROUPS * EXPERTS_PER_GROUP
TOP_K = 2
D_EXPERT = D_MODEL // 2
MOE_BLOCK = 128

kernel_name = "hybrid_retention_ssd_hmoe_prefix_dit"


def rmsnorm(x, w):
    xf = x.astype(jnp.float32)
    y = xf * lax.rsqrt(jnp.mean(xf * xf, axis=-1, keepdims=True) + EPS)
    return (y * w.astype(jnp.float32)).astype(x.dtype)


def head_norm(y, w):
    b, L, H, d = y.shape
    yf = y.astype(jnp.float32)
    mu = jnp.mean(yf, axis=-1, keepdims=True)
    var = jnp.mean(jnp.square(yf - mu), axis=-1, keepdims=True)
    yn = ((yf - mu) * lax.rsqrt(var + EPS)).reshape(b, L, H * d)
    return (yn * w.astype(jnp.float32)).astype(y.dtype)


def split_proj(p):
    parts = []
    start = 0
    for size in (RET_WIDTH, RET_WIDTH, RET_WIDTH, RET_WIDTH, SSD_WIDTH, SSD_CONV_CH, SSD_HEADS, SSD_HEADS):
        parts.append(p[..., start:start + size])
        start += size
    return parts


def axial_rope(t, row_id, col_id):
    half = t.shape[-1] // 2
    quarter = half // 2
    freqs = ROPE_BASE ** (-jnp.arange(quarter, dtype=jnp.float32) / quarter)

    def rotate(u, pos):
        ang = pos.astype(jnp.float32)[:, None] * freqs
        cos = jnp.cos(ang)[None, :, None, :]
        sin = jnp.sin(ang)[None, :, None, :]
        u1 = u[..., :quarter].astype(jnp.float32)
        u2 = u[..., quarter:].astype(jnp.float32)
        return jnp.concatenate([u1 * cos - u2 * sin, u1 * sin + u2 * cos], axis=-1).astype(u.dtype)

    return jnp.concatenate([rotate(t[..., :half], row_id), rotate(t[..., half:], col_id)], axis=-1)


def to_chunks(t):
    b, L = t.shape[:2]
    return jnp.moveaxis(t.reshape(b, L // CHUNK, CHUNK, *t.shape[2:]), 1, 0)


def from_chunks(t):
    n, b, c = t.shape[:3]
    return jnp.moveaxis(t, 0, 1).reshape(b, n * c, *t.shape[3:])


def flip_seq(t):
    return jnp.flip(t, axis=1)


def retention_scan(q, k, v, log_g, s0, strict):
    dt = q.dtype
    idx = jnp.arange(CHUNK, dtype=jnp.float32)
    rel = idx[:, None] - idx[None, :]
    mask = (rel > 0) if strict else (rel >= 0)
    decay_in = jnp.where(mask[..., None], jnp.exp(jnp.maximum(rel, 0.0)[..., None] * log_g), 0.0)
    decay_in = jnp.moveaxis(decay_in, -1, 0).astype(dt)
    decay_q = jnp.exp((idx + 1.0)[:, None] * log_g).astype(dt)
    decay_k = jnp.exp((CHUNK - 1.0 - idx)[:, None] * log_g).astype(dt)
    decay_chunk = jnp.exp(CHUNK * log_g).astype(dt)

    def step(s, inp):
        qc, kc, vc = inp
        scores = jnp.einsum('bnhd,bmhd->bhnm', qc, kc) * decay_in
        inner = jnp.einsum('bhnm,bmhe->bnhe', scores, vc)
        cross = jnp.einsum('bnhd,bhde->bnhe', qc, s) * decay_q[None, :, :, None]
        s_new = decay_chunk[None, :, None, None] * s + jnp.einsum('bmhd,bmhe,mh->bhde', kc, vc, decay_k)
        return s_new.astype(dt), inner + cross

    s_fin, y = lax.scan(step, s0.astype(dt), (to_chunks(q), to_chunks(k), to_chunks(v)))
    return from_chunks(y), s_fin


def retention_state(k, v, log_g):
    L = k.shape[1]
    w = jnp.exp((L - 1.0 - jnp.arange(L, dtype=jnp.float32))[:, None] * log_g).astype(k.dtype)
    return jnp.einsum('bmhd,bmhe,mh->bhde', k, v, w)


def retention_group(q, k, v, g, qc, kc, vc, gc, decay_f, decay_b, gn_w, need_ctx):
    b = q.shape[0]
    lg_f = -jnp.exp(decay_f.astype(jnp.float32))
    lg_b = -jnp.exp(decay_b.astype(jnp.float32))
    if need_ctx:
        s0 = jnp.zeros((b, RET_HEADS, RET_DK, RET_DV), q.dtype)
        yc_f, s_f = retention_scan(qc, kc, vc, lg_f, s0, False)
        yc_b, s_b = retention_scan(flip_seq(qc), flip_seq(kc), flip_seq(vc), lg_b, s0, True)
        yc = head_norm(yc_f + flip_seq(yc_b), gn_w) * jax.nn.silu(gc)
    else:
        s_f = retention_state(kc, vc, lg_f)
        s_b = retention_state(flip_seq(kc), flip_seq(vc), lg_b)
        yc = None
    y_f, _ = retention_scan(q, k, v, lg_f, s_f, False)
    y_b, _ = retention_scan(flip_seq(q), flip_seq(k), flip_seq(v), lg_b, s_b, True)
    y = head_norm(y_f + flip_seq(y_b), gn_w) * jax.nn.silu(g)
    return y, yc


def short_conv(u, w, bias):
    pad = SSD_CONV // 2
    out = lax.conv_general_dilated(u, w[:, None, :].astype(u.dtype), window_strides=(1,),
                                   padding=[(pad, pad)], dimension_numbers=('NWC', 'WIO', 'NWC'),
                                   feature_group_count=u.shape[-1])
    return out + bias


def ssd_inputs(xbc, dt_f_raw, dt_b_raw, conv_w, conv_b, dt_bias_f, dt_bias_b, a_f, a_b):
    u = jax.nn.silu(short_conv(xbc, conv_w, conv_b))
    b, L, _ = u.shape
    gn = SSD_GROUPS * SSD_STATE
    xs = u[..., :SSD_WIDTH].reshape(b, L, SSD_GROUPS, SSD_HPG, SSD_HEADDIM)
    bm = u[..., SSD_WIDTH:SSD_WIDTH + gn].reshape(b, L, SSD_GROUPS, SSD_STATE)
    cm = u[..., SSD_WIDTH + gn:].reshape(b, L, SSD_GROUPS, SSD_STATE)

    def discretise(raw, bias, a):
        dt = jax.nn.softplus(raw.astype(jnp.float32) + bias.astype(jnp.float32))
        dt = dt.reshape(b, L, SSD_GROUPS, SSD_HPG)
        return dt, dt * a

    dt_f, da_f = discretise(dt_f_raw, dt_bias_f, a_f)
    dt_b, da_b = discretise(dt_b_raw, dt_bias_b, a_b)
    return xs, bm, cm, dt_f, da_f, dt_b, da_b


def ssd_scan(xs, dt, da, bm, cm, s0, strict):
    dtype = xs.dtype
    idx = jnp.arange(CHUNK)
    mask = (idx[:, None] > idx[None, :]) if strict else (idx[:, None] >= idx[None, :])
    dtx = dt.astype(dtype)[..., None] * xs

    def step(s, inp):
        xc, ac, bc, cc = inp
        acum = jnp.cumsum(ac, axis=1)
        seg = acum[:, :, None] - acum[:, None, :]
        lmat = jnp.exp(jnp.where(mask[None, :, :, None, None], seg, -jnp.inf)).astype(dtype)
        cb = jnp.einsum('bngs,bmgs->bgnm', cc, bc)
        y_in = jnp.einsum('bgnm,bnmgr,bmgrp->bngrp', cb, lmat, xc)
        y_st = jnp.einsum('bngs,bgrps->bngrp', cc, s) * jnp.exp(acum).astype(dtype)[..., None]
        w = jnp.exp(acum[:, -1:] - acum).astype(dtype)
        s_new = jnp.exp(acum[:, -1]).astype(dtype)[..., None, None] * s + jnp.einsum('bmgs,bmgr,bmgrp->bgrps', bc, w, xc)
        return s_new.astype(dtype), y_in + y_st

    s_fin, y = lax.scan(step, s0.astype(dtype), (to_chunks(dtx), to_chunks(da), to_chunks(bm), to_chunks(cm)))
    return from_chunks(y), s_fin


def ssd_state(xs, dt, da, bm):
    dtx = dt.astype(xs.dtype)[..., None] * xs
    acum = jnp.cumsum(da, axis=1)
    w = jnp.exp(acum[:, -1:] - acum).astype(xs.dtype)
    return jnp.einsum('bmgs,bmgr,bmgrp->bgrps', bm, w, dtx)


def ssd_output(y, xs, z, d_skip, norm_w):
    b, L = y.shape[:2]
    y = (y + d_skip.reshape(SSD_GROUPS, SSD_HPG)[:, :, None].astype(y.dtype) * xs).reshape(b, L, SSD_WIDTH)
    return rmsnorm(y * jax.nn.silu(z), norm_w)


def ssd_group(z, xbc, dtf, dtb, zc, xbcc, dtfc, dtbc, conv_w, conv_b, dt_bias_f, dt_bias_b,
              a_log_f, a_log_b, d_skip, norm_w, need_ctx):
    a_f = -jnp.exp(a_log_f.astype(jnp.float32)).reshape(SSD_GROUPS, SSD_HPG)
    a_b = -jnp.exp(a_log_b.astype(jnp.float32)).reshape(SSD_GROUPS, SSD_HPG)
    xs, bm, cm, dt_f, da_f, dt_b, da_b = ssd_inputs(xbc, dtf, dtb, conv_w, conv_b, dt_bias_f, dt_bias_b, a_f, a_b)
    xsc, bmc, cmc, dtc_f, dac_f, dtc_b, dac_b = ssd_inputs(xbcc, dtfc, dtbc, conv_w, conv_b, dt_bias_f, dt_bias_b, a_f, a_b)
    if need_ctx:
        b = xs.shape[0]
        s0 = jnp.zeros((b, SSD_GROUPS, SSD_HPG, SSD_HEADDIM, SSD_STATE), xs.dtype)
        yc_f, s_f = ssd_scan(xsc, dtc_f, dac_f, bmc, cmc, s0, False)
        yc_b, s_b = ssd_scan(flip_seq(xsc), flip_seq(dtc_b), flip_seq(dac_b), flip_seq(bmc), flip_seq(cmc), s0, True)
        yc = ssd_output(yc_f + flip_seq(yc_b), xsc, zc, d_skip, norm_w)
    else:
        s_f = ssd_state(xsc, dtc_f, dac_f, bmc)
        s_b = ssd_state(flip_seq(xsc), flip_seq(dtc_b), flip_seq(dac_b), flip_seq(bmc))
        yc = None
    y_f, _ = ssd_scan(xs, dt_f, da_f, bm, cm, s_f, False)
    y_b, _ = ssd_scan(flip_seq(xs), flip_seq(dt_b), flip_seq(da_b), flip_seq(bm), flip_seq(cm), s_b, True)
    y = ssd_output(y_f + flip_seq(y_b), xs, z, d_skip, norm_w)
    return y, yc


def token_mixing(h, hc, w_in, w_out, ret_decay_f, ret_decay_b, ret_gn_w, conv_w, conv_b,
                 dt_bias_f, dt_bias_b, a_log_f, a_log_b, d_skip, ssd_norm_w, row_id, col_id, need_ctx):
    b, L, _ = h.shape
    lc = hc.shape[1]
    q, k, v, g, z, xbc, dtf, dtb = split_proj(h @ w_in)
    qc, kc, vc, gc, zc, xbcc, dtfc, dtbc = split_proj(hc @ w_in)
    scale = RET_DK ** -0.5
    q = axial_rope(q.reshape(b, L, RET_HEADS, RET_DK), row_id, col_id)
    k = axial_rope(k.reshape(b, L, RET_HEADS, RET_DK), row_id, col_id) * scale
    v = v.reshape(b, L, RET_HEADS, RET_DV)
    qc = qc.reshape(b, lc, RET_HEADS, RET_DK)
    kc = kc.reshape(b, lc, RET_HEADS, RET_DK) * scale
    vc = vc.reshape(b, lc, RET_HEADS, RET_DV)
    yr, yrc = retention_group(q, k, v, g, qc, kc, vc, gc, ret_decay_f, ret_decay_b, ret_gn_w, need_ctx)
    ys, ysc = ssd_group(z, xbc, dtf, dtb, zc, xbcc, dtfc, dtbc, conv_w, conv_b, dt_bias_f, dt_bias_b,
                        a_log_f, a_log_b, d_skip, ssd_norm_w, need_ctx)
    y = jnp.concatenate([yr, ys], axis=-1) @ w_out
    yc = jnp.concatenate([yrc, ysc], axis=-1) @ w_out if need_ctx else None
    return y, yc


def hier_moe(h, w_rg, b_rg, w_re, b_re, w_gate, w_up, w_down):
    T, d = h.shape
    lg = (h @ w_rg + b_rg).astype(jnp.float32)
    grp = jnp.argmax(lg, axis=-1)
    gate_g = jnp.take_along_axis(jax.nn.softmax(lg, axis=-1), grp[:, None], axis=1)[:, 0]
    le = (jnp.einsum('td,gde->tge', h, w_re) + b_re).astype(jnp.float32)
    le_sel = jnp.take_along_axis(le, grp[:, None, None], axis=1)[:, 0]
    top_v, top_i = lax.top_k(le_sel, TOP_K)
    w_top = jax.nn.softmax(top_v, axis=-1) * gate_g[:, None]
    expert = grp[:, None] * EXPERTS_PER_GROUP + top_i

    tk = T * TOP_K
    flat_e = expert.reshape(-1)
    flat_t = jnp.repeat(jnp.arange(T, dtype=jnp.int32), TOP_K)
    flat_w = w_top.reshape(-1)
    order = jnp.argsort(flat_e)
    se, st, sw = flat_e[order], flat_t[order], flat_w[order]
    counts = jnp.bincount(flat_e, length=N_EXPERTS)
    starts = jnp.cumsum(counts) - counts
    padded = (counts + MOE_BLOCK - 1) // MOE_BLOCK * MOE_BLOCK
    pad_end = jnp.cumsum(padded)
    pad_start = pad_end - padded
    dest = pad_start[se] + (jnp.arange(tk) - starts[se])
    n_blocks = -(-(tk + N_EXPERTS * (MOE_BLOCK - 1)) // MOE_BLOCK)
    cap = n_blocks * MOE_BLOCK
    slot_tok = jnp.full((cap,), T, jnp.int32).at[dest].set(st)
    slot_w = jnp.zeros((cap,), h.dtype).at[dest].set(sw.astype(h.dtype))
    blk_expert = jnp.minimum(jnp.searchsorted(pad_end, jnp.arange(n_blocks) * MOE_BLOCK, side='right'), N_EXPERTS - 1)
    h_pad = jnp.concatenate([h, jnp.zeros((1, d), h.dtype)], axis=0)

    def expert_block(args):
        tok, e = args
        xb = h_pad[tok]
        hid = jax.nn.silu(xb @ w_gate[e]) * (xb @ w_up[e])
        return hid @ w_down[e]

    yb = lax.map(expert_block, (slot_tok.reshape(n_blocks, MOE_BLOCK), blk_expert))
    out = jnp.zeros((T + 1, d), h.dtype).at[slot_tok].add(yb.reshape(cap, d) * slot_w[:, None])
    return out[:T]


def setup_inputs(seed: int = 0) -> dict:
    key = jax.random.key(seed)
    ks = iter(jax.random.split(key, 48))
    f32 = jnp.float32

    def nrm(shape, scale):
        return jax.random.normal(next(ks), shape, f32) * scale

    def gain(shape):
        return 1.0 + 0.05 * jax.random.normal(next(ks), shape, f32)

    ret_base = jnp.log(-jnp.log1p(-(2.0 ** (-5.0 - jnp.arange(RET_HEADS, dtype=f32)))))

    def dt_bias():
        dt = jnp.exp(jax.random.uniform(next(ks), (DEPTH, SSD_HEADS), f32, math.log(1e-3), math.log(1e-1)))
        return dt + jnp.log(-jnp.expm1(-dt))

    def a_log():
        return jnp.log(jax.random.uniform(next(ks), (DEPTH, SSD_HEADS), f32, 1.0, 16.0))

    return {
        'x': nrm((BATCH, SEQ, D_MODEL), 1.0),
        'c': nrm((BATCH, D_MODEL), 1.0),
        'ctx': nrm((BATCH, CTX_LEN, D_MODEL), 1.0),
        'c_ctx': nrm((D_MODEL,), 1.0),
        'w_mod': nrm((DEPTH, D_MODEL, 6 * D_MODEL), 0.5 * D_MODEL ** -0.5),
        'b_mod': nrm((DEPTH, 6 * D_MODEL), 0.02),
        'norm_pre_mix': gain((DEPTH, D_MODEL)),
        'norm_post_mix': gain((DEPTH, D_MODEL)),
        'norm_pre_ffn': gain((DEPTH, D_MODEL)),
        'norm_post_ffn': gain((DEPTH, D_MODEL)),
        'w_in': nrm((DEPTH, D_MODEL, D_PROJ), D_MODEL ** -0.5),
        'w_out': nrm((DEPTH, D_MIX, D_MODEL), D_MIX ** -0.5),
        'ret_decay_f': ret_base + nrm((DEPTH, RET_HEADS), 0.1),
        'ret_decay_b': ret_base + nrm((DEPTH, RET_HEADS), 0.1),
        'ret_gn_w': gain((DEPTH, RET_WIDTH)),
        'ssd_conv_w': nrm((DEPTH, SSD_CONV, SSD_CONV_CH), SSD_CONV ** -0.5),
        'ssd_conv_b': nrm((DEPTH, SSD_CONV_CH), 0.02),
        'ssd_dt_bias_f': dt_bias(),
        'ssd_dt_bias_b': dt_bias(),
        'ssd_a_log_f': a_log(),
        'ssd_a_log_b': a_log(),
        'ssd_d': gain((DEPTH, SSD_HEADS)),
        'ssd_norm_w': gain((DEPTH, SSD_WIDTH)),
        'moe_w_rg': nrm((DEPTH, D_MODEL, MOE_GROUPS), D_MODEL ** -0.5),
        'moe_b_rg': nrm((DEPTH, MOE_GROUPS), 0.01),
        'moe_w_re': nrm((DEPTH, MOE_GROUPS, D_MODEL, EXPERTS_PER_GROUP), D_MODEL ** -0.5),
        'moe_b_re': nrm((DEPTH, MOE_GROUPS, EXPERTS_PER_GROUP), 0.01),
        'moe_w_gate': nrm((DEPTH, N_EXPERTS, D_MODEL, D_EXPERT), D_MODEL ** -0.5),
        'moe_w_up': nrm((DEPTH, N_EXPERTS, D_MODEL, D_EXPERT), D_MODEL ** -0.5),
        'moe_w_down': nrm((DEPTH, N_EXPERTS, D_EXPERT, D_MODEL), D_EXPERT ** -0.5),
    }


def reference(x, c, ctx, c_ctx, w_mod, b_mod, norm_pre_mix, norm_post_mix, norm_pre_ffn, norm_post_ffn,
              w_in, w_out, ret_decay_f, ret_decay_b, ret_gn_w, ssd_conv_w, ssd_conv_b,
              ssd_dt_bias_f, ssd_dt_bias_b, ssd_a_log_f, ssd_a_log_b, ssd_d, ssd_norm_w,
              moe_w_rg, moe_b_rg, moe_w_re, moe_b_re, moe_w_gate, moe_w_up, moe_w_down):
    b, L, d = x.shape
    lc = ctx.shape[1]
    rows = L // GRID_W
    row_id = jnp.repeat(jnp.arange(rows), GRID_W)
    col_id = jnp.tile(jnp.arange(GRID_W), rows)
    act_c = jax.nn.silu(c)
    act_cc = jax.nn.silu(c_ctx)
    for l in range(DEPTH):
        need_ctx = l < DEPTH - 1
        mod = act_c @ w_mod[l] + b_mod[l]
        modc = act_cc @ w_mod[l] + b_mod[l]
        sh1, sc1, g1, sh2, sc2, g2 = [m[:, None, :] for m in jnp.split(mod, 6, axis=-1)]
        sh1c, sc1c, g1c, sh2c, sc2c, g2c = jnp.split(modc, 6, axis=-1)

        h = rmsnorm(x, norm_pre_mix[l]) * (1.0 + sc1) + sh1
        hc = rmsnorm(ctx, norm_pre_mix[l]) * (1.0 + sc1c) + sh1c
        y, yc = token_mixing(h, hc, w_in[l], w_out[l], ret_decay_f[l], ret_decay_b[l], ret_gn_w[l],
                             ssd_conv_w[l], ssd_conv_b[l], ssd_dt_bias_f[l], ssd_dt_bias_b[l],
                             ssd_a_log_f[l], ssd_a_log_b[l], ssd_d[l], ssd_norm_w[l], row_id, col_id, need_ctx)
        x = x + g1 * rmsnorm(y, norm_post_mix[l])

        h = rmsnorm(x, norm_pre_ffn[l]) * (1.0 + sc2) + sh2
        tokens = h.reshape(b * L, d)
        if need_ctx:
            ctx = ctx + g1c * rmsnorm(yc, norm_post_mix[l])
            hc = rmsnorm(ctx, norm_pre_ffn[l]) * (1.0 + sc2c) + sh2c
            tokens = jnp.concatenate([tokens, hc.reshape(b * lc, d)], axis=0)
        out = hier_moe(tokens, moe_w_rg[l], moe_b_rg[l], moe_w_re[l], moe_b_re[l],
                       moe_w_gate[l], moe_w_up[l], moe_w_down[l])
        x = x + g2 * rmsnorm(out[:b * L].reshape(b, L, d), norm_post_ffn[l])
        if need_ctx:
            ctx = ctx + g2c * rmsnorm(out[b * L:].reshape(b, lc, d), norm_post_ffn[l])
    return x
```

```python
import functools
import math

import jax
import jax.numpy as jnp
from jax import lax
from jax.experimental import pallas as pl
from jax.experimental.pallas import tpu as pltpu

F32 = jnp.float32
BF16 = jnp.bfloat16
HIGHEST = lax.Precision.HIGHEST

LANES = 128
SUBLANES = 8

EPS = 1e-6
CHUNK = 128
GRID_W = 64
RET_HEADS = 4
RET_DK = 128
ROPE_BASE = 10000.0
SSD_HEADS = 8
SSD_HEADDIM = 64
SSD_GROUPS = 2
SSD_STATE = 128
SSD_WIDTH = SSD_HEADS * SSD_HEADDIM
SSD_CONV = 5
SSD_PAIRS = SSD_WIDTH // LANES
MOE_GROUPS = 4
EXPERTS_PER_GROUP = 8
N_EXPERTS = MOE_GROUPS * EXPERTS_PER_GROUP
TOP_K = 2
CONV_HALO = SUBLANES

TM_PROJ = 512
TM_OUT = 512
TD_DISPATCH = 512
MB_EXPERT = 256
TF_COMBINE = 256
NEG_BIG = -1e30


def _silu(v):
    return v * jax.nn.sigmoid(v)


def _dot(a, b):
    return jnp.dot(a, b, preferred_element_type=F32)


def _dot_tn(a, b):
    return lax.dot_general(a, b, (((0,), (0,)), ((), ())), preferred_element_type=F32)


def _dot_nt(a, b):
    return lax.dot_general(a, b, (((1,), (1,)), ((), ())), preferred_element_type=F32)


def _mod_kernel(c_ref, w_ref, b_ref, o_ref):
    a = _silu(c_ref[...])
    o_ref[...] = jnp.dot(a, w_ref[...], precision=HIGHEST, preferred_element_type=F32) + b_ref[...]


def _modulation(c_all, w_mod, b_mod):
    rows, d = c_all.shape
    n = w_mod.shape[1]
    return pl.pallas_call(
        _mod_kernel,
        grid=(n // d,),
        in_specs=[
            pl.BlockSpec((rows, d), lambda j: (0, 0)),
            pl.BlockSpec((d, d), lambda j: (0, j)),
            pl.BlockSpec((1, d), lambda j: (0, j)),
        ],
        out_specs=pl.BlockSpec((rows, d), lambda j: (0, j)),
        out_shape=jax.ShapeDtypeStruct((rows, n), F32),
        name="modulation",
    )(c_all, w_mod, b_mod.reshape(1, n))


def _norm_mod(x, nw, sc, sh):
    ms = jnp.mean(x * x, axis=-1, keepdims=True)
    return (x * lax.rsqrt(ms + EPS)) * nw * (1.0 + sc) + sh


def _rope(t, cos, sin_signed, first_half):
    width = t.shape[-1]
    quarter = RET_DK // 4
    swapped = jnp.where(first_half, pltpu.roll(t, width - quarter, 1), pltpu.roll(t, quarter, 1))
    return t * cos + swapped * sin_signed


def _inproj_kernel(x_ref, sh_ref, sc_ref, nw_ref, wqk_ref, wvgz_ref, wxbc_ref, wdt_ref, cos_ref, sin_ref,
                   q_ref, k_ref, v_ref, g_ref, z_ref, xbc_ref, dt_ref):
    hb = _norm_mod(x_ref[0], nw_ref[...], sc_ref[0], sh_ref[0]).astype(BF16)
    rw = q_ref.shape[-1]
    qk = _dot(hb, wqk_ref[...])
    cos = cos_ref[...]
    sin = sin_ref[...]
    lane = lax.broadcasted_iota(jnp.int32, cos.shape, 1)
    first_half = (lane % (RET_DK // 2)) < (RET_DK // 4)
    q_ref[0] = _rope(qk[:, :rw], cos, sin, first_half).astype(BF16)
    k_ref[0] = (_rope(qk[:, rw:], cos, sin, first_half) * (RET_DK ** -0.5)).astype(BF16)
    vgz = _dot(hb, wvgz_ref[...])
    v_ref[0] = vgz[:, :rw].astype(BF16)
    g_ref[0] = vgz[:, rw:2 * rw].astype(BF16)
    z_ref[0] = vgz[:, 2 * rw:].astype(BF16)
    xbc_ref[0] = _dot(hb, wxbc_ref[...]).astype(BF16)
    dt_ref[0] = _dot(hb, wdt_ref[...])


def _inproj_ctx_kernel(x_ref, sh_ref, sc_ref, nw_ref, wk_ref, wv_ref, wxbc_ref, wdt_ref,
                       k_ref, v_ref, xbc_ref, dt_ref):
    hb = _norm_mod(x_ref[0], nw_ref[...], sc_ref[0], sh_ref[0]).astype(BF16)
    k_ref[0] = (_dot(hb, wk_ref[...]) * (RET_DK ** -0.5)).astype(BF16)
    v_ref[0] = _dot(hb, wv_ref[...]).astype(BF16)
    xbc_ref[0] = _dot(hb, wxbc_ref[...]).astype(BF16)
    dt_ref[0] = _dot(hb, wdt_ref[...])


def _const_spec(shape):
    nd = len(shape)
    return pl.BlockSpec(shape, lambda *_: (0,) * nd)


def _inproj(x, mod3, nw, wqk, wvgz, wxbc, wdt, cos_t, sin_t):
    b, L, d = x.shape
    tm = min(TM_PROJ, L)
    rw = wqk.shape[1] // 2
    tok = lambda w: pl.BlockSpec((1, tm, w), lambda i, j: (i, j, 0))
    out_bf = lambda w: jax.ShapeDtypeStruct((b, L, w), BF16)
    return pl.pallas_call(
        _inproj_kernel,
        grid=(b, L // tm),
        in_specs=[
            tok(d),
            pl.BlockSpec((1, 1, d), lambda i, j: (i, 0, 0)),
            pl.BlockSpec((1, 1, d), lambda i, j: (i, 0, 1)),
            _const_spec((1, d)),
            _const_spec(wqk.shape), _const_spec(wvgz.shape), _const_spec(wxbc.shape), _const_spec(wdt.shape),
            pl.BlockSpec((tm, rw), lambda i, j: (j, 0)),
            pl.BlockSpec((tm, rw), lambda i, j: (j, 0)),
        ],
        out_specs=[tok(rw), tok(rw), tok(rw), tok(rw), tok(rw), tok(wxbc.shape[1]), tok(LANES)],
        out_shape=[out_bf(rw), out_bf(rw), out_bf(rw), out_bf(rw), out_bf(rw), out_bf(wxbc.shape[1]),
                   jax.ShapeDtypeStruct((b, L, LANES), F32)],
        compiler_params=pltpu.CompilerParams(vmem_limit_bytes=48 * 1024 * 1024),
        name="inproj",
    )(x, mod3, mod3, nw, wqk, wvgz, wxbc, wdt, cos_t, sin_t)


def _inproj_ctx(ctx, mod3, ctx_row, nw, wk, wv, wxbc, wdt):
    b, L, d = ctx.shape
    tm = min(TM_PROJ, L)
    rw = wk.shape[1]
    tok = lambda w: pl.BlockSpec((1, tm, w), lambda i, j: (i, j, 0))
    out_bf = lambda w: jax.ShapeDtypeStruct((b, L, w), BF16)
    return pl.pallas_call(
        _inproj_ctx_kernel,
        grid=(b, L // tm),
        in_specs=[
            tok(d),
            pl.BlockSpec((1, 1, d), lambda i, j: (ctx_row, 0, 0)),
            pl.BlockSpec((1, 1, d), lambda i, j: (ctx_row, 0, 1)),
            _const_spec((1, d)),
            _const_spec(wk.shape), _const_spec(wv.shape), _const_spec(wxbc.shape), _const_spec(wdt.shape),
        ],
        out_specs=[tok(rw), tok(rw), tok(wxbc.shape[1]), tok(LANES)],
        out_shape=[out_bf(rw), out_bf(rw), out_bf(wxbc.shape[1]), jax.ShapeDtypeStruct((b, L, LANES), F32)],
        compiler_params=pltpu.CompilerParams(vmem_limit_bytes=48 * 1024 * 1024),
        name="inproj_ctx",
    )(ctx, mod3, mod3, nw, wk, wv, wxbc, wdt)


def _ssd_kernel(xbc_ref, z_ref, dt_ref, xbcc_ref, dtc_ref, cw_ref, cb_ref, dtb_ref, alog_ref, dsk_ref, nw_ref,
                y_ref,
                xpad, xpadc, u, uc, dtv, dav, dtcv, dacv, sf_scr, sb_scr):
    L = xbc_ref.shape[1]
    Lc = xbcc_ref.shape[1]
    nch = L // CHUNK
    nchc = Lc // CHUNK
    win = CHUNK + 2 * CONV_HALO
    nconv = xbc_ref.shape[2]
    nh = SSD_HEADS

    def conv_pass(src_ref, pad_ref, dst_ref, n_chunks, length):
        zeros = jnp.zeros((CONV_HALO, nconv), F32)
        pad_ref[0:CONV_HALO, :] = zeros
        pad_ref[CONV_HALO + length:2 * CONV_HALO + length, :] = zeros
        pad_ref[CONV_HALO:CONV_HALO + length, :] = src_ref[0].astype(F32)

        def chunk(c, carry):
            base = pl.multiple_of(c * CHUNK, CHUNK)
            for cb_i in range(nconv // LANES):
                cols = slice(cb_i * LANES, (cb_i + 1) * LANES)
                w = pad_ref[pl.ds(base, win), cols]
                acc = cb_ref[:, cols] + w[CONV_HALO:CONV_HALO + CHUNK] * cw_ref[SSD_CONV // 2:SSD_CONV // 2 + 1, cols]
                for j in range(SSD_CONV):
                    if j == SSD_CONV // 2:
                        continue
                    shifted = pltpu.roll(w, (SSD_CONV // 2 - j) % win, 0)
                    acc = acc + shifted[CONV_HALO:CONV_HALO + CHUNK] * cw_ref[j:j + 1, cols]
                dst_ref[pl.ds(base, CHUNK), cols] = _silu(acc).astype(BF16)
            return carry

        lax.fori_loop(0, n_chunks, chunk, 0)

    conv_pass(xbcc_ref, xpadc, uc, nchc, Lc)
    conv_pass(xbc_ref, xpad, u, nch, L)

    a_neg = -jnp.exp(alog_ref[...])
    dtv[...] = jax.nn.softplus(dt_ref[0] + dtb_ref[...])
    dav[...] = dtv[...] * a_neg
    dtcv[...] = jax.nn.softplus(dtc_ref[0] + dtb_ref[...])
    dacv[...] = dtcv[...] * a_neg

    row_i = lax.broadcasted_iota(jnp.int32, (CHUNK, CHUNK), 0)
    col_i = lax.broadcasted_iota(jnp.int32, (CHUNK, CHUNK), 1)
    causal = col_i <= row_i
    tri = causal.astype(F32)
    tri_t = (row_i <= col_i).astype(F32)
    lo_half = col_i < SSD_HEADDIM

    def colb(mat, r):
        return jnp.broadcast_to(mat[:, r:r + 1], (CHUNK, CHUNK))

    def pair_sel(a, b_):
        return jnp.where(lo_half, a, b_)

    def last_row(mat, r):
        return jnp.broadcast_to(mat[CHUNK - 1:CHUNK, r:r + 1], (CHUNK, CHUNK))

    def cumsums(da):
        acol = jnp.dot(tri, da, precision=HIGHEST, preferred_element_type=F32)
        return acol, acol - da

    def state_update(u_ref, dt_s, da_s, c, s_old, backward):
        base = pl.multiple_of(c * CHUNK, CHUNK)
        dt = dt_s[pl.ds(base, CHUNK), :]
        da = da_s[pl.ds(base, CHUNK), :]
        acol, ecol = cumsums(da)
        off = nh if backward else 0
        new = []
        for g in range(SSD_GROUPS):
            xw = []
            dec = []
            for pp in range(SSD_PAIRS // SSD_GROUPS):
                p = g * (SSD_PAIRS // SSD_GROUPS) + pp
                r0, r1 = off + 2 * p, off + 2 * p + 1
                xs = u_ref[pl.ds(base, CHUNK), p * LANES:(p + 1) * LANES].astype(F32)
                if backward:
                    wgt = pair_sel(jnp.exp(colb(ecol, r0)) * colb(dt, r0), jnp.exp(colb(ecol, r1)) * colb(dt, r1))
                else:
                    wgt = pair_sel(jnp.exp(last_row(acol, r0) - colb(acol, r0)) * colb(dt, r0),
                                   jnp.exp(last_row(acol, r1) - colb(acol, r1)) * colb(dt, r1))
                xw.append((xs * wgt).astype(BF16))
                dec.append(pair_sel(jnp.exp(last_row(acol, r0)), jnp.exp(last_row(acol, r1))))
            xw = jnp.concatenate(xw, axis=1)
            dec = jnp.concatenate(dec, axis=1)
            bm = u_ref[pl.ds(base, CHUNK), SSD_WIDTH + g * SSD_STATE:SSD_WIDTH + (g + 1) * SSD_STATE]
            new.append(dec * s_old[g] + _dot_tn(bm, xw))
        return new

    zero_state = [jnp.zeros((SSD_STATE, 2 * LANES), F32) for _ in range(SSD_GROUPS)]

    s = zero_state
    for c in range(nchc):
        s = state_update(uc, dtcv, dacv, c, s, False)
    for g in range(SSD_GROUPS):
        sf_scr[0, g] = s[g]
    s = zero_state
    for c in reversed(range(nchc)):
        s = state_update(uc, dtcv, dacv, c, s, True)
    for g in range(SSD_GROUPS):
        sb_scr[g] = s[g]

    def fwd(c, carry):
        s_old = [sf_scr[c, g] for g in range(SSD_GROUPS)]
        s_new = state_update(u, dtv, dav, c, s_old, False)
        for g in range(SSD_GROUPS):
            sf_scr[c + 1, g] = s_new[g]
        return carry

    lax.fori_loop(0, nch, fwd, 0)

    def bwd(i, carry):
        c = nch - 1 - i
        base = pl.multiple_of(c * CHUNK, CHUNK)
        dt = dtv[pl.ds(base, CHUNK), :]
        da = dav[pl.ds(base, CHUNK), :]
        acol, ecol = cumsums(da)
        da_t = da.T
        dt_t = dt.T
        arow = jnp.dot(da_t, tri_t, precision=HIGHEST, preferred_element_type=F32)
        erow = arow - da_t
        s_b = [sb_scr[g] for g in range(SSD_GROUPS)]
        ys = []
        for g in range(SSD_GROUPS):
            bm = u[pl.ds(base, CHUNK), SSD_WIDTH + g * SSD_STATE:SSD_WIDTH + (g + 1) * SSD_STATE]
            cm = u[pl.ds(base, CHUNK), SSD_WIDTH + (SSD_GROUPS + g) * SSD_STATE:SSD_WIDTH + (SSD_GROUPS + g + 1) * SSD_STATE]
            cbm = _dot_nt(cm, bm)
            cs_f = _dot(cm, sf_scr[c, g].astype(BF16))
            cs_b = _dot(cm, s_b[g].astype(BF16))
            for pp in range(SSD_PAIRS // SSD_GROUPS):
                p = g * (SSD_PAIRS // SSD_GROUPS) + pp
                xs_b = u[pl.ds(base, CHUNK), p * LANES:(p + 1) * LANES]
                y_h = []
                e_f = []
                e_b = []
                for hh in range(2):
                    r = 2 * p + hh
                    af_c = colb(acol, r)
                    eb_c = colb(ecol, nh + r)
                    arg = jnp.where(causal, af_c - arow[r:r + 1, :], erow[nh + r:nh + r + 1, :] - eb_c)
                    coef = jnp.where(causal, dt_t[r:r + 1, :], dt_t[nh + r:nh + r + 1, :])
                    gm = (cbm * (jnp.exp(arg) * coef)).astype(BF16)
                    y_h.append(_dot(gm, xs_b))
                    e_f.append(jnp.exp(af_c))
                    e_b.append(jnp.exp(last_row(acol, nh + r) - eb_c))
                sl = slice(pp * LANES, (pp + 1) * LANES)
                y_p = (pair_sel(y_h[0], y_h[1]) + cs_f[:, sl] * pair_sel(e_f[0], e_f[1])
                       + cs_b[:, sl] * pair_sel(e_b[0], e_b[1])
                       + dsk_ref[:, p * LANES:(p + 1) * LANES] * xs_b.astype(F32))
                ys.append(y_p)
        y = jnp.concatenate(ys, axis=1)
        y = y * _silu(z_ref[0, pl.ds(base, CHUNK), :].astype(F32))
        ms = jnp.mean(y * y, axis=-1, keepdims=True)
        y_ref[0, pl.ds(base, CHUNK), :] = ((y * lax.rsqrt(ms + EPS)) * nw_ref[...]).astype(BF16)
        s_new = state_update(u, dtv, dav, c, s_b, True)
        for g in range(SSD_GROUPS):
            sb_scr[g] = s_new[g]
        return carry

    lax.fori_loop(0, nch, bwd, 0)


def _ssd(xbc, z, dt, xbcc, dtc, conv_w8, conv_b, dt_bias, a_log, d_skip, norm_w):
    b, L, nconv = xbc.shape
    Lc = xbcc.shape[1]
    nch = L // CHUNK
    per_b = lambda n, w: pl.BlockSpec((1, n, w), lambda i: (i, 0, 0))
    return pl.pallas_call(
        _ssd_kernel,
        grid=(b,),
        in_specs=[
            per_b(L, nconv), per_b(L, SSD_WIDTH), per_b(L, LANES), per_b(Lc, nconv), per_b(Lc, LANES),
            _const_spec(conv_w8.shape), _const_spec(conv_b.shape), _const_spec(dt_bias.shape),
            _const_spec(a_log.shape), _const_spec(d_skip.shape), _const_spec(norm_w.shape),
        ],
        out_specs=per_b(L, SSD_WIDTH),
        out_shape=jax.ShapeDtypeStruct((b, L, SSD_WIDTH), BF16),
        scratch_shapes=[
            pltpu.VMEM((L + 2 * CONV_HALO, nconv), F32),
            pltpu.VMEM((Lc + 2 * CONV_HALO, nconv), F32),
            pltpu.VMEM((L, nconv), BF16),
            pltpu.VMEM((Lc, nconv), BF16),
            pltpu.VMEM((L, LANES), F32), pltpu.VMEM((L, LANES), F32),
            pltpu.VMEM((Lc, LANES), F32), pltpu.VMEM((Lc, LANES), F32),
            pltpu.VMEM((nch + 1, SSD_GROUPS, SSD_STATE, 2 * LANES), F32),
            pltpu.VMEM((SSD_GROUPS, SSD_STATE, 2 * LANES), F32),
        ],
        compiler_params=pltpu.CompilerParams(vmem_limit_bytes=56 * 1024 * 1024),
        name="ssd",
    )(xbc, z, dt, xbcc, dtc, conv_w8, conv_b, dt_bias, a_log, d_skip, norm_w)


def _ret_kernel(q_ref, k_ref, v_ref, g_ref, kc_ref, vc_ref, df_ref, db_ref, gn_ref, y_ref, sf_scr, sb_scr):
    L = q_ref.shape[1]
    Lc = kc_ref.shape[1]
    nch = L // CHUNK
    dk = RET_DK
    row_i = lax.broadcasted_iota(jnp.int32, (CHUNK, dk), 0).astype(F32)
    col_i = lax.broadcasted_iota(jnp.int32, (CHUNK, dk), 1).astype(F32)
    rel = row_i - col_i
    crow = lax.broadcasted_iota(jnp.int32, (Lc, dk), 0).astype(F32)

    heads = []
    for h in range(RET_HEADS):
        cols = slice(h * dk, (h + 1) * dk)
        lg_f = -jnp.exp(df_ref[:, cols])
        lg_b = -jnp.exp(db_ref[:, cols])
        heads.append(dict(
            cols=cols,
            dmat=jnp.where(rel >= 0, jnp.exp(jnp.maximum(rel, 0.0) * lg_f), jnp.exp(jnp.maximum(-rel, 0.0) * lg_b)),
            dq_f=jnp.exp((row_i + 1.0) * lg_f),
            dq_b=jnp.exp((CHUNK - row_i) * lg_b),
            dk_f=jnp.exp((CHUNK - 1.0 - row_i) * lg_f),
            dk_b=jnp.exp(row_i * lg_b),
            dc_f=jnp.exp(CHUNK * lg_f),
            dc_b=jnp.exp(CHUNK * lg_b),
        ))
        kc = kc_ref[0, :, cols].astype(F32)
        vc = vc_ref[0, :, cols]
        sf_scr[0, h] = _dot_tn((kc * jnp.exp((Lc - 1.0 - crow) * lg_f)).astype(BF16), vc)
        sb_scr[h] = _dot_tn((kc * jnp.exp(crow * lg_b)).astype(BF16), vc)

    def fwd(c, carry):
        base = pl.multiple_of(c * CHUNK, CHUNK)
        for h, hd in enumerate(heads):
            kk = k_ref[0, pl.ds(base, CHUNK), hd["cols"]].astype(F32)
            vv = v_ref[0, pl.ds(base, CHUNK), hd["cols"]]
            sf_scr[c + 1, h] = hd["dc_f"] * sf_scr[c, h] + _dot_tn((kk * hd["dk_f"]).astype(BF16), vv)
        return carry

    lax.fori_loop(0, nch, fwd, 0)

    def bwd(i, carry):
        c = nch - 1 - i
        base = pl.multiple_of(c * CHUNK, CHUNK)
        for h, hd in enumerate(heads):
            qq = q_ref[0, pl.ds(base, CHUNK), hd["cols"]]
            kk = k_ref[0, pl.ds(base, CHUNK), hd["cols"]]
            vv = v_ref[0, pl.ds(base, CHUNK), hd["cols"]]
            s_b = sb_scr[h]
            scores = (_dot_nt(qq, kk) * hd["dmat"]).astype(BF16)
            y = (_dot(scores, vv)
                 + _dot(qq, sf_scr[c, h].astype(BF16)) * hd["dq_f"]
                 + _dot(qq, s_b.astype(BF16)) * hd["dq_b"])
            mu = jnp.mean(y, axis=-1, keepdims=True)
            yc = y - mu
            var = jnp.mean(yc * yc, axis=-1, keepdims=True)
            yn = (yc * lax.rsqrt(var + EPS)) * gn_ref[:, hd["cols"]]
            gate = _silu(g_ref[0, pl.ds(base, CHUNK), hd["cols"]].astype(F32))
            y_ref[0, pl.ds(base, CHUNK), hd["cols"]] = (yn * gate).astype(BF16)
            sb_scr[h] = hd["dc_b"] * s_b + _dot_tn((kk.astype(F32) * hd["dk_b"]).astype(BF16), vv)
        return carry

    lax.fori_loop(0, nch, bwd, 0)


def _retention(q, k, v, g, kc, vc, decay_f, decay_b, gn_w):
    b, L, w = q.shape
    Lc = kc.shape[1]
    nch = L // CHUNK
    per_b = lambda n: pl.BlockSpec((1, n, w), lambda i: (i, 0, 0))
    return pl.pallas_call(
        _ret_kernel,
        grid=(b,),
        in_specs=[per_b(L), per_b(L), per_b(L), per_b(L), per_b(Lc), per_b(Lc),
                  _const_spec((1, w)), _const_spec((1, w)), _const_spec((1, w))],
        out_specs=per_b(L),
        out_shape=jax.ShapeDtypeStruct((b, L, w), BF16),
        scratch_shapes=[
            pltpu.VMEM((nch + 1, RET_HEADS, RET_DK, RET_DK), F32),
            pltpu.VMEM((RET_HEADS, RET_DK, RET_DK), F32),
        ],
        compiler_params=pltpu.CompilerParams(vmem_limit_bytes=48 * 1024 * 1024),
        name="retention",
    )(q, k, v, g, kc, vc, decay_f, decay_b, gn_w)


def _outproj_router_kernel(yr_ref, ys_ref, x_ref, g1_ref, sh2_ref, sc2_ref, npost_ref, npre_ref,
                           wor_ref, wos_ref, wr_ref, br_ref, tri_ref,
                           x1_ref, h2_ref, route_ref, cnt_ref,
                           wcat, carry):
    i = pl.program_id(0)

    @pl.when(i == 0)
    def _():
        wr = wr_ref[...]
        hi = wr.astype(BF16)
        wcat[:, :LANES] = hi
        wcat[:, LANES:] = (wr - hi.astype(F32)).astype(BF16)
        carry[...] = jnp.zeros_like(carry)

    y = _dot(yr_ref[...], wor_ref[...]) + _dot(ys_ref[...], wos_ref[...])
    ms = jnp.mean(y * y, axis=-1, keepdims=True)
    x1 = x_ref[...] + g1_ref[0] * ((y * lax.rsqrt(ms + EPS)) * npost_ref[...])
    x1_ref[...] = x1
    h2 = _norm_mod(x1, npre_ref[...], sc2_ref[0], sh2_ref[0])
    h2_ref[...] = h2

    h_hi = h2.astype(BF16)
    h_lo = (h2 - h_hi.astype(F32)).astype(BF16)
    both = _dot(h_hi, wcat[...])
    lg = both[:, :LANES] + both[:, LANES:] + _dot(h_lo, wcat[:, :LANES]) + br_ref[...]

    tm = lg.shape[0]
    lane = lax.broadcasted_iota(jnp.int32, (tm, LANES), 1)
    lane_f = lane.astype(F32)
    is_grp = (lane >= N_EXPERTS) & (lane < N_EXPERTS + MOE_GROUPS)
    gl = jnp.where(is_grp, lg, NEG_BIG)
    mg = jnp.max(gl, axis=-1, keepdims=True)
    grp_lane = jnp.min(jnp.where(gl == mg, lane_f, 1e9), axis=-1, keepdims=True)
    p_g = 1.0 / jnp.sum(jnp.where(is_grp, jnp.exp(gl - mg), 0.0), axis=-1, keepdims=True)
    first = (grp_lane - N_EXPERTS) * EXPERTS_PER_GROUP
    in_grp = (lane_f >= first) & (lane_f < first + EXPERTS_PER_GROUP)
    el = jnp.where(in_grp, lg, NEG_BIG)
    t1 = jnp.max(el, axis=-1, keepdims=True)
    i1 = jnp.min(jnp.where(el == t1, lane_f, 1e9), axis=-1, keepdims=True)
    el2 = jnp.where(lane_f == i1, NEG_BIG, el)
    t2 = jnp.max(el2, axis=-1, keepdims=True)
    i2 = jnp.min(jnp.where(el2 == t2, lane_f, 1e9), axis=-1, keepdims=True)
    s = jnp.exp(t2 - t1)
    w1 = p_g / (1.0 + s)
    w2 = p_g * s / (1.0 + s)

    oh1 = (lane_f == i1)
    oh2 = (lane_f == i2)
    oh = (oh1 | oh2).astype(BF16)
    before = _dot(tri_ref[...], oh) + carry[...]
    rank1 = jnp.sum(jnp.where(oh1, before, 0.0), axis=-1, keepdims=True)
    rank2 = jnp.sum(jnp.where(oh2, before, 0.0), axis=-1, keepdims=True)
    total = carry[...] + jnp.sum(oh.astype(F32), axis=0, keepdims=True)
    carry[...] = total
    cnt_ref[...] = total

    route = jnp.where(lane == 0, i1, 0.0)
    route = jnp.where(lane == 1, i2, route)
    route = jnp.where(lane == 2, w1, route)
    route = jnp.where(lane == 3, w2, route)
    route = jnp.where(lane == 4, rank1, route)
    route = jnp.where(lane == 5, rank2, route)
    route_ref[...] = route


def _outproj_router(yr, ys, x2, mod3, npost, npre, wo_r, wo_s, w_router, b_router, seq_len):
    T, d = x2.shape
    tm = TM_OUT
    per_seq = seq_len // tm
    rw = yr.shape[1]
    tri = (jnp.arange(tm)[:, None] > jnp.arange(tm)[None, :]).astype(BF16)
    tok = lambda w: pl.BlockSpec((tm, w), lambda i: (i, 0))
    modv = lambda k: pl.BlockSpec((1, 1, d), lambda i: (i // per_seq, 0, k))
    return pl.pallas_call(
        _outproj_router_kernel,
        grid=(T // tm,),
        in_specs=[
            tok(rw), tok(rw), tok(d), modv(2), modv(3), modv(4),
            _const_spec((1, d)), _const_spec((1, d)),
            _const_spec(wo_r.shape), _const_spec(wo_s.shape), _const_spec(w_router.shape), _const_spec((1, LANES)),
            _const_spec((tm, tm)),
        ],
        out_specs=[tok(d), tok(d), tok(LANES), _const_spec((1, LANES))],
        out_shape=[jax.ShapeDtypeStruct((T, d), F32), jax.ShapeDtypeStruct((T, d), F32),
                   jax.ShapeDtypeStruct((T, LANES), F32), jax.ShapeDtypeStruct((1, LANES), F32)],
        scratch_shapes=[pltpu.VMEM((d, 2 * LANES), BF16), pltpu.VMEM((1, LANES), F32)],
        compiler_params=pltpu.CompilerParams(dimension_semantics=("arbitrary",),
                                             vmem_limit_bytes=48 * 1024 * 1024),
        name="outproj_router",
    )(yr, ys, x2, mod3, mod3, mod3, npost, npre, wo_r, wo_s, w_router, b_router, tri)


def _dispatch_kernel(dest_ref, h_hbm, xs_init_hbm, xs_hbm, sem):
    del xs_init_hbm
    i = pl.program_id(0)
    td = dest_ref.shape[2] // TOP_K

    def row_copy(t, d):
        return pltpu.make_async_copy(h_hbm.at[pl.ds(t, 1)], xs_hbm.at[pl.ds(d, 1)], sem)

    def issue(j, carry):
        t = i * td + j
        for kk in range(TOP_K):
            row_copy(t, dest_ref[0, 0, TOP_K * j + kk]).start()
        return carry

    lax.fori_loop(0, td, issue, 0)

    def drain(j, carry):
        for kk in range(TOP_K):
            row_copy(0, 0).wait()
        return carry

    lax.fori_loop(0, td, drain, 0)


def _dispatch(dest, h2, cap):
    T, d = h2.shape
    td = TD_DISPATCH
    nt = T // td
    return pl.pallas_call(
        _dispatch_kernel,
        grid=(nt,),
        in_specs=[
            pl.BlockSpec((1, 1, TOP_K * td), lambda i: (i, 0, 0), memory_space=pltpu.SMEM),
            pl.BlockSpec(memory_space=pl.ANY),
            pl.BlockSpec(memory_space=pl.ANY),
        ],
        out_specs=pl.BlockSpec(memory_space=pl.ANY),
        out_shape=jax.ShapeDtypeStruct((cap, d), F32),
        scratch_shapes=[pltpu.SemaphoreType.DMA(())],
        input_output_aliases={2: 0},
        name="dispatch",
    )(dest.reshape(nt, 1, TOP_K * td), h2, jnp.zeros((cap, d), F32))


def _expert_kernel(be_ref, nused_ref, xs_ref, wg_ref, wu_ref, wd_ref, y_ref, wg_b, wu_b, wd_b):
    i = pl.program_id(0)
    prev = be_ref[jnp.maximum(i - 1, 0)]

    @pl.when((i == 0) | (be_ref[i] != prev))
    def _():
        wg_b[...] = wg_ref[0].astype(BF16)
        wu_b[...] = wu_ref[0].astype(BF16)
        wd_b[...] = wd_ref[0].astype(BF16)

    @pl.when(i < nused_ref[0])
    def _():
        xb = xs_ref[...].astype(BF16)
        hid = (_silu(_dot(xb, wg_b[...])) * _dot(xb, wu_b[...])).astype(BF16)
        y_ref[...] = _dot(hid, wd_b[...])

    @pl.when(i >= nused_ref[0])
    def _():
        y_ref[...] = jnp.zeros_like(y_ref)


def _experts(blk_expert, n_used, xs, w_gate, w_up, w_down):
    cap, d = xs.shape
    de = w_gate.shape[2]
    mb = MB_EXPERT
    grid_spec = pltpu.PrefetchScalarGridSpec(
        num_scalar_prefetch=2,
        grid=(cap // mb,),
        in_specs=[
            pl.BlockSpec((mb, d), lambda i, be, nu: (i, 0)),
            pl.BlockSpec((1, d, de), lambda i, be, nu: (be[i], 0, 0)),
            pl.BlockSpec((1, d, de), lambda i, be, nu: (be[i], 0, 0)),
            pl.BlockSpec((1, de, d), lambda i, be, nu: (be[i], 0, 0)),
        ],
        out_specs=pl.BlockSpec((mb, d), lambda i, be, nu: (i, 0)),
        scratch_shapes=[pltpu.VMEM((d, de), BF16), pltpu.VMEM((d, de), BF16), pltpu.VMEM((de, d), BF16)],
    )
    return pl.pallas_call(
        _expert_kernel,
        grid_spec=grid_spec,
        out_shape=jax.ShapeDtypeStruct((cap, d), F32),
        compiler_params=pltpu.CompilerParams(dimension_semantics=("arbitrary",),
                                             vmem_limit_bytes=48 * 1024 * 1024),
        name="experts",
    )(blk_expert, n_used, xs, w_gate, w_up, w_down)


def _combine_kernel(dest_ref, route_ref, x1_ref, g2_ref, nw_ref, yb_hbm, o_ref, buf, sem):
    tf = x1_ref.shape[0]

    def row_copy(j, kk, d):
        return pltpu.make_async_copy(yb_hbm.at[pl.ds(d, 1)], buf.at[kk, pl.ds(j, 1)], sem)

    def issue(j, carry):
        for kk in range(TOP_K):
            row_copy(j, kk, dest_ref[0, 0, TOP_K * j + kk]).start()
        return carry

    lax.fori_loop(0, tf, issue, 0)

    def drain(j, carry):
        for kk in range(TOP_K):
            row_copy(0, kk, 0).wait()
        return carry

    lax.fori_loop(0, tf, drain, 0)

    route = route_ref[...]
    out = route[:, 2:3] * buf[0] + route[:, 3:4] * buf[1]
    ms = jnp.mean(out * out, axis=-1, keepdims=True)
    o_ref[...] = x1_ref[...] + g2_ref[0] * ((out * lax.rsqrt(ms + EPS)) * nw_ref[...])


def _combine(dest, route, x1, mod3, nw, yb, seq_len):
    T, d = x1.shape
    tf = TF_COMBINE
    nt = T // tf
    per_seq = seq_len // tf
    return pl.pallas_call(
        _combine_kernel,
        grid=(nt,),
        in_specs=[
            pl.BlockSpec((1, 1, TOP_K * tf), lambda i: (i, 0, 0), memory_space=pltpu.SMEM),
            pl.BlockSpec((tf, LANES), lambda i: (i, 0)),
            pl.BlockSpec((tf, d), lambda i: (i, 0)),
            pl.BlockSpec((1, 1, d), lambda i: (i // per_seq, 0, 5)),
            _const_spec((1, d)),
            pl.BlockSpec(memory_space=pl.ANY),
        ],
        out_specs=pl.BlockSpec((tf, d), lambda i: (i, 0)),
        out_shape=jax.ShapeDtypeStruct((T, d), F32),
        scratch_shapes=[pltpu.VMEM((TOP_K, tf, d), F32), pltpu.SemaphoreType.DMA(())],
        name="combine",
    )(dest.reshape(nt, 1, TOP_K * tf), route, x1, mod3, nw, yb)


def _rope_tables(L, n_heads):
    quarter = RET_DK // 4
    freqs = ROPE_BASE ** (-jnp.arange(quarter, dtype=F32) / quarter)
    t = jnp.arange(L)
    ang_r = (t // GRID_W).astype(F32)[:, None] * freqs
    ang_c = (t % GRID_W).astype(F32)[:, None] * freqs
    cos = jnp.concatenate([jnp.cos(ang_r)] * 2 + [jnp.cos(ang_c)] * 2, axis=-1)
    sin = jnp.concatenate([-jnp.sin(ang_r), jnp.sin(ang_r), -jnp.sin(ang_c), jnp.sin(ang_c)], axis=-1)
    return jnp.tile(cos, (1, n_heads)), jnp.tile(sin, (1, n_heads))


def _lane_pad(v, width=LANES):
    return jnp.pad(v, [(0, 0)] * (v.ndim - 1) + [(0, width - v.shape[-1])])


def kernel(x, c, ctx, c_ctx, w_mod, b_mod, norm_pre_mix, norm_post_mix, norm_pre_ffn, norm_post_ffn, w_in, w_out, ret_decay_f, ret_decay_b, ret_gn_w, ssd_conv_w, ssd_conv_b, ssd_dt_bias_f, ssd_dt_bias_b, ssd_a_log_f, ssd_a_log_b, ssd_d, ssd_norm_w, moe_w_rg, moe_b_rg, moe_w_re, moe_b_re, moe_w_gate, moe_w_up, moe_w_down):
    b, L, d = x.shape
    assert w_mod.shape[0] == 1, "single layer: context outputs are never needed"
    rw = RET_HEADS * RET_DK
    nconv = SSD_WIDTH + 2 * SSD_GROUPS * SSD_STATE
    T = b * L

    mod_rows = -(-(b + 1) // SUBLANES) * SUBLANES
    c_all = jnp.zeros((mod_rows, d), F32).at[:b].set(c).at[b].set(c_ctx)
    mod3 = _modulation(c_all, w_mod[0], b_mod[0]).reshape(mod_rows, 1, 6 * d)

    wi = w_in[0]
    o = 0
    wq = wi[:, o:o + rw]; o += rw
    wk = wi[:, o:o + rw]; o += rw
    wv = wi[:, o:o + rw]; o += rw
    wg = wi[:, o:o + rw]; o += rw
    wz = wi[:, o:o + SSD_WIDTH]; o += SSD_WIDTH
    wxbc = wi[:, o:o + nconv].astype(BF16); o += nconv
    wdt = _lane_pad(wi[:, o:o + 2 * SSD_HEADS]).astype(BF16)
    wqk = jnp.concatenate([wq, wk], axis=1).astype(BF16)
    wvgz = jnp.concatenate([wv, wg, wz], axis=1).astype(BF16)
    cos_t, sin_t = _rope_tables(L, RET_HEADS)
    nw1 = norm_pre_mix[0].reshape(1, d)

    q, k, v, g, z, xbc, dt = _inproj(x, mod3, nw1, wqk, wvgz, wxbc, wdt, cos_t, sin_t)
    kc, vc, xbcc, dtc = _inproj_ctx(ctx, mod3, b, nw1, wk.astype(BF16), wv.astype(BF16), wxbc, wdt)

    conv_w8 = jnp.pad(ssd_conv_w[0], ((0, SUBLANES - SSD_CONV), (0, 0)))
    dt_bias = _lane_pad(jnp.concatenate([ssd_dt_bias_f[0], ssd_dt_bias_b[0]])[None, :])
    a_log = _lane_pad(jnp.concatenate([ssd_a_log_f[0], ssd_a_log_b[0]])[None, :])
    d_skip = jnp.repeat(ssd_d[0], SSD_HEADDIM)[None, :]
    ys = _ssd(xbc, z, dt, xbcc, dtc, conv_w8, ssd_conv_b[0][None, :], dt_bias, a_log, d_skip,
              ssd_norm_w[0][None, :])

    yr = _retention(q, k, v, g, kc, vc,
                    jnp.repeat(ret_decay_f[0], RET_DK)[None, :], jnp.repeat(ret_decay_b[0], RET_DK)[None, :],
                    ret_gn_w[0][None, :])

    wo = w_out[0].astype(BF16)
    w_router = _lane_pad(jnp.concatenate(
        [jnp.transpose(moe_w_re[0], (1, 0, 2)).reshape(d, N_EXPERTS), moe_w_rg[0]], axis=1))
    b_router = _lane_pad(jnp.concatenate([moe_b_re[0].reshape(-1), moe_b_rg[0]])[None, :])
    x1, h2, route, counts = _outproj_router(
        yr.reshape(T, rw), ys.reshape(T, SSD_WIDTH), x.reshape(T, d), mod3,
        norm_post_mix[0][None, :], norm_pre_ffn[0][None, :], wo[:rw], wo[rw:], w_router, b_router, L)

    mb = MB_EXPERT
    n_blocks = -(-(T * TOP_K + N_EXPERTS * (mb - 1)) // mb)
    cap = n_blocks * mb
    cnt = counts[0, :N_EXPERTS].astype(jnp.int32)
    padded = (cnt + mb - 1) // mb * mb
    pad_end = jnp.cumsum(padded)
    pad_start = pad_end - padded
    expert = route[:, 0:TOP_K].astype(jnp.int32)
    dest = (pad_start[expert] + route[:, 4:4 + TOP_K].astype(jnp.int32)).reshape(-1)
    blk_expert = jnp.minimum(
        jnp.searchsorted(pad_end, jnp.arange(n_blocks, dtype=jnp.int32) * mb, side='right'),
        N_EXPERTS - 1).astype(jnp.int32)
    n_used = (pad_end[-1:] // mb).astype(jnp.int32)

    xs = _dispatch(dest, h2, cap)
    yb = _experts(blk_expert, n_used, xs, moe_w_gate[0], moe_w_up[0], moe_w_down[0])
    out = _combine(dest, route, x1, mod3, norm_post_ffn[0][None, :], yb, L)
    return out.reshape(b, L, d)
```

```python
import functools
import math

import jax
import jax.numpy as jnp
from jax import lax
from jax.experimental import pallas as pl
from jax.experimental.pallas import tpu as pltpu

F32 = jnp.float32
BF16 = jnp.bfloat16
HIGHEST = lax.Precision.HIGHEST

LANES = 128
SUBLANES = 8

EPS = 1e-6
CHUNK = 128
GRID_W = 64
RET_HEADS = 4
RET_DK = 128
ROPE_BASE = 10000.0
SSD_HEADS = 8
SSD_HEADDIM = 64
SSD_GROUPS = 2
SSD_STATE = 128
SSD_WIDTH = SSD_HEADS * SSD_HEADDIM
SSD_CONV = 5
SSD_PAIRS = SSD_WIDTH // LANES
MOE_GROUPS = 4
EXPERTS_PER_GROUP = 8
N_EXPERTS = MOE_GROUPS * EXPERTS_PER_GROUP
TOP_K = 2
CONV_HALO = SUBLANES

TM_PROJ = 512
TM_OUT = 512
TD_DISPATCH = 512
MB_EXPERT = 256
TF_COMBINE = 256
NEG_BIG = -1e30


def _silu(v):
    return v * jax.nn.sigmoid(v)


def _dot(a, b):
    return jnp.dot(a, b, preferred_element_type=F32)


def _dot_tn(a, b):
    return lax.dot_general(a, b, (((0,), (0,)), ((), ())), preferred_element_type=F32)


def _dot_nt(a, b):
    return lax.dot_general(a, b, (((1,), (1,)), ((), ())), preferred_element_type=F32)


def _mod_kernel(c_ref, w_ref, b_ref, o_ref):
    a = _silu(c_ref[...])
    o_ref[...] = jnp.dot(a, w_ref[...], precision=HIGHEST, preferred_element_type=F32) + b_ref[...]


def _modulation(c_all, w_mod, b_mod):
    rows, d = c_all.shape
    n = w_mod.shape[1]
    return pl.pallas_call(
        _mod_kernel,
        grid=(n // d,),
        in_specs=[
            pl.BlockSpec((rows, d), lambda j: (0, 0)),
            pl.BlockSpec((d, d), lambda j: (0, j)),
            pl.BlockSpec((1, d), lambda j: (0, j)),
        ],
        out_specs=pl.BlockSpec((rows, d), lambda j: (0, j)),
        out_shape=jax.ShapeDtypeStruct((rows, n), F32),
        name="modulation",
    )(c_all, w_mod, b_mod.reshape(1, n))


def _norm_mod(x, nw, sc, sh):
    ms = jnp.mean(x * x, axis=-1, keepdims=True)
    return (x * lax.rsqrt(ms + EPS)) * nw * (1.0 + sc) + sh


def _rope(t, cos, sin_signed, first_half):
    width = t.shape[-1]
    quarter = RET_DK // 4
    swapped = jnp.where(first_half, pltpu.roll(t, width - quarter, 1), pltpu.roll(t, quarter, 1))
    return t * cos + swapped * sin_signed


def _inproj_kernel(x_ref, sh_ref, sc_ref, nw_ref, wqk_ref, wvgz_ref, wxbc_ref, wdt_ref, cos_ref, sin_ref,
                   q_ref, k_ref, v_ref, g_ref, z_ref, xbc_ref, dt_ref):
    hb = _norm_mod(x_ref[0], nw_ref[...], sc_ref[0], sh_ref[0]).astype(BF16)
    rw = q_ref.shape[-1]
    qk = _dot(hb, wqk_ref[...])
    cos = cos_ref[...]
    sin = sin_ref[...]
    lane = lax.broadcasted_iota(jnp.int32, cos.shape, 1)
    first_half = (lane % (RET_DK // 2)) < (RET_DK // 4)
    q_ref[0] = _rope(qk[:, :rw], cos, sin, first_half).astype(BF16)
    k_ref[0] = (_rope(qk[:, rw:], cos, sin, first_half) * (RET_DK ** -0.5)).astype(BF16)
    vgz = _dot(hb, wvgz_ref[...])
    v_ref[0] = vgz[:, :rw].astype(BF16)
    g_ref[0] = vgz[:, rw:2 * rw].astype(BF16)
    z_ref[0] = vgz[:, 2 * rw:].astype(BF16)
    xbc_ref[0] = _dot(hb, wxbc_ref[...]).astype(BF16)
    dt_ref[0] = _dot(hb, wdt_ref[...])


def _inproj_ctx_kernel(x_ref, sh_ref, sc_ref, nw_ref, wk_ref, wv_ref, wxbc_ref, wdt_ref,
                       k_ref, v_ref, xbc_ref, dt_ref):
    hb = _norm_mod(x_ref[0], nw_ref[...], sc_ref[0], sh_ref[0]).astype(BF16)
    k_ref[0] = (_dot(hb, wk_ref[...]) * (RET_DK ** -0.5)).astype(BF16)
    v_ref[0] = _dot(hb, wv_ref[...]).astype(BF16)
    xbc_ref[0] = _dot(hb, wxbc_ref[...]).astype(BF16)
    dt_ref[0] = _dot(hb, wdt_ref[...])


def _const_spec(shape):
    nd = len(shape)
    return pl.BlockSpec(shape, lambda *_: (0,) * nd)


def _inproj(x, mod3, nw, wqk, wvgz, wxbc, wdt, cos_t, sin_t):
    b, L, d = x.shape
    tm = min(TM_PROJ, L)
    rw = wqk.shape[1] // 2
    tok = lambda w: pl.BlockSpec((1, tm, w), lambda i, j: (i, j, 0))
    out_bf = lambda w: jax.ShapeDtypeStruct((b, L, w), BF16)
    return pl.pallas_call(
        _inproj_kernel,
        grid=(b, L // tm),
        in_specs=[
            tok(d),
            pl.BlockSpec((1, 1, d), lambda i, j: (i, 0, 0)),
            pl.BlockSpec((1, 1, d), lambda i, j: (i, 0, 1)),
            _const_spec((1, d)),
            _const_spec(wqk.shape), _const_spec(wvgz.shape), _const_spec(wxbc.shape), _const_spec(wdt.shape),
            pl.BlockSpec((tm, rw), lambda i, j: (j, 0)),
            pl.BlockSpec((tm, rw), lambda i, j: (j, 0)),
        ],
        out_specs=[tok(rw), tok(rw), tok(rw), tok(rw), tok(rw), tok(wxbc.shape[1]), tok(LANES)],
        out_shape=[out_bf(rw), out_bf(rw), out_bf(rw), out_bf(rw), out_bf(rw), out_bf(wxbc.shape[1]),
                   jax.ShapeDtypeStruct((b, L, LANES), F32)],
        compiler_params=pltpu.CompilerParams(vmem_limit_bytes=48 * 1024 * 1024),
        name="inproj",
    )(x, mod3, mod3, nw, wqk, wvgz, wxbc, wdt, cos_t, sin_t)


def _inproj_ctx(ctx, mod3, ctx_row, nw, wk, wv, wxbc, wdt):
    b, L, d = ctx.shape
    tm = min(TM_PROJ, L)
    rw = wk.shape[1]
    tok = lambda w: pl.BlockSpec((1, tm, w), lambda i, j: (i, j, 0))
    out_bf = lambda w: jax.ShapeDtypeStruct((b, L, w), BF16)
    return pl.pallas_call(
        _inproj_ctx_kernel,
        grid=(b, L // tm),
        in_specs=[
            tok(d),
            pl.BlockSpec((1, 1, d), lambda i, j: (ctx_row, 0, 0)),
            pl.BlockSpec((1, 1, d), lambda i, j: (ctx_row, 0, 1)),
            _const_spec((1, d)),
            _const_spec(wk.shape), _const_spec(wv.shape), _const_spec(wxbc.shape), _const_spec(wdt.shape),
        ],
        out_specs=[tok(rw), tok(rw), tok(wxbc.shape[1]), tok(LANES)],
        out_shape=[out_bf(rw), out_bf(rw), out_bf(wxbc.shape[1]), jax.ShapeDtypeStruct((b, L, LANES), F32)],
        compiler_params=pltpu.CompilerParams(vmem_limit_bytes=48 * 1024 * 1024),
        name="inproj_ctx",
    )(ctx, mod3, mod3, nw, wk, wv, wxbc, wdt)


def _ssd_kernel(xbc_ref, z_ref, dt_ref, xbcc_ref, dtc_ref, cw_ref, cb_ref, dtb_ref, alog_ref, dsk_ref, nw_ref,
                y_ref,
                xpad, xpadc, u, uc, dtv, dav, dtcv, dacv, sf_scr, sb_scr):
    L = xbc_ref.shape[1]
    Lc = xbcc_ref.shape[1]
    nch = L // CHUNK
    nchc = Lc // CHUNK
    win = CHUNK + 2 * CONV_HALO
    nconv = xbc_ref.shape[2]
    nh = SSD_HEADS

    def conv_pass(src_ref, pad_ref, dst_ref, n_chunks, length):
        zeros = jnp.zeros((CONV_HALO, nconv), F32)
        pad_ref[0:CONV_HALO, :] = zeros
        pad_ref[CONV_HALO + length:2 * CONV_HALO + length, :] = zeros
        pad_ref[CONV_HALO:CONV_HALO + length, :] = src_ref[0].astype(F32)

        def chunk(c, carry):
            base = pl.multiple_of(c * CHUNK, CHUNK)
            for cb_i in range(nconv // LANES):
                cols = slice(cb_i * LANES, (cb_i + 1) * LANES)
                w = pad_ref[pl.ds(base, win), cols]
                acc = cb_ref[:, cols] + w[CONV_HALO:CONV_HALO + CHUNK] * cw_ref[SSD_CONV // 2:SSD_CONV // 2 + 1, cols]
                for j in range(SSD_CONV):
                    if j == SSD_CONV // 2:
                        continue
                    shifted = pltpu.roll(w, (SSD_CONV // 2 - j) % win, 0)
                    acc = acc + shifted[CONV_HALO:CONV_HALO + CHUNK] * cw_ref[j:j + 1, cols]
                dst_ref[pl.ds(base, CHUNK), cols] = _silu(acc).astype(BF16)
            return carry

        lax.fori_loop(0, n_chunks, chunk, 0)

    conv_pass(xbcc_ref, xpadc, uc, nchc, Lc)
    conv_pass(xbc_ref, xpad, u, nch, L)

    a_neg = -jnp.exp(alog_ref[...])
    dtv[...] = jax.nn.softplus(dt_ref[0] + dtb_ref[...])
    dav[...] = dtv[...] * a_neg
    dtcv[...] = jax.nn.softplus(dtc_ref[0] + dtb_ref[...])
    dacv[...] = dtcv[...] * a_neg

    row_i = lax.broadcasted_iota(jnp.int32, (CHUNK, CHUNK), 0)
    col_i = lax.broadcasted_iota(jnp.int32, (CHUNK, CHUNK), 1)
    causal = col_i <= row_i
    tri = causal.astype(F32)
    tri_t = (row_i <= col_i).astype(F32)
    lo_half = col_i < SSD_HEADDIM

    def colb(mat, r):
        return jnp.broadcast_to(mat[:, r:r + 1], (CHUNK, CHUNK))

    def pair_sel(a, b_):
        return jnp.where(lo_half, a, b_)

    def last_row(mat, r):
        return jnp.broadcast_to(mat[CHUNK - 1:CHUNK, r:r + 1], (CHUNK, CHUNK))

    def cumsums(da):
        acol = jnp.dot(tri, da, precision=HIGHEST, preferred_element_type=F32)
        return acol, acol - da

    def state_update(u_ref, dt_s, da_s, c, s_old, backward):
        base = pl.multiple_of(c * CHUNK, CHUNK)
        dt = dt_s[pl.ds(base, CHUNK), :]
        da = da_s[pl.ds(base, CHUNK), :]
        acol, ecol = cumsums(da)
        off = nh if backward else 0
        new = []
        for g in range(SSD_GROUPS):
            xw = []
            dec = []
            for pp in range(SSD_PAIRS // SSD_GROUPS):
                p = g * (SSD_PAIRS // SSD_GROUPS) + pp
                r0, r1 = off + 2 * p, off + 2 * p + 1
                xs = u_ref[pl.ds(base, CHUNK), p * LANES:(p + 1) * LANES].astype(F32)
                if backward:
                    wgt = pair_sel(jnp.exp(colb(ecol, r0)) * colb(dt, r0), jnp.exp(colb(ecol, r1)) * colb(dt, r1))
                else:
                    wgt = pair_sel(jnp.exp(last_row(acol, r0) - colb(acol, r0)) * colb(dt, r0),
                                   jnp.exp(last_row(acol, r1) - colb(acol, r1)) * colb(dt, r1))
                xw.append((xs * wgt).astype(BF16))
                dec.append(pair_sel(jnp.exp(last_row(acol, r0)), jnp.exp(last_row(acol, r1))))
            xw = jnp.concatenate(xw, axis=1)
            dec = jnp.concatenate(dec, axis=1)
            bm = u_ref[pl.ds(base, CHUNK), SSD_WIDTH + g * SSD_STATE:SSD_WIDTH + (g + 1) * SSD_STATE]
            new.append(dec * s_old[g] + _dot_tn(bm, xw))
        return new

    zero_state = [jnp.zeros((SSD_STATE, 2 * LANES), F32) for _ in range(SSD_GROUPS)]

    s = zero_state
    for c in range(nchc):
        s = state_update(uc, dtcv, dacv, c, s, False)
    for g in range(SSD_GROUPS):
        sf_scr[0, g] = s[g]
    s = zero_state
    for c in reversed(range(nchc)):
        s = state_update(uc, dtcv, dacv, c, s, True)
    for g in range(SSD_GROUPS):
        sb_scr[g] = s[g]

    def fwd(c, carry):
        s_old = [sf_scr[c, g] for g in range(SSD_GROUPS)]
        s_new = state_update(u, dtv, dav, c, s_old, False)
        for g in range(SSD_GROUPS):
            sf_scr[c + 1, g] = s_new[g]
        return carry

    lax.fori_loop(0, nch, fwd, 0)

    def bwd(i, carry):
        c = nch - 1 - i
        base = pl.multiple_of(c * CHUNK, CHUNK)
        dt = dtv[pl.ds(base, CHUNK), :]
        da = dav[pl.ds(base, CHUNK), :]
        acol, ecol = cumsums(da)
        da_t = da.T
        dt_t = dt.T
        arow = jnp.dot(da_t, tri_t, precision=HIGHEST, preferred_element_type=F32)
        erow = arow - da_t
        s_b = [sb_scr[g] for g in range(SSD_GROUPS)]
        ys = []
        for g in range(SSD_GROUPS):
            bm = u[pl.ds(base, CHUNK), SSD_WIDTH + g * SSD_STATE:SSD_WIDTH + (g + 1) * SSD_STATE]
            cm = u[pl.ds(base, CHUNK), SSD_WIDTH + (SSD_GROUPS + g) * SSD_STATE:SSD_WIDTH + (SSD_GROUPS + g + 1) * SSD_STATE]
            cbm = _dot_nt(cm, bm)
            cs_f = _dot(cm, sf_scr[c, g].astype(BF16))
            cs_b = _dot(cm, s_b[g].astype(BF16))
            for pp in range(SSD_PAIRS // SSD_GROUPS):
                p = g * (SSD_PAIRS // SSD_GROUPS) + pp
                xs_b = u[pl.ds(base, CHUNK), p * LANES:(p + 1) * LANES]
                y_h = []
                e_f = []
                e_b = []
                for hh in range(2):
                    r = 2 * p + hh
                    af_c = colb(acol, r)
                    eb_c = colb(ecol, nh + r)
                    arg = jnp.where(causal, af_c - arow[r:r + 1, :], erow[nh + r:nh + r + 1, :] - eb_c)
                    coef = jnp.where(causal, dt_t[r:r + 1, :], dt_t[nh + r:nh + r + 1, :])
                    gm = (cbm * (jnp.exp(arg) * coef)).astype(BF16)
                    y_h.append(_dot(gm, xs_b))
                    e_f.append(jnp.exp(af_c))
                    e_b.append(jnp.exp(last_row(acol, nh + r) - eb_c))
                sl = slice(pp * LANES, (pp + 1) * LANES)
                y_p = (pair_sel(y_h[0], y_h[1]) + cs_f[:, sl] * pair_sel(e_f[0], e_f[1])
                       + cs_b[:, sl] * pair_sel(e_b[0], e_b[1])
                       + dsk_ref[:, p * LANES:(p + 1) * LANES] * xs_b.astype(F32))
                ys.append(y_p)
        y = jnp.concatenate(ys, axis=1)
        y = y * _silu(z_ref[0, pl.ds(base, CHUNK), :].astype(F32))
        ms = jnp.mean(y * y, axis=-1, keepdims=True)
        y_ref[0, pl.ds(base, CHUNK), :] = ((y * lax.rsqrt(ms + EPS)) * nw_ref[...]).astype(BF16)
        s_new = state_update(u, dtv, dav, c, s_b, True)
        for g in range(SSD_GROUPS):
            sb_scr[g] = s_new[g]
        return carry

    lax.fori_loop(0, nch, bwd, 0)


def _ssd(xbc, z, dt, xbcc, dtc, conv_w8, conv_b, dt_bias, a_log, d_skip, norm_w):
    b, L, nconv = xbc.shape
    Lc = xbcc.shape[1]
    nch = L // CHUNK
    per_b = lambda n, w: pl.BlockSpec((1, n, w), lambda i: (i, 0, 0))
    return pl.pallas_call(
        _ssd_kernel,
        grid=(b,),
        in_specs=[
            per_b(L, nconv), per_b(L, SSD_WIDTH), per_b(L, LANES), per_b(Lc, nconv), per_b(Lc, LANES),
            _const_spec(conv_w8.shape), _const_spec(conv_b.shape), _const_spec(dt_bias.shape),
            _const_spec(a_log.shape), _const_spec(d_skip.shape), _const_spec(norm_w.shape),
        ],
        out_specs=per_b(L, SSD_WIDTH),
        out_shape=jax.ShapeDtypeStruct((b, L, SSD_WIDTH), BF16),
        scratch_shapes=[
            pltpu.VMEM((L + 2 * CONV_HALO, nconv), F32),
            pltpu.VMEM((Lc + 2 * CONV_HALO, nconv), F32),
            pltpu.VMEM((L, nconv), BF16),
            pltpu.VMEM((Lc, nconv), BF16),
            pltpu.VMEM((L, LANES), F32), pltpu.VMEM((L, LANES), F32),
            pltpu.VMEM((Lc, LANES), F32), pltpu.VMEM((Lc, LANES), F32),
            pltpu.VMEM((nch + 1, SSD_GROUPS, SSD_STATE, 2 * LANES), F32),
            pltpu.VMEM((SSD_GROUPS, SSD_STATE, 2 * LANES), F32),
        ],
        compiler_params=pltpu.CompilerParams(vmem_limit_bytes=56 * 1024 * 1024),
        name="ssd",
    )(xbc, z, dt, xbcc, dtc, conv_w8, conv_b, dt_bias, a_log, d_skip, norm_w)


def _ret_kernel(q_ref, k_ref, v_ref, g_ref, kc_ref, vc_ref, df_ref, db_ref, gn_ref, y_ref, sf_scr, sb_scr):
    L = q_ref.shape[1]
    Lc = kc_ref.shape[1]
    nch = L // CHUNK
    dk = RET_DK
    row_i = lax.broadcasted_iota(jnp.int32, (CHUNK, dk), 0).astype(F32)
    col_i = lax.broadcasted_iota(jnp.int32, (CHUNK, dk), 1).astype(F32)
    rel = row_i - col_i
    crow = lax.broadcasted_iota(jnp.int32, (Lc, dk), 0).astype(F32)

    heads = []
    for h in range(RET_HEADS):
        cols = slice(h * dk, (h + 1) * dk)
        lg_f = -jnp.exp(df_ref[:, cols])
        lg_b = -jnp.exp(db_ref[:, cols])
        heads.append(dict(
            cols=cols,
            dmat=jnp.where(rel >= 0, jnp.exp(jnp.maximum(rel, 0.0) * lg_f), jnp.exp(jnp.maximum(-rel, 0.0) * lg_b)),
            dq_f=jnp.exp((row_i + 1.0) * lg_f),
            dq_b=jnp.exp((CHUNK - row_i) * lg_b),
            dk_f=jnp.exp((CHUNK - 1.0 - row_i) * lg_f),
            dk_b=jnp.exp(row_i * lg_b),
            dc_f=jnp.exp(CHUNK * lg_f),
            dc_b=jnp.exp(CHUNK * lg_b),
        ))
        kc = kc_ref[0, :, cols].astype(F32)
        vc = vc_ref[0, :, cols]
        sf_scr[0, h] = _dot_tn((kc * jnp.exp((Lc - 1.0 - crow) * lg_f)).astype(BF16), vc)
        sb_scr[h] = _dot_tn((kc * jnp.exp(crow * lg_b)).astype(BF16), vc)

    def fwd(c, carry):
        base = pl.multiple_of(c * CHUNK, CHUNK)
        for h, hd in enumerate(heads):
            kk = k_ref[0, pl.ds(base, CHUNK), hd["cols"]].astype(F32)
            vv = v_ref[0, pl.ds(base, CHUNK), hd["cols"]]
            sf_scr[c + 1, h] = hd["dc_f"] * sf_scr[c, h] + _dot_tn((kk * hd["dk_f"]).astype(BF16), vv)
        return carry

    lax.fori_loop(0, nch, fwd, 0)

    def bwd(i, carry):
        c = nch - 1 - i
        base = pl.multiple_of(c * CHUNK, CHUNK)
        for h, hd in enumerate(heads):
            qq = q_ref[0, pl.ds(base, CHUNK), hd["cols"]]
            kk = k_ref[0, pl.ds(base, CHUNK), hd["cols"]]
            vv = v_ref[0, pl.ds(base, CHUNK), hd["cols"]]
            s_b = sb_scr[h]
            scores = (_dot_nt(qq, kk) * hd["dmat"]).astype(BF16)
            y = (_dot(scores, vv)
                 + _dot(qq, sf_scr[c, h].astype(BF16)) * hd["dq_f"]
                 + _dot(qq, s_b.astype(BF16)) * hd["dq_b"])
            mu = jnp.mean(y, axis=-1, keepdims=True)
            yc = y - mu
            var = jnp.mean(yc * yc, axis=-1, keepdims=True)
            yn = (yc * lax.rsqrt(var + EPS)) * gn_ref[:, hd["cols"]]
            gate = _silu(g_ref[0, pl.ds(base, CHUNK), hd["cols"]].astype(F32))
            y_ref[0, pl.ds(base, CHUNK), hd["cols"]] = (yn * gate).astype(BF16)
            sb_scr[h] = hd["dc_b"] * s_b + _dot_tn((kk.astype(F32) * hd["dk_b"]).astype(BF16), vv)
        return carry

    lax.fori_loop(0, nch, bwd, 0)


def _retention(q, k, v, g, kc, vc, decay_f, decay_b, gn_w):
    b, L, w = q.shape
    Lc = kc.shape[1]
    nch = L // CHUNK
    per_b = lambda n: pl.BlockSpec((1, n, w), lambda i: (i, 0, 0))
    return pl.pallas_call(
        _ret_kernel,
        grid=(b,),
        in_specs=[per_b(L), per_b(L), per_b(L), per_b(L), per_b(Lc), per_b(Lc),
                  _const_spec((1, w)), _const_spec((1, w)), _const_spec((1, w))],
        out_specs=per_b(L),
        out_shape=jax.ShapeDtypeStruct((b, L, w), BF16),
        scratch_shapes=[
            pltpu.VMEM((nch + 1, RET_HEADS, RET_DK, RET_DK), F32),
            pltpu.VMEM((RET_HEADS, RET_DK, RET_DK), F32),
        ],
        compiler_params=pltpu.CompilerParams(vmem_limit_bytes=48 * 1024 * 1024),
        name="retention",
    )(q, k, v, g, kc, vc, decay_f, decay_b, gn_w)


def _outproj_router_kernel(yr_ref, ys_ref, x_ref, g1_ref, sh2_ref, sc2_ref, npost_ref, npre_ref,
                           wor_ref, wos_ref, wr_ref, br_ref, tri_ref,
                           x1_ref, h2_ref, route_ref, cnt_ref,
                           wcat, carry):
    i = pl.program_id(0)

    @pl.when(i == 0)
    def _():
        wr = wr_ref[...]
        hi = wr.astype(BF16)
        wcat[:, :LANES] = hi
        wcat[:, LANES:] = (wr - hi.astype(F32)).astype(BF16)
        carry[...] = jnp.zeros_like(carry)

    y = _dot(yr_ref[...], wor_ref[...]) + _dot(ys_ref[...], wos_ref[...])
    ms = jnp.mean(y * y, axis=-1, keepdims=True)
    x1 = x_ref[...] + g1_ref[0] * ((y * lax.rsqrt(ms + EPS)) * npost_ref[...])
    x1_ref[...] = x1
    h2 = _norm_mod(x1, npre_ref[...], sc2_ref[0], sh2_ref[0])
    h2_ref[...] = h2

    h_hi = h2.astype(BF16)
    h_lo = (h2 - h_hi.astype(F32)).astype(BF16)
    both = _dot(h_hi, wcat[...])
    lg = both[:, :LANES] + both[:, LANES:] + _dot(h_lo, wcat[:, :LANES]) + br_ref[...]

    tm = lg.shape[0]
    lane = lax.broadcasted_iota(jnp.int32, (tm, LANES), 1)
    lane_f = lane.astype(F32)
    is_grp = (lane >= N_EXPERTS) & (lane < N_EXPERTS + MOE_GROUPS)
    gl = jnp.where(is_grp, lg, NEG_BIG)
    mg = jnp.max(gl, axis=-1, keepdims=True)
    grp_lane = jnp.min(jnp.where(gl == mg, lane_f, 1e9), axis=-1, keepdims=True)
    p_g = 1.0 / jnp.sum(jnp.where(is_grp, jnp.exp(gl - mg), 0.0), axis=-1, keepdims=True)
    first = (grp_lane - N_EXPERTS) * EXPERTS_PER_GROUP
    in_grp = (lane_f >= first) & (lane_f < first + EXPERTS_PER_GROUP)
    el = jnp.where(in_grp, lg, NEG_BIG)
    t1 = jnp.max(el, axis=-1, keepdims=True)
    i1 = jnp.min(jnp.where(el == t1, lane_f, 1e9), axis=-1, keepdims=True)
    el2 = jnp.where(lane_f == i1, NEG_BIG, el)
    t2 = jnp.max(el2, axis=-1, keepdims=True)
    i2 = jnp.min(jnp.where(el2 == t2, lane_f, 1e9), axis=-1, keepdims=True)
    s = jnp.exp(t2 - t1)
    w1 = p_g / (1.0 + s)
    w2 = p_g * s / (1.0 + s)

    oh1 = (lane_f == i1)
    oh2 = (lane_f == i2)
    oh = (oh1 | oh2).astype(BF16)
    before = _dot(tri_ref[...], oh) + carry[...]
    rank1 = jnp.sum(jnp.where(oh1, before, 0.0), axis=-1, keepdims=True)
    rank2 = jnp.sum(jnp.where(oh2, before, 0.0), axis=-1, keepdims=True)
    total = carry[...] + jnp.sum(oh.astype(F32), axis=0, keepdims=True)
    carry[...] = total
    cnt_ref[...] = total

    route = jnp.where(lane == 0, i1, 0.0)
    route = jnp.where(lane == 1, i2, route)
    route = jnp.where(lane == 2, w1, route)
    route = jnp.where(lane == 3, w2, route)
    route = jnp.where(lane == 4, rank1, route)
    route = jnp.where(lane == 5, rank2, route)
    route_ref[...] = route


def _outproj_router(yr, ys, x2, mod3, npost, npre, wo_r, wo_s, w_router, b_router, seq_len):
    T, d = x2.shape
    tm = TM_OUT
    per_seq = seq_len // tm
    rw = yr.shape[1]
    tri = (jnp.arange(tm)[:, None] > jnp.arange(tm)[None, :]).astype(BF16)
    tok = lambda w: pl.BlockSpec((tm, w), lambda i: (i, 0))
    modv = lambda k: pl.BlockSpec((1, 1, d), lambda i: (i // per_seq, 0, k))
    return pl.pallas_call(
        _outproj_router_kernel,
        grid=(T // tm,),
        in_specs=[
            tok(rw), tok(rw), tok(d), modv(2), modv(3), modv(4),
            _const_spec((1, d)), _const_spec((1, d)),
            _const_spec(wo_r.shape), _const_spec(wo_s.shape), _const_spec(w_router.shape), _const_spec((1, LANES)),
            _const_spec((tm, tm)),
        ],
        out_specs=[tok(d), tok(d), tok(LANES), _const_spec((1, LANES))],
        out_shape=[jax.ShapeDtypeStruct((T, d), F32), jax.ShapeDtypeStruct((T, d), F32),
                   jax.ShapeDtypeStruct((T, LANES), F32), jax.ShapeDtypeStruct((1, LANES), F32)],
        scratch_shapes=[pltpu.VMEM((d, 2 * LANES), BF16), pltpu.VMEM((1, LANES), F32)],
        compiler_params=pltpu.CompilerParams(dimension_semantics=("arbitrary",),
                                             vmem_limit_bytes=48 * 1024 * 1024),
        name="outproj_router",
    )(yr, ys, x2, mod3, mod3, mod3, npost, npre, wo_r, wo_s, w_router, b_router, tri)


def _dispatch_kernel(pad_end_ref, dest_ref, h_ref, xs_hbm, zbuf, sem, zsem):
    i = pl.program_id(0)
    td = h_ref.shape[0]
    mb = zbuf.shape[0]

    @pl.when(i == 0)
    def _():
        zbuf[...] = jnp.zeros_like(zbuf)

        def fill(e):
            return pltpu.make_async_copy(zbuf, xs_hbm.at[pl.ds(pl.multiple_of(pad_end_ref[e] - mb, mb), mb)], zsem)

        def has_rows(e):
            return pad_end_ref[e] > jnp.where(e == 0, 0, pad_end_ref[jnp.maximum(e - 1, 0)])

        def start(e, carry):
            @pl.when(has_rows(e))
            def _():
                fill(e).start()
            return carry

        def wait(e, carry):
            @pl.when(has_rows(e))
            def _():
                fill(e).wait()
            return carry

        def tail_fill(blk):
            return pltpu.make_async_copy(zbuf, xs_hbm.at[pl.ds(pl.multiple_of(blk * mb, mb), mb)], zsem)

        def tail_start(blk, carry):
            @pl.when(blk * mb >= pad_end_ref[N_EXPERTS - 1])
            def _():
                tail_fill(blk).start()
            return carry

        def tail_wait(blk, carry):
            @pl.when(blk * mb >= pad_end_ref[N_EXPERTS - 1])
            def _():
                tail_fill(blk).wait()
            return carry

        n_blocks = xs_hbm.shape[0] // mb
        lax.fori_loop(0, N_EXPERTS, start, 0)
        lax.fori_loop(0, n_blocks, tail_start, 0)
        lax.fori_loop(0, N_EXPERTS, wait, 0)
        lax.fori_loop(0, n_blocks, tail_wait, 0)

    def issue(j, carry):
        for kk in range(TOP_K):
            d = dest_ref[0, 0, TOP_K * j + kk]
            pltpu.make_async_copy(h_ref.at[pl.ds(j, 1)], xs_hbm.at[pl.ds(d, 1)], sem).start()
        return carry

    lax.fori_loop(0, td, issue, 0)
    for kk in range(TOP_K):
        pltpu.make_async_copy(h_ref, xs_hbm.at[pl.ds(0, td)], sem).wait()


def _dispatch(pad_end, dest, h2, cap):
    T, d = h2.shape
    td = TD_DISPATCH
    nt = T // td
    grid_spec = pltpu.PrefetchScalarGridSpec(
        num_scalar_prefetch=1,
        grid=(nt,),
        in_specs=[
            pl.BlockSpec((1, 1, TOP_K * td), lambda i, pe: (i, 0, 0), memory_space=pltpu.SMEM),
            pl.BlockSpec((td, d), lambda i, pe: (i, 0)),
        ],
        out_specs=pl.BlockSpec(memory_space=pl.ANY),
        scratch_shapes=[pltpu.VMEM((MB_EXPERT, d), F32), pltpu.SemaphoreType.DMA(()), pltpu.SemaphoreType.DMA(())],
    )
    return pl.pallas_call(
        _dispatch_kernel,
        grid_spec=grid_spec,
        out_shape=jax.ShapeDtypeStruct((cap, d), F32),
        compiler_params=pltpu.CompilerParams(dimension_semantics=("arbitrary",)),
        name="dispatch",
    )(pad_end, dest.reshape(nt, 1, TOP_K * td), h2)


def _expert_kernel(be_ref, nused_ref, xs_ref, wg_ref, wu_ref, wd_ref, y_ref, wg_b, wu_b, wd_b):
    i = pl.program_id(0)
    prev = be_ref[jnp.maximum(i - 1, 0)]

    @pl.when((i == 0) | (be_ref[i] != prev))
    def _():
        wg_b[...] = wg_ref[0].astype(BF16)
        wu_b[...] = wu_ref[0].astype(BF16)
        wd_b[...] = wd_ref[0].astype(BF16)

    @pl.when(i < nused_ref[0])
    def _():
        xb = xs_ref[...].astype(BF16)
        hid = (_silu(_dot(xb, wg_b[...])) * _dot(xb, wu_b[...])).astype(BF16)
        y_ref[...] = _dot(hid, wd_b[...])

    @pl.when(i >= nused_ref[0])
    def _():
        y_ref[...] = jnp.zeros_like(y_ref)


def _experts(blk_expert, n_used, xs, w_gate, w_up, w_down):
    cap, d = xs.shape
    de = w_gate.shape[2]
    mb = MB_EXPERT
    grid_spec = pltpu.PrefetchScalarGridSpec(
        num_scalar_prefetch=2,
        grid=(cap // mb,),
        in_specs=[
            pl.BlockSpec((mb, d), lambda i, be, nu: (jnp.minimum(i, nu[0] - 1), 0)),
            pl.BlockSpec((1, d, de), lambda i, be, nu: (be[i], 0, 0)),
            pl.BlockSpec((1, d, de), lambda i, be, nu: (be[i], 0, 0)),
            pl.BlockSpec((1, de, d), lambda i, be, nu: (be[i], 0, 0)),
        ],
        out_specs=pl.BlockSpec((mb, d), lambda i, be, nu: (i, 0)),
        scratch_shapes=[pltpu.VMEM((d, de), BF16), pltpu.VMEM((d, de), BF16), pltpu.VMEM((de, d), BF16)],
    )
    return pl.pallas_call(
        _expert_kernel,
        grid_spec=grid_spec,
        out_shape=jax.ShapeDtypeStruct((cap, d), F32),
        compiler_params=pltpu.CompilerParams(dimension_semantics=("arbitrary",),
                                             vmem_limit_bytes=48 * 1024 * 1024),
        name="experts",
    )(blk_expert, n_used, xs, w_gate, w_up, w_down)


def _combine_kernel(dest_ref, route_ref, x1_ref, g2_ref, nw_ref, yb_hbm, o_ref, buf, sem):
    tf = x1_ref.shape[0]

    def row_copy(j, kk, d):
        return pltpu.make_async_copy(yb_hbm.at[pl.ds(d, 1)], buf.at[kk, pl.ds(j, 1)], sem)

    def issue(j, carry):
        for kk in range(TOP_K):
            row_copy(j, kk, dest_ref[0, 0, TOP_K * j + kk]).start()
        return carry

    lax.fori_loop(0, tf, issue, 0)

    for kk in range(TOP_K):
        pltpu.make_async_copy(yb_hbm.at[pl.ds(0, tf)], buf.at[kk], sem).wait()

    route = route_ref[...]
    out = route[:, 2:3] * buf[0] + route[:, 3:4] * buf[1]
    ms = jnp.mean(out * out, axis=-1, keepdims=True)
    o_ref[...] = x1_ref[...] + g2_ref[0] * ((out * lax.rsqrt(ms + EPS)) * nw_ref[...])


def _combine(dest, route, x1, mod3, nw, yb, seq_len):
    T, d = x1.shape
    tf = TF_COMBINE
    nt = T // tf
    per_seq = seq_len // tf
    return pl.pallas_call(
        _combine_kernel,
        grid=(nt,),
        in_specs=[
            pl.BlockSpec((1, 1, TOP_K * tf), lambda i: (i, 0, 0), memory_space=pltpu.SMEM),
            pl.BlockSpec((tf, LANES), lambda i: (i, 0)),
            pl.BlockSpec((tf, d), lambda i: (i, 0)),
            pl.BlockSpec((1, 1, d), lambda i: (i // per_seq, 0, 5)),
            _const_spec((1, d)),
            pl.BlockSpec(memory_space=pl.ANY),
        ],
        out_specs=pl.BlockSpec((tf, d), lambda i: (i, 0)),
        out_shape=jax.ShapeDtypeStruct((T, d), F32),
        scratch_shapes=[pltpu.VMEM((TOP_K, tf, d), F32), pltpu.SemaphoreType.DMA(())],
        name="combine",
    )(dest.reshape(nt, 1, TOP_K * tf), route, x1, mod3, nw, yb)


def _rope_tables(L, n_heads):
    quarter = RET_DK // 4
    freqs = ROPE_BASE ** (-jnp.arange(quarter, dtype=F32) / quarter)
    t = jnp.arange(L)
    ang_r = (t // GRID_W).astype(F32)[:, None] * freqs
    ang_c = (t % GRID_W).astype(F32)[:, None] * freqs
    cos = jnp.concatenate([jnp.cos(ang_r)] * 2 + [jnp.cos(ang_c)] * 2, axis=-1)
    sin = jnp.concatenate([-jnp.sin(ang_r), jnp.sin(ang_r), -jnp.sin(ang_c), jnp.sin(ang_c)], axis=-1)
    return jnp.tile(cos, (1, n_heads)), jnp.tile(sin, (1, n_heads))


def _lane_pad(v, width=LANES):
    return jnp.pad(v, [(0, 0)] * (v.ndim - 1) + [(0, width - v.shape[-1])])


def kernel(x, c, ctx, c_ctx, w_mod, b_mod, norm_pre_mix, norm_post_mix, norm_pre_ffn, norm_post_ffn, w_in, w_out, ret_decay_f, ret_decay_b, ret_gn_w, ssd_conv_w, ssd_conv_b, ssd_dt_bias_f, ssd_dt_bias_b, ssd_a_log_f, ssd_a_log_b, ssd_d, ssd_norm_w, moe_w_rg, moe_b_rg, moe_w_re, moe_b_re, moe_w_gate, moe_w_up, moe_w_down):
    b, L, d = x.shape
    assert w_mod.shape[0] == 1, "single layer: context outputs are never needed"
    rw = RET_HEADS * RET_DK
    nconv = SSD_WIDTH + 2 * SSD_GROUPS * SSD_STATE
    T = b * L

    mod_rows = -(-(b + 1) // SUBLANES) * SUBLANES
    c_all = jnp.zeros((mod_rows, d), F32).at[:b].set(c).at[b].set(c_ctx)
    mod3 = _modulation(c_all, w_mod[0], b_mod[0]).reshape(mod_rows, 1, 6 * d)

    wi = w_in[0]
    o = 0
    wq = wi[:, o:o + rw]; o += rw
    wk = wi[:, o:o + rw]; o += rw
    wv = wi[:, o:o + rw]; o += rw
    wg = wi[:, o:o + rw]; o += rw
    wz = wi[:, o:o + SSD_WIDTH]; o += SSD_WIDTH
    wxbc = wi[:, o:o + nconv].astype(BF16); o += nconv
    wdt = _lane_pad(wi[:, o:o + 2 * SSD_HEADS]).astype(BF16)
    wqk = jnp.concatenate([wq, wk], axis=1).astype(BF16)
    wvgz = jnp.concatenate([wv, wg, wz], axis=1).astype(BF16)
    cos_t, sin_t = _rope_tables(L, RET_HEADS)
    nw1 = norm_pre_mix[0].reshape(1, d)

    q, k, v, g, z, xbc, dt = _inproj(x, mod3, nw1, wqk, wvgz, wxbc, wdt, cos_t, sin_t)
    kc, vc, xbcc, dtc = _inproj_ctx(ctx, mod3, b, nw1, wk.astype(BF16), wv.astype(BF16), wxbc, wdt)

    conv_w8 = jnp.pad(ssd_conv_w[0], ((0, SUBLANES - SSD_CONV), (0, 0)))
    dt_bias = _lane_pad(jnp.concatenate([ssd_dt_bias_f[0], ssd_dt_bias_b[0]])[None, :])
    a_log = _lane_pad(jnp.concatenate([ssd_a_log_f[0], ssd_a_log_b[0]])[None, :])
    d_skip = jnp.repeat(ssd_d[0], SSD_HEADDIM)[None, :]
    ys = _ssd(xbc, z, dt, xbcc, dtc, conv_w8, ssd_conv_b[0][None, :], dt_bias, a_log, d_skip,
              ssd_norm_w[0][None, :])

    yr = _retention(q, k, v, g, kc, vc,
                    jnp.repeat(ret_decay_f[0], RET_DK)[None, :], jnp.repeat(ret_decay_b[0], RET_DK)[None, :],
                    ret_gn_w[0][None, :])

    wo = w_out[0].astype(BF16)
    w_router = _lane_pad(jnp.concatenate(
        [jnp.transpose(moe_w_re[0], (1, 0, 2)).reshape(d, N_EXPERTS), moe_w_rg[0]], axis=1))
    b_router = _lane_pad(jnp.concatenate([moe_b_re[0].reshape(-1), moe_b_rg[0]])[None, :])
    x1, h2, route, counts = _outproj_router(
        yr.reshape(T, rw), ys.reshape(T, SSD_WIDTH), x.reshape(T, d), mod3,
        norm_post_mix[0][None, :], norm_pre_ffn[0][None, :], wo[:rw], wo[rw:], w_router, b_router, L)

    mb = MB_EXPERT
    n_blocks = -(-(T * TOP_K + N_EXPERTS * (mb - 1)) // mb)
    cap = n_blocks * mb
    cnt = counts[0, :N_EXPERTS].astype(jnp.int32)
    padded = (cnt + mb - 1) // mb * mb
    pad_end = jnp.cumsum(padded)
    pad_start = pad_end - padded
    expert = route[:, 0:TOP_K].astype(jnp.int32)
    dest = (pad_start[expert] + route[:, 4:4 + TOP_K].astype(jnp.int32)).reshape(-1)
    blk_start = jnp.arange(n_blocks, dtype=jnp.int32) * mb
    blk_expert = jnp.minimum(jnp.sum((pad_end[None, :] <= blk_start[:, None]).astype(jnp.int32), axis=1),
                             N_EXPERTS - 1)
    n_used = (pad_end[-1:] // mb).astype(jnp.int32)

    xs = _dispatch(pad_end.astype(jnp.int32), dest, h2, cap)
    yb = _experts(blk_expert, n_used, xs, moe_w_gate[0], moe_w_up[0], moe_w_down[0])
    out = _combine(dest, route, x1, mod3, norm_post_ffn[0][None, :], yb, L)
    return out.reshape(b, L, d)
```

```python
import functools
import math

import jax
import jax.numpy as jnp
from jax import lax
from jax.experimental import pallas as pl
from jax.experimental.pallas import tpu as pltpu

F32 = jnp.float32
BF16 = jnp.bfloat16
HIGHEST = lax.Precision.HIGHEST

LANES = 128
SUBLANES = 8

EPS = 1e-6
CHUNK = 128
GRID_W = 64
RET_HEADS = 4
RET_DK = 128
ROPE_BASE = 10000.0
SSD_HEADS = 8
SSD_HEADDIM = 64
SSD_GROUPS = 2
SSD_STATE = 128
SSD_WIDTH = SSD_HEADS * SSD_HEADDIM
SSD_CONV = 5
SSD_PAIRS = SSD_WIDTH // LANES
MOE_GROUPS = 4
EXPERTS_PER_GROUP = 8
N_EXPERTS = MOE_GROUPS * EXPERTS_PER_GROUP
TOP_K = 2
CONV_HALO = SUBLANES

TM_PROJ = 512
TM_OUT = 512
TD_DISPATCH = 512
MB_EXPERT = 256
TF_COMBINE = 256
DMA_UNROLL = 8
NEG_BIG = -1e30


def _silu(v):
    return v * jax.nn.sigmoid(v)


def _dot(a, b):
    return jnp.dot(a, b, preferred_element_type=F32)


def _dot_tn(a, b):
    return lax.dot_general(a, b, (((0,), (0,)), ((), ())), preferred_element_type=F32)


def _dot_nt(a, b):
    return lax.dot_general(a, b, (((1,), (1,)), ((), ())), preferred_element_type=F32)


def _mod_kernel(c_ref, w_ref, b_ref, o_ref):
    a = _silu(c_ref[...])
    o_ref[...] = jnp.dot(a, w_ref[...], precision=HIGHEST, preferred_element_type=F32) + b_ref[...]


def _modulation(c_all, w_mod, b_mod):
    rows, d = c_all.shape
    n = w_mod.shape[1]
    return pl.pallas_call(
        _mod_kernel,
        grid=(n // d,),
        in_specs=[
            pl.BlockSpec((rows, d), lambda j: (0, 0)),
            pl.BlockSpec((d, d), lambda j: (0, j)),
            pl.BlockSpec((1, d), lambda j: (0, j)),
        ],
        out_specs=pl.BlockSpec((rows, d), lambda j: (0, j)),
        out_shape=jax.ShapeDtypeStruct((rows, n), F32),
        name="modulation",
    )(c_all, w_mod, b_mod.reshape(1, n))


def _norm_mod(x, nw, sc, sh):
    ms = jnp.mean(x * x, axis=-1, keepdims=True)
    return (x * lax.rsqrt(ms + EPS)) * nw * (1.0 + sc) + sh


def _rope(t, cos, sin_signed, first_half):
    width = t.shape[-1]
    quarter = RET_DK // 4
    swapped = jnp.where(first_half, pltpu.roll(t, width - quarter, 1), pltpu.roll(t, quarter, 1))
    return t * cos + swapped * sin_signed


def _inproj_kernel(x_ref, sh_ref, sc_ref, nw_ref, wqk_ref, wvgz_ref, wxbc_ref, wdt_ref, cos_ref, sin_ref,
                   q_ref, k_ref, v_ref, g_ref, z_ref, xbc_ref, dt_ref):
    hb = _norm_mod(x_ref[0], nw_ref[...], sc_ref[0], sh_ref[0]).astype(BF16)
    rw = q_ref.shape[-1]
    qk = _dot(hb, wqk_ref[...])
    cos = cos_ref[...]
    sin = sin_ref[...]
    lane = lax.broadcasted_iota(jnp.int32, cos.shape, 1)
    first_half = (lane % (RET_DK // 2)) < (RET_DK // 4)
    q_ref[0] = _rope(qk[:, :rw], cos, sin, first_half).astype(BF16)
    k_ref[0] = (_rope(qk[:, rw:], cos, sin, first_half) * (RET_DK ** -0.5)).astype(BF16)
    vgz = _dot(hb, wvgz_ref[...])
    v_ref[0] = vgz[:, :rw].astype(BF16)
    g_ref[0] = vgz[:, rw:2 * rw].astype(BF16)
    z_ref[0] = vgz[:, 2 * rw:].astype(BF16)
    xbc_ref[0] = _dot(hb, wxbc_ref[...]).astype(BF16)
    dt_ref[0] = _dot(hb, wdt_ref[...])


def _inproj_ctx_kernel(x_ref, sh_ref, sc_ref, nw_ref, wk_ref, wv_ref, wxbc_ref, wdt_ref,
                       k_ref, v_ref, xbc_ref, dt_ref):
    hb = _norm_mod(x_ref[0], nw_ref[...], sc_ref[0], sh_ref[0]).astype(BF16)
    k_ref[0] = (_dot(hb, wk_ref[...]) * (RET_DK ** -0.5)).astype(BF16)
    v_ref[0] = _dot(hb, wv_ref[...]).astype(BF16)
    xbc_ref[0] = _dot(hb, wxbc_ref[...]).astype(BF16)
    dt_ref[0] = _dot(hb, wdt_ref[...])


def _const_spec(shape):
    nd = len(shape)
    return pl.BlockSpec(shape, lambda *_: (0,) * nd)


def _inproj(x, mod3, nw, wqk, wvgz, wxbc, wdt, cos_t, sin_t):
    b, L, d = x.shape
    tm = min(TM_PROJ, L)
    rw = wqk.shape[1] // 2
    tok = lambda w: pl.BlockSpec((1, tm, w), lambda i, j: (i, j, 0))
    out_bf = lambda w: jax.ShapeDtypeStruct((b, L, w), BF16)
    return pl.pallas_call(
        _inproj_kernel,
        grid=(b, L // tm),
        in_specs=[
            tok(d),
            pl.BlockSpec((1, 1, d), lambda i, j: (i, 0, 0)),
            pl.BlockSpec((1, 1, d), lambda i, j: (i, 0, 1)),
            _const_spec((1, d)),
            _const_spec(wqk.shape), _const_spec(wvgz.shape), _const_spec(wxbc.shape), _const_spec(wdt.shape),
            pl.BlockSpec((tm, rw), lambda i, j: (j, 0)),
            pl.BlockSpec((tm, rw), lambda i, j: (j, 0)),
        ],
        out_specs=[tok(rw), tok(rw), tok(rw), tok(rw), tok(rw), tok(wxbc.shape[1]), tok(LANES)],
        out_shape=[out_bf(rw), out_bf(rw), out_bf(rw), out_bf(rw), out_bf(rw), out_bf(wxbc.shape[1]),
                   jax.ShapeDtypeStruct((b, L, LANES), F32)],
        compiler_params=pltpu.CompilerParams(vmem_limit_bytes=48 * 1024 * 1024),
        name="inproj",
    )(x, mod3, mod3, nw, wqk, wvgz, wxbc, wdt, cos_t, sin_t)


def _inproj_ctx(ctx, mod3, ctx_row, nw, wk, wv, wxbc, wdt):
    b, L, d = ctx.shape
    tm = min(TM_PROJ, L)
    rw = wk.shape[1]
    tok = lambda w: pl.BlockSpec((1, tm, w), lambda i, j: (i, j, 0))
    out_bf = lambda w: jax.ShapeDtypeStruct((b, L, w), BF16)
    return pl.pallas_call(
        _inproj_ctx_kernel,
        grid=(b, L // tm),
        in_specs=[
            tok(d),
            pl.BlockSpec((1, 1, d), lambda i, j: (ctx_row, 0, 0)),
            pl.BlockSpec((1, 1, d), lambda i, j: (ctx_row, 0, 1)),
            _const_spec((1, d)),
            _const_spec(wk.shape), _const_spec(wv.shape), _const_spec(wxbc.shape), _const_spec(wdt.shape),
        ],
        out_specs=[tok(rw), tok(rw), tok(wxbc.shape[1]), tok(LANES)],
        out_shape=[out_bf(rw), out_bf(rw), out_bf(wxbc.shape[1]), jax.ShapeDtypeStruct((b, L, LANES), F32)],
        compiler_params=pltpu.CompilerParams(vmem_limit_bytes=48 * 1024 * 1024),
        name="inproj_ctx",
    )(ctx, mod3, mod3, nw, wk, wv, wxbc, wdt)


def _ssd_kernel(xbc_ref, z_ref, dt_ref, xbcc_ref, dtc_ref, cw_ref, cb_ref, dtb_ref, alog_ref, dsk_ref, nw_ref,
                y_ref,
                xpad, xpadc, u, uc, dtv, dav, dtcv, dacv, sf_scr, sb_scr):
    L = xbc_ref.shape[1]
    Lc = xbcc_ref.shape[1]
    nch = L // CHUNK
    nchc = Lc // CHUNK
    win = CHUNK + 2 * CONV_HALO
    nconv = xbc_ref.shape[2]
    nh = SSD_HEADS

    def conv_pass(src_ref, pad_ref, dst_ref, n_chunks, length):
        zeros = jnp.zeros((CONV_HALO, nconv), F32)
        pad_ref[0:CONV_HALO, :] = zeros
        pad_ref[CONV_HALO + length:2 * CONV_HALO + length, :] = zeros
        pad_ref[CONV_HALO:CONV_HALO + length, :] = src_ref[0].astype(F32)

        def chunk(c, carry):
            base = pl.multiple_of(c * CHUNK, CHUNK)
            for cb_i in range(nconv // LANES):
                cols = slice(cb_i * LANES, (cb_i + 1) * LANES)
                w = pad_ref[pl.ds(base, win), cols]
                acc = cb_ref[:, cols] + w[CONV_HALO:CONV_HALO + CHUNK] * cw_ref[SSD_CONV // 2:SSD_CONV // 2 + 1, cols]
                for j in range(SSD_CONV):
                    if j == SSD_CONV // 2:
                        continue
                    shifted = pltpu.roll(w, (SSD_CONV // 2 - j) % win, 0)
                    acc = acc + shifted[CONV_HALO:CONV_HALO + CHUNK] * cw_ref[j:j + 1, cols]
                dst_ref[pl.ds(base, CHUNK), cols] = _silu(acc).astype(BF16)
            return carry

        lax.fori_loop(0, n_chunks, chunk, 0)

    conv_pass(xbcc_ref, xpadc, uc, nchc, Lc)
    conv_pass(xbc_ref, xpad, u, nch, L)

    a_neg = -jnp.exp(alog_ref[...])
    dtv[...] = jax.nn.softplus(dt_ref[0] + dtb_ref[...])
    dav[...] = dtv[...] * a_neg
    dtcv[...] = jax.nn.softplus(dtc_ref[0] + dtb_ref[...])
    dacv[...] = dtcv[...] * a_neg

    row_i = lax.broadcasted_iota(jnp.int32, (CHUNK, CHUNK), 0)
    col_i = lax.broadcasted_iota(jnp.int32, (CHUNK, CHUNK), 1)
    causal = col_i <= row_i
    tri = causal.astype(F32)
    tri_t = (row_i <= col_i).astype(F32)
    lo_half = col_i < SSD_HEADDIM

    def colb(mat, r):
        return jnp.broadcast_to(mat[:, r:r + 1], (CHUNK, CHUNK))

    def pair_sel(a, b_):
        return jnp.where(lo_half, a, b_)

    def last_row(mat, r):
        return jnp.broadcast_to(mat[CHUNK - 1:CHUNK, r:r + 1], (CHUNK, CHUNK))

    def cumsums(da):
        acol = jnp.dot(tri, da, precision=HIGHEST, preferred_element_type=F32)
        return acol, acol - da

    def state_update(u_ref, dt_s, da_s, c, s_old, backward):
        base = pl.multiple_of(c * CHUNK, CHUNK)
        dt = dt_s[pl.ds(base, CHUNK), :]
        da = da_s[pl.ds(base, CHUNK), :]
        acol, ecol = cumsums(da)
        off = nh if backward else 0
        new = []
        for g in range(SSD_GROUPS):
            xw = []
            dec = []
            for pp in range(SSD_PAIRS // SSD_GROUPS):
                p = g * (SSD_PAIRS // SSD_GROUPS) + pp
                r0, r1 = off + 2 * p, off + 2 * p + 1
                xs = u_ref[pl.ds(base, CHUNK), p * LANES:(p + 1) * LANES].astype(F32)
                if backward:
                    wgt = pair_sel(jnp.exp(colb(ecol, r0)) * colb(dt, r0), jnp.exp(colb(ecol, r1)) * colb(dt, r1))
                else:
                    wgt = pair_sel(jnp.exp(last_row(acol, r0) - colb(acol, r0)) * colb(dt, r0),
                                   jnp.exp(last_row(acol, r1) - colb(acol, r1)) * colb(dt, r1))
                xw.append((xs * wgt).astype(BF16))
                dec.append(pair_sel(jnp.exp(last_row(acol, r0)), jnp.exp(last_row(acol, r1))))
            xw = jnp.concatenate(xw, axis=1)
            dec = jnp.concatenate(dec, axis=1)
            bm = u_ref[pl.ds(base, CHUNK), SSD_WIDTH + g * SSD_STATE:SSD_WIDTH + (g + 1) * SSD_STATE]
            new.append(dec * s_old[g] + _dot_tn(bm, xw))
        return new

    zero_state = [jnp.zeros((SSD_STATE, 2 * LANES), F32) for _ in range(SSD_GROUPS)]

    s = zero_state
    for c in range(nchc):
        s = state_update(uc, dtcv, dacv, c, s, False)
    for g in range(SSD_GROUPS):
        sf_scr[0, g] = s[g]
    s = zero_state
    for c in reversed(range(nchc)):
        s = state_update(uc, dtcv, dacv, c, s, True)
    for g in range(SSD_GROUPS):
        sb_scr[g] = s[g]

    def fwd(c, carry):
        s_old = [sf_scr[c, g] for g in range(SSD_GROUPS)]
        s_new = state_update(u, dtv, dav, c, s_old, False)
        for g in range(SSD_GROUPS):
            sf_scr[c + 1, g] = s_new[g]
        return carry

    lax.fori_loop(0, nch, fwd, 0)

    def bwd(i, carry):
        c = nch - 1 - i
        base = pl.multiple_of(c * CHUNK, CHUNK)
        dt = dtv[pl.ds(base, CHUNK), :]
        da = dav[pl.ds(base, CHUNK), :]
        acol, ecol = cumsums(da)
        da_t = da.T
        dt_t = dt.T
        arow = jnp.dot(da_t, tri_t, precision=HIGHEST, preferred_element_type=F32)
        erow = arow - da_t
        s_b = [sb_scr[g] for g in range(SSD_GROUPS)]
        ys = []
        for g in range(SSD_GROUPS):
            bm = u[pl.ds(base, CHUNK), SSD_WIDTH + g * SSD_STATE:SSD_WIDTH + (g + 1) * SSD_STATE]
            cm = u[pl.ds(base, CHUNK), SSD_WIDTH + (SSD_GROUPS + g) * SSD_STATE:SSD_WIDTH + (SSD_GROUPS + g + 1) * SSD_STATE]
            cbm = _dot_nt(cm, bm)
            cs_f = _dot(cm, sf_scr[c, g].astype(BF16))
            cs_b = _dot(cm, s_b[g].astype(BF16))
            for pp in range(SSD_PAIRS // SSD_GROUPS):
                p = g * (SSD_PAIRS // SSD_GROUPS) + pp
                xs_b = u[pl.ds(base, CHUNK), p * LANES:(p + 1) * LANES]
                y_h = []
                e_f = []
                e_b = []
                for hh in range(2):
                    r = 2 * p + hh
                    af_c = colb(acol, r)
                    eb_c = colb(ecol, nh + r)
                    arg = jnp.where(causal, af_c - arow[r:r + 1, :], erow[nh + r:nh + r + 1, :] - eb_c)
                    coef = jnp.where(causal, dt_t[r:r + 1, :], dt_t[nh + r:nh + r + 1, :])
                    gm = (cbm * (jnp.exp(arg) * coef)).astype(BF16)
                    y_h.append(_dot(gm, xs_b))
                    e_f.append(jnp.exp(af_c))
                    e_b.append(jnp.exp(last_row(acol, nh + r) - eb_c))
                sl = slice(pp * LANES, (pp + 1) * LANES)
                y_p = (pair_sel(y_h[0], y_h[1]) + cs_f[:, sl] * pair_sel(e_f[0], e_f[1])
                       + cs_b[:, sl] * pair_sel(e_b[0], e_b[1])
                       + dsk_ref[:, p * LANES:(p + 1) * LANES] * xs_b.astype(F32))
                ys.append(y_p)
        y = jnp.concatenate(ys, axis=1)
        y = y * _silu(z_ref[0, pl.ds(base, CHUNK), :].astype(F32))
        ms = jnp.mean(y * y, axis=-1, keepdims=True)
        y_ref[0, pl.ds(base, CHUNK), :] = ((y * lax.rsqrt(ms + EPS)) * nw_ref[...]).astype(BF16)
        s_new = state_update(u, dtv, dav, c, s_b, True)
        for g in range(SSD_GROUPS):
            sb_scr[g] = s_new[g]
        return carry

    lax.fori_loop(0, nch, bwd, 0)


def _ssd(xbc, z, dt, xbcc, dtc, conv_w8, conv_b, dt_bias, a_log, d_skip, norm_w):
    b, L, nconv = xbc.shape
    Lc = xbcc.shape[1]
    nch = L // CHUNK
    per_b = lambda n, w: pl.BlockSpec((1, n, w), lambda i: (i, 0, 0))
    return pl.pallas_call(
        _ssd_kernel,
        grid=(b,),
        in_specs=[
            per_b(L, nconv), per_b(L, SSD_WIDTH), per_b(L, LANES), per_b(Lc, nconv), per_b(Lc, LANES),
            _const_spec(conv_w8.shape), _const_spec(conv_b.shape), _const_spec(dt_bias.shape),
            _const_spec(a_log.shape), _const_spec(d_skip.shape), _const_spec(norm_w.shape),
        ],
        out_specs=per_b(L, SSD_WIDTH),
        out_shape=jax.ShapeDtypeStruct((b, L, SSD_WIDTH), BF16),
        scratch_shapes=[
            pltpu.VMEM((L + 2 * CONV_HALO, nconv), F32),
            pltpu.VMEM((Lc + 2 * CONV_HALO, nconv), F32),
            pltpu.VMEM((L, nconv), BF16),
            pltpu.VMEM((Lc, nconv), BF16),
            pltpu.VMEM((L, LANES), F32), pltpu.VMEM((L, LANES), F32),
            pltpu.VMEM((Lc, LANES), F32), pltpu.VMEM((Lc, LANES), F32),
            pltpu.VMEM((nch + 1, SSD_GROUPS, SSD_STATE, 2 * LANES), F32),
            pltpu.VMEM((SSD_GROUPS, SSD_STATE, 2 * LANES), F32),
        ],
        compiler_params=pltpu.CompilerParams(vmem_limit_bytes=56 * 1024 * 1024),
        name="ssd",
    )(xbc, z, dt, xbcc, dtc, conv_w8, conv_b, dt_bias, a_log, d_skip, norm_w)


def _ret_kernel(q_ref, k_ref, v_ref, g_ref, kc_ref, vc_ref, df_ref, db_ref, gn_ref, y_ref, sf_scr, sb_scr):
    L = q_ref.shape[1]
    Lc = kc_ref.shape[1]
    nch = L // CHUNK
    dk = RET_DK
    row_i = lax.broadcasted_iota(jnp.int32, (CHUNK, dk), 0).astype(F32)
    col_i = lax.broadcasted_iota(jnp.int32, (CHUNK, dk), 1).astype(F32)
    rel = row_i - col_i
    crow = lax.broadcasted_iota(jnp.int32, (Lc, dk), 0).astype(F32)

    heads = []
    for h in range(RET_HEADS):
        cols = slice(h * dk, (h + 1) * dk)
        lg_f = -jnp.exp(df_ref[:, cols])
        lg_b = -jnp.exp(db_ref[:, cols])
        heads.append(dict(
            cols=cols,
            dmat=jnp.where(rel >= 0, jnp.exp(jnp.maximum(rel, 0.0) * lg_f), jnp.exp(jnp.maximum(-rel, 0.0) * lg_b)),
            dq_f=jnp.exp((row_i + 1.0) * lg_f),
            dq_b=jnp.exp((CHUNK - row_i) * lg_b),
            dk_f=jnp.exp((CHUNK - 1.0 - row_i) * lg_f),
            dk_b=jnp.exp(row_i * lg_b),
            dc_f=jnp.exp(CHUNK * lg_f),
            dc_b=jnp.exp(CHUNK * lg_b),
        ))
        kc = kc_ref[0, :, cols].astype(F32)
        vc = vc_ref[0, :, cols]
        sf_scr[0, h] = _dot_tn((kc * jnp.exp((Lc - 1.0 - crow) * lg_f)).astype(BF16), vc)
        sb_scr[h] = _dot_tn((kc * jnp.exp(crow * lg_b)).astype(BF16), vc)

    def fwd(c, carry):
        base = pl.multiple_of(c * CHUNK, CHUNK)
        for h, hd in enumerate(heads):
            kk = k_ref[0, pl.ds(base, CHUNK), hd["cols"]].astype(F32)
            vv = v_ref[0, pl.ds(base, CHUNK), hd["cols"]]
            sf_scr[c + 1, h] = hd["dc_f"] * sf_scr[c, h] + _dot_tn((kk * hd["dk_f"]).astype(BF16), vv)
        return carry

    lax.fori_loop(0, nch, fwd, 0)

    def bwd(i, carry):
        c = nch - 1 - i
        base = pl.multiple_of(c * CHUNK, CHUNK)
        for h, hd in enumerate(heads):
            qq = q_ref[0, pl.ds(base, CHUNK), hd["cols"]]
            kk = k_ref[0, pl.ds(base, CHUNK), hd["cols"]]
            vv = v_ref[0, pl.ds(base, CHUNK), hd["cols"]]
            s_b = sb_scr[h]
            scores = (_dot_nt(qq, kk) * hd["dmat"]).astype(BF16)
            y = (_dot(scores, vv)
                 + _dot(qq, sf_scr[c, h].astype(BF16)) * hd["dq_f"]
                 + _dot(qq, s_b.astype(BF16)) * hd["dq_b"])
            mu = jnp.mean(y, axis=-1, keepdims=True)
            yc = y - mu
            var = jnp.mean(yc * yc, axis=-1, keepdims=True)
            yn = (yc * lax.rsqrt(var + EPS)) * gn_ref[:, hd["cols"]]
            gate = _silu(g_ref[0, pl.ds(base, CHUNK), hd["cols"]].astype(F32))
            y_ref[0, pl.ds(base, CHUNK), hd["cols"]] = (yn * gate).astype(BF16)
            sb_scr[h] = hd["dc_b"] * s_b + _dot_tn((kk.astype(F32) * hd["dk_b"]).astype(BF16), vv)
        return carry

    lax.fori_loop(0, nch, bwd, 0)


def _retention(q, k, v, g, kc, vc, decay_f, decay_b, gn_w):
    b, L, w = q.shape
    Lc = kc.shape[1]
    nch = L // CHUNK
    per_b = lambda n: pl.BlockSpec((1, n, w), lambda i: (i, 0, 0))
    return pl.pallas_call(
        _ret_kernel,
        grid=(b,),
        in_specs=[per_b(L), per_b(L), per_b(L), per_b(L), per_b(Lc), per_b(Lc),
                  _const_spec((1, w)), _const_spec((1, w)), _const_spec((1, w))],
        out_specs=per_b(L),
        out_shape=jax.ShapeDtypeStruct((b, L, w), BF16),
        scratch_shapes=[
            pltpu.VMEM((nch + 1, RET_HEADS, RET_DK, RET_DK), F32),
            pltpu.VMEM((RET_HEADS, RET_DK, RET_DK), F32),
        ],
        compiler_params=pltpu.CompilerParams(vmem_limit_bytes=48 * 1024 * 1024),
        name="retention",
    )(q, k, v, g, kc, vc, decay_f, decay_b, gn_w)


def _outproj_router_kernel(yr_ref, ys_ref, x_ref, g1_ref, sh2_ref, sc2_ref, npost_ref, npre_ref,
                           wor_ref, wos_ref, wr_ref, br_ref, tri_ref,
                           x1_ref, h2_ref, route_ref, cnt_ref, slots_ref,
                           wcat, carry):
    i = pl.program_id(0)

    @pl.when(i == 0)
    def _():
        wr = wr_ref[...]
        hi = wr.astype(BF16)
        wcat[:, :LANES] = hi
        wcat[:, LANES:] = (wr - hi.astype(F32)).astype(BF16)
        carry[...] = jnp.zeros_like(carry)

    y = _dot(yr_ref[...], wor_ref[...]) + _dot(ys_ref[...], wos_ref[...])
    ms = jnp.mean(y * y, axis=-1, keepdims=True)
    x1 = x_ref[...] + g1_ref[0] * ((y * lax.rsqrt(ms + EPS)) * npost_ref[...])
    x1_ref[...] = x1
    h2 = _norm_mod(x1, npre_ref[...], sc2_ref[0], sh2_ref[0])
    h2_ref[...] = h2

    h_hi = h2.astype(BF16)
    h_lo = (h2 - h_hi.astype(F32)).astype(BF16)
    both = _dot(h_hi, wcat[...])
    lg = both[:, :LANES] + both[:, LANES:] + _dot(h_lo, wcat[:, :LANES]) + br_ref[...]

    tm = lg.shape[0]
    lane = lax.broadcasted_iota(jnp.int32, (tm, LANES), 1)
    lane_f = lane.astype(F32)
    is_grp = (lane >= N_EXPERTS) & (lane < N_EXPERTS + MOE_GROUPS)
    gl = jnp.where(is_grp, lg, NEG_BIG)
    mg = jnp.max(gl, axis=-1, keepdims=True)
    grp_lane = jnp.min(jnp.where(gl == mg, lane_f, 1e9), axis=-1, keepdims=True)
    p_g = 1.0 / jnp.sum(jnp.where(is_grp, jnp.exp(gl - mg), 0.0), axis=-1, keepdims=True)
    first = (grp_lane - N_EXPERTS) * EXPERTS_PER_GROUP
    in_grp = (lane_f >= first) & (lane_f < first + EXPERTS_PER_GROUP)
    el = jnp.where(in_grp, lg, NEG_BIG)
    t1 = jnp.max(el, axis=-1, keepdims=True)
    i1 = jnp.min(jnp.where(el == t1, lane_f, 1e9), axis=-1, keepdims=True)
    el2 = jnp.where(lane_f == i1, NEG_BIG, el)
    t2 = jnp.max(el2, axis=-1, keepdims=True)
    i2 = jnp.min(jnp.where(el2 == t2, lane_f, 1e9), axis=-1, keepdims=True)
    s = jnp.exp(t2 - t1)
    w1 = p_g / (1.0 + s)
    w2 = p_g * s / (1.0 + s)

    oh1 = (lane_f == i1)
    oh2 = (lane_f == i2)
    oh = (oh1 | oh2).astype(BF16)
    before = _dot(tri_ref[...], oh) + carry[...]
    rank1 = jnp.sum(jnp.where(oh1, before, 0.0), axis=-1, keepdims=True)
    rank2 = jnp.sum(jnp.where(oh2, before, 0.0), axis=-1, keepdims=True)
    total = carry[...] + jnp.sum(oh.astype(F32), axis=0, keepdims=True)
    carry[...] = total
    cnt_ref[...] = total

    route_ref[...] = jnp.where(lane == 0, w1, jnp.where(lane == 1, w2, 0.0))

    row = lax.broadcasted_iota(jnp.int32, (tm, LANES), 0)
    on_diag = (row % LANES) == lane

    def dense(col):
        picked = jnp.where(on_diag, col, 0.0)
        return jnp.sum(picked.reshape(tm // LANES, LANES, LANES), axis=1)

    slots_ref[0] = jnp.concatenate([dense(i1), dense(i2), dense(rank1), dense(rank2)], axis=0).astype(jnp.int32)


def _outproj_router(yr, ys, x2, mod3, npost, npre, wo_r, wo_s, w_router, b_router, seq_len):
    T, d = x2.shape
    tm = TM_OUT
    per_seq = seq_len // tm
    rw = yr.shape[1]
    tri = (jnp.arange(tm)[:, None] > jnp.arange(tm)[None, :]).astype(BF16)
    tok = lambda w: pl.BlockSpec((tm, w), lambda i: (i, 0))
    modv = lambda k: pl.BlockSpec((1, 1, d), lambda i: (i // per_seq, 0, k))
    return pl.pallas_call(
        _outproj_router_kernel,
        grid=(T // tm,),
        in_specs=[
            tok(rw), tok(rw), tok(d), modv(2), modv(3), modv(4),
            _const_spec((1, d)), _const_spec((1, d)),
            _const_spec(wo_r.shape), _const_spec(wo_s.shape), _const_spec(w_router.shape), _const_spec((1, LANES)),
            _const_spec((tm, tm)),
        ],
        out_specs=[tok(d), tok(d), tok(LANES), _const_spec((1, LANES)),
                   pl.BlockSpec((1, 2 * TOP_K * (tm // LANES), LANES), lambda i: (i, 0, 0))],
        out_shape=[jax.ShapeDtypeStruct((T, d), F32), jax.ShapeDtypeStruct((T, d), F32),
                   jax.ShapeDtypeStruct((T, LANES), F32), jax.ShapeDtypeStruct((1, LANES), F32),
                   jax.ShapeDtypeStruct((T // tm, 2 * TOP_K * (tm // LANES), LANES), jnp.int32)],
        scratch_shapes=[pltpu.VMEM((d, 2 * LANES), BF16), pltpu.VMEM((1, LANES), F32)],
        compiler_params=pltpu.CompilerParams(dimension_semantics=("arbitrary",),
                                             vmem_limit_bytes=48 * 1024 * 1024),
        name="outproj_router",
    )(yr, ys, x2, mod3, mod3, mod3, npost, npre, wo_r, wo_s, w_router, b_router, tri)


def _dispatch_kernel(pad_end_ref, dest_ref, h_ref, xs_hbm, zbuf, sem, zsem):
    i = pl.program_id(0)
    td = h_ref.shape[0]
    mb = zbuf.shape[0]

    @pl.when(i == 0)
    def _():
        zbuf[...] = jnp.zeros_like(zbuf)

        def fill(e):
            return pltpu.make_async_copy(zbuf, xs_hbm.at[pl.ds(pl.multiple_of(pad_end_ref[e] - mb, mb), mb)], zsem)

        def has_rows(e):
            return pad_end_ref[e] > jnp.where(e == 0, 0, pad_end_ref[jnp.maximum(e - 1, 0)])

        def start(e, carry):
            @pl.when(has_rows(e))
            def _():
                fill(e).start()
            return carry

        def wait(e, carry):
            @pl.when(has_rows(e))
            def _():
                fill(e).wait()
            return carry

        def tail_fill(blk):
            return pltpu.make_async_copy(zbuf, xs_hbm.at[pl.ds(pl.multiple_of(blk * mb, mb), mb)], zsem)

        def tail_start(blk, carry):
            @pl.when(blk * mb >= pad_end_ref[N_EXPERTS - 1])
            def _():
                tail_fill(blk).start()
            return carry

        def tail_wait(blk, carry):
            @pl.when(blk * mb >= pad_end_ref[N_EXPERTS - 1])
            def _():
                tail_fill(blk).wait()
            return carry

        n_blocks = xs_hbm.shape[0] // mb
        lax.fori_loop(0, N_EXPERTS, start, 0)
        lax.fori_loop(0, n_blocks, tail_start, 0)
        lax.fori_loop(0, N_EXPERTS, wait, 0)
        lax.fori_loop(0, n_blocks, tail_wait, 0)

    def issue(j, carry):
        for kk in range(TOP_K):
            d = dest_ref[0, 0, kk * td + j]
            pltpu.make_async_copy(h_ref.at[pl.ds(j, 1)], xs_hbm.at[pl.ds(d, 1)], sem).start()
        return carry

    lax.fori_loop(0, td, issue, 0, unroll=DMA_UNROLL)
    for kk in range(TOP_K):
        pltpu.make_async_copy(h_ref, xs_hbm.at[pl.ds(0, td)], sem).wait()


def _slot_blocks(dest, tile):
    nt = dest.shape[1] // tile
    return dest.reshape(TOP_K, nt, 1, tile).transpose(1, 2, 0, 3).reshape(nt, 1, TOP_K * tile)


def _dispatch(pad_end, dest, h2, cap):
    T, d = h2.shape
    td = TD_DISPATCH
    nt = T // td
    grid_spec = pltpu.PrefetchScalarGridSpec(
        num_scalar_prefetch=1,
        grid=(nt,),
        in_specs=[
            pl.BlockSpec((1, 1, TOP_K * td), lambda i, pe: (i, 0, 0), memory_space=pltpu.SMEM),
            pl.BlockSpec((td, d), lambda i, pe: (i, 0)),
        ],
        out_specs=pl.BlockSpec(memory_space=pl.ANY),
        scratch_shapes=[pltpu.VMEM((MB_EXPERT, d), F32), pltpu.SemaphoreType.DMA(()), pltpu.SemaphoreType.DMA(())],
    )
    return pl.pallas_call(
        _dispatch_kernel,
        grid_spec=grid_spec,
        out_shape=jax.ShapeDtypeStruct((cap, d), F32),
        compiler_params=pltpu.CompilerParams(dimension_semantics=("arbitrary",)),
        name="dispatch",
    )(pad_end, _slot_blocks(dest, td), h2)


def _expert_kernel(be_ref, nused_ref, xs_ref, wg_ref, wu_ref, wd_ref, y_ref, wg_b, wu_b, wd_b):
    i = pl.program_id(0)
    prev = be_ref[jnp.maximum(i - 1, 0)]

    @pl.when((i == 0) | (be_ref[i] != prev))
    def _():
        wg_b[...] = wg_ref[0].astype(BF16)
        wu_b[...] = wu_ref[0].astype(BF16)
        wd_b[...] = wd_ref[0].astype(BF16)

    @pl.when(i < nused_ref[0])
    def _():
        xb = xs_ref[...].astype(BF16)
        hid = (_silu(_dot(xb, wg_b[...])) * _dot(xb, wu_b[...])).astype(BF16)
        y_ref[...] = _dot(hid, wd_b[...])

    @pl.when(i >= nused_ref[0])
    def _():
        y_ref[...] = jnp.zeros_like(y_ref)


def _experts(blk_expert, n_used, xs, w_gate, w_up, w_down):
    cap, d = xs.shape
    de = w_gate.shape[2]
    mb = MB_EXPERT
    grid_spec = pltpu.PrefetchScalarGridSpec(
        num_scalar_prefetch=2,
        grid=(cap // mb,),
        in_specs=[
            pl.BlockSpec((mb, d), lambda i, be, nu: (jnp.minimum(i, nu[0] - 1), 0)),
            pl.BlockSpec((1, d, de), lambda i, be, nu: (be[i], 0, 0)),
            pl.BlockSpec((1, d, de), lambda i, be, nu: (be[i], 0, 0)),
            pl.BlockSpec((1, de, d), lambda i, be, nu: (be[i], 0, 0)),
        ],
        out_specs=pl.BlockSpec((mb, d), lambda i, be, nu: (i, 0)),
        scratch_shapes=[pltpu.VMEM((d, de), BF16), pltpu.VMEM((d, de), BF16), pltpu.VMEM((de, d), BF16)],
    )
    return pl.pallas_call(
        _expert_kernel,
        grid_spec=grid_spec,
        out_shape=jax.ShapeDtypeStruct((cap, d), F32),
        compiler_params=pltpu.CompilerParams(dimension_semantics=("arbitrary",),
                                             vmem_limit_bytes=48 * 1024 * 1024),
        name="experts",
    )(blk_expert, n_used, xs, w_gate, w_up, w_down)


def _combine_kernel(dest_ref, route_ref, x1_ref, g2_ref, nw_ref, yb_hbm, o_ref, buf, sem):
    tf = x1_ref.shape[0]

    def row_copy(j, kk, d):
        return pltpu.make_async_copy(yb_hbm.at[pl.ds(d, 1)], buf.at[kk, pl.ds(j, 1)], sem)

    def issue(j, carry):
        for kk in range(TOP_K):
            row_copy(j, kk, dest_ref[0, 0, kk * tf + j]).start()
        return carry

    lax.fori_loop(0, tf, issue, 0, unroll=DMA_UNROLL)

    for kk in range(TOP_K):
        pltpu.make_async_copy(yb_hbm.at[pl.ds(0, tf)], buf.at[kk], sem).wait()

    route = route_ref[...]
    out = route[:, 0:1] * buf[0] + route[:, 1:2] * buf[1]
    ms = jnp.mean(out * out, axis=-1, keepdims=True)
    o_ref[...] = x1_ref[...] + g2_ref[0] * ((out * lax.rsqrt(ms + EPS)) * nw_ref[...])


def _combine(dest, route, x1, mod3, nw, yb, seq_len):
    T, d = x1.shape
    tf = TF_COMBINE
    nt = T // tf
    per_seq = seq_len // tf
    return pl.pallas_call(
        _combine_kernel,
        grid=(nt,),
        in_specs=[
            pl.BlockSpec((1, 1, TOP_K * tf), lambda i: (i, 0, 0), memory_space=pltpu.SMEM),
            pl.BlockSpec((tf, LANES), lambda i: (i, 0)),
            pl.BlockSpec((tf, d), lambda i: (i, 0)),
            pl.BlockSpec((1, 1, d), lambda i: (i // per_seq, 0, 5)),
            _const_spec((1, d)),
            pl.BlockSpec(memory_space=pl.ANY),
        ],
        out_specs=pl.BlockSpec((tf, d), lambda i: (i, 0)),
        out_shape=jax.ShapeDtypeStruct((T, d), F32),
        scratch_shapes=[pltpu.VMEM((TOP_K, tf, d), F32), pltpu.SemaphoreType.DMA(())],
        name="combine",
    )(_slot_blocks(dest, tf), route, x1, mod3, nw, yb)


def _rope_tables(L, n_heads):
    quarter = RET_DK // 4
    freqs = ROPE_BASE ** (-jnp.arange(quarter, dtype=F32) / quarter)
    t = jnp.arange(L)
    ang_r = (t // GRID_W).astype(F32)[:, None] * freqs
    ang_c = (t % GRID_W).astype(F32)[:, None] * freqs
    cos = jnp.concatenate([jnp.cos(ang_r)] * 2 + [jnp.cos(ang_c)] * 2, axis=-1)
    sin = jnp.concatenate([-jnp.sin(ang_r), jnp.sin(ang_r), -jnp.sin(ang_c), jnp.sin(ang_c)], axis=-1)
    return jnp.tile(cos, (1, n_heads)), jnp.tile(sin, (1, n_heads))


def _lane_pad(v, width=LANES):
    return jnp.pad(v, [(0, 0)] * (v.ndim - 1) + [(0, width - v.shape[-1])])


def kernel(x, c, ctx, c_ctx, w_mod, b_mod, norm_pre_mix, norm_post_mix, norm_pre_ffn, norm_post_ffn, w_in, w_out, ret_decay_f, ret_decay_b, ret_gn_w, ssd_conv_w, ssd_conv_b, ssd_dt_bias_f, ssd_dt_bias_b, ssd_a_log_f, ssd_a_log_b, ssd_d, ssd_norm_w, moe_w_rg, moe_b_rg, moe_w_re, moe_b_re, moe_w_gate, moe_w_up, moe_w_down):
    b, L, d = x.shape
    assert w_mod.shape[0] == 1, "single layer: context outputs are never needed"
    rw = RET_HEADS * RET_DK
    nconv = SSD_WIDTH + 2 * SSD_GROUPS * SSD_STATE
    T = b * L

    mod_rows = -(-(b + 1) // SUBLANES) * SUBLANES
    c_all = jnp.zeros((mod_rows, d), F32).at[:b].set(c).at[b].set(c_ctx)
    mod3 = _modulation(c_all, w_mod[0], b_mod[0]).reshape(mod_rows, 1, 6 * d)

    wi = w_in[0]
    o = 0
    wq = wi[:, o:o + rw]; o += rw
    wk = wi[:, o:o + rw]; o += rw
    wv = wi[:, o:o + rw]; o += rw
    wg = wi[:, o:o + rw]; o += rw
    wz = wi[:, o:o + SSD_WIDTH]; o += SSD_WIDTH
    wxbc = wi[:, o:o + nconv].astype(BF16); o += nconv
    wdt = _lane_pad(wi[:, o:o + 2 * SSD_HEADS]).astype(BF16)
    wqk = jnp.concatenate([wq, wk], axis=1).astype(BF16)
    wvgz = jnp.concatenate([wv, wg, wz], axis=1).astype(BF16)
    cos_t, sin_t = _rope_tables(L, RET_HEADS)
    nw1 = norm_pre_mix[0].reshape(1, d)

    q, k, v, g, z, xbc, dt = _inproj(x, mod3, nw1, wqk, wvgz, wxbc, wdt, cos_t, sin_t)
    kc, vc, xbcc, dtc = _inproj_ctx(ctx, mod3, b, nw1, wk.astype(BF16), wv.astype(BF16), wxbc, wdt)

    conv_w8 = jnp.pad(ssd_conv_w[0], ((0, SUBLANES - SSD_CONV), (0, 0)))
    dt_bias = _lane_pad(jnp.concatenate([ssd_dt_bias_f[0], ssd_dt_bias_b[0]])[None, :])
    a_log = _lane_pad(jnp.concatenate([ssd_a_log_f[0], ssd_a_log_b[0]])[None, :])
    d_skip = jnp.repeat(ssd_d[0], SSD_HEADDIM)[None, :]
    ys = _ssd(xbc, z, dt, xbcc, dtc, conv_w8, ssd_conv_b[0][None, :], dt_bias, a_log, d_skip,
              ssd_norm_w[0][None, :])

    yr = _retention(q, k, v, g, kc, vc,
                    jnp.repeat(ret_decay_f[0], RET_DK)[None, :], jnp.repeat(ret_decay_b[0], RET_DK)[None, :],
                    ret_gn_w[0][None, :])

    wo = w_out[0].astype(BF16)
    w_router = _lane_pad(jnp.concatenate(
        [jnp.transpose(moe_w_re[0], (1, 0, 2)).reshape(d, N_EXPERTS), moe_w_rg[0]], axis=1))
    b_router = _lane_pad(jnp.concatenate([moe_b_re[0].reshape(-1), moe_b_rg[0]])[None, :])
    x1, h2, route, counts, slots = _outproj_router(
        yr.reshape(T, rw), ys.reshape(T, SSD_WIDTH), x.reshape(T, d), mod3,
        norm_post_mix[0][None, :], norm_pre_ffn[0][None, :], wo[:rw], wo[rw:], w_router, b_router, L)

    mb = MB_EXPERT
    n_blocks = -(-(T * TOP_K + N_EXPERTS * (mb - 1)) // mb)
    cap = n_blocks * mb
    cnt = counts[0, :N_EXPERTS].astype(jnp.int32)
    padded = (cnt + mb - 1) // mb * mb
    pad_end = jnp.cumsum(padded)
    pad_start = pad_end - padded
    per = slots.shape[1] // (2 * TOP_K)
    experts_kt = jnp.stack([slots[:, kk * per:(kk + 1) * per].reshape(T) for kk in range(TOP_K)])
    ranks_kt = jnp.stack([slots[:, (TOP_K + kk) * per:(TOP_K + kk + 1) * per].reshape(T) for kk in range(TOP_K)])
    dest = pad_start[experts_kt] + ranks_kt
    blk_start = jnp.arange(n_blocks, dtype=jnp.int32) * mb
    blk_expert = jnp.minimum(jnp.sum((pad_end[None, :] <= blk_start[:, None]).astype(jnp.int32), axis=1),
                             N_EXPERTS - 1)
    n_used = (pad_end[-1:] // mb).astype(jnp.int32)

    xs = _dispatch(pad_end.astype(jnp.int32), dest, h2, cap)
    yb = _experts(blk_expert, n_used, xs, moe_w_gate[0], moe_w_up[0], moe_w_down[0])
    out = _combine(dest, route, x1, mod3, norm_post_ffn[0][None, :], yb, L)
    return out.reshape(b, L, d)
```

```python
import functools
import math

import jax
import jax.numpy as jnp
from jax import lax
from jax.experimental import pallas as pl
from jax.experimental.pallas import tpu as pltpu

F32 = jnp.float32
BF16 = jnp.bfloat16
HIGHEST = lax.Precision.HIGHEST

LANES = 128
SUBLANES = 8

EPS = 1e-6
CHUNK = 128
GRID_W = 64
RET_HEADS = 4
RET_DK = 128
ROPE_BASE = 10000.0
SSD_HEADS = 8
SSD_HEADDIM = 64
SSD_GROUPS = 2
SSD_STATE = 128
SSD_WIDTH = SSD_HEADS * SSD_HEADDIM
SSD_CONV = 5
SSD_PAIRS = SSD_WIDTH // LANES
MOE_GROUPS = 4
EXPERTS_PER_GROUP = 8
N_EXPERTS = MOE_GROUPS * EXPERTS_PER_GROUP
TOP_K = 2
CONV_HALO = SUBLANES

TM_PROJ = 512
TM_OUT = 512
TM_ROUTE_SUB = 512
TD_DISPATCH = 512
MB_EXPERT = 256
TF_COMBINE = 256
DMA_UNROLL = 8
SSD_UNROLL = 2
RET_UNROLL = 2
NEG_BIG = -1e30


def _silu(v):
    return v * jax.nn.sigmoid(v)


def _dot(a, b):
    return jnp.dot(a, b, preferred_element_type=F32)


def _dot_tn(a, b):
    return lax.dot_general(a, b, (((0,), (0,)), ((), ())), preferred_element_type=F32)


def _dot_nt(a, b):
    return lax.dot_general(a, b, (((1,), (1,)), ((), ())), preferred_element_type=F32)


def _mod_kernel(c_ref, w_ref, b_ref, o_ref):
    a = _silu(c_ref[...])
    o_ref[...] = jnp.dot(a, w_ref[...], precision=HIGHEST, preferred_element_type=F32) + b_ref[...]


def _modulation(c_all, w_mod, b_mod):
    rows, d = c_all.shape
    n = w_mod.shape[1]
    return pl.pallas_call(
        _mod_kernel,
        grid=(n // d,),
        in_specs=[
            pl.BlockSpec((rows, d), lambda j: (0, 0)),
            pl.BlockSpec((d, d), lambda j: (0, j)),
            pl.BlockSpec((1, d), lambda j: (0, j)),
        ],
        out_specs=pl.BlockSpec((rows, d), lambda j: (0, j)),
        out_shape=jax.ShapeDtypeStruct((rows, n), F32),
        name="modulation",
    )(c_all, w_mod, b_mod.reshape(1, n))


def _norm_mod(x, nw, sc, sh):
    ms = jnp.mean(x * x, axis=-1, keepdims=True)
    return (x * lax.rsqrt(ms + EPS)) * nw * (1.0 + sc) + sh


def _rope(t, cos, sin_signed, first_half):
    width = t.shape[-1]
    quarter = RET_DK // 4
    swapped = jnp.where(first_half, pltpu.roll(t, width - quarter, 1), pltpu.roll(t, quarter, 1))
    return t * cos + swapped * sin_signed


def _inproj_kernel(x_ref, sh_ref, sc_ref, nw_ref, wqk_ref, wvgz_ref, wxbc_ref, wdt_ref, cos_ref, sin_ref,
                   q_ref, k_ref, v_ref, g_ref, z_ref, xbc_ref, dt_ref):
    hb = _norm_mod(x_ref[0], nw_ref[...], sc_ref[0], sh_ref[0]).astype(BF16)
    rw = q_ref.shape[-1]
    qk = _dot(hb, wqk_ref[...])
    cos = cos_ref[...]
    sin = sin_ref[...]
    lane = lax.broadcasted_iota(jnp.int32, cos.shape, 1)
    first_half = (lane % (RET_DK // 2)) < (RET_DK // 4)
    q_ref[0] = _rope(qk[:, :rw], cos, sin, first_half).astype(BF16)
    k_ref[0] = (_rope(qk[:, rw:], cos, sin, first_half) * (RET_DK ** -0.5)).astype(BF16)
    vgz = _dot(hb, wvgz_ref[...])
    v_ref[0] = vgz[:, :rw].astype(BF16)
    g_ref[0] = vgz[:, rw:2 * rw].astype(BF16)
    z_ref[0] = vgz[:, 2 * rw:].astype(BF16)
    xbc_ref[0] = _dot(hb, wxbc_ref[...]).astype(BF16)
    dt_ref[0] = _dot(hb, wdt_ref[...])


def _inproj_ctx_kernel(x_ref, sh_ref, sc_ref, nw_ref, wk_ref, wv_ref, wxbc_ref, wdt_ref,
                       k_ref, v_ref, xbc_ref, dt_ref):
    hb = _norm_mod(x_ref[0], nw_ref[...], sc_ref[0], sh_ref[0]).astype(BF16)
    k_ref[0] = (_dot(hb, wk_ref[...]) * (RET_DK ** -0.5)).astype(BF16)
    v_ref[0] = _dot(hb, wv_ref[...]).astype(BF16)
    xbc_ref[0] = _dot(hb, wxbc_ref[...]).astype(BF16)
    dt_ref[0] = _dot(hb, wdt_ref[...])


def _const_spec(shape):
    nd = len(shape)
    return pl.BlockSpec(shape, lambda *_: (0,) * nd)


def _inproj(x, mod3, nw, wqk, wvgz, wxbc, wdt, cos_t, sin_t):
    b, L, d = x.shape
    tm = min(TM_PROJ, L)
    rw = wqk.shape[1] // 2
    tok = lambda w: pl.BlockSpec((1, tm, w), lambda i, j: (i, j, 0))
    out_bf = lambda w: jax.ShapeDtypeStruct((b, L, w), BF16)
    return pl.pallas_call(
        _inproj_kernel,
        grid=(b, L // tm),
        in_specs=[
            tok(d),
            pl.BlockSpec((1, 1, d), lambda i, j: (i, 0, 0)),
            pl.BlockSpec((1, 1, d), lambda i, j: (i, 0, 1)),
            _const_spec((1, d)),
            _const_spec(wqk.shape), _const_spec(wvgz.shape), _const_spec(wxbc.shape), _const_spec(wdt.shape),
            pl.BlockSpec((tm, rw), lambda i, j: (j, 0)),
            pl.BlockSpec((tm, rw), lambda i, j: (j, 0)),
        ],
        out_specs=[tok(rw), tok(rw), tok(rw), tok(rw), tok(rw), tok(wxbc.shape[1]), tok(LANES)],
        out_shape=[out_bf(rw), out_bf(rw), out_bf(rw), out_bf(rw), out_bf(rw), out_bf(wxbc.shape[1]),
                   jax.ShapeDtypeStruct((b, L, LANES), F32)],
        compiler_params=pltpu.CompilerParams(vmem_limit_bytes=48 * 1024 * 1024),
        name="inproj",
    )(x, mod3, mod3, nw, wqk, wvgz, wxbc, wdt, cos_t, sin_t)


def _inproj_ctx(ctx, mod3, ctx_row, nw, wk, wv, wxbc, wdt):
    b, L, d = ctx.shape
    tm = min(TM_PROJ, L)
    rw = wk.shape[1]
    tok = lambda w: pl.BlockSpec((1, tm, w), lambda i, j: (i, j, 0))
    out_bf = lambda w: jax.ShapeDtypeStruct((b, L, w), BF16)
    return pl.pallas_call(
        _inproj_ctx_kernel,
        grid=(b, L // tm),
        in_specs=[
            tok(d),
            pl.BlockSpec((1, 1, d), lambda i, j: (ctx_row, 0, 0)),
            pl.BlockSpec((1, 1, d), lambda i, j: (ctx_row, 0, 1)),
            _const_spec((1, d)),
            _const_spec(wk.shape), _const_spec(wv.shape), _const_spec(wxbc.shape), _const_spec(wdt.shape),
        ],
        out_specs=[tok(rw), tok(rw), tok(wxbc.shape[1]), tok(LANES)],
        out_shape=[out_bf(rw), out_bf(rw), out_bf(wxbc.shape[1]), jax.ShapeDtypeStruct((b, L, LANES), F32)],
        compiler_params=pltpu.CompilerParams(vmem_limit_bytes=48 * 1024 * 1024),
        name="inproj_ctx",
    )(ctx, mod3, mod3, nw, wk, wv, wxbc, wdt)


def _ssd_kernel(xbc_ref, z_ref, dt_ref, xbcc_ref, dtc_ref, cw_ref, cb_ref, dtb_ref, alog_ref, dsk_ref, nw_ref,
                y_ref,
                xpad, xpadc, u, uc, dtv, dav, dtcv, dacv, sf_scr):
    L = xbc_ref.shape[1]
    Lc = xbcc_ref.shape[1]
    nch = L // CHUNK
    nchc = Lc // CHUNK
    win = CHUNK + 2 * CONV_HALO
    nconv = xbc_ref.shape[2]
    nh = SSD_HEADS

    def conv_pass(src_ref, pad_ref, dst_ref, n_chunks, length):
        zeros = jnp.zeros((CONV_HALO, nconv), F32)
        pad_ref[0:CONV_HALO, :] = zeros
        pad_ref[CONV_HALO + length:2 * CONV_HALO + length, :] = zeros
        pad_ref[CONV_HALO:CONV_HALO + length, :] = src_ref[0].astype(F32)

        def chunk(c, carry):
            base = pl.multiple_of(c * CHUNK, CHUNK)
            for cb_i in range(nconv // LANES):
                cols = slice(cb_i * LANES, (cb_i + 1) * LANES)
                w = pad_ref[pl.ds(base, win), cols]
                acc = cb_ref[:, cols] + w[CONV_HALO:CONV_HALO + CHUNK] * cw_ref[SSD_CONV // 2:SSD_CONV // 2 + 1, cols]
                for j in range(SSD_CONV):
                    if j == SSD_CONV // 2:
                        continue
                    shifted = pltpu.roll(w, (SSD_CONV // 2 - j) % win, 0)
                    acc = acc + shifted[CONV_HALO:CONV_HALO + CHUNK] * cw_ref[j:j + 1, cols]
                dst_ref[pl.ds(base, CHUNK), cols] = _silu(acc).astype(BF16)
            return carry

        lax.fori_loop(0, n_chunks, chunk, 0)

    conv_pass(xbcc_ref, xpadc, uc, nchc, Lc)
    conv_pass(xbc_ref, xpad, u, nch, L)

    a_neg = -jnp.exp(alog_ref[...])
    dtv[...] = jax.nn.softplus(dt_ref[0] + dtb_ref[...])
    dav[...] = dtv[...] * a_neg
    dtcv[...] = jax.nn.softplus(dtc_ref[0] + dtb_ref[...])
    dacv[...] = dtcv[...] * a_neg

    row_i = lax.broadcasted_iota(jnp.int32, (CHUNK, CHUNK), 0)
    col_i = lax.broadcasted_iota(jnp.int32, (CHUNK, CHUNK), 1)
    causal = col_i <= row_i
    tri = causal.astype(BF16)
    tri_t = (row_i <= col_i).astype(BF16)
    lo_half = col_i < SSD_HEADDIM
    fwd_lane = col_i < nh
    head_of = lax.broadcasted_iota(jnp.int32, (CHUNK, SSD_WIDTH), 1) // SSD_HEADDIM
    src_col = lax.broadcasted_iota(jnp.int32, (CHUNK, SSD_WIDTH), 0)
    exp_f = (head_of == src_col).astype(BF16)
    exp_b = (head_of == src_col - nh).astype(BF16)
    exp_fb = jnp.concatenate([exp_f, exp_b], axis=1)

    def split3(v):
        hi = v.astype(BF16)
        r1 = v - hi.astype(F32)
        mid = r1.astype(BF16)
        return hi, mid, (r1 - mid.astype(F32)).astype(BF16)

    def times_onehot(v, m):
        hi, mid, lo = split3(v)
        return _dot(hi, m) + _dot(mid, m) + _dot(lo, m)

    def onehot_times(m, v):
        hi, mid, lo = split3(v)
        return _dot(m, hi) + _dot(m, mid) + _dot(m, lo)

    def colb(mat, r):
        return jnp.broadcast_to(mat[:, r:r + 1], (CHUNK, CHUNK))

    def pair_sel(a, b_):
        return jnp.where(lo_half, a, b_)

    def chunk_scalars(dt_s, da_s, base):
        dt = dt_s[pl.ds(base, CHUNK), :]
        da = da_s[pl.ds(base, CHUNK), :]
        acol = onehot_times(tri, da)
        return dt, da, acol, acol - da

    def state_update(u_ref, base, dt, acol, ecol, s_old, backward):
        last = acol[CHUNK - 1:CHUNK, :]
        wgt = jnp.exp(ecol) * dt if backward else jnp.exp(last - acol) * dt
        small = jnp.concatenate([wgt, jnp.broadcast_to(jnp.exp(last), (SUBLANES, LANES))], axis=0)
        wide = times_onehot(small, exp_b if backward else exp_f)
        xw = (u_ref[pl.ds(base, CHUNK), 0:SSD_WIDTH].astype(F32) * wide[:CHUNK]).astype(BF16)
        dec = wide[CHUNK:CHUNK + 1]
        new = []
        for g in range(SSD_GROUPS):
            gl = slice(g * 2 * LANES, (g + 1) * 2 * LANES)
            bm = u_ref[pl.ds(base, CHUNK), SSD_WIDTH + g * SSD_STATE:SSD_WIDTH + (g + 1) * SSD_STATE]
            new.append(dec[:, gl] * s_old[g] + _dot_tn(bm, xw[:, gl]))
        return new

    def ctx_update(c, s, backward):
        dt, _, acol, ecol = chunk_scalars(dtcv, dacv, c * CHUNK)
        return state_update(uc, c * CHUNK, dt, acol, ecol, s, backward)

    zero_state = [jnp.zeros((SSD_STATE, 2 * LANES), F32) for _ in range(SSD_GROUPS)]

    s = zero_state
    for c in range(nchc):
        s = ctx_update(c, s, False)
    s_f0 = s
    s = zero_state
    for c in reversed(range(nchc)):
        s = ctx_update(c, s, True)
    s_b0 = s

    def fwd(c, s_old):
        base = pl.multiple_of(c * CHUNK, CHUNK)
        for g in range(SSD_GROUPS):
            sf_scr[c, g] = s_old[g]
        dt, _, acol, ecol = chunk_scalars(dtv, dav, base)
        return tuple(state_update(u, base, dt, acol, ecol, s_old, False))

    lax.fori_loop(0, nch, fwd, tuple(s_f0), unroll=SSD_UNROLL)

    def bwd(i, s_b):
        c = nch - 1 - i
        base = pl.multiple_of(c * CHUNK, CHUNK)
        dt, da, acol, ecol = chunk_scalars(dtv, dav, base)
        da_t = da.T
        dt_t = dt.T
        arow = times_onehot(da_t, tri_t)
        erow = arow - da_t
        last = acol[CHUNK - 1:CHUNK, :]
        scale = times_onehot(jnp.where(fwd_lane, jnp.exp(acol), jnp.exp(last - ecol)), exp_fb)
        ys = []
        for g in range(SSD_GROUPS):
            bm = u[pl.ds(base, CHUNK), SSD_WIDTH + g * SSD_STATE:SSD_WIDTH + (g + 1) * SSD_STATE]
            cm = u[pl.ds(base, CHUNK), SSD_WIDTH + (SSD_GROUPS + g) * SSD_STATE:SSD_WIDTH + (SSD_GROUPS + g + 1) * SSD_STATE]
            cbm = _dot_nt(cm, bm)
            cs_f = _dot(cm, sf_scr[c, g].astype(BF16))
            cs_b = _dot(cm, s_b[g].astype(BF16))
            for pp in range(SSD_PAIRS // SSD_GROUPS):
                p = g * (SSD_PAIRS // SSD_GROUPS) + pp
                xs_b = u[pl.ds(base, CHUNK), p * LANES:(p + 1) * LANES]
                y_h = []
                for hh in range(2):
                    r = 2 * p + hh
                    arg = jnp.where(causal, colb(acol, r) - arow[r:r + 1, :],
                                    erow[nh + r:nh + r + 1, :] - colb(ecol, nh + r))
                    coef = jnp.where(causal, dt_t[r:r + 1, :], dt_t[nh + r:nh + r + 1, :])
                    gm = (cbm * (jnp.exp(arg) * coef)).astype(BF16)
                    y_h.append(_dot(gm, xs_b))
                sl = slice(pp * LANES, (pp + 1) * LANES)
                wl = slice(p * LANES, (p + 1) * LANES)
                wlb = slice(SSD_WIDTH + p * LANES, SSD_WIDTH + (p + 1) * LANES)
                ys.append(pair_sel(y_h[0], y_h[1]) + cs_f[:, sl] * scale[:, wl] + cs_b[:, sl] * scale[:, wlb]
                          + dsk_ref[:, wl] * xs_b.astype(F32))
        y = jnp.concatenate(ys, axis=1)
        y = y * _silu(z_ref[0, pl.ds(base, CHUNK), :].astype(F32))
        ms = jnp.mean(y * y, axis=-1, keepdims=True)
        y_ref[0, pl.ds(base, CHUNK), :] = ((y * lax.rsqrt(ms + EPS)) * nw_ref[...]).astype(BF16)
        return tuple(state_update(u, base, dt, acol, ecol, s_b, True))

    lax.fori_loop(0, nch, bwd, tuple(s_b0), unroll=SSD_UNROLL)


def _ssd(xbc, z, dt, xbcc, dtc, conv_w8, conv_b, dt_bias, a_log, d_skip, norm_w):
    b, L, nconv = xbc.shape
    Lc = xbcc.shape[1]
    nch = L // CHUNK
    per_b = lambda n, w: pl.BlockSpec((1, n, w), lambda i: (i, 0, 0))
    return pl.pallas_call(
        _ssd_kernel,
        grid=(b,),
        in_specs=[
            per_b(L, nconv), per_b(L, SSD_WIDTH), per_b(L, LANES), per_b(Lc, nconv), per_b(Lc, LANES),
            _const_spec(conv_w8.shape), _const_spec(conv_b.shape), _const_spec(dt_bias.shape),
            _const_spec(a_log.shape), _const_spec(d_skip.shape), _const_spec(norm_w.shape),
        ],
        out_specs=per_b(L, SSD_WIDTH),
        out_shape=jax.ShapeDtypeStruct((b, L, SSD_WIDTH), BF16),
        scratch_shapes=[
            pltpu.VMEM((L + 2 * CONV_HALO, nconv), F32),
            pltpu.VMEM((Lc + 2 * CONV_HALO, nconv), F32),
            pltpu.VMEM((L, nconv), BF16),
            pltpu.VMEM((Lc, nconv), BF16),
            pltpu.VMEM((L, LANES), F32), pltpu.VMEM((L, LANES), F32),
            pltpu.VMEM((Lc, LANES), F32), pltpu.VMEM((Lc, LANES), F32),
            pltpu.VMEM((nch, SSD_GROUPS, SSD_STATE, 2 * LANES), F32),
        ],
        compiler_params=pltpu.CompilerParams(vmem_limit_bytes=56 * 1024 * 1024),
        name="ssd",
    )(xbc, z, dt, xbcc, dtc, conv_w8, conv_b, dt_bias, a_log, d_skip, norm_w)


def _ret_kernel(q_ref, k_ref, v_ref, g_ref, kc_ref, vc_ref, df_ref, db_ref, gn_ref, y_ref, sf_scr):
    L = q_ref.shape[1]
    Lc = kc_ref.shape[1]
    nch = L // CHUNK
    dk = RET_DK
    row_i = lax.broadcasted_iota(jnp.int32, (CHUNK, dk), 0).astype(F32)
    col_i = lax.broadcasted_iota(jnp.int32, (CHUNK, dk), 1).astype(F32)
    rel = row_i - col_i
    crow = lax.broadcasted_iota(jnp.int32, (Lc, dk), 0).astype(F32)

    heads = []
    s_f0 = []
    s_b0 = []
    for h in range(RET_HEADS):
        cols = slice(h * dk, (h + 1) * dk)
        lg_f = -jnp.exp(df_ref[:, cols])
        lg_b = -jnp.exp(db_ref[:, cols])
        heads.append(dict(
            cols=cols,
            dmat=jnp.where(rel >= 0, jnp.exp(jnp.maximum(rel, 0.0) * lg_f), jnp.exp(jnp.maximum(-rel, 0.0) * lg_b)),
            dq_f=jnp.exp((row_i + 1.0) * lg_f),
            dq_b=jnp.exp((CHUNK - row_i) * lg_b),
            dk_f=jnp.exp((CHUNK - 1.0 - row_i) * lg_f),
            dk_b=jnp.exp(row_i * lg_b),
            dc_f=jnp.exp(CHUNK * lg_f),
            dc_b=jnp.exp(CHUNK * lg_b),
        ))
        kc = kc_ref[0, :, cols].astype(F32)
        vc = vc_ref[0, :, cols]
        s_f0.append(_dot_tn((kc * jnp.exp((Lc - 1.0 - crow) * lg_f)).astype(BF16), vc))
        s_b0.append(_dot_tn((kc * jnp.exp(crow * lg_b)).astype(BF16), vc))

    def fwd(c, s_f):
        base = pl.multiple_of(c * CHUNK, CHUNK)
        new = []
        for h, hd in enumerate(heads):
            sf_scr[c, h] = s_f[h]
            kk = k_ref[0, pl.ds(base, CHUNK), hd["cols"]].astype(F32)
            vv = v_ref[0, pl.ds(base, CHUNK), hd["cols"]]
            new.append(hd["dc_f"] * s_f[h] + _dot_tn((kk * hd["dk_f"]).astype(BF16), vv))
        return tuple(new)

    lax.fori_loop(0, nch, fwd, tuple(s_f0), unroll=RET_UNROLL)

    def bwd(i, s_bs):
        c = nch - 1 - i
        base = pl.multiple_of(c * CHUNK, CHUNK)
        new = []
        for h, hd in enumerate(heads):
            qq = q_ref[0, pl.ds(base, CHUNK), hd["cols"]]
            kk = k_ref[0, pl.ds(base, CHUNK), hd["cols"]]
            vv = v_ref[0, pl.ds(base, CHUNK), hd["cols"]]
            s_b = s_bs[h]
            scores = (_dot_nt(qq, kk) * hd["dmat"]).astype(BF16)
            y = (_dot(scores, vv)
                 + _dot(qq, sf_scr[c, h].astype(BF16)) * hd["dq_f"]
                 + _dot(qq, s_b.astype(BF16)) * hd["dq_b"])
            mu = jnp.mean(y, axis=-1, keepdims=True)
            yc = y - mu
            var = jnp.mean(yc * yc, axis=-1, keepdims=True)
            yn = (yc * lax.rsqrt(var + EPS)) * gn_ref[:, hd["cols"]]
            gate = _silu(g_ref[0, pl.ds(base, CHUNK), hd["cols"]].astype(F32))
            y_ref[0, pl.ds(base, CHUNK), hd["cols"]] = (yn * gate).astype(BF16)
            new.append(hd["dc_b"] * s_b + _dot_tn((kk.astype(F32) * hd["dk_b"]).astype(BF16), vv))
        return tuple(new)

    lax.fori_loop(0, nch, bwd, tuple(s_b0), unroll=RET_UNROLL)


def _retention(q, k, v, g, kc, vc, decay_f, decay_b, gn_w):
    b, L, w = q.shape
    Lc = kc.shape[1]
    nch = L // CHUNK
    per_b = lambda n: pl.BlockSpec((1, n, w), lambda i: (i, 0, 0))
    return pl.pallas_call(
        _ret_kernel,
        grid=(b,),
        in_specs=[per_b(L), per_b(L), per_b(L), per_b(L), per_b(Lc), per_b(Lc),
                  _const_spec((1, w)), _const_spec((1, w)), _const_spec((1, w))],
        out_specs=per_b(L),
        out_shape=jax.ShapeDtypeStruct((b, L, w), BF16),
        scratch_shapes=[
            pltpu.VMEM((nch, RET_HEADS, RET_DK, RET_DK), F32),
        ],
        compiler_params=pltpu.CompilerParams(vmem_limit_bytes=48 * 1024 * 1024),
        name="retention",
    )(q, k, v, g, kc, vc, decay_f, decay_b, gn_w)


def _outproj_router_kernel(yr_ref, ys_ref, x_ref, g1_ref, sh2_ref, sc2_ref, npost_ref, npre_ref,
                           wor_ref, wos_ref, wr_ref, br_ref, tri_ref,
                           x1_ref, h2_ref, route_ref, cnt_ref, slots_ref,
                           wcat, carry):
    i = pl.program_id(0)

    @pl.when(i == 0)
    def _():
        wr = wr_ref[...]
        hi = wr.astype(BF16)
        wcat[:, :LANES] = hi
        wcat[:, LANES:] = (wr - hi.astype(F32)).astype(BF16)
        carry[...] = jnp.zeros_like(carry)

    tm = tri_ref.shape[0]
    running = carry[...]
    for sub in range(x_ref.shape[0] // tm):
        running = _route_subtile(sub, tm, running, yr_ref, ys_ref, x_ref, g1_ref, sh2_ref, sc2_ref, npost_ref,
                                 npre_ref, wor_ref, wos_ref, br_ref, tri_ref, x1_ref, h2_ref, route_ref,
                                 slots_ref, wcat)
    carry[...] = running
    cnt_ref[...] = running


def _route_subtile(sub, tm, running, yr_ref, ys_ref, x_ref, g1_ref, sh2_ref, sc2_ref, npost_ref, npre_ref,
                   wor_ref, wos_ref, br_ref, tri_ref, x1_ref, h2_ref, route_ref, slots_ref, wcat):
    rows = slice(sub * tm, (sub + 1) * tm)
    y = _dot(yr_ref[rows, :], wor_ref[...]) + _dot(ys_ref[rows, :], wos_ref[...])
    ms = jnp.mean(y * y, axis=-1, keepdims=True)
    x1 = x_ref[rows, :] + g1_ref[0] * ((y * lax.rsqrt(ms + EPS)) * npost_ref[...])
    x1_ref[rows, :] = x1
    h2 = _norm_mod(x1, npre_ref[...], sc2_ref[0], sh2_ref[0])
    h2_ref[rows, :] = h2

    h_hi = h2.astype(BF16)
    h_lo = (h2 - h_hi.astype(F32)).astype(BF16)
    both = _dot(h_hi, wcat[...])
    lg = both[:, :LANES] + both[:, LANES:] + _dot(h_lo, wcat[:, :LANES]) + br_ref[...]

    lane = lax.broadcasted_iota(jnp.int32, (tm, LANES), 1)
    lane_f = lane.astype(F32)
    is_grp = (lane >= N_EXPERTS) & (lane < N_EXPERTS + MOE_GROUPS)
    gl = jnp.where(is_grp, lg, NEG_BIG)
    mg = jnp.max(gl, axis=-1, keepdims=True)
    grp_lane = jnp.min(jnp.where(gl == mg, lane_f, 1e9), axis=-1, keepdims=True)
    p_g = 1.0 / jnp.sum(jnp.where(is_grp, jnp.exp(gl - mg), 0.0), axis=-1, keepdims=True)
    first = (grp_lane - N_EXPERTS) * EXPERTS_PER_GROUP
    in_grp = (lane_f >= first) & (lane_f < first + EXPERTS_PER_GROUP)
    el = jnp.where(in_grp, lg, NEG_BIG)
    t1 = jnp.max(el, axis=-1, keepdims=True)
    i1 = jnp.min(jnp.where(el == t1, lane_f, 1e9), axis=-1, keepdims=True)
    el2 = jnp.where(lane_f == i1, NEG_BIG, el)
    t2 = jnp.max(el2, axis=-1, keepdims=True)
    i2 = jnp.min(jnp.where(el2 == t2, lane_f, 1e9), axis=-1, keepdims=True)
    s = jnp.exp(t2 - t1)
    w1 = p_g / (1.0 + s)
    w2 = p_g * s / (1.0 + s)

    oh1 = (lane_f == i1)
    oh2 = (lane_f == i2)
    oh = (oh1 | oh2).astype(BF16)
    before = _dot(tri_ref[...], oh) + running
    rank1 = jnp.sum(jnp.where(oh1, before, 0.0), axis=-1, keepdims=True)
    rank2 = jnp.sum(jnp.where(oh2, before, 0.0), axis=-1, keepdims=True)

    route_ref[rows, :] = jnp.where(lane == 0, w1, jnp.where(lane == 1, w2, 0.0))

    row = lax.broadcasted_iota(jnp.int32, (tm, LANES), 0)
    on_diag = (row % LANES) == lane
    per = tm // LANES
    n_sub = x_ref.shape[0] // tm
    for qi, col in enumerate((i1, i2, rank1, rank2)):
        picked = jnp.where(on_diag, col, 0.0)
        dense = jnp.sum(picked.reshape(per, LANES, LANES), axis=1).astype(jnp.int32)
        slots_ref[0, (qi * n_sub + sub) * per:(qi * n_sub + sub + 1) * per, :] = dense
    return running + jnp.sum(oh.astype(F32), axis=0, keepdims=True)


def _outproj_router(yr, ys, x2, mod3, npost, npre, wo_r, wo_s, w_router, b_router, seq_len):
    T, d = x2.shape
    tm = TM_OUT
    per_seq = seq_len // tm
    rw = yr.shape[1]
    sub = TM_ROUTE_SUB
    tri = (jnp.arange(sub)[:, None] > jnp.arange(sub)[None, :]).astype(BF16)
    tok = lambda w: pl.BlockSpec((tm, w), lambda i: (i, 0))
    modv = lambda k: pl.BlockSpec((1, 1, d), lambda i: (i // per_seq, 0, k))
    return pl.pallas_call(
        _outproj_router_kernel,
        grid=(T // tm,),
        in_specs=[
            tok(rw), tok(rw), tok(d), modv(2), modv(3), modv(4),
            _const_spec((1, d)), _const_spec((1, d)),
            _const_spec(wo_r.shape), _const_spec(wo_s.shape), _const_spec(w_router.shape), _const_spec((1, LANES)),
            _const_spec((sub, sub)),
        ],
        out_specs=[tok(d), tok(d), tok(LANES), _const_spec((1, LANES)),
                   pl.BlockSpec((1, 2 * TOP_K * (tm // LANES), LANES), lambda i: (i, 0, 0))],
        out_shape=[jax.ShapeDtypeStruct((T, d), F32), jax.ShapeDtypeStruct((T, d), F32),
                   jax.ShapeDtypeStruct((T, LANES), F32), jax.ShapeDtypeStruct((1, LANES), F32),
                   jax.ShapeDtypeStruct((T // tm, 2 * TOP_K * (tm // LANES), LANES), jnp.int32)],
        scratch_shapes=[pltpu.VMEM((d, 2 * LANES), BF16), pltpu.VMEM((1, LANES), F32)],
        compiler_params=pltpu.CompilerParams(dimension_semantics=("arbitrary",),
                                             vmem_limit_bytes=48 * 1024 * 1024),
        name="outproj_router",
    )(yr, ys, x2, mod3, mod3, mod3, npost, npre, wo_r, wo_s, w_router, b_router, tri)


def _dispatch_kernel(pad_end_ref, dest_ref, h_ref, xs_hbm, zbuf, sem, zsem):
    i = pl.program_id(0)
    td = h_ref.shape[0]
    mb = zbuf.shape[0]

    @pl.when(i == 0)
    def _():
        zbuf[...] = jnp.zeros_like(zbuf)

        def fill(e):
            return pltpu.make_async_copy(zbuf, xs_hbm.at[pl.ds(pl.multiple_of(pad_end_ref[e] - mb, mb), mb)], zsem)

        def has_rows(e):
            return pad_end_ref[e] > jnp.where(e == 0, 0, pad_end_ref[jnp.maximum(e - 1, 0)])

        def start(e, carry):
            @pl.when(has_rows(e))
            def _():
                fill(e).start()
            return carry

        def wait(e, carry):
            @pl.when(has_rows(e))
            def _():
                fill(e).wait()
            return carry

        def tail_fill(blk):
            return pltpu.make_async_copy(zbuf, xs_hbm.at[pl.ds(pl.multiple_of(blk * mb, mb), mb)], zsem)

        def tail_start(blk, carry):
            @pl.when(blk * mb >= pad_end_ref[N_EXPERTS - 1])
            def _():
                tail_fill(blk).start()
            return carry

        def tail_wait(blk, carry):
            @pl.when(blk * mb >= pad_end_ref[N_EXPERTS - 1])
            def _():
                tail_fill(blk).wait()
            return carry

        n_blocks = xs_hbm.shape[0] // mb
        lax.fori_loop(0, N_EXPERTS, start, 0)
        lax.fori_loop(0, n_blocks, tail_start, 0)
        lax.fori_loop(0, N_EXPERTS, wait, 0)
        lax.fori_loop(0, n_blocks, tail_wait, 0)

    def issue(j, carry):
        for kk in range(TOP_K):
            d = dest_ref[0, 0, kk * td + j]
            pltpu.make_async_copy(h_ref.at[pl.ds(j, 1)], xs_hbm.at[pl.ds(d, 1)], sem).start()
        return carry

    lax.fori_loop(0, td, issue, 0, unroll=DMA_UNROLL)
    for kk in range(TOP_K):
        pltpu.make_async_copy(h_ref, xs_hbm.at[pl.ds(0, td)], sem).wait()


def _slot_blocks(dest, tile):
    nt = dest.shape[1] // tile
    return dest.reshape(TOP_K, nt, 1, tile).transpose(1, 2, 0, 3).reshape(nt, 1, TOP_K * tile)


def _dispatch(pad_end, dest, h2, cap):
    T, d = h2.shape
    td = TD_DISPATCH
    nt = T // td
    grid_spec = pltpu.PrefetchScalarGridSpec(
        num_scalar_prefetch=1,
        grid=(nt,),
        in_specs=[
            pl.BlockSpec((1, 1, TOP_K * td), lambda i, pe: (i, 0, 0), memory_space=pltpu.SMEM),
            pl.BlockSpec((td, d), lambda i, pe: (i, 0)),
        ],
        out_specs=pl.BlockSpec(memory_space=pl.ANY),
        scratch_shapes=[pltpu.VMEM((MB_EXPERT, d), F32), pltpu.SemaphoreType.DMA(()), pltpu.SemaphoreType.DMA(())],
    )
    return pl.pallas_call(
        _dispatch_kernel,
        grid_spec=grid_spec,
        out_shape=jax.ShapeDtypeStruct((cap, d), F32),
        compiler_params=pltpu.CompilerParams(dimension_semantics=("arbitrary",)),
        name="dispatch",
    )(pad_end, _slot_blocks(dest, td), h2)


def _expert_kernel(be_ref, nused_ref, xs_ref, wg_ref, wu_ref, wd_ref, y_ref, wg_b, wu_b, wd_b):
    i = pl.program_id(0)
    prev = be_ref[jnp.maximum(i - 1, 0)]

    @pl.when((i == 0) | (be_ref[i] != prev))
    def _():
        wg_b[...] = wg_ref[0].astype(BF16)
        wu_b[...] = wu_ref[0].astype(BF16)
        wd_b[...] = wd_ref[0].astype(BF16)

    @pl.when(i < nused_ref[0])
    def _():
        xb = xs_ref[...].astype(BF16)
        hid = (_silu(_dot(xb, wg_b[...])) * _dot(xb, wu_b[...])).astype(BF16)
        y_ref[...] = _dot(hid, wd_b[...])

    @pl.when(i >= nused_ref[0])
    def _():
        y_ref[...] = jnp.zeros_like(y_ref)


def _experts(blk_expert, n_used, xs, w_gate, w_up, w_down):
    cap, d = xs.shape
    de = w_gate.shape[2]
    mb = MB_EXPERT
    grid_spec = pltpu.PrefetchScalarGridSpec(
        num_scalar_prefetch=2,
        grid=(cap // mb,),
        in_specs=[
            pl.BlockSpec((mb, d), lambda i, be, nu: (jnp.minimum(i, nu[0] - 1), 0)),
            pl.BlockSpec((1, d, de), lambda i, be, nu: (be[i], 0, 0)),
            pl.BlockSpec((1, d, de), lambda i, be, nu: (be[i], 0, 0)),
            pl.BlockSpec((1, de, d), lambda i, be, nu: (be[i], 0, 0)),
        ],
        out_specs=pl.BlockSpec((mb, d), lambda i, be, nu: (i, 0)),
        scratch_shapes=[pltpu.VMEM((d, de), BF16), pltpu.VMEM((d, de), BF16), pltpu.VMEM((de, d), BF16)],
    )
    return pl.pallas_call(
        _expert_kernel,
        grid_spec=grid_spec,
        out_shape=jax.ShapeDtypeStruct((cap, d), F32),
        compiler_params=pltpu.CompilerParams(dimension_semantics=("arbitrary",),
                                             vmem_limit_bytes=48 * 1024 * 1024),
        name="experts",
    )(blk_expert, n_used, xs, w_gate, w_up, w_down)


def _combine_kernel(dest_ref, route_ref, x1_ref, g2_ref, nw_ref, yb_hbm, o_ref, buf, sem):
    tf = x1_ref.shape[0]

    def row_copy(j, kk, d):
        return pltpu.make_async_copy(yb_hbm.at[pl.ds(d, 1)], buf.at[kk, pl.ds(j, 1)], sem)

    def issue(j, carry):
        for kk in range(TOP_K):
            row_copy(j, kk, dest_ref[0, 0, kk * tf + j]).start()
        return carry

    lax.fori_loop(0, tf, issue, 0, unroll=DMA_UNROLL)

    for kk in range(TOP_K):
        pltpu.make_async_copy(yb_hbm.at[pl.ds(0, tf)], buf.at[kk], sem).wait()

    route = route_ref[...]
    out = route[:, 0:1] * buf[0] + route[:, 1:2] * buf[1]
    ms = jnp.mean(out * out, axis=-1, keepdims=True)
    o_ref[...] = x1_ref[...] + g2_ref[0] * ((out * lax.rsqrt(ms + EPS)) * nw_ref[...])


def _combine(dest, route, x1, mod3, nw, yb, seq_len):
    T, d = x1.shape
    tf = TF_COMBINE
    nt = T // tf
    per_seq = seq_len // tf
    return pl.pallas_call(
        _combine_kernel,
        grid=(nt,),
        in_specs=[
            pl.BlockSpec((1, 1, TOP_K * tf), lambda i: (i, 0, 0), memory_space=pltpu.SMEM),
            pl.BlockSpec((tf, LANES), lambda i: (i, 0)),
            pl.BlockSpec((tf, d), lambda i: (i, 0)),
            pl.BlockSpec((1, 1, d), lambda i: (i // per_seq, 0, 5)),
            _const_spec((1, d)),
            pl.BlockSpec(memory_space=pl.ANY),
        ],
        out_specs=pl.BlockSpec((tf, d), lambda i: (i, 0)),
        out_shape=jax.ShapeDtypeStruct((T, d), F32),
        scratch_shapes=[pltpu.VMEM((TOP_K, tf, d), F32), pltpu.SemaphoreType.DMA(())],
        name="combine",
    )(_slot_blocks(dest, tf), route, x1, mod3, nw, yb)


def _rope_tables(L, n_heads):
    quarter = RET_DK // 4
    freqs = ROPE_BASE ** (-jnp.arange(quarter, dtype=F32) / quarter)
    t = jnp.arange(L)
    ang_r = (t // GRID_W).astype(F32)[:, None] * freqs
    ang_c = (t % GRID_W).astype(F32)[:, None] * freqs
    cos = jnp.concatenate([jnp.cos(ang_r)] * 2 + [jnp.cos(ang_c)] * 2, axis=-1)
    sin = jnp.concatenate([-jnp.sin(ang_r), jnp.sin(ang_r), -jnp.sin(ang_c), jnp.sin(ang_c)], axis=-1)
    return jnp.tile(cos, (1, n_heads)), jnp.tile(sin, (1, n_heads))


def _lane_pad(v, width=LANES):
    return jnp.pad(v, [(0, 0)] * (v.ndim - 1) + [(0, width - v.shape[-1])])


def kernel(x, c, ctx, c_ctx, w_mod, b_mod, norm_pre_mix, norm_post_mix, norm_pre_ffn, norm_post_ffn, w_in, w_out, ret_decay_f, ret_decay_b, ret_gn_w, ssd_conv_w, ssd_conv_b, ssd_dt_bias_f, ssd_dt_bias_b, ssd_a_log_f, ssd_a_log_b, ssd_d, ssd_norm_w, moe_w_rg, moe_b_rg, moe_w_re, moe_b_re, moe_w_gate, moe_w_up, moe_w_down):
    b, L, d = x.shape
    assert w_mod.shape[0] == 1, "single layer: context outputs are never needed"
    rw = RET_HEADS * RET_DK
    nconv = SSD_WIDTH + 2 * SSD_GROUPS * SSD_STATE
    T = b * L

    mod_rows = -(-(b + 1) // SUBLANES) * SUBLANES
    c_all = jnp.zeros((mod_rows, d), F32).at[:b].set(c).at[b].set(c_ctx)
    mod3 = _modulation(c_all, w_mod[0], b_mod[0]).reshape(mod_rows, 1, 6 * d)

    wi = w_in[0]
    o = 0
    wq = wi[:, o:o + rw]; o += rw
    wk = wi[:, o:o + rw]; o += rw
    wv = wi[:, o:o + rw]; o += rw
    wg = wi[:, o:o + rw]; o += rw
    wz = wi[:, o:o + SSD_WIDTH]; o += SSD_WIDTH
    wxbc = wi[:, o:o + nconv].astype(BF16); o += nconv
    wdt = _lane_pad(wi[:, o:o + 2 * SSD_HEADS]).astype(BF16)
    wqk = jnp.concatenate([wq, wk], axis=1).astype(BF16)
    wvgz = jnp.concatenate([wv, wg, wz], axis=1).astype(BF16)
    cos_t, sin_t = _rope_tables(L, RET_HEADS)
    nw1 = norm_pre_mix[0].reshape(1, d)

    q, k, v, g, z, xbc, dt = _inproj(x, mod3, nw1, wqk, wvgz, wxbc, wdt, cos_t, sin_t)
    kc, vc, xbcc, dtc = _inproj_ctx(ctx, mod3, b, nw1, wk.astype(BF16), wv.astype(BF16), wxbc, wdt)

    conv_w8 = jnp.pad(ssd_conv_w[0], ((0, SUBLANES - SSD_CONV), (0, 0)))
    dt_bias = _lane_pad(jnp.concatenate([ssd_dt_bias_f[0], ssd_dt_bias_b[0]])[None, :])
    a_log = _lane_pad(jnp.concatenate([ssd_a_log_f[0], ssd_a_log_b[0]])[None, :])
    d_skip = jnp.repeat(ssd_d[0], SSD_HEADDIM)[None, :]
    ys = _ssd(xbc, z, dt, xbcc, dtc, conv_w8, ssd_conv_b[0][None, :], dt_bias, a_log, d_skip,
              ssd_norm_w[0][None, :])

    yr = _retention(q, k, v, g, kc, vc,
                    jnp.repeat(ret_decay_f[0], RET_DK)[None, :], jnp.repeat(ret_decay_b[0], RET_DK)[None, :],
                    ret_gn_w[0][None, :])

    wo = w_out[0].astype(BF16)
    w_router = _lane_pad(jnp.concatenate(
        [jnp.transpose(moe_w_re[0], (1, 0, 2)).reshape(d, N_EXPERTS), moe_w_rg[0]], axis=1))
    b_router = _lane_pad(jnp.concatenate([moe_b_re[0].reshape(-1), moe_b_rg[0]])[None, :])
    x1, h2, route, counts, slots = _outproj_router(
        yr.reshape(T, rw), ys.reshape(T, SSD_WIDTH), x.reshape(T, d), mod3,
        norm_post_mix[0][None, :], norm_pre_ffn[0][None, :], wo[:rw], wo[rw:], w_router, b_router, L)

    mb = MB_EXPERT
    n_blocks = -(-(T * TOP_K + N_EXPERTS * (mb - 1)) // mb)
    cap = n_blocks * mb
    cnt = counts[0, :N_EXPERTS].astype(jnp.int32)
    padded = (cnt + mb - 1) // mb * mb
    pad_end = jnp.cumsum(padded)
    pad_start = pad_end - padded
    per = slots.shape[1] // (2 * TOP_K)
    experts_kt = jnp.stack([slots[:, kk * per:(kk + 1) * per].reshape(T) for kk in range(TOP_K)])
    ranks_kt = jnp.stack([slots[:, (TOP_K + kk) * per:(TOP_K + kk + 1) * per].reshape(T) for kk in range(TOP_K)])
    dest = ranks_kt
    for e in range(N_EXPERTS):
        dest = dest + jnp.where(experts_kt == e, pad_start[e], 0)
    blk_start = jnp.arange(n_blocks, dtype=jnp.int32) * mb
    blk_expert = jnp.minimum(jnp.sum((pad_end[None, :] <= blk_start[:, None]).astype(jnp.int32), axis=1),
                             N_EXPERTS - 1)
    n_used = (pad_end[-1:] // mb).astype(jnp.int32)

    xs = _dispatch(pad_end.astype(jnp.int32), dest, h2, cap)
    yb = _experts(blk_expert, n_used, xs, moe_w_gate[0], moe_w_up[0], moe_w_down[0])
    out = _combine(dest, route, x1, mod3, norm_post_ffn[0][None, :], yb, L)
    return out.reshape(b, L, d)
```

```python
import functools
import math

import jax
import jax.numpy as jnp
from jax import lax
from jax.experimental import pallas as pl
from jax.experimental.pallas import tpu as pltpu

F32 = jnp.float32
BF16 = jnp.bfloat16
HIGHEST = lax.Precision.HIGHEST

LANES = 128
SUBLANES = 8

EPS = 1e-6
CHUNK = 128
GRID_W = 64
RET_HEADS = 4
RET_DK = 128
ROPE_BASE = 10000.0
SSD_HEADS = 8
SSD_HEADDIM = 64
SSD_GROUPS = 2
SSD_STATE = 128
SSD_WIDTH = SSD_HEADS * SSD_HEADDIM
SSD_CONV = 5
SSD_PAIRS = SSD_WIDTH // LANES
MOE_GROUPS = 4
EXPERTS_PER_GROUP = 8
N_EXPERTS = MOE_GROUPS * EXPERTS_PER_GROUP
TOP_K = 2
CONV_HALO = SUBLANES

TM_PROJ = 512
TM_OUT = 512
TM_ROUTE_SUB = 512
TD_DISPATCH = 512
MB_EXPERT = 256
TF_COMBINE = 512
DMA_UNROLL = 8
DISPATCH_RING = 3
SSD_UNROLL = 2
RET_UNROLL = 2
NEG_BIG = -1e30


def _silu(v):
    return v * jax.nn.sigmoid(v)


def _dot(a, b):
    return jnp.dot(a, b, preferred_element_type=F32)


def _dot_tn(a, b):
    return lax.dot_general(a, b, (((0,), (0,)), ((), ())), preferred_element_type=F32)


def _dot_nt(a, b):
    return lax.dot_general(a, b, (((1,), (1,)), ((), ())), preferred_element_type=F32)


def _mod_kernel(c_ref, w_ref, b_ref, o_ref):
    a = _silu(c_ref[...])
    o_ref[...] = jnp.dot(a, w_ref[...], precision=HIGHEST, preferred_element_type=F32) + b_ref[...]


def _modulation(c_all, w_mod, b_mod):
    rows, d = c_all.shape
    n = w_mod.shape[1]
    return pl.pallas_call(
        _mod_kernel,
        grid=(n // d,),
        in_specs=[
            pl.BlockSpec((rows, d), lambda j: (0, 0)),
            pl.BlockSpec((d, d), lambda j: (0, j)),
            pl.BlockSpec((1, d), lambda j: (0, j)),
        ],
        out_specs=pl.BlockSpec((rows, d), lambda j: (0, j)),
        out_shape=jax.ShapeDtypeStruct((rows, n), F32),
        name="modulation",
    )(c_all, w_mod, b_mod.reshape(1, n))


def _norm_mod(x, nw, sc, sh):
    ms = jnp.mean(x * x, axis=-1, keepdims=True)
    return (x * lax.rsqrt(ms + EPS)) * nw * (1.0 + sc) + sh


def _rope(t, cos, sin_signed, first_half):
    width = t.shape[-1]
    quarter = RET_DK // 4
    swapped = jnp.where(first_half, pltpu.roll(t, width - quarter, 1), pltpu.roll(t, quarter, 1))
    return t * cos + swapped * sin_signed


def _inproj_kernel(x_ref, sh_ref, sc_ref, nw_ref, wqk_ref, wvgz_ref, wxbc_ref, wdt_ref, cos_ref, sin_ref,
                   q_ref, k_ref, v_ref, g_ref, z_ref, xbc_ref, dt_ref):
    hb = _norm_mod(x_ref[0], nw_ref[...], sc_ref[0], sh_ref[0]).astype(BF16)
    rw = q_ref.shape[-1]
    qk = _dot(hb, wqk_ref[...])
    cos = cos_ref[...]
    sin = sin_ref[...]
    lane = lax.broadcasted_iota(jnp.int32, cos.shape, 1)
    first_half = (lane % (RET_DK // 2)) < (RET_DK // 4)
    q_ref[0] = _rope(qk[:, :rw], cos, sin, first_half).astype(BF16)
    k_ref[0] = (_rope(qk[:, rw:], cos, sin, first_half) * (RET_DK ** -0.5)).astype(BF16)
    vgz = _dot(hb, wvgz_ref[...])
    v_ref[0] = vgz[:, :rw].astype(BF16)
    g_ref[0] = vgz[:, rw:2 * rw].astype(BF16)
    z_ref[0] = vgz[:, 2 * rw:].astype(BF16)
    xbc_ref[0] = _dot(hb, wxbc_ref[...]).astype(BF16)
    dt_ref[0] = _dot(hb, wdt_ref[...])


def _inproj_ctx_kernel(x_ref, sh_ref, sc_ref, nw_ref, wk_ref, wv_ref, wxbc_ref, wdt_ref,
                       k_ref, v_ref, xbc_ref, dt_ref):
    hb = _norm_mod(x_ref[0], nw_ref[...], sc_ref[0], sh_ref[0]).astype(BF16)
    k_ref[0] = (_dot(hb, wk_ref[...]) * (RET_DK ** -0.5)).astype(BF16)
    v_ref[0] = _dot(hb, wv_ref[...]).astype(BF16)
    xbc_ref[0] = _dot(hb, wxbc_ref[...]).astype(BF16)
    dt_ref[0] = _dot(hb, wdt_ref[...])


def _const_spec(shape):
    nd = len(shape)
    return pl.BlockSpec(shape, lambda *_: (0,) * nd)


def _inproj(x, mod3, nw, wqk, wvgz, wxbc, wdt, cos_t, sin_t):
    b, L, d = x.shape
    tm = min(TM_PROJ, L)
    rw = wqk.shape[1] // 2
    tok = lambda w: pl.BlockSpec((1, tm, w), lambda i, j: (i, j, 0))
    out_bf = lambda w: jax.ShapeDtypeStruct((b, L, w), BF16)
    return pl.pallas_call(
        _inproj_kernel,
        grid=(b, L // tm),
        in_specs=[
            tok(d),
            pl.BlockSpec((1, 1, d), lambda i, j: (i, 0, 0)),
            pl.BlockSpec((1, 1, d), lambda i, j: (i, 0, 1)),
            _const_spec((1, d)),
            _const_spec(wqk.shape), _const_spec(wvgz.shape), _const_spec(wxbc.shape), _const_spec(wdt.shape),
            pl.BlockSpec((tm, rw), lambda i, j: (j, 0)),
            pl.BlockSpec((tm, rw), lambda i, j: (j, 0)),
        ],
        out_specs=[tok(rw), tok(rw), tok(rw), tok(rw), tok(rw), tok(wxbc.shape[1]), tok(LANES)],
        out_shape=[out_bf(rw), out_bf(rw), out_bf(rw), out_bf(rw), out_bf(rw), out_bf(wxbc.shape[1]),
                   jax.ShapeDtypeStruct((b, L, LANES), F32)],
        compiler_params=pltpu.CompilerParams(vmem_limit_bytes=48 * 1024 * 1024),
        name="inproj",
    )(x, mod3, mod3, nw, wqk, wvgz, wxbc, wdt, cos_t, sin_t)


def _inproj_ctx(ctx, mod3, ctx_row, nw, wk, wv, wxbc, wdt):
    b, L, d = ctx.shape
    tm = min(TM_PROJ, L)
    rw = wk.shape[1]
    tok = lambda w: pl.BlockSpec((1, tm, w), lambda i, j: (i, j, 0))
    out_bf = lambda w: jax.ShapeDtypeStruct((b, L, w), BF16)
    return pl.pallas_call(
        _inproj_ctx_kernel,
        grid=(b, L // tm),
        in_specs=[
            tok(d),
            pl.BlockSpec((1, 1, d), lambda i, j: (ctx_row, 0, 0)),
            pl.BlockSpec((1, 1, d), lambda i, j: (ctx_row, 0, 1)),
            _const_spec((1, d)),
            _const_spec(wk.shape), _const_spec(wv.shape), _const_spec(wxbc.shape), _const_spec(wdt.shape),
        ],
        out_specs=[tok(rw), tok(rw), tok(wxbc.shape[1]), tok(LANES)],
        out_shape=[out_bf(rw), out_bf(rw), out_bf(wxbc.shape[1]), jax.ShapeDtypeStruct((b, L, LANES), F32)],
        compiler_params=pltpu.CompilerParams(vmem_limit_bytes=48 * 1024 * 1024),
        name="inproj_ctx",
    )(ctx, mod3, mod3, nw, wk, wv, wxbc, wdt)


def _ssd_kernel(xbc_ref, z_ref, dt_ref, xbcc_ref, dtc_ref, cw_ref, cb_ref, dtb_ref, alog_ref, dsk_ref, nw_ref,
                y_ref,
                xpad, xpadc, u, uc, dtv, dav, dtcv, dacv, sf_scr):
    L = xbc_ref.shape[1]
    Lc = xbcc_ref.shape[1]
    nch = L // CHUNK
    nchc = Lc // CHUNK
    win = CHUNK + 2 * CONV_HALO
    nconv = xbc_ref.shape[2]
    nh = SSD_HEADS

    def conv_pass(src_ref, pad_ref, dst_ref, n_chunks, length):
        zeros = jnp.zeros((CONV_HALO, nconv), F32)
        pad_ref[0:CONV_HALO, :] = zeros
        pad_ref[CONV_HALO + length:2 * CONV_HALO + length, :] = zeros
        pad_ref[CONV_HALO:CONV_HALO + length, :] = src_ref[0].astype(F32)

        def chunk(c, carry):
            base = pl.multiple_of(c * CHUNK, CHUNK)
            for cb_i in range(nconv // LANES):
                cols = slice(cb_i * LANES, (cb_i + 1) * LANES)
                w = pad_ref[pl.ds(base, win), cols]
                acc = cb_ref[:, cols] + w[CONV_HALO:CONV_HALO + CHUNK] * cw_ref[SSD_CONV // 2:SSD_CONV // 2 + 1, cols]
                for j in range(SSD_CONV):
                    if j == SSD_CONV // 2:
                        continue
                    shifted = pltpu.roll(w, (SSD_CONV // 2 - j) % win, 0)
                    acc = acc + shifted[CONV_HALO:CONV_HALO + CHUNK] * cw_ref[j:j + 1, cols]
                dst_ref[pl.ds(base, CHUNK), cols] = _silu(acc).astype(BF16)
            return carry

        lax.fori_loop(0, n_chunks, chunk, 0)

    conv_pass(xbcc_ref, xpadc, uc, nchc, Lc)
    conv_pass(xbc_ref, xpad, u, nch, L)

    a_neg = -jnp.exp(alog_ref[...])
    dtv[...] = jax.nn.softplus(dt_ref[0] + dtb_ref[...])
    dav[...] = dtv[...] * a_neg
    dtcv[...] = jax.nn.softplus(dtc_ref[0] + dtb_ref[...])
    dacv[...] = dtcv[...] * a_neg

    row_i = lax.broadcasted_iota(jnp.int32, (CHUNK, CHUNK), 0)
    col_i = lax.broadcasted_iota(jnp.int32, (CHUNK, CHUNK), 1)
    causal = col_i <= row_i
    tri = causal.astype(BF16)
    tri_t = (row_i <= col_i).astype(BF16)
    lo_half = col_i < SSD_HEADDIM
    fwd_lane = col_i < nh
    head_of = lax.broadcasted_iota(jnp.int32, (CHUNK, SSD_WIDTH), 1) // SSD_HEADDIM
    src_col = lax.broadcasted_iota(jnp.int32, (CHUNK, SSD_WIDTH), 0)
    exp_f = (head_of == src_col).astype(BF16)
    exp_b = (head_of == src_col - nh).astype(BF16)
    exp_fb = jnp.concatenate([exp_f, exp_b], axis=1)

    def split3(v):
        hi = v.astype(BF16)
        r1 = v - hi.astype(F32)
        mid = r1.astype(BF16)
        return hi, mid, (r1 - mid.astype(F32)).astype(BF16)

    def times_onehot(v, m):
        hi, mid, lo = split3(v)
        return _dot(hi, m) + _dot(mid, m) + _dot(lo, m)

    def onehot_times(m, v):
        hi, mid, lo = split3(v)
        return _dot(m, hi) + _dot(m, mid) + _dot(m, lo)

    def colb(mat, r):
        return jnp.broadcast_to(mat[:, r:r + 1], (CHUNK, CHUNK))

    def pair_sel(a, b_):
        return jnp.where(lo_half, a, b_)

    def chunk_scalars(dt_s, da_s, base):
        dt = dt_s[pl.ds(base, CHUNK), :]
        da = da_s[pl.ds(base, CHUNK), :]
        acol = onehot_times(tri, da)
        return dt, da, acol, acol - da

    def state_update(u_ref, base, dt, acol, ecol, s_old, backward):
        last = acol[CHUNK - 1:CHUNK, :]
        wgt = jnp.exp(ecol) * dt if backward else jnp.exp(last - acol) * dt
        small = jnp.concatenate([wgt, jnp.broadcast_to(jnp.exp(last), (SUBLANES, LANES))], axis=0)
        wide = times_onehot(small, exp_b if backward else exp_f)
        xw = (u_ref[pl.ds(base, CHUNK), 0:SSD_WIDTH].astype(F32) * wide[:CHUNK]).astype(BF16)
        dec = wide[CHUNK:CHUNK + 1]
        new = []
        for g in range(SSD_GROUPS):
            gl = slice(g * 2 * LANES, (g + 1) * 2 * LANES)
            bm = u_ref[pl.ds(base, CHUNK), SSD_WIDTH + g * SSD_STATE:SSD_WIDTH + (g + 1) * SSD_STATE]
            new.append(dec[:, gl] * s_old[g] + _dot_tn(bm, xw[:, gl]))
        return new

    def ctx_update(c, s, backward):
        dt, _, acol, ecol = chunk_scalars(dtcv, dacv, c * CHUNK)
        return state_update(uc, c * CHUNK, dt, acol, ecol, s, backward)

    zero_state = [jnp.zeros((SSD_STATE, 2 * LANES), F32) for _ in range(SSD_GROUPS)]

    s = zero_state
    for c in range(nchc):
        s = ctx_update(c, s, False)
    s_f0 = s
    s = zero_state
    for c in reversed(range(nchc)):
        s = ctx_update(c, s, True)
    s_b0 = s

    def fwd(c, s_old):
        base = pl.multiple_of(c * CHUNK, CHUNK)
        for g in range(SSD_GROUPS):
            sf_scr[c, g] = s_old[g]
        dt, _, acol, ecol = chunk_scalars(dtv, dav, base)
        return tuple(state_update(u, base, dt, acol, ecol, s_old, False))

    lax.fori_loop(0, nch, fwd, tuple(s_f0), unroll=SSD_UNROLL)

    def bwd(i, s_b):
        c = nch - 1 - i
        base = pl.multiple_of(c * CHUNK, CHUNK)
        dt, da, acol, ecol = chunk_scalars(dtv, dav, base)
        da_t = da.T
        dt_t = dt.T
        arow = times_onehot(da_t, tri_t)
        erow = arow - da_t
        last = acol[CHUNK - 1:CHUNK, :]
        scale = times_onehot(jnp.where(fwd_lane, jnp.exp(acol), jnp.exp(last - ecol)), exp_fb)
        ys = []
        for g in range(SSD_GROUPS):
            bm = u[pl.ds(base, CHUNK), SSD_WIDTH + g * SSD_STATE:SSD_WIDTH + (g + 1) * SSD_STATE]
            cm = u[pl.ds(base, CHUNK), SSD_WIDTH + (SSD_GROUPS + g) * SSD_STATE:SSD_WIDTH + (SSD_GROUPS + g + 1) * SSD_STATE]
            cbm = _dot_nt(cm, bm)
            cs_f = _dot(cm, sf_scr[c, g].astype(BF16))
            cs_b = _dot(cm, s_b[g].astype(BF16))
            for pp in range(SSD_PAIRS // SSD_GROUPS):
                p = g * (SSD_PAIRS // SSD_GROUPS) + pp
                xs_b = u[pl.ds(base, CHUNK), p * LANES:(p + 1) * LANES]
                y_h = []
                for hh in range(2):
                    r = 2 * p + hh
                    arg = jnp.where(causal, colb(acol, r) - arow[r:r + 1, :],
                                    erow[nh + r:nh + r + 1, :] - colb(ecol, nh + r))
                    coef = jnp.where(causal, dt_t[r:r + 1, :], dt_t[nh + r:nh + r + 1, :])
                    gm = (cbm * (jnp.exp(arg) * coef)).astype(BF16)
                    y_h.append(_dot(gm, xs_b))
                sl = slice(pp * LANES, (pp + 1) * LANES)
                wl = slice(p * LANES, (p + 1) * LANES)
                wlb = slice(SSD_WIDTH + p * LANES, SSD_WIDTH + (p + 1) * LANES)
                ys.append(pair_sel(y_h[0], y_h[1]) + cs_f[:, sl] * scale[:, wl] + cs_b[:, sl] * scale[:, wlb]
                          + dsk_ref[:, wl] * xs_b.astype(F32))
        y = jnp.concatenate(ys, axis=1)
        y = y * _silu(z_ref[0, pl.ds(base, CHUNK), :].astype(F32))
        ms = jnp.mean(y * y, axis=-1, keepdims=True)
        y_ref[0, pl.ds(base, CHUNK), :] = ((y * lax.rsqrt(ms + EPS)) * nw_ref[...]).astype(BF16)
        return tuple(state_update(u, base, dt, acol, ecol, s_b, True))

    lax.fori_loop(0, nch, bwd, tuple(s_b0), unroll=SSD_UNROLL)


def _ssd(xbc, z, dt, xbcc, dtc, conv_w8, conv_b, dt_bias, a_log, d_skip, norm_w):
    b, L, nconv = xbc.shape
    Lc = xbcc.shape[1]
    nch = L // CHUNK
    per_b = lambda n, w: pl.BlockSpec((1, n, w), lambda i: (i, 0, 0))
    return pl.pallas_call(
        _ssd_kernel,
        grid=(b,),
        in_specs=[
            per_b(L, nconv), per_b(L, SSD_WIDTH), per_b(L, LANES), per_b(Lc, nconv), per_b(Lc, LANES),
            _const_spec(conv_w8.shape), _const_spec(conv_b.shape), _const_spec(dt_bias.shape),
            _const_spec(a_log.shape), _const_spec(d_skip.shape), _const_spec(norm_w.shape),
        ],
        out_specs=per_b(L, SSD_WIDTH),
        out_shape=jax.ShapeDtypeStruct((b, L, SSD_WIDTH), BF16),
        scratch_shapes=[
            pltpu.VMEM((L + 2 * CONV_HALO, nconv), F32),
            pltpu.VMEM((Lc + 2 * CONV_HALO, nconv), F32),
            pltpu.VMEM((L, nconv), BF16),
            pltpu.VMEM((Lc, nconv), BF16),
            pltpu.VMEM((L, LANES), F32), pltpu.VMEM((L, LANES), F32),
            pltpu.VMEM((Lc, LANES), F32), pltpu.VMEM((Lc, LANES), F32),
            pltpu.VMEM((nch, SSD_GROUPS, SSD_STATE, 2 * LANES), F32),
        ],
        compiler_params=pltpu.CompilerParams(vmem_limit_bytes=56 * 1024 * 1024),
        name="ssd",
    )(xbc, z, dt, xbcc, dtc, conv_w8, conv_b, dt_bias, a_log, d_skip, norm_w)


def _ret_kernel(q_ref, k_ref, v_ref, g_ref, kc_ref, vc_ref, df_ref, db_ref, gn_ref, y_ref, sf_scr):
    L = q_ref.shape[1]
    Lc = kc_ref.shape[1]
    nch = L // CHUNK
    dk = RET_DK
    row_i = lax.broadcasted_iota(jnp.int32, (CHUNK, dk), 0).astype(F32)
    col_i = lax.broadcasted_iota(jnp.int32, (CHUNK, dk), 1).astype(F32)
    rel = row_i - col_i
    crow = lax.broadcasted_iota(jnp.int32, (Lc, dk), 0).astype(F32)

    heads = []
    s_f0 = []
    s_b0 = []
    for h in range(RET_HEADS):
        cols = slice(h * dk, (h + 1) * dk)
        lg_f = -jnp.exp(df_ref[:, cols])
        lg_b = -jnp.exp(db_ref[:, cols])
        heads.append(dict(
            cols=cols,
            dmat=jnp.where(rel >= 0, jnp.exp(jnp.maximum(rel, 0.0) * lg_f), jnp.exp(jnp.maximum(-rel, 0.0) * lg_b)),
            dq_f=jnp.exp((row_i + 1.0) * lg_f),
            dq_b=jnp.exp((CHUNK - row_i) * lg_b),
            dk_f=jnp.exp((CHUNK - 1.0 - row_i) * lg_f),
            dk_b=jnp.exp(row_i * lg_b),
            dc_f=jnp.exp(CHUNK * lg_f),
            dc_b=jnp.exp(CHUNK * lg_b),
        ))
        kc = kc_ref[0, :, cols].astype(F32)
        vc = vc_ref[0, :, cols]
        s_f0.append(_dot_tn((kc * jnp.exp((Lc - 1.0 - crow) * lg_f)).astype(BF16), vc))
        s_b0.append(_dot_tn((kc * jnp.exp(crow * lg_b)).astype(BF16), vc))

    def fwd(c, s_f):
        base = pl.multiple_of(c * CHUNK, CHUNK)
        new = []
        for h, hd in enumerate(heads):
            sf_scr[c, h] = s_f[h]
            kk = k_ref[0, pl.ds(base, CHUNK), hd["cols"]].astype(F32)
            vv = v_ref[0, pl.ds(base, CHUNK), hd["cols"]]
            new.append(hd["dc_f"] * s_f[h] + _dot_tn((kk * hd["dk_f"]).astype(BF16), vv))
        return tuple(new)

    lax.fori_loop(0, nch, fwd, tuple(s_f0), unroll=RET_UNROLL)

    def bwd(i, s_bs):
        c = nch - 1 - i
        base = pl.multiple_of(c * CHUNK, CHUNK)
        new = []
        for h, hd in enumerate(heads):
            qq = q_ref[0, pl.ds(base, CHUNK), hd["cols"]]
            kk = k_ref[0, pl.ds(base, CHUNK), hd["cols"]]
            vv = v_ref[0, pl.ds(base, CHUNK), hd["cols"]]
            s_b = s_bs[h]
            scores = (_dot_nt(qq, kk) * hd["dmat"]).astype(BF16)
            y = (_dot(scores, vv)
                 + _dot(qq, sf_scr[c, h].astype(BF16)) * hd["dq_f"]
                 + _dot(qq, s_b.astype(BF16)) * hd["dq_b"])
            mu = jnp.mean(y, axis=-1, keepdims=True)
            yc = y - mu
            var = jnp.mean(yc * yc, axis=-1, keepdims=True)
            yn = (yc * lax.rsqrt(var + EPS)) * gn_ref[:, hd["cols"]]
            gate = _silu(g_ref[0, pl.ds(base, CHUNK), hd["cols"]].astype(F32))
            y_ref[0, pl.ds(base, CHUNK), hd["cols"]] = (yn * gate).astype(BF16)
            new.append(hd["dc_b"] * s_b + _dot_tn((kk.astype(F32) * hd["dk_b"]).astype(BF16), vv))
        return tuple(new)

    lax.fori_loop(0, nch, bwd, tuple(s_b0), unroll=RET_UNROLL)


def _retention(q, k, v, g, kc, vc, decay_f, decay_b, gn_w):
    b, L, w = q.shape
    Lc = kc.shape[1]
    nch = L // CHUNK
    per_b = lambda n: pl.BlockSpec((1, n, w), lambda i: (i, 0, 0))
    return pl.pallas_call(
        _ret_kernel,
        grid=(b,),
        in_specs=[per_b(L), per_b(L), per_b(L), per_b(L), per_b(Lc), per_b(Lc),
                  _const_spec((1, w)), _const_spec((1, w)), _const_spec((1, w))],
        out_specs=per_b(L),
        out_shape=jax.ShapeDtypeStruct((b, L, w), BF16),
        scratch_shapes=[
            pltpu.VMEM((nch, RET_HEADS, RET_DK, RET_DK), F32),
        ],
        compiler_params=pltpu.CompilerParams(vmem_limit_bytes=48 * 1024 * 1024),
        name="retention",
    )(q, k, v, g, kc, vc, decay_f, decay_b, gn_w)


def _outproj_router_kernel(yr_ref, ys_ref, x_ref, g1_ref, sh2_ref, sc2_ref, npost_ref, npre_ref,
                           wor_ref, wos_ref, wr_ref, br_ref, tri_ref,
                           x1_ref, h2_ref, route_ref, cnt_ref, slots_ref,
                           wcat, carry):
    i = pl.program_id(0)

    @pl.when(i == 0)
    def _():
        wr = wr_ref[...]
        hi = wr.astype(BF16)
        wcat[:, :LANES] = hi
        wcat[:, LANES:] = (wr - hi.astype(F32)).astype(BF16)
        carry[...] = jnp.zeros_like(carry)

    tm = tri_ref.shape[0]
    running = carry[...]
    for sub in range(x_ref.shape[0] // tm):
        running = _route_subtile(sub, tm, running, yr_ref, ys_ref, x_ref, g1_ref, sh2_ref, sc2_ref, npost_ref,
                                 npre_ref, wor_ref, wos_ref, br_ref, tri_ref, x1_ref, h2_ref, route_ref,
                                 slots_ref, wcat)
    carry[...] = running
    cnt_ref[...] = running


def _route_subtile(sub, tm, running, yr_ref, ys_ref, x_ref, g1_ref, sh2_ref, sc2_ref, npost_ref, npre_ref,
                   wor_ref, wos_ref, br_ref, tri_ref, x1_ref, h2_ref, route_ref, slots_ref, wcat):
    rows = slice(sub * tm, (sub + 1) * tm)
    y = _dot(yr_ref[rows, :], wor_ref[...]) + _dot(ys_ref[rows, :], wos_ref[...])
    ms = jnp.mean(y * y, axis=-1, keepdims=True)
    x1 = x_ref[rows, :] + g1_ref[0] * ((y * lax.rsqrt(ms + EPS)) * npost_ref[...])
    x1_ref[rows, :] = x1
    h2 = _norm_mod(x1, npre_ref[...], sc2_ref[0], sh2_ref[0])
    h2_ref[rows, :] = h2

    h_hi = h2.astype(BF16)
    h_lo = (h2 - h_hi.astype(F32)).astype(BF16)
    both = _dot(h_hi, wcat[...])
    lg = both[:, :LANES] + both[:, LANES:] + _dot(h_lo, wcat[:, :LANES]) + br_ref[...]

    lane = lax.broadcasted_iota(jnp.int32, (tm, LANES), 1)
    lane_f = lane.astype(F32)
    is_grp = (lane >= N_EXPERTS) & (lane < N_EXPERTS + MOE_GROUPS)
    gl = jnp.where(is_grp, lg, NEG_BIG)
    mg = jnp.max(gl, axis=-1, keepdims=True)
    grp_lane = jnp.min(jnp.where(gl == mg, lane_f, 1e9), axis=-1, keepdims=True)
    p_g = 1.0 / jnp.sum(jnp.where(is_grp, jnp.exp(gl - mg), 0.0), axis=-1, keepdims=True)
    first = (grp_lane - N_EXPERTS) * EXPERTS_PER_GROUP
    in_grp = (lane_f >= first) & (lane_f < first + EXPERTS_PER_GROUP)
    el = jnp.where(in_grp, lg, NEG_BIG)
    t1 = jnp.max(el, axis=-1, keepdims=True)
    i1 = jnp.min(jnp.where(el == t1, lane_f, 1e9), axis=-1, keepdims=True)
    el2 = jnp.where(lane_f == i1, NEG_BIG, el)
    t2 = jnp.max(el2, axis=-1, keepdims=True)
    i2 = jnp.min(jnp.where(el2 == t2, lane_f, 1e9), axis=-1, keepdims=True)
    s = jnp.exp(t2 - t1)
    w1 = p_g / (1.0 + s)
    w2 = p_g * s / (1.0 + s)

    oh1 = (lane_f == i1)
    oh2 = (lane_f == i2)
    oh = (oh1 | oh2).astype(BF16)
    before = _dot(tri_ref[...], oh) + running
    rank1 = jnp.sum(jnp.where(oh1, before, 0.0), axis=-1, keepdims=True)
    rank2 = jnp.sum(jnp.where(oh2, before, 0.0), axis=-1, keepdims=True)

    route_ref[rows, :] = jnp.where(lane == 0, w1, jnp.where(lane == 1, w2, 0.0))

    row = lax.broadcasted_iota(jnp.int32, (tm, LANES), 0)
    on_diag = (row % LANES) == lane
    per = tm // LANES
    n_sub = x_ref.shape[0] // tm
    for qi, col in enumerate((i1, i2, rank1, rank2)):
        picked = jnp.where(on_diag, col, 0.0)
        dense = jnp.sum(picked.reshape(per, LANES, LANES), axis=1).astype(jnp.int32)
        slots_ref[0, (qi * n_sub + sub) * per:(qi * n_sub + sub + 1) * per, :] = dense
    return running + jnp.sum(oh.astype(F32), axis=0, keepdims=True)


def _outproj_router(yr, ys, x2, mod3, npost, npre, wo_r, wo_s, w_router, b_router, seq_len):
    T, d = x2.shape
    tm = TM_OUT
    per_seq = seq_len // tm
    rw = yr.shape[1]
    sub = TM_ROUTE_SUB
    tri = (jnp.arange(sub)[:, None] > jnp.arange(sub)[None, :]).astype(BF16)
    tok = lambda w: pl.BlockSpec((tm, w), lambda i: (i, 0))
    modv = lambda k: pl.BlockSpec((1, 1, d), lambda i: (i // per_seq, 0, k))
    return pl.pallas_call(
        _outproj_router_kernel,
        grid=(T // tm,),
        in_specs=[
            tok(rw), tok(rw), tok(d), modv(2), modv(3), modv(4),
            _const_spec((1, d)), _const_spec((1, d)),
            _const_spec(wo_r.shape), _const_spec(wo_s.shape), _const_spec(w_router.shape), _const_spec((1, LANES)),
            _const_spec((sub, sub)),
        ],
        out_specs=[tok(d), tok(d), tok(LANES), _const_spec((1, LANES)),
                   pl.BlockSpec((1, 2 * TOP_K * (tm // LANES), LANES), lambda i: (i, 0, 0))],
        out_shape=[jax.ShapeDtypeStruct((T, d), F32), jax.ShapeDtypeStruct((T, d), F32),
                   jax.ShapeDtypeStruct((T, LANES), F32), jax.ShapeDtypeStruct((1, LANES), F32),
                   jax.ShapeDtypeStruct((T // tm, 2 * TOP_K * (tm // LANES), LANES), jnp.int32)],
        scratch_shapes=[pltpu.VMEM((d, 2 * LANES), BF16), pltpu.VMEM((1, LANES), F32)],
        compiler_params=pltpu.CompilerParams(dimension_semantics=("arbitrary",),
                                             vmem_limit_bytes=48 * 1024 * 1024),
        name="outproj_router",
    )(yr, ys, x2, mod3, mod3, mod3, npost, npre, wo_r, wo_s, w_router, b_router, tri)


def _dispatch_kernel(pad_end_ref, dest_ref, h_hbm, xs_hbm, zbuf, hbuf, sem, zsem, in_sem):
    i = pl.program_id(0)
    nt = pl.num_programs(0)
    td = hbuf.shape[1]
    mb = zbuf.shape[0]

    @pl.when(i == 0)
    def _():
        zbuf[...] = jnp.zeros_like(zbuf)

        def fill(e):
            return pltpu.make_async_copy(zbuf, xs_hbm.at[pl.ds(pl.multiple_of(pad_end_ref[e] - mb, mb), mb)], zsem)

        def has_rows(e):
            return pad_end_ref[e] > jnp.where(e == 0, 0, pad_end_ref[jnp.maximum(e - 1, 0)])

        def start(e, carry):
            @pl.when(has_rows(e))
            def _():
                fill(e).start()
            return carry

        def wait(e, carry):
            @pl.when(has_rows(e))
            def _():
                fill(e).wait()
            return carry

        def tail_fill(blk):
            return pltpu.make_async_copy(zbuf, xs_hbm.at[pl.ds(pl.multiple_of(blk * mb, mb), mb)], zsem)

        def tail_start(blk, carry):
            @pl.when(blk * mb >= pad_end_ref[N_EXPERTS - 1])
            def _():
                tail_fill(blk).start()
            return carry

        def tail_wait(blk, carry):
            @pl.when(blk * mb >= pad_end_ref[N_EXPERTS - 1])
            def _():
                tail_fill(blk).wait()
            return carry

        n_blocks = xs_hbm.shape[0] // mb
        lax.fori_loop(0, N_EXPERTS, start, 0)
        lax.fori_loop(0, n_blocks, tail_start, 0)
        lax.fori_loop(0, N_EXPERTS, wait, 0)
        lax.fori_loop(0, n_blocks, tail_wait, 0)

    def fetch(t):
        return pltpu.make_async_copy(h_hbm.at[pl.ds(pl.multiple_of(t * td, td), td)], hbuf.at[t % DISPATCH_RING],
                                     in_sem.at[t % DISPATCH_RING])

    @pl.when(i == 0)
    def _():
        fetch(i).start()

    @pl.when(i + 1 < nt)
    def _():
        fetch(i + 1).start()

    fetch(i).wait()
    for slot in range(DISPATCH_RING):
        @pl.when(i % DISPATCH_RING == slot)
        def _(slot=slot):
            _issue_row_copies(dest_ref, td, lambda kk, j, d: pltpu.make_async_copy(
                hbuf.at[slot, pl.ds(j, 1)], xs_hbm.at[pl.ds(d, 1)], sem.at[slot]))

    def drain(t):
        for kk in range(TOP_K):
            pltpu.make_async_copy(hbuf.at[t % DISPATCH_RING], xs_hbm.at[pl.ds(0, td)],
                                  sem.at[t % DISPATCH_RING]).wait()

    @pl.when(i > 0)
    def _():
        drain(i - 1)

    @pl.when(i == nt - 1)
    def _():
        drain(i)


def _issue_row_copies(dest_ref, n_tok, row_copy):
    def issue(j, carry):
        for kk in range(TOP_K):
            row_copy(kk, j, dest_ref[0, 0, kk * n_tok + j]).start()
        return carry

    lax.fori_loop(0, n_tok, issue, 0, unroll=DMA_UNROLL)


def _dispatch(pad_end, dest, h2, cap):
    T, d = h2.shape
    td = TD_DISPATCH
    nt = T // td
    grid_spec = pltpu.PrefetchScalarGridSpec(
        num_scalar_prefetch=1,
        grid=(nt,),
        in_specs=[
            pl.BlockSpec((1, 1, TOP_K * td), lambda i, pe: (i, 0, 0), memory_space=pltpu.SMEM),
            pl.BlockSpec(memory_space=pl.ANY),
        ],
        out_specs=pl.BlockSpec(memory_space=pl.ANY),
        scratch_shapes=[pltpu.VMEM((MB_EXPERT, d), F32), pltpu.VMEM((DISPATCH_RING, td, d), F32),
                        pltpu.SemaphoreType.DMA((DISPATCH_RING,)), pltpu.SemaphoreType.DMA(()),
                        pltpu.SemaphoreType.DMA((DISPATCH_RING,))],
    )
    return pl.pallas_call(
        _dispatch_kernel,
        grid_spec=grid_spec,
        out_shape=jax.ShapeDtypeStruct((cap, d), F32),
        compiler_params=pltpu.CompilerParams(dimension_semantics=("arbitrary",)),
        name="dispatch",
    )(pad_end, dest, h2)


def _expert_kernel(be_ref, nused_ref, xs_ref, wg_ref, wu_ref, wd_ref, y_ref, wg_b, wu_b, wd_b):
    i = pl.program_id(0)
    prev = be_ref[jnp.maximum(i - 1, 0)]

    @pl.when((i == 0) | (be_ref[i] != prev))
    def _():
        wg_b[...] = wg_ref[0].astype(BF16)
        wu_b[...] = wu_ref[0].astype(BF16)
        wd_b[...] = wd_ref[0].astype(BF16)

    @pl.when(i < nused_ref[0])
    def _():
        xb = xs_ref[...].astype(BF16)
        hid = (_silu(_dot(xb, wg_b[...])) * _dot(xb, wu_b[...])).astype(BF16)
        y_ref[...] = _dot(hid, wd_b[...])

    @pl.when(i >= nused_ref[0])
    def _():
        y_ref[...] = jnp.zeros_like(y_ref)


def _experts(blk_expert, n_used, xs, w_gate, w_up, w_down):
    cap, d = xs.shape
    de = w_gate.shape[2]
    mb = MB_EXPERT
    grid_spec = pltpu.PrefetchScalarGridSpec(
        num_scalar_prefetch=2,
        grid=(cap // mb,),
        in_specs=[
            pl.BlockSpec((mb, d), lambda i, be, nu: (jnp.minimum(i, nu[0] - 1), 0)),
            pl.BlockSpec((1, d, de), lambda i, be, nu: (be[i], 0, 0)),
            pl.BlockSpec((1, d, de), lambda i, be, nu: (be[i], 0, 0)),
            pl.BlockSpec((1, de, d), lambda i, be, nu: (be[i], 0, 0)),
        ],
        out_specs=pl.BlockSpec((mb, d), lambda i, be, nu: (i, 0)),
        scratch_shapes=[pltpu.VMEM((d, de), BF16), pltpu.VMEM((d, de), BF16), pltpu.VMEM((de, d), BF16)],
    )
    return pl.pallas_call(
        _expert_kernel,
        grid_spec=grid_spec,
        out_shape=jax.ShapeDtypeStruct((cap, d), F32),
        compiler_params=pltpu.CompilerParams(dimension_semantics=("arbitrary",),
                                             vmem_limit_bytes=48 * 1024 * 1024),
        name="experts",
    )(blk_expert, n_used, xs, w_gate, w_up, w_down)


def _combine_kernel(dest_ref, dest_next_ref, route_ref, x1_ref, g2_ref, nw_ref, yb_hbm, o_ref, buf, sem):
    i = pl.program_id(0)
    nt = pl.num_programs(0)
    tf = x1_ref.shape[0]
    stage = i % 2

    def gather(refs, st):
        _issue_row_copies(refs, tf, lambda kk, j, d: pltpu.make_async_copy(
            yb_hbm.at[pl.ds(d, 1)], buf.at[st, kk, pl.ds(j, 1)], sem.at[st]))

    @pl.when(i == 0)
    def _():
        gather(dest_ref, 0)

    for st in range(2):
        @pl.when((i + 1 < nt) & (stage != st))
        def _(st=st):
            gather(dest_next_ref, st)

    for kk in range(TOP_K):
        pltpu.make_async_copy(yb_hbm.at[pl.ds(0, tf)], buf.at[stage, kk], sem.at[stage]).wait()

    route = route_ref[...]
    out = route[:, 0:1] * buf[stage, 0] + route[:, 1:2] * buf[stage, 1]
    ms = jnp.mean(out * out, axis=-1, keepdims=True)
    o_ref[...] = x1_ref[...] + g2_ref[0] * ((out * lax.rsqrt(ms + EPS)) * nw_ref[...])


def _combine(dest, route, x1, mod3, nw, yb, seq_len):
    T, d = x1.shape
    tf = TF_COMBINE
    nt = T // tf
    per_seq = seq_len // tf
    slot_spec = functools.partial(pl.BlockSpec, (1, 1, TOP_K * tf), memory_space=pltpu.SMEM)
    return pl.pallas_call(
        _combine_kernel,
        grid=(nt,),
        in_specs=[
            slot_spec(lambda i: (i, 0, 0)),
            slot_spec(lambda i: (jnp.minimum(i + 1, nt - 1), 0, 0)),
            pl.BlockSpec((tf, LANES), lambda i: (i, 0)),
            pl.BlockSpec((tf, d), lambda i: (i, 0)),
            pl.BlockSpec((1, 1, d), lambda i: (i // per_seq, 0, 5)),
            _const_spec((1, d)),
            pl.BlockSpec(memory_space=pl.ANY),
        ],
        out_specs=pl.BlockSpec((tf, d), lambda i: (i, 0)),
        out_shape=jax.ShapeDtypeStruct((T, d), F32),
        scratch_shapes=[pltpu.VMEM((2, TOP_K, tf, d), F32), pltpu.SemaphoreType.DMA((2,))],
        compiler_params=pltpu.CompilerParams(dimension_semantics=("arbitrary",),
                                             vmem_limit_bytes=48 * 1024 * 1024),
        name="combine",
    )(dest, dest, route, x1, mod3, nw, yb)


def _rope_tables(L, n_heads):
    quarter = RET_DK // 4
    freqs = ROPE_BASE ** (-jnp.arange(quarter, dtype=F32) / quarter)
    t = jnp.arange(L)
    ang_r = (t // GRID_W).astype(F32)[:, None] * freqs
    ang_c = (t % GRID_W).astype(F32)[:, None] * freqs
    cos = jnp.concatenate([jnp.cos(ang_r)] * 2 + [jnp.cos(ang_c)] * 2, axis=-1)
    sin = jnp.concatenate([-jnp.sin(ang_r), jnp.sin(ang_r), -jnp.sin(ang_c), jnp.sin(ang_c)], axis=-1)
    return jnp.tile(cos, (1, n_heads)), jnp.tile(sin, (1, n_heads))


def _lane_pad(v, width=LANES):
    return jnp.pad(v, [(0, 0)] * (v.ndim - 1) + [(0, width - v.shape[-1])])


def kernel(x, c, ctx, c_ctx, w_mod, b_mod, norm_pre_mix, norm_post_mix, norm_pre_ffn, norm_post_ffn, w_in, w_out, ret_decay_f, ret_decay_b, ret_gn_w, ssd_conv_w, ssd_conv_b, ssd_dt_bias_f, ssd_dt_bias_b, ssd_a_log_f, ssd_a_log_b, ssd_d, ssd_norm_w, moe_w_rg, moe_b_rg, moe_w_re, moe_b_re, moe_w_gate, moe_w_up, moe_w_down):
    b, L, d = x.shape
    assert w_mod.shape[0] == 1, "single layer: context outputs are never needed"
    assert TM_OUT == TD_DISPATCH == TF_COMBINE, "router, dispatch and combine share one slot-row layout"
    rw = RET_HEADS * RET_DK
    nconv = SSD_WIDTH + 2 * SSD_GROUPS * SSD_STATE
    T = b * L

    mod_rows = -(-(b + 1) // SUBLANES) * SUBLANES
    c_all = jnp.zeros((mod_rows, d), F32).at[:b].set(c).at[b].set(c_ctx)
    mod3 = _modulation(c_all, w_mod[0], b_mod[0]).reshape(mod_rows, 1, 6 * d)

    wi = w_in[0]
    o = 0
    wq = wi[:, o:o + rw]; o += rw
    wk = wi[:, o:o + rw]; o += rw
    wv = wi[:, o:o + rw]; o += rw
    wg = wi[:, o:o + rw]; o += rw
    wz = wi[:, o:o + SSD_WIDTH]; o += SSD_WIDTH
    wxbc = wi[:, o:o + nconv].astype(BF16); o += nconv
    wdt = _lane_pad(wi[:, o:o + 2 * SSD_HEADS]).astype(BF16)
    wqk = jnp.concatenate([wq, wk], axis=1).astype(BF16)
    wvgz = jnp.concatenate([wv, wg, wz], axis=1).astype(BF16)
    cos_t, sin_t = _rope_tables(L, RET_HEADS)
    nw1 = norm_pre_mix[0].reshape(1, d)

    q, k, v, g, z, xbc, dt = _inproj(x, mod3, nw1, wqk, wvgz, wxbc, wdt, cos_t, sin_t)
    kc, vc, xbcc, dtc = _inproj_ctx(ctx, mod3, b, nw1, wk.astype(BF16), wv.astype(BF16), wxbc, wdt)

    conv_w8 = jnp.pad(ssd_conv_w[0], ((0, SUBLANES - SSD_CONV), (0, 0)))
    dt_bias = _lane_pad(jnp.concatenate([ssd_dt_bias_f[0], ssd_dt_bias_b[0]])[None, :])
    a_log = _lane_pad(jnp.concatenate([ssd_a_log_f[0], ssd_a_log_b[0]])[None, :])
    d_skip = jnp.repeat(ssd_d[0], SSD_HEADDIM)[None, :]
    ys = _ssd(xbc, z, dt, xbcc, dtc, conv_w8, ssd_conv_b[0][None, :], dt_bias, a_log, d_skip,
              ssd_norm_w[0][None, :])

    yr = _retention(q, k, v, g, kc, vc,
                    jnp.repeat(ret_decay_f[0], RET_DK)[None, :], jnp.repeat(ret_decay_b[0], RET_DK)[None, :],
                    ret_gn_w[0][None, :])

    wo = w_out[0].astype(BF16)
    w_router = _lane_pad(jnp.concatenate(
        [jnp.transpose(moe_w_re[0], (1, 0, 2)).reshape(d, N_EXPERTS), moe_w_rg[0]], axis=1))
    b_router = _lane_pad(jnp.concatenate([moe_b_re[0].reshape(-1), moe_b_rg[0]])[None, :])
    x1, h2, route, counts, slots = _outproj_router(
        yr.reshape(T, rw), ys.reshape(T, SSD_WIDTH), x.reshape(T, d), mod3,
        norm_post_mix[0][None, :], norm_pre_ffn[0][None, :], wo[:rw], wo[rw:], w_router, b_router, L)

    mb = MB_EXPERT
    n_blocks = -(-(T * TOP_K + N_EXPERTS * (mb - 1)) // mb)
    cap = n_blocks * mb
    cnt = counts[0, :N_EXPERTS].astype(jnp.int32)
    padded = (cnt + mb - 1) // mb * mb
    pad_end = jnp.cumsum(padded)
    pad_start = pad_end - padded
    half = slots.shape[1] // 2
    dest = slots[:, half:]
    for e in range(N_EXPERTS):
        dest = dest + jnp.where(slots[:, :half] == e, pad_start[e], 0)
    blk_start = jnp.arange(n_blocks, dtype=jnp.int32) * mb
    blk_expert = jnp.minimum(jnp.sum((pad_end[None, :] <= blk_start[:, None]).astype(jnp.int32), axis=1),
                             N_EXPERTS - 1)
    n_used = (pad_end[-1:] // mb).astype(jnp.int32)

    dest = dest.reshape(dest.shape[0], 1, -1)
    xs = _dispatch(pad_end.astype(jnp.int32), dest, h2, cap)
    yb = _experts(blk_expert, n_used, xs, moe_w_gate[0], moe_w_up[0], moe_w_down[0])
    out = _combine(dest, route, x1, mod3, norm_post_ffn[0][None, :], yb, L)
    return out.reshape(b, L, d)
```

```python
import functools
import math

import jax
import jax.numpy as jnp
from jax import lax
from jax.experimental import pallas as pl
from jax.experimental.pallas import tpu as pltpu

F32 = jnp.float32
BF16 = jnp.bfloat16
HIGHEST = lax.Precision.HIGHEST

LANES = 128
SUBLANES = 8

EPS = 1e-6
CHUNK = 128
GRID_W = 64
RET_HEADS = 4
RET_DK = 128
ROPE_BASE = 10000.0
SSD_HEADS = 8
SSD_HEADDIM = 64
SSD_GROUPS = 2
SSD_STATE = 128
SSD_WIDTH = SSD_HEADS * SSD_HEADDIM
SSD_CONV = 5
SSD_PAIRS = SSD_WIDTH // LANES
MOE_GROUPS = 4
EXPERTS_PER_GROUP = 8
N_EXPERTS = MOE_GROUPS * EXPERTS_PER_GROUP
TOP_K = 2
CONV_HALO = SUBLANES

TM_PROJ = 512
TM_OUT = 512
TM_ROUTE_SUB = 512
TD_DISPATCH = 512
MB_EXPERT = 512
TF_COMBINE = 512
DMA_UNROLL = 8
DISPATCH_RING = 3
SSD_UNROLL = 2
RET_UNROLL = 2
NEG_BIG = -1e30


def _silu(v):
    return v * jax.nn.sigmoid(v)


def _dot(a, b):
    return jnp.dot(a, b, preferred_element_type=F32)


def _dot_tn(a, b):
    return lax.dot_general(a, b, (((0,), (0,)), ((), ())), preferred_element_type=F32)


def _dot_nt(a, b):
    return lax.dot_general(a, b, (((1,), (1,)), ((), ())), preferred_element_type=F32)


def _mod_kernel(c_ref, w_ref, b_ref, o_ref):
    a = _silu(c_ref[...])
    o_ref[...] = jnp.dot(a, w_ref[...], precision=HIGHEST, preferred_element_type=F32) + b_ref[...]


def _modulation(c_all, w_mod, b_mod):
    rows, d = c_all.shape
    n = w_mod.shape[1]
    return pl.pallas_call(
        _mod_kernel,
        grid=(n // d,),
        in_specs=[
            pl.BlockSpec((rows, d), lambda j: (0, 0)),
            pl.BlockSpec((d, d), lambda j: (0, j)),
            pl.BlockSpec((1, d), lambda j: (0, j)),
        ],
        out_specs=pl.BlockSpec((rows, d), lambda j: (0, j)),
        out_shape=jax.ShapeDtypeStruct((rows, n), F32),
        name="modulation",
    )(c_all, w_mod, b_mod.reshape(1, n))


def _norm_mod(x, nw, sc, sh):
    ms = jnp.mean(x * x, axis=-1, keepdims=True)
    return (x * lax.rsqrt(ms + EPS)) * nw * (1.0 + sc) + sh


def _rope(t, cos, sin_signed, first_half):
    width = t.shape[-1]
    quarter = RET_DK // 4
    swapped = jnp.where(first_half, pltpu.roll(t, width - quarter, 1), pltpu.roll(t, quarter, 1))
    return t * cos + swapped * sin_signed


def _inproj_kernel(x_ref, sh_ref, sc_ref, nw_ref, wqk_ref, wvgz_ref, wxbc_ref, wdt_ref, cos_ref, sin_ref,
                   q_ref, k_ref, v_ref, g_ref, z_ref, xbc_ref, dt_ref):
    hb = _norm_mod(x_ref[0], nw_ref[...], sc_ref[0], sh_ref[0]).astype(BF16)
    rw = q_ref.shape[-1]
    qk = _dot(hb, wqk_ref[...])
    cos = cos_ref[...]
    sin = sin_ref[...]
    lane = lax.broadcasted_iota(jnp.int32, cos.shape, 1)
    first_half = (lane % (RET_DK // 2)) < (RET_DK // 4)
    q_ref[0] = _rope(qk[:, :rw], cos, sin, first_half).astype(BF16)
    k_ref[0] = (_rope(qk[:, rw:], cos, sin, first_half) * (RET_DK ** -0.5)).astype(BF16)
    vgz = _dot(hb, wvgz_ref[...])
    v_ref[0] = vgz[:, :rw].astype(BF16)
    g_ref[0] = vgz[:, rw:2 * rw].astype(BF16)
    z_ref[0] = vgz[:, 2 * rw:].astype(BF16)
    xbc_ref[0] = _dot(hb, wxbc_ref[...]).astype(BF16)
    dt_ref[0] = _dot(hb, wdt_ref[...])


def _inproj_ctx_kernel(x_ref, sh_ref, sc_ref, nw_ref, wk_ref, wv_ref, wxbc_ref, wdt_ref,
                       k_ref, v_ref, xbc_ref, dt_ref):
    hb = _norm_mod(x_ref[0], nw_ref[...], sc_ref[0], sh_ref[0]).astype(BF16)
    k_ref[0] = (_dot(hb, wk_ref[...]) * (RET_DK ** -0.5)).astype(BF16)
    v_ref[0] = _dot(hb, wv_ref[...]).astype(BF16)
    xbc_ref[0] = _dot(hb, wxbc_ref[...]).astype(BF16)
    dt_ref[0] = _dot(hb, wdt_ref[...])


def _const_spec(shape):
    nd = len(shape)
    return pl.BlockSpec(shape, lambda *_: (0,) * nd)


def _inproj(x, mod3, nw, wqk, wvgz, wxbc, wdt, cos_t, sin_t):
    b, L, d = x.shape
    tm = min(TM_PROJ, L)
    rw = wqk.shape[1] // 2
    tok = lambda w: pl.BlockSpec((1, tm, w), lambda i, j: (i, j, 0))
    out_bf = lambda w: jax.ShapeDtypeStruct((b, L, w), BF16)
    return pl.pallas_call(
        _inproj_kernel,
        grid=(b, L // tm),
        in_specs=[
            tok(d),
            pl.BlockSpec((1, 1, d), lambda i, j: (i, 0, 0)),
            pl.BlockSpec((1, 1, d), lambda i, j: (i, 0, 1)),
            _const_spec((1, d)),
            _const_spec(wqk.shape), _const_spec(wvgz.shape), _const_spec(wxbc.shape), _const_spec(wdt.shape),
            pl.BlockSpec((tm, rw), lambda i, j: (j, 0)),
            pl.BlockSpec((tm, rw), lambda i, j: (j, 0)),
        ],
        out_specs=[tok(rw), tok(rw), tok(rw), tok(rw), tok(rw), tok(wxbc.shape[1]), tok(LANES)],
        out_shape=[out_bf(rw), out_bf(rw), out_bf(rw), out_bf(rw), out_bf(rw), out_bf(wxbc.shape[1]),
                   jax.ShapeDtypeStruct((b, L, LANES), F32)],
        compiler_params=pltpu.CompilerParams(vmem_limit_bytes=48 * 1024 * 1024),
        name="inproj",
    )(x, mod3, mod3, nw, wqk, wvgz, wxbc, wdt, cos_t, sin_t)


def _inproj_ctx(ctx, mod3, ctx_row, nw, wk, wv, wxbc, wdt):
    b, L, d = ctx.shape
    tm = min(TM_PROJ, L)
    rw = wk.shape[1]
    tok = lambda w: pl.BlockSpec((1, tm, w), lambda i, j: (i, j, 0))
    out_bf = lambda w: jax.ShapeDtypeStruct((b, L, w), BF16)
    return pl.pallas_call(
        _inproj_ctx_kernel,
        grid=(b, L // tm),
        in_specs=[
            tok(d),
            pl.BlockSpec((1, 1, d), lambda i, j: (ctx_row, 0, 0)),
            pl.BlockSpec((1, 1, d), lambda i, j: (ctx_row, 0, 1)),
            _const_spec((1, d)),
            _const_spec(wk.shape), _const_spec(wv.shape), _const_spec(wxbc.shape), _const_spec(wdt.shape),
        ],
        out_specs=[tok(rw), tok(rw), tok(wxbc.shape[1]), tok(LANES)],
        out_shape=[out_bf(rw), out_bf(rw), out_bf(wxbc.shape[1]), jax.ShapeDtypeStruct((b, L, LANES), F32)],
        compiler_params=pltpu.CompilerParams(vmem_limit_bytes=48 * 1024 * 1024),
        name="inproj_ctx",
    )(ctx, mod3, mod3, nw, wk, wv, wxbc, wdt)


def _ssd_kernel(xbc_ref, z_ref, dt_ref, xbcc_ref, dtc_ref, cw_ref, cb_ref, dtb_ref, alog_ref, dsk_ref, nw_ref,
                y_ref,
                xpad, xpadc, u, uc, dtv, dav, dtcv, dacv, sf_scr):
    L = xbc_ref.shape[1]
    Lc = xbcc_ref.shape[1]
    nch = L // CHUNK
    nchc = Lc // CHUNK
    win = CHUNK + 2 * CONV_HALO
    nconv = xbc_ref.shape[2]
    nh = SSD_HEADS

    def conv_pass(src_ref, pad_ref, dst_ref, n_chunks, length):
        zeros = jnp.zeros((CONV_HALO, nconv), F32)
        pad_ref[0:CONV_HALO, :] = zeros
        pad_ref[CONV_HALO + length:2 * CONV_HALO + length, :] = zeros
        pad_ref[CONV_HALO:CONV_HALO + length, :] = src_ref[0].astype(F32)

        def chunk(c, carry):
            base = pl.multiple_of(c * CHUNK, CHUNK)
            for cb_i in range(nconv // LANES):
                cols = slice(cb_i * LANES, (cb_i + 1) * LANES)
                w = pad_ref[pl.ds(base, win), cols]
                acc = cb_ref[:, cols] + w[CONV_HALO:CONV_HALO + CHUNK] * cw_ref[SSD_CONV // 2:SSD_CONV // 2 + 1, cols]
                for j in range(SSD_CONV):
                    if j == SSD_CONV // 2:
                        continue
                    shifted = pltpu.roll(w, (SSD_CONV // 2 - j) % win, 0)
                    acc = acc + shifted[CONV_HALO:CONV_HALO + CHUNK] * cw_ref[j:j + 1, cols]
                dst_ref[pl.ds(base, CHUNK), cols] = _silu(acc).astype(BF16)
            return carry

        lax.fori_loop(0, n_chunks, chunk, 0)

    conv_pass(xbcc_ref, xpadc, uc, nchc, Lc)
    conv_pass(xbc_ref, xpad, u, nch, L)

    a_neg = -jnp.exp(alog_ref[...])
    dtv[...] = jax.nn.softplus(dt_ref[0] + dtb_ref[...])
    dav[...] = dtv[...] * a_neg
    dtcv[...] = jax.nn.softplus(dtc_ref[0] + dtb_ref[...])
    dacv[...] = dtcv[...] * a_neg

    row_i = lax.broadcasted_iota(jnp.int32, (CHUNK, CHUNK), 0)
    col_i = lax.broadcasted_iota(jnp.int32, (CHUNK, CHUNK), 1)
    causal = col_i <= row_i
    tri = causal.astype(BF16)
    tri_t = (row_i <= col_i).astype(BF16)
    lo_half = col_i < SSD_HEADDIM
    fwd_lane = col_i < nh
    head_of = lax.broadcasted_iota(jnp.int32, (CHUNK, SSD_WIDTH), 1) // SSD_HEADDIM
    src_col = lax.broadcasted_iota(jnp.int32, (CHUNK, SSD_WIDTH), 0)
    exp_f = (head_of == src_col).astype(BF16)
    exp_b = (head_of == src_col - nh).astype(BF16)
    exp_fb = jnp.concatenate([exp_f, exp_b], axis=1)

    def split3(v):
        hi = v.astype(BF16)
        r1 = v - hi.astype(F32)
        mid = r1.astype(BF16)
        return hi, mid, (r1 - mid.astype(F32)).astype(BF16)

    def times_onehot(v, m):
        hi, mid, lo = split3(v)
        return _dot(hi, m) + _dot(mid, m) + _dot(lo, m)

    def onehot_times(m, v):
        hi, mid, lo = split3(v)
        return _dot(m, hi) + _dot(m, mid) + _dot(m, lo)

    def colb(mat, r):
        return jnp.broadcast_to(mat[:, r:r + 1], (CHUNK, CHUNK))

    def pair_sel(a, b_):
        return jnp.where(lo_half, a, b_)

    def chunk_scalars(dt_s, da_s, base):
        dt = dt_s[pl.ds(base, CHUNK), :]
        da = da_s[pl.ds(base, CHUNK), :]
        acol = onehot_times(tri, da)
        return dt, da, acol, acol - da

    def state_update(u_ref, base, dt, acol, ecol, s_old, backward):
        last = acol[CHUNK - 1:CHUNK, :]
        wgt = jnp.exp(ecol) * dt if backward else jnp.exp(last - acol) * dt
        small = jnp.concatenate([wgt, jnp.broadcast_to(jnp.exp(last), (SUBLANES, LANES))], axis=0)
        wide = times_onehot(small, exp_b if backward else exp_f)
        xw = (u_ref[pl.ds(base, CHUNK), 0:SSD_WIDTH].astype(F32) * wide[:CHUNK]).astype(BF16)
        dec = wide[CHUNK:CHUNK + 1]
        new = []
        for g in range(SSD_GROUPS):
            gl = slice(g * 2 * LANES, (g + 1) * 2 * LANES)
            bm = u_ref[pl.ds(base, CHUNK), SSD_WIDTH + g * SSD_STATE:SSD_WIDTH + (g + 1) * SSD_STATE]
            new.append(dec[:, gl] * s_old[g] + _dot_tn(bm, xw[:, gl]))
        return new

    def ctx_update(c, s, backward):
        dt, _, acol, ecol = chunk_scalars(dtcv, dacv, c * CHUNK)
        return state_update(uc, c * CHUNK, dt, acol, ecol, s, backward)

    zero_state = [jnp.zeros((SSD_STATE, 2 * LANES), F32) for _ in range(SSD_GROUPS)]

    s = zero_state
    for c in range(nchc):
        s = ctx_update(c, s, False)
    s_f0 = s
    s = zero_state
    for c in reversed(range(nchc)):
        s = ctx_update(c, s, True)
    s_b0 = s

    def fwd(c, s_old):
        base = pl.multiple_of(c * CHUNK, CHUNK)
        for g in range(SSD_GROUPS):
            sf_scr[c, g] = s_old[g]
        dt, _, acol, ecol = chunk_scalars(dtv, dav, base)
        return tuple(state_update(u, base, dt, acol, ecol, s_old, False))

    lax.fori_loop(0, nch, fwd, tuple(s_f0), unroll=SSD_UNROLL)

    def bwd(i, s_b):
        c = nch - 1 - i
        base = pl.multiple_of(c * CHUNK, CHUNK)
        dt, da, acol, ecol = chunk_scalars(dtv, dav, base)
        da_t = da.T
        dt_t = dt.T
        arow = times_onehot(da_t, tri_t)
        erow = arow - da_t
        last = acol[CHUNK - 1:CHUNK, :]
        scale = times_onehot(jnp.where(fwd_lane, jnp.exp(acol), jnp.exp(last - ecol)), exp_fb)
        ys = []
        for g in range(SSD_GROUPS):
            bm = u[pl.ds(base, CHUNK), SSD_WIDTH + g * SSD_STATE:SSD_WIDTH + (g + 1) * SSD_STATE]
            cm = u[pl.ds(base, CHUNK), SSD_WIDTH + (SSD_GROUPS + g) * SSD_STATE:SSD_WIDTH + (SSD_GROUPS + g + 1) * SSD_STATE]
            cbm = _dot_nt(cm, bm)
            cs_f = _dot(cm, sf_scr[c, g].astype(BF16))
            cs_b = _dot(cm, s_b[g].astype(BF16))
            for pp in range(SSD_PAIRS // SSD_GROUPS):
                p = g * (SSD_PAIRS // SSD_GROUPS) + pp
                xs_b = u[pl.ds(base, CHUNK), p * LANES:(p + 1) * LANES]
                y_h = []
                for hh in range(2):
                    r = 2 * p + hh
                    arg = jnp.where(causal, colb(acol, r) - arow[r:r + 1, :],
                                    erow[nh + r:nh + r + 1, :] - colb(ecol, nh + r))
                    coef = jnp.where(causal, dt_t[r:r + 1, :], dt_t[nh + r:nh + r + 1, :])
                    gm = (cbm * (jnp.exp(arg) * coef)).astype(BF16)
                    y_h.append(_dot(gm, xs_b))
                sl = slice(pp * LANES, (pp + 1) * LANES)
                wl = slice(p * LANES, (p + 1) * LANES)
                wlb = slice(SSD_WIDTH + p * LANES, SSD_WIDTH + (p + 1) * LANES)
                ys.append(pair_sel(y_h[0], y_h[1]) + cs_f[:, sl] * scale[:, wl] + cs_b[:, sl] * scale[:, wlb]
                          + dsk_ref[:, wl] * xs_b.astype(F32))
        y = jnp.concatenate(ys, axis=1)
        y = y * _silu(z_ref[0, pl.ds(base, CHUNK), :].astype(F32))
        ms = jnp.mean(y * y, axis=-1, keepdims=True)
        y_ref[0, pl.ds(base, CHUNK), :] = ((y * lax.rsqrt(ms + EPS)) * nw_ref[...]).astype(BF16)
        return tuple(state_update(u, base, dt, acol, ecol, s_b, True))

    lax.fori_loop(0, nch, bwd, tuple(s_b0), unroll=SSD_UNROLL)


def _ssd(xbc, z, dt, xbcc, dtc, conv_w8, conv_b, dt_bias, a_log, d_skip, norm_w):
    b, L, nconv = xbc.shape
    Lc = xbcc.shape[1]
    nch = L // CHUNK
    per_b = lambda n, w: pl.BlockSpec((1, n, w), lambda i: (i, 0, 0))
    return pl.pallas_call(
        _ssd_kernel,
        grid=(b,),
        in_specs=[
            per_b(L, nconv), per_b(L, SSD_WIDTH), per_b(L, LANES), per_b(Lc, nconv), per_b(Lc, LANES),
            _const_spec(conv_w8.shape), _const_spec(conv_b.shape), _const_spec(dt_bias.shape),
            _const_spec(a_log.shape), _const_spec(d_skip.shape), _const_spec(norm_w.shape),
        ],
        out_specs=per_b(L, SSD_WIDTH),
        out_shape=jax.ShapeDtypeStruct((b, L, SSD_WIDTH), BF16),
        scratch_shapes=[
            pltpu.VMEM((L + 2 * CONV_HALO, nconv), F32),
            pltpu.VMEM((Lc + 2 * CONV_HALO, nconv), F32),
            pltpu.VMEM((L, nconv), BF16),
            pltpu.VMEM((Lc, nconv), BF16),
            pltpu.VMEM((L, LANES), F32), pltpu.VMEM((L, LANES), F32),
            pltpu.VMEM((Lc, LANES), F32), pltpu.VMEM((Lc, LANES), F32),
            pltpu.VMEM((nch, SSD_GROUPS, SSD_STATE, 2 * LANES), F32),
        ],
        compiler_params=pltpu.CompilerParams(vmem_limit_bytes=56 * 1024 * 1024),
        name="ssd",
    )(xbc, z, dt, xbcc, dtc, conv_w8, conv_b, dt_bias, a_log, d_skip, norm_w)


def _ret_kernel(q_ref, k_ref, v_ref, g_ref, kc_ref, vc_ref, df_ref, db_ref, gn_ref, y_ref, sf_scr):
    L = q_ref.shape[1]
    Lc = kc_ref.shape[1]
    nch = L // CHUNK
    dk = RET_DK
    row_i = lax.broadcasted_iota(jnp.int32, (CHUNK, dk), 0).astype(F32)
    col_i = lax.broadcasted_iota(jnp.int32, (CHUNK, dk), 1).astype(F32)
    rel = row_i - col_i
    crow = lax.broadcasted_iota(jnp.int32, (Lc, dk), 0).astype(F32)

    heads = []
    s_f0 = []
    s_b0 = []
    for h in range(RET_HEADS):
        cols = slice(h * dk, (h + 1) * dk)
        lg_f = -jnp.exp(df_ref[:, cols])
        lg_b = -jnp.exp(db_ref[:, cols])
        heads.append(dict(
            cols=cols,
            dmat=jnp.where(rel >= 0, jnp.exp(jnp.maximum(rel, 0.0) * lg_f), jnp.exp(jnp.maximum(-rel, 0.0) * lg_b)),
            dq_f=jnp.exp((row_i + 1.0) * lg_f),
            dq_b=jnp.exp((CHUNK - row_i) * lg_b),
            dk_f=jnp.exp((CHUNK - 1.0 - row_i) * lg_f),
            dk_b=jnp.exp(row_i * lg_b),
            dc_f=jnp.exp(CHUNK * lg_f),
            dc_b=jnp.exp(CHUNK * lg_b),
        ))
        kc = kc_ref[0, :, cols].astype(F32)
        vc = vc_ref[0, :, cols]
        s_f0.append(_dot_tn((kc * jnp.exp((Lc - 1.0 - crow) * lg_f)).astype(BF16), vc))
        s_b0.append(_dot_tn((kc * jnp.exp(crow * lg_b)).astype(BF16), vc))

    def fwd(c, s_f):
        base = pl.multiple_of(c * CHUNK, CHUNK)
        new = []
        for h, hd in enumerate(heads):
            sf_scr[c, h] = s_f[h]
            kk = k_ref[0, pl.ds(base, CHUNK), hd["cols"]].astype(F32)
            vv = v_ref[0, pl.ds(base, CHUNK), hd["cols"]]
            new.append(hd["dc_f"] * s_f[h] + _dot_tn((kk * hd["dk_f"]).astype(BF16), vv))
        return tuple(new)

    lax.fori_loop(0, nch, fwd, tuple(s_f0), unroll=RET_UNROLL)

    def bwd(i, s_bs):
        c = nch - 1 - i
        base = pl.multiple_of(c * CHUNK, CHUNK)
        new = []
        for h, hd in enumerate(heads):
            qq = q_ref[0, pl.ds(base, CHUNK), hd["cols"]]
            kk = k_ref[0, pl.ds(base, CHUNK), hd["cols"]]
            vv = v_ref[0, pl.ds(base, CHUNK), hd["cols"]]
            s_b = s_bs[h]
            scores = (_dot_nt(qq, kk) * hd["dmat"]).astype(BF16)
            y = (_dot(scores, vv)
                 + _dot(qq, sf_scr[c, h].astype(BF16)) * hd["dq_f"]
                 + _dot(qq, s_b.astype(BF16)) * hd["dq_b"])
            mu = jnp.mean(y, axis=-1, keepdims=True)
            yc = y - mu
            var = jnp.mean(yc * yc, axis=-1, keepdims=True)
            yn = (yc * lax.rsqrt(var + EPS)) * gn_ref[:, hd["cols"]]
            gate = _silu(g_ref[0, pl.ds(base, CHUNK), hd["cols"]].astype(F32))
            y_ref[0, pl.ds(base, CHUNK), hd["cols"]] = (yn * gate).astype(BF16)
            new.append(hd["dc_b"] * s_b + _dot_tn((kk.astype(F32) * hd["dk_b"]).astype(BF16), vv))
        return tuple(new)

    lax.fori_loop(0, nch, bwd, tuple(s_b0), unroll=RET_UNROLL)


def _retention(q, k, v, g, kc, vc, decay_f, decay_b, gn_w):
    b, L, w = q.shape
    Lc = kc.shape[1]
    nch = L // CHUNK
    per_b = lambda n: pl.BlockSpec((1, n, w), lambda i: (i, 0, 0))
    return pl.pallas_call(
        _ret_kernel,
        grid=(b,),
        in_specs=[per_b(L), per_b(L), per_b(L), per_b(L), per_b(Lc), per_b(Lc),
                  _const_spec((1, w)), _const_spec((1, w)), _const_spec((1, w))],
        out_specs=per_b(L),
        out_shape=jax.ShapeDtypeStruct((b, L, w), BF16),
        scratch_shapes=[
            pltpu.VMEM((nch, RET_HEADS, RET_DK, RET_DK), F32),
        ],
        compiler_params=pltpu.CompilerParams(vmem_limit_bytes=48 * 1024 * 1024),
        name="retention",
    )(q, k, v, g, kc, vc, decay_f, decay_b, gn_w)


def _outproj_router_kernel(yr_ref, ys_ref, x_ref, g1_ref, sh2_ref, sc2_ref, npost_ref, npre_ref,
                           wor_ref, wos_ref, wr_ref, br_ref, tri_ref,
                           x1_ref, h2_ref, route_ref, cnt_ref, slots_ref,
                           wcat, carry):
    i = pl.program_id(0)

    @pl.when(i == 0)
    def _():
        wr = wr_ref[...]
        hi = wr.astype(BF16)
        wcat[:, :LANES] = hi
        wcat[:, LANES:] = (wr - hi.astype(F32)).astype(BF16)
        carry[...] = jnp.zeros_like(carry)

    tm = tri_ref.shape[0]
    running = carry[...]
    for sub in range(x_ref.shape[0] // tm):
        running = _route_subtile(sub, tm, running, yr_ref, ys_ref, x_ref, g1_ref, sh2_ref, sc2_ref, npost_ref,
                                 npre_ref, wor_ref, wos_ref, br_ref, tri_ref, x1_ref, h2_ref, route_ref,
                                 slots_ref, wcat)
    carry[...] = running
    cnt_ref[...] = running


def _route_subtile(sub, tm, running, yr_ref, ys_ref, x_ref, g1_ref, sh2_ref, sc2_ref, npost_ref, npre_ref,
                   wor_ref, wos_ref, br_ref, tri_ref, x1_ref, h2_ref, route_ref, slots_ref, wcat):
    rows = slice(sub * tm, (sub + 1) * tm)
    y = _dot(yr_ref[rows, :], wor_ref[...]) + _dot(ys_ref[rows, :], wos_ref[...])
    ms = jnp.mean(y * y, axis=-1, keepdims=True)
    x1 = x_ref[rows, :] + g1_ref[0] * ((y * lax.rsqrt(ms + EPS)) * npost_ref[...])
    x1_ref[rows, :] = x1
    h2 = _norm_mod(x1, npre_ref[...], sc2_ref[0], sh2_ref[0])
    h2_ref[rows, :] = h2

    h_hi = h2.astype(BF16)
    h_lo = (h2 - h_hi.astype(F32)).astype(BF16)
    both = _dot(h_hi, wcat[...])
    lg = both[:, :LANES] + both[:, LANES:] + _dot(h_lo, wcat[:, :LANES]) + br_ref[...]

    lane = lax.broadcasted_iota(jnp.int32, (tm, LANES), 1)
    lane_f = lane.astype(F32)
    is_grp = (lane >= N_EXPERTS) & (lane < N_EXPERTS + MOE_GROUPS)
    gl = jnp.where(is_grp, lg, NEG_BIG)
    mg = jnp.max(gl, axis=-1, keepdims=True)
    grp_lane = jnp.min(jnp.where(gl == mg, lane_f, 1e9), axis=-1, keepdims=True)
    p_g = 1.0 / jnp.sum(jnp.where(is_grp, jnp.exp(gl - mg), 0.0), axis=-1, keepdims=True)
    first = (grp_lane - N_EXPERTS) * EXPERTS_PER_GROUP
    in_grp = (lane_f >= first) & (lane_f < first + EXPERTS_PER_GROUP)
    el = jnp.where(in_grp, lg, NEG_BIG)
    t1 = jnp.max(el, axis=-1, keepdims=True)
    i1 = jnp.min(jnp.where(el == t1, lane_f, 1e9), axis=-1, keepdims=True)
    el2 = jnp.where(lane_f == i1, NEG_BIG, el)
    t2 = jnp.max(el2, axis=-1, keepdims=True)
    i2 = jnp.min(jnp.where(el2 == t2, lane_f, 1e9), axis=-1, keepdims=True)
    s = jnp.exp(t2 - t1)
    w1 = p_g / (1.0 + s)
    w2 = p_g * s / (1.0 + s)

    oh1 = (lane_f == i1)
    oh2 = (lane_f == i2)
    oh = (oh1 | oh2).astype(BF16)
    before = _dot(tri_ref[...], oh) + running
    rank1 = jnp.sum(jnp.where(oh1, before, 0.0), axis=-1, keepdims=True)
    rank2 = jnp.sum(jnp.where(oh2, before, 0.0), axis=-1, keepdims=True)

    route_ref[rows, :] = jnp.where(lane == 0, w1, jnp.where(lane == 1, w2, 0.0))

    row = lax.broadcasted_iota(jnp.int32, (tm, LANES), 0)
    on_diag = (row % LANES) == lane
    per = tm // LANES
    n_sub = x_ref.shape[0] // tm
    for qi, col in enumerate((i1, i2, rank1, rank2)):
        picked = jnp.where(on_diag, col, 0.0)
        dense = jnp.sum(picked.reshape(per, LANES, LANES), axis=1).astype(jnp.int32)
        slots_ref[0, (qi * n_sub + sub) * per:(qi * n_sub + sub + 1) * per, :] = dense
    return running + jnp.sum(oh.astype(F32), axis=0, keepdims=True)


def _outproj_router(yr, ys, x2, mod3, npost, npre, wo_r, wo_s, w_router, b_router, seq_len):
    T, d = x2.shape
    tm = TM_OUT
    per_seq = seq_len // tm
    rw = yr.shape[1]
    sub = TM_ROUTE_SUB
    tri = (jnp.arange(sub)[:, None] > jnp.arange(sub)[None, :]).astype(BF16)
    tok = lambda w: pl.BlockSpec((tm, w), lambda i: (i, 0))
    modv = lambda k: pl.BlockSpec((1, 1, d), lambda i: (i // per_seq, 0, k))
    return pl.pallas_call(
        _outproj_router_kernel,
        grid=(T // tm,),
        in_specs=[
            tok(rw), tok(rw), tok(d), modv(2), modv(3), modv(4),
            _const_spec((1, d)), _const_spec((1, d)),
            _const_spec(wo_r.shape), _const_spec(wo_s.shape), _const_spec(w_router.shape), _const_spec((1, LANES)),
            _const_spec((sub, sub)),
        ],
        out_specs=[tok(d), tok(d), tok(LANES), _const_spec((1, LANES)),
                   pl.BlockSpec((1, 2 * TOP_K * (tm // LANES), LANES), lambda i: (i, 0, 0))],
        out_shape=[jax.ShapeDtypeStruct((T, d), F32), jax.ShapeDtypeStruct((T, d), F32),
                   jax.ShapeDtypeStruct((T, LANES), F32), jax.ShapeDtypeStruct((1, LANES), F32),
                   jax.ShapeDtypeStruct((T // tm, 2 * TOP_K * (tm // LANES), LANES), jnp.int32)],
        scratch_shapes=[pltpu.VMEM((d, 2 * LANES), BF16), pltpu.VMEM((1, LANES), F32)],
        compiler_params=pltpu.CompilerParams(dimension_semantics=("arbitrary",),
                                             vmem_limit_bytes=48 * 1024 * 1024),
        name="outproj_router",
    )(yr, ys, x2, mod3, mod3, mod3, npost, npre, wo_r, wo_s, w_router, b_router, tri)


def _dispatch_kernel(pad_end_ref, dest_ref, h_hbm, xs_hbm, zbuf, hbuf, sem, zsem, in_sem):
    i = pl.program_id(0)
    nt = pl.num_programs(0)
    td = hbuf.shape[1]
    mb = zbuf.shape[0]

    @pl.when(i == 0)
    def _():
        zbuf[...] = jnp.zeros_like(zbuf)

        def fill(e):
            return pltpu.make_async_copy(zbuf, xs_hbm.at[pl.ds(pl.multiple_of(pad_end_ref[e] - mb, mb), mb)], zsem)

        def has_rows(e):
            return pad_end_ref[e] > jnp.where(e == 0, 0, pad_end_ref[jnp.maximum(e - 1, 0)])

        def start(e, carry):
            @pl.when(has_rows(e))
            def _():
                fill(e).start()
            return carry

        def wait(e, carry):
            @pl.when(has_rows(e))
            def _():
                fill(e).wait()
            return carry

        def tail_fill(blk):
            return pltpu.make_async_copy(zbuf, xs_hbm.at[pl.ds(pl.multiple_of(blk * mb, mb), mb)], zsem)

        def tail_start(blk, carry):
            @pl.when(blk * mb >= pad_end_ref[N_EXPERTS - 1])
            def _():
                tail_fill(blk).start()
            return carry

        def tail_wait(blk, carry):
            @pl.when(blk * mb >= pad_end_ref[N_EXPERTS - 1])
            def _():
                tail_fill(blk).wait()
            return carry

        n_blocks = xs_hbm.shape[0] // mb
        lax.fori_loop(0, N_EXPERTS, start, 0)
        lax.fori_loop(0, n_blocks, tail_start, 0)
        lax.fori_loop(0, N_EXPERTS, wait, 0)
        lax.fori_loop(0, n_blocks, tail_wait, 0)

    def fetch(t):
        return pltpu.make_async_copy(h_hbm.at[pl.ds(pl.multiple_of(t * td, td), td)], hbuf.at[t % DISPATCH_RING],
                                     in_sem.at[t % DISPATCH_RING])

    @pl.when(i == 0)
    def _():
        fetch(i).start()

    @pl.when(i + 1 < nt)
    def _():
        fetch(i + 1).start()

    fetch(i).wait()
    for slot in range(DISPATCH_RING):
        @pl.when(i % DISPATCH_RING == slot)
        def _(slot=slot):
            _issue_row_copies(dest_ref, td, lambda kk, j, d: pltpu.make_async_copy(
                hbuf.at[slot, pl.ds(j, 1)], xs_hbm.at[pl.ds(d, 1)], sem.at[slot]))

    def drain(t):
        for kk in range(TOP_K):
            pltpu.make_async_copy(hbuf.at[t % DISPATCH_RING], xs_hbm.at[pl.ds(0, td)],
                                  sem.at[t % DISPATCH_RING]).wait()

    @pl.when(i > 0)
    def _():
        drain(i - 1)

    @pl.when(i == nt - 1)
    def _():
        drain(i)


def _issue_row_copies(dest_ref, n_tok, row_copy):
    def issue(j, carry):
        for kk in range(TOP_K):
            row_copy(kk, j, dest_ref[0, 0, kk * n_tok + j]).start()
        return carry

    lax.fori_loop(0, n_tok, issue, 0, unroll=DMA_UNROLL)


def _dispatch(pad_end, dest, h2, cap):
    T, d = h2.shape
    td = TD_DISPATCH
    nt = T // td
    grid_spec = pltpu.PrefetchScalarGridSpec(
        num_scalar_prefetch=1,
        grid=(nt,),
        in_specs=[
            pl.BlockSpec((1, 1, TOP_K * td), lambda i, pe: (i, 0, 0), memory_space=pltpu.SMEM),
            pl.BlockSpec(memory_space=pl.ANY),
        ],
        out_specs=pl.BlockSpec(memory_space=pl.ANY),
        scratch_shapes=[pltpu.VMEM((MB_EXPERT, d), F32), pltpu.VMEM((DISPATCH_RING, td, d), F32),
                        pltpu.SemaphoreType.DMA((DISPATCH_RING,)), pltpu.SemaphoreType.DMA(()),
                        pltpu.SemaphoreType.DMA((DISPATCH_RING,))],
    )
    return pl.pallas_call(
        _dispatch_kernel,
        grid_spec=grid_spec,
        out_shape=jax.ShapeDtypeStruct((cap, d), F32),
        compiler_params=pltpu.CompilerParams(dimension_semantics=("arbitrary",)),
        name="dispatch",
    )(pad_end, dest, h2)


def _expert_kernel(be_ref, nused_ref, xs_ref, wg_ref, wu_ref, wd_ref, y_ref, wg_b, wu_b, wd_b):
    i = pl.program_id(0)
    prev = be_ref[jnp.maximum(i - 1, 0)]

    @pl.when((i == 0) | (be_ref[i] != prev))
    def _():
        wg_b[...] = wg_ref[0].astype(BF16)
        wu_b[...] = wu_ref[0].astype(BF16)
        wd_b[...] = wd_ref[0].astype(BF16)

    @pl.when(i < nused_ref[0])
    def _():
        xb = xs_ref[...].astype(BF16)
        hid = (_silu(_dot(xb, wg_b[...])) * _dot(xb, wu_b[...])).astype(BF16)
        y_ref[...] = _dot(hid, wd_b[...])

    @pl.when(i >= nused_ref[0])
    def _():
        y_ref[...] = jnp.zeros_like(y_ref)


def _experts(blk_expert, n_used, xs, w_gate, w_up, w_down):
    cap, d = xs.shape
    de = w_gate.shape[2]
    mb = MB_EXPERT
    grid_spec = pltpu.PrefetchScalarGridSpec(
        num_scalar_prefetch=2,
        grid=(cap // mb,),
        in_specs=[
            pl.BlockSpec((mb, d), lambda i, be, nu: (jnp.minimum(i, nu[0] - 1), 0)),
            pl.BlockSpec((1, d, de), lambda i, be, nu: (be[i], 0, 0)),
            pl.BlockSpec((1, d, de), lambda i, be, nu: (be[i], 0, 0)),
            pl.BlockSpec((1, de, d), lambda i, be, nu: (be[i], 0, 0)),
        ],
        out_specs=pl.BlockSpec((mb, d), lambda i, be, nu: (i, 0)),
        scratch_shapes=[pltpu.VMEM((d, de), BF16), pltpu.VMEM((d, de), BF16), pltpu.VMEM((de, d), BF16)],
    )
    return pl.pallas_call(
        _expert_kernel,
        grid_spec=grid_spec,
        out_shape=jax.ShapeDtypeStruct((cap, d), F32),
        compiler_params=pltpu.CompilerParams(dimension_semantics=("arbitrary",),
                                             vmem_limit_bytes=48 * 1024 * 1024),
        name="experts",
    )(blk_expert, n_used, xs, w_gate, w_up, w_down)


def _combine_kernel(dest_ref, dest_next_ref, route_ref, x1_ref, g2_ref, nw_ref, yb_hbm, o_ref, buf, sem):
    i = pl.program_id(0)
    nt = pl.num_programs(0)
    tf = x1_ref.shape[0]
    stage = i % 2

    def gather(refs, st):
        _issue_row_copies(refs, tf, lambda kk, j, d: pltpu.make_async_copy(
            yb_hbm.at[pl.ds(d, 1)], buf.at[st, kk, pl.ds(j, 1)], sem.at[st]))

    @pl.when(i == 0)
    def _():
        gather(dest_ref, 0)

    for st in range(2):
        @pl.when((i + 1 < nt) & (stage != st))
        def _(st=st):
            gather(dest_next_ref, st)

    for kk in range(TOP_K):
        pltpu.make_async_copy(yb_hbm.at[pl.ds(0, tf)], buf.at[stage, kk], sem.at[stage]).wait()

    route = route_ref[...]
    out = route[:, 0:1] * buf[stage, 0] + route[:, 1:2] * buf[stage, 1]
    ms = jnp.mean(out * out, axis=-1, keepdims=True)
    o_ref[...] = x1_ref[...] + g2_ref[0] * ((out * lax.rsqrt(ms + EPS)) * nw_ref[...])


def _combine(dest, route, x1, mod3, nw, yb, seq_len):
    T, d = x1.shape
    tf = TF_COMBINE
    nt = T // tf
    per_seq = seq_len // tf
    slot_spec = functools.partial(pl.BlockSpec, (1, 1, TOP_K * tf), memory_space=pltpu.SMEM)
    return pl.pallas_call(
        _combine_kernel,
        grid=(nt,),
        in_specs=[
            slot_spec(lambda i: (i, 0, 0)),
            slot_spec(lambda i: (jnp.minimum(i + 1, nt - 1), 0, 0)),
            pl.BlockSpec((tf, LANES), lambda i: (i, 0)),
            pl.BlockSpec((tf, d), lambda i: (i, 0)),
            pl.BlockSpec((1, 1, d), lambda i: (i // per_seq, 0, 5)),
            _const_spec((1, d)),
            pl.BlockSpec(memory_space=pl.ANY),
        ],
        out_specs=pl.BlockSpec((tf, d), lambda i: (i, 0)),
        out_shape=jax.ShapeDtypeStruct((T, d), F32),
        scratch_shapes=[pltpu.VMEM((2, TOP_K, tf, d), F32), pltpu.SemaphoreType.DMA((2,))],
        compiler_params=pltpu.CompilerParams(dimension_semantics=("arbitrary",),
                                             vmem_limit_bytes=48 * 1024 * 1024),
        name="combine",
    )(dest, dest, route, x1, mod3, nw, yb)


def _rope_tables(L, n_heads):
    quarter = RET_DK // 4
    freqs = ROPE_BASE ** (-jnp.arange(quarter, dtype=F32) / quarter)
    t = jnp.arange(L)
    ang_r = (t // GRID_W).astype(F32)[:, None] * freqs
    ang_c = (t % GRID_W).astype(F32)[:, None] * freqs
    cos = jnp.concatenate([jnp.cos(ang_r)] * 2 + [jnp.cos(ang_c)] * 2, axis=-1)
    sin = jnp.concatenate([-jnp.sin(ang_r), jnp.sin(ang_r), -jnp.sin(ang_c), jnp.sin(ang_c)], axis=-1)
    return jnp.tile(cos, (1, n_heads)), jnp.tile(sin, (1, n_heads))


def _lane_pad(v, width=LANES):
    return jnp.pad(v, [(0, 0)] * (v.ndim - 1) + [(0, width - v.shape[-1])])


def kernel(x, c, ctx, c_ctx, w_mod, b_mod, norm_pre_mix, norm_post_mix, norm_pre_ffn, norm_post_ffn, w_in, w_out, ret_decay_f, ret_decay_b, ret_gn_w, ssd_conv_w, ssd_conv_b, ssd_dt_bias_f, ssd_dt_bias_b, ssd_a_log_f, ssd_a_log_b, ssd_d, ssd_norm_w, moe_w_rg, moe_b_rg, moe_w_re, moe_b_re, moe_w_gate, moe_w_up, moe_w_down):
    b, L, d = x.shape
    assert w_mod.shape[0] == 1, "single layer: context outputs are never needed"
    assert TM_OUT == TD_DISPATCH == TF_COMBINE, "router, dispatch and combine share one slot-row layout"
    rw = RET_HEADS * RET_DK
    nconv = SSD_WIDTH + 2 * SSD_GROUPS * SSD_STATE
    T = b * L

    mod_rows = -(-(b + 1) // SUBLANES) * SUBLANES
    c_all = jnp.zeros((mod_rows, d), F32).at[:b].set(c).at[b].set(c_ctx)
    mod3 = _modulation(c_all, w_mod[0], b_mod[0]).reshape(mod_rows, 1, 6 * d)

    wi = w_in[0]
    o = 0
    wq = wi[:, o:o + rw]; o += rw
    wk = wi[:, o:o + rw]; o += rw
    wv = wi[:, o:o + rw]; o += rw
    wg = wi[:, o:o + rw]; o += rw
    wz = wi[:, o:o + SSD_WIDTH]; o += SSD_WIDTH
    wxbc = wi[:, o:o + nconv].astype(BF16); o += nconv
    wdt = _lane_pad(wi[:, o:o + 2 * SSD_HEADS]).astype(BF16)
    wqk = jnp.concatenate([wq, wk], axis=1).astype(BF16)
    wvgz = jnp.concatenate([wv, wg, wz], axis=1).astype(BF16)
    cos_t, sin_t = _rope_tables(L, RET_HEADS)
    nw1 = norm_pre_mix[0].reshape(1, d)

    q, k, v, g, z, xbc, dt = _inproj(x, mod3, nw1, wqk, wvgz, wxbc, wdt, cos_t, sin_t)
    kc, vc, xbcc, dtc = _inproj_ctx(ctx, mod3, b, nw1, wk.astype(BF16), wv.astype(BF16), wxbc, wdt)

    conv_w8 = jnp.pad(ssd_conv_w[0], ((0, SUBLANES - SSD_CONV), (0, 0)))
    dt_bias = _lane_pad(jnp.concatenate([ssd_dt_bias_f[0], ssd_dt_bias_b[0]])[None, :])
    a_log = _lane_pad(jnp.concatenate([ssd_a_log_f[0], ssd_a_log_b[0]])[None, :])
    d_skip = jnp.repeat(ssd_d[0], SSD_HEADDIM)[None, :]
    ys = _ssd(xbc, z, dt, xbcc, dtc, conv_w8, ssd_conv_b[0][None, :], dt_bias, a_log, d_skip,
              ssd_norm_w[0][None, :])

    yr = _retention(q, k, v, g, kc, vc,
                    jnp.repeat(ret_decay_f[0], RET_DK)[None, :], jnp.repeat(ret_decay_b[0], RET_DK)[None, :],
                    ret_gn_w[0][None, :])

    wo = w_out[0].astype(BF16)
    w_router = _lane_pad(jnp.concatenate(
        [jnp.transpose(moe_w_re[0], (1, 0, 2)).reshape(d, N_EXPERTS), moe_w_rg[0]], axis=1))
    b_router = _lane_pad(jnp.concatenate([moe_b_re[0].reshape(-1), moe_b_rg[0]])[None, :])
    x1, h2, route, counts, slots = _outproj_router(
        yr.reshape(T, rw), ys.reshape(T, SSD_WIDTH), x.reshape(T, d), mod3,
        norm_post_mix[0][None, :], norm_pre_ffn[0][None, :], wo[:rw], wo[rw:], w_router, b_router, L)

    mb = MB_EXPERT
    n_blocks = -(-(T * TOP_K + N_EXPERTS * (mb - 1)) // mb)
    cap = n_blocks * mb
    cnt = counts[0, :N_EXPERTS].astype(jnp.int32)
    padded = (cnt + mb - 1) // mb * mb
    pad_end = jnp.cumsum(padded)
    pad_start = pad_end - padded
    half = slots.shape[1] // 2
    dest = slots[:, half:]
    for e in range(N_EXPERTS):
        dest = dest + jnp.where(slots[:, :half] == e, pad_start[e], 0)
    blk_start = jnp.arange(n_blocks, dtype=jnp.int32) * mb
    blk_expert = jnp.minimum(jnp.sum((pad_end[None, :] <= blk_start[:, None]).astype(jnp.int32), axis=1),
                             N_EXPERTS - 1)
    n_used = (pad_end[-1:] // mb).astype(jnp.int32)

    dest = dest.reshape(dest.shape[0], 1, -1)
    xs = _dispatch(pad_end.astype(jnp.int32), dest, h2, cap)
    yb = _experts(blk_expert, n_used, xs, moe_w_gate[0], moe_w_up[0], moe_w_down[0])
    out = _combine(dest, route, x1, mod3, norm_post_ffn[0][None, :], yb, L)
    return out.reshape(b, L, d)
```

```python
import functools
import math

import jax
import jax.numpy as jnp
from jax import lax
from jax.experimental import pallas as pl
from jax.experimental.pallas import tpu as pltpu

F32 = jnp.float32
BF16 = jnp.bfloat16
HIGHEST = lax.Precision.HIGHEST

LANES = 128
SUBLANES = 8

EPS = 1e-6
CHUNK = 128
GRID_W = 64
RET_HEADS = 4
RET_DK = 128
ROPE_BASE = 10000.0
SSD_HEADS = 8
SSD_HEADDIM = 64
SSD_GROUPS = 2
SSD_STATE = 128
SSD_WIDTH = SSD_HEADS * SSD_HEADDIM
SSD_CONV = 5
SSD_PAIRS = SSD_WIDTH // LANES
MOE_GROUPS = 4
EXPERTS_PER_GROUP = 8
N_EXPERTS = MOE_GROUPS * EXPERTS_PER_GROUP
TOP_K = 2
CONV_HALO = SUBLANES

TM_PROJ = 512
TM_OUT = 512
TM_ROUTE_SUB = 512
TD_DISPATCH = 512
MB_EXPERT = 512
TF_COMBINE = 512
DMA_UNROLL = 8
DISPATCH_RING = 3
SSD_UNROLL = 2
RET_UNROLL = 2
NEG_BIG = -1e30


def _silu(v):
    return v * jax.nn.sigmoid(v)


def _dot(a, b):
    return jnp.dot(a, b, preferred_element_type=F32)


def _dot_tn(a, b):
    return lax.dot_general(a, b, (((0,), (0,)), ((), ())), preferred_element_type=F32)


def _dot_nt(a, b):
    return lax.dot_general(a, b, (((1,), (1,)), ((), ())), preferred_element_type=F32)


def _mod_kernel(c_ref, w_ref, b_ref, o_ref):
    a = _silu(c_ref[...])
    o_ref[...] = jnp.dot(a, w_ref[...], precision=HIGHEST, preferred_element_type=F32) + b_ref[...]


def _modulation(c_all, w_mod, b_mod):
    rows, d = c_all.shape
    n = w_mod.shape[1]
    return pl.pallas_call(
        _mod_kernel,
        grid=(n // d,),
        in_specs=[
            pl.BlockSpec((rows, d), lambda j: (0, 0)),
            pl.BlockSpec((d, d), lambda j: (0, j)),
            pl.BlockSpec((1, d), lambda j: (0, j)),
        ],
        out_specs=pl.BlockSpec((rows, d), lambda j: (0, j)),
        out_shape=jax.ShapeDtypeStruct((rows, n), F32),
        name="modulation",
    )(c_all, w_mod, b_mod.reshape(1, n))


def _norm_mod(x, nw, sc, sh):
    ms = jnp.mean(x * x, axis=-1, keepdims=True)
    return (x * lax.rsqrt(ms + EPS)) * nw * (1.0 + sc) + sh


def _rope(t, cos, sin_signed, first_half):
    width = t.shape[-1]
    quarter = RET_DK // 4
    swapped = jnp.where(first_half, pltpu.roll(t, width - quarter, 1), pltpu.roll(t, quarter, 1))
    return t * cos + swapped * sin_signed


def _inproj_kernel(x_ref, sh_ref, sc_ref, nw_ref, wqk_ref, wvgz_ref, wxbc_ref, wdt_ref, cos_ref, sin_ref,
                   q_ref, k_ref, v_ref, g_ref, z_ref, xbc_ref, dt_ref):
    hb = _norm_mod(x_ref[0], nw_ref[...], sc_ref[0], sh_ref[0]).astype(BF16)
    rw = q_ref.shape[-1]
    qk = _dot(hb, wqk_ref[...])
    cos = cos_ref[...]
    sin = sin_ref[...]
    lane = lax.broadcasted_iota(jnp.int32, cos.shape, 1)
    first_half = (lane % (RET_DK // 2)) < (RET_DK // 4)
    q_ref[0] = _rope(qk[:, :rw], cos, sin, first_half).astype(BF16)
    k_ref[0] = (_rope(qk[:, rw:], cos, sin, first_half) * (RET_DK ** -0.5)).astype(BF16)
    vgz = _dot(hb, wvgz_ref[...])
    v_ref[0] = vgz[:, :rw].astype(BF16)
    g_ref[0] = vgz[:, rw:2 * rw].astype(BF16)
    z_ref[0] = vgz[:, 2 * rw:].astype(BF16)
    xbc_ref[0] = _dot(hb, wxbc_ref[...]).astype(BF16)
    dt_ref[0] = _dot(hb, wdt_ref[...])


def _inproj_ctx_kernel(x_ref, sh_ref, sc_ref, nw_ref, wk_ref, wv_ref, wxbc_ref, wdt_ref,
                       k_ref, v_ref, xbc_ref, dt_ref):
    hb = _norm_mod(x_ref[0], nw_ref[...], sc_ref[0], sh_ref[0]).astype(BF16)
    k_ref[0] = (_dot(hb, wk_ref[...]) * (RET_DK ** -0.5)).astype(BF16)
    v_ref[0] = _dot(hb, wv_ref[...]).astype(BF16)
    xbc_ref[0] = _dot(hb, wxbc_ref[...]).astype(BF16)
    dt_ref[0] = _dot(hb, wdt_ref[...])


def _const_spec(shape):
    nd = len(shape)
    return pl.BlockSpec(shape, lambda *_: (0,) * nd)


def _inproj(x, mod3, nw, wqk, wvgz, wxbc, wdt, cos_t, sin_t):
    b, L, d = x.shape
    tm = min(TM_PROJ, L)
    rw = wqk.shape[1] // 2
    tok = lambda w: pl.BlockSpec((1, tm, w), lambda i, j: (i, j, 0))
    out_bf = lambda w: jax.ShapeDtypeStruct((b, L, w), BF16)
    return pl.pallas_call(
        _inproj_kernel,
        grid=(b, L // tm),
        in_specs=[
            tok(d),
            pl.BlockSpec((1, 1, d), lambda i, j: (i, 0, 0)),
            pl.BlockSpec((1, 1, d), lambda i, j: (i, 0, 1)),
            _const_spec((1, d)),
            _const_spec(wqk.shape), _const_spec(wvgz.shape), _const_spec(wxbc.shape), _const_spec(wdt.shape),
            pl.BlockSpec((tm, rw), lambda i, j: (j, 0)),
            pl.BlockSpec((tm, rw), lambda i, j: (j, 0)),
        ],
        out_specs=[tok(rw), tok(rw), tok(rw), tok(rw), tok(rw), tok(wxbc.shape[1]), tok(LANES)],
        out_shape=[out_bf(rw), out_bf(rw), out_bf(rw), out_bf(rw), out_bf(rw), out_bf(wxbc.shape[1]),
                   jax.ShapeDtypeStruct((b, L, LANES), F32)],
        compiler_params=pltpu.CompilerParams(vmem_limit_bytes=48 * 1024 * 1024),
        name="inproj",
    )(x, mod3, mod3, nw, wqk, wvgz, wxbc, wdt, cos_t, sin_t)


def _inproj_ctx(ctx, mod3, ctx_row, nw, wk, wv, wxbc, wdt):
    b, L, d = ctx.shape
    tm = min(TM_PROJ, L)
    rw = wk.shape[1]
    tok = lambda w: pl.BlockSpec((1, tm, w), lambda i, j: (i, j, 0))
    out_bf = lambda w: jax.ShapeDtypeStruct((b, L, w), BF16)
    return pl.pallas_call(
        _inproj_ctx_kernel,
        grid=(b, L // tm),
        in_specs=[
            tok(d),
            pl.BlockSpec((1, 1, d), lambda i, j: (ctx_row, 0, 0)),
            pl.BlockSpec((1, 1, d), lambda i, j: (ctx_row, 0, 1)),
            _const_spec((1, d)),
            _const_spec(wk.shape), _const_spec(wv.shape), _const_spec(wxbc.shape), _const_spec(wdt.shape),
        ],
        out_specs=[tok(rw), tok(rw), tok(wxbc.shape[1]), tok(LANES)],
        out_shape=[out_bf(rw), out_bf(rw), out_bf(wxbc.shape[1]), jax.ShapeDtypeStruct((b, L, LANES), F32)],
        compiler_params=pltpu.CompilerParams(vmem_limit_bytes=48 * 1024 * 1024),
        name="inproj_ctx",
    )(ctx, mod3, mod3, nw, wk, wv, wxbc, wdt)


def _ssd_kernel(xbc_ref, z_ref, dt_ref, xbcc_ref, dtc_ref, cw_ref, cb_ref, dtb_ref, alog_ref, dsk_ref, nw_ref,
                y_ref,
                xpad, xpadc, u, uc, dtv, dav, dtcv, dacv, sf_scr, kb_scr, acum, ecum, dec_scr):
    L = xbc_ref.shape[1]
    Lc = xbcc_ref.shape[1]
    nch = L // CHUNK
    nchc = Lc // CHUNK
    win = CHUNK + 2 * CONV_HALO
    nconv = xbc_ref.shape[2]
    nh = SSD_HEADS

    def conv_pass(src_ref, pad_ref, dst_ref, n_chunks, length):
        zeros = jnp.zeros((CONV_HALO, nconv), F32)
        pad_ref[0:CONV_HALO, :] = zeros
        pad_ref[CONV_HALO + length:2 * CONV_HALO + length, :] = zeros
        pad_ref[CONV_HALO:CONV_HALO + length, :] = src_ref[0].astype(F32)

        def chunk(c, carry):
            base = pl.multiple_of(c * CHUNK, CHUNK)
            for cb_i in range(nconv // LANES):
                cols = slice(cb_i * LANES, (cb_i + 1) * LANES)
                w = pad_ref[pl.ds(base, win), cols]
                acc = cb_ref[:, cols] + w[CONV_HALO:CONV_HALO + CHUNK] * cw_ref[SSD_CONV // 2:SSD_CONV // 2 + 1, cols]
                for j in range(SSD_CONV):
                    if j == SSD_CONV // 2:
                        continue
                    shifted = pltpu.roll(w, (SSD_CONV // 2 - j) % win, 0)
                    acc = acc + shifted[CONV_HALO:CONV_HALO + CHUNK] * cw_ref[j:j + 1, cols]
                dst_ref[pl.ds(base, CHUNK), cols] = _silu(acc).astype(BF16)
            return carry

        lax.fori_loop(0, n_chunks, chunk, 0)

    conv_pass(xbcc_ref, xpadc, uc, nchc, Lc)
    conv_pass(xbc_ref, xpad, u, nch, L)

    a_neg = -jnp.exp(alog_ref[...])
    dtv[...] = jax.nn.softplus(dt_ref[0] + dtb_ref[...])
    dav[...] = dtv[...] * a_neg
    dtcv[...] = jax.nn.softplus(dtc_ref[0] + dtb_ref[...])
    dacv[...] = dtcv[...] * a_neg

    row_i = lax.broadcasted_iota(jnp.int32, (CHUNK, CHUNK), 0)
    col_i = lax.broadcasted_iota(jnp.int32, (CHUNK, CHUNK), 1)
    causal = col_i <= row_i
    lo_half = col_i < SSD_HEADDIM
    fwd_lane = col_i < nh
    head_of = lax.broadcasted_iota(jnp.int32, (CHUNK, SSD_WIDTH), 1) // SSD_HEADDIM
    src_col = lax.broadcasted_iota(jnp.int32, (CHUNK, SSD_WIDTH), 0)
    exp_f = (head_of == src_col).astype(BF16)
    exp_b = (head_of == src_col - nh).astype(BF16)
    exp_fb = jnp.concatenate([exp_f, exp_b], axis=1)

    def split3(v):
        hi = v.astype(BF16)
        r1 = v - hi.astype(F32)
        mid = r1.astype(BF16)
        return hi, mid, (r1 - mid.astype(F32)).astype(BF16)

    def times_onehot(v, m, passes=3):
        parts = split3(v)[:passes]
        acc = _dot(parts[0], m)
        for part in parts[1:]:
            acc = acc + _dot(part, m)
        return acc

    def colb(mat, r):
        return jnp.broadcast_to(mat[:, r:r + 1], (CHUNK, CHUNK))

    def pair_sel(a, b_):
        return jnp.where(lo_half, a, b_)

    gw = 2 * LANES

    def chunk_terms(u_ref, dt_s, da_s, base):
        dt = dt_s[pl.ds(base, CHUNK), :]
        da = da_s[pl.ds(base, CHUNK), :]
        acol = da
        for step in (1, 2, 4, 8, 16, 32, 64):
            acol = acol + jnp.where(row_i >= step, pltpu.roll(acol, step, 0), 0.0)
        ecol = acol - da
        last = acol[CHUNK - 1:CHUNK, :]
        wgt = jnp.where(fwd_lane, jnp.exp(last - acol), jnp.exp(ecol)) * dt
        scale = jnp.where(fwd_lane, jnp.exp(acol), jnp.exp(last - ecol))
        wide = times_onehot(jnp.concatenate([wgt, scale], axis=0), exp_fb, passes=1)
        dec = times_onehot(jnp.broadcast_to(jnp.exp(last), (SUBLANES, LANES)), exp_fb)[0:1]
        xs = u_ref[pl.ds(base, CHUNK), 0:SSD_WIDTH].astype(F32)
        kmats = []
        for g in range(SSD_GROUPS):
            xw = jnp.concatenate([xs[:, g * gw:(g + 1) * gw] * wide[:CHUNK, g * gw:(g + 1) * gw],
                                  xs[:, g * gw:(g + 1) * gw] * wide[:CHUNK, SSD_WIDTH + g * gw:SSD_WIDTH + (g + 1) * gw]],
                                 axis=1).astype(BF16)
            bm = u_ref[pl.ds(base, CHUNK), SSD_WIDTH + g * SSD_STATE:SSD_WIDTH + (g + 1) * SSD_STATE]
            kmats.append(_dot_tn(bm, xw))
        return dt, acol, ecol, wide[CHUNK:], dec, kmats

    def advance(s, dec, kmats, backward):
        off = SSD_WIDTH if backward else 0
        koff = gw if backward else 0
        return [dec[:, off + g * gw:off + (g + 1) * gw] * s[g] + kmats[g][:, koff:koff + gw]
                for g in range(SSD_GROUPS)]

    ctx_terms = [chunk_terms(uc, dtcv, dacv, c * CHUNK) for c in range(nchc)]
    s_f0 = [jnp.zeros((SSD_STATE, gw), F32) for _ in range(SSD_GROUPS)]
    for c in range(nchc):
        s_f0 = advance(s_f0, ctx_terms[c][4], ctx_terms[c][5], False)
    s_b0 = [jnp.zeros((SSD_STATE, gw), F32) for _ in range(SSD_GROUPS)]
    for c in reversed(range(nchc)):
        s_b0 = advance(s_b0, ctx_terms[c][4], ctx_terms[c][5], True)

    def prep(c, carry):
        base = pl.multiple_of(c * CHUNK, CHUNK)
        dt, acol, ecol, scale, dec, kmats = chunk_terms(u, dtv, dav, base)
        acum[pl.ds(base, CHUNK), :] = acol
        ecum[pl.ds(base, CHUNK), :] = ecol
        xpad[pl.ds(base, CHUNK), :] = scale
        dec_scr[pl.ds(pl.multiple_of(c * SUBLANES, SUBLANES), SUBLANES), :] = jnp.broadcast_to(dec, (SUBLANES, 2 * SSD_WIDTH))
        for g in range(SSD_GROUPS):
            sf_scr[c, g] = kmats[g][:, :gw]
            kb_scr[c, g] = kmats[g][:, gw:]
        return carry

    lax.fori_loop(0, nch, prep, 0)

    def chunk_dec(c):
        return dec_scr[pl.ds(pl.multiple_of(c * SUBLANES, SUBLANES), 1), :]

    def fwd(c, s_old):
        dec = chunk_dec(c)
        new = []
        for g in range(SSD_GROUPS):
            new.append(dec[:, g * gw:(g + 1) * gw] * s_old[g] + sf_scr[c, g])
            sf_scr[c, g] = s_old[g]
        return tuple(new)

    lax.fori_loop(0, nch, fwd, tuple(s_f0))

    def bwd(i, s_b):
        c = nch - 1 - i
        base = pl.multiple_of(c * CHUNK, CHUNK)
        acol = acum[pl.ds(base, CHUNK), :]
        ecol = ecum[pl.ds(base, CHUNK), :]
        arow = acol.T
        erow = ecol.T
        dt_t = dtv[pl.ds(base, CHUNK), :].T
        scale = xpad[pl.ds(base, CHUNK), :]
        ys = []
        for g in range(SSD_GROUPS):
            bm = u[pl.ds(base, CHUNK), SSD_WIDTH + g * SSD_STATE:SSD_WIDTH + (g + 1) * SSD_STATE]
            cm = u[pl.ds(base, CHUNK), SSD_WIDTH + (SSD_GROUPS + g) * SSD_STATE:SSD_WIDTH + (SSD_GROUPS + g + 1) * SSD_STATE]
            cbm = _dot_nt(cm, bm)
            cs_f = _dot(cm, sf_scr[c, g].astype(BF16))
            cs_b = _dot(cm, s_b[g].astype(BF16))
            for pp in range(SSD_PAIRS // SSD_GROUPS):
                p = g * (SSD_PAIRS // SSD_GROUPS) + pp
                xs_b = u[pl.ds(base, CHUNK), p * LANES:(p + 1) * LANES]
                y_h = []
                for hh in range(2):
                    r = 2 * p + hh
                    arg = jnp.where(causal, colb(acol, r) - arow[r:r + 1, :],
                                    erow[nh + r:nh + r + 1, :] - colb(ecol, nh + r))
                    coef = jnp.where(causal, dt_t[r:r + 1, :], dt_t[nh + r:nh + r + 1, :])
                    gm = (cbm * (jnp.exp(arg) * coef)).astype(BF16)
                    y_h.append(_dot(gm, xs_b))
                sl = slice(pp * LANES, (pp + 1) * LANES)
                wl = slice(p * LANES, (p + 1) * LANES)
                wlb = slice(SSD_WIDTH + p * LANES, SSD_WIDTH + (p + 1) * LANES)
                ys.append(pair_sel(y_h[0], y_h[1]) + cs_f[:, sl] * scale[:, wl] + cs_b[:, sl] * scale[:, wlb]
                          + dsk_ref[:, wl] * xs_b.astype(F32))
        y = jnp.concatenate(ys, axis=1)
        y = y * _silu(z_ref[0, pl.ds(base, CHUNK), :].astype(F32))
        ms = jnp.mean(y * y, axis=-1, keepdims=True)
        y_ref[0, pl.ds(base, CHUNK), :] = ((y * lax.rsqrt(ms + EPS)) * nw_ref[...]).astype(BF16)
        dec = chunk_dec(c)
        return tuple(dec[:, SSD_WIDTH + g * gw:SSD_WIDTH + (g + 1) * gw] * s_b[g] + kb_scr[c, g]
                     for g in range(SSD_GROUPS))

    lax.fori_loop(0, nch, bwd, tuple(s_b0), unroll=SSD_UNROLL)


def _ssd(xbc, z, dt, xbcc, dtc, conv_w8, conv_b, dt_bias, a_log, d_skip, norm_w):
    b, L, nconv = xbc.shape
    Lc = xbcc.shape[1]
    nch = L // CHUNK
    per_b = lambda n, w: pl.BlockSpec((1, n, w), lambda i: (i, 0, 0))
    return pl.pallas_call(
        _ssd_kernel,
        grid=(b,),
        in_specs=[
            per_b(L, nconv), per_b(L, SSD_WIDTH), per_b(L, LANES), per_b(Lc, nconv), per_b(Lc, LANES),
            _const_spec(conv_w8.shape), _const_spec(conv_b.shape), _const_spec(dt_bias.shape),
            _const_spec(a_log.shape), _const_spec(d_skip.shape), _const_spec(norm_w.shape),
        ],
        out_specs=per_b(L, SSD_WIDTH),
        out_shape=jax.ShapeDtypeStruct((b, L, SSD_WIDTH), BF16),
        scratch_shapes=[
            pltpu.VMEM((L + 2 * CONV_HALO, nconv), F32),
            pltpu.VMEM((Lc + 2 * CONV_HALO, nconv), F32),
            pltpu.VMEM((L, nconv), BF16),
            pltpu.VMEM((Lc, nconv), BF16),
            pltpu.VMEM((L, LANES), F32), pltpu.VMEM((L, LANES), F32),
            pltpu.VMEM((Lc, LANES), F32), pltpu.VMEM((Lc, LANES), F32),
            pltpu.VMEM((nch, SSD_GROUPS, SSD_STATE, 2 * LANES), F32),
            pltpu.VMEM((nch, SSD_GROUPS, SSD_STATE, 2 * LANES), F32),
            pltpu.VMEM((L, LANES), F32), pltpu.VMEM((L, LANES), F32),
            pltpu.VMEM((nch * SUBLANES, 2 * SSD_WIDTH), F32),
        ],
        compiler_params=pltpu.CompilerParams(vmem_limit_bytes=56 * 1024 * 1024),
        name="ssd",
    )(xbc, z, dt, xbcc, dtc, conv_w8, conv_b, dt_bias, a_log, d_skip, norm_w)


def _ret_kernel(q_ref, k_ref, v_ref, g_ref, kc_ref, vc_ref, df_ref, db_ref, gn_ref, y_ref, sf_scr):
    L = q_ref.shape[1]
    Lc = kc_ref.shape[1]
    nch = L // CHUNK
    dk = RET_DK
    row_i = lax.broadcasted_iota(jnp.int32, (CHUNK, dk), 0).astype(F32)
    col_i = lax.broadcasted_iota(jnp.int32, (CHUNK, dk), 1).astype(F32)
    rel = row_i - col_i
    crow = lax.broadcasted_iota(jnp.int32, (Lc, dk), 0).astype(F32)

    heads = []
    s_f0 = []
    s_b0 = []
    for h in range(RET_HEADS):
        cols = slice(h * dk, (h + 1) * dk)
        lg_f = -jnp.exp(df_ref[:, cols])
        lg_b = -jnp.exp(db_ref[:, cols])
        heads.append(dict(
            cols=cols,
            dmat=jnp.where(rel >= 0, jnp.exp(jnp.maximum(rel, 0.0) * lg_f), jnp.exp(jnp.maximum(-rel, 0.0) * lg_b)),
            dq_f=jnp.exp((row_i + 1.0) * lg_f),
            dq_b=jnp.exp((CHUNK - row_i) * lg_b),
            dk_f=jnp.exp((CHUNK - 1.0 - row_i) * lg_f),
            dk_b=jnp.exp(row_i * lg_b),
            dc_f=jnp.exp(CHUNK * lg_f),
            dc_b=jnp.exp(CHUNK * lg_b),
        ))
        kc = kc_ref[0, :, cols].astype(F32)
        vc = vc_ref[0, :, cols]
        s_f0.append(_dot_tn((kc * jnp.exp((Lc - 1.0 - crow) * lg_f)).astype(BF16), vc))
        s_b0.append(_dot_tn((kc * jnp.exp(crow * lg_b)).astype(BF16), vc))

    def fwd(c, s_f):
        base = pl.multiple_of(c * CHUNK, CHUNK)
        new = []
        for h, hd in enumerate(heads):
            sf_scr[c, h] = s_f[h]
            kk = k_ref[0, pl.ds(base, CHUNK), hd["cols"]].astype(F32)
            vv = v_ref[0, pl.ds(base, CHUNK), hd["cols"]]
            new.append(hd["dc_f"] * s_f[h] + _dot_tn((kk * hd["dk_f"]).astype(BF16), vv))
        return tuple(new)

    lax.fori_loop(0, nch, fwd, tuple(s_f0), unroll=RET_UNROLL)

    def bwd(i, s_bs):
        c = nch - 1 - i
        base = pl.multiple_of(c * CHUNK, CHUNK)
        new = []
        for h, hd in enumerate(heads):
            qq = q_ref[0, pl.ds(base, CHUNK), hd["cols"]]
            kk = k_ref[0, pl.ds(base, CHUNK), hd["cols"]]
            vv = v_ref[0, pl.ds(base, CHUNK), hd["cols"]]
            s_b = s_bs[h]
            scores = (_dot_nt(qq, kk) * hd["dmat"]).astype(BF16)
            y = (_dot(scores, vv)
                 + _dot(qq, sf_scr[c, h].astype(BF16)) * hd["dq_f"]
                 + _dot(qq, s_b.astype(BF16)) * hd["dq_b"])
            mu = jnp.mean(y, axis=-1, keepdims=True)
            yc = y - mu
            var = jnp.mean(yc * yc, axis=-1, keepdims=True)
            yn = (yc * lax.rsqrt(var + EPS)) * gn_ref[:, hd["cols"]]
            gate = _silu(g_ref[0, pl.ds(base, CHUNK), hd["cols"]].astype(F32))
            y_ref[0, pl.ds(base, CHUNK), hd["cols"]] = (yn * gate).astype(BF16)
            new.append(hd["dc_b"] * s_b + _dot_tn((kk.astype(F32) * hd["dk_b"]).astype(BF16), vv))
        return tuple(new)

    lax.fori_loop(0, nch, bwd, tuple(s_b0), unroll=RET_UNROLL)


def _retention(q, k, v, g, kc, vc, decay_f, decay_b, gn_w):
    b, L, w = q.shape
    Lc = kc.shape[1]
    nch = L // CHUNK
    per_b = lambda n: pl.BlockSpec((1, n, w), lambda i: (i, 0, 0))
    return pl.pallas_call(
        _ret_kernel,
        grid=(b,),
        in_specs=[per_b(L), per_b(L), per_b(L), per_b(L), per_b(Lc), per_b(Lc),
                  _const_spec((1, w)), _const_spec((1, w)), _const_spec((1, w))],
        out_specs=per_b(L),
        out_shape=jax.ShapeDtypeStruct((b, L, w), BF16),
        scratch_shapes=[
            pltpu.VMEM((nch, RET_HEADS, RET_DK, RET_DK), F32),
        ],
        compiler_params=pltpu.CompilerParams(vmem_limit_bytes=48 * 1024 * 1024),
        name="retention",
    )(q, k, v, g, kc, vc, decay_f, decay_b, gn_w)


def _outproj_router_kernel(yr_ref, ys_ref, x_ref, g1_ref, sh2_ref, sc2_ref, npost_ref, npre_ref,
                           wor_ref, wos_ref, wr_ref, br_ref, tri_ref,
                           x1_ref, h2_ref, route_ref, cnt_ref, slots_ref,
                           wcat, carry):
    i = pl.program_id(0)

    @pl.when(i == 0)
    def _():
        wr = wr_ref[...]
        hi = wr.astype(BF16)
        wcat[:, :LANES] = hi
        wcat[:, LANES:] = (wr - hi.astype(F32)).astype(BF16)
        carry[...] = jnp.zeros_like(carry)

    tm = tri_ref.shape[0]
    running = carry[...]
    for sub in range(x_ref.shape[0] // tm):
        running = _route_subtile(sub, tm, running, yr_ref, ys_ref, x_ref, g1_ref, sh2_ref, sc2_ref, npost_ref,
                                 npre_ref, wor_ref, wos_ref, br_ref, tri_ref, x1_ref, h2_ref, route_ref,
                                 slots_ref, wcat)
    carry[...] = running
    cnt_ref[...] = running


def _route_subtile(sub, tm, running, yr_ref, ys_ref, x_ref, g1_ref, sh2_ref, sc2_ref, npost_ref, npre_ref,
                   wor_ref, wos_ref, br_ref, tri_ref, x1_ref, h2_ref, route_ref, slots_ref, wcat):
    rows = slice(sub * tm, (sub + 1) * tm)
    y = _dot(yr_ref[rows, :], wor_ref[...]) + _dot(ys_ref[rows, :], wos_ref[...])
    ms = jnp.mean(y * y, axis=-1, keepdims=True)
    x1 = x_ref[rows, :] + g1_ref[0] * ((y * lax.rsqrt(ms + EPS)) * npost_ref[...])
    x1_ref[rows, :] = x1
    h2 = _norm_mod(x1, npre_ref[...], sc2_ref[0], sh2_ref[0])
    h2_ref[rows, :] = h2

    h_hi = h2.astype(BF16)
    h_lo = (h2 - h_hi.astype(F32)).astype(BF16)
    both = _dot(h_hi, wcat[...])
    lg = both[:, :LANES] + both[:, LANES:] + _dot(h_lo, wcat[:, :LANES]) + br_ref[...]

    lane = lax.broadcasted_iota(jnp.int32, (tm, LANES), 1)
    lane_f = lane.astype(F32)
    is_grp = (lane >= N_EXPERTS) & (lane < N_EXPERTS + MOE_GROUPS)
    gl = jnp.where(is_grp, lg, NEG_BIG)
    mg = jnp.max(gl, axis=-1, keepdims=True)
    grp_lane = jnp.min(jnp.where(gl == mg, lane_f, 1e9), axis=-1, keepdims=True)
    p_g = 1.0 / jnp.sum(jnp.where(is_grp, jnp.exp(gl - mg), 0.0), axis=-1, keepdims=True)
    first = (grp_lane - N_EXPERTS) * EXPERTS_PER_GROUP
    in_grp = (lane_f >= first) & (lane_f < first + EXPERTS_PER_GROUP)
    el = jnp.where(in_grp, lg, NEG_BIG)
    t1 = jnp.max(el, axis=-1, keepdims=True)
    i1 = jnp.min(jnp.where(el == t1, lane_f, 1e9), axis=-1, keepdims=True)
    el2 = jnp.where(lane_f == i1, NEG_BIG, el)
    t2 = jnp.max(el2, axis=-1, keepdims=True)
    i2 = jnp.min(jnp.where(el2 == t2, lane_f, 1e9), axis=-1, keepdims=True)
    s = jnp.exp(t2 - t1)
    w1 = p_g / (1.0 + s)
    w2 = p_g * s / (1.0 + s)

    oh1 = (lane_f == i1)
    oh2 = (lane_f == i2)
    oh = (oh1 | oh2).astype(BF16)
    before = _dot(tri_ref[...], oh) + running
    rank1 = jnp.sum(jnp.where(oh1, before, 0.0), axis=-1, keepdims=True)
    rank2 = jnp.sum(jnp.where(oh2, before, 0.0), axis=-1, keepdims=True)

    route_ref[rows, :] = jnp.where(lane == 0, w1, jnp.where(lane == 1, w2, 0.0))

    row = lax.broadcasted_iota(jnp.int32, (tm, LANES), 0)
    on_diag = (row % LANES) == lane
    per = tm // LANES
    n_sub = x_ref.shape[0] // tm
    for qi, col in enumerate((i1, i2, rank1, rank2)):
        picked = jnp.where(on_diag, col, 0.0)
        dense = jnp.sum(picked.reshape(per, LANES, LANES), axis=1).astype(jnp.int32)
        slots_ref[0, (qi * n_sub + sub) * per:(qi * n_sub + sub + 1) * per, :] = dense
    return running + jnp.sum(oh.astype(F32), axis=0, keepdims=True)


def _outproj_router(yr, ys, x2, mod3, npost, npre, wo_r, wo_s, w_router, b_router, seq_len):
    T, d = x2.shape
    tm = TM_OUT
    per_seq = seq_len // tm
    rw = yr.shape[1]
    sub = TM_ROUTE_SUB
    tri = (jnp.arange(sub)[:, None] > jnp.arange(sub)[None, :]).astype(BF16)
    tok = lambda w: pl.BlockSpec((tm, w), lambda i: (i, 0))
    modv = lambda k: pl.BlockSpec((1, 1, d), lambda i: (i // per_seq, 0, k))
    return pl.pallas_call(
        _outproj_router_kernel,
        grid=(T // tm,),
        in_specs=[
            tok(rw), tok(rw), tok(d), modv(2), modv(3), modv(4),
            _const_spec((1, d)), _const_spec((1, d)),
            _const_spec(wo_r.shape), _const_spec(wo_s.shape), _const_spec(w_router.shape), _const_spec((1, LANES)),
            _const_spec((sub, sub)),
        ],
        out_specs=[tok(d), tok(d), tok(LANES), _const_spec((1, LANES)),
                   pl.BlockSpec((1, 2 * TOP_K * (tm // LANES), LANES), lambda i: (i, 0, 0))],
        out_shape=[jax.ShapeDtypeStruct((T, d), F32), jax.ShapeDtypeStruct((T, d), F32),
                   jax.ShapeDtypeStruct((T, LANES), F32), jax.ShapeDtypeStruct((1, LANES), F32),
                   jax.ShapeDtypeStruct((T // tm, 2 * TOP_K * (tm // LANES), LANES), jnp.int32)],
        scratch_shapes=[pltpu.VMEM((d, 2 * LANES), BF16), pltpu.VMEM((1, LANES), F32)],
        compiler_params=pltpu.CompilerParams(dimension_semantics=("arbitrary",),
                                             vmem_limit_bytes=48 * 1024 * 1024),
        name="outproj_router",
    )(yr, ys, x2, mod3, mod3, mod3, npost, npre, wo_r, wo_s, w_router, b_router, tri)


def _slot_rows_kernel(start_ref, slots_ref, o_ref):
    half = slots_ref.shape[1] // 2
    expert = slots_ref[:, :half, :]
    row = slots_ref[:, half:, :]
    for e in range(N_EXPERTS):
        row = row + jnp.where(expert == e, start_ref[e], 0)
    for r in range(half):
        o_ref[:, 0, r * LANES:(r + 1) * LANES] = row[:, r, :]


def _slot_rows(pad_start, slots):
    nt, rows, _ = slots.shape
    grid_spec = pltpu.PrefetchScalarGridSpec(
        num_scalar_prefetch=1,
        grid=(1,),
        in_specs=[pl.BlockSpec(slots.shape, lambda i, ps: (0, 0, 0))],
        out_specs=pl.BlockSpec((nt, 1, rows // 2 * LANES), lambda i, ps: (0, 0, 0)),
    )
    return pl.pallas_call(
        _slot_rows_kernel,
        grid_spec=grid_spec,
        out_shape=jax.ShapeDtypeStruct((nt, 1, rows // 2 * LANES), jnp.int32),
        name="slot_rows",
    )(pad_start, slots)


def _dispatch_kernel(pad_end_ref, dest_ref, h_hbm, xs_hbm, zbuf, hbuf, sem, zsem, in_sem):
    i = pl.program_id(0)
    nt = pl.num_programs(0)
    td = hbuf.shape[1]
    mb = zbuf.shape[0]

    @pl.when(i == 0)
    def _():
        zbuf[...] = jnp.zeros_like(zbuf)

        def fill(e):
            return pltpu.make_async_copy(zbuf, xs_hbm.at[pl.ds(pl.multiple_of(pad_end_ref[e] - mb, mb), mb)], zsem)

        def has_rows(e):
            return pad_end_ref[e] > jnp.where(e == 0, 0, pad_end_ref[jnp.maximum(e - 1, 0)])

        def start(e, carry):
            @pl.when(has_rows(e))
            def _():
                fill(e).start()
            return carry

        def wait(e, carry):
            @pl.when(has_rows(e))
            def _():
                fill(e).wait()
            return carry

        def tail_fill(blk):
            return pltpu.make_async_copy(zbuf, xs_hbm.at[pl.ds(pl.multiple_of(blk * mb, mb), mb)], zsem)

        def tail_start(blk, carry):
            @pl.when(blk * mb >= pad_end_ref[N_EXPERTS - 1])
            def _():
                tail_fill(blk).start()
            return carry

        def tail_wait(blk, carry):
            @pl.when(blk * mb >= pad_end_ref[N_EXPERTS - 1])
            def _():
                tail_fill(blk).wait()
            return carry

        n_blocks = xs_hbm.shape[0] // mb
        lax.fori_loop(0, N_EXPERTS, start, 0)
        lax.fori_loop(0, n_blocks, tail_start, 0)
        lax.fori_loop(0, N_EXPERTS, wait, 0)
        lax.fori_loop(0, n_blocks, tail_wait, 0)

    def fetch(t):
        return pltpu.make_async_copy(h_hbm.at[pl.ds(pl.multiple_of(t * td, td), td)], hbuf.at[t % DISPATCH_RING],
                                     in_sem.at[t % DISPATCH_RING])

    @pl.when(i == 0)
    def _():
        fetch(i).start()

    @pl.when(i + 1 < nt)
    def _():
        fetch(i + 1).start()

    fetch(i).wait()
    for slot in range(DISPATCH_RING):
        @pl.when(i % DISPATCH_RING == slot)
        def _(slot=slot):
            _issue_row_copies(dest_ref, td, lambda kk, j, d: pltpu.make_async_copy(
                hbuf.at[slot, pl.ds(j, 1)], xs_hbm.at[pl.ds(d, 1)], sem.at[slot]))

    def drain(t):
        for kk in range(TOP_K):
            pltpu.make_async_copy(hbuf.at[t % DISPATCH_RING], xs_hbm.at[pl.ds(0, td)],
                                  sem.at[t % DISPATCH_RING]).wait()

    @pl.when(i > 0)
    def _():
        drain(i - 1)

    @pl.when(i == nt - 1)
    def _():
        drain(i)


def _issue_row_copies(dest_ref, n_tok, row_copy):
    def issue(j, carry):
        for kk in range(TOP_K):
            row_copy(kk, j, dest_ref[0, 0, kk * n_tok + j]).start()
        return carry

    lax.fori_loop(0, n_tok, issue, 0, unroll=DMA_UNROLL)


def _dispatch(pad_end, dest, h2, cap):
    T, d = h2.shape
    td = TD_DISPATCH
    nt = T // td
    grid_spec = pltpu.PrefetchScalarGridSpec(
        num_scalar_prefetch=1,
        grid=(nt,),
        in_specs=[
            pl.BlockSpec((1, 1, TOP_K * td), lambda i, pe: (i, 0, 0), memory_space=pltpu.SMEM),
            pl.BlockSpec(memory_space=pl.ANY),
        ],
        out_specs=pl.BlockSpec(memory_space=pl.ANY),
        scratch_shapes=[pltpu.VMEM((MB_EXPERT, d), h2.dtype), pltpu.VMEM((DISPATCH_RING, td, d), h2.dtype),
                        pltpu.SemaphoreType.DMA((DISPATCH_RING,)), pltpu.SemaphoreType.DMA(()),
                        pltpu.SemaphoreType.DMA((DISPATCH_RING,))],
    )
    return pl.pallas_call(
        _dispatch_kernel,
        grid_spec=grid_spec,
        out_shape=jax.ShapeDtypeStruct((cap, d), h2.dtype),
        compiler_params=pltpu.CompilerParams(dimension_semantics=("arbitrary",)),
        name="dispatch",
    )(pad_end, dest, h2)


def _expert_kernel(be_ref, nused_ref, xs_ref, wg_ref, wu_ref, wd_ref, y_ref, wg_b, wu_b, wd_b):
    i = pl.program_id(0)
    prev = be_ref[jnp.maximum(i - 1, 0)]

    @pl.when((i == 0) | (be_ref[i] != prev))
    def _():
        wg_b[...] = wg_ref[0].astype(BF16)
        wu_b[...] = wu_ref[0].astype(BF16)
        wd_b[...] = wd_ref[0].astype(BF16)

    @pl.when(i < nused_ref[0])
    def _():
        xb = xs_ref[...].astype(BF16)
        hid = (_silu(_dot(xb, wg_b[...])) * _dot(xb, wu_b[...])).astype(BF16)
        y_ref[...] = _dot(hid, wd_b[...])

    @pl.when(i >= nused_ref[0])
    def _():
        y_ref[...] = jnp.zeros_like(y_ref)


def _experts(blk_expert, n_used, xs, w_gate, w_up, w_down):
    cap, dp = xs.shape
    d, de = w_gate.shape[1:]
    assert dp == d
    mb = MB_EXPERT
    grid_spec = pltpu.PrefetchScalarGridSpec(
        num_scalar_prefetch=2,
        grid=(cap // mb,),
        in_specs=[
            pl.BlockSpec((mb, dp), lambda i, be, nu: (jnp.minimum(i, nu[0] - 1), 0)),
            pl.BlockSpec((1, d, de), lambda i, be, nu: (be[i], 0, 0)),
            pl.BlockSpec((1, d, de), lambda i, be, nu: (be[i], 0, 0)),
            pl.BlockSpec((1, de, d), lambda i, be, nu: (be[i], 0, 0)),
        ],
        out_specs=pl.BlockSpec((mb, dp), lambda i, be, nu: (i, 0)),
        scratch_shapes=[pltpu.VMEM((d, de), BF16), pltpu.VMEM((d, de), BF16), pltpu.VMEM((de, d), BF16)],
    )
    return pl.pallas_call(
        _expert_kernel,
        grid_spec=grid_spec,
        out_shape=jax.ShapeDtypeStruct((cap, dp), xs.dtype),
        compiler_params=pltpu.CompilerParams(dimension_semantics=("arbitrary",),
                                             vmem_limit_bytes=48 * 1024 * 1024),
        name="experts",
    )(blk_expert, n_used, xs, w_gate, w_up, w_down)


def _combine_kernel(dest_ref, dest_next_ref, route_ref, x1_ref, g2_ref, nw_ref, yb_hbm, o_ref, buf, sem):
    i = pl.program_id(0)
    nt = pl.num_programs(0)
    tf = x1_ref.shape[0]
    stage = i % 2

    def gather(refs, st):
        _issue_row_copies(refs, tf, lambda kk, j, d: pltpu.make_async_copy(
            yb_hbm.at[pl.ds(d, 1)], buf.at[st, kk, pl.ds(j, 1)], sem.at[st]))

    @pl.when(i == 0)
    def _():
        gather(dest_ref, 0)

    for st in range(2):
        @pl.when((i + 1 < nt) & (stage != st))
        def _(st=st):
            gather(dest_next_ref, st)

    for kk in range(TOP_K):
        pltpu.make_async_copy(yb_hbm.at[pl.ds(0, tf)], buf.at[stage, kk], sem.at[stage]).wait()

    route = route_ref[...]
    out = route[:, 0:1] * buf[stage, 0] + route[:, 1:2] * buf[stage, 1]
    ms = jnp.mean(out * out, axis=-1, keepdims=True)
    o_ref[...] = x1_ref[...] + g2_ref[0] * ((out * lax.rsqrt(ms + EPS)) * nw_ref[...])


def _combine(dest, route, x1, mod3, nw, yb, seq_len):
    T, d = x1.shape
    tf = TF_COMBINE
    nt = T // tf
    per_seq = seq_len // tf
    slot_spec = functools.partial(pl.BlockSpec, (1, 1, TOP_K * tf), memory_space=pltpu.SMEM)
    return pl.pallas_call(
        _combine_kernel,
        grid=(nt,),
        in_specs=[
            slot_spec(lambda i: (i, 0, 0)),
            slot_spec(lambda i: (jnp.minimum(i + 1, nt - 1), 0, 0)),
            pl.BlockSpec((tf, LANES), lambda i: (i, 0)),
            pl.BlockSpec((tf, d), lambda i: (i, 0)),
            pl.BlockSpec((1, 1, d), lambda i: (i // per_seq, 0, 5)),
            _const_spec((1, d)),
            pl.BlockSpec(memory_space=pl.ANY),
        ],
        out_specs=pl.BlockSpec((tf, d), lambda i: (i, 0)),
        out_shape=jax.ShapeDtypeStruct((T, d), F32),
        scratch_shapes=[pltpu.VMEM((2, TOP_K, tf, yb.shape[1]), yb.dtype), pltpu.SemaphoreType.DMA((2,))],
        compiler_params=pltpu.CompilerParams(dimension_semantics=("arbitrary",),
                                             vmem_limit_bytes=48 * 1024 * 1024),
        name="combine",
    )(dest, dest, route, x1, mod3, nw, yb)


def _rope_tables(L, n_heads):
    quarter = RET_DK // 4
    freqs = ROPE_BASE ** (-jnp.arange(quarter, dtype=F32) / quarter)
    t = jnp.arange(L)
    ang_r = (t // GRID_W).astype(F32)[:, None] * freqs
    ang_c = (t % GRID_W).astype(F32)[:, None] * freqs
    cos = jnp.concatenate([jnp.cos(ang_r)] * 2 + [jnp.cos(ang_c)] * 2, axis=-1)
    sin = jnp.concatenate([-jnp.sin(ang_r), jnp.sin(ang_r), -jnp.sin(ang_c), jnp.sin(ang_c)], axis=-1)
    return jnp.tile(cos, (1, n_heads)), jnp.tile(sin, (1, n_heads))


def _lane_pad(v, width=LANES):
    return jnp.pad(v, [(0, 0)] * (v.ndim - 1) + [(0, width - v.shape[-1])])


def kernel(x, c, ctx, c_ctx, w_mod, b_mod, norm_pre_mix, norm_post_mix, norm_pre_ffn, norm_post_ffn, w_in, w_out, ret_decay_f, ret_decay_b, ret_gn_w, ssd_conv_w, ssd_conv_b, ssd_dt_bias_f, ssd_dt_bias_b, ssd_a_log_f, ssd_a_log_b, ssd_d, ssd_norm_w, moe_w_rg, moe_b_rg, moe_w_re, moe_b_re, moe_w_gate, moe_w_up, moe_w_down):
    b, L, d = x.shape
    assert w_mod.shape[0] == 1, "single layer: context outputs are never needed"
    assert TM_OUT == TD_DISPATCH == TF_COMBINE, "router, dispatch and combine share one slot-row layout"
    rw = RET_HEADS * RET_DK
    nconv = SSD_WIDTH + 2 * SSD_GROUPS * SSD_STATE
    T = b * L

    mod_rows = -(-(b + 1) // SUBLANES) * SUBLANES
    c_all = jnp.zeros((mod_rows, d), F32).at[:b].set(c).at[b].set(c_ctx)
    mod3 = _modulation(c_all, w_mod[0], b_mod[0]).reshape(mod_rows, 1, 6 * d)

    wi = w_in[0]
    o = 0
    wq = wi[:, o:o + rw]; o += rw
    wk = wi[:, o:o + rw]; o += rw
    wv = wi[:, o:o + rw]; o += rw
    wg = wi[:, o:o + rw]; o += rw
    wz = wi[:, o:o + SSD_WIDTH]; o += SSD_WIDTH
    wxbc = wi[:, o:o + nconv].astype(BF16); o += nconv
    wdt = _lane_pad(wi[:, o:o + 2 * SSD_HEADS]).astype(BF16)
    wqk = jnp.concatenate([wq, wk], axis=1).astype(BF16)
    wvgz = jnp.concatenate([wv, wg, wz], axis=1).astype(BF16)
    cos_t, sin_t = _rope_tables(L, RET_HEADS)
    nw1 = norm_pre_mix[0].reshape(1, d)

    q, k, v, g, z, xbc, dt = _inproj(x, mod3, nw1, wqk, wvgz, wxbc, wdt, cos_t, sin_t)
    kc, vc, xbcc, dtc = _inproj_ctx(ctx, mod3, b, nw1, wk.astype(BF16), wv.astype(BF16), wxbc, wdt)

    conv_w8 = jnp.pad(ssd_conv_w[0], ((0, SUBLANES - SSD_CONV), (0, 0)))
    dt_bias = _lane_pad(jnp.concatenate([ssd_dt_bias_f[0], ssd_dt_bias_b[0]])[None, :])
    a_log = _lane_pad(jnp.concatenate([ssd_a_log_f[0], ssd_a_log_b[0]])[None, :])
    d_skip = jnp.repeat(ssd_d[0], SSD_HEADDIM)[None, :]
    ys = _ssd(xbc, z, dt, xbcc, dtc, conv_w8, ssd_conv_b[0][None, :], dt_bias, a_log, d_skip,
              ssd_norm_w[0][None, :])

    yr = _retention(q, k, v, g, kc, vc,
                    jnp.repeat(ret_decay_f[0], RET_DK)[None, :], jnp.repeat(ret_decay_b[0], RET_DK)[None, :],
                    ret_gn_w[0][None, :])

    wo = w_out[0].astype(BF16)
    w_router = _lane_pad(jnp.concatenate(
        [jnp.transpose(moe_w_re[0], (1, 0, 2)).reshape(d, N_EXPERTS), moe_w_rg[0]], axis=1))
    b_router = _lane_pad(jnp.concatenate([moe_b_re[0].reshape(-1), moe_b_rg[0]])[None, :])
    x1, h2, route, counts, slots = _outproj_router(
        yr.reshape(T, rw), ys.reshape(T, SSD_WIDTH), x.reshape(T, d), mod3,
        norm_post_mix[0][None, :], norm_pre_ffn[0][None, :], wo[:rw], wo[rw:], w_router, b_router, L)

    mb = MB_EXPERT
    n_blocks = -(-(T * TOP_K + N_EXPERTS * (mb - 1)) // mb)
    cap = n_blocks * mb
    cnt = counts[0, :N_EXPERTS].astype(jnp.int32)
    padded = (cnt + mb - 1) // mb * mb
    pad_end = jnp.cumsum(padded)
    pad_start = pad_end - padded
    blk_start = jnp.arange(n_blocks, dtype=jnp.int32) * mb
    blk_expert = jnp.minimum(jnp.sum((pad_end[None, :] <= blk_start[:, None]).astype(jnp.int32), axis=1),
                             N_EXPERTS - 1)
    n_used = (pad_end[-1:] // mb).astype(jnp.int32)

    dest = _slot_rows(pad_start.astype(jnp.int32), slots)
    xs = _dispatch(pad_end.astype(jnp.int32), dest, h2, cap)
    yb = _experts(blk_expert, n_used, xs, moe_w_gate[0], moe_w_up[0], moe_w_down[0])
    out = _combine(dest, route, x1, mod3, norm_post_ffn[0][None, :], yb, L)
    return out.reshape(b, L, d)
```

```python
import functools
import math

import jax
import jax.numpy as jnp
from jax import lax
from jax.experimental import pallas as pl
from jax.experimental.pallas import tpu as pltpu

F32 = jnp.float32
BF16 = jnp.bfloat16
HIGHEST = lax.Precision.HIGHEST

LANES = 128
SUBLANES = 8

EPS = 1e-6
CHUNK = 128
GRID_W = 64
RET_HEADS = 4
RET_DK = 128
ROPE_BASE = 10000.0
SSD_HEADS = 8
SSD_HEADDIM = 64
SSD_GROUPS = 2
SSD_STATE = 128
SSD_WIDTH = SSD_HEADS * SSD_HEADDIM
SSD_CONV = 5
SSD_PAIRS = SSD_WIDTH // LANES
MOE_GROUPS = 4
EXPERTS_PER_GROUP = 8
N_EXPERTS = MOE_GROUPS * EXPERTS_PER_GROUP
TOP_K = 2
CONV_HALO = SUBLANES

TM_PROJ = 512
TM_OUT = 512
TM_ROUTE_SUB = 512
TD_DISPATCH = 512
MB_EXPERT = 512
TF_COMBINE = 512
DMA_UNROLL = 8
DISPATCH_RING = 3
SSD_UNROLL = 2
RET_UNROLL = 2
NEG_BIG = -1e30


def _silu(v):
    return v * jax.nn.sigmoid(v)


def _dot(a, b):
    return jnp.dot(a, b, preferred_element_type=F32)


def _dot_tn(a, b):
    return lax.dot_general(a, b, (((0,), (0,)), ((), ())), preferred_element_type=F32)


def _dot_nt(a, b):
    return lax.dot_general(a, b, (((1,), (1,)), ((), ())), preferred_element_type=F32)


def _mod_kernel(c_ref, w_ref, b_ref, o_ref):
    a = _silu(c_ref[...])
    o_ref[...] = jnp.dot(a, w_ref[...], precision=HIGHEST, preferred_element_type=F32) + b_ref[...]


def _modulation(c_all, w_mod, b_mod):
    rows, d = c_all.shape
    n = w_mod.shape[1]
    return pl.pallas_call(
        _mod_kernel,
        grid=(n // d,),
        in_specs=[
            pl.BlockSpec((rows, d), lambda j: (0, 0)),
            pl.BlockSpec((d, d), lambda j: (0, j)),
            pl.BlockSpec((1, d), lambda j: (0, j)),
        ],
        out_specs=pl.BlockSpec((rows, d), lambda j: (0, j)),
        out_shape=jax.ShapeDtypeStruct((rows, n), F32),
        name="modulation",
    )(c_all, w_mod, b_mod.reshape(1, n))


def _norm_mod(x, nw, sc, sh):
    ms = jnp.mean(x * x, axis=-1, keepdims=True)
    return (x * lax.rsqrt(ms + EPS)) * nw * (1.0 + sc) + sh


def _rope(t, cos, sin_signed, first_half):
    width = t.shape[-1]
    quarter = RET_DK // 4
    swapped = jnp.where(first_half, pltpu.roll(t, width - quarter, 1), pltpu.roll(t, quarter, 1))
    return t * cos + swapped * sin_signed


def _inproj_kernel(x_ref, sh_ref, sc_ref, nw_ref, wqk_ref, wvgz_ref, wxbc_ref, wdt_ref, cos_ref, sin_ref,
                   q_ref, k_ref, v_ref, g_ref, z_ref, xbc_ref, dt_ref):
    hb = _norm_mod(x_ref[0], nw_ref[...], sc_ref[0], sh_ref[0]).astype(BF16)
    rw = q_ref.shape[-1]
    qk = _dot(hb, wqk_ref[...])
    cos = cos_ref[...]
    sin = sin_ref[...]
    lane = lax.broadcasted_iota(jnp.int32, cos.shape, 1)
    first_half = (lane % (RET_DK // 2)) < (RET_DK // 4)
    q_ref[0] = _rope(qk[:, :rw], cos, sin, first_half).astype(BF16)
    k_ref[0] = (_rope(qk[:, rw:], cos, sin, first_half) * (RET_DK ** -0.5)).astype(BF16)
    vgz = _dot(hb, wvgz_ref[...])
    v_ref[0] = vgz[:, :rw].astype(BF16)
    g_ref[0] = vgz[:, rw:2 * rw].astype(BF16)
    z_ref[0] = vgz[:, 2 * rw:].astype(BF16)
    xbc_ref[0] = _dot(hb, wxbc_ref[...]).astype(BF16)
    dt_ref[0] = _dot(hb, wdt_ref[...])


def _inproj_ctx_kernel(x_ref, sh_ref, sc_ref, nw_ref, wk_ref, wv_ref, wxbc_ref, wdt_ref,
                       k_ref, v_ref, xbc_ref, dt_ref):
    hb = _norm_mod(x_ref[0], nw_ref[...], sc_ref[0], sh_ref[0]).astype(BF16)
    k_ref[0] = (_dot(hb, wk_ref[...]) * (RET_DK ** -0.5)).astype(BF16)
    v_ref[0] = _dot(hb, wv_ref[...]).astype(BF16)
    xbc_ref[0] = _dot(hb, wxbc_ref[...]).astype(BF16)
    dt_ref[0] = _dot(hb, wdt_ref[...])


def _const_spec(shape):
    nd = len(shape)
    return pl.BlockSpec(shape, lambda *_: (0,) * nd)


def _inproj(x, mod3, nw, wqk, wvgz, wxbc, wdt, cos_t, sin_t):
    b, L, d = x.shape
    tm = min(TM_PROJ, L)
    rw = wqk.shape[1] // 2
    tok = lambda w: pl.BlockSpec((1, tm, w), lambda i, j: (i, j, 0))
    out_bf = lambda w: jax.ShapeDtypeStruct((b, L, w), BF16)
    return pl.pallas_call(
        _inproj_kernel,
        grid=(b, L // tm),
        in_specs=[
            tok(d),
            pl.BlockSpec((1, 1, d), lambda i, j: (i, 0, 0)),
            pl.BlockSpec((1, 1, d), lambda i, j: (i, 0, 1)),
            _const_spec((1, d)),
            _const_spec(wqk.shape), _const_spec(wvgz.shape), _const_spec(wxbc.shape), _const_spec(wdt.shape),
            pl.BlockSpec((tm, rw), lambda i, j: (j, 0)),
            pl.BlockSpec((tm, rw), lambda i, j: (j, 0)),
        ],
        out_specs=[tok(rw), tok(rw), tok(rw), tok(rw), tok(rw), tok(wxbc.shape[1]), tok(LANES)],
        out_shape=[out_bf(rw), out_bf(rw), out_bf(rw), out_bf(rw), out_bf(rw), out_bf(wxbc.shape[1]),
                   jax.ShapeDtypeStruct((b, L, LANES), F32)],
        compiler_params=pltpu.CompilerParams(vmem_limit_bytes=48 * 1024 * 1024),
        name="inproj",
    )(x, mod3, mod3, nw, wqk, wvgz, wxbc, wdt, cos_t, sin_t)


def _inproj_ctx(ctx, mod3, ctx_row, nw, wk, wv, wxbc, wdt):
    b, L, d = ctx.shape
    tm = min(TM_PROJ, L)
    rw = wk.shape[1]
    tok = lambda w: pl.BlockSpec((1, tm, w), lambda i, j: (i, j, 0))
    out_bf = lambda w: jax.ShapeDtypeStruct((b, L, w), BF16)
    return pl.pallas_call(
        _inproj_ctx_kernel,
        grid=(b, L // tm),
        in_specs=[
            tok(d),
            pl.BlockSpec((1, 1, d), lambda i, j: (ctx_row, 0, 0)),
            pl.BlockSpec((1, 1, d), lambda i, j: (ctx_row, 0, 1)),
            _const_spec((1, d)),
            _const_spec(wk.shape), _const_spec(wv.shape), _const_spec(wxbc.shape), _const_spec(wdt.shape),
        ],
        out_specs=[tok(rw), tok(rw), tok(wxbc.shape[1]), tok(LANES)],
        out_shape=[out_bf(rw), out_bf(rw), out_bf(wxbc.shape[1]), jax.ShapeDtypeStruct((b, L, LANES), F32)],
        compiler_params=pltpu.CompilerParams(vmem_limit_bytes=48 * 1024 * 1024),
        name="inproj_ctx",
    )(ctx, mod3, mod3, nw, wk, wv, wxbc, wdt)


def _ssd_kernel(xbc_ref, z_ref, dt_ref, xbcc_ref, dtc_ref, cw_ref, cb_ref, dtb_ref, alog_ref, dsk_ref, nw_ref,
                y_ref,
                xpad, xpadc, u, uc, dtv, dav, dtcv, dacv, sf_scr, kb_scr, acum, ecum, dec_scr):
    L = xbc_ref.shape[1]
    Lc = xbcc_ref.shape[1]
    nch = L // CHUNK
    nchc = Lc // CHUNK
    win = CHUNK + 2 * CONV_HALO
    nconv = xbc_ref.shape[2]
    nh = SSD_HEADS

    def conv_pass(src_ref, pad_ref, dst_ref, n_chunks, length):
        zeros = jnp.zeros((CONV_HALO, nconv), F32)
        pad_ref[0:CONV_HALO, :] = zeros
        pad_ref[CONV_HALO + length:2 * CONV_HALO + length, :] = zeros
        pad_ref[CONV_HALO:CONV_HALO + length, :] = src_ref[0].astype(F32)

        def chunk(c, carry):
            base = pl.multiple_of(c * CHUNK, CHUNK)
            for cb_i in range(nconv // LANES):
                cols = slice(cb_i * LANES, (cb_i + 1) * LANES)
                w = pad_ref[pl.ds(base, win), cols]
                acc = cb_ref[:, cols] + w[CONV_HALO:CONV_HALO + CHUNK] * cw_ref[SSD_CONV // 2:SSD_CONV // 2 + 1, cols]
                for j in range(SSD_CONV):
                    if j == SSD_CONV // 2:
                        continue
                    shifted = pltpu.roll(w, (SSD_CONV // 2 - j) % win, 0)
                    acc = acc + shifted[CONV_HALO:CONV_HALO + CHUNK] * cw_ref[j:j + 1, cols]
                dst_ref[pl.ds(base, CHUNK), cols] = _silu(acc).astype(BF16)
            return carry

        lax.fori_loop(0, n_chunks, chunk, 0)

    conv_pass(xbcc_ref, xpadc, uc, nchc, Lc)
    conv_pass(xbc_ref, xpad, u, nch, L)

    a_neg = -jnp.exp(alog_ref[...])
    dtv[...] = jax.nn.softplus(dt_ref[0] + dtb_ref[...])
    dav[...] = dtv[...] * a_neg
    dtcv[...] = jax.nn.softplus(dtc_ref[0] + dtb_ref[...])
    dacv[...] = dtcv[...] * a_neg

    row_i = lax.broadcasted_iota(jnp.int32, (CHUNK, CHUNK), 0)
    col_i = lax.broadcasted_iota(jnp.int32, (CHUNK, CHUNK), 1)
    causal = col_i <= row_i
    lo_half = col_i < SSD_HEADDIM
    fwd_lane = col_i < nh
    head_of = lax.broadcasted_iota(jnp.int32, (CHUNK, SSD_WIDTH), 1) // SSD_HEADDIM
    src_col = lax.broadcasted_iota(jnp.int32, (CHUNK, SSD_WIDTH), 0)
    exp_f = (head_of == src_col).astype(BF16)
    exp_b = (head_of == src_col - nh).astype(BF16)
    exp_fb = jnp.concatenate([exp_f, exp_b], axis=1)

    def split3(v):
        hi = v.astype(BF16)
        r1 = v - hi.astype(F32)
        mid = r1.astype(BF16)
        return hi, mid, (r1 - mid.astype(F32)).astype(BF16)

    def times_onehot(v, m, passes=3):
        parts = split3(v)[:passes]
        acc = _dot(parts[0], m)
        for part in parts[1:]:
            acc = acc + _dot(part, m)
        return acc

    def colb(mat, r):
        return jnp.broadcast_to(mat[:, r:r + 1], (CHUNK, CHUNK))

    def pair_sel(a, b_):
        return jnp.where(lo_half, a, b_)

    gw = 2 * LANES

    def chunk_terms(u_ref, dt_s, da_s, base):
        dt = dt_s[pl.ds(base, CHUNK), :]
        da = da_s[pl.ds(base, CHUNK), :]
        acol = da
        for step in (1, 2, 4, 8, 16, 32, 64):
            acol = acol + jnp.where(row_i >= step, pltpu.roll(acol, step, 0), 0.0)
        ecol = acol - da
        last = acol[CHUNK - 1:CHUNK, :]
        wgt = jnp.where(fwd_lane, jnp.exp(last - acol), jnp.exp(ecol)) * dt
        scale = jnp.where(fwd_lane, jnp.exp(acol), jnp.exp(last - ecol))
        wide = times_onehot(jnp.concatenate([wgt, scale], axis=0), exp_fb, passes=1)
        dec = times_onehot(jnp.broadcast_to(jnp.exp(last), (SUBLANES, LANES)), exp_fb)[0:1]
        xs = u_ref[pl.ds(base, CHUNK), 0:SSD_WIDTH].astype(F32)
        kmats = []
        for g in range(SSD_GROUPS):
            xw = jnp.concatenate([xs[:, g * gw:(g + 1) * gw] * wide[:CHUNK, g * gw:(g + 1) * gw],
                                  xs[:, g * gw:(g + 1) * gw] * wide[:CHUNK, SSD_WIDTH + g * gw:SSD_WIDTH + (g + 1) * gw]],
                                 axis=1).astype(BF16)
            bm = u_ref[pl.ds(base, CHUNK), SSD_WIDTH + g * SSD_STATE:SSD_WIDTH + (g + 1) * SSD_STATE]
            kmats.append(_dot_tn(bm, xw))
        return dt, acol, ecol, wide[CHUNK:], dec, kmats

    def advance(s, dec, kmats, backward):
        off = SSD_WIDTH if backward else 0
        koff = gw if backward else 0
        return [dec[:, off + g * gw:off + (g + 1) * gw] * s[g] + kmats[g][:, koff:koff + gw]
                for g in range(SSD_GROUPS)]

    ctx_terms = [chunk_terms(uc, dtcv, dacv, c * CHUNK) for c in range(nchc)]
    s_f0 = [jnp.zeros((SSD_STATE, gw), F32) for _ in range(SSD_GROUPS)]
    for c in range(nchc):
        s_f0 = advance(s_f0, ctx_terms[c][4], ctx_terms[c][5], False)
    s_b0 = [jnp.zeros((SSD_STATE, gw), F32) for _ in range(SSD_GROUPS)]
    for c in reversed(range(nchc)):
        s_b0 = advance(s_b0, ctx_terms[c][4], ctx_terms[c][5], True)

    def prep(c, carry):
        base = pl.multiple_of(c * CHUNK, CHUNK)
        dt, acol, ecol, scale, dec, kmats = chunk_terms(u, dtv, dav, base)
        acum[pl.ds(base, CHUNK), :] = acol
        ecum[pl.ds(base, CHUNK), :] = ecol
        xpad[pl.ds(base, CHUNK), :] = scale
        dec_scr[pl.ds(pl.multiple_of(c * SUBLANES, SUBLANES), SUBLANES), :] = jnp.broadcast_to(dec, (SUBLANES, 2 * SSD_WIDTH))
        for g in range(SSD_GROUPS):
            sf_scr[c, g] = kmats[g][:, :gw]
            kb_scr[c, g] = kmats[g][:, gw:]
        return carry

    lax.fori_loop(0, nch, prep, 0)

    def chunk_dec(c):
        return dec_scr[pl.ds(pl.multiple_of(c * SUBLANES, SUBLANES), 1), :]

    def fwd(c, s_old):
        dec = chunk_dec(c)
        new = []
        for g in range(SSD_GROUPS):
            new.append(dec[:, g * gw:(g + 1) * gw] * s_old[g] + sf_scr[c, g])
            sf_scr[c, g] = s_old[g]
        return tuple(new)

    lax.fori_loop(0, nch, fwd, tuple(s_f0))

    def bwd(i, s_b):
        c = nch - 1 - i
        base = pl.multiple_of(c * CHUNK, CHUNK)
        acol = acum[pl.ds(base, CHUNK), :]
        ecol = ecum[pl.ds(base, CHUNK), :]
        arow = acol.T
        erow = ecol.T
        dt_t = dtv[pl.ds(base, CHUNK), :].T
        scale = xpad[pl.ds(base, CHUNK), :]
        ys = []
        for g in range(SSD_GROUPS):
            bm = u[pl.ds(base, CHUNK), SSD_WIDTH + g * SSD_STATE:SSD_WIDTH + (g + 1) * SSD_STATE]
            cm = u[pl.ds(base, CHUNK), SSD_WIDTH + (SSD_GROUPS + g) * SSD_STATE:SSD_WIDTH + (SSD_GROUPS + g + 1) * SSD_STATE]
            cbm = _dot_nt(cm, bm)
            cs_f = _dot(cm, sf_scr[c, g].astype(BF16))
            cs_b = _dot(cm, s_b[g].astype(BF16))
            for pp in range(SSD_PAIRS // SSD_GROUPS):
                p = g * (SSD_PAIRS // SSD_GROUPS) + pp
                xs_b = u[pl.ds(base, CHUNK), p * LANES:(p + 1) * LANES]
                y_h = []
                for hh in range(2):
                    r = 2 * p + hh
                    arg = jnp.where(causal, colb(acol, r) - arow[r:r + 1, :],
                                    erow[nh + r:nh + r + 1, :] - colb(ecol, nh + r))
                    coef = jnp.where(causal, dt_t[r:r + 1, :], dt_t[nh + r:nh + r + 1, :])
                    gm = (cbm * (jnp.exp(arg) * coef)).astype(BF16)
                    y_h.append(_dot(gm, xs_b))
                sl = slice(pp * LANES, (pp + 1) * LANES)
                wl = slice(p * LANES, (p + 1) * LANES)
                wlb = slice(SSD_WIDTH + p * LANES, SSD_WIDTH + (p + 1) * LANES)
                ys.append(pair_sel(y_h[0], y_h[1]) + cs_f[:, sl] * scale[:, wl] + cs_b[:, sl] * scale[:, wlb]
                          + dsk_ref[:, wl] * xs_b.astype(F32))
        y = jnp.concatenate(ys, axis=1)
        y = y * _silu(z_ref[0, pl.ds(base, CHUNK), :].astype(F32))
        ms = jnp.mean(y * y, axis=-1, keepdims=True)
        y_ref[0, pl.ds(base, CHUNK), :] = ((y * lax.rsqrt(ms + EPS)) * nw_ref[...]).astype(BF16)
        dec = chunk_dec(c)
        return tuple(dec[:, SSD_WIDTH + g * gw:SSD_WIDTH + (g + 1) * gw] * s_b[g] + kb_scr[c, g]
                     for g in range(SSD_GROUPS))

    lax.fori_loop(0, nch, bwd, tuple(s_b0), unroll=SSD_UNROLL)


def _ssd(xbc, z, dt, xbcc, dtc, conv_w8, conv_b, dt_bias, a_log, d_skip, norm_w):
    b, L, nconv = xbc.shape
    Lc = xbcc.shape[1]
    nch = L // CHUNK
    per_b = lambda n, w: pl.BlockSpec((1, n, w), lambda i: (i, 0, 0))
    return pl.pallas_call(
        _ssd_kernel,
        grid=(b,),
        in_specs=[
            per_b(L, nconv), per_b(L, SSD_WIDTH), per_b(L, LANES), per_b(Lc, nconv), per_b(Lc, LANES),
            _const_spec(conv_w8.shape), _const_spec(conv_b.shape), _const_spec(dt_bias.shape),
            _const_spec(a_log.shape), _const_spec(d_skip.shape), _const_spec(norm_w.shape),
        ],
        out_specs=per_b(L, SSD_WIDTH),
        out_shape=jax.ShapeDtypeStruct((b, L, SSD_WIDTH), BF16),
        scratch_shapes=[
            pltpu.VMEM((L + 2 * CONV_HALO, nconv), F32),
            pltpu.VMEM((Lc + 2 * CONV_HALO, nconv), F32),
            pltpu.VMEM((L, nconv), BF16),
            pltpu.VMEM((Lc, nconv), BF16),
            pltpu.VMEM((L, LANES), F32), pltpu.VMEM((L, LANES), F32),
            pltpu.VMEM((Lc, LANES), F32), pltpu.VMEM((Lc, LANES), F32),
            pltpu.VMEM((nch, SSD_GROUPS, SSD_STATE, 2 * LANES), F32),
            pltpu.VMEM((nch, SSD_GROUPS, SSD_STATE, 2 * LANES), F32),
            pltpu.VMEM((L, LANES), F32), pltpu.VMEM((L, LANES), F32),
            pltpu.VMEM((nch * SUBLANES, 2 * SSD_WIDTH), F32),
        ],
        compiler_params=pltpu.CompilerParams(vmem_limit_bytes=56 * 1024 * 1024),
        name="ssd",
    )(xbc, z, dt, xbcc, dtc, conv_w8, conv_b, dt_bias, a_log, d_skip, norm_w)


def _ret_kernel(q_ref, k_ref, v_ref, g_ref, kc_ref, vc_ref, df_ref, db_ref, gn_ref, y_ref, sf_scr):
    L = q_ref.shape[1]
    Lc = kc_ref.shape[1]
    nch = L // CHUNK
    dk = RET_DK
    row_i = lax.broadcasted_iota(jnp.int32, (CHUNK, dk), 0).astype(F32)
    col_i = lax.broadcasted_iota(jnp.int32, (CHUNK, dk), 1).astype(F32)
    rel = row_i - col_i
    crow = lax.broadcasted_iota(jnp.int32, (Lc, dk), 0).astype(F32)

    heads = []
    s_f0 = []
    s_b0 = []
    for h in range(RET_HEADS):
        cols = slice(h * dk, (h + 1) * dk)
        lg_f = -jnp.exp(df_ref[:, cols])
        lg_b = -jnp.exp(db_ref[:, cols])
        heads.append(dict(
            cols=cols,
            dmat=jnp.where(rel >= 0, jnp.exp(jnp.maximum(rel, 0.0) * lg_f), jnp.exp(jnp.maximum(-rel, 0.0) * lg_b)),
            dq_f=jnp.exp((row_i + 1.0) * lg_f),
            dq_b=jnp.exp((CHUNK - row_i) * lg_b),
            dk_f=jnp.exp((CHUNK - 1.0 - row_i) * lg_f),
            dk_b=jnp.exp(row_i * lg_b),
            dc_f=jnp.exp(CHUNK * lg_f),
            dc_b=jnp.exp(CHUNK * lg_b),
        ))
        kc = kc_ref[0, :, cols].astype(F32)
        vc = vc_ref[0, :, cols]
        s_f0.append(_dot_tn((kc * jnp.exp((Lc - 1.0 - crow) * lg_f)).astype(BF16), vc))
        s_b0.append(_dot_tn((kc * jnp.exp(crow * lg_b)).astype(BF16), vc))

    def fwd(c, s_f):
        base = pl.multiple_of(c * CHUNK, CHUNK)
        new = []
        for h, hd in enumerate(heads):
            sf_scr[c, h] = s_f[h]
            kk = k_ref[0, pl.ds(base, CHUNK), hd["cols"]].astype(F32)
            vv = v_ref[0, pl.ds(base, CHUNK), hd["cols"]]
            new.append(hd["dc_f"] * s_f[h] + _dot_tn((kk * hd["dk_f"]).astype(BF16), vv))
        return tuple(new)

    lax.fori_loop(0, nch, fwd, tuple(s_f0), unroll=RET_UNROLL)

    def bwd(i, s_bs):
        c = nch - 1 - i
        base = pl.multiple_of(c * CHUNK, CHUNK)
        new = []
        for h, hd in enumerate(heads):
            qq = q_ref[0, pl.ds(base, CHUNK), hd["cols"]]
            kk = k_ref[0, pl.ds(base, CHUNK), hd["cols"]]
            vv = v_ref[0, pl.ds(base, CHUNK), hd["cols"]]
            s_b = s_bs[h]
            scores = (_dot_nt(qq, kk) * hd["dmat"]).astype(BF16)
            y = (_dot(scores, vv)
                 + _dot(qq, sf_scr[c, h].astype(BF16)) * hd["dq_f"]
                 + _dot(qq, s_b.astype(BF16)) * hd["dq_b"])
            mu = jnp.mean(y, axis=-1, keepdims=True)
            yc = y - mu
            var = jnp.mean(yc * yc, axis=-1, keepdims=True)
            yn = (yc * lax.rsqrt(var + EPS)) * gn_ref[:, hd["cols"]]
            gate = _silu(g_ref[0, pl.ds(base, CHUNK), hd["cols"]].astype(F32))
            y_ref[0, pl.ds(base, CHUNK), hd["cols"]] = (yn * gate).astype(BF16)
            new.append(hd["dc_b"] * s_b + _dot_tn((kk.astype(F32) * hd["dk_b"]).astype(BF16), vv))
        return tuple(new)

    lax.fori_loop(0, nch, bwd, tuple(s_b0), unroll=RET_UNROLL)


def _retention(q, k, v, g, kc, vc, decay_f, decay_b, gn_w):
    b, L, w = q.shape
    Lc = kc.shape[1]
    nch = L // CHUNK
    per_b = lambda n: pl.BlockSpec((1, n, w), lambda i: (i, 0, 0))
    return pl.pallas_call(
        _ret_kernel,
        grid=(b,),
        in_specs=[per_b(L), per_b(L), per_b(L), per_b(L), per_b(Lc), per_b(Lc),
                  _const_spec((1, w)), _const_spec((1, w)), _const_spec((1, w))],
        out_specs=per_b(L),
        out_shape=jax.ShapeDtypeStruct((b, L, w), BF16),
        scratch_shapes=[
            pltpu.VMEM((nch, RET_HEADS, RET_DK, RET_DK), F32),
        ],
        compiler_params=pltpu.CompilerParams(vmem_limit_bytes=48 * 1024 * 1024),
        name="retention",
    )(q, k, v, g, kc, vc, decay_f, decay_b, gn_w)


def _outproj_router_kernel(yr_ref, ys_ref, x_ref, g1_ref, sh2_ref, sc2_ref, npost_ref, npre_ref,
                           wor_ref, wos_ref, wr_ref, br_ref, tri_ref,
                           x1_ref, h2_ref, route_ref, cnt_ref, slots_ref,
                           wcat, carry):
    i = pl.program_id(0)

    @pl.when(i == 0)
    def _():
        wr = wr_ref[...]
        hi = wr.astype(BF16)
        wcat[:, :LANES] = hi
        wcat[:, LANES:] = (wr - hi.astype(F32)).astype(BF16)
        carry[...] = jnp.zeros_like(carry)

    tm = tri_ref.shape[0]
    running = carry[...]
    for sub in range(x_ref.shape[0] // tm):
        running = _route_subtile(sub, tm, running, yr_ref, ys_ref, x_ref, g1_ref, sh2_ref, sc2_ref, npost_ref,
                                 npre_ref, wor_ref, wos_ref, br_ref, tri_ref, x1_ref, h2_ref, route_ref,
                                 slots_ref, wcat)
    carry[...] = running
    cnt_ref[...] = running


def _route_subtile(sub, tm, running, yr_ref, ys_ref, x_ref, g1_ref, sh2_ref, sc2_ref, npost_ref, npre_ref,
                   wor_ref, wos_ref, br_ref, tri_ref, x1_ref, h2_ref, route_ref, slots_ref, wcat):
    rows = slice(sub * tm, (sub + 1) * tm)
    y = _dot(yr_ref[rows, :], wor_ref[...]) + _dot(ys_ref[rows, :], wos_ref[...])
    ms = jnp.mean(y * y, axis=-1, keepdims=True)
    x1 = x_ref[rows, :] + g1_ref[0] * ((y * lax.rsqrt(ms + EPS)) * npost_ref[...])
    x1_ref[rows, :] = x1
    h2 = _norm_mod(x1, npre_ref[...], sc2_ref[0], sh2_ref[0])
    h2_ref[rows, :] = h2

    h_hi = h2.astype(BF16)
    h_lo = (h2 - h_hi.astype(F32)).astype(BF16)
    both = _dot(h_hi, wcat[...])
    lg = both[:, :LANES] + both[:, LANES:] + _dot(h_lo, wcat[:, :LANES]) + br_ref[...]

    lane = lax.broadcasted_iota(jnp.int32, (tm, LANES), 1)
    lane_f = lane.astype(F32)
    is_grp = (lane >= N_EXPERTS) & (lane < N_EXPERTS + MOE_GROUPS)
    gl = jnp.where(is_grp, lg, NEG_BIG)
    mg = jnp.max(gl, axis=-1, keepdims=True)
    grp_lane = jnp.min(jnp.where(gl == mg, lane_f, 1e9), axis=-1, keepdims=True)
    p_g = 1.0 / jnp.sum(jnp.where(is_grp, jnp.exp(gl - mg), 0.0), axis=-1, keepdims=True)
    first = (grp_lane - N_EXPERTS) * EXPERTS_PER_GROUP
    in_grp = (lane_f >= first) & (lane_f < first + EXPERTS_PER_GROUP)
    el = jnp.where(in_grp, lg, NEG_BIG)
    t1 = jnp.max(el, axis=-1, keepdims=True)
    i1 = jnp.min(jnp.where(el == t1, lane_f, 1e9), axis=-1, keepdims=True)
    el2 = jnp.where(lane_f == i1, NEG_BIG, el)
    t2 = jnp.max(el2, axis=-1, keepdims=True)
    i2 = jnp.min(jnp.where(el2 == t2, lane_f, 1e9), axis=-1, keepdims=True)
    s = jnp.exp(t2 - t1)
    w1 = p_g / (1.0 + s)
    w2 = p_g * s / (1.0 + s)

    oh1 = (lane_f == i1)
    oh2 = (lane_f == i2)
    oh = (oh1 | oh2).astype(BF16)
    before = _dot(tri_ref[...], oh) + running
    rank1 = jnp.sum(jnp.where(oh1, before, 0.0), axis=-1, keepdims=True)
    rank2 = jnp.sum(jnp.where(oh2, before, 0.0), axis=-1, keepdims=True)

    route_ref[rows, :] = jnp.where(lane == 0, w1, jnp.where(lane == 1, w2, 0.0))

    row = lax.broadcasted_iota(jnp.int32, (tm, LANES), 0)
    on_diag = (row % LANES) == lane
    per = tm // LANES
    n_sub = x_ref.shape[0] // tm
    for qi, col in enumerate((i1, i2, rank1, rank2)):
        picked = jnp.where(on_diag, col, 0.0)
        dense = jnp.sum(picked.reshape(per, LANES, LANES), axis=1).astype(jnp.int32)
        slots_ref[0, (qi * n_sub + sub) * per:(qi * n_sub + sub + 1) * per, :] = dense
    return running + jnp.sum(oh.astype(F32), axis=0, keepdims=True)


def _outproj_router(yr, ys, x2, mod3, npost, npre, wo_r, wo_s, w_router, b_router, seq_len):
    T, d = x2.shape
    tm = TM_OUT
    per_seq = seq_len // tm
    rw = yr.shape[1]
    sub = TM_ROUTE_SUB
    tri = (jnp.arange(sub)[:, None] > jnp.arange(sub)[None, :]).astype(BF16)
    tok = lambda w: pl.BlockSpec((tm, w), lambda i: (i, 0))
    modv = lambda k: pl.BlockSpec((1, 1, d), lambda i: (i // per_seq, 0, k))
    return pl.pallas_call(
        _outproj_router_kernel,
        grid=(T // tm,),
        in_specs=[
            tok(rw), tok(rw), tok(d), modv(2), modv(3), modv(4),
            _const_spec((1, d)), _const_spec((1, d)),
            _const_spec(wo_r.shape), _const_spec(wo_s.shape), _const_spec(w_router.shape), _const_spec((1, LANES)),
            _const_spec((sub, sub)),
        ],
        out_specs=[tok(d), tok(d), tok(LANES), _const_spec((1, LANES)),
                   pl.BlockSpec((1, 2 * TOP_K * (tm // LANES), LANES), lambda i: (i, 0, 0))],
        out_shape=[jax.ShapeDtypeStruct((T, d), F32), jax.ShapeDtypeStruct((T, d), F32),
                   jax.ShapeDtypeStruct((T, LANES), F32), jax.ShapeDtypeStruct((1, LANES), F32),
                   jax.ShapeDtypeStruct((T // tm, 2 * TOP_K * (tm // LANES), LANES), jnp.int32)],
        scratch_shapes=[pltpu.VMEM((d, 2 * LANES), BF16), pltpu.VMEM((1, LANES), F32)],
        compiler_params=pltpu.CompilerParams(dimension_semantics=("arbitrary",),
                                             vmem_limit_bytes=48 * 1024 * 1024),
        name="outproj_router",
    )(yr, ys, x2, mod3, mod3, mod3, npost, npre, wo_r, wo_s, w_router, b_router, tri)


def _slot_rows_kernel(start_ref, slots_ref, o_ref):
    half = slots_ref.shape[1] // 2
    expert = slots_ref[:, :half, :]
    row = slots_ref[:, half:, :]
    for e in range(N_EXPERTS):
        row = row + jnp.where(expert == e, start_ref[e], 0)
    for r in range(half):
        o_ref[:, 0, r * LANES:(r + 1) * LANES] = row[:, r, :]


def _slot_rows(pad_start, slots):
    nt, rows, _ = slots.shape
    grid_spec = pltpu.PrefetchScalarGridSpec(
        num_scalar_prefetch=1,
        grid=(1,),
        in_specs=[pl.BlockSpec(slots.shape, lambda i, ps: (0, 0, 0))],
        out_specs=pl.BlockSpec((nt, 1, rows // 2 * LANES), lambda i, ps: (0, 0, 0)),
    )
    return pl.pallas_call(
        _slot_rows_kernel,
        grid_spec=grid_spec,
        out_shape=jax.ShapeDtypeStruct((nt, 1, rows // 2 * LANES), jnp.int32),
        name="slot_rows",
    )(pad_start, slots)


def _dispatch_kernel(pad_end_ref, dest_ref, h_hbm, xs_hbm, zbuf, hbuf, sem, zsem, in_sem):
    i = pl.program_id(0)
    nt = pl.num_programs(0)
    td = hbuf.shape[1]
    mb = zbuf.shape[0]

    @pl.when(i == 0)
    def _():
        zbuf[...] = jnp.zeros_like(zbuf)

        def fill(e):
            return pltpu.make_async_copy(zbuf, xs_hbm.at[pl.ds(pl.multiple_of(pad_end_ref[e] - mb, mb), mb)], zsem)

        def has_rows(e):
            return pad_end_ref[e] > jnp.where(e == 0, 0, pad_end_ref[jnp.maximum(e - 1, 0)])

        def start(e, carry):
            @pl.when(has_rows(e))
            def _():
                fill(e).start()
            return carry

        def wait(e, carry):
            @pl.when(has_rows(e))
            def _():
                fill(e).wait()
            return carry

        def tail_fill(blk):
            return pltpu.make_async_copy(zbuf, xs_hbm.at[pl.ds(pl.multiple_of(blk * mb, mb), mb)], zsem)

        def tail_start(blk, carry):
            @pl.when(blk * mb >= pad_end_ref[N_EXPERTS - 1])
            def _():
                tail_fill(blk).start()
            return carry

        def tail_wait(blk, carry):
            @pl.when(blk * mb >= pad_end_ref[N_EXPERTS - 1])
            def _():
                tail_fill(blk).wait()
            return carry

        n_blocks = xs_hbm.shape[0] // mb
        lax.fori_loop(0, N_EXPERTS, start, 0)
        lax.fori_loop(0, n_blocks, tail_start, 0)
        lax.fori_loop(0, N_EXPERTS, wait, 0)
        lax.fori_loop(0, n_blocks, tail_wait, 0)

    def fetch(t):
        return pltpu.make_async_copy(h_hbm.at[pl.ds(pl.multiple_of(t * td, td), td)], hbuf.at[t % DISPATCH_RING],
                                     in_sem.at[t % DISPATCH_RING])

    @pl.when(i == 0)
    def _():
        fetch(i).start()

    @pl.when(i + 1 < nt)
    def _():
        fetch(i + 1).start()

    fetch(i).wait()
    for slot in range(DISPATCH_RING):
        @pl.when(i % DISPATCH_RING == slot)
        def _(slot=slot):
            _issue_row_copies(dest_ref, td, lambda kk, j, d: pltpu.make_async_copy(
                hbuf.at[slot, pl.ds(j, 1)], xs_hbm.at[pl.ds(d, 1)], sem.at[slot]))

    def drain(t):
        for kk in range(TOP_K):
            pltpu.make_async_copy(hbuf.at[t % DISPATCH_RING], xs_hbm.at[pl.ds(0, td)],
                                  sem.at[t % DISPATCH_RING]).wait()

    @pl.when(i > 0)
    def _():
        drain(i - 1)

    @pl.when(i == nt - 1)
    def _():
        drain(i)


def _issue_row_copies(dest_ref, n_tok, row_copy):
    def issue(j, carry):
        for kk in range(TOP_K):
            row_copy(kk, j, dest_ref[0, 0, kk * n_tok + j]).start()
        return carry

    lax.fori_loop(0, n_tok, issue, 0, unroll=DMA_UNROLL)


def _dispatch(pad_end, dest, h2, cap):
    T, d = h2.shape
    td = TD_DISPATCH
    nt = T // td
    grid_spec = pltpu.PrefetchScalarGridSpec(
        num_scalar_prefetch=1,
        grid=(nt,),
        in_specs=[
            pl.BlockSpec((1, 1, TOP_K * td), lambda i, pe: (i, 0, 0), memory_space=pltpu.SMEM),
            pl.BlockSpec(memory_space=pl.ANY),
        ],
        out_specs=pl.BlockSpec(memory_space=pl.ANY),
        scratch_shapes=[pltpu.VMEM((MB_EXPERT, d), h2.dtype), pltpu.VMEM((DISPATCH_RING, td, d), h2.dtype),
                        pltpu.SemaphoreType.DMA((DISPATCH_RING,)), pltpu.SemaphoreType.DMA(()),
                        pltpu.SemaphoreType.DMA((DISPATCH_RING,))],
    )
    return pl.pallas_call(
        _dispatch_kernel,
        grid_spec=grid_spec,
        out_shape=jax.ShapeDtypeStruct((cap, d), h2.dtype),
        compiler_params=pltpu.CompilerParams(dimension_semantics=("arbitrary",)),
        name="dispatch",
    )(pad_end, dest, h2)


def _expert_kernel(be_ref, first_ref, slot_ref, next_ref, nused_ref, xs_ref, wg_hbm, wu_hbm, wd_hbm, y_ref,
                   wg_f, wu_f, wd_f, wg_b, wu_b, wd_b, sem):
    i = pl.program_id(0)

    def fetch(e, s):
        return [pltpu.make_async_copy(src.at[e], dst.at[s], sem.at[s, k])
                for k, (src, dst) in enumerate(((wg_hbm, wg_f), (wu_hbm, wu_f), (wd_hbm, wd_f)))]

    @pl.when(i == 0)
    def _():
        for c in fetch(be_ref[0], slot_ref[0]):
            c.start()

    @pl.when(first_ref[i] == 1)
    def _():
        s = slot_ref[i]

        @pl.when(next_ref[i] >= 0)
        def _():
            for c in fetch(next_ref[i], 1 - s):
                c.start()

        for c in fetch(be_ref[i], s):
            c.wait()
        wg_b[...] = wg_f[s].astype(BF16)
        wu_b[...] = wu_f[s].astype(BF16)
        wd_b[...] = wd_f[s].astype(BF16)

    @pl.when(i < nused_ref[0])
    def _():
        xb = xs_ref[...].astype(BF16)
        hid = (_silu(_dot(xb, wg_b[...])) * _dot(xb, wu_b[...])).astype(BF16)
        y_ref[...] = _dot(hid, wd_b[...])

    @pl.when(i >= nused_ref[0])
    def _():
        y_ref[...] = jnp.zeros_like(y_ref)


def _expert_plan(padded, pad_end, n_blocks, mb):
    n_used = (pad_end[-1:] // mb).astype(jnp.int32)
    blk_start = jnp.arange(n_blocks, dtype=jnp.int32) * mb
    experts = jnp.arange(N_EXPERTS, dtype=jnp.int32)
    blk_expert = jnp.minimum(jnp.sum((pad_end[None, :] <= blk_start[:, None]).astype(jnp.int32), axis=1),
                             N_EXPERTS - 1)
    prev = jnp.concatenate([jnp.full((1,), -1, jnp.int32), blk_expert[:-1]])
    first = ((blk_start < pad_end[-1]) & (blk_expert != prev)).astype(jnp.int32)
    slot = (jnp.cumsum(first) - 1) % 2
    later = jnp.where((padded > 0)[None, :] & (experts[None, :] > experts[:, None]), experts[None, :], N_EXPERTS)
    next_nonempty = jnp.min(later, axis=1)
    next_nonempty = jnp.where(next_nonempty == N_EXPERTS, -1, next_nonempty)
    nxt = jnp.sum(jnp.where(blk_expert[:, None] == experts[None, :], next_nonempty[None, :], 0), axis=1)
    return [a.astype(jnp.int32) for a in (blk_expert, first, slot, nxt, n_used)]


def _experts(plan, xs, w_gate, w_up, w_down):
    cap, dp = xs.shape
    n_exp, d, de = w_gate.shape
    assert dp == d
    mb = MB_EXPERT
    grid_spec = pltpu.PrefetchScalarGridSpec(
        num_scalar_prefetch=len(plan),
        grid=(cap // mb,),
        in_specs=[
            pl.BlockSpec((mb, dp), lambda i, be, fi, sl, nx, nu: (jnp.minimum(i, nu[0] - 1), 0)),
            pl.BlockSpec(memory_space=pl.ANY), pl.BlockSpec(memory_space=pl.ANY), pl.BlockSpec(memory_space=pl.ANY),
        ],
        out_specs=pl.BlockSpec((mb, dp), lambda i, be, fi, sl, nx, nu: (i, 0)),
        scratch_shapes=[pltpu.VMEM((2, d, de), F32), pltpu.VMEM((2, d, de), F32), pltpu.VMEM((2, de, d), F32),
                        pltpu.VMEM((d, de), BF16), pltpu.VMEM((d, de), BF16), pltpu.VMEM((de, d), BF16),
                        pltpu.SemaphoreType.DMA((2, 3))],
    )
    return pl.pallas_call(
        _expert_kernel,
        grid_spec=grid_spec,
        out_shape=jax.ShapeDtypeStruct((cap, dp), xs.dtype),
        compiler_params=pltpu.CompilerParams(dimension_semantics=("arbitrary",),
                                             vmem_limit_bytes=48 * 1024 * 1024),
        name="experts",
    )(*plan, xs, w_gate, w_up, w_down)


def _combine_kernel(dest_ref, dest_next_ref, route_ref, x1_ref, g2_ref, nw_ref, yb_hbm, o_ref, buf, sem):
    i = pl.program_id(0)
    nt = pl.num_programs(0)
    tf = x1_ref.shape[0]
    stage = i % 2

    def gather(refs, st):
        _issue_row_copies(refs, tf, lambda kk, j, d: pltpu.make_async_copy(
            yb_hbm.at[pl.ds(d, 1)], buf.at[st, kk, pl.ds(j, 1)], sem.at[st]))

    @pl.when(i == 0)
    def _():
        gather(dest_ref, 0)

    for st in range(2):
        @pl.when((i + 1 < nt) & (stage != st))
        def _(st=st):
            gather(dest_next_ref, st)

    for kk in range(TOP_K):
        pltpu.make_async_copy(yb_hbm.at[pl.ds(0, tf)], buf.at[stage, kk], sem.at[stage]).wait()

    route = route_ref[...]
    out = route[:, 0:1] * buf[stage, 0] + route[:, 1:2] * buf[stage, 1]
    ms = jnp.mean(out * out, axis=-1, keepdims=True)
    o_ref[...] = x1_ref[...] + g2_ref[0] * ((out * lax.rsqrt(ms + EPS)) * nw_ref[...])


def _combine(dest, route, x1, mod3, nw, yb, seq_len):
    T, d = x1.shape
    tf = TF_COMBINE
    nt = T // tf
    per_seq = seq_len // tf
    slot_spec = functools.partial(pl.BlockSpec, (1, 1, TOP_K * tf), memory_space=pltpu.SMEM)
    return pl.pallas_call(
        _combine_kernel,
        grid=(nt,),
        in_specs=[
            slot_spec(lambda i: (i, 0, 0)),
            slot_spec(lambda i: (jnp.minimum(i + 1, nt - 1), 0, 0)),
            pl.BlockSpec((tf, LANES), lambda i: (i, 0)),
            pl.BlockSpec((tf, d), lambda i: (i, 0)),
            pl.BlockSpec((1, 1, d), lambda i: (i // per_seq, 0, 5)),
            _const_spec((1, d)),
            pl.BlockSpec(memory_space=pl.ANY),
        ],
        out_specs=pl.BlockSpec((tf, d), lambda i: (i, 0)),
        out_shape=jax.ShapeDtypeStruct((T, d), F32),
        scratch_shapes=[pltpu.VMEM((2, TOP_K, tf, yb.shape[1]), yb.dtype), pltpu.SemaphoreType.DMA((2,))],
        compiler_params=pltpu.CompilerParams(dimension_semantics=("arbitrary",),
                                             vmem_limit_bytes=48 * 1024 * 1024),
        name="combine",
    )(dest, dest, route, x1, mod3, nw, yb)


def _rope_tables(L, n_heads):
    quarter = RET_DK // 4
    freqs = ROPE_BASE ** (-jnp.arange(quarter, dtype=F32) / quarter)
    t = jnp.arange(L)
    ang_r = (t // GRID_W).astype(F32)[:, None] * freqs
    ang_c = (t % GRID_W).astype(F32)[:, None] * freqs
    cos = jnp.concatenate([jnp.cos(ang_r)] * 2 + [jnp.cos(ang_c)] * 2, axis=-1)
    sin = jnp.concatenate([-jnp.sin(ang_r), jnp.sin(ang_r), -jnp.sin(ang_c), jnp.sin(ang_c)], axis=-1)
    return jnp.tile(cos, (1, n_heads)), jnp.tile(sin, (1, n_heads))


def _lane_pad(v, width=LANES):
    return jnp.pad(v, [(0, 0)] * (v.ndim - 1) + [(0, width - v.shape[-1])])


def kernel(x, c, ctx, c_ctx, w_mod, b_mod, norm_pre_mix, norm_post_mix, norm_pre_ffn, norm_post_ffn, w_in, w_out, ret_decay_f, ret_decay_b, ret_gn_w, ssd_conv_w, ssd_conv_b, ssd_dt_bias_f, ssd_dt_bias_b, ssd_a_log_f, ssd_a_log_b, ssd_d, ssd_norm_w, moe_w_rg, moe_b_rg, moe_w_re, moe_b_re, moe_w_gate, moe_w_up, moe_w_down):
    b, L, d = x.shape
    assert w_mod.shape[0] == 1, "single layer: context outputs are never needed"
    assert TM_OUT == TD_DISPATCH == TF_COMBINE, "router, dispatch and combine share one slot-row layout"
    rw = RET_HEADS * RET_DK
    nconv = SSD_WIDTH + 2 * SSD_GROUPS * SSD_STATE
    T = b * L

    mod_rows = -(-(b + 1) // SUBLANES) * SUBLANES
    c_all = jnp.zeros((mod_rows, d), F32).at[:b].set(c).at[b].set(c_ctx)
    mod3 = _modulation(c_all, w_mod[0], b_mod[0]).reshape(mod_rows, 1, 6 * d)

    wi = w_in[0]
    o = 0
    wq = wi[:, o:o + rw]; o += rw
    wk = wi[:, o:o + rw]; o += rw
    wv = wi[:, o:o + rw]; o += rw
    wg = wi[:, o:o + rw]; o += rw
    wz = wi[:, o:o + SSD_WIDTH]; o += SSD_WIDTH
    wxbc = wi[:, o:o + nconv].astype(BF16); o += nconv
    wdt = _lane_pad(wi[:, o:o + 2 * SSD_HEADS]).astype(BF16)
    wqk = jnp.concatenate([wq, wk], axis=1).astype(BF16)
    wvgz = jnp.concatenate([wv, wg, wz], axis=1).astype(BF16)
    cos_t, sin_t = _rope_tables(L, RET_HEADS)
    nw1 = norm_pre_mix[0].reshape(1, d)

    q, k, v, g, z, xbc, dt = _inproj(x, mod3, nw1, wqk, wvgz, wxbc, wdt, cos_t, sin_t)
    kc, vc, xbcc, dtc = _inproj_ctx(ctx, mod3, b, nw1, wk.astype(BF16), wv.astype(BF16), wxbc, wdt)

    conv_w8 = jnp.pad(ssd_conv_w[0], ((0, SUBLANES - SSD_CONV), (0, 0)))
    dt_bias = _lane_pad(jnp.concatenate([ssd_dt_bias_f[0], ssd_dt_bias_b[0]])[None, :])
    a_log = _lane_pad(jnp.concatenate([ssd_a_log_f[0], ssd_a_log_b[0]])[None, :])
    d_skip = jnp.repeat(ssd_d[0], SSD_HEADDIM)[None, :]
    ys = _ssd(xbc, z, dt, xbcc, dtc, conv_w8, ssd_conv_b[0][None, :], dt_bias, a_log, d_skip,
              ssd_norm_w[0][None, :])

    yr = _retention(q, k, v, g, kc, vc,
                    jnp.repeat(ret_decay_f[0], RET_DK)[None, :], jnp.repeat(ret_decay_b[0], RET_DK)[None, :],
                    ret_gn_w[0][None, :])

    wo = w_out[0].astype(BF16)
    w_router = _lane_pad(jnp.concatenate(
        [jnp.transpose(moe_w_re[0], (1, 0, 2)).reshape(d, N_EXPERTS), moe_w_rg[0]], axis=1))
    b_router = _lane_pad(jnp.concatenate([moe_b_re[0].reshape(-1), moe_b_rg[0]])[None, :])
    x1, h2, route, counts, slots = _outproj_router(
        yr.reshape(T, rw), ys.reshape(T, SSD_WIDTH), x.reshape(T, d), mod3,
        norm_post_mix[0][None, :], norm_pre_ffn[0][None, :], wo[:rw], wo[rw:], w_router, b_router, L)

    mb = MB_EXPERT
    n_blocks = -(-(T * TOP_K + N_EXPERTS * (mb - 1)) // mb)
    cap = n_blocks * mb
    cnt = counts[0, :N_EXPERTS].astype(jnp.int32)
    padded = (cnt + mb - 1) // mb * mb
    pad_end = jnp.cumsum(padded)
    pad_start = pad_end - padded

    dest = _slot_rows(pad_start.astype(jnp.int32), slots)
    xs = _dispatch(pad_end.astype(jnp.int32), dest, h2, cap)
    yb = _experts(_expert_plan(padded, pad_end, n_blocks, mb), xs, moe_w_gate[0], moe_w_up[0], moe_w_down[0])
    out = _combine(dest, route, x1, mod3, norm_post_ffn[0][None, :], yb, L)
    return out.reshape(b, L, d)
```

```python
import functools
import math

import jax
import jax.numpy as jnp
from jax import lax
from jax.experimental import pallas as pl
from jax.experimental.pallas import tpu as pltpu

F32 = jnp.float32
BF16 = jnp.bfloat16

LANES = 128
SUBLANES = 8

EPS = 1e-6
CHUNK = 128
GRID_W = 64
RET_HEADS = 4
RET_DK = 128
ROPE_BASE = 10000.0
SSD_HEADS = 8
SSD_HEADDIM = 64
SSD_GROUPS = 2
SSD_STATE = 128
SSD_WIDTH = SSD_HEADS * SSD_HEADDIM
SSD_CONV = 5
SSD_PAIRS = SSD_WIDTH // LANES
MOE_GROUPS = 4
EXPERTS_PER_GROUP = 8
N_EXPERTS = MOE_GROUPS * EXPERTS_PER_GROUP
TOP_K = 2
CONV_HALO = SUBLANES

TM_PROJ = 512
TM_OUT = 512
TM_ROUTE_SUB = 512
TD_DISPATCH = 512
MB_EXPERT = 256
TF_COMBINE = 512
DMA_UNROLL = 8
DISPATCH_RING = 3
SSD_UNROLL = 2
RET_UNROLL = 2
NEG_BIG = -1e30


def _silu(v):
    return v * jax.nn.sigmoid(v)


def _dot(a, b):
    return jnp.dot(a, b, preferred_element_type=F32)


def _dot_tn(a, b):
    return lax.dot_general(a, b, (((0,), (0,)), ((), ())), preferred_element_type=F32)


def _dot_nt(a, b):
    return lax.dot_general(a, b, (((1,), (1,)), ((), ())), preferred_element_type=F32)


def _mod_kernel(c_ref, w_ref, b_ref, o_ref):
    a = _silu(c_ref[...])
    w = w_ref[...]
    a_hi = a.astype(BF16)
    a_lo = (a - a_hi.astype(F32)).astype(BF16)
    w_hi = w.astype(BF16)
    w_lo = (w - w_hi.astype(F32)).astype(BF16)
    o_ref[...] = _dot(a_hi, w_hi) + _dot(a_lo, w_hi) + _dot(a_hi, w_lo) + b_ref[...]


def _modulation(c_all, w_mod, b_mod):
    rows, d = c_all.shape
    n = w_mod.shape[1]
    return pl.pallas_call(
        _mod_kernel,
        grid=(n // d,),
        in_specs=[
            pl.BlockSpec((rows, d), lambda j: (0, 0)),
            pl.BlockSpec((d, d), lambda j: (0, j)),
            pl.BlockSpec((1, d), lambda j: (0, j)),
        ],
        out_specs=pl.BlockSpec((rows, d), lambda j: (0, j)),
        out_shape=jax.ShapeDtypeStruct((rows, n), F32),
        name="modulation",
    )(c_all, w_mod, b_mod.reshape(1, n))


def _norm_mod(x, nw, sc, sh):
    ms = jnp.mean(x * x, axis=-1, keepdims=True)
    return (x * lax.rsqrt(ms + EPS)) * (nw * (1.0 + sc)) + sh


def _rope(t, cos, sin_signed, first_half):
    width = t.shape[-1]
    quarter = RET_DK // 4
    swapped = jnp.where(first_half, pltpu.roll(t, width - quarter, 1), pltpu.roll(t, quarter, 1))
    return t * cos + swapped * sin_signed


def _inproj_kernel(x_ref, sh_ref, sc_ref, nw_ref, wqk_ref, wvgz_ref, wxbc_ref, wdt_ref, cos_ref, sin_ref,
                   q_ref, k_ref, v_ref, g_ref, z_ref, xbc_ref, dt_ref):
    hb = _norm_mod(x_ref[0], nw_ref[...], sc_ref[0], sh_ref[0]).astype(BF16)
    rw = q_ref.shape[-1]
    qk = _dot(hb, wqk_ref[...])
    cos = cos_ref[...]
    sin = sin_ref[...]
    lane = lax.broadcasted_iota(jnp.int32, cos.shape, 1)
    first_half = (lane % (RET_DK // 2)) < (RET_DK // 4)
    q_ref[0] = _rope(qk[:, :rw], cos, sin, first_half).astype(BF16)
    k_ref[0] = (_rope(qk[:, rw:], cos, sin, first_half) * (RET_DK ** -0.5)).astype(BF16)
    vgz = _dot(hb, wvgz_ref[...])
    v_ref[0] = vgz[:, :rw].astype(BF16)
    g_ref[0] = vgz[:, rw:2 * rw].astype(BF16)
    z_ref[0] = vgz[:, 2 * rw:].astype(BF16)
    xbc_ref[0] = _dot(hb, wxbc_ref[...]).astype(BF16)
    dt_ref[0] = _dot(hb, wdt_ref[...])


def _inproj_ctx_kernel(x_ref, sh_ref, sc_ref, nw_ref, wk_ref, wv_ref, wxbc_ref, wdt_ref,
                       k_ref, v_ref, xbc_ref, dt_ref):
    hb = _norm_mod(x_ref[0], nw_ref[...], sc_ref[0], sh_ref[0]).astype(BF16)
    k_ref[0] = (_dot(hb, wk_ref[...]) * (RET_DK ** -0.5)).astype(BF16)
    v_ref[0] = _dot(hb, wv_ref[...]).astype(BF16)
    xbc_ref[0] = _dot(hb, wxbc_ref[...]).astype(BF16)
    dt_ref[0] = _dot(hb, wdt_ref[...])


def _const_spec(shape):
    nd = len(shape)
    return pl.BlockSpec(shape, lambda *_: (0,) * nd)


def _inproj(x, mod3, nw, wqk, wvgz, wxbc, wdt, cos_t, sin_t):
    b, L, d = x.shape
    tm = min(TM_PROJ, L)
    rw = wqk.shape[1] // 2
    tok = lambda w: pl.BlockSpec((1, tm, w), lambda i, j: (i, j, 0))
    out_bf = lambda w: jax.ShapeDtypeStruct((b, L, w), BF16)
    return pl.pallas_call(
        _inproj_kernel,
        grid=(b, L // tm),
        in_specs=[
            tok(d),
            pl.BlockSpec((1, 1, d), lambda i, j: (i, 0, 0)),
            pl.BlockSpec((1, 1, d), lambda i, j: (i, 0, 1)),
            _const_spec((1, d)),
            _const_spec(wqk.shape), _const_spec(wvgz.shape), _const_spec(wxbc.shape), _const_spec(wdt.shape),
            pl.BlockSpec((tm, rw), lambda i, j: (j, 0)),
            pl.BlockSpec((tm, rw), lambda i, j: (j, 0)),
        ],
        out_specs=[tok(rw), tok(rw), tok(rw), tok(rw), tok(rw), tok(wxbc.shape[1]), tok(LANES)],
        out_shape=[out_bf(rw), out_bf(rw), out_bf(rw), out_bf(rw), out_bf(rw), out_bf(wxbc.shape[1]),
                   jax.ShapeDtypeStruct((b, L, LANES), F32)],
        compiler_params=pltpu.CompilerParams(vmem_limit_bytes=48 * 1024 * 1024),
        name="inproj",
    )(x, mod3, mod3, nw, wqk, wvgz, wxbc, wdt, cos_t, sin_t)


def _inproj_ctx(ctx, mod3, ctx_row, nw, wk, wv, wxbc, wdt):
    b, L, d = ctx.shape
    tm = min(TM_PROJ, L)
    rw = wk.shape[1]
    tok = lambda w: pl.BlockSpec((1, tm, w), lambda i, j: (i, j, 0))
    out_bf = lambda w: jax.ShapeDtypeStruct((b, L, w), BF16)
    return pl.pallas_call(
        _inproj_ctx_kernel,
        grid=(b, L // tm),
        in_specs=[
            tok(d),
            pl.BlockSpec((1, 1, d), lambda i, j: (ctx_row, 0, 0)),
            pl.BlockSpec((1, 1, d), lambda i, j: (ctx_row, 0, 1)),
            _const_spec((1, d)),
            _const_spec(wk.shape), _const_spec(wv.shape), _const_spec(wxbc.shape), _const_spec(wdt.shape),
        ],
        out_specs=[tok(rw), tok(rw), tok(wxbc.shape[1]), tok(LANES)],
        out_shape=[out_bf(rw), out_bf(rw), out_bf(wxbc.shape[1]), jax.ShapeDtypeStruct((b, L, LANES), F32)],
        compiler_params=pltpu.CompilerParams(vmem_limit_bytes=48 * 1024 * 1024),
        name="inproj_ctx",
    )(ctx, mod3, mod3, nw, wk, wv, wxbc, wdt)


def _ssd_kernel(xbc_ref, z_ref, dt_ref, xbcc_ref, dtc_ref, cw_ref, cb_ref, dtb_ref, alog_ref, dsk_ref, nw_ref,
                y_ref,
                xpad, xpadc, u, uc, dtv, dav, dtcv, dacv, sf_scr, kb_scr, acum, ecum, dec_scr):
    L = xbc_ref.shape[1]
    Lc = xbcc_ref.shape[1]
    nch = L // CHUNK
    nchc = Lc // CHUNK
    win = CHUNK + 2 * CONV_HALO
    nconv = xbc_ref.shape[2]
    nh = SSD_HEADS

    def conv_pass(src_ref, pad_ref, dst_ref, n_chunks, length):
        zeros = jnp.zeros((CONV_HALO, nconv), F32)
        pad_ref[0:CONV_HALO, :] = zeros
        pad_ref[CONV_HALO + length:2 * CONV_HALO + length, :] = zeros
        pad_ref[CONV_HALO:CONV_HALO + length, :] = src_ref[0].astype(F32)

        def chunk(c, carry):
            base = pl.multiple_of(c * CHUNK, CHUNK)
            for cb_i in range(nconv // LANES):
                cols = slice(cb_i * LANES, (cb_i + 1) * LANES)
                w = pad_ref[pl.ds(base, win), cols]
                acc = cb_ref[:, cols] + w[CONV_HALO:CONV_HALO + CHUNK] * cw_ref[SSD_CONV // 2:SSD_CONV // 2 + 1, cols]
                for j in range(SSD_CONV):
                    if j == SSD_CONV // 2:
                        continue
                    shifted = pltpu.roll(w, (SSD_CONV // 2 - j) % win, 0)
                    acc = acc + shifted[CONV_HALO:CONV_HALO + CHUNK] * cw_ref[j:j + 1, cols]
                dst_ref[pl.ds(base, CHUNK), cols] = _silu(acc).astype(BF16)
            return carry

        lax.fori_loop(0, n_chunks, chunk, 0)

    conv_pass(xbcc_ref, xpadc, uc, nchc, Lc)
    conv_pass(xbc_ref, xpad, u, nch, L)

    a_neg = -jnp.exp(alog_ref[...])
    dtv[...] = jax.nn.softplus(dt_ref[0] + dtb_ref[...])
    dav[...] = dtv[...] * a_neg
    dtcv[...] = jax.nn.softplus(dtc_ref[0] + dtb_ref[...])
    dacv[...] = dtcv[...] * a_neg

    row_i = lax.broadcasted_iota(jnp.int32, (CHUNK, CHUNK), 0)
    col_i = lax.broadcasted_iota(jnp.int32, (CHUNK, CHUNK), 1)
    causal = col_i <= row_i
    lo_half = col_i < SSD_HEADDIM
    fwd_lane = col_i < nh
    head_of = lax.broadcasted_iota(jnp.int32, (CHUNK, SSD_WIDTH), 1) // SSD_HEADDIM
    src_col = lax.broadcasted_iota(jnp.int32, (CHUNK, SSD_WIDTH), 0)
    exp_f = (head_of == src_col).astype(BF16)
    exp_b = (head_of == src_col - nh).astype(BF16)
    exp_fb = jnp.concatenate([exp_f, exp_b], axis=1)

    def split3(v):
        hi = v.astype(BF16)
        r1 = v - hi.astype(F32)
        mid = r1.astype(BF16)
        return hi, mid, (r1 - mid.astype(F32)).astype(BF16)

    def times_onehot(v, m, passes=3):
        parts = split3(v)[:passes]
        acc = _dot(parts[0], m)
        for part in parts[1:]:
            acc = acc + _dot(part, m)
        return acc

    def colb(mat, r):
        return jnp.broadcast_to(mat[:, r:r + 1], (CHUNK, CHUNK))

    def pair_sel(a, b_):
        return jnp.where(lo_half, a, b_)

    gw = 2 * LANES

    def chunk_terms(u_ref, dt_s, da_s, base):
        dt = dt_s[pl.ds(base, CHUNK), :]
        da = da_s[pl.ds(base, CHUNK), :]
        acol = da
        for step in (1, 2, 4, 8, 16, 32, 64):
            acol = acol + jnp.where(row_i >= step, pltpu.roll(acol, step, 0), 0.0)
        ecol = acol - da
        last = acol[CHUNK - 1:CHUNK, :]
        wgt = jnp.where(fwd_lane, jnp.exp(last - acol), jnp.exp(ecol)) * dt
        scale = jnp.where(fwd_lane, jnp.exp(acol), jnp.exp(last - ecol))
        wide = times_onehot(jnp.concatenate([wgt, scale], axis=0), exp_fb, passes=1)
        dec = times_onehot(jnp.broadcast_to(jnp.exp(last), (SUBLANES, LANES)), exp_fb)[0:1]
        xs = u_ref[pl.ds(base, CHUNK), 0:SSD_WIDTH].astype(F32)
        kmats = []
        for g in range(SSD_GROUPS):
            xw = jnp.concatenate([xs[:, g * gw:(g + 1) * gw] * wide[:CHUNK, g * gw:(g + 1) * gw],
                                  xs[:, g * gw:(g + 1) * gw] * wide[:CHUNK, SSD_WIDTH + g * gw:SSD_WIDTH + (g + 1) * gw]],
                                 axis=1).astype(BF16)
            bm = u_ref[pl.ds(base, CHUNK), SSD_WIDTH + g * SSD_STATE:SSD_WIDTH + (g + 1) * SSD_STATE]
            kmats.append(_dot_tn(bm, xw))
        return dt, acol, ecol, wide[CHUNK:], dec, kmats

    def advance(s, dec, kmats, backward):
        off = SSD_WIDTH if backward else 0
        koff = gw if backward else 0
        return [dec[:, off + g * gw:off + (g + 1) * gw] * s[g] + kmats[g][:, koff:koff + gw]
                for g in range(SSD_GROUPS)]

    ctx_terms = [chunk_terms(uc, dtcv, dacv, c * CHUNK) for c in range(nchc)]
    s_f0 = [jnp.zeros((SSD_STATE, gw), F32) for _ in range(SSD_GROUPS)]
    for c in range(nchc):
        s_f0 = advance(s_f0, ctx_terms[c][4], ctx_terms[c][5], False)
    s_b0 = [jnp.zeros((SSD_STATE, gw), F32) for _ in range(SSD_GROUPS)]
    for c in reversed(range(nchc)):
        s_b0 = advance(s_b0, ctx_terms[c][4], ctx_terms[c][5], True)

    def prep(c, carry):
        base = pl.multiple_of(c * CHUNK, CHUNK)
        dt, acol, ecol, scale, dec, kmats = chunk_terms(u, dtv, dav, base)
        acum[pl.ds(base, CHUNK), :] = acol
        ecum[pl.ds(base, CHUNK), :] = ecol
        xpad[pl.ds(base, CHUNK), :] = scale
        dec_scr[pl.ds(pl.multiple_of(c * SUBLANES, SUBLANES), SUBLANES), :] = jnp.broadcast_to(dec, (SUBLANES, 2 * SSD_WIDTH))
        for g in range(SSD_GROUPS):
            sf_scr[c, g] = kmats[g][:, :gw]
            kb_scr[c, g] = kmats[g][:, gw:]
        return carry

    lax.fori_loop(0, nch, prep, 0)

    def chunk_dec(c):
        return dec_scr[pl.ds(pl.multiple_of(c * SUBLANES, SUBLANES), 1), :]

    def fwd(c, s_old):
        dec = chunk_dec(c)
        new = []
        for g in range(SSD_GROUPS):
            new.append(dec[:, g * gw:(g + 1) * gw] * s_old[g] + sf_scr[c, g])
            sf_scr[c, g] = s_old[g]
        return tuple(new)

    lax.fori_loop(0, nch, fwd, tuple(s_f0))

    def bwd(i, s_b):
        c = nch - 1 - i
        base = pl.multiple_of(c * CHUNK, CHUNK)
        acol = acum[pl.ds(base, CHUNK), :]
        ecol = ecum[pl.ds(base, CHUNK), :]
        arow = acol.T
        erow = ecol.T
        dt_t = dtv[pl.ds(base, CHUNK), :].T
        scale = xpad[pl.ds(base, CHUNK), :]
        ys = []
        for g in range(SSD_GROUPS):
            bm = u[pl.ds(base, CHUNK), SSD_WIDTH + g * SSD_STATE:SSD_WIDTH + (g + 1) * SSD_STATE]
            cm = u[pl.ds(base, CHUNK), SSD_WIDTH + (SSD_GROUPS + g) * SSD_STATE:SSD_WIDTH + (SSD_GROUPS + g + 1) * SSD_STATE]
            cbm = _dot_nt(cm, bm)
            cs_f = _dot(cm, sf_scr[c, g].astype(BF16))
            cs_b = _dot(cm, s_b[g].astype(BF16))
            for pp in range(SSD_PAIRS // SSD_GROUPS):
                p = g * (SSD_PAIRS // SSD_GROUPS) + pp
                xs_b = u[pl.ds(base, CHUNK), p * LANES:(p + 1) * LANES]
                y_h = []
                for hh in range(2):
                    r = 2 * p + hh
                    arg = jnp.where(causal, colb(acol, r) - arow[r:r + 1, :],
                                    erow[nh + r:nh + r + 1, :] - colb(ecol, nh + r))
                    coef = jnp.where(causal, dt_t[r:r + 1, :], dt_t[nh + r:nh + r + 1, :])
                    gm = (cbm * (jnp.exp(arg) * coef)).astype(BF16)
                    y_h.append(_dot(gm, xs_b))
                sl = slice(pp * LANES, (pp + 1) * LANES)
                wl = slice(p * LANES, (p + 1) * LANES)
                wlb = slice(SSD_WIDTH + p * LANES, SSD_WIDTH + (p + 1) * LANES)
                ys.append(pair_sel(y_h[0], y_h[1]) + cs_f[:, sl] * scale[:, wl] + cs_b[:, sl] * scale[:, wlb]
                          + dsk_ref[:, wl] * xs_b.astype(F32))
        y = jnp.concatenate(ys, axis=1)
        y = y * _silu(z_ref[0, pl.ds(base, CHUNK), :].astype(F32))
        ms = jnp.mean(y * y, axis=-1, keepdims=True)
        y_ref[0, pl.ds(base, CHUNK), :] = ((y * lax.rsqrt(ms + EPS)) * nw_ref[...]).astype(BF16)
        dec = chunk_dec(c)
        return tuple(dec[:, SSD_WIDTH + g * gw:SSD_WIDTH + (g + 1) * gw] * s_b[g] + kb_scr[c, g]
                     for g in range(SSD_GROUPS))

    lax.fori_loop(0, nch, bwd, tuple(s_b0), unroll=SSD_UNROLL)


def _ssd(xbc, z, dt, xbcc, dtc, conv_w8, conv_b, dt_bias, a_log, d_skip, norm_w):
    b, L, nconv = xbc.shape
    Lc = xbcc.shape[1]
    nch = L // CHUNK
    per_b = lambda n, w: pl.BlockSpec((1, n, w), lambda i: (i, 0, 0))
    return pl.pallas_call(
        _ssd_kernel,
        grid=(b,),
        in_specs=[
            per_b(L, nconv), per_b(L, SSD_WIDTH), per_b(L, LANES), per_b(Lc, nconv), per_b(Lc, LANES),
            _const_spec(conv_w8.shape), _const_spec(conv_b.shape), _const_spec(dt_bias.shape),
            _const_spec(a_log.shape), _const_spec(d_skip.shape), _const_spec(norm_w.shape),
        ],
        out_specs=per_b(L, SSD_WIDTH),
        out_shape=jax.ShapeDtypeStruct((b, L, SSD_WIDTH), BF16),
        scratch_shapes=[
            pltpu.VMEM((L + 2 * CONV_HALO, nconv), F32),
            pltpu.VMEM((Lc + 2 * CONV_HALO, nconv), F32),
            pltpu.VMEM((L, nconv), BF16),
            pltpu.VMEM((Lc, nconv), BF16),
            pltpu.VMEM((L, LANES), F32), pltpu.VMEM((L, LANES), F32),
            pltpu.VMEM((Lc, LANES), F32), pltpu.VMEM((Lc, LANES), F32),
            pltpu.VMEM((nch, SSD_GROUPS, SSD_STATE, 2 * LANES), F32),
            pltpu.VMEM((nch, SSD_GROUPS, SSD_STATE, 2 * LANES), F32),
            pltpu.VMEM((L, LANES), F32), pltpu.VMEM((L, LANES), F32),
            pltpu.VMEM((nch * SUBLANES, 2 * SSD_WIDTH), F32),
        ],
        compiler_params=pltpu.CompilerParams(vmem_limit_bytes=56 * 1024 * 1024),
        name="ssd",
    )(xbc, z, dt, xbcc, dtc, conv_w8, conv_b, dt_bias, a_log, d_skip, norm_w)


def _ret_kernel(q_ref, k_ref, v_ref, g_ref, kc_ref, vc_ref, df_ref, db_ref, gn_ref, y_ref, sf_scr):
    L = q_ref.shape[1]
    Lc = kc_ref.shape[1]
    nch = L // CHUNK
    dk = RET_DK
    row_i = lax.broadcasted_iota(jnp.int32, (CHUNK, dk), 0).astype(F32)
    col_i = lax.broadcasted_iota(jnp.int32, (CHUNK, dk), 1).astype(F32)
    rel = row_i - col_i
    crow = lax.broadcasted_iota(jnp.int32, (Lc, dk), 0).astype(F32)

    heads = []
    s_f0 = []
    s_b0 = []
    for h in range(RET_HEADS):
        cols = slice(h * dk, (h + 1) * dk)
        lg_f = -jnp.exp(df_ref[:, cols])
        lg_b = -jnp.exp(db_ref[:, cols])
        heads.append(dict(
            cols=cols,
            dmat=jnp.where(rel >= 0, jnp.exp(jnp.maximum(rel, 0.0) * lg_f), jnp.exp(jnp.maximum(-rel, 0.0) * lg_b)),
            dq_f=jnp.exp((row_i + 1.0) * lg_f),
            dq_b=jnp.exp((CHUNK - row_i) * lg_b),
            dk_f=jnp.exp((CHUNK - 1.0 - row_i) * lg_f),
            dk_b=jnp.exp(row_i * lg_b),
            dc_f=jnp.exp(CHUNK * lg_f),
            dc_b=jnp.exp(CHUNK * lg_b),
        ))
        kc = kc_ref[0, :, cols].astype(F32)
        vc = vc_ref[0, :, cols]
        s_f0.append(_dot_tn((kc * jnp.exp((Lc - 1.0 - crow) * lg_f)).astype(BF16), vc))
        s_b0.append(_dot_tn((kc * jnp.exp(crow * lg_b)).astype(BF16), vc))

    def fwd(c, s_f):
        base = pl.multiple_of(c * CHUNK, CHUNK)
        new = []
        for h, hd in enumerate(heads):
            sf_scr[c, h] = s_f[h]
            kk = k_ref[0, pl.ds(base, CHUNK), hd["cols"]].astype(F32)
            vv = v_ref[0, pl.ds(base, CHUNK), hd["cols"]]
            new.append(hd["dc_f"] * s_f[h] + _dot_tn((kk * hd["dk_f"]).astype(BF16), vv))
        return tuple(new)

    lax.fori_loop(0, nch, fwd, tuple(s_f0), unroll=RET_UNROLL)

    def bwd(i, s_bs):
        c = nch - 1 - i
        base = pl.multiple_of(c * CHUNK, CHUNK)
        new = []
        for h, hd in enumerate(heads):
            qq = q_ref[0, pl.ds(base, CHUNK), hd["cols"]]
            kk = k_ref[0, pl.ds(base, CHUNK), hd["cols"]]
            vv = v_ref[0, pl.ds(base, CHUNK), hd["cols"]]
            s_b = s_bs[h]
            scores = (_dot_nt(qq, kk) * hd["dmat"]).astype(BF16)
            y = (_dot(scores, vv)
                 + _dot(qq, sf_scr[c, h].astype(BF16)) * hd["dq_f"]
                 + _dot(qq, s_b.astype(BF16)) * hd["dq_b"])
            mu = jnp.mean(y, axis=-1, keepdims=True)
            yc = y - mu
            var = jnp.mean(yc * yc, axis=-1, keepdims=True)
            yn = (yc * lax.rsqrt(var + EPS)) * gn_ref[:, hd["cols"]]
            gate = _silu(g_ref[0, pl.ds(base, CHUNK), hd["cols"]].astype(F32))
            y_ref[0, pl.ds(base, CHUNK), hd["cols"]] = (yn * gate).astype(BF16)
            new.append(hd["dc_b"] * s_b + _dot_tn((kk.astype(F32) * hd["dk_b"]).astype(BF16), vv))
        return tuple(new)

    lax.fori_loop(0, nch, bwd, tuple(s_b0), unroll=RET_UNROLL)


def _retention(q, k, v, g, kc, vc, decay_f, decay_b, gn_w):
    b, L, w = q.shape
    Lc = kc.shape[1]
    nch = L // CHUNK
    per_b = lambda n: pl.BlockSpec((1, n, w), lambda i: (i, 0, 0))
    return pl.pallas_call(
        _ret_kernel,
        grid=(b,),
        in_specs=[per_b(L), per_b(L), per_b(L), per_b(L), per_b(Lc), per_b(Lc),
                  _const_spec((1, w)), _const_spec((1, w)), _const_spec((1, w))],
        out_specs=per_b(L),
        out_shape=jax.ShapeDtypeStruct((b, L, w), BF16),
        scratch_shapes=[
            pltpu.VMEM((nch, RET_HEADS, RET_DK, RET_DK), F32),
        ],
        compiler_params=pltpu.CompilerParams(vmem_limit_bytes=48 * 1024 * 1024),
        name="retention",
    )(q, k, v, g, kc, vc, decay_f, decay_b, gn_w)


def _outproj_router_kernel(yr_ref, ys_ref, x_ref, g1_ref, sh2_ref, sc2_ref, npost_ref, npre_ref,
                           wor_ref, wos_ref, wr_ref, br_ref, tri_ref,
                           x1_ref, h2_ref, route_ref, cnt_ref, slots_ref,
                           wcat, carry):
    i = pl.program_id(0)

    @pl.when(i == 0)
    def _():
        wr = wr_ref[...]
        hi = wr.astype(BF16)
        wcat[:, :LANES] = hi
        wcat[:, LANES:] = (wr - hi.astype(F32)).astype(BF16)
        carry[...] = jnp.zeros_like(carry)

    tm = tri_ref.shape[0]
    running = carry[...]
    for sub in range(x_ref.shape[0] // tm):
        running = _route_subtile(sub, tm, running, yr_ref, ys_ref, x_ref, g1_ref, sh2_ref, sc2_ref, npost_ref,
                                 npre_ref, wor_ref, wos_ref, br_ref, tri_ref, x1_ref, h2_ref, route_ref,
                                 slots_ref, wcat)
    carry[...] = running
    cnt_ref[...] = running


def _route_subtile(sub, tm, running, yr_ref, ys_ref, x_ref, g1_ref, sh2_ref, sc2_ref, npost_ref, npre_ref,
                   wor_ref, wos_ref, br_ref, tri_ref, x1_ref, h2_ref, route_ref, slots_ref, wcat):
    rows = slice(sub * tm, (sub + 1) * tm)
    y = _dot(yr_ref[rows, :], wor_ref[...]) + _dot(ys_ref[rows, :], wos_ref[...])
    ms = jnp.mean(y * y, axis=-1, keepdims=True)
    x1 = x_ref[rows, :] + (y * lax.rsqrt(ms + EPS)) * (g1_ref[0] * npost_ref[...])
    x1_ref[rows, :] = x1
    h2 = _norm_mod(x1, npre_ref[...], sc2_ref[0], sh2_ref[0])
    h2_ref[rows, :] = h2

    h_hi = h2.astype(BF16)
    h_lo = (h2 - h_hi.astype(F32)).astype(BF16)
    both = _dot(h_hi, wcat[...])
    lg = both[:, :LANES] + both[:, LANES:] + _dot(h_lo, wcat[:, :LANES]) + br_ref[...]

    lane = lax.broadcasted_iota(jnp.int32, (tm, LANES), 1)
    lane_f = lane.astype(F32)
    is_grp = (lane >= N_EXPERTS) & (lane < N_EXPERTS + MOE_GROUPS)
    gl = jnp.where(is_grp, lg, NEG_BIG)
    mg = jnp.max(gl, axis=-1, keepdims=True)
    grp_lane = jnp.min(jnp.where(gl == mg, lane_f, 1e9), axis=-1, keepdims=True)
    p_g = 1.0 / jnp.sum(jnp.where(is_grp, jnp.exp(gl - mg), 0.0), axis=-1, keepdims=True)
    first = (grp_lane - N_EXPERTS) * EXPERTS_PER_GROUP
    in_grp = (lane_f >= first) & (lane_f < first + EXPERTS_PER_GROUP)
    el = jnp.where(in_grp, lg, NEG_BIG)
    t1 = jnp.max(el, axis=-1, keepdims=True)
    i1 = jnp.min(jnp.where(el == t1, lane_f, 1e9), axis=-1, keepdims=True)
    el2 = jnp.where(lane_f == i1, NEG_BIG, el)
    t2 = jnp.max(el2, axis=-1, keepdims=True)
    i2 = jnp.min(jnp.where(el2 == t2, lane_f, 1e9), axis=-1, keepdims=True)
    s = jnp.exp(t2 - t1)
    w1 = p_g / (1.0 + s)
    w2 = p_g * s / (1.0 + s)

    oh1 = (lane_f == i1)
    oh2 = (lane_f == i2)
    oh = (oh1 | oh2).astype(BF16)
    before = _dot(tri_ref[...], oh) + running
    rank1 = jnp.sum(jnp.where(oh1, before, 0.0), axis=-1, keepdims=True)
    rank2 = jnp.sum(jnp.where(oh2, before, 0.0), axis=-1, keepdims=True)

    route_ref[rows, :] = jnp.where(lane == 0, w1, jnp.where(lane == 1, w2, 0.0))

    row = lax.broadcasted_iota(jnp.int32, (tm, LANES), 0)
    on_diag = (row % LANES) == lane
    per = tm // LANES
    n_sub = x_ref.shape[0] // tm
    for qi, col in enumerate((i1, i2, rank1, rank2)):
        picked = jnp.where(on_diag, col, 0.0)
        dense = jnp.sum(picked.reshape(per, LANES, LANES), axis=1).astype(jnp.int32)
        slots_ref[0, (qi * n_sub + sub) * per:(qi * n_sub + sub + 1) * per, :] = dense
    return running + jnp.sum(oh.astype(F32), axis=0, keepdims=True)


def _outproj_router(yr, ys, x2, mod3, npost, npre, wo_r, wo_s, w_router, b_router, seq_len):
    T, d = x2.shape
    tm = TM_OUT
    per_seq = seq_len // tm
    rw = yr.shape[1]
    sub = TM_ROUTE_SUB
    tri = (jnp.arange(sub)[:, None] > jnp.arange(sub)[None, :]).astype(BF16)
    tok = lambda w: pl.BlockSpec((tm, w), lambda i: (i, 0))
    modv = lambda k: pl.BlockSpec((1, 1, d), lambda i: (i // per_seq, 0, k))
    return pl.pallas_call(
        _outproj_router_kernel,
        grid=(T // tm,),
        in_specs=[
            tok(rw), tok(rw), tok(d), modv(2), modv(3), modv(4),
            _const_spec((1, d)), _const_spec((1, d)),
            _const_spec(wo_r.shape), _const_spec(wo_s.shape), _const_spec(w_router.shape), _const_spec((1, LANES)),
            _const_spec((sub, sub)),
        ],
        out_specs=[tok(d), tok(d), tok(LANES), _const_spec((1, LANES)),
                   pl.BlockSpec((1, 2 * TOP_K * (tm // LANES), LANES), lambda i: (i, 0, 0))],
        out_shape=[jax.ShapeDtypeStruct((T, d), F32), jax.ShapeDtypeStruct((T, d), F32),
                   jax.ShapeDtypeStruct((T, LANES), F32), jax.ShapeDtypeStruct((1, LANES), F32),
                   jax.ShapeDtypeStruct((T // tm, 2 * TOP_K * (tm // LANES), LANES), jnp.int32)],
        scratch_shapes=[pltpu.VMEM((d, 2 * LANES), BF16), pltpu.VMEM((1, LANES), F32)],
        compiler_params=pltpu.CompilerParams(dimension_semantics=("arbitrary",),
                                             vmem_limit_bytes=48 * 1024 * 1024),
        name="outproj_router",
    )(yr, ys, x2, mod3, mod3, mod3, npost, npre, wo_r, wo_s, w_router, b_router, tri)


def _slot_rows_kernel(start_ref, slots_ref, o_ref):
    half = slots_ref.shape[1] // 2
    expert = slots_ref[:, :half, :]
    row = slots_ref[:, half:, :]
    for e in range(N_EXPERTS):
        row = row + jnp.where(expert == e, start_ref[e], 0)
    for r in range(half):
        o_ref[:, 0, r * LANES:(r + 1) * LANES] = row[:, r, :]


def _slot_rows(pad_start, slots):
    nt, rows, _ = slots.shape
    grid_spec = pltpu.PrefetchScalarGridSpec(
        num_scalar_prefetch=1,
        grid=(1,),
        in_specs=[pl.BlockSpec(slots.shape, lambda i, ps: (0, 0, 0))],
        out_specs=pl.BlockSpec((nt, 1, rows // 2 * LANES), lambda i, ps: (0, 0, 0)),
    )
    return pl.pallas_call(
        _slot_rows_kernel,
        grid_spec=grid_spec,
        out_shape=jax.ShapeDtypeStruct((nt, 1, rows // 2 * LANES), jnp.int32),
        name="slot_rows",
    )(pad_start, slots)


def _dispatch_kernel(pad_end_ref, dest_ref, h_hbm, xs_hbm, zbuf, hbuf, sem, zsem, in_sem):
    i = pl.program_id(0)
    nt = pl.num_programs(0)
    td = hbuf.shape[1]
    mb = zbuf.shape[0]

    @pl.when(i == 0)
    def _():
        zbuf[...] = jnp.zeros_like(zbuf)

        def fill(e):
            return pltpu.make_async_copy(zbuf, xs_hbm.at[pl.ds(pl.multiple_of(pad_end_ref[e] - mb, mb), mb)], zsem)

        def has_rows(e):
            return pad_end_ref[e] > jnp.where(e == 0, 0, pad_end_ref[jnp.maximum(e - 1, 0)])

        def start(e, carry):
            @pl.when(has_rows(e))
            def _():
                fill(e).start()
            return carry

        def wait(e, carry):
            @pl.when(has_rows(e))
            def _():
                fill(e).wait()
            return carry

        def tail_fill(blk):
            return pltpu.make_async_copy(zbuf, xs_hbm.at[pl.ds(pl.multiple_of(blk * mb, mb), mb)], zsem)

        def tail_start(blk, carry):
            @pl.when(blk * mb >= pad_end_ref[N_EXPERTS - 1])
            def _():
                tail_fill(blk).start()
            return carry

        def tail_wait(blk, carry):
            @pl.when(blk * mb >= pad_end_ref[N_EXPERTS - 1])
            def _():
                tail_fill(blk).wait()
            return carry

        n_blocks = xs_hbm.shape[0] // mb
        lax.fori_loop(0, N_EXPERTS, start, 0)
        lax.fori_loop(0, n_blocks, tail_start, 0)
        lax.fori_loop(0, N_EXPERTS, wait, 0)
        lax.fori_loop(0, n_blocks, tail_wait, 0)

    def fetch(t):
        return pltpu.make_async_copy(h_hbm.at[pl.ds(pl.multiple_of(t * td, td), td)], hbuf.at[t % DISPATCH_RING],
                                     in_sem.at[t % DISPATCH_RING])

    @pl.when(i == 0)
    def _():
        fetch(i).start()

    @pl.when(i + 1 < nt)
    def _():
        fetch(i + 1).start()

    fetch(i).wait()
    for slot in range(DISPATCH_RING):
        @pl.when(i % DISPATCH_RING == slot)
        def _(slot=slot):
            _issue_row_copies(dest_ref, td, lambda kk, j, d: pltpu.make_async_copy(
                hbuf.at[slot, pl.ds(j, 1)], xs_hbm.at[pl.ds(d, 1)], sem.at[slot]))

    def drain(t):
        for kk in range(TOP_K):
            pltpu.make_async_copy(hbuf.at[t % DISPATCH_RING], xs_hbm.at[pl.ds(0, td)],
                                  sem.at[t % DISPATCH_RING]).wait()

    @pl.when(i > 0)
    def _():
        drain(i - 1)

    @pl.when(i == nt - 1)
    def _():
        drain(i)


def _issue_row_copies(dest_ref, n_tok, row_copy):
    def issue(j, carry):
        for kk in range(TOP_K):
            row_copy(kk, j, dest_ref[0, 0, kk * n_tok + j]).start()
        return carry

    lax.fori_loop(0, n_tok, issue, 0, unroll=DMA_UNROLL)


def _dispatch(pad_end, dest, h2, cap):
    T, d = h2.shape
    td = TD_DISPATCH
    nt = T // td
    grid_spec = pltpu.PrefetchScalarGridSpec(
        num_scalar_prefetch=1,
        grid=(nt,),
        in_specs=[
            pl.BlockSpec((1, 1, TOP_K * td), lambda i, pe: (i, 0, 0), memory_space=pltpu.SMEM),
            pl.BlockSpec(memory_space=pl.ANY),
        ],
        out_specs=pl.BlockSpec(memory_space=pl.ANY),
        scratch_shapes=[pltpu.VMEM((MB_EXPERT, d), h2.dtype), pltpu.VMEM((DISPATCH_RING, td, d), h2.dtype),
                        pltpu.SemaphoreType.DMA((DISPATCH_RING,)), pltpu.SemaphoreType.DMA(()),
                        pltpu.SemaphoreType.DMA((DISPATCH_RING,))],
    )
    return pl.pallas_call(
        _dispatch_kernel,
        grid_spec=grid_spec,
        out_shape=jax.ShapeDtypeStruct((cap, d), h2.dtype),
        compiler_params=pltpu.CompilerParams(dimension_semantics=("arbitrary",)),
        name="dispatch",
    )(pad_end, dest, h2)


def _expert_kernel(be_ref, first_ref, slot_ref, next_ref, nused_ref, xs_ref, wg_hbm, wu_hbm, wd_hbm, y_ref,
                   wg_f, wu_f, wd_f, wg_b, wu_b, wd_b, sem):
    i = pl.program_id(0)

    def fetch(e, s):
        return [pltpu.make_async_copy(src.at[e], dst.at[s], sem.at[s, k])
                for k, (src, dst) in enumerate(((wg_hbm, wg_f), (wu_hbm, wu_f), (wd_hbm, wd_f)))]

    @pl.when(i == 0)
    def _():
        for c in fetch(be_ref[0], slot_ref[0]):
            c.start()

    @pl.when(first_ref[i] == 1)
    def _():
        s = slot_ref[i]

        @pl.when(next_ref[i] >= 0)
        def _():
            for c in fetch(next_ref[i], 1 - s):
                c.start()

        for c in fetch(be_ref[i], s):
            c.wait()
        wg_b[...] = wg_f[s].astype(BF16)
        wu_b[...] = wu_f[s].astype(BF16)
        wd_b[...] = wd_f[s].astype(BF16)

    @pl.when(i < nused_ref[0])
    def _():
        xb = xs_ref[...].astype(BF16)
        hid = (_silu(_dot(xb, wg_b[...])) * _dot(xb, wu_b[...])).astype(BF16)
        y_ref[...] = _dot(hid, wd_b[...])

    @pl.when(i >= nused_ref[0])
    def _():
        y_ref[...] = jnp.zeros_like(y_ref)


def _expert_plan(padded, pad_end, n_blocks, mb):
    n_used = (pad_end[-1:] // mb).astype(jnp.int32)
    blk_start = jnp.arange(n_blocks, dtype=jnp.int32) * mb
    experts = jnp.arange(N_EXPERTS, dtype=jnp.int32)
    blk_expert = jnp.minimum(jnp.sum((pad_end[None, :] <= blk_start[:, None]).astype(jnp.int32), axis=1),
                             N_EXPERTS - 1)
    prev = jnp.concatenate([jnp.full((1,), -1, jnp.int32), blk_expert[:-1]])
    first = ((blk_start < pad_end[-1]) & (blk_expert != prev)).astype(jnp.int32)
    slot = (jnp.cumsum(first) - 1) % 2
    later = jnp.where((padded > 0)[None, :] & (experts[None, :] > experts[:, None]), experts[None, :], N_EXPERTS)
    next_nonempty = jnp.min(later, axis=1)
    next_nonempty = jnp.where(next_nonempty == N_EXPERTS, -1, next_nonempty)
    nxt = jnp.sum(jnp.where(blk_expert[:, None] == experts[None, :], next_nonempty[None, :], 0), axis=1)
    return [a.astype(jnp.int32) for a in (blk_expert, first, slot, nxt, n_used)]


def _experts(plan, xs, w_gate, w_up, w_down):
    cap, dp = xs.shape
    n_exp, d, de = w_gate.shape
    assert dp == d
    mb = MB_EXPERT
    grid_spec = pltpu.PrefetchScalarGridSpec(
        num_scalar_prefetch=len(plan),
        grid=(cap // mb,),
        in_specs=[
            pl.BlockSpec((mb, dp), lambda i, be, fi, sl, nx, nu: (jnp.minimum(i, nu[0] - 1), 0)),
            pl.BlockSpec(memory_space=pl.ANY), pl.BlockSpec(memory_space=pl.ANY), pl.BlockSpec(memory_space=pl.ANY),
        ],
        out_specs=pl.BlockSpec((mb, dp), lambda i, be, fi, sl, nx, nu: (i, 0)),
        scratch_shapes=[pltpu.VMEM((2, d, de), F32), pltpu.VMEM((2, d, de), F32), pltpu.VMEM((2, de, d), F32),
                        pltpu.VMEM((d, de), BF16), pltpu.VMEM((d, de), BF16), pltpu.VMEM((de, d), BF16),
                        pltpu.SemaphoreType.DMA((2, 3))],
    )
    return pl.pallas_call(
        _expert_kernel,
        grid_spec=grid_spec,
        out_shape=jax.ShapeDtypeStruct((cap, dp), xs.dtype),
        compiler_params=pltpu.CompilerParams(dimension_semantics=("arbitrary",),
                                             vmem_limit_bytes=48 * 1024 * 1024),
        name="experts",
    )(*plan, xs, w_gate, w_up, w_down)


def _combine_kernel(dest_ref, dest_next_ref, route_ref, x1_ref, g2_ref, nw_ref, yb_hbm, o_ref, buf, sem):
    i = pl.program_id(0)
    nt = pl.num_programs(0)
    tf = x1_ref.shape[0]
    stage = i % 2

    def gather(refs, st):
        _issue_row_copies(refs, tf, lambda kk, j, d: pltpu.make_async_copy(
            yb_hbm.at[pl.ds(d, 1)], buf.at[st, kk, pl.ds(j, 1)], sem.at[st]))

    @pl.when(i == 0)
    def _():
        gather(dest_ref, 0)

    for st in range(2):
        @pl.when((i + 1 < nt) & (stage != st))
        def _(st=st):
            gather(dest_next_ref, st)

    for kk in range(TOP_K):
        pltpu.make_async_copy(yb_hbm.at[pl.ds(0, tf)], buf.at[stage, kk], sem.at[stage]).wait()

    route = route_ref[...]
    out = route[:, 0:1] * buf[stage, 0] + route[:, 1:2] * buf[stage, 1]
    ms = jnp.mean(out * out, axis=-1, keepdims=True)
    o_ref[...] = x1_ref[...] + g2_ref[0] * ((out * lax.rsqrt(ms + EPS)) * nw_ref[...])


def _combine(dest, route, x1, mod3, nw, yb, seq_len):
    T, d = x1.shape
    tf = TF_COMBINE
    nt = T // tf
    per_seq = seq_len // tf
    slot_spec = functools.partial(pl.BlockSpec, (1, 1, TOP_K * tf), memory_space=pltpu.SMEM)
    return pl.pallas_call(
        _combine_kernel,
        grid=(nt,),
        in_specs=[
            slot_spec(lambda i: (i, 0, 0)),
            slot_spec(lambda i: (jnp.minimum(i + 1, nt - 1), 0, 0)),
            pl.BlockSpec((tf, LANES), lambda i: (i, 0)),
            pl.BlockSpec((tf, d), lambda i: (i, 0)),
            pl.BlockSpec((1, 1, d), lambda i: (i // per_seq, 0, 5)),
            _const_spec((1, d)),
            pl.BlockSpec(memory_space=pl.ANY),
        ],
        out_specs=pl.BlockSpec((tf, d), lambda i: (i, 0)),
        out_shape=jax.ShapeDtypeStruct((T, d), F32),
        scratch_shapes=[pltpu.VMEM((2, TOP_K, tf, yb.shape[1]), yb.dtype), pltpu.SemaphoreType.DMA((2,))],
        compiler_params=pltpu.CompilerParams(dimension_semantics=("arbitrary",),
                                             vmem_limit_bytes=48 * 1024 * 1024),
        name="combine",
    )(dest, dest, route, x1, mod3, nw, yb)


def _rope_tables(L, n_heads):
    quarter = RET_DK // 4
    freqs = ROPE_BASE ** (-jnp.arange(quarter, dtype=F32) / quarter)
    t = jnp.arange(L)
    ang_r = (t // GRID_W).astype(F32)[:, None] * freqs
    ang_c = (t % GRID_W).astype(F32)[:, None] * freqs
    cos = jnp.concatenate([jnp.cos(ang_r)] * 2 + [jnp.cos(ang_c)] * 2, axis=-1)
    sin = jnp.concatenate([-jnp.sin(ang_r), jnp.sin(ang_r), -jnp.sin(ang_c), jnp.sin(ang_c)], axis=-1)
    return jnp.tile(cos, (1, n_heads)), jnp.tile(sin, (1, n_heads))


def _lane_pad(v, width=LANES):
    return jnp.pad(v, [(0, 0)] * (v.ndim - 1) + [(0, width - v.shape[-1])])


def kernel(x, c, ctx, c_ctx, w_mod, b_mod, norm_pre_mix, norm_post_mix, norm_pre_ffn, norm_post_ffn, w_in, w_out, ret_decay_f, ret_decay_b, ret_gn_w, ssd_conv_w, ssd_conv_b, ssd_dt_bias_f, ssd_dt_bias_b, ssd_a_log_f, ssd_a_log_b, ssd_d, ssd_norm_w, moe_w_rg, moe_b_rg, moe_w_re, moe_b_re, moe_w_gate, moe_w_up, moe_w_down):
    b, L, d = x.shape
    assert w_mod.shape[0] == 1, "single layer: context outputs are never needed"
    assert TM_OUT == TD_DISPATCH == TF_COMBINE, "router, dispatch and combine share one slot-row layout"
    rw = RET_HEADS * RET_DK
    nconv = SSD_WIDTH + 2 * SSD_GROUPS * SSD_STATE
    T = b * L

    mod_rows = -(-(b + 1) // SUBLANES) * SUBLANES
    c_all = jnp.zeros((mod_rows, d), F32).at[:b].set(c).at[b].set(c_ctx)
    mod3 = _modulation(c_all, w_mod[0], b_mod[0]).reshape(mod_rows, 1, 6 * d)

    wi = w_in[0]
    o = 0
    wq = wi[:, o:o + rw]; o += rw
    wk = wi[:, o:o + rw]; o += rw
    wv = wi[:, o:o + rw]; o += rw
    wg = wi[:, o:o + rw]; o += rw
    wz = wi[:, o:o + SSD_WIDTH]; o += SSD_WIDTH
    wxbc = wi[:, o:o + nconv].astype(BF16); o += nconv
    wdt = _lane_pad(wi[:, o:o + 2 * SSD_HEADS]).astype(BF16)
    wqk = jnp.concatenate([wq, wk], axis=1).astype(BF16)
    wvgz = jnp.concatenate([wv, wg, wz], axis=1).astype(BF16)
    cos_t, sin_t = _rope_tables(L, RET_HEADS)
    nw1 = norm_pre_mix[0].reshape(1, d)

    q, k, v, g, z, xbc, dt = _inproj(x, mod3, nw1, wqk, wvgz, wxbc, wdt, cos_t, sin_t)
    kc, vc, xbcc, dtc = _inproj_ctx(ctx, mod3, b, nw1, wk.astype(BF16), wv.astype(BF16), wxbc, wdt)

    conv_w8 = jnp.pad(ssd_conv_w[0], ((0, SUBLANES - SSD_CONV), (0, 0)))
    dt_bias = _lane_pad(jnp.concatenate([ssd_dt_bias_f[0], ssd_dt_bias_b[0]])[None, :])
    a_log = _lane_pad(jnp.concatenate([ssd_a_log_f[0], ssd_a_log_b[0]])[None, :])
    d_skip = jnp.repeat(ssd_d[0], SSD_HEADDIM)[None, :]
    ys = _ssd(xbc, z, dt, xbcc, dtc, conv_w8, ssd_conv_b[0][None, :], dt_bias, a_log, d_skip,
              ssd_norm_w[0][None, :])

    yr = _retention(q, k, v, g, kc, vc,
                    jnp.repeat(ret_decay_f[0], RET_DK)[None, :], jnp.repeat(ret_decay_b[0], RET_DK)[None, :],
                    ret_gn_w[0][None, :])

    wo = w_out[0].astype(BF16)
    w_router = _lane_pad(jnp.concatenate(
        [jnp.transpose(moe_w_re[0], (1, 0, 2)).reshape(d, N_EXPERTS), moe_w_rg[0]], axis=1))
    b_router = _lane_pad(jnp.concatenate([moe_b_re[0].reshape(-1), moe_b_rg[0]])[None, :])
    x1, h2, route, counts, slots = _outproj_router(
        yr.reshape(T, rw), ys.reshape(T, SSD_WIDTH), x.reshape(T, d), mod3,
        norm_post_mix[0][None, :], norm_pre_ffn[0][None, :], wo[:rw], wo[rw:], w_router, b_router, L)

    mb = MB_EXPERT
    n_blocks = -(-(T * TOP_K + N_EXPERTS * (mb - 1)) // mb)
    cap = n_blocks * mb
    cnt = counts[0, :N_EXPERTS].astype(jnp.int32)
    padded = (cnt + mb - 1) // mb * mb
    pad_end = jnp.cumsum(padded)
    pad_start = pad_end - padded

    dest = _slot_rows(pad_start.astype(jnp.int32), slots)
    xs = _dispatch(pad_end.astype(jnp.int32), dest, h2, cap)
    yb = _experts(_expert_plan(padded, pad_end, n_blocks, mb), xs, moe_w_gate[0], moe_w_up[0], moe_w_down[0])
    out = _combine(dest, route, x1, mod3, norm_post_ffn[0][None, :], yb, L)
    return out.reshape(b, L, d)
```

```python
import functools
import math

import jax
import jax.numpy as jnp
from jax import lax
from jax.experimental import pallas as pl
from jax.experimental.pallas import tpu as pltpu

F32 = jnp.float32
BF16 = jnp.bfloat16

LANES = 128
SUBLANES = 8

EPS = 1e-6
CHUNK = 128
GRID_W = 64
RET_HEADS = 4
RET_DK = 128
ROPE_BASE = 10000.0
SSD_HEADS = 8
SSD_HEADDIM = 64
SSD_GROUPS = 2
SSD_STATE = 128
SSD_WIDTH = SSD_HEADS * SSD_HEADDIM
SSD_CONV = 5
SSD_PAIRS = SSD_WIDTH // LANES
MOE_GROUPS = 4
EXPERTS_PER_GROUP = 8
N_EXPERTS = MOE_GROUPS * EXPERTS_PER_GROUP
TOP_K = 2
CONV_HALO = SUBLANES

TM_PROJ = 512
TM_OUT = 512
TM_ROUTE_SUB = 512
TD_DISPATCH = 512
MB_EXPERT = 512
ZERO_PIECE = 64
TF_COMBINE = 512
DMA_UNROLL = 8
DISPATCH_RING = 3
SSD_UNROLL = 2
RET_UNROLL = 2
NEG_BIG = -1e30


def _silu(v):
    return v * jax.nn.sigmoid(v)


def _dot(a, b):
    return jnp.dot(a, b, preferred_element_type=F32)


def _dot_tn(a, b):
    return lax.dot_general(a, b, (((0,), (0,)), ((), ())), preferred_element_type=F32)


def _dot_nt(a, b):
    return lax.dot_general(a, b, (((1,), (1,)), ((), ())), preferred_element_type=F32)


def _mod_kernel(c_ref, w_ref, b_ref, o_ref):
    a = _silu(c_ref[...])
    w = w_ref[...]
    a_hi = a.astype(BF16)
    a_lo = (a - a_hi.astype(F32)).astype(BF16)
    w_hi = w.astype(BF16)
    w_lo = (w - w_hi.astype(F32)).astype(BF16)
    o_ref[...] = _dot(a_hi, w_hi) + _dot(a_lo, w_hi) + _dot(a_hi, w_lo) + b_ref[...]


def _modulation(c_all, w_mod, b_mod):
    rows, d = c_all.shape
    n = w_mod.shape[1]
    return pl.pallas_call(
        _mod_kernel,
        grid=(n // d,),
        in_specs=[
            pl.BlockSpec((rows, d), lambda j: (0, 0)),
            pl.BlockSpec((d, d), lambda j: (0, j)),
            pl.BlockSpec((1, d), lambda j: (0, j)),
        ],
        out_specs=pl.BlockSpec((rows, d), lambda j: (0, j)),
        out_shape=jax.ShapeDtypeStruct((rows, n), F32),
        name="modulation",
    )(c_all, w_mod, b_mod.reshape(1, n))


def _norm_mod(x, nw, sc, sh):
    ms = jnp.mean(x * x, axis=-1, keepdims=True)
    return (x * lax.rsqrt(ms + EPS)) * (nw * (1.0 + sc)) + sh


def _rope(t, cos, sin_signed, first_half):
    width = t.shape[-1]
    quarter = RET_DK // 4
    swapped = jnp.where(first_half, pltpu.roll(t, width - quarter, 1), pltpu.roll(t, quarter, 1))
    return t * cos + swapped * sin_signed


def _inproj_kernel(x_ref, sh_ref, sc_ref, nw_ref, wqk_ref, wvgz_ref, wxbc_ref, wdt_ref, cos_ref, sin_ref,
                   q_ref, k_ref, v_ref, g_ref, z_ref, xbc_ref, dt_ref):
    hb = _norm_mod(x_ref[0], nw_ref[...], sc_ref[0], sh_ref[0]).astype(BF16)
    rw = q_ref.shape[-1]
    qk = _dot(hb, wqk_ref[...])
    cos = cos_ref[...]
    sin = sin_ref[...]
    lane = lax.broadcasted_iota(jnp.int32, cos.shape, 1)
    first_half = (lane % (RET_DK // 2)) < (RET_DK // 4)
    q_ref[0] = _rope(qk[:, :rw], cos, sin, first_half).astype(BF16)
    k_ref[0] = (_rope(qk[:, rw:], cos, sin, first_half) * (RET_DK ** -0.5)).astype(BF16)
    vgz = _dot(hb, wvgz_ref[...])
    v_ref[0] = vgz[:, :rw].astype(BF16)
    g_ref[0] = vgz[:, rw:2 * rw].astype(BF16)
    z_ref[0] = vgz[:, 2 * rw:].astype(BF16)
    xbc_ref[0] = _dot(hb, wxbc_ref[...]).astype(BF16)
    dt_ref[0] = _dot(hb, wdt_ref[...])


def _inproj_ctx_kernel(x_ref, sh_ref, sc_ref, nw_ref, wk_ref, wv_ref, wxbc_ref, wdt_ref,
                       k_ref, v_ref, xbc_ref, dt_ref):
    hb = _norm_mod(x_ref[0], nw_ref[...], sc_ref[0], sh_ref[0]).astype(BF16)
    k_ref[0] = (_dot(hb, wk_ref[...]) * (RET_DK ** -0.5)).astype(BF16)
    v_ref[0] = _dot(hb, wv_ref[...]).astype(BF16)
    xbc_ref[0] = _dot(hb, wxbc_ref[...]).astype(BF16)
    dt_ref[0] = _dot(hb, wdt_ref[...])


def _const_spec(shape):
    nd = len(shape)
    return pl.BlockSpec(shape, lambda *_: (0,) * nd)


def _inproj(x, mod3, nw, wqk, wvgz, wxbc, wdt, cos_t, sin_t):
    b, L, d = x.shape
    tm = min(TM_PROJ, L)
    rw = wqk.shape[1] // 2
    tok = lambda w: pl.BlockSpec((1, tm, w), lambda i, j: (i, j, 0))
    out_bf = lambda w: jax.ShapeDtypeStruct((b, L, w), BF16)
    return pl.pallas_call(
        _inproj_kernel,
        grid=(b, L // tm),
        in_specs=[
            tok(d),
            pl.BlockSpec((1, 1, d), lambda i, j: (i, 0, 0)),
            pl.BlockSpec((1, 1, d), lambda i, j: (i, 0, 1)),
            _const_spec((1, d)),
            _const_spec(wqk.shape), _const_spec(wvgz.shape), _const_spec(wxbc.shape), _const_spec(wdt.shape),
            pl.BlockSpec((tm, rw), lambda i, j: (j, 0)),
            pl.BlockSpec((tm, rw), lambda i, j: (j, 0)),
        ],
        out_specs=[tok(rw), tok(rw), tok(rw), tok(rw), tok(rw), tok(wxbc.shape[1]), tok(LANES)],
        out_shape=[out_bf(rw), out_bf(rw), out_bf(rw), out_bf(rw), out_bf(rw), out_bf(wxbc.shape[1]),
                   jax.ShapeDtypeStruct((b, L, LANES), F32)],
        compiler_params=pltpu.CompilerParams(vmem_limit_bytes=48 * 1024 * 1024),
        name="inproj",
    )(x, mod3, mod3, nw, wqk, wvgz, wxbc, wdt, cos_t, sin_t)


def _inproj_ctx(ctx, mod3, ctx_row, nw, wk, wv, wxbc, wdt):
    b, L, d = ctx.shape
    tm = min(TM_PROJ, L)
    rw = wk.shape[1]
    tok = lambda w: pl.BlockSpec((1, tm, w), lambda i, j: (i, j, 0))
    out_bf = lambda w: jax.ShapeDtypeStruct((b, L, w), BF16)
    return pl.pallas_call(
        _inproj_ctx_kernel,
        grid=(b, L // tm),
        in_specs=[
            tok(d),
            pl.BlockSpec((1, 1, d), lambda i, j: (ctx_row, 0, 0)),
            pl.BlockSpec((1, 1, d), lambda i, j: (ctx_row, 0, 1)),
            _const_spec((1, d)),
            _const_spec(wk.shape), _const_spec(wv.shape), _const_spec(wxbc.shape), _const_spec(wdt.shape),
        ],
        out_specs=[tok(rw), tok(rw), tok(wxbc.shape[1]), tok(LANES)],
        out_shape=[out_bf(rw), out_bf(rw), out_bf(wxbc.shape[1]), jax.ShapeDtypeStruct((b, L, LANES), F32)],
        compiler_params=pltpu.CompilerParams(vmem_limit_bytes=48 * 1024 * 1024),
        name="inproj_ctx",
    )(ctx, mod3, mod3, nw, wk, wv, wxbc, wdt)


def _ssd_kernel(xbc_ref, z_ref, dt_ref, xbcc_ref, dtc_ref, cw_ref, cb_ref, dtb_ref, alog_ref, dsk_ref, nw_ref,
                y_ref,
                xpad, xpadc, u, uc, dtv, dav, dtcv, dacv, sf_scr, kb_scr, acum, ecum, dec_scr):
    L = xbc_ref.shape[1]
    Lc = xbcc_ref.shape[1]
    nch = L // CHUNK
    nchc = Lc // CHUNK
    win = CHUNK + 2 * CONV_HALO
    nconv = xbc_ref.shape[2]
    nh = SSD_HEADS

    def conv_pass(src_ref, pad_ref, dst_ref, n_chunks, length):
        zeros = jnp.zeros((CONV_HALO, nconv), F32)
        pad_ref[0:CONV_HALO, :] = zeros
        pad_ref[CONV_HALO + length:2 * CONV_HALO + length, :] = zeros
        pad_ref[CONV_HALO:CONV_HALO + length, :] = src_ref[0].astype(F32)

        def chunk(c, carry):
            base = pl.multiple_of(c * CHUNK, CHUNK)
            for cb_i in range(nconv // LANES):
                cols = slice(cb_i * LANES, (cb_i + 1) * LANES)
                w = pad_ref[pl.ds(base, win), cols]
                acc = cb_ref[:, cols] + w[CONV_HALO:CONV_HALO + CHUNK] * cw_ref[SSD_CONV // 2:SSD_CONV // 2 + 1, cols]
                for j in range(SSD_CONV):
                    if j == SSD_CONV // 2:
                        continue
                    shifted = pltpu.roll(w, (SSD_CONV // 2 - j) % win, 0)
                    acc = acc + shifted[CONV_HALO:CONV_HALO + CHUNK] * cw_ref[j:j + 1, cols]
                dst_ref[pl.ds(base, CHUNK), cols] = _silu(acc).astype(BF16)
            return carry

        lax.fori_loop(0, n_chunks, chunk, 0)

    conv_pass(xbcc_ref, xpadc, uc, nchc, Lc)
    conv_pass(xbc_ref, xpad, u, nch, L)

    a_neg = -jnp.exp(alog_ref[...])
    dtv[...] = jax.nn.softplus(dt_ref[0] + dtb_ref[...])
    dav[...] = dtv[...] * a_neg
    dtcv[...] = jax.nn.softplus(dtc_ref[0] + dtb_ref[...])
    dacv[...] = dtcv[...] * a_neg

    row_i = lax.broadcasted_iota(jnp.int32, (CHUNK, CHUNK), 0)
    col_i = lax.broadcasted_iota(jnp.int32, (CHUNK, CHUNK), 1)
    causal = col_i <= row_i
    lo_half = col_i < SSD_HEADDIM
    fwd_lane = col_i < nh
    head_of = lax.broadcasted_iota(jnp.int32, (CHUNK, SSD_WIDTH), 1) // SSD_HEADDIM
    src_col = lax.broadcasted_iota(jnp.int32, (CHUNK, SSD_WIDTH), 0)
    exp_f = (head_of == src_col).astype(BF16)
    exp_b = (head_of == src_col - nh).astype(BF16)
    exp_fb = jnp.concatenate([exp_f, exp_b], axis=1)

    def split3(v):
        hi = v.astype(BF16)
        r1 = v - hi.astype(F32)
        mid = r1.astype(BF16)
        return hi, mid, (r1 - mid.astype(F32)).astype(BF16)

    def times_onehot(v, m, passes=3):
        parts = split3(v)[:passes]
        acc = _dot(parts[0], m)
        for part in parts[1:]:
            acc = acc + _dot(part, m)
        return acc

    def colb(mat, r):
        return jnp.broadcast_to(mat[:, r:r + 1], (CHUNK, CHUNK))

    def pair_sel(a, b_):
        return jnp.where(lo_half, a, b_)

    gw = 2 * LANES

    def chunk_terms(u_ref, dt_s, da_s, base):
        dt = dt_s[pl.ds(base, CHUNK), :]
        da = da_s[pl.ds(base, CHUNK), :]
        acol = da
        for step in (1, 2, 4, 8, 16, 32, 64):
            acol = acol + jnp.where(row_i >= step, pltpu.roll(acol, step, 0), 0.0)
        ecol = acol - da
        last = acol[CHUNK - 1:CHUNK, :]
        wgt = jnp.where(fwd_lane, jnp.exp(last - acol), jnp.exp(ecol)) * dt
        scale = jnp.where(fwd_lane, jnp.exp(acol), jnp.exp(last - ecol))
        wide = times_onehot(jnp.concatenate([wgt, scale], axis=0), exp_fb, passes=1)
        dec = times_onehot(jnp.broadcast_to(jnp.exp(last), (SUBLANES, LANES)), exp_fb)[0:1]
        xs = u_ref[pl.ds(base, CHUNK), 0:SSD_WIDTH].astype(F32)
        kmats = []
        for g in range(SSD_GROUPS):
            xw = jnp.concatenate([xs[:, g * gw:(g + 1) * gw] * wide[:CHUNK, g * gw:(g + 1) * gw],
                                  xs[:, g * gw:(g + 1) * gw] * wide[:CHUNK, SSD_WIDTH + g * gw:SSD_WIDTH + (g + 1) * gw]],
                                 axis=1).astype(BF16)
            bm = u_ref[pl.ds(base, CHUNK), SSD_WIDTH + g * SSD_STATE:SSD_WIDTH + (g + 1) * SSD_STATE]
            kmats.append(_dot_tn(bm, xw))
        return dt, acol, ecol, wide[CHUNK:], dec, kmats

    def advance(s, dec, kmats, backward):
        off = SSD_WIDTH if backward else 0
        koff = gw if backward else 0
        return [dec[:, off + g * gw:off + (g + 1) * gw] * s[g] + kmats[g][:, koff:koff + gw]
                for g in range(SSD_GROUPS)]

    ctx_terms = [chunk_terms(uc, dtcv, dacv, c * CHUNK) for c in range(nchc)]
    s_f0 = [jnp.zeros((SSD_STATE, gw), F32) for _ in range(SSD_GROUPS)]
    for c in range(nchc):
        s_f0 = advance(s_f0, ctx_terms[c][4], ctx_terms[c][5], False)
    s_b0 = [jnp.zeros((SSD_STATE, gw), F32) for _ in range(SSD_GROUPS)]
    for c in reversed(range(nchc)):
        s_b0 = advance(s_b0, ctx_terms[c][4], ctx_terms[c][5], True)

    def prep(c, carry):
        base = pl.multiple_of(c * CHUNK, CHUNK)
        dt, acol, ecol, scale, dec, kmats = chunk_terms(u, dtv, dav, base)
        acum[pl.ds(base, CHUNK), :] = acol
        ecum[pl.ds(base, CHUNK), :] = ecol
        xpad[pl.ds(base, CHUNK), :] = scale
        dec_scr[pl.ds(pl.multiple_of(c * SUBLANES, SUBLANES), SUBLANES), :] = jnp.broadcast_to(dec, (SUBLANES, 2 * SSD_WIDTH))
        for g in range(SSD_GROUPS):
            sf_scr[c, g] = kmats[g][:, :gw]
            kb_scr[c, g] = kmats[g][:, gw:]
        return carry

    lax.fori_loop(0, nch, prep, 0)

    def chunk_dec(c):
        return dec_scr[pl.ds(pl.multiple_of(c * SUBLANES, SUBLANES), 1), :]

    def fwd(c, s_old):
        dec = chunk_dec(c)
        new = []
        for g in range(SSD_GROUPS):
            new.append(dec[:, g * gw:(g + 1) * gw] * s_old[g] + sf_scr[c, g])
            sf_scr[c, g] = s_old[g]
        return tuple(new)

    lax.fori_loop(0, nch, fwd, tuple(s_f0))

    def bwd(i, s_b):
        c = nch - 1 - i
        base = pl.multiple_of(c * CHUNK, CHUNK)
        acol = acum[pl.ds(base, CHUNK), :]
        ecol = ecum[pl.ds(base, CHUNK), :]
        arow = acol.T
        erow = ecol.T
        dt_t = dtv[pl.ds(base, CHUNK), :].T
        scale = xpad[pl.ds(base, CHUNK), :]
        ys = []
        for g in range(SSD_GROUPS):
            bm = u[pl.ds(base, CHUNK), SSD_WIDTH + g * SSD_STATE:SSD_WIDTH + (g + 1) * SSD_STATE]
            cm = u[pl.ds(base, CHUNK), SSD_WIDTH + (SSD_GROUPS + g) * SSD_STATE:SSD_WIDTH + (SSD_GROUPS + g + 1) * SSD_STATE]
            cbm = _dot_nt(cm, bm)
            cs_f = _dot(cm, sf_scr[c, g].astype(BF16))
            cs_b = _dot(cm, s_b[g].astype(BF16))
            for pp in range(SSD_PAIRS // SSD_GROUPS):
                p = g * (SSD_PAIRS // SSD_GROUPS) + pp
                xs_b = u[pl.ds(base, CHUNK), p * LANES:(p + 1) * LANES]
                y_h = []
                for hh in range(2):
                    r = 2 * p + hh
                    arg = jnp.where(causal, colb(acol, r) - arow[r:r + 1, :],
                                    erow[nh + r:nh + r + 1, :] - colb(ecol, nh + r))
                    coef = jnp.where(causal, dt_t[r:r + 1, :], dt_t[nh + r:nh + r + 1, :])
                    gm = (cbm * (jnp.exp(arg) * coef)).astype(BF16)
                    y_h.append(_dot(gm, xs_b))
                sl = slice(pp * LANES, (pp + 1) * LANES)
                wl = slice(p * LANES, (p + 1) * LANES)
                wlb = slice(SSD_WIDTH + p * LANES, SSD_WIDTH + (p + 1) * LANES)
                ys.append(pair_sel(y_h[0], y_h[1]) + cs_f[:, sl] * scale[:, wl] + cs_b[:, sl] * scale[:, wlb]
                          + dsk_ref[:, wl] * xs_b.astype(F32))
        y = jnp.concatenate(ys, axis=1)
        y = y * _silu(z_ref[0, pl.ds(base, CHUNK), :].astype(F32))
        ms = jnp.mean(y * y, axis=-1, keepdims=True)
        y_ref[0, pl.ds(base, CHUNK), :] = ((y * lax.rsqrt(ms + EPS)) * nw_ref[...]).astype(BF16)
        dec = chunk_dec(c)
        return tuple(dec[:, SSD_WIDTH + g * gw:SSD_WIDTH + (g + 1) * gw] * s_b[g] + kb_scr[c, g]
                     for g in range(SSD_GROUPS))

    lax.fori_loop(0, nch, bwd, tuple(s_b0), unroll=SSD_UNROLL)


def _ssd(xbc, z, dt, xbcc, dtc, conv_w8, conv_b, dt_bias, a_log, d_skip, norm_w):
    b, L, nconv = xbc.shape
    Lc = xbcc.shape[1]
    nch = L // CHUNK
    per_b = lambda n, w: pl.BlockSpec((1, n, w), lambda i: (i, 0, 0))
    return pl.pallas_call(
        _ssd_kernel,
        grid=(b,),
        in_specs=[
            per_b(L, nconv), per_b(L, SSD_WIDTH), per_b(L, LANES), per_b(Lc, nconv), per_b(Lc, LANES),
            _const_spec(conv_w8.shape), _const_spec(conv_b.shape), _const_spec(dt_bias.shape),
            _const_spec(a_log.shape), _const_spec(d_skip.shape), _const_spec(norm_w.shape),
        ],
        out_specs=per_b(L, SSD_WIDTH),
        out_shape=jax.ShapeDtypeStruct((b, L, SSD_WIDTH), BF16),
        scratch_shapes=[
            pltpu.VMEM((L + 2 * CONV_HALO, nconv), F32),
            pltpu.VMEM((Lc + 2 * CONV_HALO, nconv), F32),
            pltpu.VMEM((L, nconv), BF16),
            pltpu.VMEM((Lc, nconv), BF16),
            pltpu.VMEM((L, LANES), F32), pltpu.VMEM((L, LANES), F32),
            pltpu.VMEM((Lc, LANES), F32), pltpu.VMEM((Lc, LANES), F32),
            pltpu.VMEM((nch, SSD_GROUPS, SSD_STATE, 2 * LANES), F32),
            pltpu.VMEM((nch, SSD_GROUPS, SSD_STATE, 2 * LANES), F32),
            pltpu.VMEM((L, LANES), F32), pltpu.VMEM((L, LANES), F32),
            pltpu.VMEM((nch * SUBLANES, 2 * SSD_WIDTH), F32),
        ],
        compiler_params=pltpu.CompilerParams(vmem_limit_bytes=56 * 1024 * 1024),
        name="ssd",
    )(xbc, z, dt, xbcc, dtc, conv_w8, conv_b, dt_bias, a_log, d_skip, norm_w)


def _ret_kernel(q_ref, k_ref, v_ref, g_ref, kc_ref, vc_ref, df_ref, db_ref, gn_ref, y_ref, sf_scr):
    L = q_ref.shape[1]
    Lc = kc_ref.shape[1]
    nch = L // CHUNK
    dk = RET_DK
    row_i = lax.broadcasted_iota(jnp.int32, (CHUNK, dk), 0).astype(F32)
    col_i = lax.broadcasted_iota(jnp.int32, (CHUNK, dk), 1).astype(F32)
    rel = row_i - col_i
    crow = lax.broadcasted_iota(jnp.int32, (Lc, dk), 0).astype(F32)

    heads = []
    s_f0 = []
    s_b0 = []
    for h in range(RET_HEADS):
        cols = slice(h * dk, (h + 1) * dk)
        lg_f = -jnp.exp(df_ref[:, cols])
        lg_b = -jnp.exp(db_ref[:, cols])
        heads.append(dict(
            cols=cols,
            dmat=jnp.where(rel >= 0, jnp.exp(jnp.maximum(rel, 0.0) * lg_f), jnp.exp(jnp.maximum(-rel, 0.0) * lg_b)),
            dq_f=jnp.exp((row_i + 1.0) * lg_f),
            dq_b=jnp.exp((CHUNK - row_i) * lg_b),
            dk_f=jnp.exp((CHUNK - 1.0 - row_i) * lg_f),
            dk_b=jnp.exp(row_i * lg_b),
            dc_f=jnp.exp(CHUNK * lg_f),
            dc_b=jnp.exp(CHUNK * lg_b),
        ))
        kc = kc_ref[0, :, cols].astype(F32)
        vc = vc_ref[0, :, cols]
        s_f0.append(_dot_tn((kc * jnp.exp((Lc - 1.0 - crow) * lg_f)).astype(BF16), vc))
        s_b0.append(_dot_tn((kc * jnp.exp(crow * lg_b)).astype(BF16), vc))

    def fwd(c, s_f):
        base = pl.multiple_of(c * CHUNK, CHUNK)
        new = []
        for h, hd in enumerate(heads):
            sf_scr[c, h] = s_f[h]
            kk = k_ref[0, pl.ds(base, CHUNK), hd["cols"]].astype(F32)
            vv = v_ref[0, pl.ds(base, CHUNK), hd["cols"]]
            new.append(hd["dc_f"] * s_f[h] + _dot_tn((kk * hd["dk_f"]).astype(BF16), vv))
        return tuple(new)

    lax.fori_loop(0, nch, fwd, tuple(s_f0), unroll=RET_UNROLL)

    def bwd(i, s_bs):
        c = nch - 1 - i
        base = pl.multiple_of(c * CHUNK, CHUNK)
        new = []
        for h, hd in enumerate(heads):
            qq = q_ref[0, pl.ds(base, CHUNK), hd["cols"]]
            kk = k_ref[0, pl.ds(base, CHUNK), hd["cols"]]
            vv = v_ref[0, pl.ds(base, CHUNK), hd["cols"]]
            s_b = s_bs[h]
            scores = (_dot_nt(qq, kk) * hd["dmat"]).astype(BF16)
            y = (_dot(scores, vv)
                 + _dot(qq, sf_scr[c, h].astype(BF16)) * hd["dq_f"]
                 + _dot(qq, s_b.astype(BF16)) * hd["dq_b"])
            mu = jnp.mean(y, axis=-1, keepdims=True)
            yc = y - mu
            var = jnp.mean(yc * yc, axis=-1, keepdims=True)
            yn = (yc * lax.rsqrt(var + EPS)) * gn_ref[:, hd["cols"]]
            gate = _silu(g_ref[0, pl.ds(base, CHUNK), hd["cols"]].astype(F32))
            y_ref[0, pl.ds(base, CHUNK), hd["cols"]] = (yn * gate).astype(BF16)
            new.append(hd["dc_b"] * s_b + _dot_tn((kk.astype(F32) * hd["dk_b"]).astype(BF16), vv))
        return tuple(new)

    lax.fori_loop(0, nch, bwd, tuple(s_b0), unroll=RET_UNROLL)


def _retention(q, k, v, g, kc, vc, decay_f, decay_b, gn_w):
    b, L, w = q.shape
    Lc = kc.shape[1]
    nch = L // CHUNK
    per_b = lambda n: pl.BlockSpec((1, n, w), lambda i: (i, 0, 0))
    return pl.pallas_call(
        _ret_kernel,
        grid=(b,),
        in_specs=[per_b(L), per_b(L), per_b(L), per_b(L), per_b(Lc), per_b(Lc),
                  _const_spec((1, w)), _const_spec((1, w)), _const_spec((1, w))],
        out_specs=per_b(L),
        out_shape=jax.ShapeDtypeStruct((b, L, w), BF16),
        scratch_shapes=[
            pltpu.VMEM((nch, RET_HEADS, RET_DK, RET_DK), F32),
        ],
        compiler_params=pltpu.CompilerParams(vmem_limit_bytes=48 * 1024 * 1024),
        name="retention",
    )(q, k, v, g, kc, vc, decay_f, decay_b, gn_w)


def _outproj_router_kernel(yr_ref, ys_ref, x_ref, g1_ref, sh2_ref, sc2_ref, npost_ref, npre_ref,
                           wor_ref, wos_ref, wr_ref, br_ref, tri_ref,
                           x1_ref, h2_ref, route_ref, cnt_ref, slots_ref,
                           wcat, carry):
    i = pl.program_id(0)

    @pl.when(i == 0)
    def _():
        wr = wr_ref[...]
        hi = wr.astype(BF16)
        wcat[:, :LANES] = hi
        wcat[:, LANES:] = (wr - hi.astype(F32)).astype(BF16)
        carry[...] = jnp.zeros_like(carry)

    tm = tri_ref.shape[0]
    running = carry[...]
    for sub in range(x_ref.shape[0] // tm):
        running = _route_subtile(sub, tm, running, yr_ref, ys_ref, x_ref, g1_ref, sh2_ref, sc2_ref, npost_ref,
                                 npre_ref, wor_ref, wos_ref, br_ref, tri_ref, x1_ref, h2_ref, route_ref,
                                 slots_ref, wcat)
    carry[...] = running
    cnt_ref[...] = running


def _route_subtile(sub, tm, running, yr_ref, ys_ref, x_ref, g1_ref, sh2_ref, sc2_ref, npost_ref, npre_ref,
                   wor_ref, wos_ref, br_ref, tri_ref, x1_ref, h2_ref, route_ref, slots_ref, wcat):
    rows = slice(sub * tm, (sub + 1) * tm)
    y = _dot(yr_ref[rows, :], wor_ref[...]) + _dot(ys_ref[rows, :], wos_ref[...])
    ms = jnp.mean(y * y, axis=-1, keepdims=True)
    x1 = x_ref[rows, :] + (y * lax.rsqrt(ms + EPS)) * (g1_ref[0] * npost_ref[...])
    x1_ref[rows, :] = x1
    h2 = _norm_mod(x1, npre_ref[...], sc2_ref[0], sh2_ref[0])
    h2_ref[rows, :] = h2

    h_hi = h2.astype(BF16)
    h_lo = (h2 - h_hi.astype(F32)).astype(BF16)
    both = _dot(h_hi, wcat[...])
    lg = both[:, :LANES] + both[:, LANES:] + _dot(h_lo, wcat[:, :LANES]) + br_ref[...]

    lane = lax.broadcasted_iota(jnp.int32, (tm, LANES), 1)
    lane_f = lane.astype(F32)
    is_grp = (lane >= N_EXPERTS) & (lane < N_EXPERTS + MOE_GROUPS)
    gl = jnp.where(is_grp, lg, NEG_BIG)
    mg = jnp.max(gl, axis=-1, keepdims=True)
    grp_lane = jnp.min(jnp.where(gl == mg, lane_f, 1e9), axis=-1, keepdims=True)
    p_g = 1.0 / jnp.sum(jnp.where(is_grp, jnp.exp(gl - mg), 0.0), axis=-1, keepdims=True)
    first = (grp_lane - N_EXPERTS) * EXPERTS_PER_GROUP
    in_grp = (lane_f >= first) & (lane_f < first + EXPERTS_PER_GROUP)
    el = jnp.where(in_grp, lg, NEG_BIG)
    t1 = jnp.max(el, axis=-1, keepdims=True)
    i1 = jnp.min(jnp.where(el == t1, lane_f, 1e9), axis=-1, keepdims=True)
    el2 = jnp.where(lane_f == i1, NEG_BIG, el)
    t2 = jnp.max(el2, axis=-1, keepdims=True)
    i2 = jnp.min(jnp.where(el2 == t2, lane_f, 1e9), axis=-1, keepdims=True)
    s = jnp.exp(t2 - t1)
    w1 = p_g / (1.0 + s)
    w2 = p_g * s / (1.0 + s)

    oh1 = (lane_f == i1)
    oh2 = (lane_f == i2)
    oh = (oh1 | oh2).astype(BF16)
    before = _dot(tri_ref[...], oh) + running
    rank1 = jnp.sum(jnp.where(oh1, before, 0.0), axis=-1, keepdims=True)
    rank2 = jnp.sum(jnp.where(oh2, before, 0.0), axis=-1, keepdims=True)

    route_ref[rows, :] = jnp.where(lane == 0, w1, jnp.where(lane == 1, w2, 0.0))

    row = lax.broadcasted_iota(jnp.int32, (tm, LANES), 0)
    on_diag = (row % LANES) == lane
    per = tm // LANES
    n_sub = x_ref.shape[0] // tm
    for qi, col in enumerate((i1, i2, rank1, rank2)):
        picked = jnp.where(on_diag, col, 0.0)
        dense = jnp.sum(picked.reshape(per, LANES, LANES), axis=1).astype(jnp.int32)
        slots_ref[0, (qi * n_sub + sub) * per:(qi * n_sub + sub + 1) * per, :] = dense
    return running + jnp.sum(oh.astype(F32), axis=0, keepdims=True)


def _outproj_router(yr, ys, x2, mod3, npost, npre, wo_r, wo_s, w_router, b_router, seq_len):
    T, d = x2.shape
    tm = TM_OUT
    per_seq = seq_len // tm
    rw = yr.shape[1]
    sub = TM_ROUTE_SUB
    tri = (jnp.arange(sub)[:, None] > jnp.arange(sub)[None, :]).astype(BF16)
    tok = lambda w: pl.BlockSpec((tm, w), lambda i: (i, 0))
    modv = lambda k: pl.BlockSpec((1, 1, d), lambda i: (i // per_seq, 0, k))
    return pl.pallas_call(
        _outproj_router_kernel,
        grid=(T // tm,),
        in_specs=[
            tok(rw), tok(rw), tok(d), modv(2), modv(3), modv(4),
            _const_spec((1, d)), _const_spec((1, d)),
            _const_spec(wo_r.shape), _const_spec(wo_s.shape), _const_spec(w_router.shape), _const_spec((1, LANES)),
            _const_spec((sub, sub)),
        ],
        out_specs=[tok(d), tok(d), tok(LANES), _const_spec((1, LANES)),
                   pl.BlockSpec((1, 2 * TOP_K * (tm // LANES), LANES), lambda i: (i, 0, 0))],
        out_shape=[jax.ShapeDtypeStruct((T, d), F32), jax.ShapeDtypeStruct((T, d), F32),
                   jax.ShapeDtypeStruct((T, LANES), F32), jax.ShapeDtypeStruct((1, LANES), F32),
                   jax.ShapeDtypeStruct((T // tm, 2 * TOP_K * (tm // LANES), LANES), jnp.int32)],
        scratch_shapes=[pltpu.VMEM((d, 2 * LANES), BF16), pltpu.VMEM((1, LANES), F32)],
        compiler_params=pltpu.CompilerParams(dimension_semantics=("arbitrary",),
                                             vmem_limit_bytes=48 * 1024 * 1024),
        name="outproj_router",
    )(yr, ys, x2, mod3, mod3, mod3, npost, npre, wo_r, wo_s, w_router, b_router, tri)


def _slot_rows_kernel(start_ref, slots_ref, o_ref):
    half = slots_ref.shape[1] // 2
    expert = slots_ref[:, :half, :]
    row = slots_ref[:, half:, :]
    for e in range(N_EXPERTS):
        row = row + jnp.where(expert == e, start_ref[e], 0)
    for r in range(half):
        o_ref[:, 0, r * LANES:(r + 1) * LANES] = row[:, r, :]


def _slot_rows(pad_start, slots):
    nt, rows, _ = slots.shape
    grid_spec = pltpu.PrefetchScalarGridSpec(
        num_scalar_prefetch=1,
        grid=(1,),
        in_specs=[pl.BlockSpec(slots.shape, lambda i, ps: (0, 0, 0))],
        out_specs=pl.BlockSpec((nt, 1, rows // 2 * LANES), lambda i, ps: (0, 0, 0)),
    )
    return pl.pallas_call(
        _slot_rows_kernel,
        grid_spec=grid_spec,
        out_shape=jax.ShapeDtypeStruct((nt, 1, rows // 2 * LANES), jnp.int32),
        name="slot_rows",
    )(pad_start, slots)


def _dispatch_kernel(pad_end_ref, zero_from_ref, dest_ref, h_hbm, xs_hbm, zbuf, hbuf, sem, zsem, in_sem):
    i = pl.program_id(0)
    nt = pl.num_programs(0)
    td = hbuf.shape[1]
    mb = zbuf.shape[0]

    @pl.when(i == 0)
    def _():
        zbuf[...] = jnp.zeros_like(zbuf)

        def pieces(e, act):
            for p in range(mb // ZERO_PIECE):
                row = zero_from_ref[e] + p * ZERO_PIECE

                @pl.when(row < pad_end_ref[e])
                def _():
                    act(pltpu.make_async_copy(zbuf.at[pl.ds(0, ZERO_PIECE)],
                                              xs_hbm.at[pl.ds(pl.multiple_of(row, ZERO_PIECE), ZERO_PIECE)], zsem))

        def start(e, carry):
            pieces(e, lambda c: c.start())
            return carry

        def wait(e, carry):
            pieces(e, lambda c: c.wait())
            return carry

        def tail_fill(blk):
            return pltpu.make_async_copy(zbuf, xs_hbm.at[pl.ds(pl.multiple_of(blk * mb, mb), mb)], zsem)

        def tail_start(blk, carry):
            @pl.when(blk * mb >= pad_end_ref[N_EXPERTS - 1])
            def _():
                tail_fill(blk).start()
            return carry

        def tail_wait(blk, carry):
            @pl.when(blk * mb >= pad_end_ref[N_EXPERTS - 1])
            def _():
                tail_fill(blk).wait()
            return carry

        n_blocks = xs_hbm.shape[0] // mb
        lax.fori_loop(0, N_EXPERTS, start, 0)
        lax.fori_loop(0, n_blocks, tail_start, 0)
        lax.fori_loop(0, N_EXPERTS, wait, 0)
        lax.fori_loop(0, n_blocks, tail_wait, 0)

    def fetch(t):
        return pltpu.make_async_copy(h_hbm.at[pl.ds(pl.multiple_of(t * td, td), td)], hbuf.at[t % DISPATCH_RING],
                                     in_sem.at[t % DISPATCH_RING])

    @pl.when(i == 0)
    def _():
        fetch(i).start()

    @pl.when(i + 1 < nt)
    def _():
        fetch(i + 1).start()

    fetch(i).wait()
    for slot in range(DISPATCH_RING):
        @pl.when(i % DISPATCH_RING == slot)
        def _(slot=slot):
            _issue_row_copies(dest_ref, td, lambda kk, j, d: pltpu.make_async_copy(
                hbuf.at[slot, pl.ds(j, 1)], xs_hbm.at[pl.ds(d, 1)], sem.at[slot]))

    def drain(t):
        for kk in range(TOP_K):
            pltpu.make_async_copy(hbuf.at[t % DISPATCH_RING], xs_hbm.at[pl.ds(0, td)],
                                  sem.at[t % DISPATCH_RING]).wait()

    @pl.when(i > 0)
    def _():
        drain(i - 1)

    @pl.when(i == nt - 1)
    def _():
        drain(i)


def _issue_row_copies(dest_ref, n_tok, row_copy):
    def issue(j, carry):
        for kk in range(TOP_K):
            row_copy(kk, j, dest_ref[0, 0, kk * n_tok + j]).start()
        return carry

    lax.fori_loop(0, n_tok, issue, 0, unroll=DMA_UNROLL)


def _dispatch(pad_end, zero_from, dest, h2, cap):
    T, d = h2.shape
    td = TD_DISPATCH
    nt = T // td
    grid_spec = pltpu.PrefetchScalarGridSpec(
        num_scalar_prefetch=2,
        grid=(nt,),
        in_specs=[
            pl.BlockSpec((1, 1, TOP_K * td), lambda i, pe, zf: (i, 0, 0), memory_space=pltpu.SMEM),
            pl.BlockSpec(memory_space=pl.ANY),
        ],
        out_specs=pl.BlockSpec(memory_space=pl.ANY),
        scratch_shapes=[pltpu.VMEM((MB_EXPERT, d), h2.dtype), pltpu.VMEM((DISPATCH_RING, td, d), h2.dtype),
                        pltpu.SemaphoreType.DMA((DISPATCH_RING,)), pltpu.SemaphoreType.DMA(()),
                        pltpu.SemaphoreType.DMA((DISPATCH_RING,))],
    )
    return pl.pallas_call(
        _dispatch_kernel,
        grid_spec=grid_spec,
        out_shape=jax.ShapeDtypeStruct((cap, d), h2.dtype),
        compiler_params=pltpu.CompilerParams(dimension_semantics=("arbitrary",)),
        name="dispatch",
    )(pad_end, zero_from, dest, h2)


def _expert_kernel(be_ref, first_ref, slot_ref, next_ref, nused_ref, xs_ref, wg_hbm, wu_hbm, wd_hbm, y_ref,
                   wg_f, wu_f, wd_f, wg_b, wu_b, wd_b, sem):
    i = pl.program_id(0)

    def fetch(e, s):
        return [pltpu.make_async_copy(src.at[e], dst.at[s], sem.at[s, k])
                for k, (src, dst) in enumerate(((wg_hbm, wg_f), (wu_hbm, wu_f), (wd_hbm, wd_f)))]

    @pl.when(i == 0)
    def _():
        for c in fetch(be_ref[0], slot_ref[0]):
            c.start()

    @pl.when(first_ref[i] == 1)
    def _():
        s = slot_ref[i]

        @pl.when(next_ref[i] >= 0)
        def _():
            for c in fetch(next_ref[i], 1 - s):
                c.start()

        for c in fetch(be_ref[i], s):
            c.wait()
        wg_b[...] = wg_f[s].astype(BF16)
        wu_b[...] = wu_f[s].astype(BF16)
        wd_b[...] = wd_f[s].astype(BF16)

    @pl.when(i < nused_ref[0])
    def _():
        xb = xs_ref[...].astype(BF16)
        hid = (_silu(_dot(xb, wg_b[...])) * _dot(xb, wu_b[...])).astype(BF16)
        y_ref[...] = _dot(hid, wd_b[...])

    @pl.when(i >= nused_ref[0])
    def _():
        y_ref[...] = jnp.zeros_like(y_ref)


def _expert_plan(padded, pad_end, n_blocks, mb):
    n_used = (pad_end[-1:] // mb).astype(jnp.int32)
    blk_start = jnp.arange(n_blocks, dtype=jnp.int32) * mb
    experts = jnp.arange(N_EXPERTS, dtype=jnp.int32)
    blk_expert = jnp.minimum(jnp.sum((pad_end[None, :] <= blk_start[:, None]).astype(jnp.int32), axis=1),
                             N_EXPERTS - 1)
    prev = jnp.concatenate([jnp.full((1,), -1, jnp.int32), blk_expert[:-1]])
    first = ((blk_start < pad_end[-1]) & (blk_expert != prev)).astype(jnp.int32)
    slot = (jnp.cumsum(first) - 1) % 2
    later = jnp.where((padded > 0)[None, :] & (experts[None, :] > experts[:, None]), experts[None, :], N_EXPERTS)
    next_nonempty = jnp.min(later, axis=1)
    next_nonempty = jnp.where(next_nonempty == N_EXPERTS, -1, next_nonempty)
    nxt = jnp.sum(jnp.where(blk_expert[:, None] == experts[None, :], next_nonempty[None, :], 0), axis=1)
    return [a.astype(jnp.int32) for a in (blk_expert, first, slot, nxt, n_used)]


def _experts(plan, xs, w_gate, w_up, w_down):
    cap, dp = xs.shape
    n_exp, d, de = w_gate.shape
    assert dp == d
    mb = MB_EXPERT
    grid_spec = pltpu.PrefetchScalarGridSpec(
        num_scalar_prefetch=len(plan),
        grid=(cap // mb,),
        in_specs=[
            pl.BlockSpec((mb, dp), lambda i, be, fi, sl, nx, nu: (jnp.minimum(i, nu[0] - 1), 0)),
            pl.BlockSpec(memory_space=pl.ANY), pl.BlockSpec(memory_space=pl.ANY), pl.BlockSpec(memory_space=pl.ANY),
        ],
        out_specs=pl.BlockSpec((mb, dp), lambda i, be, fi, sl, nx, nu: (i, 0)),
        scratch_shapes=[pltpu.VMEM((2, d, de), F32), pltpu.VMEM((2, d, de), F32), pltpu.VMEM((2, de, d), F32),
                        pltpu.VMEM((d, de), BF16), pltpu.VMEM((d, de), BF16), pltpu.VMEM((de, d), BF16),
                        pltpu.SemaphoreType.DMA((2, 3))],
    )
    return pl.pallas_call(
        _expert_kernel,
        grid_spec=grid_spec,
        out_shape=jax.ShapeDtypeStruct((cap, dp), xs.dtype),
        compiler_params=pltpu.CompilerParams(dimension_semantics=("arbitrary",),
                                             vmem_limit_bytes=48 * 1024 * 1024),
        name="experts",
    )(*plan, xs, w_gate, w_up, w_down)


def _combine_kernel(dest_ref, dest_next_ref, route_ref, x1_ref, g2_ref, nw_ref, yb_hbm, o_ref, buf, sem):
    i = pl.program_id(0)
    nt = pl.num_programs(0)
    tf = x1_ref.shape[0]
    stage = i % 2

    def gather(refs, st):
        _issue_row_copies(refs, tf, lambda kk, j, d: pltpu.make_async_copy(
            yb_hbm.at[pl.ds(d, 1)], buf.at[st, kk, pl.ds(j, 1)], sem.at[st]))

    @pl.when(i == 0)
    def _():
        gather(dest_ref, 0)

    for st in range(2):
        @pl.when((i + 1 < nt) & (stage != st))
        def _(st=st):
            gather(dest_next_ref, st)

    for kk in range(TOP_K):
        pltpu.make_async_copy(yb_hbm.at[pl.ds(0, tf)], buf.at[stage, kk], sem.at[stage]).wait()

    route = route_ref[...]
    out = route[:, 0:1] * buf[stage, 0] + route[:, 1:2] * buf[stage, 1]
    ms = jnp.mean(out * out, axis=-1, keepdims=True)
    o_ref[...] = x1_ref[...] + g2_ref[0] * ((out * lax.rsqrt(ms + EPS)) * nw_ref[...])


def _combine(dest, route, x1, mod3, nw, yb, seq_len):
    T, d = x1.shape
    tf = TF_COMBINE
    nt = T // tf
    per_seq = seq_len // tf
    slot_spec = functools.partial(pl.BlockSpec, (1, 1, TOP_K * tf), memory_space=pltpu.SMEM)
    return pl.pallas_call(
        _combine_kernel,
        grid=(nt,),
        in_specs=[
            slot_spec(lambda i: (i, 0, 0)),
            slot_spec(lambda i: (jnp.minimum(i + 1, nt - 1), 0, 0)),
            pl.BlockSpec((tf, LANES), lambda i: (i, 0)),
            pl.BlockSpec((tf, d), lambda i: (i, 0)),
            pl.BlockSpec((1, 1, d), lambda i: (i // per_seq, 0, 5)),
            _const_spec((1, d)),
            pl.BlockSpec(memory_space=pl.ANY),
        ],
        out_specs=pl.BlockSpec((tf, d), lambda i: (i, 0)),
        out_shape=jax.ShapeDtypeStruct((T, d), F32),
        scratch_shapes=[pltpu.VMEM((2, TOP_K, tf, yb.shape[1]), yb.dtype), pltpu.SemaphoreType.DMA((2,))],
        compiler_params=pltpu.CompilerParams(dimension_semantics=("arbitrary",),
                                             vmem_limit_bytes=48 * 1024 * 1024),
        name="combine",
    )(dest, dest, route, x1, mod3, nw, yb)


def _rope_tables(L, n_heads):
    quarter = RET_DK // 4
    freqs = ROPE_BASE ** (-jnp.arange(quarter, dtype=F32) / quarter)
    t = jnp.arange(L)
    ang_r = (t // GRID_W).astype(F32)[:, None] * freqs
    ang_c = (t % GRID_W).astype(F32)[:, None] * freqs
    cos = jnp.concatenate([jnp.cos(ang_r)] * 2 + [jnp.cos(ang_c)] * 2, axis=-1)
    sin = jnp.concatenate([-jnp.sin(ang_r), jnp.sin(ang_r), -jnp.sin(ang_c), jnp.sin(ang_c)], axis=-1)
    return jnp.tile(cos, (1, n_heads)), jnp.tile(sin, (1, n_heads))


def _lane_pad(v, width=LANES):
    return jnp.pad(v, [(0, 0)] * (v.ndim - 1) + [(0, width - v.shape[-1])])


def kernel(x, c, ctx, c_ctx, w_mod, b_mod, norm_pre_mix, norm_post_mix, norm_pre_ffn, norm_post_ffn, w_in, w_out, ret_decay_f, ret_decay_b, ret_gn_w, ssd_conv_w, ssd_conv_b, ssd_dt_bias_f, ssd_dt_bias_b, ssd_a_log_f, ssd_a_log_b, ssd_d, ssd_norm_w, moe_w_rg, moe_b_rg, moe_w_re, moe_b_re, moe_w_gate, moe_w_up, moe_w_down):
    b, L, d = x.shape
    assert w_mod.shape[0] == 1, "single layer: context outputs are never needed"
    assert TM_OUT == TD_DISPATCH == TF_COMBINE, "router, dispatch and combine share one slot-row layout"
    rw = RET_HEADS * RET_DK
    nconv = SSD_WIDTH + 2 * SSD_GROUPS * SSD_STATE
    T = b * L

    mod_rows = -(-(b + 1) // SUBLANES) * SUBLANES
    c_all = jnp.zeros((mod_rows, d), F32).at[:b].set(c).at[b].set(c_ctx)
    mod3 = _modulation(c_all, w_mod[0], b_mod[0]).reshape(mod_rows, 1, 6 * d)

    wi = w_in[0]
    o = 0
    wq = wi[:, o:o + rw]; o += rw
    wk = wi[:, o:o + rw]; o += rw
    wv = wi[:, o:o + rw]; o += rw
    wg = wi[:, o:o + rw]; o += rw
    wz = wi[:, o:o + SSD_WIDTH]; o += SSD_WIDTH
    wxbc = wi[:, o:o + nconv].astype(BF16); o += nconv
    wdt = _lane_pad(wi[:, o:o + 2 * SSD_HEADS]).astype(BF16)
    wqk = jnp.concatenate([wq, wk], axis=1).astype(BF16)
    wvgz = jnp.concatenate([wv, wg, wz], axis=1).astype(BF16)
    cos_t, sin_t = _rope_tables(L, RET_HEADS)
    nw1 = norm_pre_mix[0].reshape(1, d)

    q, k, v, g, z, xbc, dt = _inproj(x, mod3, nw1, wqk, wvgz, wxbc, wdt, cos_t, sin_t)
    kc, vc, xbcc, dtc = _inproj_ctx(ctx, mod3, b, nw1, wk.astype(BF16), wv.astype(BF16), wxbc, wdt)

    conv_w8 = jnp.pad(ssd_conv_w[0], ((0, SUBLANES - SSD_CONV), (0, 0)))
    dt_bias = _lane_pad(jnp.concatenate([ssd_dt_bias_f[0], ssd_dt_bias_b[0]])[None, :])
    a_log = _lane_pad(jnp.concatenate([ssd_a_log_f[0], ssd_a_log_b[0]])[None, :])
    d_skip = jnp.repeat(ssd_d[0], SSD_HEADDIM)[None, :]
    ys = _ssd(xbc, z, dt, xbcc, dtc, conv_w8, ssd_conv_b[0][None, :], dt_bias, a_log, d_skip,
              ssd_norm_w[0][None, :])

    yr = _retention(q, k, v, g, kc, vc,
                    jnp.repeat(ret_decay_f[0], RET_DK)[None, :], jnp.repeat(ret_decay_b[0], RET_DK)[None, :],
                    ret_gn_w[0][None, :])

    wo = w_out[0].astype(BF16)
    w_router = _lane_pad(jnp.concatenate(
        [jnp.transpose(moe_w_re[0], (1, 0, 2)).reshape(d, N_EXPERTS), moe_w_rg[0]], axis=1))
    b_router = _lane_pad(jnp.concatenate([moe_b_re[0].reshape(-1), moe_b_rg[0]])[None, :])
    x1, h2, route, counts, slots = _outproj_router(
        yr.reshape(T, rw), ys.reshape(T, SSD_WIDTH), x.reshape(T, d), mod3,
        norm_post_mix[0][None, :], norm_pre_ffn[0][None, :], wo[:rw], wo[rw:], w_router, b_router, L)

    mb = MB_EXPERT
    n_blocks = -(-(T * TOP_K + N_EXPERTS * (mb - 1)) // mb)
    cap = n_blocks * mb
    cnt = counts[0, :N_EXPERTS].astype(jnp.int32)
    padded = (cnt + mb - 1) // mb * mb
    pad_end = jnp.cumsum(padded)
    pad_start = pad_end - padded

    dest = _slot_rows(pad_start.astype(jnp.int32), slots)
    zero_from = pad_start + cnt // ZERO_PIECE * ZERO_PIECE
    xs = _dispatch(pad_end.astype(jnp.int32), zero_from.astype(jnp.int32), dest, h2, cap)
    yb = _experts(_expert_plan(padded, pad_end, n_blocks, mb), xs, moe_w_gate[0], moe_w_up[0], moe_w_down[0])
    out = _combine(dest, route, x1, mod3, norm_post_ffn[0][None, :], yb, L)
    return out.reshape(b, L, d)
```

```python
import functools
import math

import jax
import jax.numpy as jnp
from jax import lax
from jax.experimental import pallas as pl
from jax.experimental.pallas import tpu as pltpu

F32 = jnp.float32
BF16 = jnp.bfloat16

LANES = 128
SUBLANES = 8

EPS = 1e-6
CHUNK = 128
GRID_W = 64
RET_HEADS = 4
RET_DK = 128
ROPE_BASE = 10000.0
SSD_HEADS = 8
SSD_HEADDIM = 64
SSD_GROUPS = 2
SSD_STATE = 128
SSD_WIDTH = SSD_HEADS * SSD_HEADDIM
SSD_CONV = 5
SSD_PAIRS = SSD_WIDTH // LANES
MOE_GROUPS = 4
EXPERTS_PER_GROUP = 8
N_EXPERTS = MOE_GROUPS * EXPERTS_PER_GROUP
TOP_K = 2
CONV_HALO = SUBLANES

TM_PROJ = 512
TM_OUT = 512
TM_ROUTE_SUB = 512
TD_DISPATCH = 512
MB_EXPERT = 512
ZERO_PIECE = 64
TF_COMBINE = 512
DMA_UNROLL = 8
DISPATCH_RING = 3
SSD_UNROLL = 1
RET_UNROLL = 2
NEG_BIG = -1e30


def _silu(v):
    return v * jax.nn.sigmoid(v)


def _dot(a, b):
    return jnp.dot(a, b, preferred_element_type=F32)


def _dot_tn(a, b):
    return lax.dot_general(a, b, (((0,), (0,)), ((), ())), preferred_element_type=F32)


def _dot_nt(a, b):
    return lax.dot_general(a, b, (((1,), (1,)), ((), ())), preferred_element_type=F32)


def _mod_kernel(c_ref, w_ref, b_ref, o_ref):
    a = _silu(c_ref[...])
    w = w_ref[...]
    a_hi = a.astype(BF16)
    a_lo = (a - a_hi.astype(F32)).astype(BF16)
    w_hi = w.astype(BF16)
    w_lo = (w - w_hi.astype(F32)).astype(BF16)
    o_ref[...] = _dot(a_hi, w_hi) + _dot(a_lo, w_hi) + _dot(a_hi, w_lo) + b_ref[...]


def _modulation(c_all, w_mod, b_mod):
    rows, d = c_all.shape
    n = w_mod.shape[1]
    return pl.pallas_call(
        _mod_kernel,
        grid=(n // d,),
        in_specs=[
            pl.BlockSpec((rows, d), lambda j: (0, 0)),
            pl.BlockSpec((d, d), lambda j: (0, j)),
            pl.BlockSpec((1, d), lambda j: (0, j)),
        ],
        out_specs=pl.BlockSpec((rows, d), lambda j: (0, j)),
        out_shape=jax.ShapeDtypeStruct((rows, n), F32),
        name="modulation",
    )(c_all, w_mod, b_mod.reshape(1, n))


def _norm_mod(x, nw, sc, sh):
    ms = jnp.mean(x * x, axis=-1, keepdims=True)
    return (x * lax.rsqrt(ms + EPS)) * (nw * (1.0 + sc)) + sh


def _rope(t, cos, sin_signed, first_half):
    width = t.shape[-1]
    quarter = RET_DK // 4
    swapped = jnp.where(first_half, pltpu.roll(t, width - quarter, 1), pltpu.roll(t, quarter, 1))
    return t * cos + swapped * sin_signed


def _inproj_kernel(x_ref, sh_ref, sc_ref, nw_ref, wqk_ref, wvgz_ref, wxbc_ref, wdt_ref, cos_ref, sin_ref,
                   q_ref, k_ref, v_ref, g_ref, z_ref, xbc_ref, dt_ref):
    hb = _norm_mod(x_ref[0], nw_ref[...], sc_ref[0], sh_ref[0]).astype(BF16)
    rw = q_ref.shape[-1]
    qk = _dot(hb, wqk_ref[...])
    cos = cos_ref[...]
    sin = sin_ref[...]
    lane = lax.broadcasted_iota(jnp.int32, cos.shape, 1)
    first_half = (lane % (RET_DK // 2)) < (RET_DK // 4)
    q_ref[0] = _rope(qk[:, :rw], cos, sin, first_half).astype(BF16)
    k_ref[0] = (_rope(qk[:, rw:], cos, sin, first_half) * (RET_DK ** -0.5)).astype(BF16)
    vgz = _dot(hb, wvgz_ref[...])
    v_ref[0] = vgz[:, :rw].astype(BF16)
    g_ref[0] = vgz[:, rw:2 * rw].astype(BF16)
    z_ref[0] = vgz[:, 2 * rw:].astype(BF16)
    xbc_ref[0] = _dot(hb, wxbc_ref[...]).astype(BF16)
    dt_ref[0] = _dot(hb, wdt_ref[...])


def _inproj_ctx_kernel(x_ref, sh_ref, sc_ref, nw_ref, wk_ref, wv_ref, wxbc_ref, wdt_ref,
                       k_ref, v_ref, xbc_ref, dt_ref):
    hb = _norm_mod(x_ref[0], nw_ref[...], sc_ref[0], sh_ref[0]).astype(BF16)
    k_ref[0] = (_dot(hb, wk_ref[...]) * (RET_DK ** -0.5)).astype(BF16)
    v_ref[0] = _dot(hb, wv_ref[...]).astype(BF16)
    xbc_ref[0] = _dot(hb, wxbc_ref[...]).astype(BF16)
    dt_ref[0] = _dot(hb, wdt_ref[...])


def _const_spec(shape):
    nd = len(shape)
    return pl.BlockSpec(shape, lambda *_: (0,) * nd)


def _inproj(x, mod3, nw, wqk, wvgz, wxbc, wdt, cos_t, sin_t):
    b, L, d = x.shape
    tm = min(TM_PROJ, L)
    rw = wqk.shape[1] // 2
    tok = lambda w: pl.BlockSpec((1, tm, w), lambda i, j: (i, j, 0))
    out_bf = lambda w: jax.ShapeDtypeStruct((b, L, w), BF16)
    return pl.pallas_call(
        _inproj_kernel,
        grid=(b, L // tm),
        in_specs=[
            tok(d),
            pl.BlockSpec((1, 1, d), lambda i, j: (i, 0, 0)),
            pl.BlockSpec((1, 1, d), lambda i, j: (i, 0, 1)),
            _const_spec((1, d)),
            _const_spec(wqk.shape), _const_spec(wvgz.shape), _const_spec(wxbc.shape), _const_spec(wdt.shape),
            pl.BlockSpec((tm, rw), lambda i, j: (j, 0)),
            pl.BlockSpec((tm, rw), lambda i, j: (j, 0)),
        ],
        out_specs=[tok(rw), tok(rw), tok(rw), tok(rw), tok(rw), tok(wxbc.shape[1]), tok(LANES)],
        out_shape=[out_bf(rw), out_bf(rw), out_bf(rw), out_bf(rw), out_bf(rw), out_bf(wxbc.shape[1]),
                   jax.ShapeDtypeStruct((b, L, LANES), F32)],
        compiler_params=pltpu.CompilerParams(vmem_limit_bytes=48 * 1024 * 1024),
        name="inproj",
    )(x, mod3, mod3, nw, wqk, wvgz, wxbc, wdt, cos_t, sin_t)


def _inproj_ctx(ctx, mod3, ctx_row, nw, wk, wv, wxbc, wdt):
    b, L, d = ctx.shape
    tm = min(TM_PROJ, L)
    rw = wk.shape[1]
    tok = lambda w: pl.BlockSpec((1, tm, w), lambda i, j: (i, j, 0))
    out_bf = lambda w: jax.ShapeDtypeStruct((b, L, w), BF16)
    return pl.pallas_call(
        _inproj_ctx_kernel,
        grid=(b, L // tm),
        in_specs=[
            tok(d),
            pl.BlockSpec((1, 1, d), lambda i, j: (ctx_row, 0, 0)),
            pl.BlockSpec((1, 1, d), lambda i, j: (ctx_row, 0, 1)),
            _const_spec((1, d)),
            _const_spec(wk.shape), _const_spec(wv.shape), _const_spec(wxbc.shape), _const_spec(wdt.shape),
        ],
        out_specs=[tok(rw), tok(rw), tok(wxbc.shape[1]), tok(LANES)],
        out_shape=[out_bf(rw), out_bf(rw), out_bf(wxbc.shape[1]), jax.ShapeDtypeStruct((b, L, LANES), F32)],
        compiler_params=pltpu.CompilerParams(vmem_limit_bytes=48 * 1024 * 1024),
        name="inproj_ctx",
    )(ctx, mod3, mod3, nw, wk, wv, wxbc, wdt)


def _ssd_kernel(xbc_ref, z_ref, dt_ref, xbcc_ref, dtc_ref, cw_ref, cb_ref, dtb_ref, alog_ref, dsk_ref, nw_ref,
                y_ref,
                xpad, xpadc, u, uc, dtv, dav, dtcv, dacv, sf_scr, kb_scr, acum, ecum, dec_scr,
                arow_scr, erow_scr, dtrow_scr):
    L = xbc_ref.shape[1]
    Lc = xbcc_ref.shape[1]
    nch = L // CHUNK
    nchc = Lc // CHUNK
    win = CHUNK + 2 * CONV_HALO
    nconv = xbc_ref.shape[2]
    nh = SSD_HEADS

    def conv_pass(src_ref, pad_ref, dst_ref, n_chunks, length):
        zeros = jnp.zeros((CONV_HALO, nconv), F32)
        pad_ref[0:CONV_HALO, :] = zeros
        pad_ref[CONV_HALO + length:2 * CONV_HALO + length, :] = zeros
        pad_ref[CONV_HALO:CONV_HALO + length, :] = src_ref[0].astype(F32)

        def chunk(c, carry):
            base = pl.multiple_of(c * CHUNK, CHUNK)
            for cb_i in range(nconv // LANES):
                cols = slice(cb_i * LANES, (cb_i + 1) * LANES)
                w = pad_ref[pl.ds(base, win), cols]
                acc = cb_ref[:, cols] + w[CONV_HALO:CONV_HALO + CHUNK] * cw_ref[SSD_CONV // 2:SSD_CONV // 2 + 1, cols]
                for j in range(SSD_CONV):
                    if j == SSD_CONV // 2:
                        continue
                    shifted = pltpu.roll(w, (SSD_CONV // 2 - j) % win, 0)
                    acc = acc + shifted[CONV_HALO:CONV_HALO + CHUNK] * cw_ref[j:j + 1, cols]
                dst_ref[pl.ds(base, CHUNK), cols] = _silu(acc).astype(BF16)
            return carry

        lax.fori_loop(0, n_chunks, chunk, 0)

    conv_pass(xbcc_ref, xpadc, uc, nchc, Lc)
    conv_pass(xbc_ref, xpad, u, nch, L)

    a_neg = -jnp.exp(alog_ref[...])
    dtv[...] = jax.nn.softplus(dt_ref[0] + dtb_ref[...])
    dav[...] = dtv[...] * a_neg
    dtcv[...] = jax.nn.softplus(dtc_ref[0] + dtb_ref[...])
    dacv[...] = dtcv[...] * a_neg

    row_i = lax.broadcasted_iota(jnp.int32, (CHUNK, CHUNK), 0)
    col_i = lax.broadcasted_iota(jnp.int32, (CHUNK, CHUNK), 1)
    causal = col_i <= row_i
    lo_half = col_i < SSD_HEADDIM
    fwd_lane = col_i < nh
    head_of = lax.broadcasted_iota(jnp.int32, (CHUNK, SSD_WIDTH), 1) // SSD_HEADDIM
    src_col = lax.broadcasted_iota(jnp.int32, (CHUNK, SSD_WIDTH), 0)
    exp_f = (head_of == src_col).astype(BF16)
    exp_b = (head_of == src_col - nh).astype(BF16)
    exp_fb = jnp.concatenate([exp_f, exp_b], axis=1)

    def split3(v):
        hi = v.astype(BF16)
        r1 = v - hi.astype(F32)
        mid = r1.astype(BF16)
        return hi, mid, (r1 - mid.astype(F32)).astype(BF16)

    def times_onehot(v, m, passes=3):
        parts = split3(v)[:passes]
        acc = _dot(parts[0], m)
        for part in parts[1:]:
            acc = acc + _dot(part, m)
        return acc

    def colb(mat, r):
        return jnp.broadcast_to(mat[:, r:r + 1], (CHUNK, CHUNK))

    def pair_sel(a, b_):
        return jnp.where(lo_half, a, b_)

    gw = 2 * LANES

    def chunk_terms(u_ref, dt_s, da_s, base):
        dt = dt_s[pl.ds(base, CHUNK), :]
        da = da_s[pl.ds(base, CHUNK), :]
        acol = da
        for step in (1, 2, 4, 8, 16, 32, 64):
            acol = acol + jnp.where(row_i >= step, pltpu.roll(acol, step, 0), 0.0)
        ecol = acol - da
        last = acol[CHUNK - 1:CHUNK, :]
        wgt = jnp.where(fwd_lane, jnp.exp(last - acol), jnp.exp(ecol)) * dt
        scale = jnp.where(fwd_lane, jnp.exp(acol), jnp.exp(last - ecol))
        wide = times_onehot(jnp.concatenate([wgt, scale], axis=0), exp_fb, passes=1)
        dec = times_onehot(jnp.broadcast_to(jnp.exp(last), (SUBLANES, LANES)), exp_fb)[0:1]
        xs = u_ref[pl.ds(base, CHUNK), 0:SSD_WIDTH].astype(F32)
        kmats = []
        for g in range(SSD_GROUPS):
            xw = jnp.concatenate([xs[:, g * gw:(g + 1) * gw] * wide[:CHUNK, g * gw:(g + 1) * gw],
                                  xs[:, g * gw:(g + 1) * gw] * wide[:CHUNK, SSD_WIDTH + g * gw:SSD_WIDTH + (g + 1) * gw]],
                                 axis=1).astype(BF16)
            bm = u_ref[pl.ds(base, CHUNK), SSD_WIDTH + g * SSD_STATE:SSD_WIDTH + (g + 1) * SSD_STATE]
            kmats.append(_dot_tn(bm, xw))
        return dt, acol, ecol, wide[CHUNK:], dec, kmats

    def advance(s, dec, kmats, backward):
        off = SSD_WIDTH if backward else 0
        koff = gw if backward else 0
        return [dec[:, off + g * gw:off + (g + 1) * gw] * s[g] + kmats[g][:, koff:koff + gw]
                for g in range(SSD_GROUPS)]

    ctx_terms = [chunk_terms(uc, dtcv, dacv, c * CHUNK) for c in range(nchc)]
    s_f0 = [jnp.zeros((SSD_STATE, gw), F32) for _ in range(SSD_GROUPS)]
    for c in range(nchc):
        s_f0 = advance(s_f0, ctx_terms[c][4], ctx_terms[c][5], False)
    s_b0 = [jnp.zeros((SSD_STATE, gw), F32) for _ in range(SSD_GROUPS)]
    for c in reversed(range(nchc)):
        s_b0 = advance(s_b0, ctx_terms[c][4], ctx_terms[c][5], True)

    def prep(c, carry):
        base = pl.multiple_of(c * CHUNK, CHUNK)
        dt, acol, ecol, scale, dec, kmats = chunk_terms(u, dtv, dav, base)
        acum[pl.ds(base, CHUNK), :] = acol
        ecum[pl.ds(base, CHUNK), :] = ecol
        hrow = pl.ds(pl.multiple_of(c * 2 * nh, 2 * nh), 2 * nh)
        arow_scr[hrow, :] = acol.T[:2 * nh]
        erow_scr[hrow, :] = ecol.T[:2 * nh]
        dtrow_scr[hrow, :] = dt.T[:2 * nh]
        xpad[pl.ds(base, CHUNK), :] = scale
        dec_scr[pl.ds(pl.multiple_of(c * SUBLANES, SUBLANES), SUBLANES), :] = jnp.broadcast_to(dec, (SUBLANES, 2 * SSD_WIDTH))
        for g in range(SSD_GROUPS):
            sf_scr[c, g] = kmats[g][:, :gw]
            kb_scr[c, g] = kmats[g][:, gw:]
        return carry

    lax.fori_loop(0, nch, prep, 0)

    def chunk_dec(c):
        return dec_scr[pl.ds(pl.multiple_of(c * SUBLANES, SUBLANES), 1), :]

    def fwd(c, s_old):
        dec = chunk_dec(c)
        new = []
        for g in range(SSD_GROUPS):
            new.append(dec[:, g * gw:(g + 1) * gw] * s_old[g] + sf_scr[c, g])
            sf_scr[c, g] = s_old[g]
        return tuple(new)

    lax.fori_loop(0, nch, fwd, tuple(s_f0))

    def bwd(i, s_b):
        c = nch - 1 - i
        base = pl.multiple_of(c * CHUNK, CHUNK)
        acol = acum[pl.ds(base, CHUNK), :]
        ecol = ecum[pl.ds(base, CHUNK), :]
        hrow = pl.ds(pl.multiple_of(c * 2 * nh, 2 * nh), 2 * nh)
        arow = arow_scr[hrow, :]
        erow = erow_scr[hrow, :]
        dt_t = dtrow_scr[hrow, :]
        scale = xpad[pl.ds(base, CHUNK), :]
        ys = []
        for g in range(SSD_GROUPS):
            bm = u[pl.ds(base, CHUNK), SSD_WIDTH + g * SSD_STATE:SSD_WIDTH + (g + 1) * SSD_STATE]
            cm = u[pl.ds(base, CHUNK), SSD_WIDTH + (SSD_GROUPS + g) * SSD_STATE:SSD_WIDTH + (SSD_GROUPS + g + 1) * SSD_STATE]
            cbm = _dot_nt(cm, bm)
            cs_f = _dot(cm, sf_scr[c, g].astype(BF16))
            cs_b = _dot(cm, s_b[g].astype(BF16))
            for pp in range(SSD_PAIRS // SSD_GROUPS):
                p = g * (SSD_PAIRS // SSD_GROUPS) + pp
                xs_b = u[pl.ds(base, CHUNK), p * LANES:(p + 1) * LANES]
                y_h = []
                for hh in range(2):
                    r = 2 * p + hh
                    arg = jnp.where(causal, colb(acol, r) - arow[r:r + 1, :],
                                    erow[nh + r:nh + r + 1, :] - colb(ecol, nh + r))
                    coef = jnp.where(causal, dt_t[r:r + 1, :], dt_t[nh + r:nh + r + 1, :])
                    gm = (cbm * (jnp.exp(arg) * coef)).astype(BF16)
                    y_h.append(_dot(gm, xs_b))
                sl = slice(pp * LANES, (pp + 1) * LANES)
                wl = slice(p * LANES, (p + 1) * LANES)
                wlb = slice(SSD_WIDTH + p * LANES, SSD_WIDTH + (p + 1) * LANES)
                ys.append(pair_sel(y_h[0], y_h[1]) + cs_f[:, sl] * scale[:, wl] + cs_b[:, sl] * scale[:, wlb]
                          + dsk_ref[:, wl] * xs_b.astype(F32))
        y = jnp.concatenate(ys, axis=1)
        y = y * _silu(z_ref[0, pl.ds(base, CHUNK), :].astype(F32))
        ms = jnp.mean(y * y, axis=-1, keepdims=True)
        y_ref[0, pl.ds(base, CHUNK), :] = ((y * lax.rsqrt(ms + EPS)) * nw_ref[...]).astype(BF16)
        dec = chunk_dec(c)
        return tuple(dec[:, SSD_WIDTH + g * gw:SSD_WIDTH + (g + 1) * gw] * s_b[g] + kb_scr[c, g]
                     for g in range(SSD_GROUPS))

    lax.fori_loop(0, nch, bwd, tuple(s_b0), unroll=SSD_UNROLL)


def _ssd(xbc, z, dt, xbcc, dtc, conv_w8, conv_b, dt_bias, a_log, d_skip, norm_w):
    b, L, nconv = xbc.shape
    Lc = xbcc.shape[1]
    nch = L // CHUNK
    per_b = lambda n, w: pl.BlockSpec((1, n, w), lambda i: (i, 0, 0))
    return pl.pallas_call(
        _ssd_kernel,
        grid=(b,),
        in_specs=[
            per_b(L, nconv), per_b(L, SSD_WIDTH), per_b(L, LANES), per_b(Lc, nconv), per_b(Lc, LANES),
            _const_spec(conv_w8.shape), _const_spec(conv_b.shape), _const_spec(dt_bias.shape),
            _const_spec(a_log.shape), _const_spec(d_skip.shape), _const_spec(norm_w.shape),
        ],
        out_specs=per_b(L, SSD_WIDTH),
        out_shape=jax.ShapeDtypeStruct((b, L, SSD_WIDTH), BF16),
        scratch_shapes=[
            pltpu.VMEM((L + 2 * CONV_HALO, nconv), F32),
            pltpu.VMEM((Lc + 2 * CONV_HALO, nconv), F32),
            pltpu.VMEM((L, nconv), BF16),
            pltpu.VMEM((Lc, nconv), BF16),
            pltpu.VMEM((L, LANES), F32), pltpu.VMEM((L, LANES), F32),
            pltpu.VMEM((Lc, LANES), F32), pltpu.VMEM((Lc, LANES), F32),
            pltpu.VMEM((nch, SSD_GROUPS, SSD_STATE, 2 * LANES), F32),
            pltpu.VMEM((nch, SSD_GROUPS, SSD_STATE, 2 * LANES), F32),
            pltpu.VMEM((L, LANES), F32), pltpu.VMEM((L, LANES), F32),
            pltpu.VMEM((nch * SUBLANES, 2 * SSD_WIDTH), F32),
            pltpu.VMEM((nch * 2 * SSD_HEADS, CHUNK), F32), pltpu.VMEM((nch * 2 * SSD_HEADS, CHUNK), F32),
            pltpu.VMEM((nch * 2 * SSD_HEADS, CHUNK), F32),
        ],
        compiler_params=pltpu.CompilerParams(vmem_limit_bytes=56 * 1024 * 1024),
        name="ssd",
    )(xbc, z, dt, xbcc, dtc, conv_w8, conv_b, dt_bias, a_log, d_skip, norm_w)


def _ret_kernel(q_ref, k_ref, v_ref, g_ref, kc_ref, vc_ref, df_ref, db_ref, gn_ref, y_ref, sf_scr):
    L = q_ref.shape[1]
    Lc = kc_ref.shape[1]
    nch = L // CHUNK
    dk = RET_DK
    row_i = lax.broadcasted_iota(jnp.int32, (CHUNK, dk), 0).astype(F32)
    col_i = lax.broadcasted_iota(jnp.int32, (CHUNK, dk), 1).astype(F32)
    rel = row_i - col_i
    crow = lax.broadcasted_iota(jnp.int32, (Lc, dk), 0).astype(F32)

    heads = []
    s_f0 = []
    s_b0 = []
    for h in range(RET_HEADS):
        cols = slice(h * dk, (h + 1) * dk)
        lg_f = -jnp.exp(df_ref[:, cols])
        lg_b = -jnp.exp(db_ref[:, cols])
        heads.append(dict(
            cols=cols,
            dmat=jnp.where(rel >= 0, jnp.exp(jnp.maximum(rel, 0.0) * lg_f), jnp.exp(jnp.maximum(-rel, 0.0) * lg_b)),
            dq_f=jnp.exp((row_i + 1.0) * lg_f),
            dq_b=jnp.exp((CHUNK - row_i) * lg_b),
            dk_f=jnp.exp((CHUNK - 1.0 - row_i) * lg_f),
            dk_b=jnp.exp(row_i * lg_b),
            dc_f=jnp.exp(CHUNK * lg_f),
            dc_b=jnp.exp(CHUNK * lg_b),
        ))
        kc = kc_ref[0, :, cols].astype(F32)
        vc = vc_ref[0, :, cols]
        s_f0.append(_dot_tn((kc * jnp.exp((Lc - 1.0 - crow) * lg_f)).astype(BF16), vc))
        s_b0.append(_dot_tn((kc * jnp.exp(crow * lg_b)).astype(BF16), vc))

    def fwd(c, s_f):
        base = pl.multiple_of(c * CHUNK, CHUNK)
        new = []
        for h, hd in enumerate(heads):
            sf_scr[c, h] = s_f[h]
            kk = k_ref[0, pl.ds(base, CHUNK), hd["cols"]].astype(F32)
            vv = v_ref[0, pl.ds(base, CHUNK), hd["cols"]]
            new.append(hd["dc_f"] * s_f[h] + _dot_tn((kk * hd["dk_f"]).astype(BF16), vv))
        return tuple(new)

    lax.fori_loop(0, nch, fwd, tuple(s_f0), unroll=RET_UNROLL)

    def bwd(i, s_bs):
        c = nch - 1 - i
        base = pl.multiple_of(c * CHUNK, CHUNK)
        new = []
        for h, hd in enumerate(heads):
            qq = q_ref[0, pl.ds(base, CHUNK), hd["cols"]]
            kk = k_ref[0, pl.ds(base, CHUNK), hd["cols"]]
            vv = v_ref[0, pl.ds(base, CHUNK), hd["cols"]]
            s_b = s_bs[h]
            scores = (_dot_nt(qq, kk) * hd["dmat"]).astype(BF16)
            y = (_dot(scores, vv)
                 + _dot(qq, sf_scr[c, h].astype(BF16)) * hd["dq_f"]
                 + _dot(qq, s_b.astype(BF16)) * hd["dq_b"])
            mu = jnp.mean(y, axis=-1, keepdims=True)
            yc = y - mu
            var = jnp.mean(yc * yc, axis=-1, keepdims=True)
            yn = (yc * lax.rsqrt(var + EPS)) * gn_ref[:, hd["cols"]]
            gate = _silu(g_ref[0, pl.ds(base, CHUNK), hd["cols"]].astype(F32))
            y_ref[0, pl.ds(base, CHUNK), hd["cols"]] = (yn * gate).astype(BF16)
            new.append(hd["dc_b"] * s_b + _dot_tn((kk.astype(F32) * hd["dk_b"]).astype(BF16), vv))
        return tuple(new)

    lax.fori_loop(0, nch, bwd, tuple(s_b0), unroll=RET_UNROLL)


def _retention(q, k, v, g, kc, vc, decay_f, decay_b, gn_w):
    b, L, w = q.shape
    Lc = kc.shape[1]
    nch = L // CHUNK
    per_b = lambda n: pl.BlockSpec((1, n, w), lambda i: (i, 0, 0))
    return pl.pallas_call(
        _ret_kernel,
        grid=(b,),
        in_specs=[per_b(L), per_b(L), per_b(L), per_b(L), per_b(Lc), per_b(Lc),
                  _const_spec((1, w)), _const_spec((1, w)), _const_spec((1, w))],
        out_specs=per_b(L),
        out_shape=jax.ShapeDtypeStruct((b, L, w), BF16),
        scratch_shapes=[
            pltpu.VMEM((nch, RET_HEADS, RET_DK, RET_DK), F32),
        ],
        compiler_params=pltpu.CompilerParams(vmem_limit_bytes=48 * 1024 * 1024),
        name="retention",
    )(q, k, v, g, kc, vc, decay_f, decay_b, gn_w)


def _outproj_router_kernel(yr_ref, ys_ref, x_ref, g1_ref, sh2_ref, sc2_ref, npost_ref, npre_ref,
                           wor_ref, wos_ref, wr_ref, br_ref, tri_ref,
                           x1_ref, h2_ref, route_ref, cnt_ref, slots_ref,
                           wcat, carry):
    i = pl.program_id(0)

    @pl.when(i == 0)
    def _():
        wr = wr_ref[...]
        hi = wr.astype(BF16)
        wcat[:, :LANES] = hi
        wcat[:, LANES:] = (wr - hi.astype(F32)).astype(BF16)
        carry[...] = jnp.zeros_like(carry)

    tm = tri_ref.shape[0]
    running = carry[...]
    for sub in range(x_ref.shape[0] // tm):
        running = _route_subtile(sub, tm, running, yr_ref, ys_ref, x_ref, g1_ref, sh2_ref, sc2_ref, npost_ref,
                                 npre_ref, wor_ref, wos_ref, br_ref, tri_ref, x1_ref, h2_ref, route_ref,
                                 slots_ref, wcat)
    carry[...] = running
    cnt_ref[...] = running


def _route_subtile(sub, tm, running, yr_ref, ys_ref, x_ref, g1_ref, sh2_ref, sc2_ref, npost_ref, npre_ref,
                   wor_ref, wos_ref, br_ref, tri_ref, x1_ref, h2_ref, route_ref, slots_ref, wcat):
    rows = slice(sub * tm, (sub + 1) * tm)
    y = _dot(yr_ref[rows, :], wor_ref[...]) + _dot(ys_ref[rows, :], wos_ref[...])
    ms = jnp.mean(y * y, axis=-1, keepdims=True)
    x1 = x_ref[rows, :] + (y * lax.rsqrt(ms + EPS)) * (g1_ref[0] * npost_ref[...])
    x1_ref[rows, :] = x1
    h2 = _norm_mod(x1, npre_ref[...], sc2_ref[0], sh2_ref[0])
    h2_ref[rows, :] = h2

    h_hi = h2.astype(BF16)
    h_lo = (h2 - h_hi.astype(F32)).astype(BF16)
    both = _dot(h_hi, wcat[...])
    lg = both[:, :LANES] + both[:, LANES:] + _dot(h_lo, wcat[:, :LANES]) + br_ref[...]

    lane = lax.broadcasted_iota(jnp.int32, (tm, LANES), 1)
    lane_f = lane.astype(F32)
    is_grp = (lane >= N_EXPERTS) & (lane < N_EXPERTS + MOE_GROUPS)
    gl = jnp.where(is_grp, lg, NEG_BIG)
    mg = jnp.max(gl, axis=-1, keepdims=True)
    grp_lane = jnp.min(jnp.where(gl == mg, lane_f, 1e9), axis=-1, keepdims=True)
    p_g = 1.0 / jnp.sum(jnp.where(is_grp, jnp.exp(gl - mg), 0.0), axis=-1, keepdims=True)
    first = (grp_lane - N_EXPERTS) * EXPERTS_PER_GROUP
    in_grp = (lane_f >= first) & (lane_f < first + EXPERTS_PER_GROUP)
    el = jnp.where(in_grp, lg, NEG_BIG)
    t1 = jnp.max(el, axis=-1, keepdims=True)
    i1 = jnp.min(jnp.where(el == t1, lane_f, 1e9), axis=-1, keepdims=True)
    el2 = jnp.where(lane_f == i1, NEG_BIG, el)
    t2 = jnp.max(el2, axis=-1, keepdims=True)
    i2 = jnp.min(jnp.where(el2 == t2, lane_f, 1e9), axis=-1, keepdims=True)
    s = jnp.exp(t2 - t1)
    w1 = p_g / (1.0 + s)
    w2 = p_g * s / (1.0 + s)

    oh1 = (lane_f == i1)
    oh2 = (lane_f == i2)
    oh = (oh1 | oh2).astype(BF16)
    before = _dot(tri_ref[...], oh) + running
    rank1 = jnp.sum(jnp.where(oh1, before, 0.0), axis=-1, keepdims=True)
    rank2 = jnp.sum(jnp.where(oh2, before, 0.0), axis=-1, keepdims=True)

    route_ref[rows, :] = jnp.where(lane == 0, w1, jnp.where(lane == 1, w2, 0.0))

    row = lax.broadcasted_iota(jnp.int32, (tm, LANES), 0)
    on_diag = (row % LANES) == lane
    per = tm // LANES
    n_sub = x_ref.shape[0] // tm
    for qi, col in enumerate((i1, i2, rank1, rank2)):
        picked = jnp.where(on_diag, col, 0.0)
        dense = jnp.sum(picked.reshape(per, LANES, LANES), axis=1).astype(jnp.int32)
        slots_ref[0, (qi * n_sub + sub) * per:(qi * n_sub + sub + 1) * per, :] = dense
    return running + jnp.sum(oh.astype(F32), axis=0, keepdims=True)


def _outproj_router(yr, ys, x2, mod3, npost, npre, wo_r, wo_s, w_router, b_router, seq_len):
    T, d = x2.shape
    tm = TM_OUT
    per_seq = seq_len // tm
    rw = yr.shape[1]
    sub = TM_ROUTE_SUB
    tri = (jnp.arange(sub)[:, None] > jnp.arange(sub)[None, :]).astype(BF16)
    tok = lambda w: pl.BlockSpec((tm, w), lambda i: (i, 0))
    modv = lambda k: pl.BlockSpec((1, 1, d), lambda i: (i // per_seq, 0, k))
    return pl.pallas_call(
        _outproj_router_kernel,
        grid=(T // tm,),
        in_specs=[
            tok(rw), tok(rw), tok(d), modv(2), modv(3), modv(4),
            _const_spec((1, d)), _const_spec((1, d)),
            _const_spec(wo_r.shape), _const_spec(wo_s.shape), _const_spec(w_router.shape), _const_spec((1, LANES)),
            _const_spec((sub, sub)),
        ],
        out_specs=[tok(d), tok(d), tok(LANES), _const_spec((1, LANES)),
                   pl.BlockSpec((1, 2 * TOP_K * (tm // LANES), LANES), lambda i: (i, 0, 0))],
        out_shape=[jax.ShapeDtypeStruct((T, d), F32), jax.ShapeDtypeStruct((T, d), F32),
                   jax.ShapeDtypeStruct((T, LANES), F32), jax.ShapeDtypeStruct((1, LANES), F32),
                   jax.ShapeDtypeStruct((T // tm, 2 * TOP_K * (tm // LANES), LANES), jnp.int32)],
        scratch_shapes=[pltpu.VMEM((d, 2 * LANES), BF16), pltpu.VMEM((1, LANES), F32)],
        compiler_params=pltpu.CompilerParams(dimension_semantics=("arbitrary",),
                                             vmem_limit_bytes=48 * 1024 * 1024),
        name="outproj_router",
    )(yr, ys, x2, mod3, mod3, mod3, npost, npre, wo_r, wo_s, w_router, b_router, tri)


def _slot_rows_kernel(start_ref, slots_ref, o_ref):
    half = slots_ref.shape[1] // 2
    expert = slots_ref[:, :half, :]
    row = slots_ref[:, half:, :]
    for e in range(N_EXPERTS):
        row = row + jnp.where(expert == e, start_ref[e], 0)
    for r in range(half):
        o_ref[:, 0, r * LANES:(r + 1) * LANES] = row[:, r, :]


def _slot_rows(pad_start, slots):
    nt, rows, _ = slots.shape
    grid_spec = pltpu.PrefetchScalarGridSpec(
        num_scalar_prefetch=1,
        grid=(1,),
        in_specs=[pl.BlockSpec(slots.shape, lambda i, ps: (0, 0, 0))],
        out_specs=pl.BlockSpec((nt, 1, rows // 2 * LANES), lambda i, ps: (0, 0, 0)),
    )
    return pl.pallas_call(
        _slot_rows_kernel,
        grid_spec=grid_spec,
        out_shape=jax.ShapeDtypeStruct((nt, 1, rows // 2 * LANES), jnp.int32),
        name="slot_rows",
    )(pad_start, slots)


def _dispatch_kernel(pad_end_ref, zero_from_ref, dest_ref, h_hbm, xs_hbm, zbuf, hbuf, sem, zsem, in_sem):
    i = pl.program_id(0)
    nt = pl.num_programs(0)
    td = hbuf.shape[1]
    mb = zbuf.shape[0]

    @pl.when(i == 0)
    def _():
        zbuf[...] = jnp.zeros_like(zbuf)

        def pieces(e, act):
            for p in range(mb // ZERO_PIECE):
                row = zero_from_ref[e] + p * ZERO_PIECE

                @pl.when(row < pad_end_ref[e])
                def _():
                    act(pltpu.make_async_copy(zbuf.at[pl.ds(0, ZERO_PIECE)],
                                              xs_hbm.at[pl.ds(pl.multiple_of(row, ZERO_PIECE), ZERO_PIECE)], zsem))

        def start(e, carry):
            pieces(e, lambda c: c.start())
            return carry

        def wait(e, carry):
            pieces(e, lambda c: c.wait())
            return carry

        def tail_fill(blk):
            return pltpu.make_async_copy(zbuf, xs_hbm.at[pl.ds(pl.multiple_of(blk * mb, mb), mb)], zsem)

        def tail_start(blk, carry):
            @pl.when(blk * mb >= pad_end_ref[N_EXPERTS - 1])
            def _():
                tail_fill(blk).start()
            return carry

        def tail_wait(blk, carry):
            @pl.when(blk * mb >= pad_end_ref[N_EXPERTS - 1])
            def _():
                tail_fill(blk).wait()
            return carry

        n_blocks = xs_hbm.shape[0] // mb
        lax.fori_loop(0, N_EXPERTS, start, 0)
        lax.fori_loop(0, n_blocks, tail_start, 0)
        lax.fori_loop(0, N_EXPERTS, wait, 0)
        lax.fori_loop(0, n_blocks, tail_wait, 0)

    def fetch(t):
        return pltpu.make_async_copy(h_hbm.at[pl.ds(pl.multiple_of(t * td, td), td)], hbuf.at[t % DISPATCH_RING],
                                     in_sem.at[t % DISPATCH_RING])

    @pl.when(i == 0)
    def _():
        fetch(i).start()

    @pl.when(i + 1 < nt)
    def _():
        fetch(i + 1).start()

    fetch(i).wait()
    for slot in range(DISPATCH_RING):
        @pl.when(i % DISPATCH_RING == slot)
        def _(slot=slot):
            _issue_row_copies(dest_ref, td, lambda kk, j, d: pltpu.make_async_copy(
                hbuf.at[slot, pl.ds(j, 1)], xs_hbm.at[pl.ds(d, 1)], sem.at[slot]))

    def drain(t):
        for kk in range(TOP_K):
            pltpu.make_async_copy(hbuf.at[t % DISPATCH_RING], xs_hbm.at[pl.ds(0, td)],
                                  sem.at[t % DISPATCH_RING]).wait()

    @pl.when(i > 0)
    def _():
        drain(i - 1)

    @pl.when(i == nt - 1)
    def _():
        drain(i)


def _issue_row_copies(dest_ref, n_tok, row_copy):
    def issue(j, carry):
        for kk in range(TOP_K):
            row_copy(kk, j, dest_ref[0, 0, kk * n_tok + j]).start()
        return carry

    lax.fori_loop(0, n_tok, issue, 0, unroll=DMA_UNROLL)


def _dispatch(pad_end, zero_from, dest, h2, cap):
    T, d = h2.shape
    td = TD_DISPATCH
    nt = T // td
    grid_spec = pltpu.PrefetchScalarGridSpec(
        num_scalar_prefetch=2,
        grid=(nt,),
        in_specs=[
            pl.BlockSpec((1, 1, TOP_K * td), lambda i, pe, zf: (i, 0, 0), memory_space=pltpu.SMEM),
            pl.BlockSpec(memory_space=pl.ANY),
        ],
        out_specs=pl.BlockSpec(memory_space=pl.ANY),
        scratch_shapes=[pltpu.VMEM((MB_EXPERT, d), h2.dtype), pltpu.VMEM((DISPATCH_RING, td, d), h2.dtype),
                        pltpu.SemaphoreType.DMA((DISPATCH_RING,)), pltpu.SemaphoreType.DMA(()),
                        pltpu.SemaphoreType.DMA((DISPATCH_RING,))],
    )
    return pl.pallas_call(
        _dispatch_kernel,
        grid_spec=grid_spec,
        out_shape=jax.ShapeDtypeStruct((cap, d), h2.dtype),
        compiler_params=pltpu.CompilerParams(dimension_semantics=("arbitrary",)),
        name="dispatch",
    )(pad_end, zero_from, dest, h2)


def _expert_kernel(be_ref, first_ref, slot_ref, next_ref, nused_ref, xs_ref, wg_hbm, wu_hbm, wd_hbm, y_ref,
                   wg_f, wu_f, wd_f, wg_b, wu_b, wd_b, sem):
    i = pl.program_id(0)

    def fetch(e, s):
        return [pltpu.make_async_copy(src.at[e], dst.at[s], sem.at[s, k])
                for k, (src, dst) in enumerate(((wg_hbm, wg_f), (wu_hbm, wu_f), (wd_hbm, wd_f)))]

    @pl.when(i == 0)
    def _():
        for c in fetch(be_ref[0], slot_ref[0]):
            c.start()

    @pl.when(first_ref[i] == 1)
    def _():
        s = slot_ref[i]

        @pl.when(next_ref[i] >= 0)
        def _():
            for c in fetch(next_ref[i], 1 - s):
                c.start()

        for c in fetch(be_ref[i], s):
            c.wait()
        wg_b[...] = wg_f[s].astype(BF16)
        wu_b[...] = wu_f[s].astype(BF16)
        wd_b[...] = wd_f[s].astype(BF16)

    @pl.when(i < nused_ref[0])
    def _():
        xb = xs_ref[...].astype(BF16)
        hid = (_silu(_dot(xb, wg_b[...])) * _dot(xb, wu_b[...])).astype(BF16)
        y_ref[...] = _dot(hid, wd_b[...])

    @pl.when(i >= nused_ref[0])
    def _():
        y_ref[...] = jnp.zeros_like(y_ref)


def _expert_plan(padded, pad_end, n_blocks, mb):
    n_used = (pad_end[-1:] // mb).astype(jnp.int32)
    blk_start = jnp.arange(n_blocks, dtype=jnp.int32) * mb
    experts = jnp.arange(N_EXPERTS, dtype=jnp.int32)
    blk_expert = jnp.minimum(jnp.sum((pad_end[None, :] <= blk_start[:, None]).astype(jnp.int32), axis=1),
                             N_EXPERTS - 1)
    prev = jnp.concatenate([jnp.full((1,), -1, jnp.int32), blk_expert[:-1]])
    first = ((blk_start < pad_end[-1]) & (blk_expert != prev)).astype(jnp.int32)
    slot = (jnp.cumsum(first) - 1) % 2
    later = jnp.where((padded > 0)[None, :] & (experts[None, :] > experts[:, None]), experts[None, :], N_EXPERTS)
    next_nonempty = jnp.min(later, axis=1)
    next_nonempty = jnp.where(next_nonempty == N_EXPERTS, -1, next_nonempty)
    nxt = jnp.sum(jnp.where(blk_expert[:, None] == experts[None, :], next_nonempty[None, :], 0), axis=1)
    return [a.astype(jnp.int32) for a in (blk_expert, first, slot, nxt, n_used)]


def _experts(plan, xs, w_gate, w_up, w_down):
    cap, dp = xs.shape
    n_exp, d, de = w_gate.shape
    assert dp == d
    mb = MB_EXPERT
    grid_spec = pltpu.PrefetchScalarGridSpec(
        num_scalar_prefetch=len(plan),
        grid=(cap // mb,),
        in_specs=[
            pl.BlockSpec((mb, dp), lambda i, be, fi, sl, nx, nu: (jnp.minimum(i, nu[0] - 1), 0)),
            pl.BlockSpec(memory_space=pl.ANY), pl.BlockSpec(memory_space=pl.ANY), pl.BlockSpec(memory_space=pl.ANY),
        ],
        out_specs=pl.BlockSpec((mb, dp), lambda i, be, fi, sl, nx, nu: (i, 0)),
        scratch_shapes=[pltpu.VMEM((2, d, de), F32), pltpu.VMEM((2, d, de), F32), pltpu.VMEM((2, de, d), F32),
                        pltpu.VMEM((d, de), BF16), pltpu.VMEM((d, de), BF16), pltpu.VMEM((de, d), BF16),
                        pltpu.SemaphoreType.DMA((2, 3))],
    )
    return pl.pallas_call(
        _expert_kernel,
        grid_spec=grid_spec,
        out_shape=jax.ShapeDtypeStruct((cap, dp), xs.dtype),
        compiler_params=pltpu.CompilerParams(dimension_semantics=("arbitrary",),
                                             vmem_limit_bytes=48 * 1024 * 1024),
        name="experts",
    )(*plan, xs, w_gate, w_up, w_down)


def _combine_kernel(dest_ref, dest_next_ref, route_ref, x1_ref, g2_ref, nw_ref, yb_hbm, o_ref, buf, sem):
    i = pl.program_id(0)
    nt = pl.num_programs(0)
    tf = x1_ref.shape[0]
    stage = i % 2

    def gather(refs, st):
        _issue_row_copies(refs, tf, lambda kk, j, d: pltpu.make_async_copy(
            yb_hbm.at[pl.ds(d, 1)], buf.at[st, kk, pl.ds(j, 1)], sem.at[st]))

    @pl.when(i == 0)
    def _():
        gather(dest_ref, 0)

    for st in range(2):
        @pl.when((i + 1 < nt) & (stage != st))
        def _(st=st):
            gather(dest_next_ref, st)

    for kk in range(TOP_K):
        pltpu.make_async_copy(yb_hbm.at[pl.ds(0, tf)], buf.at[stage, kk], sem.at[stage]).wait()

    route = route_ref[...]
    out = route[:, 0:1] * buf[stage, 0] + route[:, 1:2] * buf[stage, 1]
    ms = jnp.mean(out * out, axis=-1, keepdims=True)
    o_ref[...] = x1_ref[...] + g2_ref[0] * ((out * lax.rsqrt(ms + EPS)) * nw_ref[...])


def _combine(dest, route, x1, mod3, nw, yb, seq_len):
    T, d = x1.shape
    tf = TF_COMBINE
    nt = T // tf
    per_seq = seq_len // tf
    slot_spec = functools.partial(pl.BlockSpec, (1, 1, TOP_K * tf), memory_space=pltpu.SMEM)
    return pl.pallas_call(
        _combine_kernel,
        grid=(nt,),
        in_specs=[
            slot_spec(lambda i: (i, 0, 0)),
            slot_spec(lambda i: (jnp.minimum(i + 1, nt - 1), 0, 0)),
            pl.BlockSpec((tf, LANES), lambda i: (i, 0)),
            pl.BlockSpec((tf, d), lambda i: (i, 0)),
            pl.BlockSpec((1, 1, d), lambda i: (i // per_seq, 0, 5)),
            _const_spec((1, d)),
            pl.BlockSpec(memory_space=pl.ANY),
        ],
        out_specs=pl.BlockSpec((tf, d), lambda i: (i, 0)),
        out_shape=jax.ShapeDtypeStruct((T, d), F32),
        scratch_shapes=[pltpu.VMEM((2, TOP_K, tf, yb.shape[1]), yb.dtype), pltpu.SemaphoreType.DMA((2,))],
        compiler_params=pltpu.CompilerParams(dimension_semantics=("arbitrary",),
                                             vmem_limit_bytes=48 * 1024 * 1024),
        name="combine",
    )(dest, dest, route, x1, mod3, nw, yb)


def _rope_tables(L, n_heads):
    quarter = RET_DK // 4
    freqs = ROPE_BASE ** (-jnp.arange(quarter, dtype=F32) / quarter)
    t = jnp.arange(L)
    ang_r = (t // GRID_W).astype(F32)[:, None] * freqs
    ang_c = (t % GRID_W).astype(F32)[:, None] * freqs
    cos = jnp.concatenate([jnp.cos(ang_r)] * 2 + [jnp.cos(ang_c)] * 2, axis=-1)
    sin = jnp.concatenate([-jnp.sin(ang_r), jnp.sin(ang_r), -jnp.sin(ang_c), jnp.sin(ang_c)], axis=-1)
    return jnp.tile(cos, (1, n_heads)), jnp.tile(sin, (1, n_heads))


def _lane_pad(v, width=LANES):
    return jnp.pad(v, [(0, 0)] * (v.ndim - 1) + [(0, width - v.shape[-1])])


def kernel(x, c, ctx, c_ctx, w_mod, b_mod, norm_pre_mix, norm_post_mix, norm_pre_ffn, norm_post_ffn, w_in, w_out, ret_decay_f, ret_decay_b, ret_gn_w, ssd_conv_w, ssd_conv_b, ssd_dt_bias_f, ssd_dt_bias_b, ssd_a_log_f, ssd_a_log_b, ssd_d, ssd_norm_w, moe_w_rg, moe_b_rg, moe_w_re, moe_b_re, moe_w_gate, moe_w_up, moe_w_down):
    b, L, d = x.shape
    assert w_mod.shape[0] == 1, "single layer: context outputs are never needed"
    assert TM_OUT == TD_DISPATCH == TF_COMBINE, "router, dispatch and combine share one slot-row layout"
    rw = RET_HEADS * RET_DK
    nconv = SSD_WIDTH + 2 * SSD_GROUPS * SSD_STATE
    T = b * L

    mod_rows = -(-(b + 1) // SUBLANES) * SUBLANES
    c_all = jnp.zeros((mod_rows, d), F32).at[:b].set(c).at[b].set(c_ctx)
    mod3 = _modulation(c_all, w_mod[0], b_mod[0]).reshape(mod_rows, 1, 6 * d)

    wi = w_in[0]
    o = 0
    wq = wi[:, o:o + rw]; o += rw
    wk = wi[:, o:o + rw]; o += rw
    wv = wi[:, o:o + rw]; o += rw
    wg = wi[:, o:o + rw]; o += rw
    wz = wi[:, o:o + SSD_WIDTH]; o += SSD_WIDTH
    wxbc = wi[:, o:o + nconv].astype(BF16); o += nconv
    wdt = _lane_pad(wi[:, o:o + 2 * SSD_HEADS]).astype(BF16)
    wqk = jnp.concatenate([wq, wk], axis=1).astype(BF16)
    wvgz = jnp.concatenate([wv, wg, wz], axis=1).astype(BF16)
    cos_t, sin_t = _rope_tables(L, RET_HEADS)
    nw1 = norm_pre_mix[0].reshape(1, d)

    q, k, v, g, z, xbc, dt = _inproj(x, mod3, nw1, wqk, wvgz, wxbc, wdt, cos_t, sin_t)
    kc, vc, xbcc, dtc = _inproj_ctx(ctx, mod3, b, nw1, wk.astype(BF16), wv.astype(BF16), wxbc, wdt)

    conv_w8 = jnp.pad(ssd_conv_w[0], ((0, SUBLANES - SSD_CONV), (0, 0)))
    dt_bias = _lane_pad(jnp.concatenate([ssd_dt_bias_f[0], ssd_dt_bias_b[0]])[None, :])
    a_log = _lane_pad(jnp.concatenate([ssd_a_log_f[0], ssd_a_log_b[0]])[None, :])
    d_skip = jnp.repeat(ssd_d[0], SSD_HEADDIM)[None, :]
    ys = _ssd(xbc, z, dt, xbcc, dtc, conv_w8, ssd_conv_b[0][None, :], dt_bias, a_log, d_skip,
              ssd_norm_w[0][None, :])

    yr = _retention(q, k, v, g, kc, vc,
                    jnp.repeat(ret_decay_f[0], RET_DK)[None, :], jnp.repeat(ret_decay_b[0], RET_DK)[None, :],
                    ret_gn_w[0][None, :])

    wo = w_out[0].astype(BF16)
    w_router = _lane_pad(jnp.concatenate(
        [jnp.transpose(moe_w_re[0], (1, 0, 2)).reshape(d, N_EXPERTS), moe_w_rg[0]], axis=1))
    b_router = _lane_pad(jnp.concatenate([moe_b_re[0].reshape(-1), moe_b_rg[0]])[None, :])
    x1, h2, route, counts, slots = _outproj_router(
        yr.reshape(T, rw), ys.reshape(T, SSD_WIDTH), x.reshape(T, d), mod3,
        norm_post_mix[0][None, :], norm_pre_ffn[0][None, :], wo[:rw], wo[rw:], w_router, b_router, L)

    mb = MB_EXPERT
    n_blocks = -(-(T * TOP_K + N_EXPERTS * (mb - 1)) // mb)
    cap = n_blocks * mb
    cnt = counts[0, :N_EXPERTS].astype(jnp.int32)
    padded = (cnt + mb - 1) // mb * mb
    pad_end = jnp.cumsum(padded)
    pad_start = pad_end - padded

    dest = _slot_rows(pad_start.astype(jnp.int32), slots)
    zero_from = pad_start + cnt // ZERO_PIECE * ZERO_PIECE
    xs = _dispatch(pad_end.astype(jnp.int32), zero_from.astype(jnp.int32), dest, h2, cap)
    yb = _experts(_expert_plan(padded, pad_end, n_blocks, mb), xs, moe_w_gate[0], moe_w_up[0], moe_w_down[0])
    out = _combine(dest, route, x1, mod3, norm_post_ffn[0][None, :], yb, L)
    return out.reshape(b, L, d)
```

```python
import functools
import math

import jax
import jax.numpy as jnp
from jax import lax
from jax.experimental import pallas as pl
from jax.experimental.pallas import tpu as pltpu

F32 = jnp.float32
BF16 = jnp.bfloat16

LANES = 128
SUBLANES = 8
V7X_VMEM_BYTES = 64 * 1024 * 1024
VMEM_LIMIT = V7X_VMEM_BYTES * 3 // 4
VMEM_LIMIT_SSD = V7X_VMEM_BYTES * 7 // 8

EPS = 1e-6
CHUNK = 128
GRID_W = 64
RET_HEADS = 4
RET_DK = 128
ROPE_BASE = 10000.0
SSD_HEADS = 8
SSD_HEADDIM = 64
SSD_GROUPS = 2
SSD_STATE = 128
SSD_WIDTH = SSD_HEADS * SSD_HEADDIM
SSD_CONV = 5
SSD_PAIRS = SSD_WIDTH // LANES
MOE_GROUPS = 4
EXPERTS_PER_GROUP = 8
N_EXPERTS = MOE_GROUPS * EXPERTS_PER_GROUP
TOP_K = 2
CONV_HALO = SUBLANES

TM_PROJ = 1024
TM_OUT = 1024
TM_ROUTE_SUB = TM_OUT
TD_DISPATCH = TM_OUT
MB_EXPERT = 512
ZERO_PIECE = 64
TF_COMBINE = TM_OUT
DMA_UNROLL = 8
DISPATCH_RING = 3
SSD_UNROLL = 1
RET_UNROLL = 2
NEG_BIG = -1e30


def _silu(v):
    return v * jax.nn.sigmoid(v)


def _dot(a, b):
    return jnp.dot(a, b, preferred_element_type=F32)


def _dot_tn(a, b):
    return lax.dot_general(a, b, (((0,), (0,)), ((), ())), preferred_element_type=F32)


def _dot_nt(a, b):
    return lax.dot_general(a, b, (((1,), (1,)), ((), ())), preferred_element_type=F32)


def _mod_kernel(c_ref, w_ref, b_ref, o_ref):
    a = _silu(c_ref[...])
    w = w_ref[...]
    a_hi = a.astype(BF16)
    a_lo = (a - a_hi.astype(F32)).astype(BF16)
    w_hi = w.astype(BF16)
    w_lo = (w - w_hi.astype(F32)).astype(BF16)
    o_ref[...] = _dot(a_hi, w_hi) + _dot(a_lo, w_hi) + _dot(a_hi, w_lo) + b_ref[...]


def _modulation(c_all, w_mod, b_mod):
    rows, d = c_all.shape
    n = w_mod.shape[1]
    return pl.pallas_call(
        _mod_kernel,
        grid=(n // d,),
        in_specs=[
            pl.BlockSpec((rows, d), lambda j: (0, 0)),
            pl.BlockSpec((d, d), lambda j: (0, j)),
            pl.BlockSpec((1, d), lambda j: (0, j)),
        ],
        out_specs=pl.BlockSpec((rows, d), lambda j: (0, j)),
        out_shape=jax.ShapeDtypeStruct((rows, n), F32),
        name="modulation",
    )(c_all, w_mod, b_mod.reshape(1, n))


def _norm_mod(x, nw, sc, sh):
    ms = jnp.mean(x * x, axis=-1, keepdims=True)
    return (x * lax.rsqrt(ms + EPS)) * (nw * (1.0 + sc)) + sh


def _rope(t, cos, sin_signed, first_half):
    width = t.shape[-1]
    quarter = RET_DK // 4
    swapped = jnp.where(first_half, pltpu.roll(t, width - quarter, 1), pltpu.roll(t, quarter, 1))
    return t * cos + swapped * sin_signed


def _inproj_kernel(x_ref, sh_ref, sc_ref, nw_ref, wqk_ref, wvgz_ref, wxbc_ref, wdt_ref, cos_ref, sin_ref,
                   q_ref, k_ref, v_ref, g_ref, z_ref, xbc_ref, dt_ref):
    hb = _norm_mod(x_ref[0], nw_ref[...], sc_ref[0], sh_ref[0]).astype(BF16)
    rw = q_ref.shape[-1]
    qk = _dot(hb, wqk_ref[...])
    cos = cos_ref[...]
    sin = sin_ref[...]
    lane = lax.broadcasted_iota(jnp.int32, cos.shape, 1)
    first_half = (lane % (RET_DK // 2)) < (RET_DK // 4)
    q_ref[0] = _rope(qk[:, :rw], cos, sin, first_half).astype(BF16)
    k_ref[0] = (_rope(qk[:, rw:], cos, sin, first_half) * (RET_DK ** -0.5)).astype(BF16)
    vgz = _dot(hb, wvgz_ref[...])
    v_ref[0] = vgz[:, :rw].astype(BF16)
    g_ref[0] = vgz[:, rw:2 * rw].astype(BF16)
    z_ref[0] = vgz[:, 2 * rw:].astype(BF16)
    xbc_ref[0] = _dot(hb, wxbc_ref[...]).astype(BF16)
    dt_ref[0] = _dot(hb, wdt_ref[...])


def _inproj_ctx_kernel(x_ref, sh_ref, sc_ref, nw_ref, wk_ref, wv_ref, wxbc_ref, wdt_ref,
                       k_ref, v_ref, xbc_ref, dt_ref):
    hb = _norm_mod(x_ref[0], nw_ref[...], sc_ref[0], sh_ref[0]).astype(BF16)
    k_ref[0] = (_dot(hb, wk_ref[...]) * (RET_DK ** -0.5)).astype(BF16)
    v_ref[0] = _dot(hb, wv_ref[...]).astype(BF16)
    xbc_ref[0] = _dot(hb, wxbc_ref[...]).astype(BF16)
    dt_ref[0] = _dot(hb, wdt_ref[...])


def _const_spec(shape):
    nd = len(shape)
    return pl.BlockSpec(shape, lambda *_: (0,) * nd)


def _inproj(x, mod3, nw, wqk, wvgz, wxbc, wdt, cos_t, sin_t):
    b, L, d = x.shape
    tm = min(TM_PROJ, L)
    rw = wqk.shape[1] // 2
    tok = lambda w: pl.BlockSpec((1, tm, w), lambda i, j: (i, j, 0))
    out_bf = lambda w: jax.ShapeDtypeStruct((b, L, w), BF16)
    return pl.pallas_call(
        _inproj_kernel,
        grid=(b, L // tm),
        in_specs=[
            tok(d),
            pl.BlockSpec((1, 1, d), lambda i, j: (i, 0, 0)),
            pl.BlockSpec((1, 1, d), lambda i, j: (i, 0, 1)),
            _const_spec((1, d)),
            _const_spec(wqk.shape), _const_spec(wvgz.shape), _const_spec(wxbc.shape), _const_spec(wdt.shape),
            pl.BlockSpec((tm, rw), lambda i, j: (j, 0)),
            pl.BlockSpec((tm, rw), lambda i, j: (j, 0)),
        ],
        out_specs=[tok(rw), tok(rw), tok(rw), tok(rw), tok(rw), tok(wxbc.shape[1]), tok(LANES)],
        out_shape=[out_bf(rw), out_bf(rw), out_bf(rw), out_bf(rw), out_bf(rw), out_bf(wxbc.shape[1]),
                   jax.ShapeDtypeStruct((b, L, LANES), F32)],
        compiler_params=pltpu.CompilerParams(vmem_limit_bytes=VMEM_LIMIT),
        name="inproj",
    )(x, mod3, mod3, nw, wqk, wvgz, wxbc, wdt, cos_t, sin_t)


def _inproj_ctx(ctx, mod3, ctx_row, nw, wk, wv, wxbc, wdt):
    b, L, d = ctx.shape
    tm = min(TM_PROJ, L)
    rw = wk.shape[1]
    tok = lambda w: pl.BlockSpec((1, tm, w), lambda i, j: (i, j, 0))
    out_bf = lambda w: jax.ShapeDtypeStruct((b, L, w), BF16)
    return pl.pallas_call(
        _inproj_ctx_kernel,
        grid=(b, L // tm),
        in_specs=[
            tok(d),
            pl.BlockSpec((1, 1, d), lambda i, j: (ctx_row, 0, 0)),
            pl.BlockSpec((1, 1, d), lambda i, j: (ctx_row, 0, 1)),
            _const_spec((1, d)),
            _const_spec(wk.shape), _const_spec(wv.shape), _const_spec(wxbc.shape), _const_spec(wdt.shape),
        ],
        out_specs=[tok(rw), tok(rw), tok(wxbc.shape[1]), tok(LANES)],
        out_shape=[out_bf(rw), out_bf(rw), out_bf(wxbc.shape[1]), jax.ShapeDtypeStruct((b, L, LANES), F32)],
        compiler_params=pltpu.CompilerParams(vmem_limit_bytes=VMEM_LIMIT),
        name="inproj_ctx",
    )(ctx, mod3, mod3, nw, wk, wv, wxbc, wdt)


def _ssd_kernel(xbc_ref, z_ref, dt_ref, xbcc_ref, dtc_ref, cw_ref, cb_ref, dtb_ref, alog_ref, dsk_ref, nw_ref,
                y_ref,
                xpad, xpadc, u, uc, dtv, dav, dtcv, dacv, sf_scr, kb_scr, acum, ecum, dec_scr,
                arow_scr, erow_scr, dtrow_scr):
    L = xbc_ref.shape[1]
    Lc = xbcc_ref.shape[1]
    nch = L // CHUNK
    nchc = Lc // CHUNK
    win = CHUNK + 2 * CONV_HALO
    nconv = xbc_ref.shape[2]
    nh = SSD_HEADS

    def conv_pass(src_ref, pad_ref, dst_ref, n_chunks, length):
        zeros = jnp.zeros((CONV_HALO, nconv), F32)
        pad_ref[0:CONV_HALO, :] = zeros
        pad_ref[CONV_HALO + length:2 * CONV_HALO + length, :] = zeros
        pad_ref[CONV_HALO:CONV_HALO + length, :] = src_ref[0].astype(F32)

        def chunk(c, carry):
            base = pl.multiple_of(c * CHUNK, CHUNK)
            for cb_i in range(nconv // LANES):
                cols = slice(cb_i * LANES, (cb_i + 1) * LANES)
                w = pad_ref[pl.ds(base, win), cols]
                acc = cb_ref[:, cols] + w[CONV_HALO:CONV_HALO + CHUNK] * cw_ref[SSD_CONV // 2:SSD_CONV // 2 + 1, cols]
                for j in range(SSD_CONV):
                    if j == SSD_CONV // 2:
                        continue
                    shifted = pltpu.roll(w, (SSD_CONV // 2 - j) % win, 0)
                    acc = acc + shifted[CONV_HALO:CONV_HALO + CHUNK] * cw_ref[j:j + 1, cols]
                dst_ref[pl.ds(base, CHUNK), cols] = _silu(acc).astype(BF16)
            return carry

        lax.fori_loop(0, n_chunks, chunk, 0)

    conv_pass(xbcc_ref, xpadc, uc, nchc, Lc)
    conv_pass(xbc_ref, xpad, u, nch, L)

    a_neg = -jnp.exp(alog_ref[...])
    dtv[...] = jax.nn.softplus(dt_ref[0] + dtb_ref[...])
    dav[...] = dtv[...] * a_neg
    dtcv[...] = jax.nn.softplus(dtc_ref[0] + dtb_ref[...])
    dacv[...] = dtcv[...] * a_neg

    row_i = lax.broadcasted_iota(jnp.int32, (CHUNK, CHUNK), 0)
    col_i = lax.broadcasted_iota(jnp.int32, (CHUNK, CHUNK), 1)
    causal = col_i <= row_i
    lo_half = col_i < SSD_HEADDIM
    fwd_lane = col_i < nh
    head_of = lax.broadcasted_iota(jnp.int32, (CHUNK, SSD_WIDTH), 1) // SSD_HEADDIM
    src_col = lax.broadcasted_iota(jnp.int32, (CHUNK, SSD_WIDTH), 0)
    exp_f = (head_of == src_col).astype(BF16)
    exp_b = (head_of == src_col - nh).astype(BF16)
    exp_fb = jnp.concatenate([exp_f, exp_b], axis=1)

    def split3(v):
        hi = v.astype(BF16)
        r1 = v - hi.astype(F32)
        mid = r1.astype(BF16)
        return hi, mid, (r1 - mid.astype(F32)).astype(BF16)

    def times_onehot(v, m, passes=3):
        parts = split3(v)[:passes]
        acc = _dot(parts[0], m)
        for part in parts[1:]:
            acc = acc + _dot(part, m)
        return acc

    def colb(mat, r):
        return jnp.broadcast_to(mat[:, r:r + 1], (CHUNK, CHUNK))

    def pair_sel(a, b_):
        return jnp.where(lo_half, a, b_)

    gw = 2 * LANES

    def chunk_terms(u_ref, dt_s, da_s, base):
        dt = dt_s[pl.ds(base, CHUNK), :]
        da = da_s[pl.ds(base, CHUNK), :]
        acol = da
        for step in (1, 2, 4, 8, 16, 32, 64):
            acol = acol + jnp.where(row_i >= step, pltpu.roll(acol, step, 0), 0.0)
        ecol = acol - da
        last = acol[CHUNK - 1:CHUNK, :]
        wgt = jnp.where(fwd_lane, jnp.exp(last - acol), jnp.exp(ecol)) * dt
        scale = jnp.where(fwd_lane, jnp.exp(acol), jnp.exp(last - ecol))
        wide = times_onehot(jnp.concatenate([wgt, scale], axis=0), exp_fb, passes=1)
        dec = times_onehot(jnp.broadcast_to(jnp.exp(last), (SUBLANES, LANES)), exp_fb)[0:1]
        xs = u_ref[pl.ds(base, CHUNK), 0:SSD_WIDTH].astype(F32)
        kmats = []
        for g in range(SSD_GROUPS):
            xw = jnp.concatenate([xs[:, g * gw:(g + 1) * gw] * wide[:CHUNK, g * gw:(g + 1) * gw],
                                  xs[:, g * gw:(g + 1) * gw] * wide[:CHUNK, SSD_WIDTH + g * gw:SSD_WIDTH + (g + 1) * gw]],
                                 axis=1).astype(BF16)
            bm = u_ref[pl.ds(base, CHUNK), SSD_WIDTH + g * SSD_STATE:SSD_WIDTH + (g + 1) * SSD_STATE]
            kmats.append(_dot_tn(bm, xw))
        return dt, acol, ecol, wide[CHUNK:], dec, kmats

    def advance(s, dec, kmats, backward):
        off = SSD_WIDTH if backward else 0
        koff = gw if backward else 0
        return [dec[:, off + g * gw:off + (g + 1) * gw] * s[g] + kmats[g][:, koff:koff + gw]
                for g in range(SSD_GROUPS)]

    ctx_terms = [chunk_terms(uc, dtcv, dacv, c * CHUNK) for c in range(nchc)]
    s_f0 = [jnp.zeros((SSD_STATE, gw), F32) for _ in range(SSD_GROUPS)]
    for c in range(nchc):
        s_f0 = advance(s_f0, ctx_terms[c][4], ctx_terms[c][5], False)
    s_b0 = [jnp.zeros((SSD_STATE, gw), F32) for _ in range(SSD_GROUPS)]
    for c in reversed(range(nchc)):
        s_b0 = advance(s_b0, ctx_terms[c][4], ctx_terms[c][5], True)

    def prep(c, carry):
        base = pl.multiple_of(c * CHUNK, CHUNK)
        dt, acol, ecol, scale, dec, kmats = chunk_terms(u, dtv, dav, base)
        acum[pl.ds(base, CHUNK), :] = acol
        ecum[pl.ds(base, CHUNK), :] = ecol
        hrow = pl.ds(pl.multiple_of(c * 2 * nh, 2 * nh), 2 * nh)
        arow_scr[hrow, :] = acol.T[:2 * nh]
        erow_scr[hrow, :] = ecol.T[:2 * nh]
        dtrow_scr[hrow, :] = dt.T[:2 * nh]
        xpad[pl.ds(base, CHUNK), :] = scale
        dec_scr[pl.ds(pl.multiple_of(c * SUBLANES, SUBLANES), SUBLANES), :] = jnp.broadcast_to(dec, (SUBLANES, 2 * SSD_WIDTH))
        for g in range(SSD_GROUPS):
            sf_scr[c, g] = kmats[g][:, :gw]
            kb_scr[c, g] = kmats[g][:, gw:]
        return carry

    lax.fori_loop(0, nch, prep, 0)

    def chunk_dec(c):
        return dec_scr[pl.ds(pl.multiple_of(c * SUBLANES, SUBLANES), 1), :]

    def fwd(c, s_old):
        dec = chunk_dec(c)
        new = []
        for g in range(SSD_GROUPS):
            new.append(dec[:, g * gw:(g + 1) * gw] * s_old[g] + sf_scr[c, g])
            sf_scr[c, g] = s_old[g]
        return tuple(new)

    lax.fori_loop(0, nch, fwd, tuple(s_f0))

    def bwd(i, s_b):
        c = nch - 1 - i
        base = pl.multiple_of(c * CHUNK, CHUNK)
        acol = acum[pl.ds(base, CHUNK), :]
        ecol = ecum[pl.ds(base, CHUNK), :]
        hrow = pl.ds(pl.multiple_of(c * 2 * nh, 2 * nh), 2 * nh)
        arow = arow_scr[hrow, :]
        erow = erow_scr[hrow, :]
        dt_t = dtrow_scr[hrow, :]
        scale = xpad[pl.ds(base, CHUNK), :]
        ys = []
        for g in range(SSD_GROUPS):
            bm = u[pl.ds(base, CHUNK), SSD_WIDTH + g * SSD_STATE:SSD_WIDTH + (g + 1) * SSD_STATE]
            cm = u[pl.ds(base, CHUNK), SSD_WIDTH + (SSD_GROUPS + g) * SSD_STATE:SSD_WIDTH + (SSD_GROUPS + g + 1) * SSD_STATE]
            cbm = _dot_nt(cm, bm)
            cs_f = _dot(cm, sf_scr[c, g].astype(BF16))
            cs_b = _dot(cm, s_b[g].astype(BF16))
            for pp in range(SSD_PAIRS // SSD_GROUPS):
                p = g * (SSD_PAIRS // SSD_GROUPS) + pp
                xs_b = u[pl.ds(base, CHUNK), p * LANES:(p + 1) * LANES]
                y_h = []
                for hh in range(2):
                    r = 2 * p + hh
                    arg = jnp.where(causal, colb(acol, r) - arow[r:r + 1, :],
                                    erow[nh + r:nh + r + 1, :] - colb(ecol, nh + r))
                    coef = jnp.where(causal, dt_t[r:r + 1, :], dt_t[nh + r:nh + r + 1, :])
                    gm = (cbm * (jnp.exp(arg) * coef)).astype(BF16)
                    y_h.append(_dot(gm, xs_b))
                sl = slice(pp * LANES, (pp + 1) * LANES)
                wl = slice(p * LANES, (p + 1) * LANES)
                wlb = slice(SSD_WIDTH + p * LANES, SSD_WIDTH + (p + 1) * LANES)
                ys.append(pair_sel(y_h[0], y_h[1]) + cs_f[:, sl] * scale[:, wl] + cs_b[:, sl] * scale[:, wlb]
                          + dsk_ref[:, wl] * xs_b.astype(F32))
        y = jnp.concatenate(ys, axis=1)
        y = y * _silu(z_ref[0, pl.ds(base, CHUNK), :].astype(F32))
        ms = jnp.mean(y * y, axis=-1, keepdims=True)
        y_ref[0, pl.ds(base, CHUNK), :] = ((y * lax.rsqrt(ms + EPS)) * nw_ref[...]).astype(BF16)
        dec = chunk_dec(c)
        return tuple(dec[:, SSD_WIDTH + g * gw:SSD_WIDTH + (g + 1) * gw] * s_b[g] + kb_scr[c, g]
                     for g in range(SSD_GROUPS))

    lax.fori_loop(0, nch, bwd, tuple(s_b0), unroll=SSD_UNROLL)


def _ssd(xbc, z, dt, xbcc, dtc, conv_w8, conv_b, dt_bias, a_log, d_skip, norm_w):
    b, L, nconv = xbc.shape
    Lc = xbcc.shape[1]
    nch = L // CHUNK
    per_b = lambda n, w: pl.BlockSpec((1, n, w), lambda i: (i, 0, 0))
    return pl.pallas_call(
        _ssd_kernel,
        grid=(b,),
        in_specs=[
            per_b(L, nconv), per_b(L, SSD_WIDTH), per_b(L, LANES), per_b(Lc, nconv), per_b(Lc, LANES),
            _const_spec(conv_w8.shape), _const_spec(conv_b.shape), _const_spec(dt_bias.shape),
            _const_spec(a_log.shape), _const_spec(d_skip.shape), _const_spec(norm_w.shape),
        ],
        out_specs=per_b(L, SSD_WIDTH),
        out_shape=jax.ShapeDtypeStruct((b, L, SSD_WIDTH), BF16),
        scratch_shapes=[
            pltpu.VMEM((L + 2 * CONV_HALO, nconv), F32),
            pltpu.VMEM((Lc + 2 * CONV_HALO, nconv), F32),
            pltpu.VMEM((L, nconv), BF16),
            pltpu.VMEM((Lc, nconv), BF16),
            pltpu.VMEM((L, LANES), F32), pltpu.VMEM((L, LANES), F32),
            pltpu.VMEM((Lc, LANES), F32), pltpu.VMEM((Lc, LANES), F32),
            pltpu.VMEM((nch, SSD_GROUPS, SSD_STATE, 2 * LANES), F32),
            pltpu.VMEM((nch, SSD_GROUPS, SSD_STATE, 2 * LANES), F32),
            pltpu.VMEM((L, LANES), F32), pltpu.VMEM((L, LANES), F32),
            pltpu.VMEM((nch * SUBLANES, 2 * SSD_WIDTH), F32),
            pltpu.VMEM((nch * 2 * SSD_HEADS, CHUNK), F32), pltpu.VMEM((nch * 2 * SSD_HEADS, CHUNK), F32),
            pltpu.VMEM((nch * 2 * SSD_HEADS, CHUNK), F32),
        ],
        compiler_params=pltpu.CompilerParams(vmem_limit_bytes=VMEM_LIMIT_SSD),
        name="ssd",
    )(xbc, z, dt, xbcc, dtc, conv_w8, conv_b, dt_bias, a_log, d_skip, norm_w)


def _ret_kernel(q_ref, k_ref, v_ref, g_ref, kc_ref, vc_ref, df_ref, db_ref, gn_ref, y_ref, sf_scr):
    L = q_ref.shape[1]
    Lc = kc_ref.shape[1]
    nch = L // CHUNK
    dk = RET_DK
    row_i = lax.broadcasted_iota(jnp.int32, (CHUNK, dk), 0).astype(F32)
    col_i = lax.broadcasted_iota(jnp.int32, (CHUNK, dk), 1).astype(F32)
    rel = row_i - col_i
    crow = lax.broadcasted_iota(jnp.int32, (Lc, dk), 0).astype(F32)

    heads = []
    s_f0 = []
    s_b0 = []
    for h in range(RET_HEADS):
        cols = slice(h * dk, (h + 1) * dk)
        lg_f = -jnp.exp(df_ref[:, cols])
        lg_b = -jnp.exp(db_ref[:, cols])
        heads.append(dict(
            cols=cols,
            dmat=jnp.where(rel >= 0, jnp.exp(jnp.maximum(rel, 0.0) * lg_f), jnp.exp(jnp.maximum(-rel, 0.0) * lg_b)),
            dq_f=jnp.exp((row_i + 1.0) * lg_f),
            dq_b=jnp.exp((CHUNK - row_i) * lg_b),
            dk_f=jnp.exp((CHUNK - 1.0 - row_i) * lg_f),
            dk_b=jnp.exp(row_i * lg_b),
            dc_f=jnp.exp(CHUNK * lg_f),
            dc_b=jnp.exp(CHUNK * lg_b),
        ))
        kc = kc_ref[0, :, cols].astype(F32)
        vc = vc_ref[0, :, cols]
        s_f0.append(_dot_tn((kc * jnp.exp((Lc - 1.0 - crow) * lg_f)).astype(BF16), vc))
        s_b0.append(_dot_tn((kc * jnp.exp(crow * lg_b)).astype(BF16), vc))

    def fwd(c, s_f):
        base = pl.multiple_of(c * CHUNK, CHUNK)
        new = []
        for h, hd in enumerate(heads):
            sf_scr[c, h] = s_f[h]
            kk = k_ref[0, pl.ds(base, CHUNK), hd["cols"]].astype(F32)
            vv = v_ref[0, pl.ds(base, CHUNK), hd["cols"]]
            new.append(hd["dc_f"] * s_f[h] + _dot_tn((kk * hd["dk_f"]).astype(BF16), vv))
        return tuple(new)

    lax.fori_loop(0, nch, fwd, tuple(s_f0), unroll=RET_UNROLL)

    def bwd(i, s_bs):
        c = nch - 1 - i
        base = pl.multiple_of(c * CHUNK, CHUNK)
        new = []
        for h, hd in enumerate(heads):
            qq = q_ref[0, pl.ds(base, CHUNK), hd["cols"]]
            kk = k_ref[0, pl.ds(base, CHUNK), hd["cols"]]
            vv = v_ref[0, pl.ds(base, CHUNK), hd["cols"]]
            s_b = s_bs[h]
            scores = (_dot_nt(qq, kk) * hd["dmat"]).astype(BF16)
            y = (_dot(scores, vv)
                 + _dot(qq, sf_scr[c, h].astype(BF16)) * hd["dq_f"]
                 + _dot(qq, s_b.astype(BF16)) * hd["dq_b"])
            mu = jnp.mean(y, axis=-1, keepdims=True)
            yc = y - mu
            var = jnp.mean(yc * yc, axis=-1, keepdims=True)
            yn = (yc * lax.rsqrt(var + EPS)) * gn_ref[:, hd["cols"]]
            gate = _silu(g_ref[0, pl.ds(base, CHUNK), hd["cols"]].astype(F32))
            y_ref[0, pl.ds(base, CHUNK), hd["cols"]] = (yn * gate).astype(BF16)
            new.append(hd["dc_b"] * s_b + _dot_tn((kk.astype(F32) * hd["dk_b"]).astype(BF16), vv))
        return tuple(new)

    lax.fori_loop(0, nch, bwd, tuple(s_b0), unroll=RET_UNROLL)


def _retention(q, k, v, g, kc, vc, decay_f, decay_b, gn_w):
    b, L, w = q.shape
    Lc = kc.shape[1]
    nch = L // CHUNK
    per_b = lambda n: pl.BlockSpec((1, n, w), lambda i: (i, 0, 0))
    return pl.pallas_call(
        _ret_kernel,
        grid=(b,),
        in_specs=[per_b(L), per_b(L), per_b(L), per_b(L), per_b(Lc), per_b(Lc),
                  _const_spec((1, w)), _const_spec((1, w)), _const_spec((1, w))],
        out_specs=per_b(L),
        out_shape=jax.ShapeDtypeStruct((b, L, w), BF16),
        scratch_shapes=[
            pltpu.VMEM((nch, RET_HEADS, RET_DK, RET_DK), F32),
        ],
        compiler_params=pltpu.CompilerParams(vmem_limit_bytes=VMEM_LIMIT),
        name="retention",
    )(q, k, v, g, kc, vc, decay_f, decay_b, gn_w)


def _outproj_router_kernel(yr_ref, ys_ref, x_ref, g1_ref, sh2_ref, sc2_ref, npost_ref, npre_ref,
                           wor_ref, wos_ref, wr_ref, br_ref, tri_ref,
                           x1_ref, h2_ref, route_ref, cnt_ref, slots_ref,
                           wcat, carry):
    i = pl.program_id(0)

    @pl.when(i == 0)
    def _():
        wr = wr_ref[...]
        hi = wr.astype(BF16)
        wcat[:, :LANES] = hi
        wcat[:, LANES:] = (wr - hi.astype(F32)).astype(BF16)
        carry[...] = jnp.zeros_like(carry)

    tm = tri_ref.shape[0]
    running = carry[...]
    for sub in range(x_ref.shape[0] // tm):
        running = _route_subtile(sub, tm, running, yr_ref, ys_ref, x_ref, g1_ref, sh2_ref, sc2_ref, npost_ref,
                                 npre_ref, wor_ref, wos_ref, br_ref, tri_ref, x1_ref, h2_ref, route_ref,
                                 slots_ref, wcat)
    carry[...] = running
    cnt_ref[...] = running


def _route_subtile(sub, tm, running, yr_ref, ys_ref, x_ref, g1_ref, sh2_ref, sc2_ref, npost_ref, npre_ref,
                   wor_ref, wos_ref, br_ref, tri_ref, x1_ref, h2_ref, route_ref, slots_ref, wcat):
    rows = slice(sub * tm, (sub + 1) * tm)
    y = _dot(yr_ref[rows, :], wor_ref[...]) + _dot(ys_ref[rows, :], wos_ref[...])
    ms = jnp.mean(y * y, axis=-1, keepdims=True)
    x1 = x_ref[rows, :] + (y * lax.rsqrt(ms + EPS)) * (g1_ref[0] * npost_ref[...])
    x1_ref[rows, :] = x1
    h2 = _norm_mod(x1, npre_ref[...], sc2_ref[0], sh2_ref[0])
    h2_ref[rows, :] = h2

    h_hi = h2.astype(BF16)
    h_lo = (h2 - h_hi.astype(F32)).astype(BF16)
    both = _dot(h_hi, wcat[...])
    lg = both[:, :LANES] + both[:, LANES:] + _dot(h_lo, wcat[:, :LANES]) + br_ref[...]

    lane = lax.broadcasted_iota(jnp.int32, (tm, LANES), 1)
    lane_f = lane.astype(F32)
    is_grp = (lane >= N_EXPERTS) & (lane < N_EXPERTS + MOE_GROUPS)
    gl = jnp.where(is_grp, lg, NEG_BIG)
    mg = jnp.max(gl, axis=-1, keepdims=True)
    grp_lane = jnp.min(jnp.where(gl == mg, lane_f, 1e9), axis=-1, keepdims=True)
    p_g = 1.0 / jnp.sum(jnp.where(is_grp, jnp.exp(gl - mg), 0.0), axis=-1, keepdims=True)
    first = (grp_lane - N_EXPERTS) * EXPERTS_PER_GROUP
    in_grp = (lane_f >= first) & (lane_f < first + EXPERTS_PER_GROUP)
    el = jnp.where(in_grp, lg, NEG_BIG)
    t1 = jnp.max(el, axis=-1, keepdims=True)
    i1 = jnp.min(jnp.where(el == t1, lane_f, 1e9), axis=-1, keepdims=True)
    el2 = jnp.where(lane_f == i1, NEG_BIG, el)
    t2 = jnp.max(el2, axis=-1, keepdims=True)
    i2 = jnp.min(jnp.where(el2 == t2, lane_f, 1e9), axis=-1, keepdims=True)
    s = jnp.exp(t2 - t1)
    w1 = p_g / (1.0 + s)
    w2 = p_g * s / (1.0 + s)

    oh1 = (lane_f == i1)
    oh2 = (lane_f == i2)
    oh = (oh1 | oh2).astype(BF16)
    before = _dot(tri_ref[...], oh) + running
    rank1 = jnp.sum(jnp.where(oh1, before, 0.0), axis=-1, keepdims=True)
    rank2 = jnp.sum(jnp.where(oh2, before, 0.0), axis=-1, keepdims=True)

    route_ref[rows, :] = jnp.where(lane == 0, w1, jnp.where(lane == 1, w2, 0.0))

    row = lax.broadcasted_iota(jnp.int32, (tm, LANES), 0)
    on_diag = (row % LANES) == lane
    per = tm // LANES
    n_sub = x_ref.shape[0] // tm
    for qi, col in enumerate((i1, i2, rank1, rank2)):
        picked = jnp.where(on_diag, col, 0.0)
        dense = jnp.sum(picked.reshape(per, LANES, LANES), axis=1).astype(jnp.int32)
        slots_ref[0, (qi * n_sub + sub) * per:(qi * n_sub + sub + 1) * per, :] = dense
    return running + jnp.sum(oh.astype(F32), axis=0, keepdims=True)


def _outproj_router(yr, ys, x2, mod3, npost, npre, wo_r, wo_s, w_router, b_router, seq_len):
    T, d = x2.shape
    tm = TM_OUT
    per_seq = seq_len // tm
    rw = yr.shape[1]
    sub = TM_ROUTE_SUB
    tri = (jnp.arange(sub)[:, None] > jnp.arange(sub)[None, :]).astype(BF16)
    tok = lambda w: pl.BlockSpec((tm, w), lambda i: (i, 0))
    modv = lambda k: pl.BlockSpec((1, 1, d), lambda i: (i // per_seq, 0, k))
    return pl.pallas_call(
        _outproj_router_kernel,
        grid=(T // tm,),
        in_specs=[
            tok(rw), tok(rw), tok(d), modv(2), modv(3), modv(4),
            _const_spec((1, d)), _const_spec((1, d)),
            _const_spec(wo_r.shape), _const_spec(wo_s.shape), _const_spec(w_router.shape), _const_spec((1, LANES)),
            _const_spec((sub, sub)),
        ],
        out_specs=[tok(d), tok(d), tok(LANES), _const_spec((1, LANES)),
                   pl.BlockSpec((1, 2 * TOP_K * (tm // LANES), LANES), lambda i: (i, 0, 0))],
        out_shape=[jax.ShapeDtypeStruct((T, d), F32), jax.ShapeDtypeStruct((T, d), F32),
                   jax.ShapeDtypeStruct((T, LANES), F32), jax.ShapeDtypeStruct((1, LANES), F32),
                   jax.ShapeDtypeStruct((T // tm, 2 * TOP_K * (tm // LANES), LANES), jnp.int32)],
        scratch_shapes=[pltpu.VMEM((d, 2 * LANES), BF16), pltpu.VMEM((1, LANES), F32)],
        compiler_params=pltpu.CompilerParams(dimension_semantics=("arbitrary",),
                                             vmem_limit_bytes=VMEM_LIMIT),
        name="outproj_router",
    )(yr, ys, x2, mod3, mod3, mod3, npost, npre, wo_r, wo_s, w_router, b_router, tri)


def _slot_rows_kernel(start_ref, slots_ref, o_ref):
    half = slots_ref.shape[1] // 2
    expert = slots_ref[:, :half, :]
    row = slots_ref[:, half:, :]
    for e in range(N_EXPERTS):
        row = row + jnp.where(expert == e, start_ref[e], 0)
    for r in range(half):
        o_ref[:, 0, r * LANES:(r + 1) * LANES] = row[:, r, :]


def _slot_rows(pad_start, slots):
    nt, rows, _ = slots.shape
    grid_spec = pltpu.PrefetchScalarGridSpec(
        num_scalar_prefetch=1,
        grid=(1,),
        in_specs=[pl.BlockSpec(slots.shape, lambda i, ps: (0, 0, 0))],
        out_specs=pl.BlockSpec((nt, 1, rows // 2 * LANES), lambda i, ps: (0, 0, 0)),
    )
    return pl.pallas_call(
        _slot_rows_kernel,
        grid_spec=grid_spec,
        out_shape=jax.ShapeDtypeStruct((nt, 1, rows // 2 * LANES), jnp.int32),
        name="slot_rows",
    )(pad_start, slots)


def _dispatch_kernel(pad_end_ref, zero_from_ref, dest_ref, h_hbm, xs_hbm, zbuf, hbuf, sem, zsem, in_sem):
    i = pl.program_id(0)
    nt = pl.num_programs(0)
    td = hbuf.shape[1]
    mb = zbuf.shape[0]

    @pl.when(i == 0)
    def _():
        zbuf[...] = jnp.zeros_like(zbuf)

        def pieces(e, act):
            for p in range(mb // ZERO_PIECE):
                row = zero_from_ref[e] + p * ZERO_PIECE

                @pl.when(row < pad_end_ref[e])
                def _():
                    act(pltpu.make_async_copy(zbuf.at[pl.ds(0, ZERO_PIECE)],
                                              xs_hbm.at[pl.ds(pl.multiple_of(row, ZERO_PIECE), ZERO_PIECE)], zsem))

        def start(e, carry):
            pieces(e, lambda c: c.start())
            return carry

        def wait(e, carry):
            pieces(e, lambda c: c.wait())
            return carry

        def tail_fill(blk):
            return pltpu.make_async_copy(zbuf, xs_hbm.at[pl.ds(pl.multiple_of(blk * mb, mb), mb)], zsem)

        def tail_start(blk, carry):
            @pl.when(blk * mb >= pad_end_ref[N_EXPERTS - 1])
            def _():
                tail_fill(blk).start()
            return carry

        def tail_wait(blk, carry):
            @pl.when(blk * mb >= pad_end_ref[N_EXPERTS - 1])
            def _():
                tail_fill(blk).wait()
            return carry

        n_blocks = xs_hbm.shape[0] // mb
        lax.fori_loop(0, N_EXPERTS, start, 0)
        lax.fori_loop(0, n_blocks, tail_start, 0)
        lax.fori_loop(0, N_EXPERTS, wait, 0)
        lax.fori_loop(0, n_blocks, tail_wait, 0)

    def fetch(t):
        return pltpu.make_async_copy(h_hbm.at[pl.ds(pl.multiple_of(t * td, td), td)], hbuf.at[t % DISPATCH_RING],
                                     in_sem.at[t % DISPATCH_RING])

    @pl.when(i == 0)
    def _():
        fetch(i).start()

    @pl.when(i + 1 < nt)
    def _():
        fetch(i + 1).start()

    fetch(i).wait()
    for slot in range(DISPATCH_RING):
        @pl.when(i % DISPATCH_RING == slot)
        def _(slot=slot):
            _issue_row_copies(dest_ref, td, lambda kk, j, d: pltpu.make_async_copy(
                hbuf.at[slot, pl.ds(j, 1)], xs_hbm.at[pl.ds(d, 1)], sem.at[slot]))

    def drain(t):
        for kk in range(TOP_K):
            pltpu.make_async_copy(hbuf.at[t % DISPATCH_RING], xs_hbm.at[pl.ds(0, td)],
                                  sem.at[t % DISPATCH_RING]).wait()

    @pl.when(i > 0)
    def _():
        drain(i - 1)

    @pl.when(i == nt - 1)
    def _():
        drain(i)


def _issue_row_copies(dest_ref, n_tok, row_copy):
    def issue(j, carry):
        for kk in range(TOP_K):
            row_copy(kk, j, dest_ref[0, 0, kk * n_tok + j]).start()
        return carry

    lax.fori_loop(0, n_tok, issue, 0, unroll=DMA_UNROLL)


def _dispatch(pad_end, zero_from, dest, h2, cap):
    T, d = h2.shape
    td = TD_DISPATCH
    nt = T // td
    grid_spec = pltpu.PrefetchScalarGridSpec(
        num_scalar_prefetch=2,
        grid=(nt,),
        in_specs=[
            pl.BlockSpec((1, 1, TOP_K * td), lambda i, pe, zf: (i, 0, 0), memory_space=pltpu.SMEM),
            pl.BlockSpec(memory_space=pl.ANY),
        ],
        out_specs=pl.BlockSpec(memory_space=pl.ANY),
        scratch_shapes=[pltpu.VMEM((MB_EXPERT, d), h2.dtype), pltpu.VMEM((DISPATCH_RING, td, d), h2.dtype),
                        pltpu.SemaphoreType.DMA((DISPATCH_RING,)), pltpu.SemaphoreType.DMA(()),
                        pltpu.SemaphoreType.DMA((DISPATCH_RING,))],
    )
    return pl.pallas_call(
        _dispatch_kernel,
        grid_spec=grid_spec,
        out_shape=jax.ShapeDtypeStruct((cap, d), h2.dtype),
        compiler_params=pltpu.CompilerParams(dimension_semantics=("arbitrary",)),
        name="dispatch",
    )(pad_end, zero_from, dest, h2)


def _expert_kernel(be_ref, first_ref, slot_ref, next_ref, nused_ref, xs_ref, wg_hbm, wu_hbm, wd_hbm, y_ref,
                   wg_f, wu_f, wd_f, wg_b, wu_b, wd_b, sem):
    i = pl.program_id(0)

    def fetch(e, s):
        return [pltpu.make_async_copy(src.at[e], dst.at[s], sem.at[s, k])
                for k, (src, dst) in enumerate(((wg_hbm, wg_f), (wu_hbm, wu_f), (wd_hbm, wd_f)))]

    @pl.when(i == 0)
    def _():
        for c in fetch(be_ref[0], slot_ref[0]):
            c.start()

    @pl.when(first_ref[i] == 1)
    def _():
        s = slot_ref[i]

        @pl.when(next_ref[i] >= 0)
        def _():
            for c in fetch(next_ref[i], 1 - s):
                c.start()

        for c in fetch(be_ref[i], s):
            c.wait()
        wg_b[...] = wg_f[s].astype(BF16)
        wu_b[...] = wu_f[s].astype(BF16)
        wd_b[...] = wd_f[s].astype(BF16)

    @pl.when(i < nused_ref[0])
    def _():
        xb = xs_ref[...].astype(BF16)
        hid = (_silu(_dot(xb, wg_b[...])) * _dot(xb, wu_b[...])).astype(BF16)
        y_ref[...] = _dot(hid, wd_b[...])

    @pl.when(i >= nused_ref[0])
    def _():
        y_ref[...] = jnp.zeros_like(y_ref)


def _expert_plan(padded, pad_end, n_blocks, mb):
    n_used = (pad_end[-1:] // mb).astype(jnp.int32)
    blk_start = jnp.arange(n_blocks, dtype=jnp.int32) * mb
    experts = jnp.arange(N_EXPERTS, dtype=jnp.int32)
    blk_expert = jnp.minimum(jnp.sum((pad_end[None, :] <= blk_start[:, None]).astype(jnp.int32), axis=1),
                             N_EXPERTS - 1)
    prev = jnp.concatenate([jnp.full((1,), -1, jnp.int32), blk_expert[:-1]])
    first = ((blk_start < pad_end[-1]) & (blk_expert != prev)).astype(jnp.int32)
    slot = (jnp.cumsum(first) - 1) % 2
    later = jnp.where((padded > 0)[None, :] & (experts[None, :] > experts[:, None]), experts[None, :], N_EXPERTS)
    next_nonempty = jnp.min(later, axis=1)
    next_nonempty = jnp.where(next_nonempty == N_EXPERTS, -1, next_nonempty)
    nxt = jnp.sum(jnp.where(blk_expert[:, None] == experts[None, :], next_nonempty[None, :], 0), axis=1)
    return [a.astype(jnp.int32) for a in (blk_expert, first, slot, nxt, n_used)]


def _experts(plan, xs, w_gate, w_up, w_down):
    cap, dp = xs.shape
    n_exp, d, de = w_gate.shape
    assert dp == d
    mb = MB_EXPERT
    grid_spec = pltpu.PrefetchScalarGridSpec(
        num_scalar_prefetch=len(plan),
        grid=(cap // mb,),
        in_specs=[
            pl.BlockSpec((mb, dp), lambda i, be, fi, sl, nx, nu: (jnp.minimum(i, nu[0] - 1), 0)),
            pl.BlockSpec(memory_space=pl.ANY), pl.BlockSpec(memory_space=pl.ANY), pl.BlockSpec(memory_space=pl.ANY),
        ],
        out_specs=pl.BlockSpec((mb, dp), lambda i, be, fi, sl, nx, nu: (i, 0)),
        scratch_shapes=[pltpu.VMEM((2, d, de), F32), pltpu.VMEM((2, d, de), F32), pltpu.VMEM((2, de, d), F32),
                        pltpu.VMEM((d, de), BF16), pltpu.VMEM((d, de), BF16), pltpu.VMEM((de, d), BF16),
                        pltpu.SemaphoreType.DMA((2, 3))],
    )
    return pl.pallas_call(
        _expert_kernel,
        grid_spec=grid_spec,
        out_shape=jax.ShapeDtypeStruct((cap, dp), xs.dtype),
        compiler_params=pltpu.CompilerParams(dimension_semantics=("arbitrary",),
                                             vmem_limit_bytes=VMEM_LIMIT),
        name="experts",
    )(*plan, xs, w_gate, w_up, w_down)


def _combine_kernel(dest_ref, dest_next_ref, route_ref, x1_ref, g2_ref, nw_ref, yb_hbm, o_ref, buf, sem):
    i = pl.program_id(0)
    nt = pl.num_programs(0)
    tf = x1_ref.shape[0]
    stage = i % 2

    def gather(refs, st):
        _issue_row_copies(refs, tf, lambda kk, j, d: pltpu.make_async_copy(
            yb_hbm.at[pl.ds(d, 1)], buf.at[st, kk, pl.ds(j, 1)], sem.at[st]))

    @pl.when(i == 0)
    def _():
        gather(dest_ref, 0)

    for st in range(2):
        @pl.when((i + 1 < nt) & (stage != st))
        def _(st=st):
            gather(dest_next_ref, st)

    for kk in range(TOP_K):
        pltpu.make_async_copy(yb_hbm.at[pl.ds(0, tf)], buf.at[stage, kk], sem.at[stage]).wait()

    route = route_ref[...]
    out = route[:, 0:1] * buf[stage, 0] + route[:, 1:2] * buf[stage, 1]
    ms = jnp.mean(out * out, axis=-1, keepdims=True)
    o_ref[...] = x1_ref[...] + g2_ref[0] * ((out * lax.rsqrt(ms + EPS)) * nw_ref[...])


def _combine(dest, route, x1, mod3, nw, yb, seq_len):
    T, d = x1.shape
    tf = TF_COMBINE
    nt = T // tf
    per_seq = seq_len // tf
    slot_spec = functools.partial(pl.BlockSpec, (1, 1, TOP_K * tf), memory_space=pltpu.SMEM)
    return pl.pallas_call(
        _combine_kernel,
        grid=(nt,),
        in_specs=[
            slot_spec(lambda i: (i, 0, 0)),
            slot_spec(lambda i: (jnp.minimum(i + 1, nt - 1), 0, 0)),
            pl.BlockSpec((tf, LANES), lambda i: (i, 0)),
            pl.BlockSpec((tf, d), lambda i: (i, 0)),
            pl.BlockSpec((1, 1, d), lambda i: (i // per_seq, 0, 5)),
            _const_spec((1, d)),
            pl.BlockSpec(memory_space=pl.ANY),
        ],
        out_specs=pl.BlockSpec((tf, d), lambda i: (i, 0)),
        out_shape=jax.ShapeDtypeStruct((T, d), F32),
        scratch_shapes=[pltpu.VMEM((2, TOP_K, tf, yb.shape[1]), yb.dtype), pltpu.SemaphoreType.DMA((2,))],
        compiler_params=pltpu.CompilerParams(dimension_semantics=("arbitrary",),
                                             vmem_limit_bytes=VMEM_LIMIT),
        name="combine",
    )(dest, dest, route, x1, mod3, nw, yb)


def _rope_tables(L, n_heads):
    quarter = RET_DK // 4
    freqs = ROPE_BASE ** (-jnp.arange(quarter, dtype=F32) / quarter)
    t = jnp.arange(L)
    ang_r = (t // GRID_W).astype(F32)[:, None] * freqs
    ang_c = (t % GRID_W).astype(F32)[:, None] * freqs
    cos = jnp.concatenate([jnp.cos(ang_r)] * 2 + [jnp.cos(ang_c)] * 2, axis=-1)
    sin = jnp.concatenate([-jnp.sin(ang_r), jnp.sin(ang_r), -jnp.sin(ang_c), jnp.sin(ang_c)], axis=-1)
    return jnp.tile(cos, (1, n_heads)), jnp.tile(sin, (1, n_heads))


def _lane_pad(v, width=LANES):
    return jnp.pad(v, [(0, 0)] * (v.ndim - 1) + [(0, width - v.shape[-1])])


def kernel(x, c, ctx, c_ctx, w_mod, b_mod, norm_pre_mix, norm_post_mix, norm_pre_ffn, norm_post_ffn, w_in, w_out, ret_decay_f, ret_decay_b, ret_gn_w, ssd_conv_w, ssd_conv_b, ssd_dt_bias_f, ssd_dt_bias_b, ssd_a_log_f, ssd_a_log_b, ssd_d, ssd_norm_w, moe_w_rg, moe_b_rg, moe_w_re, moe_b_re, moe_w_gate, moe_w_up, moe_w_down):
    b, L, d = x.shape
    assert w_mod.shape[0] == 1, "single layer: context outputs are never needed"
    assert TM_OUT == TD_DISPATCH == TF_COMBINE, "router, dispatch and combine share one slot-row layout"
    rw = RET_HEADS * RET_DK
    nconv = SSD_WIDTH + 2 * SSD_GROUPS * SSD_STATE
    T = b * L

    mod_rows = -(-(b + 1) // SUBLANES) * SUBLANES
    c_all = jnp.zeros((mod_rows, d), F32).at[:b].set(c).at[b].set(c_ctx)
    mod3 = _modulation(c_all, w_mod[0], b_mod[0]).reshape(mod_rows, 1, 6 * d)

    wi = w_in[0]
    o = 0
    wq = wi[:, o:o + rw]; o += rw
    wk = wi[:, o:o + rw]; o += rw
    wv = wi[:, o:o + rw]; o += rw
    wg = wi[:, o:o + rw]; o += rw
    wz = wi[:, o:o + SSD_WIDTH]; o += SSD_WIDTH
    wxbc = wi[:, o:o + nconv].astype(BF16); o += nconv
    wdt = _lane_pad(wi[:, o:o + 2 * SSD_HEADS]).astype(BF16)
    wqk = jnp.concatenate([wq, wk], axis=1).astype(BF16)
    wvgz = jnp.concatenate([wv, wg, wz], axis=1).astype(BF16)
    cos_t, sin_t = _rope_tables(L, RET_HEADS)
    nw1 = norm_pre_mix[0].reshape(1, d)

    q, k, v, g, z, xbc, dt = _inproj(x, mod3, nw1, wqk, wvgz, wxbc, wdt, cos_t, sin_t)
    kc, vc, xbcc, dtc = _inproj_ctx(ctx, mod3, b, nw1, wk.astype(BF16), wv.astype(BF16), wxbc, wdt)

    conv_w8 = jnp.pad(ssd_conv_w[0], ((0, SUBLANES - SSD_CONV), (0, 0)))
    dt_bias = _lane_pad(jnp.concatenate([ssd_dt_bias_f[0], ssd_dt_bias_b[0]])[None, :])
    a_log = _lane_pad(jnp.concatenate([ssd_a_log_f[0], ssd_a_log_b[0]])[None, :])
    d_skip = jnp.repeat(ssd_d[0], SSD_HEADDIM)[None, :]
    ys = _ssd(xbc, z, dt, xbcc, dtc, conv_w8, ssd_conv_b[0][None, :], dt_bias, a_log, d_skip,
              ssd_norm_w[0][None, :])

    yr = _retention(q, k, v, g, kc, vc,
                    jnp.repeat(ret_decay_f[0], RET_DK)[None, :], jnp.repeat(ret_decay_b[0], RET_DK)[None, :],
                    ret_gn_w[0][None, :])

    wo = w_out[0].astype(BF16)
    w_router = _lane_pad(jnp.concatenate(
        [jnp.transpose(moe_w_re[0], (1, 0, 2)).reshape(d, N_EXPERTS), moe_w_rg[0]], axis=1))
    b_router = _lane_pad(jnp.concatenate([moe_b_re[0].reshape(-1), moe_b_rg[0]])[None, :])
    x1, h2, route, counts, slots = _outproj_router(
        yr.reshape(T, rw), ys.reshape(T, SSD_WIDTH), x.reshape(T, d), mod3,
        norm_post_mix[0][None, :], norm_pre_ffn[0][None, :], wo[:rw], wo[rw:], w_router, b_router, L)

    mb = MB_EXPERT
    n_blocks = -(-(T * TOP_K + N_EXPERTS * (mb - 1)) // mb)
    cap = n_blocks * mb
    cnt = counts[0, :N_EXPERTS].astype(jnp.int32)
    padded = (cnt + mb - 1) // mb * mb
    pad_end = jnp.cumsum(padded)
    pad_start = pad_end - padded

    dest = _slot_rows(pad_start.astype(jnp.int32), slots)
    zero_from = pad_start + cnt // ZERO_PIECE * ZERO_PIECE
    xs = _dispatch(pad_end.astype(jnp.int32), zero_from.astype(jnp.int32), dest, h2, cap)
    yb = _experts(_expert_plan(padded, pad_end, n_blocks, mb), xs, moe_w_gate[0], moe_w_up[0], moe_w_down[0])
    out = _combine(dest, route, x1, mod3, norm_post_ffn[0][None, :], yb, L)
    return out.reshape(b, L, d)
```

```python
import functools

import jax
import jax.numpy as jnp
from jax import lax
from jax.experimental import pallas as pl
from jax.experimental.pallas import tpu as pltpu

F32 = jnp.float32
BF16 = jnp.bfloat16

LANES = 128
SUBLANES = 8
V7X_VMEM_BYTES = 64 * 1024 * 1024
VMEM_LIMIT = V7X_VMEM_BYTES * 3 // 4
VMEM_LIMIT_SSD = V7X_VMEM_BYTES * 7 // 8

EPS = 1e-6
CHUNK = 128
GRID_W = 64
RET_HEADS = 4
RET_DK = 128
ROPE_BASE = 10000.0
SSD_HEADS = 8
SSD_HEADDIM = 64
SSD_GROUPS = 2
SSD_STATE = 128
SSD_WIDTH = SSD_HEADS * SSD_HEADDIM
SSD_CONV = 5
SSD_PAIRS = SSD_WIDTH // LANES
MOE_GROUPS = 4
EXPERTS_PER_GROUP = 8
N_EXPERTS = MOE_GROUPS * EXPERTS_PER_GROUP
TOP_K = 2
CONV_HALO = SUBLANES

TM_PROJ = 512
TM_OUT = 512
TD_DISPATCH = TM_OUT
MB_EXPERT = 512
ZERO_PIECE = 64
TF_COMBINE = TM_OUT
DMA_UNROLL = 8
DISPATCH_RING = 3
RET_UNROLL = 2
NEG_BIG = -1e30


def _silu(v):
    return v * jax.nn.sigmoid(v)


def _dot(a, b):
    return jnp.dot(a, b, preferred_element_type=F32)


def _dot_tn(a, b):
    return lax.dot_general(a, b, (((0,), (0,)), ((), ())), preferred_element_type=F32)


def _dot_nt(a, b):
    return lax.dot_general(a, b, (((1,), (1,)), ((), ())), preferred_element_type=F32)


def _mod_kernel(c_ref, w_ref, b_ref, o_ref):
    a = _silu(c_ref[...])
    w = w_ref[...]
    a_hi = a.astype(BF16)
    a_lo = (a - a_hi.astype(F32)).astype(BF16)
    w_hi = w.astype(BF16)
    w_lo = (w - w_hi.astype(F32)).astype(BF16)
    o_ref[...] = _dot(a_hi, w_hi) + _dot(a_lo, w_hi) + _dot(a_hi, w_lo) + b_ref[...]


def _modulation(c_all, w_mod, b_mod):
    rows, d = c_all.shape
    n = w_mod.shape[1]
    return pl.pallas_call(
        _mod_kernel,
        grid=(n // d,),
        in_specs=[
            pl.BlockSpec((rows, d), lambda j: (0, 0)),
            pl.BlockSpec((d, d), lambda j: (0, j)),
            pl.BlockSpec((1, d), lambda j: (0, j)),
        ],
        out_specs=pl.BlockSpec((rows, d), lambda j: (0, j)),
        out_shape=jax.ShapeDtypeStruct((rows, n), F32),
        name="modulation",
    )(c_all, w_mod, b_mod.reshape(1, n))


def _norm_mod(x, nw, sc, sh):
    ms = jnp.mean(x * x, axis=-1, keepdims=True)
    return (x * lax.rsqrt(ms + EPS)) * (nw * (1.0 + sc)) + sh


def _rope(t, cos, sin_signed, first_half):
    width = t.shape[-1]
    quarter = RET_DK // 4
    swapped = jnp.where(first_half, pltpu.roll(t, width - quarter, 1), pltpu.roll(t, quarter, 1))
    return t * cos + swapped * sin_signed


def _inproj_kernel(x_ref, sh_ref, sc_ref, nw_ref, wqk_ref, wvgz_ref, wxbc_ref, wdt_ref, cos_ref, sin_ref,
                   q_ref, k_ref, v_ref, g_ref, z_ref, xbc_ref, dt_ref):
    hb = _norm_mod(x_ref[0], nw_ref[...], sc_ref[0], sh_ref[0]).astype(BF16)
    rw = q_ref.shape[-1]
    qk = _dot(hb, wqk_ref[...])
    cos = cos_ref[...]
    sin = sin_ref[...]
    lane = lax.broadcasted_iota(jnp.int32, cos.shape, 1)
    first_half = (lane % (RET_DK // 2)) < (RET_DK // 4)
    q_ref[0] = _rope(qk[:, :rw], cos, sin, first_half).astype(BF16)
    k_ref[0] = (_rope(qk[:, rw:], cos, sin, first_half) * (RET_DK ** -0.5)).astype(BF16)
    vgz = _dot(hb, wvgz_ref[...])
    v_ref[0] = vgz[:, :rw].astype(BF16)
    g_ref[0] = vgz[:, rw:2 * rw].astype(BF16)
    z_ref[0] = vgz[:, 2 * rw:].astype(BF16)
    xbc_ref[0] = _dot(hb, wxbc_ref[...]).astype(BF16)
    dt_ref[0] = _dot(hb, wdt_ref[...])


def _inproj_ctx_kernel(x_ref, sh_ref, sc_ref, nw_ref, wk_ref, wv_ref, wxbc_ref, wdt_ref,
                       k_ref, v_ref, xbc_ref, dt_ref):
    hb = _norm_mod(x_ref[0], nw_ref[...], sc_ref[0], sh_ref[0]).astype(BF16)
    k_ref[0] = (_dot(hb, wk_ref[...]) * (RET_DK ** -0.5)).astype(BF16)
    v_ref[0] = _dot(hb, wv_ref[...]).astype(BF16)
    xbc_ref[0] = _dot(hb, wxbc_ref[...]).astype(BF16)
    dt_ref[0] = _dot(hb, wdt_ref[...])


def _const_spec(shape):
    nd = len(shape)
    return pl.BlockSpec(shape, lambda *_: (0,) * nd)


def _inproj(x, mod3, nw, wqk, wvgz, wxbc, wdt, cos_t, sin_t):
    b, L, d = x.shape
    tm = min(TM_PROJ, L)
    rw = wqk.shape[1] // 2
    tok = lambda w: pl.BlockSpec((1, tm, w), lambda i, j: (i, j, 0))
    out_bf = lambda w: jax.ShapeDtypeStruct((b, L, w), BF16)
    return pl.pallas_call(
        _inproj_kernel,
        grid=(b, L // tm),
        in_specs=[
            tok(d),
            pl.BlockSpec((1, 1, d), lambda i, j: (i, 0, 0)),
            pl.BlockSpec((1, 1, d), lambda i, j: (i, 0, 1)),
            _const_spec((1, d)),
            _const_spec(wqk.shape), _const_spec(wvgz.shape), _const_spec(wxbc.shape), _const_spec(wdt.shape),
            pl.BlockSpec((tm, rw), lambda i, j: (j, 0)),
            pl.BlockSpec((tm, rw), lambda i, j: (j, 0)),
        ],
        out_specs=[tok(rw), tok(rw), tok(rw), tok(rw), tok(rw), tok(wxbc.shape[1]), tok(LANES)],
        out_shape=[out_bf(rw), out_bf(rw), out_bf(rw), out_bf(rw), out_bf(rw), out_bf(wxbc.shape[1]),
                   jax.ShapeDtypeStruct((b, L, LANES), F32)],
        compiler_params=pltpu.CompilerParams(vmem_limit_bytes=VMEM_LIMIT),
        name="inproj",
    )(x, mod3, mod3, nw, wqk, wvgz, wxbc, wdt, cos_t, sin_t)


def _inproj_ctx(ctx, mod3, ctx_row, nw, wk, wv, wxbc, wdt):
    b, L, d = ctx.shape
    tm = min(TM_PROJ, L)
    rw = wk.shape[1]
    tok = lambda w: pl.BlockSpec((1, tm, w), lambda i, j: (i, j, 0))
    out_bf = lambda w: jax.ShapeDtypeStruct((b, L, w), BF16)
    return pl.pallas_call(
        _inproj_ctx_kernel,
        grid=(b, L // tm),
        in_specs=[
            tok(d),
            pl.BlockSpec((1, 1, d), lambda i, j: (ctx_row, 0, 0)),
            pl.BlockSpec((1, 1, d), lambda i, j: (ctx_row, 0, 1)),
            _const_spec((1, d)),
            _const_spec(wk.shape), _const_spec(wv.shape), _const_spec(wxbc.shape), _const_spec(wdt.shape),
        ],
        out_specs=[tok(rw), tok(rw), tok(wxbc.shape[1]), tok(LANES)],
        out_shape=[out_bf(rw), out_bf(rw), out_bf(wxbc.shape[1]), jax.ShapeDtypeStruct((b, L, LANES), F32)],
        compiler_params=pltpu.CompilerParams(vmem_limit_bytes=VMEM_LIMIT),
        name="inproj_ctx",
    )(ctx, mod3, mod3, nw, wk, wv, wxbc, wdt)


def _ssd_kernel(xbc_ref, z_ref, dt_ref, xbcc_ref, dtc_ref, cw_ref, cb_ref, dtb_ref, alog_ref, dsk_ref, nw_ref,
                y_ref,
                xpad, xpadc, u, uc, dtv, dav, dtcv, dacv, sf_scr, kb_scr, acum, ecum, dec_scr,
                arow_scr, erow_scr, dtrow_scr):
    L = xbc_ref.shape[1]
    Lc = xbcc_ref.shape[1]
    nch = L // CHUNK
    nchc = Lc // CHUNK
    win = CHUNK + 2 * CONV_HALO
    nconv = xbc_ref.shape[2]
    nh = SSD_HEADS

    def conv_pass(src_ref, pad_ref, dst_ref, n_chunks, length):
        zeros = jnp.zeros((CONV_HALO, nconv), F32)
        pad_ref[0:CONV_HALO, :] = zeros
        pad_ref[CONV_HALO + length:2 * CONV_HALO + length, :] = zeros
        pad_ref[CONV_HALO:CONV_HALO + length, :] = src_ref[0].astype(F32)

        def chunk(c, carry):
            base = pl.multiple_of(c * CHUNK, CHUNK)
            for cb_i in range(nconv // LANES):
                cols = slice(cb_i * LANES, (cb_i + 1) * LANES)
                w = pad_ref[pl.ds(base, win), cols]
                acc = cb_ref[:, cols] + w[CONV_HALO:CONV_HALO + CHUNK] * cw_ref[SSD_CONV // 2:SSD_CONV // 2 + 1, cols]
                for j in range(SSD_CONV):
                    if j == SSD_CONV // 2:
                        continue
                    shifted = pltpu.roll(w, (SSD_CONV // 2 - j) % win, 0)
                    acc = acc + shifted[CONV_HALO:CONV_HALO + CHUNK] * cw_ref[j:j + 1, cols]
                dst_ref[pl.ds(base, CHUNK), cols] = _silu(acc).astype(BF16)
            return carry

        lax.fori_loop(0, n_chunks, chunk, 0)

    conv_pass(xbcc_ref, xpadc, uc, nchc, Lc)
    conv_pass(xbc_ref, xpad, u, nch, L)

    a_neg = -jnp.exp(alog_ref[...])
    dtv[...] = jax.nn.softplus(dt_ref[0] + dtb_ref[...])
    dav[...] = dtv[...] * a_neg
    dtcv[...] = jax.nn.softplus(dtc_ref[0] + dtb_ref[...])
    dacv[...] = dtcv[...] * a_neg

    row_i = lax.broadcasted_iota(jnp.int32, (CHUNK, CHUNK), 0)
    col_i = lax.broadcasted_iota(jnp.int32, (CHUNK, CHUNK), 1)
    causal = col_i <= row_i
    lo_half = col_i < SSD_HEADDIM
    fwd_lane = col_i < nh
    head_of = lax.broadcasted_iota(jnp.int32, (CHUNK, SSD_WIDTH), 1) // SSD_HEADDIM
    src_col = lax.broadcasted_iota(jnp.int32, (CHUNK, SSD_WIDTH), 0)
    exp_f = (head_of == src_col).astype(BF16)
    exp_b = (head_of == src_col - nh).astype(BF16)
    exp_fb = jnp.concatenate([exp_f, exp_b], axis=1)

    def split3(v):
        hi = v.astype(BF16)
        r1 = v - hi.astype(F32)
        mid = r1.astype(BF16)
        return hi, mid, (r1 - mid.astype(F32)).astype(BF16)

    def times_onehot(v, m, passes=3):
        parts = split3(v)[:passes]
        acc = _dot(parts[0], m)
        for part in parts[1:]:
            acc = acc + _dot(part, m)
        return acc

    def colb(mat, r):
        return jnp.broadcast_to(mat[:, r:r + 1], (CHUNK, CHUNK))

    def pair_sel(a, b_):
        return jnp.where(lo_half, a, b_)

    gw = 2 * LANES

    def chunk_terms(u_ref, dt_s, da_s, base):
        dt = dt_s[pl.ds(base, CHUNK), :]
        da = da_s[pl.ds(base, CHUNK), :]
        acol = da
        for step in (1, 2, 4, 8, 16, 32, 64):
            acol = acol + jnp.where(row_i >= step, pltpu.roll(acol, step, 0), 0.0)
        ecol = acol - da
        last = acol[CHUNK - 1:CHUNK, :]
        wgt = jnp.where(fwd_lane, jnp.exp(last - acol), jnp.exp(ecol)) * dt
        scale = jnp.where(fwd_lane, jnp.exp(acol), jnp.exp(last - ecol))
        wide = times_onehot(jnp.concatenate([wgt, scale], axis=0), exp_fb, passes=1)
        dec = times_onehot(jnp.broadcast_to(jnp.exp(last), (SUBLANES, LANES)), exp_fb)[0:1]
        xs = u_ref[pl.ds(base, CHUNK), 0:SSD_WIDTH].astype(F32)
        kmats = []
        for g in range(SSD_GROUPS):
            xw = jnp.concatenate([xs[:, g * gw:(g + 1) * gw] * wide[:CHUNK, g * gw:(g + 1) * gw],
                                  xs[:, g * gw:(g + 1) * gw] * wide[:CHUNK, SSD_WIDTH + g * gw:SSD_WIDTH + (g + 1) * gw]],
                                 axis=1).astype(BF16)
            bm = u_ref[pl.ds(base, CHUNK), SSD_WIDTH + g * SSD_STATE:SSD_WIDTH + (g + 1) * SSD_STATE]
            kmats.append(_dot_tn(bm, xw))
        return dt, acol, ecol, wide[CHUNK:], dec, kmats

    def advance(s, dec, kmats, backward):
        off = SSD_WIDTH if backward else 0
        koff = gw if backward else 0
        return [dec[:, off + g * gw:off + (g + 1) * gw] * s[g] + kmats[g][:, koff:koff + gw]
                for g in range(SSD_GROUPS)]

    ctx_terms = [chunk_terms(uc, dtcv, dacv, c * CHUNK) for c in range(nchc)]
    s_f0 = [jnp.zeros((SSD_STATE, gw), F32) for _ in range(SSD_GROUPS)]
    for c in range(nchc):
        s_f0 = advance(s_f0, ctx_terms[c][4], ctx_terms[c][5], False)
    s_b0 = [jnp.zeros((SSD_STATE, gw), F32) for _ in range(SSD_GROUPS)]
    for c in reversed(range(nchc)):
        s_b0 = advance(s_b0, ctx_terms[c][4], ctx_terms[c][5], True)

    def prep(c, carry):
        base = pl.multiple_of(c * CHUNK, CHUNK)
        dt, acol, ecol, scale, dec, kmats = chunk_terms(u, dtv, dav, base)
        acum[pl.ds(base, CHUNK), :] = acol
        ecum[pl.ds(base, CHUNK), :] = ecol
        hrow = pl.ds(pl.multiple_of(c * 2 * nh, 2 * nh), 2 * nh)
        arow_scr[hrow, :] = acol.T[:2 * nh]
        erow_scr[hrow, :] = ecol.T[:2 * nh]
        dtrow_scr[hrow, :] = dt.T[:2 * nh]
        xpad[pl.ds(base, CHUNK), :] = scale
        dec_scr[pl.ds(pl.multiple_of(c * SUBLANES, SUBLANES), SUBLANES), :] = jnp.broadcast_to(dec, (SUBLANES, 2 * SSD_WIDTH))
        for g in range(SSD_GROUPS):
            sf_scr[c, g] = kmats[g][:, :gw]
            kb_scr[c, g] = kmats[g][:, gw:]
        return carry

    lax.fori_loop(0, nch, prep, 0)

    def chunk_dec(c):
        return dec_scr[pl.ds(pl.multiple_of(c * SUBLANES, SUBLANES), 1), :]

    def fwd(c, s_old):
        dec = chunk_dec(c)
        new = []
        for g in range(SSD_GROUPS):
            new.append(dec[:, g * gw:(g + 1) * gw] * s_old[g] + sf_scr[c, g])
            sf_scr[c, g] = s_old[g]
        return tuple(new)

    lax.fori_loop(0, nch, fwd, tuple(s_f0))

    def bwd(i, s_b):
        c = nch - 1 - i
        base = pl.multiple_of(c * CHUNK, CHUNK)
        acol = acum[pl.ds(base, CHUNK), :]
        ecol = ecum[pl.ds(base, CHUNK), :]
        hrow = pl.ds(pl.multiple_of(c * 2 * nh, 2 * nh), 2 * nh)
        arow = arow_scr[hrow, :]
        erow = erow_scr[hrow, :]
        dt_t = dtrow_scr[hrow, :]
        scale = xpad[pl.ds(base, CHUNK), :]
        ys = []
        for g in range(SSD_GROUPS):
            bm = u[pl.ds(base, CHUNK), SSD_WIDTH + g * SSD_STATE:SSD_WIDTH + (g + 1) * SSD_STATE]
            cm = u[pl.ds(base, CHUNK), SSD_WIDTH + (SSD_GROUPS + g) * SSD_STATE:SSD_WIDTH + (SSD_GROUPS + g + 1) * SSD_STATE]
            cbm = _dot_nt(cm, bm)
            cs_f = _dot(cm, sf_scr[c, g].astype(BF16))
            cs_b = _dot(cm, s_b[g].astype(BF16))
            for pp in range(SSD_PAIRS // SSD_GROUPS):
                p = g * (SSD_PAIRS // SSD_GROUPS) + pp
                xs_b = u[pl.ds(base, CHUNK), p * LANES:(p + 1) * LANES]
                y_h = []
                for hh in range(2):
                    r = 2 * p + hh
                    arg = jnp.where(causal, colb(acol, r) - arow[r:r + 1, :],
                                    erow[nh + r:nh + r + 1, :] - colb(ecol, nh + r))
                    coef = jnp.where(causal, dt_t[r:r + 1, :], dt_t[nh + r:nh + r + 1, :])
                    gm = (cbm * (jnp.exp(arg) * coef)).astype(BF16)
                    y_h.append(_dot(gm, xs_b))
                sl = slice(pp * LANES, (pp + 1) * LANES)
                wl = slice(p * LANES, (p + 1) * LANES)
                wlb = slice(SSD_WIDTH + p * LANES, SSD_WIDTH + (p + 1) * LANES)
                ys.append(pair_sel(y_h[0], y_h[1]) + cs_f[:, sl] * scale[:, wl] + cs_b[:, sl] * scale[:, wlb]
                          + dsk_ref[:, wl] * xs_b.astype(F32))
        y = jnp.concatenate(ys, axis=1)
        y = y * _silu(z_ref[0, pl.ds(base, CHUNK), :].astype(F32))
        ms = jnp.mean(y * y, axis=-1, keepdims=True)
        y_ref[0, pl.ds(base, CHUNK), :] = ((y * lax.rsqrt(ms + EPS)) * nw_ref[...]).astype(BF16)
        dec = chunk_dec(c)
        return tuple(dec[:, SSD_WIDTH + g * gw:SSD_WIDTH + (g + 1) * gw] * s_b[g] + kb_scr[c, g]
                     for g in range(SSD_GROUPS))

    lax.fori_loop(0, nch, bwd, tuple(s_b0))


def _ssd(xbc, z, dt, xbcc, dtc, conv_w8, conv_b, dt_bias, a_log, d_skip, norm_w):
    b, L, nconv = xbc.shape
    Lc = xbcc.shape[1]
    nch = L // CHUNK
    per_b = lambda n, w: pl.BlockSpec((1, n, w), lambda i: (i, 0, 0))
    return pl.pallas_call(
        _ssd_kernel,
        grid=(b,),
        in_specs=[
            per_b(L, nconv), per_b(L, SSD_WIDTH), per_b(L, LANES), per_b(Lc, nconv), per_b(Lc, LANES),
            _const_spec(conv_w8.shape), _const_spec(conv_b.shape), _const_spec(dt_bias.shape),
            _const_spec(a_log.shape), _const_spec(d_skip.shape), _const_spec(norm_w.shape),
        ],
        out_specs=per_b(L, SSD_WIDTH),
        out_shape=jax.ShapeDtypeStruct((b, L, SSD_WIDTH), BF16),
        scratch_shapes=[
            pltpu.VMEM((L + 2 * CONV_HALO, nconv), F32),
            pltpu.VMEM((Lc + 2 * CONV_HALO, nconv), F32),
            pltpu.VMEM((L, nconv), BF16),
            pltpu.VMEM((Lc, nconv), BF16),
            pltpu.VMEM((L, LANES), F32), pltpu.VMEM((L, LANES), F32),
            pltpu.VMEM((Lc, LANES), F32), pltpu.VMEM((Lc, LANES), F32),
            pltpu.VMEM((nch, SSD_GROUPS, SSD_STATE, 2 * LANES), F32),
            pltpu.VMEM((nch, SSD_GROUPS, SSD_STATE, 2 * LANES), F32),
            pltpu.VMEM((L, LANES), F32), pltpu.VMEM((L, LANES), F32),
            pltpu.VMEM((nch * SUBLANES, 2 * SSD_WIDTH), F32),
            pltpu.VMEM((nch * 2 * SSD_HEADS, CHUNK), F32), pltpu.VMEM((nch * 2 * SSD_HEADS, CHUNK), F32),
            pltpu.VMEM((nch * 2 * SSD_HEADS, CHUNK), F32),
        ],
        compiler_params=pltpu.CompilerParams(vmem_limit_bytes=VMEM_LIMIT_SSD),
        name="ssd",
    )(xbc, z, dt, xbcc, dtc, conv_w8, conv_b, dt_bias, a_log, d_skip, norm_w)


def _ret_kernel(q_ref, k_ref, v_ref, g_ref, kc_ref, vc_ref, df_ref, db_ref, gn_ref, y_ref, sf_scr):
    L = q_ref.shape[1]
    Lc = kc_ref.shape[1]
    nch = L // CHUNK
    dk = RET_DK
    row_i = lax.broadcasted_iota(jnp.int32, (CHUNK, dk), 0).astype(F32)
    col_i = lax.broadcasted_iota(jnp.int32, (CHUNK, dk), 1).astype(F32)
    rel = row_i - col_i
    crow = lax.broadcasted_iota(jnp.int32, (Lc, dk), 0).astype(F32)

    heads = []
    s_f0 = []
    s_b0 = []
    for h in range(RET_HEADS):
        cols = slice(h * dk, (h + 1) * dk)
        lg_f = -jnp.exp(df_ref[:, cols])
        lg_b = -jnp.exp(db_ref[:, cols])
        heads.append(dict(
            cols=cols,
            dmat=jnp.where(rel >= 0, jnp.exp(jnp.maximum(rel, 0.0) * lg_f), jnp.exp(jnp.maximum(-rel, 0.0) * lg_b)),
            dq_f=jnp.exp((row_i + 1.0) * lg_f),
            dq_b=jnp.exp((CHUNK - row_i) * lg_b),
            dk_f=jnp.exp((CHUNK - 1.0 - row_i) * lg_f),
            dk_b=jnp.exp(row_i * lg_b),
            dc_f=jnp.exp(CHUNK * lg_f),
            dc_b=jnp.exp(CHUNK * lg_b),
        ))
        kc = kc_ref[0, :, cols].astype(F32)
        vc = vc_ref[0, :, cols]
        s_f0.append(_dot_tn((kc * jnp.exp((Lc - 1.0 - crow) * lg_f)).astype(BF16), vc))
        s_b0.append(_dot_tn((kc * jnp.exp(crow * lg_b)).astype(BF16), vc))

    def fwd(c, s_f):
        base = pl.multiple_of(c * CHUNK, CHUNK)
        new = []
        for h, hd in enumerate(heads):
            sf_scr[c, h] = s_f[h]
            kk = k_ref[0, pl.ds(base, CHUNK), hd["cols"]].astype(F32)
            vv = v_ref[0, pl.ds(base, CHUNK), hd["cols"]]
            new.append(hd["dc_f"] * s_f[h] + _dot_tn((kk * hd["dk_f"]).astype(BF16), vv))
        return tuple(new)

    lax.fori_loop(0, nch, fwd, tuple(s_f0), unroll=RET_UNROLL)

    def bwd(i, s_bs):
        c = nch - 1 - i
        base = pl.multiple_of(c * CHUNK, CHUNK)
        new = []
        for h, hd in enumerate(heads):
            qq = q_ref[0, pl.ds(base, CHUNK), hd["cols"]]
            kk = k_ref[0, pl.ds(base, CHUNK), hd["cols"]]
            vv = v_ref[0, pl.ds(base, CHUNK), hd["cols"]]
            s_b = s_bs[h]
            scores = (_dot_nt(qq, kk) * hd["dmat"]).astype(BF16)
            y = (_dot(scores, vv)
                 + _dot(qq, sf_scr[c, h].astype(BF16)) * hd["dq_f"]
                 + _dot(qq, s_b.astype(BF16)) * hd["dq_b"])
            mu = jnp.mean(y, axis=-1, keepdims=True)
            yc = y - mu
            var = jnp.mean(yc * yc, axis=-1, keepdims=True)
            yn = (yc * lax.rsqrt(var + EPS)) * gn_ref[:, hd["cols"]]
            gate = _silu(g_ref[0, pl.ds(base, CHUNK), hd["cols"]].astype(F32))
            y_ref[0, pl.ds(base, CHUNK), hd["cols"]] = (yn * gate).astype(BF16)
            new.append(hd["dc_b"] * s_b + _dot_tn((kk.astype(F32) * hd["dk_b"]).astype(BF16), vv))
        return tuple(new)

    lax.fori_loop(0, nch, bwd, tuple(s_b0), unroll=RET_UNROLL)


def _retention(q, k, v, g, kc, vc, decay_f, decay_b, gn_w):
    b, L, w = q.shape
    Lc = kc.shape[1]
    nch = L // CHUNK
    per_b = lambda n: pl.BlockSpec((1, n, w), lambda i: (i, 0, 0))
    return pl.pallas_call(
        _ret_kernel,
        grid=(b,),
        in_specs=[per_b(L), per_b(L), per_b(L), per_b(L), per_b(Lc), per_b(Lc),
                  _const_spec((1, w)), _const_spec((1, w)), _const_spec((1, w))],
        out_specs=per_b(L),
        out_shape=jax.ShapeDtypeStruct((b, L, w), BF16),
        scratch_shapes=[
            pltpu.VMEM((nch, RET_HEADS, RET_DK, RET_DK), F32),
        ],
        compiler_params=pltpu.CompilerParams(vmem_limit_bytes=VMEM_LIMIT),
        name="retention",
    )(q, k, v, g, kc, vc, decay_f, decay_b, gn_w)


def _outproj_router_kernel(yr_ref, ys_ref, x_ref, g1_ref, sh2_ref, sc2_ref, npost_ref, npre_ref,
                           wor_ref, wos_ref, wr_ref, br_ref, tri_ref,
                           x1_ref, h2_ref, route_ref, cnt_ref, slots_ref,
                           wcat, carry):
    i = pl.program_id(0)

    @pl.when(i == 0)
    def _():
        wr = wr_ref[...]
        hi = wr.astype(BF16)
        wcat[:, :LANES] = hi
        wcat[:, LANES:] = (wr - hi.astype(F32)).astype(BF16)
        carry[...] = jnp.zeros_like(carry)

    tm = tri_ref.shape[0]
    running = carry[...]
    for sub in range(x_ref.shape[0] // tm):
        running = _route_subtile(sub, tm, running, yr_ref, ys_ref, x_ref, g1_ref, sh2_ref, sc2_ref, npost_ref,
                                 npre_ref, wor_ref, wos_ref, br_ref, tri_ref, x1_ref, h2_ref, route_ref,
                                 slots_ref, wcat)
    carry[...] = running
    cnt_ref[...] = running


def _route_subtile(sub, tm, running, yr_ref, ys_ref, x_ref, g1_ref, sh2_ref, sc2_ref, npost_ref, npre_ref,
                   wor_ref, wos_ref, br_ref, tri_ref, x1_ref, h2_ref, route_ref, slots_ref, wcat):
    rows = slice(sub * tm, (sub + 1) * tm)
    y = _dot(yr_ref[rows, :], wor_ref[...]) + _dot(ys_ref[rows, :], wos_ref[...])
    ms = jnp.mean(y * y, axis=-1, keepdims=True)
    x1 = x_ref[rows, :] + (y * lax.rsqrt(ms + EPS)) * (g1_ref[0] * npost_ref[...])
    x1_ref[rows, :] = x1
    h2 = _norm_mod(x1, npre_ref[...], sc2_ref[0], sh2_ref[0])
    h2_ref[rows, :] = h2

    h_hi = h2.astype(BF16)
    h_lo = (h2 - h_hi.astype(F32)).astype(BF16)
    both = _dot(h_hi, wcat[...])
    lg = both[:, :LANES] + both[:, LANES:] + _dot(h_lo, wcat[:, :LANES]) + br_ref[...]

    lane = lax.broadcasted_iota(jnp.int32, (tm, LANES), 1)
    lane_f = lane.astype(F32)
    is_grp = (lane >= N_EXPERTS) & (lane < N_EXPERTS + MOE_GROUPS)
    gl = jnp.where(is_grp, lg, NEG_BIG)
    mg = jnp.max(gl, axis=-1, keepdims=True)
    grp_lane = jnp.min(jnp.where(gl == mg, lane_f, 1e9), axis=-1, keepdims=True)
    p_g = 1.0 / jnp.sum(jnp.where(is_grp, jnp.exp(gl - mg), 0.0), axis=-1, keepdims=True)
    first = (grp_lane - N_EXPERTS) * EXPERTS_PER_GROUP
    in_grp = (lane_f >= first) & (lane_f < first + EXPERTS_PER_GROUP)
    el = jnp.where(in_grp, lg, NEG_BIG)
    t1 = jnp.max(el, axis=-1, keepdims=True)
    i1 = jnp.min(jnp.where(el == t1, lane_f, 1e9), axis=-1, keepdims=True)
    el2 = jnp.where(lane_f == i1, NEG_BIG, el)
    t2 = jnp.max(el2, axis=-1, keepdims=True)
    i2 = jnp.min(jnp.where(el2 == t2, lane_f, 1e9), axis=-1, keepdims=True)
    s = jnp.exp(t2 - t1)
    w1 = p_g / (1.0 + s)
    w2 = p_g * s / (1.0 + s)

    oh1 = (lane_f == i1)
    oh2 = (lane_f == i2)
    oh = (oh1 | oh2).astype(BF16)
    before = _dot(tri_ref[...], oh) + running
    rank1 = jnp.sum(jnp.where(oh1, before, 0.0), axis=-1, keepdims=True)
    rank2 = jnp.sum(jnp.where(oh2, before, 0.0), axis=-1, keepdims=True)

    route_ref[rows, :] = jnp.where(lane == 0, w1, jnp.where(lane == 1, w2, 0.0))

    row = lax.broadcasted_iota(jnp.int32, (tm, LANES), 0)
    on_diag = (row % LANES) == lane
    per = tm // LANES
    n_sub = x_ref.shape[0] // tm
    for qi, col in enumerate((i1, i2, rank1, rank2)):
        picked = jnp.where(on_diag, col, 0.0)
        dense = jnp.sum(picked.reshape(per, LANES, LANES), axis=1).astype(jnp.int32)
        slots_ref[0, (qi * n_sub + sub) * per:(qi * n_sub + sub + 1) * per, :] = dense
    return running + jnp.sum(oh.astype(F32), axis=0, keepdims=True)


def _outproj_router(yr, ys, x2, mod3, npost, npre, wo_r, wo_s, w_router, b_router, seq_len):
    T, d = x2.shape
    tm = TM_OUT
    per_seq = seq_len // tm
    rw = yr.shape[1]
    sub = tm
    tri = (jnp.arange(sub)[:, None] > jnp.arange(sub)[None, :]).astype(BF16)
    tok = lambda w: pl.BlockSpec((tm, w), lambda i: (i, 0))
    modv = lambda k: pl.BlockSpec((1, 1, d), lambda i: (i // per_seq, 0, k))
    return pl.pallas_call(
        _outproj_router_kernel,
        grid=(T // tm,),
        in_specs=[
            tok(rw), tok(rw), tok(d), modv(2), modv(3), modv(4),
            _const_spec((1, d)), _const_spec((1, d)),
            _const_spec(wo_r.shape), _const_spec(wo_s.shape), _const_spec(w_router.shape), _const_spec((1, LANES)),
            _const_spec((sub, sub)),
        ],
        out_specs=[tok(d), tok(d), tok(LANES), _const_spec((1, LANES)),
                   pl.BlockSpec((1, 2 * TOP_K * (tm // LANES), LANES), lambda i: (i, 0, 0))],
        out_shape=[jax.ShapeDtypeStruct((T, d), F32), jax.ShapeDtypeStruct((T, d), F32),
                   jax.ShapeDtypeStruct((T, LANES), F32), jax.ShapeDtypeStruct((1, LANES), F32),
                   jax.ShapeDtypeStruct((T // tm, 2 * TOP_K * (tm // LANES), LANES), jnp.int32)],
        scratch_shapes=[pltpu.VMEM((d, 2 * LANES), BF16), pltpu.VMEM((1, LANES), F32)],
        compiler_params=pltpu.CompilerParams(dimension_semantics=("arbitrary",),
                                             vmem_limit_bytes=VMEM_LIMIT),
        name="outproj_router",
    )(yr, ys, x2, mod3, mod3, mod3, npost, npre, wo_r, wo_s, w_router, b_router, tri)


def _slot_rows_kernel(start_ref, slots_ref, o_ref):
    half = slots_ref.shape[1] // 2
    expert = slots_ref[:, :half, :]
    row = slots_ref[:, half:, :]
    for e in range(N_EXPERTS):
        row = row + jnp.where(expert == e, start_ref[e], 0)
    for r in range(half):
        o_ref[:, 0, r * LANES:(r + 1) * LANES] = row[:, r, :]


def _slot_rows(pad_start, slots):
    nt, rows, _ = slots.shape
    grid_spec = pltpu.PrefetchScalarGridSpec(
        num_scalar_prefetch=1,
        grid=(1,),
        in_specs=[pl.BlockSpec(slots.shape, lambda i, ps: (0, 0, 0))],
        out_specs=pl.BlockSpec((nt, 1, rows // 2 * LANES), lambda i, ps: (0, 0, 0)),
    )
    return pl.pallas_call(
        _slot_rows_kernel,
        grid_spec=grid_spec,
        out_shape=jax.ShapeDtypeStruct((nt, 1, rows // 2 * LANES), jnp.int32),
        name="slot_rows",
    )(pad_start, slots)


def _dispatch_kernel(pad_end_ref, zero_from_ref, dest_ref, h_hbm, xs_hbm, zbuf, hbuf, sem, zsem, in_sem):
    i = pl.program_id(0)
    nt = pl.num_programs(0)
    td = hbuf.shape[1]
    mb = zbuf.shape[0]

    @pl.when(i == 0)
    def _():
        zbuf[...] = jnp.zeros_like(zbuf)

        def pieces(e, act):
            for p in range(mb // ZERO_PIECE):
                row = zero_from_ref[e] + p * ZERO_PIECE

                @pl.when(row < pad_end_ref[e])
                def _():
                    act(pltpu.make_async_copy(zbuf.at[pl.ds(0, ZERO_PIECE)],
                                              xs_hbm.at[pl.ds(pl.multiple_of(row, ZERO_PIECE), ZERO_PIECE)], zsem))

        def start(e, carry):
            pieces(e, lambda c: c.start())
            return carry

        def wait(e, carry):
            pieces(e, lambda c: c.wait())
            return carry

        def tail_fill(blk):
            return pltpu.make_async_copy(zbuf, xs_hbm.at[pl.ds(pl.multiple_of(blk * mb, mb), mb)], zsem)

        def tail_start(blk, carry):
            @pl.when(blk * mb >= pad_end_ref[N_EXPERTS - 1])
            def _():
                tail_fill(blk).start()
            return carry

        def tail_wait(blk, carry):
            @pl.when(blk * mb >= pad_end_ref[N_EXPERTS - 1])
            def _():
                tail_fill(blk).wait()
            return carry

        n_blocks = xs_hbm.shape[0] // mb
        lax.fori_loop(0, N_EXPERTS, start, 0)
        lax.fori_loop(0, n_blocks, tail_start, 0)
        lax.fori_loop(0, N_EXPERTS, wait, 0)
        lax.fori_loop(0, n_blocks, tail_wait, 0)

    def fetch(t):
        return pltpu.make_async_copy(h_hbm.at[pl.ds(pl.multiple_of(t * td, td), td)], hbuf.at[t % DISPATCH_RING],
                                     in_sem.at[t % DISPATCH_RING])

    @pl.when(i == 0)
    def _():
        fetch(i).start()

    @pl.when(i + 1 < nt)
    def _():
        fetch(i + 1).start()

    fetch(i).wait()
    for slot in range(DISPATCH_RING):
        @pl.when(i % DISPATCH_RING == slot)
        def _(slot=slot):
            _issue_row_copies(dest_ref, td, lambda kk, j, d: pltpu.make_async_copy(
                hbuf.at[slot, pl.ds(j, 1)], xs_hbm.at[pl.ds(d, 1)], sem.at[slot]))

    def drain(t):
        for kk in range(TOP_K):
            pltpu.make_async_copy(hbuf.at[t % DISPATCH_RING], xs_hbm.at[pl.ds(0, td)],
                                  sem.at[t % DISPATCH_RING]).wait()

    @pl.when(i > 0)
    def _():
        drain(i - 1)

    @pl.when(i == nt - 1)
    def _():
        drain(i)


def _issue_row_copies(dest_ref, n_tok, row_copy):
    def issue(j, carry):
        for kk in range(TOP_K):
            row_copy(kk, j, dest_ref[0, 0, kk * n_tok + j]).start()
        return carry

    lax.fori_loop(0, n_tok, issue, 0, unroll=DMA_UNROLL)


def _dispatch(pad_end, zero_from, dest, h2, cap):
    T, d = h2.shape
    td = TD_DISPATCH
    nt = T // td
    grid_spec = pltpu.PrefetchScalarGridSpec(
        num_scalar_prefetch=2,
        grid=(nt,),
        in_specs=[
            pl.BlockSpec((1, 1, TOP_K * td), lambda i, pe, zf: (i, 0, 0), memory_space=pltpu.SMEM),
            pl.BlockSpec(memory_space=pl.ANY),
        ],
        out_specs=pl.BlockSpec(memory_space=pl.ANY),
        scratch_shapes=[pltpu.VMEM((MB_EXPERT, d), h2.dtype), pltpu.VMEM((DISPATCH_RING, td, d), h2.dtype),
                        pltpu.SemaphoreType.DMA((DISPATCH_RING,)), pltpu.SemaphoreType.DMA(()),
                        pltpu.SemaphoreType.DMA((DISPATCH_RING,))],
    )
    return pl.pallas_call(
        _dispatch_kernel,
        grid_spec=grid_spec,
        out_shape=jax.ShapeDtypeStruct((cap, d), h2.dtype),
        compiler_params=pltpu.CompilerParams(dimension_semantics=("arbitrary",)),
        name="dispatch",
    )(pad_end, zero_from, dest, h2)


def _expert_kernel(be_ref, first_ref, slot_ref, next_ref, nused_ref, xs_ref, wg_hbm, wu_hbm, wd_hbm, y_ref,
                   wg_f, wu_f, wd_f, wg_b, wu_b, wd_b, sem):
    i = pl.program_id(0)

    def fetch(e, s):
        return [pltpu.make_async_copy(src.at[e], dst.at[s], sem.at[s, k])
                for k, (src, dst) in enumerate(((wg_hbm, wg_f), (wu_hbm, wu_f), (wd_hbm, wd_f)))]

    @pl.when(i == 0)
    def _():
        for c in fetch(be_ref[0], slot_ref[0]):
            c.start()

    @pl.when(first_ref[i] == 1)
    def _():
        s = slot_ref[i]

        @pl.when(next_ref[i] >= 0)
        def _():
            for c in fetch(next_ref[i], 1 - s):
                c.start()

        for c in fetch(be_ref[i], s):
            c.wait()
        wg_b[...] = wg_f[s].astype(BF16)
        wu_b[...] = wu_f[s].astype(BF16)
        wd_b[...] = wd_f[s].astype(BF16)

    @pl.when(i < nused_ref[0])
    def _():
        xb = xs_ref[...].astype(BF16)
        hid = (_silu(_dot(xb, wg_b[...])) * _dot(xb, wu_b[...])).astype(BF16)
        y_ref[...] = _dot(hid, wd_b[...])

    @pl.when(i >= nused_ref[0])
    def _():
        y_ref[...] = jnp.zeros_like(y_ref)


def _expert_plan(padded, pad_end, n_blocks, mb):
    n_used = (pad_end[-1:] // mb).astype(jnp.int32)
    blk_start = jnp.arange(n_blocks, dtype=jnp.int32) * mb
    experts = jnp.arange(N_EXPERTS, dtype=jnp.int32)
    blk_expert = jnp.minimum(jnp.sum((pad_end[None, :] <= blk_start[:, None]).astype(jnp.int32), axis=1),
                             N_EXPERTS - 1)
    prev = jnp.concatenate([jnp.full((1,), -1, jnp.int32), blk_expert[:-1]])
    first = ((blk_start < pad_end[-1]) & (blk_expert != prev)).astype(jnp.int32)
    slot = (jnp.cumsum(first) - 1) % 2
    later = jnp.where((padded > 0)[None, :] & (experts[None, :] > experts[:, None]), experts[None, :], N_EXPERTS)
    next_nonempty = jnp.min(later, axis=1)
    next_nonempty = jnp.where(next_nonempty == N_EXPERTS, -1, next_nonempty)
    nxt = jnp.sum(jnp.where(blk_expert[:, None] == experts[None, :], next_nonempty[None, :], 0), axis=1)
    return [a.astype(jnp.int32) for a in (blk_expert, first, slot, nxt, n_used)]


def _experts(plan, xs, w_gate, w_up, w_down):
    cap, dp = xs.shape
    n_exp, d, de = w_gate.shape
    assert dp == d
    mb = MB_EXPERT
    grid_spec = pltpu.PrefetchScalarGridSpec(
        num_scalar_prefetch=len(plan),
        grid=(cap // mb,),
        in_specs=[
            pl.BlockSpec((mb, dp), lambda i, be, fi, sl, nx, nu: (jnp.minimum(i, nu[0] - 1), 0)),
            pl.BlockSpec(memory_space=pl.ANY), pl.BlockSpec(memory_space=pl.ANY), pl.BlockSpec(memory_space=pl.ANY),
        ],
        out_specs=pl.BlockSpec((mb, dp), lambda i, be, fi, sl, nx, nu: (i, 0)),
        scratch_shapes=[pltpu.VMEM((2, d, de), F32), pltpu.VMEM((2, d, de), F32), pltpu.VMEM((2, de, d), F32),
                        pltpu.VMEM((d, de), BF16), pltpu.VMEM((d, de), BF16), pltpu.VMEM((de, d), BF16),
                        pltpu.SemaphoreType.DMA((2, 3))],
    )
    return pl.pallas_call(
        _expert_kernel,
        grid_spec=grid_spec,
        out_shape=jax.ShapeDtypeStruct((cap, dp), xs.dtype),
        compiler_params=pltpu.CompilerParams(dimension_semantics=("arbitrary",),
                                             vmem_limit_bytes=VMEM_LIMIT),
        name="experts",
    )(*plan, xs, w_gate, w_up, w_down)


def _combine_kernel(dest_ref, dest_next_ref, route_ref, x1_ref, g2_ref, nw_ref, yb_hbm, o_ref, buf, sem):
    i = pl.program_id(0)
    nt = pl.num_programs(0)
    tf = x1_ref.shape[0]
    stage = i % 2

    def gather(refs, st):
        _issue_row_copies(refs, tf, lambda kk, j, d: pltpu.make_async_copy(
            yb_hbm.at[pl.ds(d, 1)], buf.at[st, kk, pl.ds(j, 1)], sem.at[st]))

    @pl.when(i == 0)
    def _():
        gather(dest_ref, 0)

    for st in range(2):
        @pl.when((i + 1 < nt) & (stage != st))
        def _(st=st):
            gather(dest_next_ref, st)

    for kk in range(TOP_K):
        pltpu.make_async_copy(yb_hbm.at[pl.ds(0, tf)], buf.at[stage, kk], sem.at[stage]).wait()

    route = route_ref[...]
    out = route[:, 0:1] * buf[stage, 0] + route[:, 1:2] * buf[stage, 1]
    ms = jnp.mean(out * out, axis=-1, keepdims=True)
    o_ref[...] = x1_ref[...] + g2_ref[0] * ((out * lax.rsqrt(ms + EPS)) * nw_ref[...])


def _combine(dest, route, x1, mod3, nw, yb, seq_len):
    T, d = x1.shape
    tf = TF_COMBINE
    nt = T // tf
    per_seq = seq_len // tf
    slot_spec = functools.partial(pl.BlockSpec, (1, 1, TOP_K * tf), memory_space=pltpu.SMEM)
    return pl.pallas_call(
        _combine_kernel,
        grid=(nt,),
        in_specs=[
            slot_spec(lambda i: (i, 0, 0)),
            slot_spec(lambda i: (jnp.minimum(i + 1, nt - 1), 0, 0)),
            pl.BlockSpec((tf, LANES), lambda i: (i, 0)),
            pl.BlockSpec((tf, d), lambda i: (i, 0)),
            pl.BlockSpec((1, 1, d), lambda i: (i // per_seq, 0, 5)),
            _const_spec((1, d)),
            pl.BlockSpec(memory_space=pl.ANY),
        ],
        out_specs=pl.BlockSpec((tf, d), lambda i: (i, 0)),
        out_shape=jax.ShapeDtypeStruct((T, d), F32),
        scratch_shapes=[pltpu.VMEM((2, TOP_K, tf, yb.shape[1]), yb.dtype), pltpu.SemaphoreType.DMA((2,))],
        compiler_params=pltpu.CompilerParams(dimension_semantics=("arbitrary",),
                                             vmem_limit_bytes=VMEM_LIMIT),
        name="combine",
    )(dest, dest, route, x1, mod3, nw, yb)


def _rope_tables(L, n_heads):
    quarter = RET_DK // 4
    freqs = ROPE_BASE ** (-jnp.arange(quarter, dtype=F32) / quarter)
    t = jnp.arange(L)
    ang_r = (t // GRID_W).astype(F32)[:, None] * freqs
    ang_c = (t % GRID_W).astype(F32)[:, None] * freqs
    cos = jnp.concatenate([jnp.cos(ang_r)] * 2 + [jnp.cos(ang_c)] * 2, axis=-1)
    sin = jnp.concatenate([-jnp.sin(ang_r), jnp.sin(ang_r), -jnp.sin(ang_c), jnp.sin(ang_c)], axis=-1)
    return jnp.tile(cos, (1, n_heads)), jnp.tile(sin, (1, n_heads))


def _lane_pad(v, width=LANES):
    return jnp.pad(v, [(0, 0)] * (v.ndim - 1) + [(0, width - v.shape[-1])])


def kernel(x, c, ctx, c_ctx, w_mod, b_mod, norm_pre_mix, norm_post_mix, norm_pre_ffn, norm_post_ffn, w_in, w_out, ret_decay_f, ret_decay_b, ret_gn_w, ssd_conv_w, ssd_conv_b, ssd_dt_bias_f, ssd_dt_bias_b, ssd_a_log_f, ssd_a_log_b, ssd_d, ssd_norm_w, moe_w_rg, moe_b_rg, moe_w_re, moe_b_re, moe_w_gate, moe_w_up, moe_w_down):
    b, L, d = x.shape
    assert w_mod.shape[0] == 1, "single layer: context outputs are never needed"
    assert TM_OUT == TD_DISPATCH == TF_COMBINE, "router, dispatch and combine share one slot-row layout"
    rw = RET_HEADS * RET_DK
    nconv = SSD_WIDTH + 2 * SSD_GROUPS * SSD_STATE
    T = b * L

    mod_rows = -(-(b + 1) // SUBLANES) * SUBLANES
    c_all = jnp.zeros((mod_rows, d), F32).at[:b].set(c).at[b].set(c_ctx)
    mod3 = _modulation(c_all, w_mod[0], b_mod[0]).reshape(mod_rows, 1, 6 * d)

    wi = w_in[0]
    o = 0
    wq = wi[:, o:o + rw]; o += rw
    wk = wi[:, o:o + rw]; o += rw
    wv = wi[:, o:o + rw]; o += rw
    wg = wi[:, o:o + rw]; o += rw
    wz = wi[:, o:o + SSD_WIDTH]; o += SSD_WIDTH
    wxbc = wi[:, o:o + nconv].astype(BF16); o += nconv
    wdt = _lane_pad(wi[:, o:o + 2 * SSD_HEADS]).astype(BF16)
    wqk = jnp.concatenate([wq, wk], axis=1).astype(BF16)
    wvgz = jnp.concatenate([wv, wg, wz], axis=1).astype(BF16)
    cos_t, sin_t = _rope_tables(L, RET_HEADS)
    nw1 = norm_pre_mix[0].reshape(1, d)

    q, k, v, g, z, xbc, dt = _inproj(x, mod3, nw1, wqk, wvgz, wxbc, wdt, cos_t, sin_t)
    kc, vc, xbcc, dtc = _inproj_ctx(ctx, mod3, b, nw1, wk.astype(BF16), wv.astype(BF16), wxbc, wdt)

    conv_w8 = jnp.pad(ssd_conv_w[0], ((0, SUBLANES - SSD_CONV), (0, 0)))
    dt_bias = _lane_pad(jnp.concatenate([ssd_dt_bias_f[0], ssd_dt_bias_b[0]])[None, :])
    a_log = _lane_pad(jnp.concatenate([ssd_a_log_f[0], ssd_a_log_b[0]])[None, :])
    d_skip = jnp.repeat(ssd_d[0], SSD_HEADDIM)[None, :]
    ys = _ssd(xbc, z, dt, xbcc, dtc, conv_w8, ssd_conv_b[0][None, :], dt_bias, a_log, d_skip,
              ssd_norm_w[0][None, :])

    yr = _retention(q, k, v, g, kc, vc,
                    jnp.repeat(ret_decay_f[0], RET_DK)[None, :], jnp.repeat(ret_decay_b[0], RET_DK)[None, :],
                    ret_gn_w[0][None, :])

    wo = w_out[0].astype(BF16)
    w_router = _lane_pad(jnp.concatenate(
        [jnp.transpose(moe_w_re[0], (1, 0, 2)).reshape(d, N_EXPERTS), moe_w_rg[0]], axis=1))
    b_router = _lane_pad(jnp.concatenate([moe_b_re[0].reshape(-1), moe_b_rg[0]])[None, :])
    x1, h2, route, counts, slots = _outproj_router(
        yr.reshape(T, rw), ys.reshape(T, SSD_WIDTH), x.reshape(T, d), mod3,
        norm_post_mix[0][None, :], norm_pre_ffn[0][None, :], wo[:rw], wo[rw:], w_router, b_router, L)

    mb = MB_EXPERT
    n_blocks = -(-(T * TOP_K + N_EXPERTS * (mb - 1)) // mb)
    cap = n_blocks * mb
    cnt = counts[0, :N_EXPERTS].astype(jnp.int32)
    padded = (cnt + mb - 1) // mb * mb
    pad_end = jnp.cumsum(padded)
    pad_start = pad_end - padded

    dest = _slot_rows(pad_start.astype(jnp.int32), slots)
    zero_from = pad_start + cnt // ZERO_PIECE * ZERO_PIECE
    xs = _dispatch(pad_end.astype(jnp.int32), zero_from.astype(jnp.int32), dest, h2, cap)
    yb = _experts(_expert_plan(padded, pad_end, n_blocks, mb), xs, moe_w_gate[0], moe_w_up[0], moe_w_down[0])
    out = _combine(dest, route, x1, mod3, norm_post_ffn[0][None, :], yb, L)
    return out.reshape(b, L, d)
```

```python
import functools

import jax
import jax.numpy as jnp
from jax import lax
from jax.experimental import pallas as pl
from jax.experimental.pallas import tpu as pltpu

F32 = jnp.float32
BF16 = jnp.bfloat16

LANES = 128
SUBLANES = 8
V7X_VMEM_BYTES = 64 * 1024 * 1024
VMEM_LIMIT = V7X_VMEM_BYTES * 3 // 4
VMEM_LIMIT_SSD = V7X_VMEM_BYTES * 7 // 8

EPS = 1e-6
CHUNK = 128
GRID_W = 64
RET_HEADS = 4
RET_DK = 128
ROPE_BASE = 10000.0
SSD_HEADS = 8
SSD_HEADDIM = 64
SSD_GROUPS = 2
SSD_STATE = 128
SSD_WIDTH = SSD_HEADS * SSD_HEADDIM
SSD_CONV = 5
SSD_PAIRS = SSD_WIDTH // LANES
MOE_GROUPS = 4
EXPERTS_PER_GROUP = 8
N_EXPERTS = MOE_GROUPS * EXPERTS_PER_GROUP
TOP_K = 2
CONV_HALO = SUBLANES

TM_PROJ = 512
TM_OUT = 512
TD_DISPATCH = TM_OUT
MB_EXPERT = 512
ZERO_PIECE = 64
TF_COMBINE = TM_OUT
DMA_UNROLL = 8
DISPATCH_RING = 3
RET_UNROLL = 2
NEG_BIG = -1e30


def _silu(v):
    return v * jax.nn.sigmoid(v)


def _dot(a, b):
    return jnp.dot(a, b, preferred_element_type=F32)


def _dot_tn(a, b):
    return lax.dot_general(a, b, (((0,), (0,)), ((), ())), preferred_element_type=F32)


def _dot_nt(a, b):
    return lax.dot_general(a, b, (((1,), (1,)), ((), ())), preferred_element_type=F32)


def _mod_kernel(c_ref, w_ref, b_ref, o_ref):
    a = _silu(c_ref[...])
    w = w_ref[...]
    a_hi = a.astype(BF16)
    a_lo = (a - a_hi.astype(F32)).astype(BF16)
    w_hi = w.astype(BF16)
    w_lo = (w - w_hi.astype(F32)).astype(BF16)
    o_ref[...] = _dot(a_hi, w_hi) + _dot(a_lo, w_hi) + _dot(a_hi, w_lo) + b_ref[...]


def _modulation(c_all, w_mod, b_mod):
    rows, d = c_all.shape
    n = w_mod.shape[1]
    return pl.pallas_call(
        _mod_kernel,
        grid=(n // d,),
        in_specs=[
            pl.BlockSpec((rows, d), lambda j: (0, 0)),
            pl.BlockSpec((d, d), lambda j: (0, j)),
            pl.BlockSpec((1, d), lambda j: (0, j)),
        ],
        out_specs=pl.BlockSpec((rows, d), lambda j: (0, j)),
        out_shape=jax.ShapeDtypeStruct((rows, n), F32),
        name="modulation",
    )(c_all, w_mod, b_mod.reshape(1, n))


def _norm_mod(x, nw, sc, sh):
    ms = jnp.mean(x * x, axis=-1, keepdims=True)
    return (x * lax.rsqrt(ms + EPS)) * (nw * (1.0 + sc)) + sh


def _rope(t, cos, sin_signed, first_half):
    width = t.shape[-1]
    quarter = RET_DK // 4
    swapped = jnp.where(first_half, pltpu.roll(t, width - quarter, 1), pltpu.roll(t, quarter, 1))
    return t * cos + swapped * sin_signed


def _inproj_kernel(x_ref, sh_ref, sc_ref, nw_ref, wqk_ref, wvgz_ref, wxbc_ref, wdt_ref, cos_ref, sin_ref,
                   q_ref, k_ref, v_ref, g_ref, z_ref, xbc_ref, dt_ref):
    hb = _norm_mod(x_ref[0], nw_ref[...], sc_ref[0], sh_ref[0]).astype(BF16)
    rw = q_ref.shape[-1]
    qk = _dot(hb, wqk_ref[...])
    cos = cos_ref[...]
    sin = sin_ref[...]
    lane = lax.broadcasted_iota(jnp.int32, cos.shape, 1)
    first_half = (lane % (RET_DK // 2)) < (RET_DK // 4)
    q_ref[0] = _rope(qk[:, :rw], cos, sin, first_half).astype(BF16)
    k_ref[0] = (_rope(qk[:, rw:], cos, sin, first_half) * (RET_DK ** -0.5)).astype(BF16)
    vgz = _dot(hb, wvgz_ref[...])
    v_ref[0] = vgz[:, :rw].astype(BF16)
    g_ref[0] = vgz[:, rw:2 * rw].astype(BF16)
    z_ref[0] = vgz[:, 2 * rw:].astype(BF16)
    xbc_ref[0] = _dot(hb, wxbc_ref[...]).astype(BF16)
    dt_ref[0] = _dot(hb, wdt_ref[...])


def _inproj_ctx_kernel(x_ref, sh_ref, sc_ref, nw_ref, wk_ref, wv_ref, wxbc_ref, wdt_ref,
                       k_ref, v_ref, xbc_ref, dt_ref):
    hb = _norm_mod(x_ref[0], nw_ref[...], sc_ref[0], sh_ref[0]).astype(BF16)
    k_ref[0] = (_dot(hb, wk_ref[...]) * (RET_DK ** -0.5)).astype(BF16)
    v_ref[0] = _dot(hb, wv_ref[...]).astype(BF16)
    xbc_ref[0] = _dot(hb, wxbc_ref[...]).astype(BF16)
    dt_ref[0] = _dot(hb, wdt_ref[...])


def _const_spec(shape):
    nd = len(shape)
    return pl.BlockSpec(shape, lambda *_: (0,) * nd)


def _inproj(x, mod3, nw, wqk, wvgz, wxbc, wdt, cos_t, sin_t):
    b, L, d = x.shape
    tm = min(TM_PROJ, L)
    rw = wqk.shape[1] // 2
    tok = lambda w: pl.BlockSpec((1, tm, w), lambda i, j: (i, j, 0))
    out_bf = lambda w: jax.ShapeDtypeStruct((b, L, w), BF16)
    return pl.pallas_call(
        _inproj_kernel,
        grid=(b, L // tm),
        in_specs=[
            tok(d),
            pl.BlockSpec((1, 1, d), lambda i, j: (i, 0, 0)),
            pl.BlockSpec((1, 1, d), lambda i, j: (i, 0, 1)),
            _const_spec((1, d)),
            _const_spec(wqk.shape), _const_spec(wvgz.shape), _const_spec(wxbc.shape), _const_spec(wdt.shape),
            pl.BlockSpec((tm, rw), lambda i, j: (j, 0)),
            pl.BlockSpec((tm, rw), lambda i, j: (j, 0)),
        ],
        out_specs=[tok(rw), tok(rw), tok(rw), tok(rw), tok(rw), tok(wxbc.shape[1]), tok(LANES)],
        out_shape=[out_bf(rw), out_bf(rw), out_bf(rw), out_bf(rw), out_bf(rw), out_bf(wxbc.shape[1]),
                   jax.ShapeDtypeStruct((b, L, LANES), F32)],
        compiler_params=pltpu.CompilerParams(vmem_limit_bytes=VMEM_LIMIT),
        name="inproj",
    )(x, mod3, mod3, nw, wqk, wvgz, wxbc, wdt, cos_t, sin_t)


def _inproj_ctx(ctx, mod3, ctx_row, nw, wk, wv, wxbc, wdt):
    b, L, d = ctx.shape
    tm = min(TM_PROJ, L)
    rw = wk.shape[1]
    tok = lambda w: pl.BlockSpec((1, tm, w), lambda i, j: (i, j, 0))
    out_bf = lambda w: jax.ShapeDtypeStruct((b, L, w), BF16)
    return pl.pallas_call(
        _inproj_ctx_kernel,
        grid=(b, L // tm),
        in_specs=[
            tok(d),
            pl.BlockSpec((1, 1, d), lambda i, j: (ctx_row, 0, 0)),
            pl.BlockSpec((1, 1, d), lambda i, j: (ctx_row, 0, 1)),
            _const_spec((1, d)),
            _const_spec(wk.shape), _const_spec(wv.shape), _const_spec(wxbc.shape), _const_spec(wdt.shape),
        ],
        out_specs=[tok(rw), tok(rw), tok(wxbc.shape[1]), tok(LANES)],
        out_shape=[out_bf(rw), out_bf(rw), out_bf(wxbc.shape[1]), jax.ShapeDtypeStruct((b, L, LANES), F32)],
        compiler_params=pltpu.CompilerParams(vmem_limit_bytes=VMEM_LIMIT),
        name="inproj_ctx",
    )(ctx, mod3, mod3, nw, wk, wv, wxbc, wdt)


def _ssd_kernel(xbc_ref, z_ref, dt_ref, xbcc_ref, dtc_ref, cw_ref, cb_ref, dtb_ref, alog_ref, dsk_ref, nw_ref,
                y_ref,
                xpad, xpadc, u, uc, dtv, dav, dtcv, dacv, sf_scr, kb_scr, acum, ecum, dec_scr,
                arow_scr, erow_scr, dtrow_scr, xbd):
    L = xbc_ref.shape[1]
    Lc = xbcc_ref.shape[1]
    nch = L // CHUNK
    nchc = Lc // CHUNK
    win = CHUNK + 2 * CONV_HALO
    nconv = xbc_ref.shape[2]
    nh = SSD_HEADS

    def conv_pass(src_ref, pad_ref, dst_ref, n_chunks, length):
        zeros = jnp.zeros((CONV_HALO, nconv), F32)
        pad_ref[0:CONV_HALO, :] = zeros
        pad_ref[CONV_HALO + length:2 * CONV_HALO + length, :] = zeros
        pad_ref[CONV_HALO:CONV_HALO + length, :] = src_ref[0].astype(F32)

        def chunk(c, carry):
            base = pl.multiple_of(c * CHUNK, CHUNK)
            for cb_i in range(nconv // LANES):
                cols = slice(cb_i * LANES, (cb_i + 1) * LANES)
                w = pad_ref[pl.ds(base, win), cols]
                acc = cb_ref[:, cols] + w[CONV_HALO:CONV_HALO + CHUNK] * cw_ref[SSD_CONV // 2:SSD_CONV // 2 + 1, cols]
                for j in range(SSD_CONV):
                    if j == SSD_CONV // 2:
                        continue
                    shifted = pltpu.roll(w, (SSD_CONV // 2 - j) % win, 0)
                    acc = acc + shifted[CONV_HALO:CONV_HALO + CHUNK] * cw_ref[j:j + 1, cols]
                dst_ref[pl.ds(base, CHUNK), cols] = _silu(acc).astype(BF16)
            return carry

        lax.fori_loop(0, n_chunks, chunk, 0)

    conv_pass(xbcc_ref, xpadc, uc, nchc, Lc)
    conv_pass(xbc_ref, xpad, u, nch, L)

    a_neg = -jnp.exp(alog_ref[...])
    dtv[...] = jax.nn.softplus(dt_ref[0] + dtb_ref[...])
    dav[...] = dtv[...] * a_neg
    dtcv[...] = jax.nn.softplus(dtc_ref[0] + dtb_ref[...])
    dacv[...] = dtcv[...] * a_neg

    row_i = lax.broadcasted_iota(jnp.int32, (CHUNK, CHUNK), 0)
    col_i = lax.broadcasted_iota(jnp.int32, (CHUNK, CHUNK), 1)
    causal = col_i <= row_i
    fwd_lane = col_i < nh
    head_of = lax.broadcasted_iota(jnp.int32, (CHUNK, SSD_WIDTH), 1) // SSD_HEADDIM
    src_col = lax.broadcasted_iota(jnp.int32, (CHUNK, SSD_WIDTH), 0)
    exp_f = (head_of == src_col).astype(BF16)
    exp_b = (head_of == src_col - nh).astype(BF16)
    exp_fb = jnp.concatenate([exp_f, exp_b], axis=1)

    def split3(v):
        hi = v.astype(BF16)
        r1 = v - hi.astype(F32)
        mid = r1.astype(BF16)
        return hi, mid, (r1 - mid.astype(F32)).astype(BF16)

    def times_onehot(v, m, passes=3):
        parts = split3(v)[:passes]
        acc = _dot(parts[0], m)
        for part in parts[1:]:
            acc = acc + _dot(part, m)
        return acc

    def colb(mat, r):
        return jnp.broadcast_to(mat[:, r:r + 1], (CHUNK, CHUNK))

    gw = 2 * LANES

    def chunk_terms(u_ref, dt_s, da_s, base):
        dt = dt_s[pl.ds(base, CHUNK), :]
        da = da_s[pl.ds(base, CHUNK), :]
        acol = da
        for step in (1, 2, 4, 8, 16, 32, 64):
            acol = acol + jnp.where(row_i >= step, pltpu.roll(acol, step, 0), 0.0)
        ecol = acol - da
        last = acol[CHUNK - 1:CHUNK, :]
        wgt = jnp.where(fwd_lane, jnp.exp(last - acol), jnp.exp(ecol)) * dt
        scale = jnp.where(fwd_lane, jnp.exp(acol), jnp.exp(last - ecol))
        wide = times_onehot(jnp.concatenate([wgt, scale], axis=0), exp_fb, passes=1)
        dec = times_onehot(jnp.broadcast_to(jnp.exp(last), (SUBLANES, LANES)), exp_fb)[0:1]
        xs = u_ref[pl.ds(base, CHUNK), 0:SSD_WIDTH].astype(F32)
        kmats = []
        for g in range(SSD_GROUPS):
            xw = jnp.concatenate([xs[:, g * gw:(g + 1) * gw] * wide[:CHUNK, g * gw:(g + 1) * gw],
                                  xs[:, g * gw:(g + 1) * gw] * wide[:CHUNK, SSD_WIDTH + g * gw:SSD_WIDTH + (g + 1) * gw]],
                                 axis=1).astype(BF16)
            bm = u_ref[pl.ds(base, CHUNK), SSD_WIDTH + g * SSD_STATE:SSD_WIDTH + (g + 1) * SSD_STATE]
            kmats.append(_dot_tn(bm, xw))
        return dt, acol, ecol, wide[CHUNK:], dec, kmats

    def advance(s, dec, kmats, backward):
        off = SSD_WIDTH if backward else 0
        koff = gw if backward else 0
        return [dec[:, off + g * gw:off + (g + 1) * gw] * s[g] + kmats[g][:, koff:koff + gw]
                for g in range(SSD_GROUPS)]

    ctx_terms = [chunk_terms(uc, dtcv, dacv, c * CHUNK) for c in range(nchc)]
    s_f0 = [jnp.zeros((SSD_STATE, gw), F32) for _ in range(SSD_GROUPS)]
    for c in range(nchc):
        s_f0 = advance(s_f0, ctx_terms[c][4], ctx_terms[c][5], False)
    s_b0 = [jnp.zeros((SSD_STATE, gw), F32) for _ in range(SSD_GROUPS)]
    for c in reversed(range(nchc)):
        s_b0 = advance(s_b0, ctx_terms[c][4], ctx_terms[c][5], True)

    def prep(c, carry):
        base = pl.multiple_of(c * CHUNK, CHUNK)
        dt, acol, ecol, scale, dec, kmats = chunk_terms(u, dtv, dav, base)
        acum[pl.ds(base, CHUNK), :] = acol
        ecum[pl.ds(base, CHUNK), :] = ecol
        hrow = pl.ds(pl.multiple_of(c * 2 * nh, 2 * nh), 2 * nh)
        arow_scr[hrow, :] = acol.T[:2 * nh]
        erow_scr[hrow, :] = ecol.T[:2 * nh]
        dtrow_scr[hrow, :] = dt.T[:2 * nh]
        xpad[pl.ds(base, CHUNK), :] = scale
        dec_scr[pl.ds(pl.multiple_of(c * SUBLANES, SUBLANES), SUBLANES), :] = jnp.broadcast_to(dec, (SUBLANES, 2 * SSD_WIDTH))
        for g in range(SSD_GROUPS):
            sf_scr[c, g] = kmats[g][:, :gw]
            kb_scr[c, g] = kmats[g][:, gw:]
        return carry

    lax.fori_loop(0, nch, prep, 0)

    def chunk_dec(c):
        return dec_scr[pl.ds(pl.multiple_of(c * SUBLANES, SUBLANES), 1), :]

    def fwd(c, s_old):
        dec = chunk_dec(c)
        new = []
        for g in range(SSD_GROUPS):
            new.append(dec[:, g * gw:(g + 1) * gw] * s_old[g] + sf_scr[c, g])
            sf_scr[c, g] = s_old[g]
        return tuple(new)

    lax.fori_loop(0, nch, fwd, tuple(s_f0))

    xbd[...] = jnp.zeros_like(xbd)

    def bwd(i, s_b):
        c = nch - 1 - i
        base = pl.multiple_of(c * CHUNK, CHUNK)
        acol = acum[pl.ds(base, CHUNK), :]
        ecol = ecum[pl.ds(base, CHUNK), :]
        hrow = pl.ds(pl.multiple_of(c * 2 * nh, 2 * nh), 2 * nh)
        arow = arow_scr[hrow, :]
        erow = erow_scr[hrow, :]
        dt_t = dtrow_scr[hrow, :]
        scale = xpad[pl.ds(base, CHUNK), :]
        xs = u[pl.ds(base, CHUNK), 0:SSD_WIDTH]
        for r in range(nh):
            xbd[r * CHUNK:(r + 1) * CHUNK, r * SSD_HEADDIM:(r + 1) * SSD_HEADDIM] = (
                xs[:, r * SSD_HEADDIM:(r + 1) * SSD_HEADDIM])
        gms = []
        cs_f = []
        cs_b = []
        for g in range(SSD_GROUPS):
            bm = u[pl.ds(base, CHUNK), SSD_WIDTH + g * SSD_STATE:SSD_WIDTH + (g + 1) * SSD_STATE]
            cm = u[pl.ds(base, CHUNK), SSD_WIDTH + (SSD_GROUPS + g) * SSD_STATE:SSD_WIDTH + (SSD_GROUPS + g + 1) * SSD_STATE]
            cbm = _dot_nt(cm, bm)
            cs_f.append(_dot(cm, sf_scr[c, g].astype(BF16)))
            cs_b.append(_dot(cm, s_b[g].astype(BF16)))
            for r in range(g * nh // SSD_GROUPS, (g + 1) * nh // SSD_GROUPS):
                arg = jnp.where(causal, colb(acol, r) - arow[r:r + 1, :],
                                erow[nh + r:nh + r + 1, :] - colb(ecol, nh + r))
                coef = jnp.where(causal, dt_t[r:r + 1, :], dt_t[nh + r:nh + r + 1, :])
                gms.append((cbm * (jnp.exp(arg) * coef)).astype(BF16))
        y = (_dot(jnp.concatenate(gms, axis=1), xbd[...])
             + jnp.concatenate(cs_f, axis=1) * scale[:, :SSD_WIDTH]
             + jnp.concatenate(cs_b, axis=1) * scale[:, SSD_WIDTH:]
             + dsk_ref[...] * xs.astype(F32))
        y = y * _silu(z_ref[0, pl.ds(base, CHUNK), :].astype(F32))
        ms = jnp.mean(y * y, axis=-1, keepdims=True)
        y_ref[0, pl.ds(base, CHUNK), :] = ((y * lax.rsqrt(ms + EPS)) * nw_ref[...]).astype(BF16)
        dec = chunk_dec(c)
        return tuple(dec[:, SSD_WIDTH + g * gw:SSD_WIDTH + (g + 1) * gw] * s_b[g] + kb_scr[c, g]
                     for g in range(SSD_GROUPS))

    lax.fori_loop(0, nch, bwd, tuple(s_b0))


def _ssd(xbc, z, dt, xbcc, dtc, conv_w8, conv_b, dt_bias, a_log, d_skip, norm_w):
    b, L, nconv = xbc.shape
    Lc = xbcc.shape[1]
    nch = L // CHUNK
    per_b = lambda n, w: pl.BlockSpec((1, n, w), lambda i: (i, 0, 0))
    return pl.pallas_call(
        _ssd_kernel,
        grid=(b,),
        in_specs=[
            per_b(L, nconv), per_b(L, SSD_WIDTH), per_b(L, LANES), per_b(Lc, nconv), per_b(Lc, LANES),
            _const_spec(conv_w8.shape), _const_spec(conv_b.shape), _const_spec(dt_bias.shape),
            _const_spec(a_log.shape), _const_spec(d_skip.shape), _const_spec(norm_w.shape),
        ],
        out_specs=per_b(L, SSD_WIDTH),
        out_shape=jax.ShapeDtypeStruct((b, L, SSD_WIDTH), BF16),
        scratch_shapes=[
            pltpu.VMEM((L + 2 * CONV_HALO, nconv), F32),
            pltpu.VMEM((Lc + 2 * CONV_HALO, nconv), F32),
            pltpu.VMEM((L, nconv), BF16),
            pltpu.VMEM((Lc, nconv), BF16),
            pltpu.VMEM((L, LANES), F32), pltpu.VMEM((L, LANES), F32),
            pltpu.VMEM((Lc, LANES), F32), pltpu.VMEM((Lc, LANES), F32),
            pltpu.VMEM((nch, SSD_GROUPS, SSD_STATE, 2 * LANES), F32),
            pltpu.VMEM((nch, SSD_GROUPS, SSD_STATE, 2 * LANES), F32),
            pltpu.VMEM((L, LANES), F32), pltpu.VMEM((L, LANES), F32),
            pltpu.VMEM((nch * SUBLANES, 2 * SSD_WIDTH), F32),
            pltpu.VMEM((nch * 2 * SSD_HEADS, CHUNK), F32), pltpu.VMEM((nch * 2 * SSD_HEADS, CHUNK), F32),
            pltpu.VMEM((nch * 2 * SSD_HEADS, CHUNK), F32),
            pltpu.VMEM((SSD_HEADS * CHUNK, SSD_WIDTH), BF16),
        ],
        compiler_params=pltpu.CompilerParams(vmem_limit_bytes=VMEM_LIMIT_SSD),
        name="ssd",
    )(xbc, z, dt, xbcc, dtc, conv_w8, conv_b, dt_bias, a_log, d_skip, norm_w)


def _ret_kernel(q_ref, k_ref, v_ref, g_ref, kc_ref, vc_ref, df_ref, db_ref, gn_ref, y_ref, sf_scr):
    L = q_ref.shape[1]
    Lc = kc_ref.shape[1]
    nch = L // CHUNK
    dk = RET_DK
    row_i = lax.broadcasted_iota(jnp.int32, (CHUNK, dk), 0).astype(F32)
    col_i = lax.broadcasted_iota(jnp.int32, (CHUNK, dk), 1).astype(F32)
    rel = row_i - col_i
    crow = lax.broadcasted_iota(jnp.int32, (Lc, dk), 0).astype(F32)

    heads = []
    s_f0 = []
    s_b0 = []
    for h in range(RET_HEADS):
        cols = slice(h * dk, (h + 1) * dk)
        lg_f = -jnp.exp(df_ref[:, cols])
        lg_b = -jnp.exp(db_ref[:, cols])
        heads.append(dict(
            cols=cols,
            dmat=jnp.where(rel >= 0, jnp.exp(jnp.maximum(rel, 0.0) * lg_f), jnp.exp(jnp.maximum(-rel, 0.0) * lg_b)),
            dq_f=jnp.exp((row_i + 1.0) * lg_f),
            dq_b=jnp.exp((CHUNK - row_i) * lg_b),
            dk_f=jnp.exp((CHUNK - 1.0 - row_i) * lg_f),
            dk_b=jnp.exp(row_i * lg_b),
            dc_f=jnp.exp(CHUNK * lg_f),
            dc_b=jnp.exp(CHUNK * lg_b),
        ))
        kc = kc_ref[0, :, cols].astype(F32)
        vc = vc_ref[0, :, cols]
        s_f0.append(_dot_tn((kc * jnp.exp((Lc - 1.0 - crow) * lg_f)).astype(BF16), vc))
        s_b0.append(_dot_tn((kc * jnp.exp(crow * lg_b)).astype(BF16), vc))

    def fwd(c, s_f):
        base = pl.multiple_of(c * CHUNK, CHUNK)
        new = []
        for h, hd in enumerate(heads):
            sf_scr[c, h] = s_f[h]
            kk = k_ref[0, pl.ds(base, CHUNK), hd["cols"]].astype(F32)
            vv = v_ref[0, pl.ds(base, CHUNK), hd["cols"]]
            new.append(hd["dc_f"] * s_f[h] + _dot_tn((kk * hd["dk_f"]).astype(BF16), vv))
        return tuple(new)

    lax.fori_loop(0, nch, fwd, tuple(s_f0), unroll=RET_UNROLL)

    def bwd(i, s_bs):
        c = nch - 1 - i
        base = pl.multiple_of(c * CHUNK, CHUNK)
        new = []
        for h, hd in enumerate(heads):
            qq = q_ref[0, pl.ds(base, CHUNK), hd["cols"]]
            kk = k_ref[0, pl.ds(base, CHUNK), hd["cols"]]
            vv = v_ref[0, pl.ds(base, CHUNK), hd["cols"]]
            s_b = s_bs[h]
            scores = (_dot_nt(qq, kk) * hd["dmat"]).astype(BF16)
            y = (_dot(scores, vv)
                 + _dot(qq, sf_scr[c, h].astype(BF16)) * hd["dq_f"]
                 + _dot(qq, s_b.astype(BF16)) * hd["dq_b"])
            mu = jnp.mean(y, axis=-1, keepdims=True)
            yc = y - mu
            var = jnp.mean(yc * yc, axis=-1, keepdims=True)
            yn = (yc * lax.rsqrt(var + EPS)) * gn_ref[:, hd["cols"]]
            gate = _silu(g_ref[0, pl.ds(base, CHUNK), hd["cols"]].astype(F32))
            y_ref[0, pl.ds(base, CHUNK), hd["cols"]] = (yn * gate).astype(BF16)
            new.append(hd["dc_b"] * s_b + _dot_tn((kk.astype(F32) * hd["dk_b"]).astype(BF16), vv))
        return tuple(new)

    lax.fori_loop(0, nch, bwd, tuple(s_b0), unroll=RET_UNROLL)


def _retention(q, k, v, g, kc, vc, decay_f, decay_b, gn_w):
    b, L, w = q.shape
    Lc = kc.shape[1]
    nch = L // CHUNK
    per_b = lambda n: pl.BlockSpec((1, n, w), lambda i: (i, 0, 0))
    return pl.pallas_call(
        _ret_kernel,
        grid=(b,),
        in_specs=[per_b(L), per_b(L), per_b(L), per_b(L), per_b(Lc), per_b(Lc),
                  _const_spec((1, w)), _const_spec((1, w)), _const_spec((1, w))],
        out_specs=per_b(L),
        out_shape=jax.ShapeDtypeStruct((b, L, w), BF16),
        scratch_shapes=[
            pltpu.VMEM((nch, RET_HEADS, RET_DK, RET_DK), F32),
        ],
        compiler_params=pltpu.CompilerParams(vmem_limit_bytes=VMEM_LIMIT),
        name="retention",
    )(q, k, v, g, kc, vc, decay_f, decay_b, gn_w)


def _outproj_router_kernel(yr_ref, ys_ref, x_ref, g1_ref, sh2_ref, sc2_ref, npost_ref, npre_ref,
                           wor_ref, wos_ref, wr_ref, br_ref, tri_ref,
                           x1_ref, h2_ref, route_ref, cnt_ref, slots_ref,
                           wcat, carry):
    i = pl.program_id(0)

    @pl.when(i == 0)
    def _():
        wr = wr_ref[...]
        hi = wr.astype(BF16)
        wcat[:, :LANES] = hi
        wcat[:, LANES:] = (wr - hi.astype(F32)).astype(BF16)
        carry[...] = jnp.zeros_like(carry)

    tm = tri_ref.shape[0]
    running = carry[...]
    for sub in range(x_ref.shape[0] // tm):
        running = _route_subtile(sub, tm, running, yr_ref, ys_ref, x_ref, g1_ref, sh2_ref, sc2_ref, npost_ref,
                                 npre_ref, wor_ref, wos_ref, br_ref, tri_ref, x1_ref, h2_ref, route_ref,
                                 slots_ref, wcat)
    carry[...] = running
    cnt_ref[...] = running


def _route_subtile(sub, tm, running, yr_ref, ys_ref, x_ref, g1_ref, sh2_ref, sc2_ref, npost_ref, npre_ref,
                   wor_ref, wos_ref, br_ref, tri_ref, x1_ref, h2_ref, route_ref, slots_ref, wcat):
    rows = slice(sub * tm, (sub + 1) * tm)
    y = _dot(yr_ref[rows, :], wor_ref[...]) + _dot(ys_ref[rows, :], wos_ref[...])
    ms = jnp.mean(y * y, axis=-1, keepdims=True)
    x1 = x_ref[rows, :] + (y * lax.rsqrt(ms + EPS)) * (g1_ref[0] * npost_ref[...])
    x1_ref[rows, :] = x1
    h2 = _norm_mod(x1, npre_ref[...], sc2_ref[0], sh2_ref[0])
    h2_ref[rows, :] = h2

    h_hi = h2.astype(BF16)
    h_lo = (h2 - h_hi.astype(F32)).astype(BF16)
    both = _dot(h_hi, wcat[...])
    lg = both[:, :LANES] + both[:, LANES:] + _dot(h_lo, wcat[:, :LANES]) + br_ref[...]

    lane = lax.broadcasted_iota(jnp.int32, (tm, LANES), 1)
    lane_f = lane.astype(F32)
    is_grp = (lane >= N_EXPERTS) & (lane < N_EXPERTS + MOE_GROUPS)
    gl = jnp.where(is_grp, lg, NEG_BIG)
    mg = jnp.max(gl, axis=-1, keepdims=True)
    grp_lane = jnp.min(jnp.where(gl == mg, lane_f, 1e9), axis=-1, keepdims=True)
    p_g = 1.0 / jnp.sum(jnp.where(is_grp, jnp.exp(gl - mg), 0.0), axis=-1, keepdims=True)
    first = (grp_lane - N_EXPERTS) * EXPERTS_PER_GROUP
    in_grp = (lane_f >= first) & (lane_f < first + EXPERTS_PER_GROUP)
    el = jnp.where(in_grp, lg, NEG_BIG)
    t1 = jnp.max(el, axis=-1, keepdims=True)
    i1 = jnp.min(jnp.where(el == t1, lane_f, 1e9), axis=-1, keepdims=True)
    el2 = jnp.where(lane_f == i1, NEG_BIG, el)
    t2 = jnp.max(el2, axis=-1, keepdims=True)
    i2 = jnp.min(jnp.where(el2 == t2, lane_f, 1e9), axis=-1, keepdims=True)
    s = jnp.exp(t2 - t1)
    w1 = p_g / (1.0 + s)
    w2 = p_g * s / (1.0 + s)

    oh1 = (lane_f == i1)
    oh2 = (lane_f == i2)
    oh = (oh1 | oh2).astype(BF16)
    before = _dot(tri_ref[...], oh) + running
    rank1 = jnp.sum(jnp.where(oh1, before, 0.0), axis=-1, keepdims=True)
    rank2 = jnp.sum(jnp.where(oh2, before, 0.0), axis=-1, keepdims=True)

    route_ref[rows, :] = jnp.where(lane == 0, w1, jnp.where(lane == 1, w2, 0.0))

    row = lax.broadcasted_iota(jnp.int32, (tm, LANES), 0)
    on_diag = (row % LANES) == lane
    per = tm // LANES
    n_sub = x_ref.shape[0] // tm
    for qi, col in enumerate((i1, i2, rank1, rank2)):
        picked = jnp.where(on_diag, col, 0.0)
        dense = jnp.sum(picked.reshape(per, LANES, LANES), axis=1).astype(jnp.int32)
        slots_ref[0, (qi * n_sub + sub) * per:(qi * n_sub + sub + 1) * per, :] = dense
    return running + jnp.sum(oh.astype(F32), axis=0, keepdims=True)


def _outproj_router(yr, ys, x2, mod3, npost, npre, wo_r, wo_s, w_router, b_router, seq_len):
    T, d = x2.shape
    tm = TM_OUT
    per_seq = seq_len // tm
    rw = yr.shape[1]
    sub = tm
    tri = (jnp.arange(sub)[:, None] > jnp.arange(sub)[None, :]).astype(BF16)
    tok = lambda w: pl.BlockSpec((tm, w), lambda i: (i, 0))
    modv = lambda k: pl.BlockSpec((1, 1, d), lambda i: (i // per_seq, 0, k))
    return pl.pallas_call(
        _outproj_router_kernel,
        grid=(T // tm,),
        in_specs=[
            tok(rw), tok(rw), tok(d), modv(2), modv(3), modv(4),
            _const_spec((1, d)), _const_spec((1, d)),
            _const_spec(wo_r.shape), _const_spec(wo_s.shape), _const_spec(w_router.shape), _const_spec((1, LANES)),
            _const_spec((sub, sub)),
        ],
        out_specs=[tok(d), tok(d), tok(LANES), _const_spec((1, LANES)),
                   pl.BlockSpec((1, 2 * TOP_K * (tm // LANES), LANES), lambda i: (i, 0, 0))],
        out_shape=[jax.ShapeDtypeStruct((T, d), F32), jax.ShapeDtypeStruct((T, d), F32),
                   jax.ShapeDtypeStruct((T, LANES), F32), jax.ShapeDtypeStruct((1, LANES), F32),
                   jax.ShapeDtypeStruct((T // tm, 2 * TOP_K * (tm // LANES), LANES), jnp.int32)],
        scratch_shapes=[pltpu.VMEM((d, 2 * LANES), BF16), pltpu.VMEM((1, LANES), F32)],
        compiler_params=pltpu.CompilerParams(dimension_semantics=("arbitrary",),
                                             vmem_limit_bytes=VMEM_LIMIT),
        name="outproj_router",
    )(yr, ys, x2, mod3, mod3, mod3, npost, npre, wo_r, wo_s, w_router, b_router, tri)


def _slot_rows_kernel(start_ref, slots_ref, o_ref):
    half = slots_ref.shape[1] // 2
    expert = slots_ref[:, :half, :]
    row = slots_ref[:, half:, :]
    for e in range(N_EXPERTS):
        row = row + jnp.where(expert == e, start_ref[e], 0)
    for r in range(half):
        o_ref[:, 0, r * LANES:(r + 1) * LANES] = row[:, r, :]


def _slot_rows(pad_start, slots):
    nt, rows, _ = slots.shape
    grid_spec = pltpu.PrefetchScalarGridSpec(
        num_scalar_prefetch=1,
        grid=(1,),
        in_specs=[pl.BlockSpec(slots.shape, lambda i, ps: (0, 0, 0))],
        out_specs=pl.BlockSpec((nt, 1, rows // 2 * LANES), lambda i, ps: (0, 0, 0)),
    )
    return pl.pallas_call(
        _slot_rows_kernel,
        grid_spec=grid_spec,
        out_shape=jax.ShapeDtypeStruct((nt, 1, rows // 2 * LANES), jnp.int32),
        name="slot_rows",
    )(pad_start, slots)


def _dispatch_kernel(pad_end_ref, zero_from_ref, dest_ref, h_hbm, xs_hbm, zbuf, hbuf, sem, zsem, in_sem):
    i = pl.program_id(0)
    nt = pl.num_programs(0)
    td = hbuf.shape[1]
    mb = zbuf.shape[0]

    @pl.when(i == 0)
    def _():
        zbuf[...] = jnp.zeros_like(zbuf)

        def pieces(e, act):
            for p in range(mb // ZERO_PIECE):
                row = zero_from_ref[e] + p * ZERO_PIECE

                @pl.when(row < pad_end_ref[e])
                def _():
                    act(pltpu.make_async_copy(zbuf.at[pl.ds(0, ZERO_PIECE)],
                                              xs_hbm.at[pl.ds(pl.multiple_of(row, ZERO_PIECE), ZERO_PIECE)], zsem))

        def start(e, carry):
            pieces(e, lambda c: c.start())
            return carry

        def wait(e, carry):
            pieces(e, lambda c: c.wait())
            return carry

        def tail_fill(blk):
            return pltpu.make_async_copy(zbuf, xs_hbm.at[pl.ds(pl.multiple_of(blk * mb, mb), mb)], zsem)

        def tail_start(blk, carry):
            @pl.when(blk * mb >= pad_end_ref[N_EXPERTS - 1])
            def _():
                tail_fill(blk).start()
            return carry

        def tail_wait(blk, carry):
            @pl.when(blk * mb >= pad_end_ref[N_EXPERTS - 1])
            def _():
                tail_fill(blk).wait()
            return carry

        n_blocks = xs_hbm.shape[0] // mb
        lax.fori_loop(0, N_EXPERTS, start, 0)
        lax.fori_loop(0, n_blocks, tail_start, 0)
        lax.fori_loop(0, N_EXPERTS, wait, 0)
        lax.fori_loop(0, n_blocks, tail_wait, 0)

    def fetch(t):
        return pltpu.make_async_copy(h_hbm.at[pl.ds(pl.multiple_of(t * td, td), td)], hbuf.at[t % DISPATCH_RING],
                                     in_sem.at[t % DISPATCH_RING])

    @pl.when(i == 0)
    def _():
        fetch(i).start()

    @pl.when(i + 1 < nt)
    def _():
        fetch(i + 1).start()

    fetch(i).wait()
    for slot in range(DISPATCH_RING):
        @pl.when(i % DISPATCH_RING == slot)
        def _(slot=slot):
            _issue_row_copies(dest_ref, td, lambda kk, j, d: pltpu.make_async_copy(
                hbuf.at[slot, pl.ds(j, 1)], xs_hbm.at[pl.ds(d, 1)], sem.at[slot]))

    def drain(t):
        for kk in range(TOP_K):
            pltpu.make_async_copy(hbuf.at[t % DISPATCH_RING], xs_hbm.at[pl.ds(0, td)],
                                  sem.at[t % DISPATCH_RING]).wait()

    @pl.when(i > 0)
    def _():
        drain(i - 1)

    @pl.when(i == nt - 1)
    def _():
        drain(i)


def _issue_row_copies(dest_ref, n_tok, row_copy):
    def issue(j, carry):
        for kk in range(TOP_K):
            row_copy(kk, j, dest_ref[0, 0, kk * n_tok + j]).start()
        return carry

    lax.fori_loop(0, n_tok, issue, 0, unroll=DMA_UNROLL)


def _dispatch(pad_end, zero_from, dest, h2, cap):
    T, d = h2.shape
    td = TD_DISPATCH
    nt = T // td
    grid_spec = pltpu.PrefetchScalarGridSpec(
        num_scalar_prefetch=2,
        grid=(nt,),
        in_specs=[
            pl.BlockSpec((1, 1, TOP_K * td), lambda i, pe, zf: (i, 0, 0), memory_space=pltpu.SMEM),
            pl.BlockSpec(memory_space=pl.ANY),
        ],
        out_specs=pl.BlockSpec(memory_space=pl.ANY),
        scratch_shapes=[pltpu.VMEM((MB_EXPERT, d), h2.dtype), pltpu.VMEM((DISPATCH_RING, td, d), h2.dtype),
                        pltpu.SemaphoreType.DMA((DISPATCH_RING,)), pltpu.SemaphoreType.DMA(()),
                        pltpu.SemaphoreType.DMA((DISPATCH_RING,))],
    )
    return pl.pallas_call(
        _dispatch_kernel,
        grid_spec=grid_spec,
        out_shape=jax.ShapeDtypeStruct((cap, d), h2.dtype),
        compiler_params=pltpu.CompilerParams(dimension_semantics=("arbitrary",)),
        name="dispatch",
    )(pad_end, zero_from, dest, h2)


def _expert_kernel(be_ref, first_ref, slot_ref, next_ref, nused_ref, xs_ref, wg_hbm, wu_hbm, wd_hbm, y_ref,
                   wg_f, wu_f, wd_f, wg_b, wu_b, wd_b, sem):
    i = pl.program_id(0)

    def fetch(e, s):
        return [pltpu.make_async_copy(src.at[e], dst.at[s], sem.at[s, k])
                for k, (src, dst) in enumerate(((wg_hbm, wg_f), (wu_hbm, wu_f), (wd_hbm, wd_f)))]

    @pl.when(i == 0)
    def _():
        for c in fetch(be_ref[0], slot_ref[0]):
            c.start()

    @pl.when(first_ref[i] == 1)
    def _():
        s = slot_ref[i]

        @pl.when(next_ref[i] >= 0)
        def _():
            for c in fetch(next_ref[i], 1 - s):
                c.start()

        for c in fetch(be_ref[i], s):
            c.wait()
        wg_b[...] = wg_f[s].astype(BF16)
        wu_b[...] = wu_f[s].astype(BF16)
        wd_b[...] = wd_f[s].astype(BF16)

    @pl.when(i < nused_ref[0])
    def _():
        xb = xs_ref[...].astype(BF16)
        hid = (_silu(_dot(xb, wg_b[...])) * _dot(xb, wu_b[...])).astype(BF16)
        y_ref[...] = _dot(hid, wd_b[...])

    @pl.when(i >= nused_ref[0])
    def _():
        y_ref[...] = jnp.zeros_like(y_ref)


def _expert_plan(padded, pad_end, n_blocks, mb):
    n_used = (pad_end[-1:] // mb).astype(jnp.int32)
    blk_start = jnp.arange(n_blocks, dtype=jnp.int32) * mb
    experts = jnp.arange(N_EXPERTS, dtype=jnp.int32)
    blk_expert = jnp.minimum(jnp.sum((pad_end[None, :] <= blk_start[:, None]).astype(jnp.int32), axis=1),
                             N_EXPERTS - 1)
    prev = jnp.concatenate([jnp.full((1,), -1, jnp.int32), blk_expert[:-1]])
    first = ((blk_start < pad_end[-1]) & (blk_expert != prev)).astype(jnp.int32)
    slot = (jnp.cumsum(first) - 1) % 2
    later = jnp.where((padded > 0)[None, :] & (experts[None, :] > experts[:, None]), experts[None, :], N_EXPERTS)
    next_nonempty = jnp.min(later, axis=1)
    next_nonempty = jnp.where(next_nonempty == N_EXPERTS, -1, next_nonempty)
    nxt = jnp.sum(jnp.where(blk_expert[:, None] == experts[None, :], next_nonempty[None, :], 0), axis=1)
    return [a.astype(jnp.int32) for a in (blk_expert, first, slot, nxt, n_used)]


def _experts(plan, xs, w_gate, w_up, w_down):
    cap, dp = xs.shape
    n_exp, d, de = w_gate.shape
    assert dp == d
    mb = MB_EXPERT
    grid_spec = pltpu.PrefetchScalarGridSpec(
        num_scalar_prefetch=len(plan),
        grid=(cap // mb,),
        in_specs=[
            pl.BlockSpec((mb, dp), lambda i, be, fi, sl, nx, nu: (jnp.minimum(i, nu[0] - 1), 0)),
            pl.BlockSpec(memory_space=pl.ANY), pl.BlockSpec(memory_space=pl.ANY), pl.BlockSpec(memory_space=pl.ANY),
        ],
        out_specs=pl.BlockSpec((mb, dp), lambda i, be, fi, sl, nx, nu: (i, 0)),
        scratch_shapes=[pltpu.VMEM((2, d, de), F32), pltpu.VMEM((2, d, de), F32), pltpu.VMEM((2, de, d), F32),
                        pltpu.VMEM((d, de), BF16), pltpu.VMEM((d, de), BF16), pltpu.VMEM((de, d), BF16),
                        pltpu.SemaphoreType.DMA((2, 3))],
    )
    return pl.pallas_call(
        _expert_kernel,
        grid_spec=grid_spec,
        out_shape=jax.ShapeDtypeStruct((cap, dp), xs.dtype),
        compiler_params=pltpu.CompilerParams(dimension_semantics=("arbitrary",),
                                             vmem_limit_bytes=VMEM_LIMIT),
        name="experts",
    )(*plan, xs, w_gate, w_up, w_down)


def _combine_kernel(dest_ref, dest_next_ref, route_ref, x1_ref, g2_ref, nw_ref, yb_hbm, o_ref, buf, sem):
    i = pl.program_id(0)
    nt = pl.num_programs(0)
    tf = x1_ref.shape[0]
    stage = i % 2

    def gather(refs, st):
        _issue_row_copies(refs, tf, lambda kk, j, d: pltpu.make_async_copy(
            yb_hbm.at[pl.ds(d, 1)], buf.at[st, kk, pl.ds(j, 1)], sem.at[st]))

    @pl.when(i == 0)
    def _():
        gather(dest_ref, 0)

    for st in range(2):
        @pl.when((i + 1 < nt) & (stage != st))
        def _(st=st):
            gather(dest_next_ref, st)

    for kk in range(TOP_K):
        pltpu.make_async_copy(yb_hbm.at[pl.ds(0, tf)], buf.at[stage, kk], sem.at[stage]).wait()

    route = route_ref[...]
    out = route[:, 0:1] * buf[stage, 0] + route[:, 1:2] * buf[stage, 1]
    ms = jnp.mean(out * out, axis=-1, keepdims=True)
    o_ref[...] = x1_ref[...] + g2_ref[0] * ((out * lax.rsqrt(ms + EPS)) * nw_ref[...])


def _combine(dest, route, x1, mod3, nw, yb, seq_len):
    T, d = x1.shape
    tf = TF_COMBINE
    nt = T // tf
    per_seq = seq_len // tf
    slot_spec = functools.partial(pl.BlockSpec, (1, 1, TOP_K * tf), memory_space=pltpu.SMEM)
    return pl.pallas_call(
        _combine_kernel,
        grid=(nt,),
        in_specs=[
            slot_spec(lambda i: (i, 0, 0)),
            slot_spec(lambda i: (jnp.minimum(i + 1, nt - 1), 0, 0)),
            pl.BlockSpec((tf, LANES), lambda i: (i, 0)),
            pl.BlockSpec((tf, d), lambda i: (i, 0)),
            pl.BlockSpec((1, 1, d), lambda i: (i // per_seq, 0, 5)),
            _const_spec((1, d)),
            pl.BlockSpec(memory_space=pl.ANY),
        ],
        out_specs=pl.BlockSpec((tf, d), lambda i: (i, 0)),
        out_shape=jax.ShapeDtypeStruct((T, d), F32),
        scratch_shapes=[pltpu.VMEM((2, TOP_K, tf, yb.shape[1]), yb.dtype), pltpu.SemaphoreType.DMA((2,))],
        compiler_params=pltpu.CompilerParams(dimension_semantics=("arbitrary",),
                                             vmem_limit_bytes=VMEM_LIMIT),
        name="combine",
    )(dest, dest, route, x1, mod3, nw, yb)


def _rope_tables(L, n_heads):
    quarter = RET_DK // 4
    freqs = ROPE_BASE ** (-jnp.arange(quarter, dtype=F32) / quarter)
    t = jnp.arange(L)
    ang_r = (t // GRID_W).astype(F32)[:, None] * freqs
    ang_c = (t % GRID_W).astype(F32)[:, None] * freqs
    cos = jnp.concatenate([jnp.cos(ang_r)] * 2 + [jnp.cos(ang_c)] * 2, axis=-1)
    sin = jnp.concatenate([-jnp.sin(ang_r), jnp.sin(ang_r), -jnp.sin(ang_c), jnp.sin(ang_c)], axis=-1)
    return jnp.tile(cos, (1, n_heads)), jnp.tile(sin, (1, n_heads))


def _lane_pad(v, width=LANES):
    return jnp.pad(v, [(0, 0)] * (v.ndim - 1) + [(0, width - v.shape[-1])])


def kernel(x, c, ctx, c_ctx, w_mod, b_mod, norm_pre_mix, norm_post_mix, norm_pre_ffn, norm_post_ffn, w_in, w_out, ret_decay_f, ret_decay_b, ret_gn_w, ssd_conv_w, ssd_conv_b, ssd_dt_bias_f, ssd_dt_bias_b, ssd_a_log_f, ssd_a_log_b, ssd_d, ssd_norm_w, moe_w_rg, moe_b_rg, moe_w_re, moe_b_re, moe_w_gate, moe_w_up, moe_w_down):
    b, L, d = x.shape
    assert w_mod.shape[0] == 1, "single layer: context outputs are never needed"
    assert TM_OUT == TD_DISPATCH == TF_COMBINE, "router, dispatch and combine share one slot-row layout"
    rw = RET_HEADS * RET_DK
    nconv = SSD_WIDTH + 2 * SSD_GROUPS * SSD_STATE
    T = b * L

    mod_rows = -(-(b + 1) // SUBLANES) * SUBLANES
    c_all = jnp.zeros((mod_rows, d), F32).at[:b].set(c).at[b].set(c_ctx)
    mod3 = _modulation(c_all, w_mod[0], b_mod[0]).reshape(mod_rows, 1, 6 * d)

    wi = w_in[0]
    o = 0
    wq = wi[:, o:o + rw]; o += rw
    wk = wi[:, o:o + rw]; o += rw
    wv = wi[:, o:o + rw]; o += rw
    wg = wi[:, o:o + rw]; o += rw
    wz = wi[:, o:o + SSD_WIDTH]; o += SSD_WIDTH
    wxbc = wi[:, o:o + nconv].astype(BF16); o += nconv
    wdt = _lane_pad(wi[:, o:o + 2 * SSD_HEADS]).astype(BF16)
    wqk = jnp.concatenate([wq, wk], axis=1).astype(BF16)
    wvgz = jnp.concatenate([wv, wg, wz], axis=1).astype(BF16)
    cos_t, sin_t = _rope_tables(L, RET_HEADS)
    nw1 = norm_pre_mix[0].reshape(1, d)

    q, k, v, g, z, xbc, dt = _inproj(x, mod3, nw1, wqk, wvgz, wxbc, wdt, cos_t, sin_t)
    kc, vc, xbcc, dtc = _inproj_ctx(ctx, mod3, b, nw1, wk.astype(BF16), wv.astype(BF16), wxbc, wdt)

    conv_w8 = jnp.pad(ssd_conv_w[0], ((0, SUBLANES - SSD_CONV), (0, 0)))
    dt_bias = _lane_pad(jnp.concatenate([ssd_dt_bias_f[0], ssd_dt_bias_b[0]])[None, :])
    a_log = _lane_pad(jnp.concatenate([ssd_a_log_f[0], ssd_a_log_b[0]])[None, :])
    d_skip = jnp.repeat(ssd_d[0], SSD_HEADDIM)[None, :]
    ys = _ssd(xbc, z, dt, xbcc, dtc, conv_w8, ssd_conv_b[0][None, :], dt_bias, a_log, d_skip,
              ssd_norm_w[0][None, :])

    yr = _retention(q, k, v, g, kc, vc,
                    jnp.repeat(ret_decay_f[0], RET_DK)[None, :], jnp.repeat(ret_decay_b[0], RET_DK)[None, :],
                    ret_gn_w[0][None, :])

    wo = w_out[0].astype(BF16)
    w_router = _lane_pad(jnp.concatenate(
        [jnp.transpose(moe_w_re[0], (1, 0, 2)).reshape(d, N_EXPERTS), moe_w_rg[0]], axis=1))
    b_router = _lane_pad(jnp.concatenate([moe_b_re[0].reshape(-1), moe_b_rg[0]])[None, :])
    x1, h2, route, counts, slots = _outproj_router(
        yr.reshape(T, rw), ys.reshape(T, SSD_WIDTH), x.reshape(T, d), mod3,
        norm_post_mix[0][None, :], norm_pre_ffn[0][None, :], wo[:rw], wo[rw:], w_router, b_router, L)

    mb = MB_EXPERT
    n_blocks = -(-(T * TOP_K + N_EXPERTS * (mb - 1)) // mb)
    cap = n_blocks * mb
    cnt = counts[0, :N_EXPERTS].astype(jnp.int32)
    padded = (cnt + mb - 1) // mb * mb
    pad_end = jnp.cumsum(padded)
    pad_start = pad_end - padded

    dest = _slot_rows(pad_start.astype(jnp.int32), slots)
    zero_from = pad_start + cnt // ZERO_PIECE * ZERO_PIECE
    xs = _dispatch(pad_end.astype(jnp.int32), zero_from.astype(jnp.int32), dest, h2, cap)
    yb = _experts(_expert_plan(padded, pad_end, n_blocks, mb), xs, moe_w_gate[0], moe_w_up[0], moe_w_down[0])
    out = _combine(dest, route, x1, mod3, norm_post_ffn[0][None, :], yb, L)
    return out.reshape(b, L, d)
```

```python
import functools

import jax
import jax.numpy as jnp
from jax import lax
from jax.experimental import pallas as pl
from jax.experimental.pallas import tpu as pltpu

F32 = jnp.float32
BF16 = jnp.bfloat16

LANES = 128
SUBLANES = 8
V7X_VMEM_BYTES = 64 * 1024 * 1024
VMEM_LIMIT = V7X_VMEM_BYTES * 3 // 4
VMEM_LIMIT_SSD = V7X_VMEM_BYTES * 7 // 8

EPS = 1e-6
CHUNK = 128
GRID_W = 64
RET_HEADS = 4
RET_DK = 128
ROPE_BASE = 10000.0
SSD_HEADS = 8
SSD_HEADDIM = 64
SSD_GROUPS = 2
SSD_STATE = 128
SSD_WIDTH = SSD_HEADS * SSD_HEADDIM
SSD_CONV = 5
SSD_PAIRS = SSD_WIDTH // LANES
MOE_GROUPS = 4
EXPERTS_PER_GROUP = 8
N_EXPERTS = MOE_GROUPS * EXPERTS_PER_GROUP
TOP_K = 2
CONV_HALO = SUBLANES

TM_PROJ = 512
TM_OUT = 512
TD_DISPATCH = TM_OUT
MB_EXPERT = 512
ZERO_PIECE = 64
TF_COMBINE = TM_OUT
DMA_UNROLL = 8
DISPATCH_RING = 3
RET_UNROLL = 2
NEG_BIG = -1e30


def _silu(v):
    return v * jax.nn.sigmoid(v)


def _dot(a, b):
    return jnp.dot(a, b, preferred_element_type=F32)


def _dot_tn(a, b):
    return lax.dot_general(a, b, (((0,), (0,)), ((), ())), preferred_element_type=F32)


def _dot_nt(a, b):
    return lax.dot_general(a, b, (((1,), (1,)), ((), ())), preferred_element_type=F32)


def _mod_kernel(c_ref, w_ref, b_ref, o_ref):
    a = _silu(c_ref[...])
    w = w_ref[...]
    a_hi = a.astype(BF16)
    a_lo = (a - a_hi.astype(F32)).astype(BF16)
    w_hi = w.astype(BF16)
    w_lo = (w - w_hi.astype(F32)).astype(BF16)
    o_ref[...] = _dot(a_hi, w_hi) + _dot(a_lo, w_hi) + _dot(a_hi, w_lo) + b_ref[...]


def _modulation(c_all, w_mod, b_mod):
    rows, d = c_all.shape
    n = w_mod.shape[1]
    return pl.pallas_call(
        _mod_kernel,
        grid=(n // d,),
        in_specs=[
            pl.BlockSpec((rows, d), lambda j: (0, 0)),
            pl.BlockSpec((d, d), lambda j: (0, j)),
            pl.BlockSpec((1, d), lambda j: (0, j)),
        ],
        out_specs=pl.BlockSpec((rows, d), lambda j: (0, j)),
        out_shape=jax.ShapeDtypeStruct((rows, n), F32),
        name="modulation",
    )(c_all, w_mod, b_mod.reshape(1, n))


def _norm_mod(x, nw, sc, sh):
    ms = jnp.mean(x * x, axis=-1, keepdims=True)
    return (x * lax.rsqrt(ms + EPS)) * (nw * (1.0 + sc)) + sh


def _rope(t, cos, sin_signed, first_half):
    width = t.shape[-1]
    quarter = RET_DK // 4
    swapped = jnp.where(first_half, pltpu.roll(t, width - quarter, 1), pltpu.roll(t, quarter, 1))
    return t * cos + swapped * sin_signed


def _inproj_kernel(x_ref, sh_ref, sc_ref, nw_ref, wqk_ref, wvgz_ref, wxbc_ref, wdt_ref, cos_ref, sin_ref,
                   q_ref, k_ref, v_ref, g_ref, z_ref, xbc_ref, dt_ref):
    hb = _norm_mod(x_ref[0], nw_ref[...], sc_ref[0], sh_ref[0]).astype(BF16)
    rw = q_ref.shape[-1]
    qk = _dot(hb, wqk_ref[...])
    cos = cos_ref[...]
    sin = sin_ref[...]
    lane = lax.broadcasted_iota(jnp.int32, cos.shape, 1)
    first_half = (lane % (RET_DK // 2)) < (RET_DK // 4)
    q_ref[0] = _rope(qk[:, :rw], cos, sin, first_half).astype(BF16)
    k_ref[0] = (_rope(qk[:, rw:], cos, sin, first_half) * (RET_DK ** -0.5)).astype(BF16)
    vgz = _dot(hb, wvgz_ref[...])
    v_ref[0] = vgz[:, :rw].astype(BF16)
    g_ref[0] = vgz[:, rw:2 * rw].astype(BF16)
    z_ref[0] = vgz[:, 2 * rw:].astype(BF16)
    xbc_ref[0] = _dot(hb, wxbc_ref[...]).astype(BF16)
    dt_ref[0] = _dot(hb, wdt_ref[...])


def _inproj_ctx_kernel(x_ref, sh_ref, sc_ref, nw_ref, wk_ref, wv_ref, wxbc_ref, wdt_ref,
                       k_ref, v_ref, xbc_ref, dt_ref):
    hb = _norm_mod(x_ref[0], nw_ref[...], sc_ref[0], sh_ref[0]).astype(BF16)
    k_ref[0] = (_dot(hb, wk_ref[...]) * (RET_DK ** -0.5)).astype(BF16)
    v_ref[0] = _dot(hb, wv_ref[...]).astype(BF16)
    xbc_ref[0] = _dot(hb, wxbc_ref[...]).astype(BF16)
    dt_ref[0] = _dot(hb, wdt_ref[...])


def _const_spec(shape):
    nd = len(shape)
    return pl.BlockSpec(shape, lambda *_: (0,) * nd)


def _inproj(x, mod3, nw, wqk, wvgz, wxbc, wdt, cos_t, sin_t):
    b, L, d = x.shape
    tm = min(TM_PROJ, L)
    rw = wqk.shape[1] // 2
    tok = lambda w: pl.BlockSpec((1, tm, w), lambda i, j: (i, j, 0))
    out_bf = lambda w: jax.ShapeDtypeStruct((b, L, w), BF16)
    return pl.pallas_call(
        _inproj_kernel,
        grid=(b, L // tm),
        in_specs=[
            tok(d),
            pl.BlockSpec((1, 1, d), lambda i, j: (i, 0, 0)),
            pl.BlockSpec((1, 1, d), lambda i, j: (i, 0, 1)),
            _const_spec((1, d)),
            _const_spec(wqk.shape), _const_spec(wvgz.shape), _const_spec(wxbc.shape), _const_spec(wdt.shape),
            pl.BlockSpec((tm, rw), lambda i, j: (j, 0)),
            pl.BlockSpec((tm, rw), lambda i, j: (j, 0)),
        ],
        out_specs=[tok(rw), tok(rw), tok(rw), tok(rw), tok(rw), tok(wxbc.shape[1]), tok(LANES)],
        out_shape=[out_bf(rw), out_bf(rw), out_bf(rw), out_bf(rw), out_bf(rw), out_bf(wxbc.shape[1]),
                   jax.ShapeDtypeStruct((b, L, LANES), F32)],
        compiler_params=pltpu.CompilerParams(vmem_limit_bytes=VMEM_LIMIT),
        name="inproj",
    )(x, mod3, mod3, nw, wqk, wvgz, wxbc, wdt, cos_t, sin_t)


def _inproj_ctx(ctx, mod3, ctx_row, nw, wk, wv, wxbc, wdt):
    b, L, d = ctx.shape
    tm = min(TM_PROJ, L)
    rw = wk.shape[1]
    tok = lambda w: pl.BlockSpec((1, tm, w), lambda i, j: (i, j, 0))
    out_bf = lambda w: jax.ShapeDtypeStruct((b, L, w), BF16)
    return pl.pallas_call(
        _inproj_ctx_kernel,
        grid=(b, L // tm),
        in_specs=[
            tok(d),
            pl.BlockSpec((1, 1, d), lambda i, j: (ctx_row, 0, 0)),
            pl.BlockSpec((1, 1, d), lambda i, j: (ctx_row, 0, 1)),
            _const_spec((1, d)),
            _const_spec(wk.shape), _const_spec(wv.shape), _const_spec(wxbc.shape), _const_spec(wdt.shape),
        ],
        out_specs=[tok(rw), tok(rw), tok(wxbc.shape[1]), tok(LANES)],
        out_shape=[out_bf(rw), out_bf(rw), out_bf(wxbc.shape[1]), jax.ShapeDtypeStruct((b, L, LANES), F32)],
        compiler_params=pltpu.CompilerParams(vmem_limit_bytes=VMEM_LIMIT),
        name="inproj_ctx",
    )(ctx, mod3, mod3, nw, wk, wv, wxbc, wdt)


def _ssd_kernel(xbc_ref, z_ref, dt_ref, xbcc_ref, dtc_ref, cw_ref, cb_ref, dtb_ref, alog_ref, dsk_ref, nw_ref,
                y_ref,
                xpad, xpadc, u, uc, dtv, dav, dtcv, dacv, sf_scr, kb_scr, acum, ecum, dec_scr,
                arow_scr, erow_scr, dtrow_scr):
    L = xbc_ref.shape[1]
    Lc = xbcc_ref.shape[1]
    nch = L // CHUNK
    nchc = Lc // CHUNK
    win = CHUNK + 2 * CONV_HALO
    nconv = xbc_ref.shape[2]
    nh = SSD_HEADS

    def conv_pass(src_ref, pad_ref, dst_ref, n_chunks, length):
        zeros = jnp.zeros((CONV_HALO, nconv), F32)
        pad_ref[0:CONV_HALO, :] = zeros
        pad_ref[CONV_HALO + length:2 * CONV_HALO + length, :] = zeros
        pad_ref[CONV_HALO:CONV_HALO + length, :] = src_ref[0].astype(F32)

        def chunk(c, carry):
            base = pl.multiple_of(c * CHUNK, CHUNK)
            for cb_i in range(nconv // LANES):
                cols = slice(cb_i * LANES, (cb_i + 1) * LANES)
                w = pad_ref[pl.ds(base, win), cols]
                acc = cb_ref[:, cols] + w[CONV_HALO:CONV_HALO + CHUNK] * cw_ref[SSD_CONV // 2:SSD_CONV // 2 + 1, cols]
                for j in range(SSD_CONV):
                    if j == SSD_CONV // 2:
                        continue
                    shifted = pltpu.roll(w, (SSD_CONV // 2 - j) % win, 0)
                    acc = acc + shifted[CONV_HALO:CONV_HALO + CHUNK] * cw_ref[j:j + 1, cols]
                dst_ref[pl.ds(base, CHUNK), cols] = _silu(acc).astype(BF16)
            return carry

        lax.fori_loop(0, n_chunks, chunk, 0)

    conv_pass(xbcc_ref, xpadc, uc, nchc, Lc)
    conv_pass(xbc_ref, xpad, u, nch, L)

    a_neg = -jnp.exp(alog_ref[...])
    dtv[...] = jax.nn.softplus(dt_ref[0] + dtb_ref[...])
    dav[...] = dtv[...] * a_neg
    dtcv[...] = jax.nn.softplus(dtc_ref[0] + dtb_ref[...])
    dacv[...] = dtcv[...] * a_neg

    row_i = lax.broadcasted_iota(jnp.int32, (CHUNK, CHUNK), 0)
    col_i = lax.broadcasted_iota(jnp.int32, (CHUNK, CHUNK), 1)
    causal = col_i <= row_i
    lo_half = col_i < SSD_HEADDIM
    fwd_lane = col_i < nh
    head_of = lax.broadcasted_iota(jnp.int32, (CHUNK, SSD_WIDTH), 1) // SSD_HEADDIM
    src_col = lax.broadcasted_iota(jnp.int32, (CHUNK, SSD_WIDTH), 0)
    exp_f = (head_of == src_col).astype(BF16)
    exp_b = (head_of == src_col - nh).astype(BF16)
    exp_fb = jnp.concatenate([exp_f, exp_b], axis=1)

    def split3(v):
        hi = v.astype(BF16)
        r1 = v - hi.astype(F32)
        mid = r1.astype(BF16)
        return hi, mid, (r1 - mid.astype(F32)).astype(BF16)

    def times_onehot(v, m, passes=3):
        parts = split3(v)[:passes]
        acc = _dot(parts[0], m)
        for part in parts[1:]:
            acc = acc + _dot(part, m)
        return acc

    def colb(mat, r):
        return jnp.broadcast_to(mat[:, r:r + 1], (CHUNK, CHUNK))

    def pair_sel(a, b_):
        return jnp.where(lo_half, a, b_)

    gw = 2 * LANES

    def chunk_terms(u_ref, dt_s, da_s, base):
        dt = dt_s[pl.ds(base, CHUNK), :]
        da = da_s[pl.ds(base, CHUNK), :]
        acol = da
        for step in (1, 2, 4, 8, 16, 32, 64):
            acol = acol + jnp.where(row_i >= step, pltpu.roll(acol, step, 0), 0.0)
        ecol = acol - da
        last = acol[CHUNK - 1:CHUNK, :]
        wgt = jnp.where(fwd_lane, jnp.exp(last - acol), jnp.exp(ecol)) * dt
        scale = jnp.where(fwd_lane, jnp.exp(acol), jnp.exp(last - ecol))
        wide = times_onehot(jnp.concatenate([wgt, scale], axis=0), exp_fb, passes=1)
        dec = times_onehot(jnp.broadcast_to(jnp.exp(last), (SUBLANES, LANES)), exp_fb)[0:1]
        xs = u_ref[pl.ds(base, CHUNK), 0:SSD_WIDTH].astype(F32)
        kmats = []
        for g in range(SSD_GROUPS):
            xw = jnp.concatenate([xs[:, g * gw:(g + 1) * gw] * wide[:CHUNK, g * gw:(g + 1) * gw],
                                  xs[:, g * gw:(g + 1) * gw] * wide[:CHUNK, SSD_WIDTH + g * gw:SSD_WIDTH + (g + 1) * gw]],
                                 axis=1).astype(BF16)
            bm = u_ref[pl.ds(base, CHUNK), SSD_WIDTH + g * SSD_STATE:SSD_WIDTH + (g + 1) * SSD_STATE]
            kmats.append(_dot_tn(bm, xw))
        return dt, acol, ecol, wide[CHUNK:], dec, kmats

    def advance(s, dec, kmats, backward):
        off = SSD_WIDTH if backward else 0
        koff = gw if backward else 0
        return [dec[:, off + g * gw:off + (g + 1) * gw] * s[g] + kmats[g][:, koff:koff + gw]
                for g in range(SSD_GROUPS)]

    ctx_terms = [chunk_terms(uc, dtcv, dacv, c * CHUNK) for c in range(nchc)]
    s_f0 = [jnp.zeros((SSD_STATE, gw), F32) for _ in range(SSD_GROUPS)]
    for c in range(nchc):
        s_f0 = advance(s_f0, ctx_terms[c][4], ctx_terms[c][5], False)
    s_b0 = [jnp.zeros((SSD_STATE, gw), F32) for _ in range(SSD_GROUPS)]
    for c in reversed(range(nchc)):
        s_b0 = advance(s_b0, ctx_terms[c][4], ctx_terms[c][5], True)

    def prep(c, carry):
        base = pl.multiple_of(c * CHUNK, CHUNK)
        dt, acol, ecol, scale, dec, kmats = chunk_terms(u, dtv, dav, base)
        acum[pl.ds(base, CHUNK), :] = acol
        ecum[pl.ds(base, CHUNK), :] = ecol
        hrow = pl.ds(pl.multiple_of(c * 2 * nh, 2 * nh), 2 * nh)
        arow_scr[hrow, :] = acol.T[:2 * nh]
        erow_scr[hrow, :] = ecol.T[:2 * nh]
        dtrow_scr[hrow, :] = dt.T[:2 * nh]
        xpad[pl.ds(base, CHUNK), :] = scale
        dec_scr[pl.ds(pl.multiple_of(c * SUBLANES, SUBLANES), SUBLANES), :] = jnp.broadcast_to(dec, (SUBLANES, 2 * SSD_WIDTH))
        for g in range(SSD_GROUPS):
            sf_scr[c, g] = kmats[g][:, :gw]
            kb_scr[c, g] = kmats[g][:, gw:]
        return carry

    lax.fori_loop(0, nch, prep, 0)

    def chunk_dec(c):
        return dec_scr[pl.ds(pl.multiple_of(c * SUBLANES, SUBLANES), 1), :]

    def fwd(c, s_old):
        dec = chunk_dec(c)
        new = []
        for g in range(SSD_GROUPS):
            new.append(dec[:, g * gw:(g + 1) * gw] * s_old[g] + sf_scr[c, g])
            sf_scr[c, g] = s_old[g]
        return tuple(new)

    lax.fori_loop(0, nch, fwd, tuple(s_f0))

    def bwd(i, s_b):
        c = nch - 1 - i
        base = pl.multiple_of(c * CHUNK, CHUNK)
        acol = acum[pl.ds(base, CHUNK), :]
        ecol = ecum[pl.ds(base, CHUNK), :]
        hrow = pl.ds(pl.multiple_of(c * 2 * nh, 2 * nh), 2 * nh)
        arow = arow_scr[hrow, :]
        erow = erow_scr[hrow, :]
        dt_t = dtrow_scr[hrow, :]
        scale = xpad[pl.ds(base, CHUNK), :]
        ys = []
        for g in range(SSD_GROUPS):
            bm = u[pl.ds(base, CHUNK), SSD_WIDTH + g * SSD_STATE:SSD_WIDTH + (g + 1) * SSD_STATE]
            cm = u[pl.ds(base, CHUNK), SSD_WIDTH + (SSD_GROUPS + g) * SSD_STATE:SSD_WIDTH + (SSD_GROUPS + g + 1) * SSD_STATE]
            cbm = _dot_nt(cm, bm)
            cs_f = _dot(cm, sf_scr[c, g].astype(BF16))
            cs_b = _dot(cm, s_b[g].astype(BF16))
            for pp in range(SSD_PAIRS // SSD_GROUPS):
                p = g * (SSD_PAIRS // SSD_GROUPS) + pp
                xs_b = u[pl.ds(base, CHUNK), p * LANES:(p + 1) * LANES]
                y_h = []
                for hh in range(2):
                    r = 2 * p + hh
                    arg = jnp.where(causal, colb(acol, r) - arow[r:r + 1, :],
                                    erow[nh + r:nh + r + 1, :] - colb(ecol, nh + r))
                    coef = jnp.where(causal, dt_t[r:r + 1, :], dt_t[nh + r:nh + r + 1, :])
                    gm = (cbm * (jnp.exp(arg) * coef)).astype(BF16)
                    y_h.append(_dot(gm, xs_b))
                sl = slice(pp * LANES, (pp + 1) * LANES)
                wl = slice(p * LANES, (p + 1) * LANES)
                wlb = slice(SSD_WIDTH + p * LANES, SSD_WIDTH + (p + 1) * LANES)
                ys.append(pair_sel(y_h[0], y_h[1]) + cs_f[:, sl] * scale[:, wl] + cs_b[:, sl] * scale[:, wlb]
                          + dsk_ref[:, wl] * xs_b.astype(F32))
        y = jnp.concatenate(ys, axis=1)
        y = y * _silu(z_ref[0, pl.ds(base, CHUNK), :].astype(F32))
        ms = jnp.mean(y * y, axis=-1, keepdims=True)
        y_ref[0, pl.ds(base, CHUNK), :] = ((y * lax.rsqrt(ms + EPS)) * nw_ref[...]).astype(BF16)
        dec = chunk_dec(c)
        return tuple(dec[:, SSD_WIDTH + g * gw:SSD_WIDTH + (g + 1) * gw] * s_b[g] + kb_scr[c, g]
                     for g in range(SSD_GROUPS))

    lax.fori_loop(0, nch, bwd, tuple(s_b0))


def _ssd(xbc, z, dt, xbcc, dtc, conv_w8, conv_b, dt_bias, a_log, d_skip, norm_w):
    b, L, nconv = xbc.shape
    Lc = xbcc.shape[1]
    nch = L // CHUNK
    per_b = lambda n, w: pl.BlockSpec((1, n, w), lambda i: (i, 0, 0))
    return pl.pallas_call(
        _ssd_kernel,
        grid=(b,),
        in_specs=[
            per_b(L, nconv), per_b(L, SSD_WIDTH), per_b(L, LANES), per_b(Lc, nconv), per_b(Lc, LANES),
            _const_spec(conv_w8.shape), _const_spec(conv_b.shape), _const_spec(dt_bias.shape),
            _const_spec(a_log.shape), _const_spec(d_skip.shape), _const_spec(norm_w.shape),
        ],
        out_specs=per_b(L, SSD_WIDTH),
        out_shape=jax.ShapeDtypeStruct((b, L, SSD_WIDTH), BF16),
        scratch_shapes=[
            pltpu.VMEM((L + 2 * CONV_HALO, nconv), F32),
            pltpu.VMEM((Lc + 2 * CONV_HALO, nconv), F32),
            pltpu.VMEM((L, nconv), BF16),
            pltpu.VMEM((Lc, nconv), BF16),
            pltpu.VMEM((L, LANES), F32), pltpu.VMEM((L, LANES), F32),
            pltpu.VMEM((Lc, LANES), F32), pltpu.VMEM((Lc, LANES), F32),
            pltpu.VMEM((nch, SSD_GROUPS, SSD_STATE, 2 * LANES), F32),
            pltpu.VMEM((nch, SSD_GROUPS, SSD_STATE, 2 * LANES), F32),
            pltpu.VMEM((L, LANES), F32), pltpu.VMEM((L, LANES), F32),
            pltpu.VMEM((nch * SUBLANES, 2 * SSD_WIDTH), F32),
            pltpu.VMEM((nch * 2 * SSD_HEADS, CHUNK), F32), pltpu.VMEM((nch * 2 * SSD_HEADS, CHUNK), F32),
            pltpu.VMEM((nch * 2 * SSD_HEADS, CHUNK), F32),
        ],
        compiler_params=pltpu.CompilerParams(vmem_limit_bytes=VMEM_LIMIT_SSD),
        name="ssd",
    )(xbc, z, dt, xbcc, dtc, conv_w8, conv_b, dt_bias, a_log, d_skip, norm_w)


def _ret_kernel(q_ref, k_ref, v_ref, g_ref, kc_ref, vc_ref, df_ref, db_ref, gn_ref, y_ref, sf_scr):
    L = q_ref.shape[1]
    Lc = kc_ref.shape[1]
    nch = L // CHUNK
    dk = RET_DK
    row_i = lax.broadcasted_iota(jnp.int32, (CHUNK, dk), 0).astype(F32)
    col_i = lax.broadcasted_iota(jnp.int32, (CHUNK, dk), 1).astype(F32)
    rel = row_i - col_i
    crow = lax.broadcasted_iota(jnp.int32, (Lc, dk), 0).astype(F32)

    heads = []
    s_f0 = []
    s_b0 = []
    for h in range(RET_HEADS):
        cols = slice(h * dk, (h + 1) * dk)
        lg_f = -jnp.exp(df_ref[:, cols])
        lg_b = -jnp.exp(db_ref[:, cols])
        heads.append(dict(
            cols=cols,
            dmat=jnp.where(rel >= 0, jnp.exp(jnp.maximum(rel, 0.0) * lg_f), jnp.exp(jnp.maximum(-rel, 0.0) * lg_b)),
            dq_f=jnp.exp((row_i + 1.0) * lg_f),
            dq_b=jnp.exp((CHUNK - row_i) * lg_b),
            dk_f=jnp.exp((CHUNK - 1.0 - row_i) * lg_f),
            dk_b=jnp.exp(row_i * lg_b),
            dc_f=jnp.exp(CHUNK * lg_f),
            dc_b=jnp.exp(CHUNK * lg_b),
        ))
        kc = kc_ref[0, :, cols].astype(F32)
        vc = vc_ref[0, :, cols]
        s_f0.append(_dot_tn((kc * jnp.exp((Lc - 1.0 - crow) * lg_f)).astype(BF16), vc))
        s_b0.append(_dot_tn((kc * jnp.exp(crow * lg_b)).astype(BF16), vc))

    def fwd(c, s_f):
        base = pl.multiple_of(c * CHUNK, CHUNK)
        new = []
        for h, hd in enumerate(heads):
            sf_scr[c, h] = s_f[h]
            kk = k_ref[0, pl.ds(base, CHUNK), hd["cols"]].astype(F32)
            vv = v_ref[0, pl.ds(base, CHUNK), hd["cols"]]
            new.append(hd["dc_f"] * s_f[h] + _dot_tn((kk * hd["dk_f"]).astype(BF16), vv))
        return tuple(new)

    lax.fori_loop(0, nch, fwd, tuple(s_f0), unroll=RET_UNROLL)

    def bwd(i, s_bs):
        c = nch - 1 - i
        base = pl.multiple_of(c * CHUNK, CHUNK)
        new = []
        for h, hd in enumerate(heads):
            qq = q_ref[0, pl.ds(base, CHUNK), hd["cols"]]
            kk = k_ref[0, pl.ds(base, CHUNK), hd["cols"]]
            vv = v_ref[0, pl.ds(base, CHUNK), hd["cols"]]
            s_b = s_bs[h]
            scores = (_dot_nt(qq, kk) * hd["dmat"]).astype(BF16)
            y = (_dot(scores, vv)
                 + _dot(qq, sf_scr[c, h].astype(BF16)) * hd["dq_f"]
                 + _dot(qq, s_b.astype(BF16)) * hd["dq_b"])
            mu = jnp.mean(y, axis=-1, keepdims=True)
            yc = y - mu
            var = jnp.mean(yc * yc, axis=-1, keepdims=True)
            yn = (yc * lax.rsqrt(var + EPS)) * gn_ref[:, hd["cols"]]
            gate = _silu(g_ref[0, pl.ds(base, CHUNK), hd["cols"]].astype(F32))
            y_ref[0, pl.ds(base, CHUNK), hd["cols"]] = (yn * gate).astype(BF16)
            new.append(hd["dc_b"] * s_b + _dot_tn((kk.astype(F32) * hd["dk_b"]).astype(BF16), vv))
        return tuple(new)

    lax.fori_loop(0, nch, bwd, tuple(s_b0), unroll=RET_UNROLL)


def _retention(q, k, v, g, kc, vc, decay_f, decay_b, gn_w):
    b, L, w = q.shape
    Lc = kc.shape[1]
    nch = L // CHUNK
    per_b = lambda n: pl.BlockSpec((1, n, w), lambda i: (i, 0, 0))
    return pl.pallas_call(
        _ret_kernel,
        grid=(b,),
        in_specs=[per_b(L), per_b(L), per_b(L), per_b(L), per_b(Lc), per_b(Lc),
                  _const_spec((1, w)), _const_spec((1, w)), _const_spec((1, w))],
        out_specs=per_b(L),
        out_shape=jax.ShapeDtypeStruct((b, L, w), BF16),
        scratch_shapes=[
            pltpu.VMEM((nch, RET_HEADS, RET_DK, RET_DK), F32),
        ],
        compiler_params=pltpu.CompilerParams(vmem_limit_bytes=VMEM_LIMIT),
        name="retention",
    )(q, k, v, g, kc, vc, decay_f, decay_b, gn_w)


def _outproj_router_kernel(yr_ref, ys_ref, x_ref, g1_ref, sh2_ref, sc2_ref, npost_ref, npre_ref,
                           wor_ref, wos_ref, wr_ref, br_ref, tri_ref,
                           x1_ref, h2_ref, route_ref, slots_ref, seg_ref,
                           wcat):
    i = pl.program_id(0)

    @pl.when(i == 0)
    def _():
        wr = wr_ref[...]
        hi = wr.astype(BF16)
        wcat[:, :LANES] = hi
        wcat[:, LANES:] = (wr - hi.astype(F32)).astype(BF16)

    _route_tile(yr_ref, ys_ref, x_ref, g1_ref, sh2_ref, sc2_ref, npost_ref, npre_ref, wor_ref, wos_ref, br_ref,
                tri_ref, x1_ref, h2_ref, route_ref, slots_ref, seg_ref, wcat)


def _route_tile(yr_ref, ys_ref, x_ref, g1_ref, sh2_ref, sc2_ref, npost_ref, npre_ref, wor_ref, wos_ref, br_ref,
                tri_ref, x1_ref, h2_ref, route_ref, slots_ref, seg_ref, wcat):
    tm = x_ref.shape[0]
    rows = slice(0, tm)
    y = _dot(yr_ref[rows, :], wor_ref[...]) + _dot(ys_ref[rows, :], wos_ref[...])
    ms = jnp.mean(y * y, axis=-1, keepdims=True)
    x1 = x_ref[rows, :] + (y * lax.rsqrt(ms + EPS)) * (g1_ref[0] * npost_ref[...])
    x1_ref[rows, :] = x1
    h2 = _norm_mod(x1, npre_ref[...], sc2_ref[0], sh2_ref[0])
    h2_ref[rows, :] = h2

    h_hi = h2.astype(BF16)
    h_lo = (h2 - h_hi.astype(F32)).astype(BF16)
    both = _dot(h_hi, wcat[...])
    lg = both[:, :LANES] + both[:, LANES:] + _dot(h_lo, wcat[:, :LANES]) + br_ref[...]

    lane = lax.broadcasted_iota(jnp.int32, (tm, LANES), 1)
    lane_f = lane.astype(F32)
    is_grp = (lane >= N_EXPERTS) & (lane < N_EXPERTS + MOE_GROUPS)
    gl = jnp.where(is_grp, lg, NEG_BIG)
    mg = jnp.max(gl, axis=-1, keepdims=True)
    grp_lane = jnp.min(jnp.where(gl == mg, lane_f, 1e9), axis=-1, keepdims=True)
    p_g = 1.0 / jnp.sum(jnp.where(is_grp, jnp.exp(gl - mg), 0.0), axis=-1, keepdims=True)
    first = (grp_lane - N_EXPERTS) * EXPERTS_PER_GROUP
    in_grp = (lane_f >= first) & (lane_f < first + EXPERTS_PER_GROUP)
    el = jnp.where(in_grp, lg, NEG_BIG)
    t1 = jnp.max(el, axis=-1, keepdims=True)
    i1 = jnp.min(jnp.where(el == t1, lane_f, 1e9), axis=-1, keepdims=True)
    el2 = jnp.where(lane_f == i1, NEG_BIG, el)
    t2 = jnp.max(el2, axis=-1, keepdims=True)
    i2 = jnp.min(jnp.where(el2 == t2, lane_f, 1e9), axis=-1, keepdims=True)
    s = jnp.exp(t2 - t1)
    w1 = p_g / (1.0 + s)
    w2 = p_g * s / (1.0 + s)

    oh1 = (lane_f == i1)
    oh2 = (lane_f == i2)
    ohf = jnp.where(oh1 | oh2, 1.0, 0.0)
    before = _dot(tri_ref[...], ohf.astype(BF16))
    cnt = jnp.sum(ohf, axis=0, keepdims=True)
    seg = jnp.floor((cnt + (SUBLANES - 1.0)) * (1.0 / SUBLANES)) * SUBLANES
    e_row = lax.broadcasted_iota(jnp.int32, (LANES, LANES), 0)
    e_col = lax.broadcasted_iota(jnp.int32, (LANES, LANES), 1)
    earlier = (e_row < e_col).astype(BF16)
    seg_off = _dot(jnp.broadcast_to(seg, (SUBLANES, LANES)).astype(BF16), earlier)[0:1]
    where_to = before + seg_off
    lpos1 = jnp.sum(jnp.where(oh1, where_to, 0.0), axis=-1, keepdims=True)
    lpos2 = jnp.sum(jnp.where(oh2, where_to, 0.0), axis=-1, keepdims=True)

    cols = [w1, w2, lpos1, lpos2]
    for wk in (w1, w2):
        hi = wk.astype(BF16).astype(F32)
        cols += [hi, wk - hi]
    for lp in (lpos1, lpos2):
        major = jnp.floor(lp * (1.0 / SUBLANES))
        cols += [major, lp - major * SUBLANES]
    packed = jnp.zeros((tm, LANES), F32)
    for k, col in enumerate(cols):
        packed = jnp.where(lane == k, col, packed)
    route_ref[rows, :] = packed

    row = lax.broadcasted_iota(jnp.int32, (tm, LANES), 0)
    on_diag = (row % LANES) == lane
    per = tm // LANES
    for qi, col in enumerate((lpos1, lpos2)):
        picked = jnp.where(on_diag, col, 0.0)
        dense = jnp.sum(picked.reshape(per, LANES, LANES), axis=1).astype(jnp.int32)
        slots_ref[0, qi * per:(qi + 1) * per, :] = dense
    tbl_row = lax.broadcasted_iota(jnp.int32, (SUBLANES, LANES), 0)
    seg_ref[0] = jnp.where(tbl_row == 0, seg, jnp.where(tbl_row == 1, seg_off, 0.0)).astype(jnp.int32)


def _outproj_router(yr, ys, x2, mod3, npost, npre, wo_r, wo_s, w_router, b_router, seq_len):
    T, d = x2.shape
    tm = TM_OUT
    per_seq = seq_len // tm
    rw = yr.shape[1]
    tri = (jnp.arange(tm)[:, None] > jnp.arange(tm)[None, :]).astype(BF16)
    tok = lambda w: pl.BlockSpec((tm, w), lambda i: (i, 0))
    modv = lambda k: pl.BlockSpec((1, 1, d), lambda i: (i // per_seq, 0, k))
    tile3 = lambda r: pl.BlockSpec((1, r, LANES), lambda i: (i, 0, 0))
    slot_rows = TOP_K * (tm // LANES)
    return pl.pallas_call(
        _outproj_router_kernel,
        grid=(T // tm,),
        in_specs=[
            tok(rw), tok(rw), tok(d), modv(2), modv(3), modv(4),
            _const_spec((1, d)), _const_spec((1, d)),
            _const_spec(wo_r.shape), _const_spec(wo_s.shape), _const_spec(w_router.shape), _const_spec((1, LANES)),
            _const_spec((tm, tm)),
        ],
        out_specs=[tok(d), tok(d), tok(LANES), tile3(slot_rows), tile3(SUBLANES)],
        out_shape=[jax.ShapeDtypeStruct((T, d), F32), jax.ShapeDtypeStruct((T, d), F32),
                   jax.ShapeDtypeStruct((T, LANES), F32),
                   jax.ShapeDtypeStruct((T // tm, slot_rows, LANES), jnp.int32),
                   jax.ShapeDtypeStruct((T // tm, SUBLANES, LANES), jnp.int32)],
        scratch_shapes=[pltpu.VMEM((d, 2 * LANES), BF16)],
        compiler_params=pltpu.CompilerParams(dimension_semantics=("arbitrary",),
                                             vmem_limit_bytes=VMEM_LIMIT),
        name="outproj_router",
    )(yr, ys, x2, mod3, mod3, mod3, npost, npre, wo_r, wo_s, w_router, b_router, tri)


def _slot_rows_kernel(start_ref, slots_ref, o_ref):
    half = slots_ref.shape[1] // 2
    expert = slots_ref[:, :half, :]
    row = slots_ref[:, half:, :]
    for e in range(N_EXPERTS):
        row = row + jnp.where(expert == e, start_ref[e], 0)
    for r in range(half):
        o_ref[:, 0, r * LANES:(r + 1) * LANES] = row[:, r, :]


def _slot_rows(pad_start, slots):
    nt, rows, _ = slots.shape
    grid_spec = pltpu.PrefetchScalarGridSpec(
        num_scalar_prefetch=1,
        grid=(1,),
        in_specs=[pl.BlockSpec(slots.shape, lambda i, ps: (0, 0, 0))],
        out_specs=pl.BlockSpec((nt, 1, rows // 2 * LANES), lambda i, ps: (0, 0, 0)),
    )
    return pl.pallas_call(
        _slot_rows_kernel,
        grid_spec=grid_spec,
        out_shape=jax.ShapeDtypeStruct((nt, 1, rows // 2 * LANES), jnp.int32),
        name="slot_rows",
    )(pad_start, slots)


def _dispatch_kernel(pad_end_ref, zero_from_ref, dest_ref, h_hbm, xs_hbm, zbuf, hbuf, sem, zsem, in_sem):
    i = pl.program_id(0)
    nt = pl.num_programs(0)
    td = hbuf.shape[1]
    mb = zbuf.shape[0]

    @pl.when(i == 0)
    def _():
        zbuf[...] = jnp.zeros_like(zbuf)

        def pieces(e, act):
            for p in range(mb // ZERO_PIECE):
                row = zero_from_ref[e] + p * ZERO_PIECE

                @pl.when(row < pad_end_ref[e])
                def _():
                    act(pltpu.make_async_copy(zbuf.at[pl.ds(0, ZERO_PIECE)],
                                              xs_hbm.at[pl.ds(pl.multiple_of(row, ZERO_PIECE), ZERO_PIECE)], zsem))

        def start(e, carry):
            pieces(e, lambda c: c.start())
            return carry

        def wait(e, carry):
            pieces(e, lambda c: c.wait())
            return carry

        def tail_fill(blk):
            return pltpu.make_async_copy(zbuf, xs_hbm.at[pl.ds(pl.multiple_of(blk * mb, mb), mb)], zsem)

        def tail_start(blk, carry):
            @pl.when(blk * mb >= pad_end_ref[N_EXPERTS - 1])
            def _():
                tail_fill(blk).start()
            return carry

        def tail_wait(blk, carry):
            @pl.when(blk * mb >= pad_end_ref[N_EXPERTS - 1])
            def _():
                tail_fill(blk).wait()
            return carry

        n_blocks = xs_hbm.shape[0] // mb
        lax.fori_loop(0, N_EXPERTS, start, 0)
        lax.fori_loop(0, n_blocks, tail_start, 0)
        lax.fori_loop(0, N_EXPERTS, wait, 0)
        lax.fori_loop(0, n_blocks, tail_wait, 0)

    def fetch(t):
        return pltpu.make_async_copy(h_hbm.at[pl.ds(pl.multiple_of(t * td, td), td)], hbuf.at[t % DISPATCH_RING],
                                     in_sem.at[t % DISPATCH_RING])

    @pl.when(i == 0)
    def _():
        fetch(i).start()

    @pl.when(i + 1 < nt)
    def _():
        fetch(i + 1).start()

    fetch(i).wait()
    for slot in range(DISPATCH_RING):
        @pl.when(i % DISPATCH_RING == slot)
        def _(slot=slot):
            _issue_row_copies(dest_ref, td, lambda kk, j, d: pltpu.make_async_copy(
                hbuf.at[slot, pl.ds(j, 1)], xs_hbm.at[pl.ds(d, 1)], sem.at[slot]))

    def drain(t):
        for kk in range(TOP_K):
            pltpu.make_async_copy(hbuf.at[t % DISPATCH_RING], xs_hbm.at[pl.ds(0, td)],
                                  sem.at[t % DISPATCH_RING]).wait()

    @pl.when(i > 0)
    def _():
        drain(i - 1)

    @pl.when(i == nt - 1)
    def _():
        drain(i)


def _issue_row_copies(dest_ref, n_tok, row_copy):
    def issue(j, carry):
        for kk in range(TOP_K):
            row_copy(kk, j, dest_ref[0, 0, kk * n_tok + j]).start()
        return carry

    lax.fori_loop(0, n_tok, issue, 0, unroll=DMA_UNROLL)


def _dispatch(pad_end, zero_from, dest, h2, cap):
    T, d = h2.shape
    td = TD_DISPATCH
    nt = T // td
    grid_spec = pltpu.PrefetchScalarGridSpec(
        num_scalar_prefetch=2,
        grid=(nt,),
        in_specs=[
            pl.BlockSpec((1, 1, TOP_K * td), lambda i, pe, zf: (i, 0, 0), memory_space=pltpu.SMEM),
            pl.BlockSpec(memory_space=pl.ANY),
        ],
        out_specs=pl.BlockSpec(memory_space=pl.ANY),
        scratch_shapes=[pltpu.VMEM((MB_EXPERT, d), h2.dtype), pltpu.VMEM((DISPATCH_RING, td, d), h2.dtype),
                        pltpu.SemaphoreType.DMA((DISPATCH_RING,)), pltpu.SemaphoreType.DMA(()),
                        pltpu.SemaphoreType.DMA((DISPATCH_RING,))],
    )
    return pl.pallas_call(
        _dispatch_kernel,
        grid_spec=grid_spec,
        out_shape=jax.ShapeDtypeStruct((cap, d), h2.dtype),
        compiler_params=pltpu.CompilerParams(dimension_semantics=("arbitrary",)),
        name="dispatch",
    )(pad_end, zero_from, dest, h2)


def _expert_kernel(be_ref, first_ref, slot_ref, next_ref, nused_ref, xs_ref, wg_hbm, wu_hbm, wd_hbm, y_ref,
                   wg_f, wu_f, wd_f, wg_b, wu_b, wd_b, sem):
    i = pl.program_id(0)

    def fetch(e, s):
        return [pltpu.make_async_copy(src.at[e], dst.at[s], sem.at[s, k])
                for k, (src, dst) in enumerate(((wg_hbm, wg_f), (wu_hbm, wu_f), (wd_hbm, wd_f)))]

    @pl.when(i == 0)
    def _():
        for c in fetch(be_ref[0], slot_ref[0]):
            c.start()

    @pl.when(first_ref[i] == 1)
    def _():
        s = slot_ref[i]

        @pl.when(next_ref[i] >= 0)
        def _():
            for c in fetch(next_ref[i], 1 - s):
                c.start()

        for c in fetch(be_ref[i], s):
            c.wait()
        wg_b[...] = wg_f[s].astype(BF16)
        wu_b[...] = wu_f[s].astype(BF16)
        wd_b[...] = wd_f[s].astype(BF16)

    @pl.when(i < nused_ref[0])
    def _():
        d = y_ref.shape[1]
        xb = xs_ref[:, 0:d].astype(BF16)
        hid = (_silu(_dot(xb, wg_b[...])) * _dot(xb, wu_b[...])).astype(BF16)
        y_ref[...] = (_dot(hid, wd_b[...]) * xs_ref[:, d:d + 1]).astype(BF16).astype(F32)

    @pl.when(i >= nused_ref[0])
    def _():
        y_ref[...] = jnp.zeros_like(y_ref)


def _expert_plan(padded, pad_end, n_blocks, mb):
    n_used = (pad_end[-1:] // mb).astype(jnp.int32)
    blk_start = jnp.arange(n_blocks, dtype=jnp.int32) * mb
    experts = jnp.arange(N_EXPERTS, dtype=jnp.int32)
    blk_expert = jnp.minimum(jnp.sum((pad_end[None, :] <= blk_start[:, None]).astype(jnp.int32), axis=1),
                             N_EXPERTS - 1)
    prev = jnp.concatenate([jnp.full((1,), -1, jnp.int32), blk_expert[:-1]])
    first = ((blk_start < pad_end[-1]) & (blk_expert != prev)).astype(jnp.int32)
    slot = (jnp.cumsum(first) - 1) % 2
    later = jnp.where((padded > 0)[None, :] & (experts[None, :] > experts[:, None]), experts[None, :], N_EXPERTS)
    next_nonempty = jnp.min(later, axis=1)
    next_nonempty = jnp.where(next_nonempty == N_EXPERTS, -1, next_nonempty)
    nxt = jnp.sum(jnp.where(blk_expert[:, None] == experts[None, :], next_nonempty[None, :], 0), axis=1)
    return [a.astype(jnp.int32) for a in (blk_expert, first, slot, nxt, n_used)]


def _experts(plan, xs, w_gate, w_up, w_down):
    rows, dp = xs.shape
    n_exp, d, de = w_gate.shape
    assert dp == d + LANES
    mb = MB_EXPERT
    cap = rows // mb * mb
    grid_spec = pltpu.PrefetchScalarGridSpec(
        num_scalar_prefetch=len(plan),
        grid=(cap // mb,),
        in_specs=[
            pl.BlockSpec((mb, dp), lambda i, be, fi, sl, nx, nu: (jnp.minimum(i, nu[0] - 1), 0)),
            pl.BlockSpec(memory_space=pl.ANY), pl.BlockSpec(memory_space=pl.ANY), pl.BlockSpec(memory_space=pl.ANY),
        ],
        out_specs=pl.BlockSpec((mb, d), lambda i, be, fi, sl, nx, nu: (i, 0)),
        scratch_shapes=[pltpu.VMEM((2, d, de), F32), pltpu.VMEM((2, d, de), F32), pltpu.VMEM((2, de, d), F32),
                        pltpu.VMEM((d, de), BF16), pltpu.VMEM((d, de), BF16), pltpu.VMEM((de, d), BF16),
                        pltpu.SemaphoreType.DMA((2, 3))],
    )
    return pl.pallas_call(
        _expert_kernel,
        grid_spec=grid_spec,
        out_shape=jax.ShapeDtypeStruct((cap, d), F32),
        compiler_params=pltpu.CompilerParams(dimension_semantics=("arbitrary",),
                                             vmem_limit_bytes=VMEM_LIMIT),
        name="experts",
    )(*plan, xs, w_gate, w_up, w_down)


def _combine_kernel(dest_ref, dest_next_ref, route_ref, x1_ref, g2_ref, nw_ref, yb_hbm, o_ref, buf, sem):
    i = pl.program_id(0)
    nt = pl.num_programs(0)
    tf = x1_ref.shape[0]
    stage = i % 2

    def gather(refs, st):
        _issue_row_copies(refs, tf, lambda kk, j, d: pltpu.make_async_copy(
            yb_hbm.at[pl.ds(d, 1)], buf.at[st, kk, pl.ds(j, 1)], sem.at[st]))

    @pl.when(i == 0)
    def _():
        gather(dest_ref, 0)

    for st in range(2):
        @pl.when((i + 1 < nt) & (stage != st))
        def _(st=st):
            gather(dest_next_ref, st)

    for kk in range(TOP_K):
        pltpu.make_async_copy(yb_hbm.at[pl.ds(0, tf)], buf.at[stage, kk], sem.at[stage]).wait()

    route = route_ref[...]
    out = route[:, 0:1] * buf[stage, 0] + route[:, 1:2] * buf[stage, 1]
    ms = jnp.mean(out * out, axis=-1, keepdims=True)
    o_ref[...] = x1_ref[...] + g2_ref[0] * ((out * lax.rsqrt(ms + EPS)) * nw_ref[...])


def _combine(dest, route, x1, mod3, nw, yb, seq_len):
    T, d = x1.shape
    tf = TF_COMBINE
    nt = T // tf
    per_seq = seq_len // tf
    slot_spec = functools.partial(pl.BlockSpec, (1, 1, TOP_K * tf), memory_space=pltpu.SMEM)
    return pl.pallas_call(
        _combine_kernel,
        grid=(nt,),
        in_specs=[
            slot_spec(lambda i: (i, 0, 0)),
            slot_spec(lambda i: (jnp.minimum(i + 1, nt - 1), 0, 0)),
            pl.BlockSpec((tf, LANES), lambda i: (i, 0)),
            pl.BlockSpec((tf, d), lambda i: (i, 0)),
            pl.BlockSpec((1, 1, d), lambda i: (i // per_seq, 0, 5)),
            _const_spec((1, d)),
            pl.BlockSpec(memory_space=pl.ANY),
        ],
        out_specs=pl.BlockSpec((tf, d), lambda i: (i, 0)),
        out_shape=jax.ShapeDtypeStruct((T, d), F32),
        scratch_shapes=[pltpu.VMEM((2, TOP_K, tf, yb.shape[1]), yb.dtype), pltpu.SemaphoreType.DMA((2,))],
        compiler_params=pltpu.CompilerParams(dimension_semantics=("arbitrary",),
                                             vmem_limit_bytes=VMEM_LIMIT),
        name="combine",
    )(dest, dest, route, x1, mod3, nw, yb)


def _segment_pieces(tile, segrow_ref, seglen_ref, segoff_ref, act):
    def per_expert(e, carry):
        idx = tile * N_EXPERTS + e
        g0 = segrow_ref[idx]
        l0 = segoff_ref[idx]

        def piece(j, c2):
            act(pl.multiple_of(l0 + j * SUBLANES, SUBLANES), pl.multiple_of(g0 + j * SUBLANES, SUBLANES))
            return c2

        lax.fori_loop(0, seglen_ref[idx], piece, 0)
        return carry

    lax.fori_loop(0, N_EXPERTS, per_expert, 0)


def _stage_rows(tile_tokens):
    return TOP_K * tile_tokens + N_EXPERTS * SUBLANES


def _dispatch_seg_kernel(pad_end_ref, zero_from_ref, segrow_ref, seglen_ref, segoff_ref,
                         lpos_ref, route_ref, h_ref, xs_hbm, zbuf, stage, sem, zsem):
    i = pl.program_id(0)
    nt = pl.num_programs(0)
    td, d = h_ref.shape
    sr = stage.shape[1]
    per = td // LANES

    @pl.when(i == 0)
    def _():
        zbuf[...] = jnp.zeros_like(zbuf)

        def pieces(e, act):
            for p in range(MB_EXPERT // ZERO_PIECE):
                row = zero_from_ref[e] + p * ZERO_PIECE

                @pl.when(row < pad_end_ref[e])
                def _():
                    act(pltpu.make_async_copy(zbuf, xs_hbm.at[pl.ds(pl.multiple_of(row, SUBLANES), ZERO_PIECE)], zsem))

        def start(e, carry):
            pieces(e, lambda c: c.start())
            return carry

        def wait(e, carry):
            pieces(e, lambda c: c.wait())
            return carry

        lax.fori_loop(0, N_EXPERTS, start, 0)
        lax.fori_loop(0, N_EXPERTS, wait, 0)

    def moves(tile, slot, act):
        _segment_pieces(tile, segrow_ref, seglen_ref, segoff_ref, lambda lr, gr: act(pltpu.make_async_copy(
            stage.at[slot, pl.ds(lr, SUBLANES)], xs_hbm.at[pl.ds(gr, SUBLANES)], sem.at[slot])))

    @pl.when(i >= 2)
    def _():
        moves(i - 2, i % 2, lambda c: c.wait())

    srow = lax.broadcasted_iota(jnp.int32, (sr, LANES), 0)
    place = [[srow == lpos_ref[0, kk * per + cb:kk * per + cb + 1, :] for cb in range(per)] for kk in range(TOP_K)]
    onehot = jnp.concatenate([jnp.where(place[0][cb] | place[1][cb], 1.0, 0.0) for cb in range(per)],
                             axis=1).astype(BF16)
    staged = _dot(onehot, jnp.concatenate([h_ref[...].astype(BF16), route_ref[...].astype(BF16)], axis=1))
    side = staged[:, d:d + LANES]
    own_row = lax.broadcasted_iota(jnp.int32, (sr, 1), 0).astype(F32)
    is_first = (side[:, 8:9] * SUBLANES + side[:, 9:10]) == own_row
    w_rows = jnp.where(is_first, side[:, 4:5] + side[:, 5:6], side[:, 6:7] + side[:, 7:8])

    for slot in range(2):
        @pl.when(i % 2 == slot)
        def _(slot=slot):
            stage[slot, :, 0:d] = staged[:, 0:d]
            stage[slot, :, d:d + LANES] = jnp.broadcast_to(w_rows, (sr, LANES))
            moves(i, slot, lambda c: c.start())

    @pl.when(i == nt - 1)
    def _():
        @pl.when(i >= 1)
        def _():
            moves(i - 1, (i - 1) % 2, lambda c: c.wait())
        moves(i, i % 2, lambda c: c.wait())


def _dispatch_seg(plan, lpos, route, h2, cap):
    T, d = h2.shape
    td = TD_DISPATCH
    nt = T // td
    width = d + LANES
    sr = _stage_rows(td)
    n_pre = len(plan)
    grid_spec = pltpu.PrefetchScalarGridSpec(
        num_scalar_prefetch=n_pre,
        grid=(nt,),
        in_specs=[
            pl.BlockSpec((1, lpos.shape[1], LANES), lambda i, *_: (i, 0, 0)),
            pl.BlockSpec((td, LANES), lambda i, *_: (i, 0)),
            pl.BlockSpec((td, d), lambda i, *_: (i, 0)),
        ],
        out_specs=pl.BlockSpec(memory_space=pl.ANY),
        scratch_shapes=[pltpu.VMEM((ZERO_PIECE, width), F32), pltpu.VMEM((2, sr, width), F32),
                        pltpu.SemaphoreType.DMA((2,)), pltpu.SemaphoreType.DMA(())],
    )
    return pl.pallas_call(
        _dispatch_seg_kernel,
        grid_spec=grid_spec,
        out_shape=jax.ShapeDtypeStruct((cap, width), F32),
        compiler_params=pltpu.CompilerParams(dimension_semantics=("arbitrary",), vmem_limit_bytes=VMEM_LIMIT),
        name="dispatch",
    )(*plan, lpos, route, h2)


def _combine_seg_kernel(segrow_ref, seglen_ref, segoff_ref, route_ref, x1_ref, g2_ref, nw_ref, yb_hbm, o_ref,
                        stage, sem):
    i = pl.program_id(0)
    nt = pl.num_programs(0)
    tf = x1_ref.shape[0]
    sr = stage.shape[1]

    def fetches(tile, slot, act):
        _segment_pieces(tile, segrow_ref, seglen_ref, segoff_ref, lambda lr, gr: act(pltpu.make_async_copy(
            yb_hbm.at[pl.ds(gr, SUBLANES)], stage.at[slot, pl.ds(lr, SUBLANES)], sem.at[slot])))

    @pl.when(i == 0)
    def _():
        stage[...] = jnp.zeros_like(stage)
        fetches(i, 0, lambda c: c.start())

    for slot in range(2):
        @pl.when((i + 1 < nt) & (i % 2 != slot))
        def _(slot=slot):
            fetches(i + 1, slot, lambda c: c.start())

    fetches(i, i % 2, lambda c: c.wait())

    route = route_ref[...]
    scol = lax.broadcasted_iota(jnp.int32, (tf, sr), 1).astype(F32)
    pick = jnp.where((scol == route[:, 2:3]) | (scol == route[:, 3:4]), 1.0, 0.0).astype(BF16)
    out = _dot(pick, stage[i % 2].astype(BF16))
    ms = jnp.mean(out * out, axis=-1, keepdims=True)
    o_ref[...] = x1_ref[...] + g2_ref[0] * ((out * lax.rsqrt(ms + EPS)) * nw_ref[...])


def _combine_seg(seg_plan, route, x1, mod3, nw, yb, seq_len):
    T, d = x1.shape
    tf = TF_COMBINE
    nt = T // tf
    per_seq = seq_len // tf
    grid_spec = pltpu.PrefetchScalarGridSpec(
        num_scalar_prefetch=len(seg_plan),
        grid=(nt,),
        in_specs=[
            pl.BlockSpec((tf, LANES), lambda i, *_: (i, 0)),
            pl.BlockSpec((tf, d), lambda i, *_: (i, 0)),
            pl.BlockSpec((1, 1, d), lambda i, *_: (i // per_seq, 0, 5)),
            pl.BlockSpec((1, d), lambda i, *_: (0, 0)),
            pl.BlockSpec(memory_space=pl.ANY),
        ],
        out_specs=pl.BlockSpec((tf, d), lambda i, *_: (i, 0)),
        scratch_shapes=[pltpu.VMEM((2, _stage_rows(tf), d), F32), pltpu.SemaphoreType.DMA((2,))],
    )
    return pl.pallas_call(
        _combine_seg_kernel,
        grid_spec=grid_spec,
        out_shape=jax.ShapeDtypeStruct((T, d), F32),
        compiler_params=pltpu.CompilerParams(dimension_semantics=("arbitrary",), vmem_limit_bytes=VMEM_LIMIT),
        name="combine",
    )(*seg_plan, route, x1, mod3, nw, yb)


def _rope_tables(L, n_heads):
    quarter = RET_DK // 4
    freqs = ROPE_BASE ** (-jnp.arange(quarter, dtype=F32) / quarter)
    t = jnp.arange(L)
    ang_r = (t // GRID_W).astype(F32)[:, None] * freqs
    ang_c = (t % GRID_W).astype(F32)[:, None] * freqs
    cos = jnp.concatenate([jnp.cos(ang_r)] * 2 + [jnp.cos(ang_c)] * 2, axis=-1)
    sin = jnp.concatenate([-jnp.sin(ang_r), jnp.sin(ang_r), -jnp.sin(ang_c), jnp.sin(ang_c)], axis=-1)
    return jnp.tile(cos, (1, n_heads)), jnp.tile(sin, (1, n_heads))


def _lane_pad(v, width=LANES):
    return jnp.pad(v, [(0, 0)] * (v.ndim - 1) + [(0, width - v.shape[-1])])


def kernel(x, c, ctx, c_ctx, w_mod, b_mod, norm_pre_mix, norm_post_mix, norm_pre_ffn, norm_post_ffn, w_in, w_out, ret_decay_f, ret_decay_b, ret_gn_w, ssd_conv_w, ssd_conv_b, ssd_dt_bias_f, ssd_dt_bias_b, ssd_a_log_f, ssd_a_log_b, ssd_d, ssd_norm_w, moe_w_rg, moe_b_rg, moe_w_re, moe_b_re, moe_w_gate, moe_w_up, moe_w_down):
    b, L, d = x.shape
    assert w_mod.shape[0] == 1, "single layer: context outputs are never needed"
    assert TM_OUT == TD_DISPATCH == TF_COMBINE, "router, dispatch and combine share one slot-row layout"
    rw = RET_HEADS * RET_DK
    nconv = SSD_WIDTH + 2 * SSD_GROUPS * SSD_STATE
    T = b * L

    mod_rows = -(-(b + 1) // SUBLANES) * SUBLANES
    c_all = jnp.zeros((mod_rows, d), F32).at[:b].set(c).at[b].set(c_ctx)
    mod3 = _modulation(c_all, w_mod[0], b_mod[0]).reshape(mod_rows, 1, 6 * d)

    wi = w_in[0]
    o = 0
    wq = wi[:, o:o + rw]; o += rw
    wk = wi[:, o:o + rw]; o += rw
    wv = wi[:, o:o + rw]; o += rw
    wg = wi[:, o:o + rw]; o += rw
    wz = wi[:, o:o + SSD_WIDTH]; o += SSD_WIDTH
    wxbc = wi[:, o:o + nconv].astype(BF16); o += nconv
    wdt = _lane_pad(wi[:, o:o + 2 * SSD_HEADS]).astype(BF16)
    wqk = jnp.concatenate([wq, wk], axis=1).astype(BF16)
    wvgz = jnp.concatenate([wv, wg, wz], axis=1).astype(BF16)
    cos_t, sin_t = _rope_tables(L, RET_HEADS)
    nw1 = norm_pre_mix[0].reshape(1, d)

    q, k, v, g, z, xbc, dt = _inproj(x, mod3, nw1, wqk, wvgz, wxbc, wdt, cos_t, sin_t)
    kc, vc, xbcc, dtc = _inproj_ctx(ctx, mod3, b, nw1, wk.astype(BF16), wv.astype(BF16), wxbc, wdt)

    conv_w8 = jnp.pad(ssd_conv_w[0], ((0, SUBLANES - SSD_CONV), (0, 0)))
    dt_bias = _lane_pad(jnp.concatenate([ssd_dt_bias_f[0], ssd_dt_bias_b[0]])[None, :])
    a_log = _lane_pad(jnp.concatenate([ssd_a_log_f[0], ssd_a_log_b[0]])[None, :])
    d_skip = jnp.repeat(ssd_d[0], SSD_HEADDIM)[None, :]
    ys = _ssd(xbc, z, dt, xbcc, dtc, conv_w8, ssd_conv_b[0][None, :], dt_bias, a_log, d_skip,
              ssd_norm_w[0][None, :])

    yr = _retention(q, k, v, g, kc, vc,
                    jnp.repeat(ret_decay_f[0], RET_DK)[None, :], jnp.repeat(ret_decay_b[0], RET_DK)[None, :],
                    ret_gn_w[0][None, :])

    wo = w_out[0].astype(BF16)
    w_router = _lane_pad(jnp.concatenate(
        [jnp.transpose(moe_w_re[0], (1, 0, 2)).reshape(d, N_EXPERTS), moe_w_rg[0]], axis=1))
    b_router = _lane_pad(jnp.concatenate([moe_b_re[0].reshape(-1), moe_b_rg[0]])[None, :])
    x1, h2, route, lpos, seg = _outproj_router(
        yr.reshape(T, rw), ys.reshape(T, SSD_WIDTH), x.reshape(T, d), mod3,
        norm_post_mix[0][None, :], norm_pre_ffn[0][None, :], wo[:rw], wo[rw:], w_router, b_router, L)

    mb = MB_EXPERT
    nt = T // TM_OUT
    n_blocks = -(-(T * TOP_K + nt * N_EXPERTS * (SUBLANES - 1) + N_EXPERTS * (mb - 1)) // mb)
    seg_len = seg[:, 0, :N_EXPERTS]
    seg_off = seg[:, 1, :N_EXPERTS]
    used = jnp.sum(seg_len, axis=0)
    padded = (used + mb - 1) // mb * mb
    pad_end = jnp.cumsum(padded)
    pad_start = pad_end - padded
    seg_row = pad_start[None, :] + jnp.cumsum(seg_len, axis=0) - seg_len
    seg_plan = [a.reshape(-1).astype(jnp.int32) for a in (seg_row, seg_len // SUBLANES, seg_off)]
    zero_from = (pad_start + used).astype(jnp.int32)

    xs = _dispatch_seg([pad_end.astype(jnp.int32), zero_from] + seg_plan, lpos, route, h2,
                       n_blocks * mb + ZERO_PIECE)
    yb = _experts(_expert_plan(padded, pad_end, n_blocks, mb), xs, moe_w_gate[0], moe_w_up[0], moe_w_down[0])
    out = _combine_seg(seg_plan, route, x1, mod3, norm_post_ffn[0][None, :], yb, L)
    return out.reshape(b, L, d)
```

```python
import functools

import jax
import jax.numpy as jnp
from jax import lax
from jax.experimental import pallas as pl
from jax.experimental.pallas import tpu as pltpu

F32 = jnp.float32
BF16 = jnp.bfloat16

LANES = 128
SUBLANES = 8
V7X_VMEM_BYTES = 64 * 1024 * 1024
VMEM_LIMIT = V7X_VMEM_BYTES * 3 // 4
VMEM_LIMIT_SSD = V7X_VMEM_BYTES * 7 // 8

EPS = 1e-6
CHUNK = 128
GRID_W = 64
RET_HEADS = 4
RET_DK = 128
ROPE_BASE = 10000.0
SSD_HEADS = 8
SSD_HEADDIM = 64
SSD_GROUPS = 2
SSD_STATE = 128
SSD_WIDTH = SSD_HEADS * SSD_HEADDIM
SSD_CONV = 5
SSD_PAIRS = SSD_WIDTH // LANES
MOE_GROUPS = 4
EXPERTS_PER_GROUP = 8
N_EXPERTS = MOE_GROUPS * EXPERTS_PER_GROUP
TOP_K = 2
CONV_HALO = SUBLANES

TM_PROJ = 512
TM_OUT = 512
TD_DISPATCH = TM_OUT
MB_EXPERT = 512
ZERO_PIECE = 64
TF_COMBINE = TM_OUT
DMA_UNROLL = 8
DISPATCH_RING = 3
RET_UNROLL = 2
NEG_BIG = -1e30


def _silu(v):
    return v * jax.nn.sigmoid(v)


def _dot(a, b):
    return jnp.dot(a, b, preferred_element_type=F32)


def _dot_tn(a, b):
    return lax.dot_general(a, b, (((0,), (0,)), ((), ())), preferred_element_type=F32)


def _dot_nt(a, b):
    return lax.dot_general(a, b, (((1,), (1,)), ((), ())), preferred_element_type=F32)


def _mod_kernel(c_ref, w_ref, b_ref, o_ref):
    a = _silu(c_ref[...])
    w = w_ref[...]
    a_hi = a.astype(BF16)
    a_lo = (a - a_hi.astype(F32)).astype(BF16)
    w_hi = w.astype(BF16)
    w_lo = (w - w_hi.astype(F32)).astype(BF16)
    o_ref[...] = _dot(a_hi, w_hi) + _dot(a_lo, w_hi) + _dot(a_hi, w_lo) + b_ref[...]


def _modulation(c_all, w_mod, b_mod):
    rows, d = c_all.shape
    n = w_mod.shape[1]
    return pl.pallas_call(
        _mod_kernel,
        grid=(n // d,),
        in_specs=[
            pl.BlockSpec((rows, d), lambda j: (0, 0)),
            pl.BlockSpec((d, d), lambda j: (0, j)),
            pl.BlockSpec((1, d), lambda j: (0, j)),
        ],
        out_specs=pl.BlockSpec((rows, d), lambda j: (0, j)),
        out_shape=jax.ShapeDtypeStruct((rows, n), F32),
        name="modulation",
    )(c_all, w_mod, b_mod.reshape(1, n))


def _norm_mod(x, nw, sc, sh):
    ms = jnp.mean(x * x, axis=-1, keepdims=True)
    return (x * lax.rsqrt(ms + EPS)) * (nw * (1.0 + sc)) + sh


def _rope(t, cos, sin_signed, first_half):
    width = t.shape[-1]
    quarter = RET_DK // 4
    swapped = jnp.where(first_half, pltpu.roll(t, width - quarter, 1), pltpu.roll(t, quarter, 1))
    return t * cos + swapped * sin_signed


def _inproj_kernel(x_ref, sh_ref, sc_ref, nw_ref, wqk_ref, wvgz_ref, wxbc_ref, wdt_ref, cos_ref, sin_ref,
                   q_ref, k_ref, v_ref, g_ref, z_ref, xbc_ref, dt_ref):
    hb = _norm_mod(x_ref[0], nw_ref[...], sc_ref[0], sh_ref[0]).astype(BF16)
    rw = q_ref.shape[-1]
    qk = _dot(hb, wqk_ref[...])
    cos = cos_ref[...]
    sin = sin_ref[...]
    lane = lax.broadcasted_iota(jnp.int32, cos.shape, 1)
    first_half = (lane % (RET_DK // 2)) < (RET_DK // 4)
    q_ref[0] = _rope(qk[:, :rw], cos, sin, first_half).astype(BF16)
    k_ref[0] = (_rope(qk[:, rw:], cos, sin, first_half) * (RET_DK ** -0.5)).astype(BF16)
    vgz = _dot(hb, wvgz_ref[...])
    v_ref[0] = vgz[:, :rw].astype(BF16)
    g_ref[0] = vgz[:, rw:2 * rw].astype(BF16)
    z_ref[0] = vgz[:, 2 * rw:].astype(BF16)
    xbc_ref[0] = _dot(hb, wxbc_ref[...]).astype(BF16)
    dt_ref[0] = _dot(hb, wdt_ref[...])


def _inproj_ctx_kernel(x_ref, sh_ref, sc_ref, nw_ref, wk_ref, wv_ref, wxbc_ref, wdt_ref,
                       k_ref, v_ref, xbc_ref, dt_ref):
    hb = _norm_mod(x_ref[0], nw_ref[...], sc_ref[0], sh_ref[0]).astype(BF16)
    k_ref[0] = (_dot(hb, wk_ref[...]) * (RET_DK ** -0.5)).astype(BF16)
    v_ref[0] = _dot(hb, wv_ref[...]).astype(BF16)
    xbc_ref[0] = _dot(hb, wxbc_ref[...]).astype(BF16)
    dt_ref[0] = _dot(hb, wdt_ref[...])


def _const_spec(shape):
    nd = len(shape)
    return pl.BlockSpec(shape, lambda *_: (0,) * nd)


def _inproj(x, mod3, nw, wqk, wvgz, wxbc, wdt, cos_t, sin_t):
    b, L, d = x.shape
    tm = min(TM_PROJ, L)
    rw = wqk.shape[1] // 2
    tok = lambda w: pl.BlockSpec((1, tm, w), lambda i, j: (i, j, 0))
    out_bf = lambda w: jax.ShapeDtypeStruct((b, L, w), BF16)
    return pl.pallas_call(
        _inproj_kernel,
        grid=(b, L // tm),
        in_specs=[
            tok(d),
            pl.BlockSpec((1, 1, d), lambda i, j: (i, 0, 0)),
            pl.BlockSpec((1, 1, d), lambda i, j: (i, 0, 1)),
            _const_spec((1, d)),
            _const_spec(wqk.shape), _const_spec(wvgz.shape), _const_spec(wxbc.shape), _const_spec(wdt.shape),
            pl.BlockSpec((tm, rw), lambda i, j: (j, 0)),
            pl.BlockSpec((tm, rw), lambda i, j: (j, 0)),
        ],
        out_specs=[tok(rw), tok(rw), tok(rw), tok(rw), tok(rw), tok(wxbc.shape[1]), tok(LANES)],
        out_shape=[out_bf(rw), out_bf(rw), out_bf(rw), out_bf(rw), out_bf(rw), out_bf(wxbc.shape[1]),
                   jax.ShapeDtypeStruct((b, L, LANES), F32)],
        compiler_params=pltpu.CompilerParams(vmem_limit_bytes=VMEM_LIMIT),
        name="inproj",
    )(x, mod3, mod3, nw, wqk, wvgz, wxbc, wdt, cos_t, sin_t)


def _inproj_ctx(ctx, mod3, ctx_row, nw, wk, wv, wxbc, wdt):
    b, L, d = ctx.shape
    tm = min(TM_PROJ, L)
    rw = wk.shape[1]
    tok = lambda w: pl.BlockSpec((1, tm, w), lambda i, j: (i, j, 0))
    out_bf = lambda w: jax.ShapeDtypeStruct((b, L, w), BF16)
    return pl.pallas_call(
        _inproj_ctx_kernel,
        grid=(b, L // tm),
        in_specs=[
            tok(d),
            pl.BlockSpec((1, 1, d), lambda i, j: (ctx_row, 0, 0)),
            pl.BlockSpec((1, 1, d), lambda i, j: (ctx_row, 0, 1)),
            _const_spec((1, d)),
            _const_spec(wk.shape), _const_spec(wv.shape), _const_spec(wxbc.shape), _const_spec(wdt.shape),
        ],
        out_specs=[tok(rw), tok(rw), tok(wxbc.shape[1]), tok(LANES)],
        out_shape=[out_bf(rw), out_bf(rw), out_bf(wxbc.shape[1]), jax.ShapeDtypeStruct((b, L, LANES), F32)],
        compiler_params=pltpu.CompilerParams(vmem_limit_bytes=VMEM_LIMIT),
        name="inproj_ctx",
    )(ctx, mod3, mod3, nw, wk, wv, wxbc, wdt)


def _ssd_kernel(xbc_ref, z_ref, dt_ref, xbcc_ref, dtc_ref, cw_ref, cb_ref, dtb_ref, alog_ref, dsk_ref, nw_ref,
                y_ref,
                xpad, xpadc, u, uc, dtv, dav, dtcv, dacv, sf_scr, kb_scr, acum, ecum, dec_scr,
                arow_scr, erow_scr, dtrow_scr):
    L = xbc_ref.shape[1]
    Lc = xbcc_ref.shape[1]
    nch = L // CHUNK
    nchc = Lc // CHUNK
    win = CHUNK + 2 * CONV_HALO
    nconv = xbc_ref.shape[2]
    nh = SSD_HEADS

    def conv_pass(src_ref, pad_ref, dst_ref, n_chunks, length):
        zeros = jnp.zeros((CONV_HALO, nconv), F32)
        pad_ref[0:CONV_HALO, :] = zeros
        pad_ref[CONV_HALO + length:2 * CONV_HALO + length, :] = zeros
        pad_ref[CONV_HALO:CONV_HALO + length, :] = src_ref[0].astype(F32)

        def chunk(c, carry):
            base = pl.multiple_of(c * CHUNK, CHUNK)
            for cb_i in range(nconv // LANES):
                cols = slice(cb_i * LANES, (cb_i + 1) * LANES)
                w = pad_ref[pl.ds(base, win), cols]
                acc = cb_ref[:, cols] + w[CONV_HALO:CONV_HALO + CHUNK] * cw_ref[SSD_CONV // 2:SSD_CONV // 2 + 1, cols]
                for j in range(SSD_CONV):
                    if j == SSD_CONV // 2:
                        continue
                    shifted = pltpu.roll(w, (SSD_CONV // 2 - j) % win, 0)
                    acc = acc + shifted[CONV_HALO:CONV_HALO + CHUNK] * cw_ref[j:j + 1, cols]
                dst_ref[pl.ds(base, CHUNK), cols] = _silu(acc).astype(BF16)
            return carry

        lax.fori_loop(0, n_chunks, chunk, 0)

    conv_pass(xbcc_ref, xpadc, uc, nchc, Lc)
    conv_pass(xbc_ref, xpad, u, nch, L)

    a_neg = -jnp.exp(alog_ref[...])
    dtv[...] = jax.nn.softplus(dt_ref[0] + dtb_ref[...])
    dav[...] = dtv[...] * a_neg
    dtcv[...] = jax.nn.softplus(dtc_ref[0] + dtb_ref[...])
    dacv[...] = dtcv[...] * a_neg

    row_i = lax.broadcasted_iota(jnp.int32, (CHUNK, CHUNK), 0)
    col_i = lax.broadcasted_iota(jnp.int32, (CHUNK, CHUNK), 1)
    causal = col_i <= row_i
    lo_half = col_i < SSD_HEADDIM
    fwd_lane = col_i < nh
    head_of = lax.broadcasted_iota(jnp.int32, (CHUNK, SSD_WIDTH), 1) // SSD_HEADDIM
    src_col = lax.broadcasted_iota(jnp.int32, (CHUNK, SSD_WIDTH), 0)
    exp_f = (head_of == src_col).astype(BF16)
    exp_b = (head_of == src_col - nh).astype(BF16)
    exp_fb = jnp.concatenate([exp_f, exp_b], axis=1)

    def split3(v):
        hi = v.astype(BF16)
        r1 = v - hi.astype(F32)
        mid = r1.astype(BF16)
        return hi, mid, (r1 - mid.astype(F32)).astype(BF16)

    def times_onehot(v, m, passes=3):
        parts = split3(v)[:passes]
        acc = _dot(parts[0], m)
        for part in parts[1:]:
            acc = acc + _dot(part, m)
        return acc

    def colb(mat, r):
        return jnp.broadcast_to(mat[:, r:r + 1], (CHUNK, CHUNK))

    def pair_sel(a, b_):
        return jnp.where(lo_half, a, b_)

    gw = 2 * LANES

    def chunk_terms(u_ref, dt_s, da_s, base):
        dt = dt_s[pl.ds(base, CHUNK), :]
        da = da_s[pl.ds(base, CHUNK), :]
        acol = da
        for step in (1, 2, 4, 8, 16, 32, 64):
            acol = acol + jnp.where(row_i >= step, pltpu.roll(acol, step, 0), 0.0)
        ecol = acol - da
        last = acol[CHUNK - 1:CHUNK, :]
        wgt = jnp.where(fwd_lane, jnp.exp(last - acol), jnp.exp(ecol)) * dt
        scale = jnp.where(fwd_lane, jnp.exp(acol), jnp.exp(last - ecol))
        wide = times_onehot(jnp.concatenate([wgt, scale], axis=0), exp_fb, passes=1)
        dec = times_onehot(jnp.broadcast_to(jnp.exp(last), (SUBLANES, LANES)), exp_fb)[0:1]
        xs = u_ref[pl.ds(base, CHUNK), 0:SSD_WIDTH].astype(F32)
        kmats = []
        for g in range(SSD_GROUPS):
            xw = jnp.concatenate([xs[:, g * gw:(g + 1) * gw] * wide[:CHUNK, g * gw:(g + 1) * gw],
                                  xs[:, g * gw:(g + 1) * gw] * wide[:CHUNK, SSD_WIDTH + g * gw:SSD_WIDTH + (g + 1) * gw]],
                                 axis=1).astype(BF16)
            bm = u_ref[pl.ds(base, CHUNK), SSD_WIDTH + g * SSD_STATE:SSD_WIDTH + (g + 1) * SSD_STATE]
            kmats.append(_dot_tn(bm, xw))
        return dt, acol, ecol, wide[CHUNK:], dec, kmats

    def advance(s, dec, kmats, backward):
        off = SSD_WIDTH if backward else 0
        koff = gw if backward else 0
        return [dec[:, off + g * gw:off + (g + 1) * gw] * s[g] + kmats[g][:, koff:koff + gw]
                for g in range(SSD_GROUPS)]

    ctx_terms = [chunk_terms(uc, dtcv, dacv, c * CHUNK) for c in range(nchc)]
    s_f0 = [jnp.zeros((SSD_STATE, gw), F32) for _ in range(SSD_GROUPS)]
    for c in range(nchc):
        s_f0 = advance(s_f0, ctx_terms[c][4], ctx_terms[c][5], False)
    s_b0 = [jnp.zeros((SSD_STATE, gw), F32) for _ in range(SSD_GROUPS)]
    for c in reversed(range(nchc)):
        s_b0 = advance(s_b0, ctx_terms[c][4], ctx_terms[c][5], True)

    def prep(c, carry):
        base = pl.multiple_of(c * CHUNK, CHUNK)
        dt, acol, ecol, scale, dec, kmats = chunk_terms(u, dtv, dav, base)
        acum[pl.ds(base, CHUNK), :] = acol
        ecum[pl.ds(base, CHUNK), :] = ecol
        hrow = pl.ds(pl.multiple_of(c * 2 * nh, 2 * nh), 2 * nh)
        arow_scr[hrow, :] = acol.T[:2 * nh]
        erow_scr[hrow, :] = ecol.T[:2 * nh]
        dtrow_scr[hrow, :] = dt.T[:2 * nh]
        xpad[pl.ds(base, CHUNK), :] = scale
        dec_scr[pl.ds(pl.multiple_of(c * SUBLANES, SUBLANES), SUBLANES), :] = jnp.broadcast_to(dec, (SUBLANES, 2 * SSD_WIDTH))
        for g in range(SSD_GROUPS):
            sf_scr[c, g] = kmats[g][:, :gw]
            kb_scr[c, g] = kmats[g][:, gw:]
        return carry

    lax.fori_loop(0, nch, prep, 0)

    def chunk_dec(c):
        return dec_scr[pl.ds(pl.multiple_of(c * SUBLANES, SUBLANES), 1), :]

    def fwd(c, s_old):
        dec = chunk_dec(c)
        new = []
        for g in range(SSD_GROUPS):
            new.append(dec[:, g * gw:(g + 1) * gw] * s_old[g] + sf_scr[c, g])
            sf_scr[c, g] = s_old[g]
        return tuple(new)

    lax.fori_loop(0, nch, fwd, tuple(s_f0))

    def bwd(i, s_b):
        c = nch - 1 - i
        base = pl.multiple_of(c * CHUNK, CHUNK)
        acol = acum[pl.ds(base, CHUNK), :]
        ecol = ecum[pl.ds(base, CHUNK), :]
        hrow = pl.ds(pl.multiple_of(c * 2 * nh, 2 * nh), 2 * nh)
        arow = arow_scr[hrow, :]
        erow = erow_scr[hrow, :]
        dt_t = dtrow_scr[hrow, :]
        scale = xpad[pl.ds(base, CHUNK), :]
        ys = []
        for g in range(SSD_GROUPS):
            bm = u[pl.ds(base, CHUNK), SSD_WIDTH + g * SSD_STATE:SSD_WIDTH + (g + 1) * SSD_STATE]
            cm = u[pl.ds(base, CHUNK), SSD_WIDTH + (SSD_GROUPS + g) * SSD_STATE:SSD_WIDTH + (SSD_GROUPS + g + 1) * SSD_STATE]
            cbm = _dot_nt(cm, bm)
            cs_f = _dot(cm, sf_scr[c, g].astype(BF16))
            cs_b = _dot(cm, s_b[g].astype(BF16))
            for pp in range(SSD_PAIRS // SSD_GROUPS):
                p = g * (SSD_PAIRS // SSD_GROUPS) + pp
                xs_b = u[pl.ds(base, CHUNK), p * LANES:(p + 1) * LANES]
                y_h = []
                for hh in range(2):
                    r = 2 * p + hh
                    arg = jnp.where(causal, colb(acol, r) - arow[r:r + 1, :],
                                    erow[nh + r:nh + r + 1, :] - colb(ecol, nh + r))
                    coef = jnp.where(causal, dt_t[r:r + 1, :], dt_t[nh + r:nh + r + 1, :])
                    gm = (cbm * (jnp.exp(arg) * coef)).astype(BF16)
                    y_h.append(_dot(gm, xs_b))
                sl = slice(pp * LANES, (pp + 1) * LANES)
                wl = slice(p * LANES, (p + 1) * LANES)
                wlb = slice(SSD_WIDTH + p * LANES, SSD_WIDTH + (p + 1) * LANES)
                ys.append(pair_sel(y_h[0], y_h[1]) + cs_f[:, sl] * scale[:, wl] + cs_b[:, sl] * scale[:, wlb]
                          + dsk_ref[:, wl] * xs_b.astype(F32))
        y = jnp.concatenate(ys, axis=1)
        y = y * _silu(z_ref[0, pl.ds(base, CHUNK), :].astype(F32))
        ms = jnp.mean(y * y, axis=-1, keepdims=True)
        y_ref[0, pl.ds(base, CHUNK), :] = ((y * lax.rsqrt(ms + EPS)) * nw_ref[...]).astype(BF16)
        dec = chunk_dec(c)
        return tuple(dec[:, SSD_WIDTH + g * gw:SSD_WIDTH + (g + 1) * gw] * s_b[g] + kb_scr[c, g]
                     for g in range(SSD_GROUPS))

    lax.fori_loop(0, nch, bwd, tuple(s_b0))


def _ssd(xbc, z, dt, xbcc, dtc, conv_w8, conv_b, dt_bias, a_log, d_skip, norm_w):
    b, L, nconv = xbc.shape
    Lc = xbcc.shape[1]
    nch = L // CHUNK
    per_b = lambda n, w: pl.BlockSpec((1, n, w), lambda i: (i, 0, 0))
    return pl.pallas_call(
        _ssd_kernel,
        grid=(b,),
        in_specs=[
            per_b(L, nconv), per_b(L, SSD_WIDTH), per_b(L, LANES), per_b(Lc, nconv), per_b(Lc, LANES),
            _const_spec(conv_w8.shape), _const_spec(conv_b.shape), _const_spec(dt_bias.shape),
            _const_spec(a_log.shape), _const_spec(d_skip.shape), _const_spec(norm_w.shape),
        ],
        out_specs=per_b(L, SSD_WIDTH),
        out_shape=jax.ShapeDtypeStruct((b, L, SSD_WIDTH), BF16),
        scratch_shapes=[
            pltpu.VMEM((L + 2 * CONV_HALO, nconv), F32),
            pltpu.VMEM((Lc + 2 * CONV_HALO, nconv), F32),
            pltpu.VMEM((L, nconv), BF16),
            pltpu.VMEM((Lc, nconv), BF16),
            pltpu.VMEM((L, LANES), F32), pltpu.VMEM((L, LANES), F32),
            pltpu.VMEM((Lc, LANES), F32), pltpu.VMEM((Lc, LANES), F32),
            pltpu.VMEM((nch, SSD_GROUPS, SSD_STATE, 2 * LANES), F32),
            pltpu.VMEM((nch, SSD_GROUPS, SSD_STATE, 2 * LANES), F32),
            pltpu.VMEM((L, LANES), F32), pltpu.VMEM((L, LANES), F32),
            pltpu.VMEM((nch * SUBLANES, 2 * SSD_WIDTH), F32),
            pltpu.VMEM((nch * 2 * SSD_HEADS, CHUNK), F32), pltpu.VMEM((nch * 2 * SSD_HEADS, CHUNK), F32),
            pltpu.VMEM((nch * 2 * SSD_HEADS, CHUNK), F32),
        ],
        compiler_params=pltpu.CompilerParams(vmem_limit_bytes=VMEM_LIMIT_SSD),
        name="ssd",
    )(xbc, z, dt, xbcc, dtc, conv_w8, conv_b, dt_bias, a_log, d_skip, norm_w)


def _ret_kernel(q_ref, k_ref, v_ref, g_ref, kc_ref, vc_ref, df_ref, db_ref, gn_ref, y_ref, sf_scr):
    L = q_ref.shape[1]
    Lc = kc_ref.shape[1]
    nch = L // CHUNK
    dk = RET_DK
    row_i = lax.broadcasted_iota(jnp.int32, (CHUNK, dk), 0).astype(F32)
    col_i = lax.broadcasted_iota(jnp.int32, (CHUNK, dk), 1).astype(F32)
    rel = row_i - col_i
    crow = lax.broadcasted_iota(jnp.int32, (Lc, dk), 0).astype(F32)

    heads = []
    s_f0 = []
    s_b0 = []
    for h in range(RET_HEADS):
        cols = slice(h * dk, (h + 1) * dk)
        lg_f = -jnp.exp(df_ref[:, cols])
        lg_b = -jnp.exp(db_ref[:, cols])
        heads.append(dict(
            cols=cols,
            dmat=jnp.where(rel >= 0, jnp.exp(jnp.maximum(rel, 0.0) * lg_f), jnp.exp(jnp.maximum(-rel, 0.0) * lg_b)),
            dq_f=jnp.exp((row_i + 1.0) * lg_f),
            dq_b=jnp.exp((CHUNK - row_i) * lg_b),
            dk_f=jnp.exp((CHUNK - 1.0 - row_i) * lg_f),
            dk_b=jnp.exp(row_i * lg_b),
            dc_f=jnp.exp(CHUNK * lg_f),
            dc_b=jnp.exp(CHUNK * lg_b),
        ))
        kc = kc_ref[0, :, cols].astype(F32)
        vc = vc_ref[0, :, cols]
        s_f0.append(_dot_tn((kc * jnp.exp((Lc - 1.0 - crow) * lg_f)).astype(BF16), vc))
        s_b0.append(_dot_tn((kc * jnp.exp(crow * lg_b)).astype(BF16), vc))

    def fwd(c, s_f):
        base = pl.multiple_of(c * CHUNK, CHUNK)
        new = []
        for h, hd in enumerate(heads):
            sf_scr[c, h] = s_f[h]
            kk = k_ref[0, pl.ds(base, CHUNK), hd["cols"]].astype(F32)
            vv = v_ref[0, pl.ds(base, CHUNK), hd["cols"]]
            new.append(hd["dc_f"] * s_f[h] + _dot_tn((kk * hd["dk_f"]).astype(BF16), vv))
        return tuple(new)

    lax.fori_loop(0, nch, fwd, tuple(s_f0), unroll=RET_UNROLL)

    def bwd(i, s_bs):
        c = nch - 1 - i
        base = pl.multiple_of(c * CHUNK, CHUNK)
        new = []
        for h, hd in enumerate(heads):
            qq = q_ref[0, pl.ds(base, CHUNK), hd["cols"]]
            kk = k_ref[0, pl.ds(base, CHUNK), hd["cols"]]
            vv = v_ref[0, pl.ds(base, CHUNK), hd["cols"]]
            s_b = s_bs[h]
            scores = (_dot_nt(qq, kk) * hd["dmat"]).astype(BF16)
            y = (_dot(scores, vv)
                 + _dot(qq, sf_scr[c, h].astype(BF16)) * hd["dq_f"]
                 + _dot(qq, s_b.astype(BF16)) * hd["dq_b"])
            mu = jnp.mean(y, axis=-1, keepdims=True)
            yc = y - mu
            var = jnp.mean(yc * yc, axis=-1, keepdims=True)
            yn = (yc * lax.rsqrt(var + EPS)) * gn_ref[:, hd["cols"]]
            gate = _silu(g_ref[0, pl.ds(base, CHUNK), hd["cols"]].astype(F32))
            y_ref[0, pl.ds(base, CHUNK), hd["cols"]] = (yn * gate).astype(BF16)
            new.append(hd["dc_b"] * s_b + _dot_tn((kk.astype(F32) * hd["dk_b"]).astype(BF16), vv))
        return tuple(new)

    lax.fori_loop(0, nch, bwd, tuple(s_b0), unroll=RET_UNROLL)


def _retention(q, k, v, g, kc, vc, decay_f, decay_b, gn_w):
    b, L, w = q.shape
    Lc = kc.shape[1]
    nch = L // CHUNK
    per_b = lambda n: pl.BlockSpec((1, n, w), lambda i: (i, 0, 0))
    return pl.pallas_call(
        _ret_kernel,
        grid=(b,),
        in_specs=[per_b(L), per_b(L), per_b(L), per_b(L), per_b(Lc), per_b(Lc),
                  _const_spec((1, w)), _const_spec((1, w)), _const_spec((1, w))],
        out_specs=per_b(L),
        out_shape=jax.ShapeDtypeStruct((b, L, w), BF16),
        scratch_shapes=[
            pltpu.VMEM((nch, RET_HEADS, RET_DK, RET_DK), F32),
        ],
        compiler_params=pltpu.CompilerParams(vmem_limit_bytes=VMEM_LIMIT),
        name="retention",
    )(q, k, v, g, kc, vc, decay_f, decay_b, gn_w)


def _outproj_router_kernel(yr_ref, ys_ref, x_ref, g1_ref, sh2_ref, sc2_ref, npost_ref, npre_ref,
                           wor_ref, wos_ref, wr_ref, br_ref, tri_ref,
                           x1_ref, h2_ref, route_ref, slots_ref, seg_ref,
                           wcat):
    i = pl.program_id(0)

    @pl.when(i == 0)
    def _():
        wr = wr_ref[...]
        hi = wr.astype(BF16)
        wcat[:, :LANES] = hi
        wcat[:, LANES:] = (wr - hi.astype(F32)).astype(BF16)

    _route_tile(yr_ref, ys_ref, x_ref, g1_ref, sh2_ref, sc2_ref, npost_ref, npre_ref, wor_ref, wos_ref, br_ref,
                tri_ref, x1_ref, h2_ref, route_ref, slots_ref, seg_ref, wcat)


def _route_tile(yr_ref, ys_ref, x_ref, g1_ref, sh2_ref, sc2_ref, npost_ref, npre_ref, wor_ref, wos_ref, br_ref,
                tri_ref, x1_ref, h2_ref, route_ref, slots_ref, seg_ref, wcat):
    tm = x_ref.shape[0]
    rows = slice(0, tm)
    y = _dot(yr_ref[rows, :], wor_ref[...]) + _dot(ys_ref[rows, :], wos_ref[...])
    ms = jnp.mean(y * y, axis=-1, keepdims=True)
    x1 = x_ref[rows, :] + (y * lax.rsqrt(ms + EPS)) * (g1_ref[0] * npost_ref[...])
    x1_ref[rows, :] = x1
    h2 = _norm_mod(x1, npre_ref[...], sc2_ref[0], sh2_ref[0])
    h2_ref[rows, :] = h2

    h_hi = h2.astype(BF16)
    h_lo = (h2 - h_hi.astype(F32)).astype(BF16)
    both = _dot(h_hi, wcat[...])
    lg = both[:, :LANES] + both[:, LANES:] + _dot(h_lo, wcat[:, :LANES]) + br_ref[...]

    lane = lax.broadcasted_iota(jnp.int32, (tm, LANES), 1)
    lane_f = lane.astype(F32)
    is_grp = (lane >= N_EXPERTS) & (lane < N_EXPERTS + MOE_GROUPS)
    gl = jnp.where(is_grp, lg, NEG_BIG)
    mg = jnp.max(gl, axis=-1, keepdims=True)
    grp_lane = jnp.min(jnp.where(gl == mg, lane_f, 1e9), axis=-1, keepdims=True)
    p_g = 1.0 / jnp.sum(jnp.where(is_grp, jnp.exp(gl - mg), 0.0), axis=-1, keepdims=True)
    first = (grp_lane - N_EXPERTS) * EXPERTS_PER_GROUP
    in_grp = (lane_f >= first) & (lane_f < first + EXPERTS_PER_GROUP)
    el = jnp.where(in_grp, lg, NEG_BIG)
    t1 = jnp.max(el, axis=-1, keepdims=True)
    i1 = jnp.min(jnp.where(el == t1, lane_f, 1e9), axis=-1, keepdims=True)
    el2 = jnp.where(lane_f == i1, NEG_BIG, el)
    t2 = jnp.max(el2, axis=-1, keepdims=True)
    i2 = jnp.min(jnp.where(el2 == t2, lane_f, 1e9), axis=-1, keepdims=True)
    s = jnp.exp(t2 - t1)
    w1 = p_g / (1.0 + s)
    w2 = p_g * s / (1.0 + s)

    oh1 = (lane_f == i1)
    oh2 = (lane_f == i2)
    ohf = jnp.where(oh1 | oh2, 1.0, 0.0)
    before = _dot(tri_ref[...], ohf.astype(BF16))
    cnt = jnp.sum(ohf, axis=0, keepdims=True)
    seg = jnp.floor((cnt + (SUBLANES - 1.0)) * (1.0 / SUBLANES)) * SUBLANES
    e_row = lax.broadcasted_iota(jnp.int32, (LANES, LANES), 0)
    e_col = lax.broadcasted_iota(jnp.int32, (LANES, LANES), 1)
    earlier = (e_row < e_col).astype(BF16)
    seg_off = _dot(jnp.broadcast_to(seg, (SUBLANES, LANES)).astype(BF16), earlier)[0:1]
    where_to = before + seg_off
    lpos1 = jnp.sum(jnp.where(oh1, where_to, 0.0), axis=-1, keepdims=True)
    lpos2 = jnp.sum(jnp.where(oh2, where_to, 0.0), axis=-1, keepdims=True)

    cols = [w1, w2, lpos1, lpos2]
    for wk in (w1, w2):
        hi = wk.astype(BF16).astype(F32)
        cols += [hi, wk - hi]
    cols.append(jnp.ones_like(w1))
    packed = jnp.zeros((tm, LANES), F32)
    for k, col in enumerate(cols):
        packed = jnp.where(lane == k, col, packed)
    route_ref[rows, :] = packed

    row = lax.broadcasted_iota(jnp.int32, (tm, LANES), 0)
    on_diag = (row % LANES) == lane
    per = tm // LANES
    for qi, col in enumerate((lpos1, lpos2)):
        picked = jnp.where(on_diag, col, 0.0)
        dense = jnp.sum(picked.reshape(per, LANES, LANES), axis=1).astype(jnp.int32)
        slots_ref[0, qi * per:(qi + 1) * per, :] = dense
    tbl_row = lax.broadcasted_iota(jnp.int32, (SUBLANES, LANES), 0)
    seg_ref[0] = jnp.where(tbl_row == 0, seg, jnp.where(tbl_row == 1, seg_off, 0.0)).astype(jnp.int32)


def _outproj_router(yr, ys, x2, mod3, npost, npre, wo_r, wo_s, w_router, b_router, seq_len):
    T, d = x2.shape
    tm = TM_OUT
    per_seq = seq_len // tm
    rw = yr.shape[1]
    tri = (jnp.arange(tm)[:, None] > jnp.arange(tm)[None, :]).astype(BF16)
    tok = lambda w: pl.BlockSpec((tm, w), lambda i: (i, 0))
    modv = lambda k: pl.BlockSpec((1, 1, d), lambda i: (i // per_seq, 0, k))
    tile3 = lambda r: pl.BlockSpec((1, r, LANES), lambda i: (i, 0, 0))
    slot_rows = TOP_K * (tm // LANES)
    return pl.pallas_call(
        _outproj_router_kernel,
        grid=(T // tm,),
        in_specs=[
            tok(rw), tok(rw), tok(d), modv(2), modv(3), modv(4),
            _const_spec((1, d)), _const_spec((1, d)),
            _const_spec(wo_r.shape), _const_spec(wo_s.shape), _const_spec(w_router.shape), _const_spec((1, LANES)),
            _const_spec((tm, tm)),
        ],
        out_specs=[tok(d), tok(d), tok(LANES), tile3(slot_rows), tile3(SUBLANES)],
        out_shape=[jax.ShapeDtypeStruct((T, d), F32), jax.ShapeDtypeStruct((T, d), F32),
                   jax.ShapeDtypeStruct((T, LANES), F32),
                   jax.ShapeDtypeStruct((T // tm, slot_rows, LANES), jnp.int32),
                   jax.ShapeDtypeStruct((T // tm, SUBLANES, LANES), jnp.int32)],
        scratch_shapes=[pltpu.VMEM((d, 2 * LANES), BF16)],
        compiler_params=pltpu.CompilerParams(dimension_semantics=("arbitrary",),
                                             vmem_limit_bytes=VMEM_LIMIT),
        name="outproj_router",
    )(yr, ys, x2, mod3, mod3, mod3, npost, npre, wo_r, wo_s, w_router, b_router, tri)


def _slot_rows_kernel(start_ref, slots_ref, o_ref):
    half = slots_ref.shape[1] // 2
    expert = slots_ref[:, :half, :]
    row = slots_ref[:, half:, :]
    for e in range(N_EXPERTS):
        row = row + jnp.where(expert == e, start_ref[e], 0)
    for r in range(half):
        o_ref[:, 0, r * LANES:(r + 1) * LANES] = row[:, r, :]


def _slot_rows(pad_start, slots):
    nt, rows, _ = slots.shape
    grid_spec = pltpu.PrefetchScalarGridSpec(
        num_scalar_prefetch=1,
        grid=(1,),
        in_specs=[pl.BlockSpec(slots.shape, lambda i, ps: (0, 0, 0))],
        out_specs=pl.BlockSpec((nt, 1, rows // 2 * LANES), lambda i, ps: (0, 0, 0)),
    )
    return pl.pallas_call(
        _slot_rows_kernel,
        grid_spec=grid_spec,
        out_shape=jax.ShapeDtypeStruct((nt, 1, rows // 2 * LANES), jnp.int32),
        name="slot_rows",
    )(pad_start, slots)


def _dispatch_kernel(pad_end_ref, zero_from_ref, dest_ref, h_hbm, xs_hbm, zbuf, hbuf, sem, zsem, in_sem):
    i = pl.program_id(0)
    nt = pl.num_programs(0)
    td = hbuf.shape[1]
    mb = zbuf.shape[0]

    @pl.when(i == 0)
    def _():
        zbuf[...] = jnp.zeros_like(zbuf)

        def pieces(e, act):
            for p in range(mb // ZERO_PIECE):
                row = zero_from_ref[e] + p * ZERO_PIECE

                @pl.when(row < pad_end_ref[e])
                def _():
                    act(pltpu.make_async_copy(zbuf.at[pl.ds(0, ZERO_PIECE)],
                                              xs_hbm.at[pl.ds(pl.multiple_of(row, ZERO_PIECE), ZERO_PIECE)], zsem))

        def start(e, carry):
            pieces(e, lambda c: c.start())
            return carry

        def wait(e, carry):
            pieces(e, lambda c: c.wait())
            return carry

        def tail_fill(blk):
            return pltpu.make_async_copy(zbuf, xs_hbm.at[pl.ds(pl.multiple_of(blk * mb, mb), mb)], zsem)

        def tail_start(blk, carry):
            @pl.when(blk * mb >= pad_end_ref[N_EXPERTS - 1])
            def _():
                tail_fill(blk).start()
            return carry

        def tail_wait(blk, carry):
            @pl.when(blk * mb >= pad_end_ref[N_EXPERTS - 1])
            def _():
                tail_fill(blk).wait()
            return carry

        n_blocks = xs_hbm.shape[0] // mb
        lax.fori_loop(0, N_EXPERTS, start, 0)
        lax.fori_loop(0, n_blocks, tail_start, 0)
        lax.fori_loop(0, N_EXPERTS, wait, 0)
        lax.fori_loop(0, n_blocks, tail_wait, 0)

    def fetch(t):
        return pltpu.make_async_copy(h_hbm.at[pl.ds(pl.multiple_of(t * td, td), td)], hbuf.at[t % DISPATCH_RING],
                                     in_sem.at[t % DISPATCH_RING])

    @pl.when(i == 0)
    def _():
        fetch(i).start()

    @pl.when(i + 1 < nt)
    def _():
        fetch(i + 1).start()

    fetch(i).wait()
    for slot in range(DISPATCH_RING):
        @pl.when(i % DISPATCH_RING == slot)
        def _(slot=slot):
            _issue_row_copies(dest_ref, td, lambda kk, j, d: pltpu.make_async_copy(
                hbuf.at[slot, pl.ds(j, 1)], xs_hbm.at[pl.ds(d, 1)], sem.at[slot]))

    def drain(t):
        for kk in range(TOP_K):
            pltpu.make_async_copy(hbuf.at[t % DISPATCH_RING], xs_hbm.at[pl.ds(0, td)],
                                  sem.at[t % DISPATCH_RING]).wait()

    @pl.when(i > 0)
    def _():
        drain(i - 1)

    @pl.when(i == nt - 1)
    def _():
        drain(i)


def _issue_row_copies(dest_ref, n_tok, row_copy):
    def issue(j, carry):
        for kk in range(TOP_K):
            row_copy(kk, j, dest_ref[0, 0, kk * n_tok + j]).start()
        return carry

    lax.fori_loop(0, n_tok, issue, 0, unroll=DMA_UNROLL)


def _dispatch(pad_end, zero_from, dest, h2, cap):
    T, d = h2.shape
    td = TD_DISPATCH
    nt = T // td
    grid_spec = pltpu.PrefetchScalarGridSpec(
        num_scalar_prefetch=2,
        grid=(nt,),
        in_specs=[
            pl.BlockSpec((1, 1, TOP_K * td), lambda i, pe, zf: (i, 0, 0), memory_space=pltpu.SMEM),
            pl.BlockSpec(memory_space=pl.ANY),
        ],
        out_specs=pl.BlockSpec(memory_space=pl.ANY),
        scratch_shapes=[pltpu.VMEM((MB_EXPERT, d), h2.dtype), pltpu.VMEM((DISPATCH_RING, td, d), h2.dtype),
                        pltpu.SemaphoreType.DMA((DISPATCH_RING,)), pltpu.SemaphoreType.DMA(()),
                        pltpu.SemaphoreType.DMA((DISPATCH_RING,))],
    )
    return pl.pallas_call(
        _dispatch_kernel,
        grid_spec=grid_spec,
        out_shape=jax.ShapeDtypeStruct((cap, d), h2.dtype),
        compiler_params=pltpu.CompilerParams(dimension_semantics=("arbitrary",)),
        name="dispatch",
    )(pad_end, zero_from, dest, h2)


def _expert_kernel(be_ref, first_ref, slot_ref, next_ref, nused_ref, xs_ref, wg_hbm, wu_hbm, wd_hbm, y_ref,
                   wg_f, wu_f, wd_f, wg_b, wu_b, wd_b, sem):
    i = pl.program_id(0)

    def fetch(e, s):
        return [pltpu.make_async_copy(src.at[e], dst.at[s], sem.at[s, k])
                for k, (src, dst) in enumerate(((wg_hbm, wg_f), (wu_hbm, wu_f), (wd_hbm, wd_f)))]

    @pl.when(i == 0)
    def _():
        for c in fetch(be_ref[0], slot_ref[0]):
            c.start()

    @pl.when(first_ref[i] == 1)
    def _():
        s = slot_ref[i]

        @pl.when(next_ref[i] >= 0)
        def _():
            for c in fetch(next_ref[i], 1 - s):
                c.start()

        for c in fetch(be_ref[i], s):
            c.wait()
        wg_b[...] = wg_f[s].astype(BF16)
        wu_b[...] = wu_f[s].astype(BF16)
        wd_b[...] = wd_f[s].astype(BF16)

    @pl.when(i < nused_ref[0])
    def _():
        d = y_ref.shape[1]
        side = xs_ref[:, d:d + LANES]
        second = side[:, 8:9] == 2.0
        unscale = jnp.where(second, 0.5, 1.0)
        weight = jnp.where(second, side[:, 6:7] + side[:, 7:8], side[:, 4:5] + side[:, 5:6]) * unscale
        xb = (xs_ref[:, 0:d] * unscale).astype(BF16)
        hid = (_silu(_dot(xb, wg_b[...])) * _dot(xb, wu_b[...])).astype(BF16)
        y_ref[...] = (_dot(hid, wd_b[...]) * weight).astype(BF16).astype(F32)

    @pl.when(i >= nused_ref[0])
    def _():
        y_ref[...] = jnp.zeros_like(y_ref)


def _expert_plan(padded, pad_end, n_blocks, mb):
    n_used = (pad_end[-1:] // mb).astype(jnp.int32)
    blk_start = jnp.arange(n_blocks, dtype=jnp.int32) * mb
    experts = jnp.arange(N_EXPERTS, dtype=jnp.int32)
    blk_expert = jnp.minimum(jnp.sum((pad_end[None, :] <= blk_start[:, None]).astype(jnp.int32), axis=1),
                             N_EXPERTS - 1)
    prev = jnp.concatenate([jnp.full((1,), -1, jnp.int32), blk_expert[:-1]])
    first = ((blk_start < pad_end[-1]) & (blk_expert != prev)).astype(jnp.int32)
    slot = (jnp.cumsum(first) - 1) % 2
    later = jnp.where((padded > 0)[None, :] & (experts[None, :] > experts[:, None]), experts[None, :], N_EXPERTS)
    next_nonempty = jnp.min(later, axis=1)
    next_nonempty = jnp.where(next_nonempty == N_EXPERTS, -1, next_nonempty)
    nxt = jnp.sum(jnp.where(blk_expert[:, None] == experts[None, :], next_nonempty[None, :], 0), axis=1)
    return [a.astype(jnp.int32) for a in (blk_expert, first, slot, nxt, n_used)]


def _experts(plan, xs, w_gate, w_up, w_down):
    rows, dp = xs.shape
    n_exp, d, de = w_gate.shape
    assert dp == d + LANES
    mb = MB_EXPERT
    cap = rows // mb * mb
    grid_spec = pltpu.PrefetchScalarGridSpec(
        num_scalar_prefetch=len(plan),
        grid=(cap // mb,),
        in_specs=[
            pl.BlockSpec((mb, dp), lambda i, be, fi, sl, nx, nu: (jnp.minimum(i, nu[0] - 1), 0)),
            pl.BlockSpec(memory_space=pl.ANY), pl.BlockSpec(memory_space=pl.ANY), pl.BlockSpec(memory_space=pl.ANY),
        ],
        out_specs=pl.BlockSpec((mb, d), lambda i, be, fi, sl, nx, nu: (i, 0)),
        scratch_shapes=[pltpu.VMEM((2, d, de), F32), pltpu.VMEM((2, d, de), F32), pltpu.VMEM((2, de, d), F32),
                        pltpu.VMEM((d, de), BF16), pltpu.VMEM((d, de), BF16), pltpu.VMEM((de, d), BF16),
                        pltpu.SemaphoreType.DMA((2, 3))],
    )
    return pl.pallas_call(
        _expert_kernel,
        grid_spec=grid_spec,
        out_shape=jax.ShapeDtypeStruct((cap, d), F32),
        compiler_params=pltpu.CompilerParams(dimension_semantics=("arbitrary",),
                                             vmem_limit_bytes=VMEM_LIMIT),
        name="experts",
    )(*plan, xs, w_gate, w_up, w_down)


def _combine_kernel(dest_ref, dest_next_ref, route_ref, x1_ref, g2_ref, nw_ref, yb_hbm, o_ref, buf, sem):
    i = pl.program_id(0)
    nt = pl.num_programs(0)
    tf = x1_ref.shape[0]
    stage = i % 2

    def gather(refs, st):
        _issue_row_copies(refs, tf, lambda kk, j, d: pltpu.make_async_copy(
            yb_hbm.at[pl.ds(d, 1)], buf.at[st, kk, pl.ds(j, 1)], sem.at[st]))

    @pl.when(i == 0)
    def _():
        gather(dest_ref, 0)

    for st in range(2):
        @pl.when((i + 1 < nt) & (stage != st))
        def _(st=st):
            gather(dest_next_ref, st)

    for kk in range(TOP_K):
        pltpu.make_async_copy(yb_hbm.at[pl.ds(0, tf)], buf.at[stage, kk], sem.at[stage]).wait()

    route = route_ref[...]
    out = route[:, 0:1] * buf[stage, 0] + route[:, 1:2] * buf[stage, 1]
    ms = jnp.mean(out * out, axis=-1, keepdims=True)
    o_ref[...] = x1_ref[...] + g2_ref[0] * ((out * lax.rsqrt(ms + EPS)) * nw_ref[...])


def _combine(dest, route, x1, mod3, nw, yb, seq_len):
    T, d = x1.shape
    tf = TF_COMBINE
    nt = T // tf
    per_seq = seq_len // tf
    slot_spec = functools.partial(pl.BlockSpec, (1, 1, TOP_K * tf), memory_space=pltpu.SMEM)
    return pl.pallas_call(
        _combine_kernel,
        grid=(nt,),
        in_specs=[
            slot_spec(lambda i: (i, 0, 0)),
            slot_spec(lambda i: (jnp.minimum(i + 1, nt - 1), 0, 0)),
            pl.BlockSpec((tf, LANES), lambda i: (i, 0)),
            pl.BlockSpec((tf, d), lambda i: (i, 0)),
            pl.BlockSpec((1, 1, d), lambda i: (i // per_seq, 0, 5)),
            _const_spec((1, d)),
            pl.BlockSpec(memory_space=pl.ANY),
        ],
        out_specs=pl.BlockSpec((tf, d), lambda i: (i, 0)),
        out_shape=jax.ShapeDtypeStruct((T, d), F32),
        scratch_shapes=[pltpu.VMEM((2, TOP_K, tf, yb.shape[1]), yb.dtype), pltpu.SemaphoreType.DMA((2,))],
        compiler_params=pltpu.CompilerParams(dimension_semantics=("arbitrary",),
                                             vmem_limit_bytes=VMEM_LIMIT),
        name="combine",
    )(dest, dest, route, x1, mod3, nw, yb)


def _segment_pieces(tile, segrow_ref, seglen_ref, segoff_ref, act):
    def per_expert(e, carry):
        idx = tile * N_EXPERTS + e
        g0 = segrow_ref[idx]
        l0 = segoff_ref[idx]

        def piece(j, c2):
            act(pl.multiple_of(l0 + j * SUBLANES, SUBLANES), pl.multiple_of(g0 + j * SUBLANES, SUBLANES))
            return c2

        lax.fori_loop(0, seglen_ref[idx], piece, 0)
        return carry

    lax.fori_loop(0, N_EXPERTS, per_expert, 0)


def _wait_rows(total, row_copy):
    size = SUBLANES
    while size <= _stage_rows(max(TD_DISPATCH, TF_COMBINE)):
        @pl.when((total & size) != 0)
        def _(size=size):
            row_copy(size).wait()
        size *= 2


def _stage_rows(tile_tokens):
    return TOP_K * tile_tokens + N_EXPERTS * SUBLANES


def _dispatch_seg_kernel(pad_end_ref, zero_from_ref, segrow_ref, seglen_ref, segoff_ref, tilerows_ref,
                         lpos_ref, route_ref, h_ref, xs_hbm, zbuf, stage, sem, zsem):
    i = pl.program_id(0)
    nt = pl.num_programs(0)
    td, d = h_ref.shape
    sr = stage.shape[1]
    per = td // LANES

    @pl.when(i == 0)
    def _():
        zbuf[...] = jnp.zeros_like(zbuf)

        def pieces(e, act):
            for p in range(MB_EXPERT // ZERO_PIECE):
                row = pad_end_ref[e] - (p + 1) * ZERO_PIECE

                @pl.when(row + ZERO_PIECE > zero_from_ref[e])
                def _():
                    act(pltpu.make_async_copy(zbuf, xs_hbm.at[pl.ds(pl.multiple_of(row, ZERO_PIECE), ZERO_PIECE)], zsem))

        def start(e, carry):
            pieces(e, lambda c: c.start())
            return carry

        def wait(e, carry):
            pieces(e, lambda c: c.wait())
            return carry

        def tail(p, act):
            row = pad_end_ref[N_EXPERTS - 1] + p * ZERO_PIECE

            @pl.when(row + ZERO_PIECE <= xs_hbm.shape[0])
            def _():
                act(pltpu.make_async_copy(zbuf, xs_hbm.at[pl.ds(pl.multiple_of(row, SUBLANES), ZERO_PIECE)], zsem))

        n_tail = xs_hbm.shape[0] // ZERO_PIECE
        lax.fori_loop(0, N_EXPERTS, start, 0)
        lax.fori_loop(0, n_tail, lambda p, c: (tail(p, lambda cp: cp.start()), c)[1], 0)
        lax.fori_loop(0, N_EXPERTS, wait, 0)
        lax.fori_loop(0, n_tail, lambda p, c: (tail(p, lambda cp: cp.wait()), c)[1], 0)

    def shipped(tile):
        _wait_rows(tilerows_ref[tile], lambda n: pltpu.make_async_copy(
            stage.at[tile % 2, pl.ds(0, n)], xs_hbm.at[pl.ds(0, n)], sem.at[tile % 2]))

    @pl.when(i >= 2)
    def _():
        shipped(i - 2)

    srow = lax.broadcasted_iota(jnp.int32, (sr, LANES), 0)
    place = jnp.concatenate(
        [jnp.where(srow == lpos_ref[0, cb:cb + 1, :], 1.0,
                   jnp.where(srow == lpos_ref[0, per + cb:per + cb + 1, :], 2.0, 0.0)) for cb in range(per)],
        axis=1).astype(BF16)
    staged = _dot(place, jnp.concatenate([h_ref[...].astype(BF16), route_ref[...].astype(BF16)], axis=1))

    for slot in range(2):
        @pl.when(i % 2 == slot)
        def _(slot=slot):
            stage[slot] = staged
            _segment_pieces(i, segrow_ref, seglen_ref, segoff_ref, lambda lr, gr: pltpu.make_async_copy(
                stage.at[slot, pl.ds(lr, SUBLANES)], xs_hbm.at[pl.ds(gr, SUBLANES)], sem.at[slot]).start())

    @pl.when(i == nt - 1)
    def _():
        @pl.when(i >= 1)
        def _():
            shipped(i - 1)
        shipped(i)


def _dispatch_seg(plan, lpos, route, h2, cap):
    T, d = h2.shape
    td = TD_DISPATCH
    nt = T // td
    width = d + LANES
    sr = _stage_rows(td)
    n_pre = len(plan)
    grid_spec = pltpu.PrefetchScalarGridSpec(
        num_scalar_prefetch=n_pre,
        grid=(nt,),
        in_specs=[
            pl.BlockSpec((1, lpos.shape[1], LANES), lambda i, *_: (i, 0, 0)),
            pl.BlockSpec((td, LANES), lambda i, *_: (i, 0)),
            pl.BlockSpec((td, d), lambda i, *_: (i, 0)),
        ],
        out_specs=pl.BlockSpec(memory_space=pl.ANY),
        scratch_shapes=[pltpu.VMEM((ZERO_PIECE, width), F32), pltpu.VMEM((2, sr, width), F32),
                        pltpu.SemaphoreType.DMA((2,)), pltpu.SemaphoreType.DMA(())],
    )
    return pl.pallas_call(
        _dispatch_seg_kernel,
        grid_spec=grid_spec,
        out_shape=jax.ShapeDtypeStruct((cap, width), F32),
        compiler_params=pltpu.CompilerParams(dimension_semantics=("arbitrary",), vmem_limit_bytes=VMEM_LIMIT),
        name="dispatch",
    )(*plan, lpos, route, h2)


def _combine_seg_kernel(segrow_ref, seglen_ref, segoff_ref, tilerows_ref, route_ref, x1_ref, g2_ref, nw_ref,
                        yb_hbm, o_ref, stage, sem):
    i = pl.program_id(0)
    nt = pl.num_programs(0)
    tf = x1_ref.shape[0]
    sr = stage.shape[1]

    def fetch(tile, slot):
        _segment_pieces(tile, segrow_ref, seglen_ref, segoff_ref, lambda lr, gr: pltpu.make_async_copy(
            yb_hbm.at[pl.ds(gr, SUBLANES)], stage.at[slot, pl.ds(lr, SUBLANES)], sem.at[slot]).start())

    @pl.when(i == 0)
    def _():
        stage[...] = jnp.zeros_like(stage)
        fetch(i, 0)

    for slot in range(2):
        @pl.when((i + 1 < nt) & (i % 2 != slot))
        def _(slot=slot):
            fetch(i + 1, slot)

    _wait_rows(tilerows_ref[i], lambda n: pltpu.make_async_copy(
        yb_hbm.at[pl.ds(0, n)], stage.at[i % 2, pl.ds(0, n)], sem.at[i % 2]))

    route = route_ref[...]
    scol = lax.broadcasted_iota(jnp.int32, (tf, sr), 1).astype(F32)
    pick = jnp.where((scol == route[:, 2:3]) | (scol == route[:, 3:4]), 1.0, 0.0).astype(BF16)
    out = _dot(pick, stage[i % 2].astype(BF16))
    ms = jnp.mean(out * out, axis=-1, keepdims=True)
    o_ref[...] = x1_ref[...] + g2_ref[0] * ((out * lax.rsqrt(ms + EPS)) * nw_ref[...])


def _combine_seg(seg_plan, route, x1, mod3, nw, yb, seq_len):
    T, d = x1.shape
    tf = TF_COMBINE
    nt = T // tf
    per_seq = seq_len // tf
    grid_spec = pltpu.PrefetchScalarGridSpec(
        num_scalar_prefetch=len(seg_plan),
        grid=(nt,),
        in_specs=[
            pl.BlockSpec((tf, LANES), lambda i, *_: (i, 0)),
            pl.BlockSpec((tf, d), lambda i, *_: (i, 0)),
            pl.BlockSpec((1, 1, d), lambda i, *_: (i // per_seq, 0, 5)),
            pl.BlockSpec((1, d), lambda i, *_: (0, 0)),
            pl.BlockSpec(memory_space=pl.ANY),
        ],
        out_specs=pl.BlockSpec((tf, d), lambda i, *_: (i, 0)),
        scratch_shapes=[pltpu.VMEM((2, _stage_rows(tf), d), F32), pltpu.SemaphoreType.DMA((2,))],
    )
    return pl.pallas_call(
        _combine_seg_kernel,
        grid_spec=grid_spec,
        out_shape=jax.ShapeDtypeStruct((T, d), F32),
        compiler_params=pltpu.CompilerParams(dimension_semantics=("arbitrary",), vmem_limit_bytes=VMEM_LIMIT),
        name="combine",
    )(*seg_plan, route, x1, mod3, nw, yb)


def _rope_tables(L, n_heads):
    quarter = RET_DK // 4
    freqs = ROPE_BASE ** (-jnp.arange(quarter, dtype=F32) / quarter)
    t = jnp.arange(L)
    ang_r = (t // GRID_W).astype(F32)[:, None] * freqs
    ang_c = (t % GRID_W).astype(F32)[:, None] * freqs
    cos = jnp.concatenate([jnp.cos(ang_r)] * 2 + [jnp.cos(ang_c)] * 2, axis=-1)
    sin = jnp.concatenate([-jnp.sin(ang_r), jnp.sin(ang_r), -jnp.sin(ang_c), jnp.sin(ang_c)], axis=-1)
    return jnp.tile(cos, (1, n_heads)), jnp.tile(sin, (1, n_heads))


def _lane_pad(v, width=LANES):
    return jnp.pad(v, [(0, 0)] * (v.ndim - 1) + [(0, width - v.shape[-1])])


def kernel(x, c, ctx, c_ctx, w_mod, b_mod, norm_pre_mix, norm_post_mix, norm_pre_ffn, norm_post_ffn, w_in, w_out, ret_decay_f, ret_decay_b, ret_gn_w, ssd_conv_w, ssd_conv_b, ssd_dt_bias_f, ssd_dt_bias_b, ssd_a_log_f, ssd_a_log_b, ssd_d, ssd_norm_w, moe_w_rg, moe_b_rg, moe_w_re, moe_b_re, moe_w_gate, moe_w_up, moe_w_down):
    b, L, d = x.shape
    assert w_mod.shape[0] == 1, "single layer: context outputs are never needed"
    assert TM_OUT == TD_DISPATCH == TF_COMBINE, "router, dispatch and combine share one slot-row layout"
    rw = RET_HEADS * RET_DK
    nconv = SSD_WIDTH + 2 * SSD_GROUPS * SSD_STATE
    T = b * L

    mod_rows = -(-(b + 1) // SUBLANES) * SUBLANES
    c_all = jnp.zeros((mod_rows, d), F32).at[:b].set(c).at[b].set(c_ctx)
    mod3 = _modulation(c_all, w_mod[0], b_mod[0]).reshape(mod_rows, 1, 6 * d)

    wi = w_in[0]
    o = 0
    wq = wi[:, o:o + rw]; o += rw
    wk = wi[:, o:o + rw]; o += rw
    wv = wi[:, o:o + rw]; o += rw
    wg = wi[:, o:o + rw]; o += rw
    wz = wi[:, o:o + SSD_WIDTH]; o += SSD_WIDTH
    wxbc = wi[:, o:o + nconv].astype(BF16); o += nconv
    wdt = _lane_pad(wi[:, o:o + 2 * SSD_HEADS]).astype(BF16)
    wqk = jnp.concatenate([wq, wk], axis=1).astype(BF16)
    wvgz = jnp.concatenate([wv, wg, wz], axis=1).astype(BF16)
    cos_t, sin_t = _rope_tables(L, RET_HEADS)
    nw1 = norm_pre_mix[0].reshape(1, d)

    q, k, v, g, z, xbc, dt = _inproj(x, mod3, nw1, wqk, wvgz, wxbc, wdt, cos_t, sin_t)
    kc, vc, xbcc, dtc = _inproj_ctx(ctx, mod3, b, nw1, wk.astype(BF16), wv.astype(BF16), wxbc, wdt)

    conv_w8 = jnp.pad(ssd_conv_w[0], ((0, SUBLANES - SSD_CONV), (0, 0)))
    dt_bias = _lane_pad(jnp.concatenate([ssd_dt_bias_f[0], ssd_dt_bias_b[0]])[None, :])
    a_log = _lane_pad(jnp.concatenate([ssd_a_log_f[0], ssd_a_log_b[0]])[None, :])
    d_skip = jnp.repeat(ssd_d[0], SSD_HEADDIM)[None, :]
    ys = _ssd(xbc, z, dt, xbcc, dtc, conv_w8, ssd_conv_b[0][None, :], dt_bias, a_log, d_skip,
              ssd_norm_w[0][None, :])

    yr = _retention(q, k, v, g, kc, vc,
                    jnp.repeat(ret_decay_f[0], RET_DK)[None, :], jnp.repeat(ret_decay_b[0], RET_DK)[None, :],
                    ret_gn_w[0][None, :])

    wo = w_out[0].astype(BF16)
    w_router = _lane_pad(jnp.concatenate(
        [jnp.transpose(moe_w_re[0], (1, 0, 2)).reshape(d, N_EXPERTS), moe_w_rg[0]], axis=1))
    b_router = _lane_pad(jnp.concatenate([moe_b_re[0].reshape(-1), moe_b_rg[0]])[None, :])
    x1, h2, route, lpos, seg = _outproj_router(
        yr.reshape(T, rw), ys.reshape(T, SSD_WIDTH), x.reshape(T, d), mod3,
        norm_post_mix[0][None, :], norm_pre_ffn[0][None, :], wo[:rw], wo[rw:], w_router, b_router, L)

    mb = MB_EXPERT
    nt = T // TM_OUT
    n_blocks = -(-(T * TOP_K + nt * N_EXPERTS * (SUBLANES - 1) + N_EXPERTS * (mb - 1)) // mb)
    seg_len = seg[:, 0, :N_EXPERTS]
    seg_off = seg[:, 1, :N_EXPERTS]
    used = jnp.sum(seg_len, axis=0)
    padded = (used + mb - 1) // mb * mb
    pad_end = jnp.cumsum(padded)
    pad_start = pad_end - padded
    seg_row = pad_start[None, :] + jnp.cumsum(seg_len, axis=0) - seg_len
    seg_plan = [a.reshape(-1).astype(jnp.int32)
                for a in (seg_row, seg_len // SUBLANES, seg_off, jnp.sum(seg_len, axis=1))]
    zero_from = (pad_start + used).astype(jnp.int32)

    xs = _dispatch_seg([pad_end.astype(jnp.int32), zero_from] + seg_plan, lpos, route, h2, n_blocks * mb)
    yb = _experts(_expert_plan(padded, pad_end, n_blocks, mb), xs, moe_w_gate[0], moe_w_up[0], moe_w_down[0])
    out = _combine_seg(seg_plan, route, x1, mod3, norm_post_ffn[0][None, :], yb, L)
    return out.reshape(b, L, d)
```

```python
import jax
import jax.numpy as jnp
from jax import lax
from jax.experimental import pallas as pl
from jax.experimental.pallas import tpu as pltpu

F32 = jnp.float32
BF16 = jnp.bfloat16

LANES = 128
SUBLANES = 8
V7X_VMEM_BYTES = 64 * 1024 * 1024
VMEM_LIMIT = V7X_VMEM_BYTES * 3 // 4
VMEM_LIMIT_SSD = V7X_VMEM_BYTES * 7 // 8

EPS = 1e-6
CHUNK = 128
GRID_W = 64
RET_HEADS = 4
RET_DK = 128
ROPE_BASE = 10000.0
SSD_HEADS = 8
SSD_HEADDIM = 64
SSD_GROUPS = 2
SSD_STATE = 128
SSD_WIDTH = SSD_HEADS * SSD_HEADDIM
SSD_CONV = 5
SSD_PAIRS = SSD_WIDTH // LANES
MOE_GROUPS = 4
EXPERTS_PER_GROUP = 8
N_EXPERTS = MOE_GROUPS * EXPERTS_PER_GROUP
TOP_K = 2
CONV_HALO = SUBLANES

TM_PROJ = 512
TM_OUT = 512
TD_DISPATCH = TM_OUT
MB_EXPERT = 512
ZERO_PIECE = 64
TF_COMBINE = TM_OUT
RET_UNROLL = 2
NEG_BIG = -1e30


def _silu(v):
    return v * jax.nn.sigmoid(v)


def _dot(a, b):
    return jnp.dot(a, b, preferred_element_type=F32)


def _dot_tn(a, b):
    return lax.dot_general(a, b, (((0,), (0,)), ((), ())), preferred_element_type=F32)


def _dot_nt(a, b):
    return lax.dot_general(a, b, (((1,), (1,)), ((), ())), preferred_element_type=F32)


def _mod_kernel(c_ref, w_ref, b_ref, o_ref):
    a = _silu(c_ref[...])
    w = w_ref[...]
    a_hi = a.astype(BF16)
    a_lo = (a - a_hi.astype(F32)).astype(BF16)
    w_hi = w.astype(BF16)
    w_lo = (w - w_hi.astype(F32)).astype(BF16)
    o_ref[...] = _dot(a_hi, w_hi) + _dot(a_lo, w_hi) + _dot(a_hi, w_lo) + b_ref[...]


def _modulation(c_all, w_mod, b_mod):
    rows, d = c_all.shape
    n = w_mod.shape[1]
    return pl.pallas_call(
        _mod_kernel,
        grid=(n // d,),
        in_specs=[
            pl.BlockSpec((rows, d), lambda j: (0, 0)),
            pl.BlockSpec((d, d), lambda j: (0, j)),
            pl.BlockSpec((1, d), lambda j: (0, j)),
        ],
        out_specs=pl.BlockSpec((rows, d), lambda j: (0, j)),
        out_shape=jax.ShapeDtypeStruct((rows, n), F32),
        name="modulation",
    )(c_all, w_mod, b_mod.reshape(1, n))


def _norm_mod(x, nw, sc, sh):
    ms = jnp.mean(x * x, axis=-1, keepdims=True)
    return (x * lax.rsqrt(ms + EPS)) * (nw * (1.0 + sc)) + sh


def _rope(t, cos, sin_signed, first_half):
    width = t.shape[-1]
    quarter = RET_DK // 4
    swapped = jnp.where(first_half, pltpu.roll(t, width - quarter, 1), pltpu.roll(t, quarter, 1))
    return t * cos + swapped * sin_signed


def _inproj_kernel(x_ref, sh_ref, sc_ref, nw_ref, wqk_ref, wvgz_ref, wxbc_ref, wdt_ref, cos_ref, sin_ref,
                   q_ref, k_ref, v_ref, g_ref, z_ref, xbc_ref, dt_ref):
    hb = _norm_mod(x_ref[0], nw_ref[...], sc_ref[0], sh_ref[0]).astype(BF16)
    rw = q_ref.shape[-1]
    qk = _dot(hb, wqk_ref[...])
    cos = cos_ref[...]
    sin = sin_ref[...]
    lane = lax.broadcasted_iota(jnp.int32, cos.shape, 1)
    first_half = (lane % (RET_DK // 2)) < (RET_DK // 4)
    q_ref[0] = _rope(qk[:, :rw], cos, sin, first_half).astype(BF16)
    k_ref[0] = (_rope(qk[:, rw:], cos, sin, first_half) * (RET_DK ** -0.5)).astype(BF16)
    vgz = _dot(hb, wvgz_ref[...])
    v_ref[0] = vgz[:, :rw].astype(BF16)
    g_ref[0] = vgz[:, rw:2 * rw].astype(BF16)
    z_ref[0] = vgz[:, 2 * rw:].astype(BF16)
    xbc_ref[0] = _dot(hb, wxbc_ref[...]).astype(BF16)
    dt_ref[0] = _dot(hb, wdt_ref[...])


def _inproj_ctx_kernel(x_ref, sh_ref, sc_ref, nw_ref, wk_ref, wv_ref, wxbc_ref, wdt_ref,
                       k_ref, v_ref, xbc_ref, dt_ref):
    hb = _norm_mod(x_ref[0], nw_ref[...], sc_ref[0], sh_ref[0]).astype(BF16)
    k_ref[0] = (_dot(hb, wk_ref[...]) * (RET_DK ** -0.5)).astype(BF16)
    v_ref[0] = _dot(hb, wv_ref[...]).astype(BF16)
    xbc_ref[0] = _dot(hb, wxbc_ref[...]).astype(BF16)
    dt_ref[0] = _dot(hb, wdt_ref[...])


def _const_spec(shape):
    nd = len(shape)
    return pl.BlockSpec(shape, lambda *_: (0,) * nd)


def _inproj(x, mod3, nw, wqk, wvgz, wxbc, wdt, cos_t, sin_t):
    b, L, d = x.shape
    tm = min(TM_PROJ, L)
    rw = wqk.shape[1] // 2
    tok = lambda w: pl.BlockSpec((1, tm, w), lambda i, j: (i, j, 0))
    out_bf = lambda w: jax.ShapeDtypeStruct((b, L, w), BF16)
    return pl.pallas_call(
        _inproj_kernel,
        grid=(b, L // tm),
        in_specs=[
            tok(d),
            pl.BlockSpec((1, 1, d), lambda i, j: (i, 0, 0)),
            pl.BlockSpec((1, 1, d), lambda i, j: (i, 0, 1)),
            _const_spec((1, d)),
            _const_spec(wqk.shape), _const_spec(wvgz.shape), _const_spec(wxbc.shape), _const_spec(wdt.shape),
            pl.BlockSpec((tm, rw), lambda i, j: (j, 0)),
            pl.BlockSpec((tm, rw), lambda i, j: (j, 0)),
        ],
        out_specs=[tok(rw), tok(rw), tok(rw), tok(rw), tok(rw), tok(wxbc.shape[1]), tok(LANES)],
        out_shape=[out_bf(rw), out_bf(rw), out_bf(rw), out_bf(rw), out_bf(rw), out_bf(wxbc.shape[1]),
                   jax.ShapeDtypeStruct((b, L, LANES), F32)],
        compiler_params=pltpu.CompilerParams(vmem_limit_bytes=VMEM_LIMIT),
        name="inproj",
    )(x, mod3, mod3, nw, wqk, wvgz, wxbc, wdt, cos_t, sin_t)


def _inproj_ctx(ctx, mod3, ctx_row, nw, wk, wv, wxbc, wdt):
    b, L, d = ctx.shape
    tm = min(TM_PROJ, L)
    rw = wk.shape[1]
    tok = lambda w: pl.BlockSpec((1, tm, w), lambda i, j: (i, j, 0))
    out_bf = lambda w: jax.ShapeDtypeStruct((b, L, w), BF16)
    return pl.pallas_call(
        _inproj_ctx_kernel,
        grid=(b, L // tm),
        in_specs=[
            tok(d),
            pl.BlockSpec((1, 1, d), lambda i, j: (ctx_row, 0, 0)),
            pl.BlockSpec((1, 1, d), lambda i, j: (ctx_row, 0, 1)),
            _const_spec((1, d)),
            _const_spec(wk.shape), _const_spec(wv.shape), _const_spec(wxbc.shape), _const_spec(wdt.shape),
        ],
        out_specs=[tok(rw), tok(rw), tok(wxbc.shape[1]), tok(LANES)],
        out_shape=[out_bf(rw), out_bf(rw), out_bf(wxbc.shape[1]), jax.ShapeDtypeStruct((b, L, LANES), F32)],
        compiler_params=pltpu.CompilerParams(vmem_limit_bytes=VMEM_LIMIT),
        name="inproj_ctx",
    )(ctx, mod3, mod3, nw, wk, wv, wxbc, wdt)


def _ssd_kernel(xbc_ref, z_ref, dt_ref, xbcc_ref, dtc_ref, cw_ref, cb_ref, dtb_ref, alog_ref, dsk_ref, nw_ref,
                y_ref,
                xpad, xpadc, u, uc, dtv, dav, dtcv, dacv, sf_scr, kb_scr, acum, ecum, dec_scr,
                arow_scr, erow_scr, dtrow_scr):
    L = xbc_ref.shape[1]
    Lc = xbcc_ref.shape[1]
    nch = L // CHUNK
    nchc = Lc // CHUNK
    win = CHUNK + 2 * CONV_HALO
    nconv = xbc_ref.shape[2]
    nh = SSD_HEADS

    def conv_pass(src_ref, pad_ref, dst_ref, n_chunks, length):
        zeros = jnp.zeros((CONV_HALO, nconv), F32)
        pad_ref[0:CONV_HALO, :] = zeros
        pad_ref[CONV_HALO + length:2 * CONV_HALO + length, :] = zeros
        pad_ref[CONV_HALO:CONV_HALO + length, :] = src_ref[0].astype(F32)

        def chunk(c, carry):
            base = pl.multiple_of(c * CHUNK, CHUNK)
            for cb_i in range(nconv // LANES):
                cols = slice(cb_i * LANES, (cb_i + 1) * LANES)
                w = pad_ref[pl.ds(base, win), cols]
                acc = cb_ref[:, cols] + w[CONV_HALO:CONV_HALO + CHUNK] * cw_ref[SSD_CONV // 2:SSD_CONV // 2 + 1, cols]
                for j in range(SSD_CONV):
                    if j == SSD_CONV // 2:
                        continue
                    shifted = pltpu.roll(w, (SSD_CONV // 2 - j) % win, 0)
                    acc = acc + shifted[CONV_HALO:CONV_HALO + CHUNK] * cw_ref[j:j + 1, cols]
                dst_ref[pl.ds(base, CHUNK), cols] = _silu(acc).astype(BF16)
            return carry

        lax.fori_loop(0, n_chunks, chunk, 0)

    conv_pass(xbcc_ref, xpadc, uc, nchc, Lc)
    conv_pass(xbc_ref, xpad, u, nch, L)

    a_neg = -jnp.exp(alog_ref[...])
    dtv[...] = jax.nn.softplus(dt_ref[0] + dtb_ref[...])
    dav[...] = dtv[...] * a_neg
    dtcv[...] = jax.nn.softplus(dtc_ref[0] + dtb_ref[...])
    dacv[...] = dtcv[...] * a_neg

    row_i = lax.broadcasted_iota(jnp.int32, (CHUNK, CHUNK), 0)
    col_i = lax.broadcasted_iota(jnp.int32, (CHUNK, CHUNK), 1)
    causal = col_i <= row_i
    lo_half = col_i < SSD_HEADDIM
    fwd_lane = col_i < nh
    head_of = lax.broadcasted_iota(jnp.int32, (CHUNK, SSD_WIDTH), 1) // SSD_HEADDIM
    src_col = lax.broadcasted_iota(jnp.int32, (CHUNK, SSD_WIDTH), 0)
    exp_f = (head_of == src_col).astype(BF16)
    exp_b = (head_of == src_col - nh).astype(BF16)
    exp_fb = jnp.concatenate([exp_f, exp_b], axis=1)

    def split3(v):
        hi = v.astype(BF16)
        r1 = v - hi.astype(F32)
        mid = r1.astype(BF16)
        return hi, mid, (r1 - mid.astype(F32)).astype(BF16)

    def times_onehot(v, m, passes=3):
        parts = split3(v)[:passes]
        acc = _dot(parts[0], m)
        for part in parts[1:]:
            acc = acc + _dot(part, m)
        return acc

    def colb(mat, r):
        return jnp.broadcast_to(mat[:, r:r + 1], (CHUNK, CHUNK))

    def pair_sel(a, b_):
        return jnp.where(lo_half, a, b_)

    gw = 2 * LANES

    def chunk_terms(u_ref, dt_s, da_s, base):
        dt = dt_s[pl.ds(base, CHUNK), :]
        da = da_s[pl.ds(base, CHUNK), :]
        acol = da
        for step in (1, 2, 4, 8, 16, 32, 64):
            acol = acol + jnp.where(row_i >= step, pltpu.roll(acol, step, 0), 0.0)
        ecol = acol - da
        last = acol[CHUNK - 1:CHUNK, :]
        wgt = jnp.where(fwd_lane, jnp.exp(last - acol), jnp.exp(ecol)) * dt
        scale = jnp.where(fwd_lane, jnp.exp(acol), jnp.exp(last - ecol))
        wide = times_onehot(jnp.concatenate([wgt, scale], axis=0), exp_fb, passes=1)
        dec = times_onehot(jnp.broadcast_to(jnp.exp(last), (SUBLANES, LANES)), exp_fb)[0:1]
        xs = u_ref[pl.ds(base, CHUNK), 0:SSD_WIDTH].astype(F32)
        kmats = []
        for g in range(SSD_GROUPS):
            xw = jnp.concatenate([xs[:, g * gw:(g + 1) * gw] * wide[:CHUNK, g * gw:(g + 1) * gw],
                                  xs[:, g * gw:(g + 1) * gw] * wide[:CHUNK, SSD_WIDTH + g * gw:SSD_WIDTH + (g + 1) * gw]],
                                 axis=1).astype(BF16)
            bm = u_ref[pl.ds(base, CHUNK), SSD_WIDTH + g * SSD_STATE:SSD_WIDTH + (g + 1) * SSD_STATE]
            kmats.append(_dot_tn(bm, xw))
        return dt, acol, ecol, wide[CHUNK:], dec, kmats

    def advance(s, dec, kmats, backward):
        off = SSD_WIDTH if backward else 0
        koff = gw if backward else 0
        return [dec[:, off + g * gw:off + (g + 1) * gw] * s[g] + kmats[g][:, koff:koff + gw]
                for g in range(SSD_GROUPS)]

    ctx_terms = [chunk_terms(uc, dtcv, dacv, c * CHUNK) for c in range(nchc)]
    s_f0 = [jnp.zeros((SSD_STATE, gw), F32) for _ in range(SSD_GROUPS)]
    for c in range(nchc):
        s_f0 = advance(s_f0, ctx_terms[c][4], ctx_terms[c][5], False)
    s_b0 = [jnp.zeros((SSD_STATE, gw), F32) for _ in range(SSD_GROUPS)]
    for c in reversed(range(nchc)):
        s_b0 = advance(s_b0, ctx_terms[c][4], ctx_terms[c][5], True)

    def prep(c, carry):
        base = pl.multiple_of(c * CHUNK, CHUNK)
        dt, acol, ecol, scale, dec, kmats = chunk_terms(u, dtv, dav, base)
        acum[pl.ds(base, CHUNK), :] = acol
        ecum[pl.ds(base, CHUNK), :] = ecol
        hrow = pl.ds(pl.multiple_of(c * 2 * nh, 2 * nh), 2 * nh)
        arow_scr[hrow, :] = acol.T[:2 * nh]
        erow_scr[hrow, :] = ecol.T[:2 * nh]
        dtrow_scr[hrow, :] = dt.T[:2 * nh]
        xpad[pl.ds(base, CHUNK), :] = scale
        dec_scr[pl.ds(pl.multiple_of(c * SUBLANES, SUBLANES), SUBLANES), :] = jnp.broadcast_to(dec, (SUBLANES, 2 * SSD_WIDTH))
        for g in range(SSD_GROUPS):
            sf_scr[c, g] = kmats[g][:, :gw]
            kb_scr[c, g] = kmats[g][:, gw:]
        return carry

    lax.fori_loop(0, nch, prep, 0)

    def chunk_dec(c):
        return dec_scr[pl.ds(pl.multiple_of(c * SUBLANES, SUBLANES), 1), :]

    def fwd(c, s_old):
        dec = chunk_dec(c)
        new = []
        for g in range(SSD_GROUPS):
            new.append(dec[:, g * gw:(g + 1) * gw] * s_old[g] + sf_scr[c, g])
            sf_scr[c, g] = s_old[g]
        return tuple(new)

    lax.fori_loop(0, nch, fwd, tuple(s_f0))

    def bwd(i, s_b):
        c = nch - 1 - i
        base = pl.multiple_of(c * CHUNK, CHUNK)
        acol = acum[pl.ds(base, CHUNK), :]
        ecol = ecum[pl.ds(base, CHUNK), :]
        hrow = pl.ds(pl.multiple_of(c * 2 * nh, 2 * nh), 2 * nh)
        arow = arow_scr[hrow, :]
        erow = erow_scr[hrow, :]
        dt_t = dtrow_scr[hrow, :]
        scale = xpad[pl.ds(base, CHUNK), :]
        ys = []
        for g in range(SSD_GROUPS):
            bm = u[pl.ds(base, CHUNK), SSD_WIDTH + g * SSD_STATE:SSD_WIDTH + (g + 1) * SSD_STATE]
            cm = u[pl.ds(base, CHUNK), SSD_WIDTH + (SSD_GROUPS + g) * SSD_STATE:SSD_WIDTH + (SSD_GROUPS + g + 1) * SSD_STATE]
            cbm = _dot_nt(cm, bm)
            cs_f = _dot(cm, sf_scr[c, g].astype(BF16))
            cs_b = _dot(cm, s_b[g].astype(BF16))
            for pp in range(SSD_PAIRS // SSD_GROUPS):
                p = g * (SSD_PAIRS // SSD_GROUPS) + pp
                xs_b = u[pl.ds(base, CHUNK), p * LANES:(p + 1) * LANES]
                y_h = []
                for hh in range(2):
                    r = 2 * p + hh
                    arg = jnp.where(causal, colb(acol, r) - arow[r:r + 1, :],
                                    erow[nh + r:nh + r + 1, :] - colb(ecol, nh + r))
                    coef = jnp.where(causal, dt_t[r:r + 1, :], dt_t[nh + r:nh + r + 1, :])
                    gm = (cbm * (jnp.exp(arg) * coef)).astype(BF16)
                    y_h.append(_dot(gm, xs_b))
                sl = slice(pp * LANES, (pp + 1) * LANES)
                wl = slice(p * LANES, (p + 1) * LANES)
                wlb = slice(SSD_WIDTH + p * LANES, SSD_WIDTH + (p + 1) * LANES)
                ys.append(pair_sel(y_h[0], y_h[1]) + cs_f[:, sl] * scale[:, wl] + cs_b[:, sl] * scale[:, wlb]
                          + dsk_ref[:, wl] * xs_b.astype(F32))
        y = jnp.concatenate(ys, axis=1)
        y = y * _silu(z_ref[0, pl.ds(base, CHUNK), :].astype(F32))
        ms = jnp.mean(y * y, axis=-1, keepdims=True)
        y_ref[0, pl.ds(base, CHUNK), :] = ((y * lax.rsqrt(ms + EPS)) * nw_ref[...]).astype(BF16)
        dec = chunk_dec(c)
        return tuple(dec[:, SSD_WIDTH + g * gw:SSD_WIDTH + (g + 1) * gw] * s_b[g] + kb_scr[c, g]
                     for g in range(SSD_GROUPS))

    lax.fori_loop(0, nch, bwd, tuple(s_b0))


def _ssd(xbc, z, dt, xbcc, dtc, conv_w8, conv_b, dt_bias, a_log, d_skip, norm_w):
    b, L, nconv = xbc.shape
    Lc = xbcc.shape[1]
    nch = L // CHUNK
    per_b = lambda n, w: pl.BlockSpec((1, n, w), lambda i: (i, 0, 0))
    return pl.pallas_call(
        _ssd_kernel,
        grid=(b,),
        in_specs=[
            per_b(L, nconv), per_b(L, SSD_WIDTH), per_b(L, LANES), per_b(Lc, nconv), per_b(Lc, LANES),
            _const_spec(conv_w8.shape), _const_spec(conv_b.shape), _const_spec(dt_bias.shape),
            _const_spec(a_log.shape), _const_spec(d_skip.shape), _const_spec(norm_w.shape),
        ],
        out_specs=per_b(L, SSD_WIDTH),
        out_shape=jax.ShapeDtypeStruct((b, L, SSD_WIDTH), BF16),
        scratch_shapes=[
            pltpu.VMEM((L + 2 * CONV_HALO, nconv), F32),
            pltpu.VMEM((Lc + 2 * CONV_HALO, nconv), F32),
            pltpu.VMEM((L, nconv), BF16),
            pltpu.VMEM((Lc, nconv), BF16),
            pltpu.VMEM((L, LANES), F32), pltpu.VMEM((L, LANES), F32),
            pltpu.VMEM((Lc, LANES), F32), pltpu.VMEM((Lc, LANES), F32),
            pltpu.VMEM((nch, SSD_GROUPS, SSD_STATE, 2 * LANES), F32),
            pltpu.VMEM((nch, SSD_GROUPS, SSD_STATE, 2 * LANES), F32),
            pltpu.VMEM((L, LANES), F32), pltpu.VMEM((L, LANES), F32),
            pltpu.VMEM((nch * SUBLANES, 2 * SSD_WIDTH), F32),
            pltpu.VMEM((nch * 2 * SSD_HEADS, CHUNK), F32), pltpu.VMEM((nch * 2 * SSD_HEADS, CHUNK), F32),
            pltpu.VMEM((nch * 2 * SSD_HEADS, CHUNK), F32),
        ],
        compiler_params=pltpu.CompilerParams(vmem_limit_bytes=VMEM_LIMIT_SSD),
        name="ssd",
    )(xbc, z, dt, xbcc, dtc, conv_w8, conv_b, dt_bias, a_log, d_skip, norm_w)


def _ret_kernel(q_ref, k_ref, v_ref, g_ref, kc_ref, vc_ref, df_ref, db_ref, gn_ref, y_ref, sf_scr):
    L = q_ref.shape[1]
    Lc = kc_ref.shape[1]
    nch = L // CHUNK
    dk = RET_DK
    row_i = lax.broadcasted_iota(jnp.int32, (CHUNK, dk), 0).astype(F32)
    col_i = lax.broadcasted_iota(jnp.int32, (CHUNK, dk), 1).astype(F32)
    rel = row_i - col_i
    crow = lax.broadcasted_iota(jnp.int32, (Lc, dk), 0).astype(F32)

    heads = []
    s_f0 = []
    s_b0 = []
    for h in range(RET_HEADS):
        cols = slice(h * dk, (h + 1) * dk)
        lg_f = -jnp.exp(df_ref[:, cols])
        lg_b = -jnp.exp(db_ref[:, cols])
        heads.append(dict(
            cols=cols,
            dmat=jnp.where(rel >= 0, jnp.exp(jnp.maximum(rel, 0.0) * lg_f), jnp.exp(jnp.maximum(-rel, 0.0) * lg_b)),
            dq_f=jnp.exp((row_i + 1.0) * lg_f),
            dq_b=jnp.exp((CHUNK - row_i) * lg_b),
            dk_f=jnp.exp((CHUNK - 1.0 - row_i) * lg_f),
            dk_b=jnp.exp(row_i * lg_b),
            dc_f=jnp.exp(CHUNK * lg_f),
            dc_b=jnp.exp(CHUNK * lg_b),
        ))
        kc = kc_ref[0, :, cols].astype(F32)
        vc = vc_ref[0, :, cols]
        s_f0.append(_dot_tn((kc * jnp.exp((Lc - 1.0 - crow) * lg_f)).astype(BF16), vc))
        s_b0.append(_dot_tn((kc * jnp.exp(crow * lg_b)).astype(BF16), vc))

    def fwd(c, s_f):
        base = pl.multiple_of(c * CHUNK, CHUNK)
        new = []
        for h, hd in enumerate(heads):
            sf_scr[c, h] = s_f[h]
            kk = k_ref[0, pl.ds(base, CHUNK), hd["cols"]].astype(F32)
            vv = v_ref[0, pl.ds(base, CHUNK), hd["cols"]]
            new.append(hd["dc_f"] * s_f[h] + _dot_tn((kk * hd["dk_f"]).astype(BF16), vv))
        return tuple(new)

    lax.fori_loop(0, nch, fwd, tuple(s_f0), unroll=RET_UNROLL)

    def bwd(i, s_bs):
        c = nch - 1 - i
        base = pl.multiple_of(c * CHUNK, CHUNK)
        new = []
        for h, hd in enumerate(heads):
            qq = q_ref[0, pl.ds(base, CHUNK), hd["cols"]]
            kk = k_ref[0, pl.ds(base, CHUNK), hd["cols"]]
            vv = v_ref[0, pl.ds(base, CHUNK), hd["cols"]]
            s_b = s_bs[h]
            scores = (_dot_nt(qq, kk) * hd["dmat"]).astype(BF16)
            y = (_dot(scores, vv)
                 + _dot(qq, sf_scr[c, h].astype(BF16)) * hd["dq_f"]
                 + _dot(qq, s_b.astype(BF16)) * hd["dq_b"])
            mu = jnp.mean(y, axis=-1, keepdims=True)
            yc = y - mu
            var = jnp.mean(yc * yc, axis=-1, keepdims=True)
            yn = (yc * lax.rsqrt(var + EPS)) * gn_ref[:, hd["cols"]]
            gate = _silu(g_ref[0, pl.ds(base, CHUNK), hd["cols"]].astype(F32))
            y_ref[0, pl.ds(base, CHUNK), hd["cols"]] = (yn * gate).astype(BF16)
            new.append(hd["dc_b"] * s_b + _dot_tn((kk.astype(F32) * hd["dk_b"]).astype(BF16), vv))
        return tuple(new)

    lax.fori_loop(0, nch, bwd, tuple(s_b0), unroll=RET_UNROLL)


def _retention(q, k, v, g, kc, vc, decay_f, decay_b, gn_w):
    b, L, w = q.shape
    Lc = kc.shape[1]
    nch = L // CHUNK
    per_b = lambda n: pl.BlockSpec((1, n, w), lambda i: (i, 0, 0))
    return pl.pallas_call(
        _ret_kernel,
        grid=(b,),
        in_specs=[per_b(L), per_b(L), per_b(L), per_b(L), per_b(Lc), per_b(Lc),
                  _const_spec((1, w)), _const_spec((1, w)), _const_spec((1, w))],
        out_specs=per_b(L),
        out_shape=jax.ShapeDtypeStruct((b, L, w), BF16),
        scratch_shapes=[
            pltpu.VMEM((nch, RET_HEADS, RET_DK, RET_DK), F32),
        ],
        compiler_params=pltpu.CompilerParams(vmem_limit_bytes=VMEM_LIMIT),
        name="retention",
    )(q, k, v, g, kc, vc, decay_f, decay_b, gn_w)


def _outproj_router_kernel(yr_ref, ys_ref, x_ref, g1_ref, sh2_ref, sc2_ref, npost_ref, npre_ref,
                           wor_ref, wos_ref, wr_ref, br_ref, tri_ref,
                           x1_ref, h2_ref, route_ref, slots_ref, seg_ref,
                           wcat):
    i = pl.program_id(0)

    @pl.when(i == 0)
    def _():
        wr = wr_ref[...]
        hi = wr.astype(BF16)
        wcat[:, :LANES] = hi
        wcat[:, LANES:] = (wr - hi.astype(F32)).astype(BF16)

    _route_tile(yr_ref, ys_ref, x_ref, g1_ref, sh2_ref, sc2_ref, npost_ref, npre_ref, wor_ref, wos_ref, br_ref,
                tri_ref, x1_ref, h2_ref, route_ref, slots_ref, seg_ref, wcat)


def _route_tile(yr_ref, ys_ref, x_ref, g1_ref, sh2_ref, sc2_ref, npost_ref, npre_ref, wor_ref, wos_ref, br_ref,
                tri_ref, x1_ref, h2_ref, route_ref, slots_ref, seg_ref, wcat):
    tm = x_ref.shape[0]
    rows = slice(0, tm)
    y = _dot(yr_ref[rows, :], wor_ref[...]) + _dot(ys_ref[rows, :], wos_ref[...])
    ms = jnp.mean(y * y, axis=-1, keepdims=True)
    x1 = x_ref[rows, :] + (y * lax.rsqrt(ms + EPS)) * (g1_ref[0] * npost_ref[...])
    x1_ref[rows, :] = x1
    h2 = _norm_mod(x1, npre_ref[...], sc2_ref[0], sh2_ref[0])
    h2_ref[rows, :] = h2

    h_hi = h2.astype(BF16)
    h_lo = (h2 - h_hi.astype(F32)).astype(BF16)
    both = _dot(h_hi, wcat[...])
    lg = both[:, :LANES] + both[:, LANES:] + _dot(h_lo, wcat[:, :LANES]) + br_ref[...]

    lane = lax.broadcasted_iota(jnp.int32, (tm, LANES), 1)
    lane_f = lane.astype(F32)
    is_grp = (lane >= N_EXPERTS) & (lane < N_EXPERTS + MOE_GROUPS)
    gl = jnp.where(is_grp, lg, NEG_BIG)
    mg = jnp.max(gl, axis=-1, keepdims=True)
    grp_lane = jnp.min(jnp.where(gl == mg, lane_f, 1e9), axis=-1, keepdims=True)
    p_g = 1.0 / jnp.sum(jnp.where(is_grp, jnp.exp(gl - mg), 0.0), axis=-1, keepdims=True)
    first = (grp_lane - N_EXPERTS) * EXPERTS_PER_GROUP
    in_grp = (lane_f >= first) & (lane_f < first + EXPERTS_PER_GROUP)
    el = jnp.where(in_grp, lg, NEG_BIG)
    t1 = jnp.max(el, axis=-1, keepdims=True)
    i1 = jnp.min(jnp.where(el == t1, lane_f, 1e9), axis=-1, keepdims=True)
    el2 = jnp.where(lane_f == i1, NEG_BIG, el)
    t2 = jnp.max(el2, axis=-1, keepdims=True)
    i2 = jnp.min(jnp.where(el2 == t2, lane_f, 1e9), axis=-1, keepdims=True)
    s = jnp.exp(t2 - t1)
    w1 = p_g / (1.0 + s)
    w2 = p_g * s / (1.0 + s)

    oh1 = (lane_f == i1)
    oh2 = (lane_f == i2)
    ohf = jnp.where(oh1 | oh2, 1.0, 0.0)
    before = _dot(tri_ref[...], ohf.astype(BF16))
    cnt = jnp.sum(ohf, axis=0, keepdims=True)
    seg = jnp.floor((cnt + (SUBLANES - 1.0)) * (1.0 / SUBLANES)) * SUBLANES
    e_row = lax.broadcasted_iota(jnp.int32, (LANES, LANES), 0)
    e_col = lax.broadcasted_iota(jnp.int32, (LANES, LANES), 1)
    earlier = (e_row < e_col).astype(BF16)
    seg_off = _dot(jnp.broadcast_to(seg, (SUBLANES, LANES)).astype(BF16), earlier)[0:1]
    where_to = before + seg_off
    lpos1 = jnp.sum(jnp.where(oh1, where_to, 0.0), axis=-1, keepdims=True)
    lpos2 = jnp.sum(jnp.where(oh2, where_to, 0.0), axis=-1, keepdims=True)

    cols = [w1, w2, lpos1, lpos2]
    for wk in (w1, w2):
        hi = wk.astype(BF16).astype(F32)
        cols += [hi, wk - hi]
    cols.append(jnp.ones_like(w1))
    packed = jnp.zeros((tm, LANES), F32)
    for k, col in enumerate(cols):
        packed = jnp.where(lane == k, col, packed)
    route_ref[rows, :] = packed

    row = lax.broadcasted_iota(jnp.int32, (tm, LANES), 0)
    on_diag = (row % LANES) == lane
    per = tm // LANES
    for qi, col in enumerate((lpos1, lpos2)):
        picked = jnp.where(on_diag, col, 0.0)
        dense = jnp.sum(picked.reshape(per, LANES, LANES), axis=1).astype(jnp.int32)
        slots_ref[0, qi * per:(qi + 1) * per, :] = dense
    tbl_row = lax.broadcasted_iota(jnp.int32, (SUBLANES, LANES), 0)
    seg_ref[0] = jnp.where(tbl_row == 0, seg, jnp.where(tbl_row == 1, seg_off, 0.0)).astype(jnp.int32)


def _outproj_router(yr, ys, x2, mod3, npost, npre, wo_r, wo_s, w_router, b_router, seq_len):
    T, d = x2.shape
    tm = TM_OUT
    per_seq = seq_len // tm
    rw = yr.shape[1]
    tri = (jnp.arange(tm)[:, None] > jnp.arange(tm)[None, :]).astype(BF16)
    tok = lambda w: pl.BlockSpec((tm, w), lambda i: (i, 0))
    modv = lambda k: pl.BlockSpec((1, 1, d), lambda i: (i // per_seq, 0, k))
    tile3 = lambda r: pl.BlockSpec((1, r, LANES), lambda i: (i, 0, 0))
    slot_rows = TOP_K * (tm // LANES)
    return pl.pallas_call(
        _outproj_router_kernel,
        grid=(T // tm,),
        in_specs=[
            tok(rw), tok(rw), tok(d), modv(2), modv(3), modv(4),
            _const_spec((1, d)), _const_spec((1, d)),
            _const_spec(wo_r.shape), _const_spec(wo_s.shape), _const_spec(w_router.shape), _const_spec((1, LANES)),
            _const_spec((tm, tm)),
        ],
        out_specs=[tok(d), tok(d), tok(LANES), tile3(slot_rows), tile3(SUBLANES)],
        out_shape=[jax.ShapeDtypeStruct((T, d), F32), jax.ShapeDtypeStruct((T, d), F32),
                   jax.ShapeDtypeStruct((T, LANES), F32),
                   jax.ShapeDtypeStruct((T // tm, slot_rows, LANES), jnp.int32),
                   jax.ShapeDtypeStruct((T // tm, SUBLANES, LANES), jnp.int32)],
        scratch_shapes=[pltpu.VMEM((d, 2 * LANES), BF16)],
        compiler_params=pltpu.CompilerParams(dimension_semantics=("arbitrary",),
                                             vmem_limit_bytes=VMEM_LIMIT),
        name="outproj_router",
    )(yr, ys, x2, mod3, mod3, mod3, npost, npre, wo_r, wo_s, w_router, b_router, tri)


def _expert_kernel(be_ref, first_ref, slot_ref, next_ref, nused_ref, xs_ref, wg_hbm, wu_hbm, wd_hbm, y_ref,
                   wg_f, wu_f, wd_f, wg_b, wu_b, wd_b, sem):
    i = pl.program_id(0)

    def fetch(e, s):
        return [pltpu.make_async_copy(src.at[e], dst.at[s], sem.at[s, k])
                for k, (src, dst) in enumerate(((wg_hbm, wg_f), (wu_hbm, wu_f), (wd_hbm, wd_f)))]

    @pl.when(i == 0)
    def _():
        for c in fetch(be_ref[0], slot_ref[0]):
            c.start()

    @pl.when(first_ref[i] == 1)
    def _():
        s = slot_ref[i]

        @pl.when(next_ref[i] >= 0)
        def _():
            for c in fetch(next_ref[i], 1 - s):
                c.start()

        for c in fetch(be_ref[i], s):
            c.wait()
        wg_b[...] = wg_f[s].astype(BF16)
        wu_b[...] = wu_f[s].astype(BF16)
        wd_b[...] = wd_f[s].astype(BF16)

    @pl.when(i < nused_ref[0])
    def _():
        d = y_ref.shape[1]
        side = xs_ref[:, d:d + LANES]
        second = side[:, 8:9] == 2.0
        unscale = jnp.where(second, 0.5, 1.0)
        weight = jnp.where(second, side[:, 6:7] + side[:, 7:8], side[:, 4:5] + side[:, 5:6]) * unscale
        xb = (xs_ref[:, 0:d] * unscale).astype(BF16)
        hid = (_silu(_dot(xb, wg_b[...])) * _dot(xb, wu_b[...])).astype(BF16)
        y_ref[...] = (_dot(hid, wd_b[...]) * weight).astype(BF16).astype(F32)

    @pl.when(i >= nused_ref[0])
    def _():
        y_ref[...] = jnp.zeros_like(y_ref)


def _expert_plan(padded, pad_end, n_blocks, mb):
    n_used = (pad_end[-1:] // mb).astype(jnp.int32)
    blk_start = jnp.arange(n_blocks, dtype=jnp.int32) * mb
    experts = jnp.arange(N_EXPERTS, dtype=jnp.int32)
    blk_expert = jnp.minimum(jnp.sum((pad_end[None, :] <= blk_start[:, None]).astype(jnp.int32), axis=1),
                             N_EXPERTS - 1)
    prev = jnp.concatenate([jnp.full((1,), -1, jnp.int32), blk_expert[:-1]])
    first = ((blk_start < pad_end[-1]) & (blk_expert != prev)).astype(jnp.int32)
    slot = (jnp.cumsum(first) - 1) % 2
    later = jnp.where((padded > 0)[None, :] & (experts[None, :] > experts[:, None]), experts[None, :], N_EXPERTS)
    next_nonempty = jnp.min(later, axis=1)
    next_nonempty = jnp.where(next_nonempty == N_EXPERTS, -1, next_nonempty)
    nxt = jnp.sum(jnp.where(blk_expert[:, None] == experts[None, :], next_nonempty[None, :], 0), axis=1)
    return [a.astype(jnp.int32) for a in (blk_expert, first, slot, nxt, n_used)]


def _experts(plan, xs, w_gate, w_up, w_down):
    rows, dp = xs.shape
    n_exp, d, de = w_gate.shape
    assert dp == d + LANES
    mb = MB_EXPERT
    cap = rows // mb * mb
    grid_spec = pltpu.PrefetchScalarGridSpec(
        num_scalar_prefetch=len(plan),
        grid=(cap // mb,),
        in_specs=[
            pl.BlockSpec((mb, dp), lambda i, be, fi, sl, nx, nu: (jnp.minimum(i, nu[0] - 1), 0)),
            pl.BlockSpec(memory_space=pl.ANY), pl.BlockSpec(memory_space=pl.ANY), pl.BlockSpec(memory_space=pl.ANY),
        ],
        out_specs=pl.BlockSpec((mb, d), lambda i, be, fi, sl, nx, nu: (i, 0)),
        scratch_shapes=[pltpu.VMEM((2, d, de), F32), pltpu.VMEM((2, d, de), F32), pltpu.VMEM((2, de, d), F32),
                        pltpu.VMEM((d, de), BF16), pltpu.VMEM((d, de), BF16), pltpu.VMEM((de, d), BF16),
                        pltpu.SemaphoreType.DMA((2, 3))],
    )
    return pl.pallas_call(
        _expert_kernel,
        grid_spec=grid_spec,
        out_shape=jax.ShapeDtypeStruct((cap, d), F32),
        compiler_params=pltpu.CompilerParams(dimension_semantics=("arbitrary",),
                                             vmem_limit_bytes=VMEM_LIMIT),
        name="experts",
    )(*plan, xs, w_gate, w_up, w_down)


def _segment_pieces(tile, segrow_ref, seglen_ref, segoff_ref, act):
    def per_expert(e, carry):
        idx = tile * N_EXPERTS + e
        g0 = segrow_ref[idx]
        l0 = segoff_ref[idx]

        def piece(j, c2):
            act(pl.multiple_of(l0 + j * SUBLANES, SUBLANES), pl.multiple_of(g0 + j * SUBLANES, SUBLANES))
            return c2

        lax.fori_loop(0, seglen_ref[idx], piece, 0)
        return carry

    lax.fori_loop(0, N_EXPERTS, per_expert, 0)


def _wait_rows(total, row_copy):
    size = SUBLANES
    while size <= _stage_rows(max(TD_DISPATCH, TF_COMBINE)):
        @pl.when((total & size) != 0)
        def _(size=size):
            row_copy(size).wait()
        size *= 2


def _stage_rows(tile_tokens):
    return TOP_K * tile_tokens + N_EXPERTS * SUBLANES


def _dispatch_seg_kernel(pad_end_ref, zero_from_ref, segrow_ref, seglen_ref, segoff_ref, tilerows_ref,
                         lpos_ref, route_ref, h_ref, xs_hbm, zbuf, stage, sem, zsem):
    i = pl.program_id(0)
    nt = pl.num_programs(0)
    td, d = h_ref.shape
    sr = stage.shape[1]
    per = td // LANES

    @pl.when(i == 0)
    def _():
        zbuf[...] = jnp.zeros_like(zbuf)

        def pieces(e, act):
            for p in range(MB_EXPERT // ZERO_PIECE):
                row = pad_end_ref[e] - (p + 1) * ZERO_PIECE

                @pl.when(row + ZERO_PIECE > zero_from_ref[e])
                def _():
                    act(pltpu.make_async_copy(zbuf, xs_hbm.at[pl.ds(pl.multiple_of(row, ZERO_PIECE), ZERO_PIECE)], zsem))

        def start(e, carry):
            pieces(e, lambda c: c.start())
            return carry

        def wait(e, carry):
            pieces(e, lambda c: c.wait())
            return carry

        def tail(p, act):
            row = pad_end_ref[N_EXPERTS - 1] + p * ZERO_PIECE

            @pl.when(row + ZERO_PIECE <= xs_hbm.shape[0])
            def _():
                act(pltpu.make_async_copy(zbuf, xs_hbm.at[pl.ds(pl.multiple_of(row, SUBLANES), ZERO_PIECE)], zsem))

        n_tail = xs_hbm.shape[0] // ZERO_PIECE
        lax.fori_loop(0, N_EXPERTS, start, 0)
        lax.fori_loop(0, n_tail, lambda p, c: (tail(p, lambda cp: cp.start()), c)[1], 0)
        lax.fori_loop(0, N_EXPERTS, wait, 0)
        lax.fori_loop(0, n_tail, lambda p, c: (tail(p, lambda cp: cp.wait()), c)[1], 0)

    def shipped(tile):
        _wait_rows(tilerows_ref[tile], lambda n: pltpu.make_async_copy(
            stage.at[tile % 2, pl.ds(0, n)], xs_hbm.at[pl.ds(0, n)], sem.at[tile % 2]))

    @pl.when(i >= 2)
    def _():
        shipped(i - 2)

    srow = lax.broadcasted_iota(jnp.int32, (sr, LANES), 0)
    place = jnp.concatenate(
        [jnp.where(srow == lpos_ref[0, cb:cb + 1, :], 1.0,
                   jnp.where(srow == lpos_ref[0, per + cb:per + cb + 1, :], 2.0, 0.0)) for cb in range(per)],
        axis=1).astype(BF16)
    staged = _dot(place, jnp.concatenate([h_ref[...].astype(BF16), route_ref[...].astype(BF16)], axis=1))

    for slot in range(2):
        @pl.when(i % 2 == slot)
        def _(slot=slot):
            stage[slot] = staged
            _segment_pieces(i, segrow_ref, seglen_ref, segoff_ref, lambda lr, gr: pltpu.make_async_copy(
                stage.at[slot, pl.ds(lr, SUBLANES)], xs_hbm.at[pl.ds(gr, SUBLANES)], sem.at[slot]).start())

    @pl.when(i == nt - 1)
    def _():
        @pl.when(i >= 1)
        def _():
            shipped(i - 1)
        shipped(i)


def _dispatch_seg(plan, lpos, route, h2, cap):
    T, d = h2.shape
    td = TD_DISPATCH
    nt = T // td
    width = d + LANES
    sr = _stage_rows(td)
    n_pre = len(plan)
    grid_spec = pltpu.PrefetchScalarGridSpec(
        num_scalar_prefetch=n_pre,
        grid=(nt,),
        in_specs=[
            pl.BlockSpec((1, lpos.shape[1], LANES), lambda i, *_: (i, 0, 0)),
            pl.BlockSpec((td, LANES), lambda i, *_: (i, 0)),
            pl.BlockSpec((td, d), lambda i, *_: (i, 0)),
        ],
        out_specs=pl.BlockSpec(memory_space=pl.ANY),
        scratch_shapes=[pltpu.VMEM((ZERO_PIECE, width), F32), pltpu.VMEM((2, sr, width), F32),
                        pltpu.SemaphoreType.DMA((2,)), pltpu.SemaphoreType.DMA(())],
    )
    return pl.pallas_call(
        _dispatch_seg_kernel,
        grid_spec=grid_spec,
        out_shape=jax.ShapeDtypeStruct((cap, width), F32),
        compiler_params=pltpu.CompilerParams(dimension_semantics=("arbitrary",), vmem_limit_bytes=VMEM_LIMIT),
        name="dispatch",
    )(*plan, lpos, route, h2)


def _combine_seg_kernel(segrow_ref, seglen_ref, segoff_ref, tilerows_ref, route_ref, x1_ref, g2_ref, nw_ref,
                        yb_hbm, o_ref, stage, sem):
    i = pl.program_id(0)
    nt = pl.num_programs(0)
    tf = x1_ref.shape[0]
    sr = stage.shape[1]

    def fetch(tile, slot):
        _segment_pieces(tile, segrow_ref, seglen_ref, segoff_ref, lambda lr, gr: pltpu.make_async_copy(
            yb_hbm.at[pl.ds(gr, SUBLANES)], stage.at[slot, pl.ds(lr, SUBLANES)], sem.at[slot]).start())

    @pl.when(i == 0)
    def _():
        stage[...] = jnp.zeros_like(stage)
        fetch(i, 0)

    for slot in range(2):
        @pl.when((i + 1 < nt) & (i % 2 != slot))
        def _(slot=slot):
            fetch(i + 1, slot)

    _wait_rows(tilerows_ref[i], lambda n: pltpu.make_async_copy(
        yb_hbm.at[pl.ds(0, n)], stage.at[i % 2, pl.ds(0, n)], sem.at[i % 2]))

    route = route_ref[...]
    scol = lax.broadcasted_iota(jnp.int32, (tf, sr), 1).astype(F32)
    pick = jnp.where((scol == route[:, 2:3]) | (scol == route[:, 3:4]), 1.0, 0.0).astype(BF16)
    out = _dot(pick, stage[i % 2].astype(BF16))
    ms = jnp.mean(out * out, axis=-1, keepdims=True)
    o_ref[...] = x1_ref[...] + g2_ref[0] * ((out * lax.rsqrt(ms + EPS)) * nw_ref[...])


def _combine_seg(seg_plan, route, x1, mod3, nw, yb, seq_len):
    T, d = x1.shape
    tf = TF_COMBINE
    nt = T // tf
    per_seq = seq_len // tf
    grid_spec = pltpu.PrefetchScalarGridSpec(
        num_scalar_prefetch=len(seg_plan),
        grid=(nt,),
        in_specs=[
            pl.BlockSpec((tf, LANES), lambda i, *_: (i, 0)),
            pl.BlockSpec((tf, d), lambda i, *_: (i, 0)),
            pl.BlockSpec((1, 1, d), lambda i, *_: (i // per_seq, 0, 5)),
            pl.BlockSpec((1, d), lambda i, *_: (0, 0)),
            pl.BlockSpec(memory_space=pl.ANY),
        ],
        out_specs=pl.BlockSpec((tf, d), lambda i, *_: (i, 0)),
        scratch_shapes=[pltpu.VMEM((2, _stage_rows(tf), d), F32), pltpu.SemaphoreType.DMA((2,))],
    )
    return pl.pallas_call(
        _combine_seg_kernel,
        grid_spec=grid_spec,
        out_shape=jax.ShapeDtypeStruct((T, d), F32),
        compiler_params=pltpu.CompilerParams(dimension_semantics=("arbitrary",), vmem_limit_bytes=VMEM_LIMIT),
        name="combine",
    )(*seg_plan, route, x1, mod3, nw, yb)


def _rope_tables(L, n_heads):
    quarter = RET_DK // 4
    freqs = ROPE_BASE ** (-jnp.arange(quarter, dtype=F32) / quarter)
    t = jnp.arange(L)
    ang_r = (t // GRID_W).astype(F32)[:, None] * freqs
    ang_c = (t % GRID_W).astype(F32)[:, None] * freqs
    cos = jnp.concatenate([jnp.cos(ang_r)] * 2 + [jnp.cos(ang_c)] * 2, axis=-1)
    sin = jnp.concatenate([-jnp.sin(ang_r), jnp.sin(ang_r), -jnp.sin(ang_c), jnp.sin(ang_c)], axis=-1)
    return jnp.tile(cos, (1, n_heads)), jnp.tile(sin, (1, n_heads))


def _lane_pad(v, width=LANES):
    return jnp.pad(v, [(0, 0)] * (v.ndim - 1) + [(0, width - v.shape[-1])])


def kernel(x, c, ctx, c_ctx, w_mod, b_mod, norm_pre_mix, norm_post_mix, norm_pre_ffn, norm_post_ffn, w_in, w_out, ret_decay_f, ret_decay_b, ret_gn_w, ssd_conv_w, ssd_conv_b, ssd_dt_bias_f, ssd_dt_bias_b, ssd_a_log_f, ssd_a_log_b, ssd_d, ssd_norm_w, moe_w_rg, moe_b_rg, moe_w_re, moe_b_re, moe_w_gate, moe_w_up, moe_w_down):
    b, L, d = x.shape
    assert w_mod.shape[0] == 1, "single layer: context outputs are never needed"
    assert TM_OUT == TD_DISPATCH == TF_COMBINE, "router, dispatch and combine share one slot-row layout"
    rw = RET_HEADS * RET_DK
    nconv = SSD_WIDTH + 2 * SSD_GROUPS * SSD_STATE
    T = b * L

    mod_rows = -(-(b + 1) // SUBLANES) * SUBLANES
    c_all = jnp.zeros((mod_rows, d), F32).at[:b].set(c).at[b].set(c_ctx)
    mod3 = _modulation(c_all, w_mod[0], b_mod[0]).reshape(mod_rows, 1, 6 * d)

    wi = w_in[0]
    o = 0
    wq = wi[:, o:o + rw]; o += rw
    wk = wi[:, o:o + rw]; o += rw
    wv = wi[:, o:o + rw]; o += rw
    wg = wi[:, o:o + rw]; o += rw
    wz = wi[:, o:o + SSD_WIDTH]; o += SSD_WIDTH
    wxbc = wi[:, o:o + nconv].astype(BF16); o += nconv
    wdt = _lane_pad(wi[:, o:o + 2 * SSD_HEADS]).astype(BF16)
    wqk = jnp.concatenate([wq, wk], axis=1).astype(BF16)
    wvgz = jnp.concatenate([wv, wg, wz], axis=1).astype(BF16)
    cos_t, sin_t = _rope_tables(L, RET_HEADS)
    nw1 = norm_pre_mix[0].reshape(1, d)

    q, k, v, g, z, xbc, dt = _inproj(x, mod3, nw1, wqk, wvgz, wxbc, wdt, cos_t, sin_t)
    kc, vc, xbcc, dtc = _inproj_ctx(ctx, mod3, b, nw1, wk.astype(BF16), wv.astype(BF16), wxbc, wdt)

    conv_w8 = jnp.pad(ssd_conv_w[0], ((0, SUBLANES - SSD_CONV), (0, 0)))
    dt_bias = _lane_pad(jnp.concatenate([ssd_dt_bias_f[0], ssd_dt_bias_b[0]])[None, :])
    a_log = _lane_pad(jnp.concatenate([ssd_a_log_f[0], ssd_a_log_b[0]])[None, :])
    d_skip = jnp.repeat(ssd_d[0], SSD_HEADDIM)[None, :]
    ys = _ssd(xbc, z, dt, xbcc, dtc, conv_w8, ssd_conv_b[0][None, :], dt_bias, a_log, d_skip,
              ssd_norm_w[0][None, :])

    yr = _retention(q, k, v, g, kc, vc,
                    jnp.repeat(ret_decay_f[0], RET_DK)[None, :], jnp.repeat(ret_decay_b[0], RET_DK)[None, :],
                    ret_gn_w[0][None, :])

    wo = w_out[0].astype(BF16)
    w_router = _lane_pad(jnp.concatenate(
        [jnp.transpose(moe_w_re[0], (1, 0, 2)).reshape(d, N_EXPERTS), moe_w_rg[0]], axis=1))
    b_router = _lane_pad(jnp.concatenate([moe_b_re[0].reshape(-1), moe_b_rg[0]])[None, :])
    x1, h2, route, lpos, seg = _outproj_router(
        yr.reshape(T, rw), ys.reshape(T, SSD_WIDTH), x.reshape(T, d), mod3,
        norm_post_mix[0][None, :], norm_pre_ffn[0][None, :], wo[:rw], wo[rw:], w_router, b_router, L)

    mb = MB_EXPERT
    nt = T // TM_OUT
    n_blocks = -(-(T * TOP_K + nt * N_EXPERTS * (SUBLANES - 1) + N_EXPERTS * (mb - 1)) // mb)
    seg_len = seg[:, 0, :N_EXPERTS]
    seg_off = seg[:, 1, :N_EXPERTS]
    used = jnp.sum(seg_len, axis=0)
    padded = (used + mb - 1) // mb * mb
    pad_end = jnp.cumsum(padded)
    pad_start = pad_end - padded
    seg_row = pad_start[None, :] + jnp.cumsum(seg_len, axis=0) - seg_len
    seg_plan = [a.reshape(-1).astype(jnp.int32)
                for a in (seg_row, seg_len // SUBLANES, seg_off, jnp.sum(seg_len, axis=1))]
    zero_from = (pad_start + used).astype(jnp.int32)

    xs = _dispatch_seg([pad_end.astype(jnp.int32), zero_from] + seg_plan, lpos, route, h2, n_blocks * mb)
    yb = _experts(_expert_plan(padded, pad_end, n_blocks, mb), xs, moe_w_gate[0], moe_w_up[0], moe_w_down[0])
    out = _combine_seg(seg_plan, route, x1, mod3, norm_post_ffn[0][None, :], yb, L)
    return out.reshape(b, L, d)
```

```python
import jax
import jax.numpy as jnp
from jax import lax
from jax.experimental import pallas as pl
from jax.experimental.pallas import tpu as pltpu

F32 = jnp.float32
BF16 = jnp.bfloat16

LANES = 128
SUBLANES = 8
V7X_VMEM_BYTES = 64 * 1024 * 1024
VMEM_LIMIT = V7X_VMEM_BYTES * 3 // 4
VMEM_LIMIT_SSD = V7X_VMEM_BYTES * 7 // 8

EPS = 1e-6
CHUNK = 128
GRID_W = 64
RET_HEADS = 4
RET_DK = 128
ROPE_BASE = 10000.0
SSD_HEADS = 8
SSD_HEADDIM = 64
SSD_GROUPS = 2
SSD_STATE = 128
SSD_WIDTH = SSD_HEADS * SSD_HEADDIM
SSD_CONV = 5
SSD_PAIRS = SSD_WIDTH // LANES
MOE_GROUPS = 4
EXPERTS_PER_GROUP = 8
N_EXPERTS = MOE_GROUPS * EXPERTS_PER_GROUP
TOP_K = 2
CONV_HALO = SUBLANES

TM_PROJ = 512
TM_OUT = 512
TD_DISPATCH = TM_OUT
MB_EXPERT = 512
ZERO_PIECE = 64
TF_COMBINE = TM_OUT
RET_UNROLL = 2
NEG_BIG = -1e30


def _silu(v):
    return v * jax.nn.sigmoid(v)


def _dot(a, b):
    return jnp.dot(a, b, preferred_element_type=F32)


def _dot_tn(a, b):
    return lax.dot_general(a, b, (((0,), (0,)), ((), ())), preferred_element_type=F32)


def _dot_nt(a, b):
    return lax.dot_general(a, b, (((1,), (1,)), ((), ())), preferred_element_type=F32)


def _mod_kernel(c_ref, w_ref, b_ref, o_ref):
    a = _silu(c_ref[...])
    w = w_ref[...]
    a_hi = a.astype(BF16)
    a_lo = (a - a_hi.astype(F32)).astype(BF16)
    w_hi = w.astype(BF16)
    w_lo = (w - w_hi.astype(F32)).astype(BF16)
    o_ref[...] = _dot(a_hi, w_hi) + _dot(a_lo, w_hi) + _dot(a_hi, w_lo) + b_ref[...]


def _modulation(c_all, w_mod, b_mod):
    rows, d = c_all.shape
    n = w_mod.shape[1]
    return pl.pallas_call(
        _mod_kernel,
        grid=(n // d,),
        in_specs=[
            pl.BlockSpec((rows, d), lambda j: (0, 0)),
            pl.BlockSpec((d, d), lambda j: (0, j)),
            pl.BlockSpec((1, d), lambda j: (0, j)),
        ],
        out_specs=pl.BlockSpec((rows, d), lambda j: (0, j)),
        out_shape=jax.ShapeDtypeStruct((rows, n), F32),
        name="modulation",
    )(c_all, w_mod, b_mod.reshape(1, n))


def _norm_mod(x, nw, sc, sh):
    ms = jnp.mean(x * x, axis=-1, keepdims=True)
    return (x * lax.rsqrt(ms + EPS)) * (nw * (1.0 + sc)) + sh


def _rope(t, cos, sin_signed, first_half):
    width = t.shape[-1]
    quarter = RET_DK // 4
    swapped = jnp.where(first_half, pltpu.roll(t, width - quarter, 1), pltpu.roll(t, quarter, 1))
    return t * cos + swapped * sin_signed


def _inproj_kernel(x_ref, sh_ref, sc_ref, nw_ref, wqk_ref, wvgz_ref, wxbc_ref, wdt_ref, cos_ref, sin_ref,
                   q_ref, k_ref, v_ref, g_ref, z_ref, xbc_ref, dt_ref):
    hb = _norm_mod(x_ref[0], nw_ref[...], sc_ref[0], sh_ref[0]).astype(BF16)
    rw = q_ref.shape[-1]
    qk = _dot(hb, wqk_ref[...])
    cos = cos_ref[...]
    sin = sin_ref[...]
    lane = lax.broadcasted_iota(jnp.int32, cos.shape, 1)
    first_half = (lane % (RET_DK // 2)) < (RET_DK // 4)
    q_ref[0] = _rope(qk[:, :rw], cos, sin, first_half).astype(BF16)
    k_ref[0] = (_rope(qk[:, rw:], cos, sin, first_half) * (RET_DK ** -0.5)).astype(BF16)
    vgz = _dot(hb, wvgz_ref[...])
    v_ref[0] = vgz[:, :rw].astype(BF16)
    g_ref[0] = vgz[:, rw:2 * rw].astype(BF16)
    z_ref[0] = vgz[:, 2 * rw:].astype(BF16)
    xbc_ref[0] = _dot(hb, wxbc_ref[...]).astype(BF16)
    dt_ref[0] = _dot(hb, wdt_ref[...])


def _inproj_ctx_kernel(x_ref, sh_ref, sc_ref, nw_ref, wk_ref, wv_ref, wxbc_ref, wdt_ref,
                       k_ref, v_ref, xbc_ref, dt_ref):
    hb = _norm_mod(x_ref[0], nw_ref[...], sc_ref[0], sh_ref[0]).astype(BF16)
    k_ref[0] = (_dot(hb, wk_ref[...]) * (RET_DK ** -0.5)).astype(BF16)
    v_ref[0] = _dot(hb, wv_ref[...]).astype(BF16)
    xbc_ref[0] = _dot(hb, wxbc_ref[...]).astype(BF16)
    dt_ref[0] = _dot(hb, wdt_ref[...])


def _const_spec(shape):
    nd = len(shape)
    return pl.BlockSpec(shape, lambda *_: (0,) * nd)


def _inproj(x, mod3, nw, wqk, wvgz, wxbc, wdt, cos_t, sin_t):
    b, L, d = x.shape
    tm = min(TM_PROJ, L)
    rw = wqk.shape[1] // 2
    tok = lambda w: pl.BlockSpec((1, tm, w), lambda i, j: (i, j, 0))
    out_bf = lambda w: jax.ShapeDtypeStruct((b, L, w), BF16)
    return pl.pallas_call(
        _inproj_kernel,
        grid=(b, L // tm),
        in_specs=[
            tok(d),
            pl.BlockSpec((1, 1, d), lambda i, j: (i, 0, 0)),
            pl.BlockSpec((1, 1, d), lambda i, j: (i, 0, 1)),
            _const_spec((1, d)),
            _const_spec(wqk.shape), _const_spec(wvgz.shape), _const_spec(wxbc.shape), _const_spec(wdt.shape),
            pl.BlockSpec((tm, rw), lambda i, j: (j, 0)),
            pl.BlockSpec((tm, rw), lambda i, j: (j, 0)),
        ],
        out_specs=[tok(rw), tok(rw), tok(rw), tok(rw), tok(rw), tok(wxbc.shape[1]), tok(LANES)],
        out_shape=[out_bf(rw), out_bf(rw), out_bf(rw), out_bf(rw), out_bf(rw), out_bf(wxbc.shape[1]),
                   jax.ShapeDtypeStruct((b, L, LANES), F32)],
        compiler_params=pltpu.CompilerParams(vmem_limit_bytes=VMEM_LIMIT),
        name="inproj",
    )(x, mod3, mod3, nw, wqk, wvgz, wxbc, wdt, cos_t, sin_t)


def _inproj_ctx(ctx, mod3, ctx_row, nw, wk, wv, wxbc, wdt):
    b, L, d = ctx.shape
    tm = min(TM_PROJ, L)
    rw = wk.shape[1]
    tok = lambda w: pl.BlockSpec((1, tm, w), lambda i, j: (i, j, 0))
    out_bf = lambda w: jax.ShapeDtypeStruct((b, L, w), BF16)
    return pl.pallas_call(
        _inproj_ctx_kernel,
        grid=(b, L // tm),
        in_specs=[
            tok(d),
            pl.BlockSpec((1, 1, d), lambda i, j: (ctx_row, 0, 0)),
            pl.BlockSpec((1, 1, d), lambda i, j: (ctx_row, 0, 1)),
            _const_spec((1, d)),
            _const_spec(wk.shape), _const_spec(wv.shape), _const_spec(wxbc.shape), _const_spec(wdt.shape),
        ],
        out_specs=[tok(rw), tok(rw), tok(wxbc.shape[1]), tok(LANES)],
        out_shape=[out_bf(rw), out_bf(rw), out_bf(wxbc.shape[1]), jax.ShapeDtypeStruct((b, L, LANES), F32)],
        compiler_params=pltpu.CompilerParams(vmem_limit_bytes=VMEM_LIMIT),
        name="inproj_ctx",
    )(ctx, mod3, mod3, nw, wk, wv, wxbc, wdt)


def _ssd_kernel(xbc_ref, z_ref, dt_ref, xbcc_ref, dtc_ref, cw_ref, cb_ref, dtb_ref, alog_ref, dsk_ref, nw_ref,
                y_ref,
                xpad, xpadc, u, uc, dtv, dav, dtcv, dacv, sf_scr, kb_scr, acum, ecum, dec_scr,
                arow_scr, erow_scr, dtrow_scr):
    L = xbc_ref.shape[1]
    Lc = xbcc_ref.shape[1]
    nch = L // CHUNK
    nchc = Lc // CHUNK
    win = CHUNK + 2 * CONV_HALO
    nconv = xbc_ref.shape[2]
    nh = SSD_HEADS

    def conv_pass(src_ref, pad_ref, dst_ref, n_chunks, length):
        zeros = jnp.zeros((CONV_HALO, nconv), F32)
        pad_ref[0:CONV_HALO, :] = zeros
        pad_ref[CONV_HALO + length:2 * CONV_HALO + length, :] = zeros
        pad_ref[CONV_HALO:CONV_HALO + length, :] = src_ref[0].astype(F32)

        def chunk(c, carry):
            base = pl.multiple_of(c * CHUNK, CHUNK)
            for cb_i in range(nconv // LANES):
                cols = slice(cb_i * LANES, (cb_i + 1) * LANES)
                w = pad_ref[pl.ds(base, win), cols]
                acc = cb_ref[:, cols] + w[CONV_HALO:CONV_HALO + CHUNK] * cw_ref[SSD_CONV // 2:SSD_CONV // 2 + 1, cols]
                for j in range(SSD_CONV):
                    if j == SSD_CONV // 2:
                        continue
                    shifted = pltpu.roll(w, (SSD_CONV // 2 - j) % win, 0)
                    acc = acc + shifted[CONV_HALO:CONV_HALO + CHUNK] * cw_ref[j:j + 1, cols]
                dst_ref[pl.ds(base, CHUNK), cols] = _silu(acc).astype(BF16)
            return carry

        lax.fori_loop(0, n_chunks, chunk, 0)

    conv_pass(xbcc_ref, xpadc, uc, nchc, Lc)
    conv_pass(xbc_ref, xpad, u, nch, L)

    a_neg = -jnp.exp(alog_ref[...])
    dtv[...] = jax.nn.softplus(dt_ref[0] + dtb_ref[...])
    dav[...] = dtv[...] * a_neg
    dtcv[...] = jax.nn.softplus(dtc_ref[0] + dtb_ref[...])
    dacv[...] = dtcv[...] * a_neg

    row_i = lax.broadcasted_iota(jnp.int32, (CHUNK, CHUNK), 0)
    col_i = lax.broadcasted_iota(jnp.int32, (CHUNK, CHUNK), 1)
    causal = col_i <= row_i
    lo_half = col_i < SSD_HEADDIM
    fwd_lane = col_i < nh
    head_of = lax.broadcasted_iota(jnp.int32, (CHUNK, SSD_WIDTH), 1) // SSD_HEADDIM
    src_col = lax.broadcasted_iota(jnp.int32, (CHUNK, SSD_WIDTH), 0)
    exp_f = (head_of == src_col).astype(BF16)
    exp_b = (head_of == src_col - nh).astype(BF16)
    exp_fb = jnp.concatenate([exp_f, exp_b], axis=1)

    def split3(v):
        hi = v.astype(BF16)
        r1 = v - hi.astype(F32)
        mid = r1.astype(BF16)
        return hi, mid, (r1 - mid.astype(F32)).astype(BF16)

    def times_onehot(v, m, passes=3):
        parts = split3(v)[:passes]
        acc = _dot(parts[0], m)
        for part in parts[1:]:
            acc = acc + _dot(part, m)
        return acc

    def colb(mat, r):
        return jnp.broadcast_to(mat[:, r:r + 1], (CHUNK, CHUNK))

    def pair_sel(a, b_):
        return jnp.where(lo_half, a, b_)

    gw = 2 * LANES

    def chunk_terms(u_ref, dt_s, da_s, base):
        dt = dt_s[pl.ds(base, CHUNK), :]
        da = da_s[pl.ds(base, CHUNK), :]
        acol = da
        for step in (1, 2, 4, 8, 16, 32, 64):
            acol = acol + jnp.where(row_i >= step, pltpu.roll(acol, step, 0), 0.0)
        ecol = acol - da
        last = acol[CHUNK - 1:CHUNK, :]
        wgt = jnp.where(fwd_lane, jnp.exp(last - acol), jnp.exp(ecol)) * dt
        scale = jnp.where(fwd_lane, jnp.exp(acol), jnp.exp(last - ecol))
        wide = times_onehot(jnp.concatenate([wgt, scale], axis=0), exp_fb, passes=1)
        dec = times_onehot(jnp.broadcast_to(jnp.exp(last), (SUBLANES, LANES)), exp_fb)[0:1]
        xs = u_ref[pl.ds(base, CHUNK), 0:SSD_WIDTH].astype(F32)
        kmats = []
        for g in range(SSD_GROUPS):
            xw = jnp.concatenate([xs[:, g * gw:(g + 1) * gw] * wide[:CHUNK, g * gw:(g + 1) * gw],
                                  xs[:, g * gw:(g + 1) * gw] * wide[:CHUNK, SSD_WIDTH + g * gw:SSD_WIDTH + (g + 1) * gw]],
                                 axis=1).astype(BF16)
            bm = u_ref[pl.ds(base, CHUNK), SSD_WIDTH + g * SSD_STATE:SSD_WIDTH + (g + 1) * SSD_STATE]
            kmats.append(_dot_tn(bm, xw))
        return dt, acol, ecol, wide[CHUNK:], dec, kmats

    def advance(s, dec, kmats, backward):
        off = SSD_WIDTH if backward else 0
        koff = gw if backward else 0
        return [dec[:, off + g * gw:off + (g + 1) * gw] * s[g] + kmats[g][:, koff:koff + gw]
                for g in range(SSD_GROUPS)]

    ctx_terms = [chunk_terms(uc, dtcv, dacv, c * CHUNK) for c in range(nchc)]
    s_f0 = [jnp.zeros((SSD_STATE, gw), F32) for _ in range(SSD_GROUPS)]
    for c in range(nchc):
        s_f0 = advance(s_f0, ctx_terms[c][4], ctx_terms[c][5], False)
    s_b0 = [jnp.zeros((SSD_STATE, gw), F32) for _ in range(SSD_GROUPS)]
    for c in reversed(range(nchc)):
        s_b0 = advance(s_b0, ctx_terms[c][4], ctx_terms[c][5], True)

    def prep(c, carry):
        base = pl.multiple_of(c * CHUNK, CHUNK)
        dt, acol, ecol, scale, dec, kmats = chunk_terms(u, dtv, dav, base)
        acum[pl.ds(base, CHUNK), :] = acol
        ecum[pl.ds(base, CHUNK), :] = ecol
        hrow = pl.ds(pl.multiple_of(c * 2 * nh, 2 * nh), 2 * nh)
        arow_scr[hrow, :] = acol.T[:2 * nh]
        erow_scr[hrow, :] = ecol.T[:2 * nh]
        dtrow_scr[hrow, :] = dt.T[:2 * nh]
        xpad[pl.ds(base, CHUNK), :] = scale
        dec_scr[pl.ds(pl.multiple_of(c * SUBLANES, SUBLANES), SUBLANES), :] = jnp.broadcast_to(dec, (SUBLANES, 2 * SSD_WIDTH))
        for g in range(SSD_GROUPS):
            sf_scr[c, g] = kmats[g][:, :gw]
            kb_scr[c, g] = kmats[g][:, gw:]
        return carry

    lax.fori_loop(0, nch, prep, 0)

    def chunk_dec(c):
        return dec_scr[pl.ds(pl.multiple_of(c * SUBLANES, SUBLANES), 1), :]

    def fwd(c, s_old):
        dec = chunk_dec(c)
        new = []
        for g in range(SSD_GROUPS):
            new.append(dec[:, g * gw:(g + 1) * gw] * s_old[g] + sf_scr[c, g])
            sf_scr[c, g] = s_old[g]
        return tuple(new)

    lax.fori_loop(0, nch, fwd, tuple(s_f0))

    def bwd(i, s_b):
        c = nch - 1 - i
        base = pl.multiple_of(c * CHUNK, CHUNK)
        acol = acum[pl.ds(base, CHUNK), :]
        ecol = ecum[pl.ds(base, CHUNK), :]
        hrow = pl.ds(pl.multiple_of(c * 2 * nh, 2 * nh), 2 * nh)
        arow = arow_scr[hrow, :]
        erow = erow_scr[hrow, :]
        dt_t = dtrow_scr[hrow, :]
        scale = xpad[pl.ds(base, CHUNK), :]
        ys = []
        for g in range(SSD_GROUPS):
            bm = u[pl.ds(base, CHUNK), SSD_WIDTH + g * SSD_STATE:SSD_WIDTH + (g + 1) * SSD_STATE]
            cm = u[pl.ds(base, CHUNK), SSD_WIDTH + (SSD_GROUPS + g) * SSD_STATE:SSD_WIDTH + (SSD_GROUPS + g + 1) * SSD_STATE]
            cbm = _dot_nt(cm, bm)
            cs_f = _dot(cm, sf_scr[c, g].astype(BF16))
            cs_b = _dot(cm, s_b[g].astype(BF16))
            for pp in range(SSD_PAIRS // SSD_GROUPS):
                p = g * (SSD_PAIRS // SSD_GROUPS) + pp
                xs_b = u[pl.ds(base, CHUNK), p * LANES:(p + 1) * LANES]
                y_h = []
                for hh in range(2):
                    r = 2 * p + hh
                    arg = jnp.where(causal, colb(acol, r) - arow[r:r + 1, :],
                                    erow[nh + r:nh + r + 1, :] - colb(ecol, nh + r))
                    coef = jnp.where(causal, dt_t[r:r + 1, :], dt_t[nh + r:nh + r + 1, :])
                    gm = (cbm * (jnp.exp(arg) * coef)).astype(BF16)
                    y_h.append(_dot(gm, xs_b))
                sl = slice(pp * LANES, (pp + 1) * LANES)
                wl = slice(p * LANES, (p + 1) * LANES)
                wlb = slice(SSD_WIDTH + p * LANES, SSD_WIDTH + (p + 1) * LANES)
                ys.append(pair_sel(y_h[0], y_h[1]) + cs_f[:, sl] * scale[:, wl] + cs_b[:, sl] * scale[:, wlb]
                          + dsk_ref[:, wl] * xs_b.astype(F32))
        y = jnp.concatenate(ys, axis=1)
        y = y * _silu(z_ref[0, pl.ds(base, CHUNK), :].astype(F32))
        ms = jnp.mean(y * y, axis=-1, keepdims=True)
        y_ref[0, pl.ds(base, CHUNK), :] = ((y * lax.rsqrt(ms + EPS)) * nw_ref[...]).astype(BF16)
        dec = chunk_dec(c)
        return tuple(dec[:, SSD_WIDTH + g * gw:SSD_WIDTH + (g + 1) * gw] * s_b[g] + kb_scr[c, g]
                     for g in range(SSD_GROUPS))

    lax.fori_loop(0, nch, bwd, tuple(s_b0))


def _ssd(xbc, z, dt, xbcc, dtc, conv_w8, conv_b, dt_bias, a_log, d_skip, norm_w):
    b, L, nconv = xbc.shape
    Lc = xbcc.shape[1]
    nch = L // CHUNK
    per_b = lambda n, w: pl.BlockSpec((1, n, w), lambda i: (i, 0, 0))
    return pl.pallas_call(
        _ssd_kernel,
        grid=(b,),
        in_specs=[
            per_b(L, nconv), per_b(L, SSD_WIDTH), per_b(L, LANES), per_b(Lc, nconv), per_b(Lc, LANES),
            _const_spec(conv_w8.shape), _const_spec(conv_b.shape), _const_spec(dt_bias.shape),
            _const_spec(a_log.shape), _const_spec(d_skip.shape), _const_spec(norm_w.shape),
        ],
        out_specs=per_b(L, SSD_WIDTH),
        out_shape=jax.ShapeDtypeStruct((b, L, SSD_WIDTH), BF16),
        scratch_shapes=[
            pltpu.VMEM((L + 2 * CONV_HALO, nconv), F32),
            pltpu.VMEM((Lc + 2 * CONV_HALO, nconv), F32),
            pltpu.VMEM((L, nconv), BF16),
            pltpu.VMEM((Lc, nconv), BF16),
            pltpu.VMEM((L, LANES), F32), pltpu.VMEM((L, LANES), F32),
            pltpu.VMEM((Lc, LANES), F32), pltpu.VMEM((Lc, LANES), F32),
            pltpu.VMEM((nch, SSD_GROUPS, SSD_STATE, 2 * LANES), F32),
            pltpu.VMEM((nch, SSD_GROUPS, SSD_STATE, 2 * LANES), F32),
            pltpu.VMEM((L, LANES), F32), pltpu.VMEM((L, LANES), F32),
            pltpu.VMEM((nch * SUBLANES, 2 * SSD_WIDTH), F32),
            pltpu.VMEM((nch * 2 * SSD_HEADS, CHUNK), F32), pltpu.VMEM((nch * 2 * SSD_HEADS, CHUNK), F32),
            pltpu.VMEM((nch * 2 * SSD_HEADS, CHUNK), F32),
        ],
        compiler_params=pltpu.CompilerParams(vmem_limit_bytes=VMEM_LIMIT_SSD),
        name="ssd",
    )(xbc, z, dt, xbcc, dtc, conv_w8, conv_b, dt_bias, a_log, d_skip, norm_w)


def _ret_kernel(q_ref, k_ref, v_ref, g_ref, kc_ref, vc_ref, df_ref, db_ref, gn_ref, y_ref, sf_scr):
    L = q_ref.shape[1]
    Lc = kc_ref.shape[1]
    nch = L // CHUNK
    dk = RET_DK
    row_i = lax.broadcasted_iota(jnp.int32, (CHUNK, dk), 0).astype(F32)
    col_i = lax.broadcasted_iota(jnp.int32, (CHUNK, dk), 1).astype(F32)
    rel = row_i - col_i
    crow = lax.broadcasted_iota(jnp.int32, (Lc, dk), 0).astype(F32)

    heads = []
    s_f0 = []
    s_b0 = []
    for h in range(RET_HEADS):
        cols = slice(h * dk, (h + 1) * dk)
        lg_f = -jnp.exp(df_ref[:, cols])
        lg_b = -jnp.exp(db_ref[:, cols])
        heads.append(dict(
            cols=cols,
            dmat=jnp.where(rel >= 0, jnp.exp(jnp.maximum(rel, 0.0) * lg_f), jnp.exp(jnp.maximum(-rel, 0.0) * lg_b)),
            dq_f=jnp.exp((row_i + 1.0) * lg_f),
            dq_b=jnp.exp((CHUNK - row_i) * lg_b),
            dk_f=jnp.exp((CHUNK - 1.0 - row_i) * lg_f),
            dk_b=jnp.exp(row_i * lg_b),
            dc_f=jnp.exp(CHUNK * lg_f),
            dc_b=jnp.exp(CHUNK * lg_b),
        ))
        kc = kc_ref[0, :, cols].astype(F32)
        vc = vc_ref[0, :, cols]
        s_f0.append(_dot_tn((kc * jnp.exp((Lc - 1.0 - crow) * lg_f)).astype(BF16), vc))
        s_b0.append(_dot_tn((kc * jnp.exp(crow * lg_b)).astype(BF16), vc))

    def fwd(c, s_f):
        base = pl.multiple_of(c * CHUNK, CHUNK)
        new = []
        for h, hd in enumerate(heads):
            sf_scr[c, h] = s_f[h]
            kk = k_ref[0, pl.ds(base, CHUNK), hd["cols"]].astype(F32)
            vv = v_ref[0, pl.ds(base, CHUNK), hd["cols"]]
            new.append(hd["dc_f"] * s_f[h] + _dot_tn((kk * hd["dk_f"]).astype(BF16), vv))
        return tuple(new)

    lax.fori_loop(0, nch, fwd, tuple(s_f0), unroll=RET_UNROLL)

    def bwd(i, s_bs):
        c = nch - 1 - i
        base = pl.multiple_of(c * CHUNK, CHUNK)
        new = []
        for h, hd in enumerate(heads):
            qq = q_ref[0, pl.ds(base, CHUNK), hd["cols"]]
            kk = k_ref[0, pl.ds(base, CHUNK), hd["cols"]]
            vv = v_ref[0, pl.ds(base, CHUNK), hd["cols"]]
            s_b = s_bs[h]
            scores = (_dot_nt(qq, kk) * hd["dmat"]).astype(BF16)
            y = (_dot(scores, vv)
                 + _dot(qq, sf_scr[c, h].astype(BF16)) * hd["dq_f"]
                 + _dot(qq, s_b.astype(BF16)) * hd["dq_b"])
            mu = jnp.mean(y, axis=-1, keepdims=True)
            yc = y - mu
            var = jnp.mean(yc * yc, axis=-1, keepdims=True)
            yn = (yc * lax.rsqrt(var + EPS)) * gn_ref[:, hd["cols"]]
            gate = _silu(g_ref[0, pl.ds(base, CHUNK), hd["cols"]].astype(F32))
            y_ref[0, pl.ds(base, CHUNK), hd["cols"]] = (yn * gate).astype(BF16)
            new.append(hd["dc_b"] * s_b + _dot_tn((kk.astype(F32) * hd["dk_b"]).astype(BF16), vv))
        return tuple(new)

    lax.fori_loop(0, nch, bwd, tuple(s_b0), unroll=RET_UNROLL)


def _retention(q, k, v, g, kc, vc, decay_f, decay_b, gn_w):
    b, L, w = q.shape
    Lc = kc.shape[1]
    nch = L // CHUNK
    per_b = lambda n: pl.BlockSpec((1, n, w), lambda i: (i, 0, 0))
    return pl.pallas_call(
        _ret_kernel,
        grid=(b,),
        in_specs=[per_b(L), per_b(L), per_b(L), per_b(L), per_b(Lc), per_b(Lc),
                  _const_spec((1, w)), _const_spec((1, w)), _const_spec((1, w))],
        out_specs=per_b(L),
        out_shape=jax.ShapeDtypeStruct((b, L, w), BF16),
        scratch_shapes=[
            pltpu.VMEM((nch, RET_HEADS, RET_DK, RET_DK), F32),
        ],
        compiler_params=pltpu.CompilerParams(vmem_limit_bytes=VMEM_LIMIT),
        name="retention",
    )(q, k, v, g, kc, vc, decay_f, decay_b, gn_w)


def _outproj_router_kernel(yr_ref, ys_ref, x_ref, g1_ref, sh2_ref, sc2_ref, npost_ref, npre_ref,
                           wor_ref, wos_ref, wr_ref, br_ref, tri_ref,
                           x1_ref, h2_ref, route_ref, slots_ref, seg_ref,
                           wcat):
    i = pl.program_id(0)

    @pl.when(i == 0)
    def _():
        wr = wr_ref[...]
        hi = wr.astype(BF16)
        wcat[:, :LANES] = hi
        wcat[:, LANES:] = (wr - hi.astype(F32)).astype(BF16)

    _route_tile(yr_ref, ys_ref, x_ref, g1_ref, sh2_ref, sc2_ref, npost_ref, npre_ref, wor_ref, wos_ref, br_ref,
                tri_ref, x1_ref, h2_ref, route_ref, slots_ref, seg_ref, wcat)


def _route_tile(yr_ref, ys_ref, x_ref, g1_ref, sh2_ref, sc2_ref, npost_ref, npre_ref, wor_ref, wos_ref, br_ref,
                tri_ref, x1_ref, h2_ref, route_ref, slots_ref, seg_ref, wcat):
    tm = x_ref.shape[0]
    rows = slice(0, tm)
    y = _dot(yr_ref[rows, :], wor_ref[...]) + _dot(ys_ref[rows, :], wos_ref[...])
    ms = jnp.mean(y * y, axis=-1, keepdims=True)
    x1 = x_ref[rows, :] + (y * lax.rsqrt(ms + EPS)) * (g1_ref[0] * npost_ref[...])
    x1_ref[rows, :] = x1
    h2 = _norm_mod(x1, npre_ref[...], sc2_ref[0], sh2_ref[0])
    h2_ref[rows, :] = h2

    h_hi = h2.astype(BF16)
    h_lo = (h2 - h_hi.astype(F32)).astype(BF16)
    both = _dot(h_hi, wcat[...])
    lg = both[:, :LANES] + both[:, LANES:] + _dot(h_lo, wcat[:, :LANES]) + br_ref[...]

    lane = lax.broadcasted_iota(jnp.int32, (tm, LANES), 1)
    lane_f = lane.astype(F32)
    is_grp = (lane >= N_EXPERTS) & (lane < N_EXPERTS + MOE_GROUPS)
    gl = jnp.where(is_grp, lg, NEG_BIG)
    mg = jnp.max(gl, axis=-1, keepdims=True)
    grp_lane = jnp.min(jnp.where(gl == mg, lane_f, 1e9), axis=-1, keepdims=True)
    p_g = 1.0 / jnp.sum(jnp.where(is_grp, jnp.exp(gl - mg), 0.0), axis=-1, keepdims=True)
    first = (grp_lane - N_EXPERTS) * EXPERTS_PER_GROUP
    in_grp = (lane_f >= first) & (lane_f < first + EXPERTS_PER_GROUP)
    el = jnp.where(in_grp, lg, NEG_BIG)
    t1 = jnp.max(el, axis=-1, keepdims=True)
    i1 = jnp.min(jnp.where(el == t1, lane_f, 1e9), axis=-1, keepdims=True)
    el2 = jnp.where(lane_f == i1, NEG_BIG, el)
    t2 = jnp.max(el2, axis=-1, keepdims=True)
    i2 = jnp.min(jnp.where(el2 == t2, lane_f, 1e9), axis=-1, keepdims=True)
    s = jnp.exp(t2 - t1)
    w1 = p_g / (1.0 + s)
    w2 = p_g * s / (1.0 + s)

    oh1 = (lane_f == i1)
    oh2 = (lane_f == i2)
    ohf = jnp.where(oh1 | oh2, 1.0, 0.0)
    before = _dot(tri_ref[...], ohf.astype(BF16))
    cnt = jnp.sum(ohf, axis=0, keepdims=True)
    seg = jnp.floor((cnt + (SUBLANES - 1.0)) * (1.0 / SUBLANES)) * SUBLANES
    e_row = lax.broadcasted_iota(jnp.int32, (LANES, LANES), 0)
    e_col = lax.broadcasted_iota(jnp.int32, (LANES, LANES), 1)
    earlier = (e_row < e_col).astype(BF16)
    seg_off = _dot(jnp.broadcast_to(seg, (SUBLANES, LANES)).astype(BF16), earlier)[0:1]
    where_to = before + seg_off
    lpos1 = jnp.sum(jnp.where(oh1, where_to, 0.0), axis=-1, keepdims=True)
    lpos2 = jnp.sum(jnp.where(oh2, where_to, 0.0), axis=-1, keepdims=True)

    cols = [w1, w2, lpos1, lpos2]
    for wk in (w1, w2):
        hi = wk.astype(BF16).astype(F32)
        cols += [hi, wk - hi]
    cols.append(jnp.ones_like(w1))
    packed = jnp.zeros((tm, LANES), F32)
    for k, col in enumerate(cols):
        packed = jnp.where(lane == k, col, packed)
    route_ref[rows, :] = packed

    row = lax.broadcasted_iota(jnp.int32, (tm, LANES), 0)
    on_diag = (row % LANES) == lane
    per = tm // LANES
    for qi, col in enumerate((lpos1, lpos2)):
        picked = jnp.where(on_diag, col, 0.0)
        dense = jnp.sum(picked.reshape(per, LANES, LANES), axis=1).astype(jnp.int32)
        slots_ref[0, qi * per:(qi + 1) * per, :] = dense
    tbl_row = lax.broadcasted_iota(jnp.int32, (SUBLANES, LANES), 0)
    seg_ref[0] = jnp.where(tbl_row == 0, seg, jnp.where(tbl_row == 1, seg_off, 0.0)).astype(jnp.int32)


def _outproj_router(yr, ys, x2, mod3, npost, npre, wo_r, wo_s, w_router, b_router, seq_len):
    T, d = x2.shape
    tm = TM_OUT
    per_seq = seq_len // tm
    rw = yr.shape[1]
    tri = (jnp.arange(tm)[:, None] > jnp.arange(tm)[None, :]).astype(BF16)
    tok = lambda w: pl.BlockSpec((tm, w), lambda i: (i, 0))
    modv = lambda k: pl.BlockSpec((1, 1, d), lambda i: (i // per_seq, 0, k))
    tile3 = lambda r: pl.BlockSpec((1, r, LANES), lambda i: (i, 0, 0))
    slot_rows = TOP_K * (tm // LANES)
    return pl.pallas_call(
        _outproj_router_kernel,
        grid=(T // tm,),
        in_specs=[
            tok(rw), tok(rw), tok(d), modv(2), modv(3), modv(4),
            _const_spec((1, d)), _const_spec((1, d)),
            _const_spec(wo_r.shape), _const_spec(wo_s.shape), _const_spec(w_router.shape), _const_spec((1, LANES)),
            _const_spec((tm, tm)),
        ],
        out_specs=[tok(d), tok(d), tok(LANES), tile3(slot_rows), tile3(SUBLANES)],
        out_shape=[jax.ShapeDtypeStruct((T, d), F32), jax.ShapeDtypeStruct((T, d), F32),
                   jax.ShapeDtypeStruct((T, LANES), F32),
                   jax.ShapeDtypeStruct((T // tm, slot_rows, LANES), jnp.int32),
                   jax.ShapeDtypeStruct((T // tm, SUBLANES, LANES), jnp.int32)],
        scratch_shapes=[pltpu.VMEM((d, 2 * LANES), BF16)],
        compiler_params=pltpu.CompilerParams(dimension_semantics=("arbitrary",),
                                             vmem_limit_bytes=VMEM_LIMIT),
        name="outproj_router",
    )(yr, ys, x2, mod3, mod3, mod3, npost, npre, wo_r, wo_s, w_router, b_router, tri)


def _expert_kernel(be_ref, first_ref, slot_ref, next_ref, nused_ref, xs_ref, wg_hbm, wu_hbm, wd_hbm, y_ref,
                   wg_f, wu_f, wd_f, wg_b, wu_b, wd_b, sem):
    i = pl.program_id(0)

    def fetch(e, s):
        return [pltpu.make_async_copy(src.at[e], dst.at[s], sem.at[s, k])
                for k, (src, dst) in enumerate(((wg_hbm, wg_f), (wu_hbm, wu_f), (wd_hbm, wd_f)))]

    @pl.when(i == 0)
    def _():
        for c in fetch(be_ref[0], slot_ref[0]):
            c.start()

    @pl.when(first_ref[i] == 1)
    def _():
        s = slot_ref[i]

        @pl.when(next_ref[i] >= 0)
        def _():
            for c in fetch(next_ref[i], 1 - s):
                c.start()

        for c in fetch(be_ref[i], s):
            c.wait()
        wg_b[...] = wg_f[s].astype(BF16)
        wu_b[...] = wu_f[s].astype(BF16)
        wd_b[...] = wd_f[s].astype(BF16)

    @pl.when(i < nused_ref[0])
    def _():
        d = y_ref.shape[1]
        side = xs_ref[:, d:d + LANES]
        second = side[:, 8:9] == 2.0
        unscale = jnp.where(second, 0.5, 1.0)
        weight = jnp.where(second, side[:, 6:7] + side[:, 7:8], side[:, 4:5] + side[:, 5:6]) * unscale
        xb = (xs_ref[:, 0:d] * unscale).astype(BF16)
        hid = (_silu(_dot(xb, wg_b[...])) * _dot(xb, wu_b[...])).astype(BF16)
        y_ref[...] = (_dot(hid, wd_b[...]) * weight).astype(BF16).astype(F32)

    @pl.when(i >= nused_ref[0])
    def _():
        y_ref[...] = jnp.zeros_like(y_ref)


def _expert_plan(padded, pad_end, n_blocks, mb):
    n_used = (pad_end[-1:] // mb).astype(jnp.int32)
    blk_start = jnp.arange(n_blocks, dtype=jnp.int32) * mb
    experts = jnp.arange(N_EXPERTS, dtype=jnp.int32)
    blk_expert = jnp.minimum(jnp.sum((pad_end[None, :] <= blk_start[:, None]).astype(jnp.int32), axis=1),
                             N_EXPERTS - 1)
    prev = jnp.concatenate([jnp.full((1,), -1, jnp.int32), blk_expert[:-1]])
    first = ((blk_start < pad_end[-1]) & (blk_expert != prev)).astype(jnp.int32)
    slot = (jnp.cumsum(first) - 1) % 2
    later = jnp.where((padded > 0)[None, :] & (experts[None, :] > experts[:, None]), experts[None, :], N_EXPERTS)
    next_nonempty = jnp.min(later, axis=1)
    next_nonempty = jnp.where(next_nonempty == N_EXPERTS, -1, next_nonempty)
    nxt = jnp.sum(jnp.where(blk_expert[:, None] == experts[None, :], next_nonempty[None, :], 0), axis=1)
    return [a.astype(jnp.int32) for a in (blk_expert, first, slot, nxt, n_used)]


def _experts(plan, xs, w_gate, w_up, w_down):
    rows, dp = xs.shape
    n_exp, d, de = w_gate.shape
    assert dp == d + LANES
    mb = MB_EXPERT
    cap = rows // mb * mb
    grid_spec = pltpu.PrefetchScalarGridSpec(
        num_scalar_prefetch=len(plan),
        grid=(cap // mb,),
        in_specs=[
            pl.BlockSpec((mb, dp), lambda i, be, fi, sl, nx, nu: (jnp.minimum(i, nu[0] - 1), 0)),
            pl.BlockSpec(memory_space=pl.ANY), pl.BlockSpec(memory_space=pl.ANY), pl.BlockSpec(memory_space=pl.ANY),
        ],
        out_specs=pl.BlockSpec((mb, d), lambda i, be, fi, sl, nx, nu: (i, 0)),
        scratch_shapes=[pltpu.VMEM((2, d, de), F32), pltpu.VMEM((2, d, de), F32), pltpu.VMEM((2, de, d), F32),
                        pltpu.VMEM((d, de), BF16), pltpu.VMEM((d, de), BF16), pltpu.VMEM((de, d), BF16),
                        pltpu.SemaphoreType.DMA((2, 3))],
    )
    return pl.pallas_call(
        _expert_kernel,
        grid_spec=grid_spec,
        out_shape=jax.ShapeDtypeStruct((cap, d), F32),
        compiler_params=pltpu.CompilerParams(dimension_semantics=("arbitrary",),
                                             vmem_limit_bytes=VMEM_LIMIT),
        name="experts",
    )(*plan, xs, w_gate, w_up, w_down)


def _segment_pieces(tile, segrow_ref, seglen_ref, segoff_ref, act):
    def per_expert(e, carry):
        idx = tile * N_EXPERTS + e
        g0 = segrow_ref[idx]
        l0 = segoff_ref[idx]

        def piece(j, c2):
            act(pl.multiple_of(l0 + j * SUBLANES, SUBLANES), pl.multiple_of(g0 + j * SUBLANES, SUBLANES))
            return c2

        lax.fori_loop(0, seglen_ref[idx], piece, 0)
        return carry

    lax.fori_loop(0, N_EXPERTS, per_expert, 0)


def _wait_rows(total, row_copy):
    size = SUBLANES
    while size <= _stage_rows(max(TD_DISPATCH, TF_COMBINE)):
        @pl.when((total & size) != 0)
        def _(size=size):
            row_copy(size).wait()
        size *= 2


def _stage_rows(tile_tokens):
    return TOP_K * tile_tokens + N_EXPERTS * SUBLANES


def _dispatch_seg_kernel(pad_end_ref, zero_from_ref, segrow_ref, seglen_ref, segoff_ref, tilerows_ref,
                         lpos_ref, route_ref, h_ref, xs_hbm, zbuf, stage, sem, zsem):
    i = pl.program_id(0)
    nt = pl.num_programs(0)
    td, d = h_ref.shape
    sr = stage.shape[1]
    per = td // LANES

    @pl.when(i == 0)
    def _():
        zbuf[...] = jnp.zeros_like(zbuf)

        def pieces(e, act):
            for p in range(MB_EXPERT // ZERO_PIECE):
                row = pad_end_ref[e] - (p + 1) * ZERO_PIECE

                @pl.when(row + ZERO_PIECE > zero_from_ref[e])
                def _():
                    act(pltpu.make_async_copy(zbuf, xs_hbm.at[pl.ds(pl.multiple_of(row, ZERO_PIECE), ZERO_PIECE)], zsem))

        def start(e, carry):
            pieces(e, lambda c: c.start())
            return carry

        def wait(e, carry):
            pieces(e, lambda c: c.wait())
            return carry

        def tail(p, act):
            row = pad_end_ref[N_EXPERTS - 1] + p * ZERO_PIECE

            @pl.when(row + ZERO_PIECE <= xs_hbm.shape[0])
            def _():
                act(pltpu.make_async_copy(zbuf, xs_hbm.at[pl.ds(pl.multiple_of(row, SUBLANES), ZERO_PIECE)], zsem))

        n_tail = xs_hbm.shape[0] // ZERO_PIECE
        lax.fori_loop(0, N_EXPERTS, start, 0)
        lax.fori_loop(0, n_tail, lambda p, c: (tail(p, lambda cp: cp.start()), c)[1], 0)
        lax.fori_loop(0, N_EXPERTS, wait, 0)
        lax.fori_loop(0, n_tail, lambda p, c: (tail(p, lambda cp: cp.wait()), c)[1], 0)

    def shipped(tile):
        _wait_rows(tilerows_ref[tile], lambda n: pltpu.make_async_copy(
            stage.at[tile % 2, pl.ds(0, n)], xs_hbm.at[pl.ds(0, n)], sem.at[tile % 2]))

    @pl.when(i >= 2)
    def _():
        shipped(i - 2)

    srow = lax.broadcasted_iota(jnp.int32, (sr, LANES), 0)
    place = jnp.concatenate(
        [jnp.where(srow == lpos_ref[0, cb:cb + 1, :], 1.0,
                   jnp.where(srow == lpos_ref[0, per + cb:per + cb + 1, :], 2.0, 0.0)) for cb in range(per)],
        axis=1).astype(BF16)
    slot = i % 2
    stage[slot] = _dot(place, jnp.concatenate([h_ref[...].astype(BF16), route_ref[...].astype(BF16)], axis=1))
    _segment_pieces(i, segrow_ref, seglen_ref, segoff_ref, lambda lr, gr: pltpu.make_async_copy(
        stage.at[slot, pl.ds(lr, SUBLANES)], xs_hbm.at[pl.ds(gr, SUBLANES)], sem.at[slot]).start())

    @pl.when(i == nt - 1)
    def _():
        @pl.when(i >= 1)
        def _():
            shipped(i - 1)
        shipped(i)


def _dispatch_seg(plan, lpos, route, h2, cap):
    T, d = h2.shape
    td = TD_DISPATCH
    nt = T // td
    width = d + LANES
    sr = _stage_rows(td)
    n_pre = len(plan)
    grid_spec = pltpu.PrefetchScalarGridSpec(
        num_scalar_prefetch=n_pre,
        grid=(nt,),
        in_specs=[
            pl.BlockSpec((1, lpos.shape[1], LANES), lambda i, *_: (i, 0, 0)),
            pl.BlockSpec((td, LANES), lambda i, *_: (i, 0)),
            pl.BlockSpec((td, d), lambda i, *_: (i, 0)),
        ],
        out_specs=pl.BlockSpec(memory_space=pl.ANY),
        scratch_shapes=[pltpu.VMEM((ZERO_PIECE, width), F32), pltpu.VMEM((2, sr, width), F32),
                        pltpu.SemaphoreType.DMA((2,)), pltpu.SemaphoreType.DMA(())],
    )
    return pl.pallas_call(
        _dispatch_seg_kernel,
        grid_spec=grid_spec,
        out_shape=jax.ShapeDtypeStruct((cap, width), F32),
        compiler_params=pltpu.CompilerParams(dimension_semantics=("arbitrary",), vmem_limit_bytes=VMEM_LIMIT),
        name="dispatch",
    )(*plan, lpos, route, h2)


def _combine_seg_kernel(segrow_ref, seglen_ref, segoff_ref, tilerows_ref, route_ref, x1_ref, g2_ref, nw_ref,
                        yb_hbm, o_ref, stage, sem):
    i = pl.program_id(0)
    nt = pl.num_programs(0)
    tf = x1_ref.shape[0]
    sr = stage.shape[1]

    def fetch(tile, slot):
        _segment_pieces(tile, segrow_ref, seglen_ref, segoff_ref, lambda lr, gr: pltpu.make_async_copy(
            yb_hbm.at[pl.ds(gr, SUBLANES)], stage.at[slot, pl.ds(lr, SUBLANES)], sem.at[slot]).start())

    @pl.when(i == 0)
    def _():
        stage[...] = jnp.zeros_like(stage)
        fetch(i, 0)

    for slot in range(2):
        @pl.when((i + 1 < nt) & (i % 2 != slot))
        def _(slot=slot):
            fetch(i + 1, slot)

    _wait_rows(tilerows_ref[i], lambda n: pltpu.make_async_copy(
        yb_hbm.at[pl.ds(0, n)], stage.at[i % 2, pl.ds(0, n)], sem.at[i % 2]))

    route = route_ref[...]
    scol = lax.broadcasted_iota(jnp.int32, (tf, sr), 1).astype(F32)
    pick = jnp.where((scol == route[:, 2:3]) | (scol == route[:, 3:4]), 1.0, 0.0).astype(BF16)
    out = _dot(pick, stage[i % 2].astype(BF16))
    ms = jnp.mean(out * out, axis=-1, keepdims=True)
    o_ref[...] = x1_ref[...] + g2_ref[0] * ((out * lax.rsqrt(ms + EPS)) * nw_ref[...])


def _combine_seg(seg_plan, route, x1, mod3, nw, yb, seq_len):
    T, d = x1.shape
    tf = TF_COMBINE
    nt = T // tf
    per_seq = seq_len // tf
    grid_spec = pltpu.PrefetchScalarGridSpec(
        num_scalar_prefetch=len(seg_plan),
        grid=(nt,),
        in_specs=[
            pl.BlockSpec((tf, LANES), lambda i, *_: (i, 0)),
            pl.BlockSpec((tf, d), lambda i, *_: (i, 0)),
            pl.BlockSpec((1, 1, d), lambda i, *_: (i // per_seq, 0, 5)),
            pl.BlockSpec((1, d), lambda i, *_: (0, 0)),
            pl.BlockSpec(memory_space=pl.ANY),
        ],
        out_specs=pl.BlockSpec((tf, d), lambda i, *_: (i, 0)),
        scratch_shapes=[pltpu.VMEM((2, _stage_rows(tf), d), F32), pltpu.SemaphoreType.DMA((2,))],
    )
    return pl.pallas_call(
        _combine_seg_kernel,
        grid_spec=grid_spec,
        out_shape=jax.ShapeDtypeStruct((T, d), F32),
        compiler_params=pltpu.CompilerParams(dimension_semantics=("arbitrary",), vmem_limit_bytes=VMEM_LIMIT),
        name="combine",
    )(*seg_plan, route, x1, mod3, nw, yb)


def _rope_tables(L, n_heads):
    quarter = RET_DK // 4
    freqs = ROPE_BASE ** (-jnp.arange(quarter, dtype=F32) / quarter)
    t = jnp.arange(L)
    ang_r = (t // GRID_W).astype(F32)[:, None] * freqs
    ang_c = (t % GRID_W).astype(F32)[:, None] * freqs
    cos = jnp.concatenate([jnp.cos(ang_r)] * 2 + [jnp.cos(ang_c)] * 2, axis=-1)
    sin = jnp.concatenate([-jnp.sin(ang_r), jnp.sin(ang_r), -jnp.sin(ang_c), jnp.sin(ang_c)], axis=-1)
    return jnp.tile(cos, (1, n_heads)), jnp.tile(sin, (1, n_heads))


def _lane_pad(v, width=LANES):
    return jnp.pad(v, [(0, 0)] * (v.ndim - 1) + [(0, width - v.shape[-1])])


def kernel(x, c, ctx, c_ctx, w_mod, b_mod, norm_pre_mix, norm_post_mix, norm_pre_ffn, norm_post_ffn, w_in, w_out, ret_decay_f, ret_decay_b, ret_gn_w, ssd_conv_w, ssd_conv_b, ssd_dt_bias_f, ssd_dt_bias_b, ssd_a_log_f, ssd_a_log_b, ssd_d, ssd_norm_w, moe_w_rg, moe_b_rg, moe_w_re, moe_b_re, moe_w_gate, moe_w_up, moe_w_down):
    b, L, d = x.shape
    assert w_mod.shape[0] == 1, "single layer: context outputs are never needed"
    assert TM_OUT == TD_DISPATCH == TF_COMBINE, "router, dispatch and combine share one slot-row layout"
    rw = RET_HEADS * RET_DK
    nconv = SSD_WIDTH + 2 * SSD_GROUPS * SSD_STATE
    T = b * L

    mod_rows = -(-(b + 1) // SUBLANES) * SUBLANES
    c_all = jnp.zeros((mod_rows, d), F32).at[:b].set(c).at[b].set(c_ctx)
    mod3 = _modulation(c_all, w_mod[0], b_mod[0]).reshape(mod_rows, 1, 6 * d)

    wi = w_in[0]
    o = 0
    wq = wi[:, o:o + rw]; o += rw
    wk = wi[:, o:o + rw]; o += rw
    wv = wi[:, o:o + rw]; o += rw
    wg = wi[:, o:o + rw]; o += rw
    wz = wi[:, o:o + SSD_WIDTH]; o += SSD_WIDTH
    wxbc = wi[:, o:o + nconv].astype(BF16); o += nconv
    wdt = _lane_pad(wi[:, o:o + 2 * SSD_HEADS]).astype(BF16)
    wqk = jnp.concatenate([wq, wk], axis=1).astype(BF16)
    wvgz = jnp.concatenate([wv, wg, wz], axis=1).astype(BF16)
    cos_t, sin_t = _rope_tables(L, RET_HEADS)
    nw1 = norm_pre_mix[0].reshape(1, d)

    q, k, v, g, z, xbc, dt = _inproj(x, mod3, nw1, wqk, wvgz, wxbc, wdt, cos_t, sin_t)
    kc, vc, xbcc, dtc = _inproj_ctx(ctx, mod3, b, nw1, wk.astype(BF16), wv.astype(BF16), wxbc, wdt)

    conv_w8 = jnp.pad(ssd_conv_w[0], ((0, SUBLANES - SSD_CONV), (0, 0)))
    dt_bias = _lane_pad(jnp.concatenate([ssd_dt_bias_f[0], ssd_dt_bias_b[0]])[None, :])
    a_log = _lane_pad(jnp.concatenate([ssd_a_log_f[0], ssd_a_log_b[0]])[None, :])
    d_skip = jnp.repeat(ssd_d[0], SSD_HEADDIM)[None, :]
    ys = _ssd(xbc, z, dt, xbcc, dtc, conv_w8, ssd_conv_b[0][None, :], dt_bias, a_log, d_skip,
              ssd_norm_w[0][None, :])

    yr = _retention(q, k, v, g, kc, vc,
                    jnp.repeat(ret_decay_f[0], RET_DK)[None, :], jnp.repeat(ret_decay_b[0], RET_DK)[None, :],
                    ret_gn_w[0][None, :])

    wo = w_out[0].astype(BF16)
    w_router = _lane_pad(jnp.concatenate(
        [jnp.transpose(moe_w_re[0], (1, 0, 2)).reshape(d, N_EXPERTS), moe_w_rg[0]], axis=1))
    b_router = _lane_pad(jnp.concatenate([moe_b_re[0].reshape(-1), moe_b_rg[0]])[None, :])
    x1, h2, route, lpos, seg = _outproj_router(
        yr.reshape(T, rw), ys.reshape(T, SSD_WIDTH), x.reshape(T, d), mod3,
        norm_post_mix[0][None, :], norm_pre_ffn[0][None, :], wo[:rw], wo[rw:], w_router, b_router, L)

    mb = MB_EXPERT
    nt = T // TM_OUT
    n_blocks = -(-(T * TOP_K + nt * N_EXPERTS * (SUBLANES - 1) + N_EXPERTS * (mb - 1)) // mb)
    seg_len = seg[:, 0, :N_EXPERTS]
    seg_off = seg[:, 1, :N_EXPERTS]
    used = jnp.sum(seg_len, axis=0)
    padded = (used + mb - 1) // mb * mb
    pad_end = jnp.cumsum(padded)
    pad_start = pad_end - padded
    seg_row = pad_start[None, :] + jnp.cumsum(seg_len, axis=0) - seg_len
    seg_plan = [a.reshape(-1).astype(jnp.int32)
                for a in (seg_row, seg_len // SUBLANES, seg_off, jnp.sum(seg_len, axis=1))]
    zero_from = (pad_start + used).astype(jnp.int32)

    xs = _dispatch_seg([pad_end.astype(jnp.int32), zero_from] + seg_plan, lpos, route, h2, n_blocks * mb)
    yb = _experts(_expert_plan(padded, pad_end, n_blocks, mb), xs, moe_w_gate[0], moe_w_up[0], moe_w_down[0])
    out = _combine_seg(seg_plan, route, x1, mod3, norm_post_ffn[0][None, :], yb, L)
    return out.reshape(b, L, d)
```

```python
import jax
import jax.numpy as jnp
from jax import lax
from jax.experimental import pallas as pl
from jax.experimental.pallas import tpu as pltpu

F32 = jnp.float32
BF16 = jnp.bfloat16

LANES = 128
SUBLANES = 8
V7X_VMEM_BYTES = 64 * 1024 * 1024
VMEM_LIMIT = V7X_VMEM_BYTES * 3 // 4
VMEM_LIMIT_SSD = V7X_VMEM_BYTES * 7 // 8

EPS = 1e-6
CHUNK = 128
GRID_W = 64
RET_HEADS = 4
RET_DK = 128
ROPE_BASE = 10000.0
SSD_HEADS = 8
SSD_HEADDIM = 64
SSD_GROUPS = 2
SSD_STATE = 128
SSD_WIDTH = SSD_HEADS * SSD_HEADDIM
SSD_CONV = 5
SSD_PAIRS = SSD_WIDTH // LANES
MOE_GROUPS = 4
EXPERTS_PER_GROUP = 8
N_EXPERTS = MOE_GROUPS * EXPERTS_PER_GROUP
TOP_K = 2
CONV_HALO = SUBLANES

TM_PROJ = 512
TM_OUT = 512
TD_DISPATCH = TM_OUT
MB_EXPERT = 512
ZERO_PIECE = MB_EXPERT // 2
TF_COMBINE = TM_OUT
RET_UNROLL = 2
NEG_BIG = -1e30


def _silu(v):
    return v * jax.nn.sigmoid(v)


def _dot(a, b):
    return jnp.dot(a, b, preferred_element_type=F32)


def _dot_tn(a, b):
    return lax.dot_general(a, b, (((0,), (0,)), ((), ())), preferred_element_type=F32)


def _dot_nt(a, b):
    return lax.dot_general(a, b, (((1,), (1,)), ((), ())), preferred_element_type=F32)


def _mod_kernel(c_ref, w_ref, b_ref, o_ref):
    a = _silu(c_ref[...])
    w = w_ref[...]
    a_hi = a.astype(BF16)
    a_lo = (a - a_hi.astype(F32)).astype(BF16)
    w_hi = w.astype(BF16)
    w_lo = (w - w_hi.astype(F32)).astype(BF16)
    o_ref[...] = _dot(a_hi, w_hi) + _dot(a_lo, w_hi) + _dot(a_hi, w_lo) + b_ref[...]


def _modulation(c_all, w_mod, b_mod):
    rows, d = c_all.shape
    n = w_mod.shape[1]
    return pl.pallas_call(
        _mod_kernel,
        grid=(n // d,),
        in_specs=[
            pl.BlockSpec((rows, d), lambda j: (0, 0)),
            pl.BlockSpec((d, d), lambda j: (0, j)),
            pl.BlockSpec((1, d), lambda j: (0, j)),
        ],
        out_specs=pl.BlockSpec((rows, d), lambda j: (0, j)),
        out_shape=jax.ShapeDtypeStruct((rows, n), F32),
        name="modulation",
    )(c_all, w_mod, b_mod.reshape(1, n))


def _norm_mod(x, nw, sc, sh):
    ms = jnp.mean(x * x, axis=-1, keepdims=True)
    return (x * lax.rsqrt(ms + EPS)) * (nw * (1.0 + sc)) + sh


def _rope(t, cos, sin_signed, first_half):
    width = t.shape[-1]
    quarter = RET_DK // 4
    swapped = jnp.where(first_half, pltpu.roll(t, width - quarter, 1), pltpu.roll(t, quarter, 1))
    return t * cos + swapped * sin_signed


def _inproj_kernel(x_ref, sh_ref, sc_ref, nw_ref, wqk_ref, wvgz_ref, wxbc_ref, wdt_ref, cos_ref, sin_ref,
                   q_ref, k_ref, v_ref, g_ref, z_ref, xbc_ref, dt_ref):
    hb = _norm_mod(x_ref[0], nw_ref[...], sc_ref[0], sh_ref[0]).astype(BF16)
    rw = q_ref.shape[-1]
    qk = _dot(hb, wqk_ref[...])
    cos = cos_ref[...]
    sin = sin_ref[...]
    lane = lax.broadcasted_iota(jnp.int32, cos.shape, 1)
    first_half = (lane % (RET_DK // 2)) < (RET_DK // 4)
    q_ref[0] = _rope(qk[:, :rw], cos, sin, first_half).astype(BF16)
    k_ref[0] = (_rope(qk[:, rw:], cos, sin, first_half) * (RET_DK ** -0.5)).astype(BF16)
    vgz = _dot(hb, wvgz_ref[...])
    v_ref[0] = vgz[:, :rw].astype(BF16)
    g_ref[0] = vgz[:, rw:2 * rw].astype(BF16)
    z_ref[0] = vgz[:, 2 * rw:].astype(BF16)
    xbc_ref[0] = _dot(hb, wxbc_ref[...]).astype(BF16)
    dt_ref[0] = _dot(hb, wdt_ref[...])


def _inproj_ctx_kernel(x_ref, sh_ref, sc_ref, nw_ref, wk_ref, wv_ref, wxbc_ref, wdt_ref,
                       k_ref, v_ref, xbc_ref, dt_ref):
    hb = _norm_mod(x_ref[0], nw_ref[...], sc_ref[0], sh_ref[0]).astype(BF16)
    k_ref[0] = (_dot(hb, wk_ref[...]) * (RET_DK ** -0.5)).astype(BF16)
    v_ref[0] = _dot(hb, wv_ref[...]).astype(BF16)
    xbc_ref[0] = _dot(hb, wxbc_ref[...]).astype(BF16)
    dt_ref[0] = _dot(hb, wdt_ref[...])


def _const_spec(shape):
    nd = len(shape)
    return pl.BlockSpec(shape, lambda *_: (0,) * nd)


def _inproj(x, mod3, nw, wqk, wvgz, wxbc, wdt, cos_t, sin_t):
    b, L, d = x.shape
    tm = min(TM_PROJ, L)
    rw = wqk.shape[1] // 2
    tok = lambda w: pl.BlockSpec((1, tm, w), lambda i, j: (i, j, 0))
    out_bf = lambda w: jax.ShapeDtypeStruct((b, L, w), BF16)
    return pl.pallas_call(
        _inproj_kernel,
        grid=(b, L // tm),
        in_specs=[
            tok(d),
            pl.BlockSpec((1, 1, d), lambda i, j: (i, 0, 0)),
            pl.BlockSpec((1, 1, d), lambda i, j: (i, 0, 1)),
            _const_spec((1, d)),
            _const_spec(wqk.shape), _const_spec(wvgz.shape), _const_spec(wxbc.shape), _const_spec(wdt.shape),
            pl.BlockSpec((tm, rw), lambda i, j: (j, 0)),
            pl.BlockSpec((tm, rw), lambda i, j: (j, 0)),
        ],
        out_specs=[tok(rw), tok(rw), tok(rw), tok(rw), tok(rw), tok(wxbc.shape[1]), tok(LANES)],
        out_shape=[out_bf(rw), out_bf(rw), out_bf(rw), out_bf(rw), out_bf(rw), out_bf(wxbc.shape[1]),
                   jax.ShapeDtypeStruct((b, L, LANES), F32)],
        compiler_params=pltpu.CompilerParams(vmem_limit_bytes=VMEM_LIMIT),
        name="inproj",
    )(x, mod3, mod3, nw, wqk, wvgz, wxbc, wdt, cos_t, sin_t)


def _inproj_ctx(ctx, mod3, ctx_row, nw, wk, wv, wxbc, wdt):
    b, L, d = ctx.shape
    tm = min(TM_PROJ, L)
    rw = wk.shape[1]
    tok = lambda w: pl.BlockSpec((1, tm, w), lambda i, j: (i, j, 0))
    out_bf = lambda w: jax.ShapeDtypeStruct((b, L, w), BF16)
    return pl.pallas_call(
        _inproj_ctx_kernel,
        grid=(b, L // tm),
        in_specs=[
            tok(d),
            pl.BlockSpec((1, 1, d), lambda i, j: (ctx_row, 0, 0)),
            pl.BlockSpec((1, 1, d), lambda i, j: (ctx_row, 0, 1)),
            _const_spec((1, d)),
            _const_spec(wk.shape), _const_spec(wv.shape), _const_spec(wxbc.shape), _const_spec(wdt.shape),
        ],
        out_specs=[tok(rw), tok(rw), tok(wxbc.shape[1]), tok(LANES)],
        out_shape=[out_bf(rw), out_bf(rw), out_bf(wxbc.shape[1]), jax.ShapeDtypeStruct((b, L, LANES), F32)],
        compiler_params=pltpu.CompilerParams(vmem_limit_bytes=VMEM_LIMIT),
        name="inproj_ctx",
    )(ctx, mod3, mod3, nw, wk, wv, wxbc, wdt)


def _ssd_kernel(xbc_ref, z_ref, dt_ref, xbcc_ref, dtc_ref, cw_ref, cb_ref, dtb_ref, alog_ref, dsk_ref, nw_ref,
                y_ref,
                xpad, xpadc, u, uc, dtv, dav, dtcv, dacv, sf_scr, kb_scr, acum, ecum, dec_scr,
                arow_scr, erow_scr, dtrow_scr):
    L = xbc_ref.shape[1]
    Lc = xbcc_ref.shape[1]
    nch = L // CHUNK
    nchc = Lc // CHUNK
    win = CHUNK + 2 * CONV_HALO
    nconv = xbc_ref.shape[2]
    nh = SSD_HEADS

    def conv_pass(src_ref, pad_ref, dst_ref, n_chunks, length):
        zeros = jnp.zeros((CONV_HALO, nconv), F32)
        pad_ref[0:CONV_HALO, :] = zeros
        pad_ref[CONV_HALO + length:2 * CONV_HALO + length, :] = zeros
        pad_ref[CONV_HALO:CONV_HALO + length, :] = src_ref[0].astype(F32)

        def chunk(c, carry):
            base = pl.multiple_of(c * CHUNK, CHUNK)
            for cb_i in range(nconv // LANES):
                cols = slice(cb_i * LANES, (cb_i + 1) * LANES)
                w = pad_ref[pl.ds(base, win), cols]
                acc = cb_ref[:, cols] + w[CONV_HALO:CONV_HALO + CHUNK] * cw_ref[SSD_CONV // 2:SSD_CONV // 2 + 1, cols]
                for j in range(SSD_CONV):
                    if j == SSD_CONV // 2:
                        continue
                    shifted = pltpu.roll(w, (SSD_CONV // 2 - j) % win, 0)
                    acc = acc + shifted[CONV_HALO:CONV_HALO + CHUNK] * cw_ref[j:j + 1, cols]
                dst_ref[pl.ds(base, CHUNK), cols] = _silu(acc).astype(BF16)
            return carry

        lax.fori_loop(0, n_chunks, chunk, 0)

    conv_pass(xbcc_ref, xpadc, uc, nchc, Lc)
    conv_pass(xbc_ref, xpad, u, nch, L)

    a_neg = -jnp.exp(alog_ref[...])
    dtv[...] = jax.nn.softplus(dt_ref[0] + dtb_ref[...])
    dav[...] = dtv[...] * a_neg
    dtcv[...] = jax.nn.softplus(dtc_ref[0] + dtb_ref[...])
    dacv[...] = dtcv[...] * a_neg

    row_i = lax.broadcasted_iota(jnp.int32, (CHUNK, CHUNK), 0)
    col_i = lax.broadcasted_iota(jnp.int32, (CHUNK, CHUNK), 1)
    causal = col_i <= row_i
    lo_half = col_i < SSD_HEADDIM
    fwd_lane = col_i < nh
    head_of = lax.broadcasted_iota(jnp.int32, (CHUNK, SSD_WIDTH), 1) // SSD_HEADDIM
    src_col = lax.broadcasted_iota(jnp.int32, (CHUNK, SSD_WIDTH), 0)
    exp_f = (head_of == src_col).astype(BF16)
    exp_b = (head_of == src_col - nh).astype(BF16)
    exp_fb = jnp.concatenate([exp_f, exp_b], axis=1)

    def split3(v):
        hi = v.astype(BF16)
        r1 = v - hi.astype(F32)
        mid = r1.astype(BF16)
        return hi, mid, (r1 - mid.astype(F32)).astype(BF16)

    def times_onehot(v, m, passes=3):
        parts = split3(v)[:passes]
        acc = _dot(parts[0], m)
        for part in parts[1:]:
            acc = acc + _dot(part, m)
        return acc

    def colb(mat, r):
        return jnp.broadcast_to(mat[:, r:r + 1], (CHUNK, CHUNK))

    def pair_sel(a, b_):
        return jnp.where(lo_half, a, b_)

    gw = 2 * LANES

    def chunk_terms(u_ref, dt_s, da_s, base):
        dt = dt_s[pl.ds(base, CHUNK), :]
        da = da_s[pl.ds(base, CHUNK), :]
        acol = da
        for step in (1, 2, 4, 8, 16, 32, 64):
            acol = acol + jnp.where(row_i >= step, pltpu.roll(acol, step, 0), 0.0)
        ecol = acol - da
        last = acol[CHUNK - 1:CHUNK, :]
        wgt = jnp.where(fwd_lane, jnp.exp(last - acol), jnp.exp(ecol)) * dt
        scale = jnp.where(fwd_lane, jnp.exp(acol), jnp.exp(last - ecol))
        wide = times_onehot(jnp.concatenate([wgt, scale], axis=0), exp_fb, passes=1)
        dec = times_onehot(jnp.broadcast_to(jnp.exp(last), (SUBLANES, LANES)), exp_fb)[0:1]
        xs = u_ref[pl.ds(base, CHUNK), 0:SSD_WIDTH].astype(F32)
        kmats = []
        for g in range(SSD_GROUPS):
            xw = jnp.concatenate([xs[:, g * gw:(g + 1) * gw] * wide[:CHUNK, g * gw:(g + 1) * gw],
                                  xs[:, g * gw:(g + 1) * gw] * wide[:CHUNK, SSD_WIDTH + g * gw:SSD_WIDTH + (g + 1) * gw]],
                                 axis=1).astype(BF16)
            bm = u_ref[pl.ds(base, CHUNK), SSD_WIDTH + g * SSD_STATE:SSD_WIDTH + (g + 1) * SSD_STATE]
            kmats.append(_dot_tn(bm, xw))
        return dt, acol, ecol, wide[CHUNK:], dec, kmats

    def advance(s, dec, kmats, backward):
        off = SSD_WIDTH if backward else 0
        koff = gw if backward else 0
        return [dec[:, off + g * gw:off + (g + 1) * gw] * s[g] + kmats[g][:, koff:koff + gw]
                for g in range(SSD_GROUPS)]

    ctx_terms = [chunk_terms(uc, dtcv, dacv, c * CHUNK) for c in range(nchc)]
    s_f0 = [jnp.zeros((SSD_STATE, gw), F32) for _ in range(SSD_GROUPS)]
    for c in range(nchc):
        s_f0 = advance(s_f0, ctx_terms[c][4], ctx_terms[c][5], False)
    s_b0 = [jnp.zeros((SSD_STATE, gw), F32) for _ in range(SSD_GROUPS)]
    for c in reversed(range(nchc)):
        s_b0 = advance(s_b0, ctx_terms[c][4], ctx_terms[c][5], True)

    def prep(c, carry):
        base = pl.multiple_of(c * CHUNK, CHUNK)
        dt, acol, ecol, scale, dec, kmats = chunk_terms(u, dtv, dav, base)
        acum[pl.ds(base, CHUNK), :] = acol
        ecum[pl.ds(base, CHUNK), :] = ecol
        hrow = pl.ds(pl.multiple_of(c * 2 * nh, 2 * nh), 2 * nh)
        arow_scr[hrow, :] = acol.T[:2 * nh]
        erow_scr[hrow, :] = ecol.T[:2 * nh]
        dtrow_scr[hrow, :] = dt.T[:2 * nh]
        xpad[pl.ds(base, CHUNK), :] = scale
        dec_scr[pl.ds(pl.multiple_of(c * SUBLANES, SUBLANES), SUBLANES), :] = jnp.broadcast_to(dec, (SUBLANES, 2 * SSD_WIDTH))
        for g in range(SSD_GROUPS):
            sf_scr[c, g] = kmats[g][:, :gw]
            kb_scr[c, g] = kmats[g][:, gw:]
        return carry

    lax.fori_loop(0, nch, prep, 0)

    def chunk_dec(c):
        return dec_scr[pl.ds(pl.multiple_of(c * SUBLANES, SUBLANES), 1), :]

    def fwd(c, s_old):
        dec = chunk_dec(c)
        new = []
        for g in range(SSD_GROUPS):
            new.append(dec[:, g * gw:(g + 1) * gw] * s_old[g] + sf_scr[c, g])
            sf_scr[c, g] = s_old[g]
        return tuple(new)

    lax.fori_loop(0, nch, fwd, tuple(s_f0))

    def bwd(i, s_b):
        c = nch - 1 - i
        base = pl.multiple_of(c * CHUNK, CHUNK)
        acol = acum[pl.ds(base, CHUNK), :]
        ecol = ecum[pl.ds(base, CHUNK), :]
        hrow = pl.ds(pl.multiple_of(c * 2 * nh, 2 * nh), 2 * nh)
        arow = arow_scr[hrow, :]
        erow = erow_scr[hrow, :]
        dt_t = dtrow_scr[hrow, :]
        scale = xpad[pl.ds(base, CHUNK), :]
        ys = []
        for g in range(SSD_GROUPS):
            bm = u[pl.ds(base, CHUNK), SSD_WIDTH + g * SSD_STATE:SSD_WIDTH + (g + 1) * SSD_STATE]
            cm = u[pl.ds(base, CHUNK), SSD_WIDTH + (SSD_GROUPS + g) * SSD_STATE:SSD_WIDTH + (SSD_GROUPS + g + 1) * SSD_STATE]
            cbm = _dot_nt(cm, bm)
            cs_f = _dot(cm, sf_scr[c, g].astype(BF16))
            cs_b = _dot(cm, s_b[g].astype(BF16))
            for pp in range(SSD_PAIRS // SSD_GROUPS):
                p = g * (SSD_PAIRS // SSD_GROUPS) + pp
                xs_b = u[pl.ds(base, CHUNK), p * LANES:(p + 1) * LANES]
                y_h = []
                for hh in range(2):
                    r = 2 * p + hh
                    arg = jnp.where(causal, colb(acol, r) - arow[r:r + 1, :],
                                    erow[nh + r:nh + r + 1, :] - colb(ecol, nh + r))
                    coef = jnp.where(causal, dt_t[r:r + 1, :], dt_t[nh + r:nh + r + 1, :])
                    gm = (cbm * (jnp.exp(arg) * coef)).astype(BF16)
                    y_h.append(_dot(gm, xs_b))
                sl = slice(pp * LANES, (pp + 1) * LANES)
                wl = slice(p * LANES, (p + 1) * LANES)
                wlb = slice(SSD_WIDTH + p * LANES, SSD_WIDTH + (p + 1) * LANES)
                ys.append(pair_sel(y_h[0], y_h[1]) + cs_f[:, sl] * scale[:, wl] + cs_b[:, sl] * scale[:, wlb]
                          + dsk_ref[:, wl] * xs_b.astype(F32))
        y = jnp.concatenate(ys, axis=1)
        y = y * _silu(z_ref[0, pl.ds(base, CHUNK), :].astype(F32))
        ms = jnp.mean(y * y, axis=-1, keepdims=True)
        y_ref[0, pl.ds(base, CHUNK), :] = ((y * lax.rsqrt(ms + EPS)) * nw_ref[...]).astype(BF16)
        dec = chunk_dec(c)
        return tuple(dec[:, SSD_WIDTH + g * gw:SSD_WIDTH + (g + 1) * gw] * s_b[g] + kb_scr[c, g]
                     for g in range(SSD_GROUPS))

    lax.fori_loop(0, nch, bwd, tuple(s_b0))


def _ssd(xbc, z, dt, xbcc, dtc, conv_w8, conv_b, dt_bias, a_log, d_skip, norm_w):
    b, L, nconv = xbc.shape
    Lc = xbcc.shape[1]
    nch = L // CHUNK
    per_b = lambda n, w: pl.BlockSpec((1, n, w), lambda i: (i, 0, 0))
    return pl.pallas_call(
        _ssd_kernel,
        grid=(b,),
        in_specs=[
            per_b(L, nconv), per_b(L, SSD_WIDTH), per_b(L, LANES), per_b(Lc, nconv), per_b(Lc, LANES),
            _const_spec(conv_w8.shape), _const_spec(conv_b.shape), _const_spec(dt_bias.shape),
            _const_spec(a_log.shape), _const_spec(d_skip.shape), _const_spec(norm_w.shape),
        ],
        out_specs=per_b(L, SSD_WIDTH),
        out_shape=jax.ShapeDtypeStruct((b, L, SSD_WIDTH), BF16),
        scratch_shapes=[
            pltpu.VMEM((L + 2 * CONV_HALO, nconv), F32),
            pltpu.VMEM((Lc + 2 * CONV_HALO, nconv), F32),
            pltpu.VMEM((L, nconv), BF16),
            pltpu.VMEM((Lc, nconv), BF16),
            pltpu.VMEM((L, LANES), F32), pltpu.VMEM((L, LANES), F32),
            pltpu.VMEM((Lc, LANES), F32), pltpu.VMEM((Lc, LANES), F32),
            pltpu.VMEM((nch, SSD_GROUPS, SSD_STATE, 2 * LANES), F32),
            pltpu.VMEM((nch, SSD_GROUPS, SSD_STATE, 2 * LANES), F32),
            pltpu.VMEM((L, LANES), F32), pltpu.VMEM((L, LANES), F32),
            pltpu.VMEM((nch * SUBLANES, 2 * SSD_WIDTH), F32),
            pltpu.VMEM((nch * 2 * SSD_HEADS, CHUNK), F32), pltpu.VMEM((nch * 2 * SSD_HEADS, CHUNK), F32),
            pltpu.VMEM((nch * 2 * SSD_HEADS, CHUNK), F32),
        ],
        compiler_params=pltpu.CompilerParams(vmem_limit_bytes=VMEM_LIMIT_SSD),
        name="ssd",
    )(xbc, z, dt, xbcc, dtc, conv_w8, conv_b, dt_bias, a_log, d_skip, norm_w)


def _ret_kernel(q_ref, k_ref, v_ref, g_ref, kc_ref, vc_ref, df_ref, db_ref, gn_ref, y_ref, sf_scr):
    L = q_ref.shape[1]
    Lc = kc_ref.shape[1]
    nch = L // CHUNK
    dk = RET_DK
    row_i = lax.broadcasted_iota(jnp.int32, (CHUNK, dk), 0).astype(F32)
    col_i = lax.broadcasted_iota(jnp.int32, (CHUNK, dk), 1).astype(F32)
    rel = row_i - col_i
    crow = lax.broadcasted_iota(jnp.int32, (Lc, dk), 0).astype(F32)

    heads = []
    s_f0 = []
    s_b0 = []
    for h in range(RET_HEADS):
        cols = slice(h * dk, (h + 1) * dk)
        lg_f = -jnp.exp(df_ref[:, cols])
        lg_b = -jnp.exp(db_ref[:, cols])
        heads.append(dict(
            cols=cols,
            dmat=jnp.where(rel >= 0, jnp.exp(jnp.maximum(rel, 0.0) * lg_f), jnp.exp(jnp.maximum(-rel, 0.0) * lg_b)),
            dq_f=jnp.exp((row_i + 1.0) * lg_f),
            dq_b=jnp.exp((CHUNK - row_i) * lg_b),
            dk_f=jnp.exp((CHUNK - 1.0 - row_i) * lg_f),
            dk_b=jnp.exp(row_i * lg_b),
            dc_f=jnp.exp(CHUNK * lg_f),
            dc_b=jnp.exp(CHUNK * lg_b),
        ))
        kc = kc_ref[0, :, cols].astype(F32)
        vc = vc_ref[0, :, cols]
        s_f0.append(_dot_tn((kc * jnp.exp((Lc - 1.0 - crow) * lg_f)).astype(BF16), vc))
        s_b0.append(_dot_tn((kc * jnp.exp(crow * lg_b)).astype(BF16), vc))

    def fwd(c, s_f):
        base = pl.multiple_of(c * CHUNK, CHUNK)
        new = []
        for h, hd in enumerate(heads):
            sf_scr[c, h] = s_f[h]
            kk = k_ref[0, pl.ds(base, CHUNK), hd["cols"]].astype(F32)
            vv = v_ref[0, pl.ds(base, CHUNK), hd["cols"]]
            new.append(hd["dc_f"] * s_f[h] + _dot_tn((kk * hd["dk_f"]).astype(BF16), vv))
        return tuple(new)

    lax.fori_loop(0, nch, fwd, tuple(s_f0), unroll=RET_UNROLL)

    def bwd(i, s_bs):
        c = nch - 1 - i
        base = pl.multiple_of(c * CHUNK, CHUNK)
        new = []
        for h, hd in enumerate(heads):
            qq = q_ref[0, pl.ds(base, CHUNK), hd["cols"]]
            kk = k_ref[0, pl.ds(base, CHUNK), hd["cols"]]
            vv = v_ref[0, pl.ds(base, CHUNK), hd["cols"]]
            s_b = s_bs[h]
            scores = (_dot_nt(qq, kk) * hd["dmat"]).astype(BF16)
            y = (_dot(scores, vv)
                 + _dot(qq, sf_scr[c, h].astype(BF16)) * hd["dq_f"]
                 + _dot(qq, s_b.astype(BF16)) * hd["dq_b"])
            mu = jnp.mean(y, axis=-1, keepdims=True)
            yc = y - mu
            var = jnp.mean(yc * yc, axis=-1, keepdims=True)
            yn = (yc * lax.rsqrt(var + EPS)) * gn_ref[:, hd["cols"]]
            gate = _silu(g_ref[0, pl.ds(base, CHUNK), hd["cols"]].astype(F32))
            y_ref[0, pl.ds(base, CHUNK), hd["cols"]] = (yn * gate).astype(BF16)
            new.append(hd["dc_b"] * s_b + _dot_tn((kk.astype(F32) * hd["dk_b"]).astype(BF16), vv))
        return tuple(new)

    lax.fori_loop(0, nch, bwd, tuple(s_b0), unroll=RET_UNROLL)


def _retention(q, k, v, g, kc, vc, decay_f, decay_b, gn_w):
    b, L, w = q.shape
    Lc = kc.shape[1]
    nch = L // CHUNK
    per_b = lambda n: pl.BlockSpec((1, n, w), lambda i: (i, 0, 0))
    return pl.pallas_call(
        _ret_kernel,
        grid=(b,),
        in_specs=[per_b(L), per_b(L), per_b(L), per_b(L), per_b(Lc), per_b(Lc),
                  _const_spec((1, w)), _const_spec((1, w)), _const_spec((1, w))],
        out_specs=per_b(L),
        out_shape=jax.ShapeDtypeStruct((b, L, w), BF16),
        scratch_shapes=[
            pltpu.VMEM((nch, RET_HEADS, RET_DK, RET_DK), F32),
        ],
        compiler_params=pltpu.CompilerParams(vmem_limit_bytes=VMEM_LIMIT),
        name="retention",
    )(q, k, v, g, kc, vc, decay_f, decay_b, gn_w)


def _outproj_router_kernel(yr_ref, ys_ref, x_ref, g1_ref, sh2_ref, sc2_ref, npost_ref, npre_ref,
                           wor_ref, wos_ref, wr_ref, br_ref, tri_ref,
                           x1_ref, h2_ref, route_ref, slots_ref, seg_ref,
                           wcat):
    i = pl.program_id(0)

    @pl.when(i == 0)
    def _():
        wr = wr_ref[...]
        hi = wr.astype(BF16)
        wcat[:, :LANES] = hi
        wcat[:, LANES:] = (wr - hi.astype(F32)).astype(BF16)

    _route_tile(yr_ref, ys_ref, x_ref, g1_ref, sh2_ref, sc2_ref, npost_ref, npre_ref, wor_ref, wos_ref, br_ref,
                tri_ref, x1_ref, h2_ref, route_ref, slots_ref, seg_ref, wcat)


def _route_tile(yr_ref, ys_ref, x_ref, g1_ref, sh2_ref, sc2_ref, npost_ref, npre_ref, wor_ref, wos_ref, br_ref,
                tri_ref, x1_ref, h2_ref, route_ref, slots_ref, seg_ref, wcat):
    tm = x_ref.shape[0]
    rows = slice(0, tm)
    y = _dot(yr_ref[rows, :], wor_ref[...]) + _dot(ys_ref[rows, :], wos_ref[...])
    ms = jnp.mean(y * y, axis=-1, keepdims=True)
    x1 = x_ref[rows, :] + (y * lax.rsqrt(ms + EPS)) * (g1_ref[0] * npost_ref[...])
    x1_ref[rows, :] = x1
    h2 = _norm_mod(x1, npre_ref[...], sc2_ref[0], sh2_ref[0])
    h2_ref[rows, :] = h2

    h_hi = h2.astype(BF16)
    h_lo = (h2 - h_hi.astype(F32)).astype(BF16)
    both = _dot(h_hi, wcat[...])
    lg = both[:, :LANES] + both[:, LANES:] + _dot(h_lo, wcat[:, :LANES]) + br_ref[...]

    lane = lax.broadcasted_iota(jnp.int32, (tm, LANES), 1)
    lane_f = lane.astype(F32)
    is_grp = (lane >= N_EXPERTS) & (lane < N_EXPERTS + MOE_GROUPS)
    gl = jnp.where(is_grp, lg, NEG_BIG)
    mg = jnp.max(gl, axis=-1, keepdims=True)
    grp_lane = jnp.min(jnp.where(gl == mg, lane_f, 1e9), axis=-1, keepdims=True)
    p_g = 1.0 / jnp.sum(jnp.where(is_grp, jnp.exp(gl - mg), 0.0), axis=-1, keepdims=True)
    first = (grp_lane - N_EXPERTS) * EXPERTS_PER_GROUP
    in_grp = (lane_f >= first) & (lane_f < first + EXPERTS_PER_GROUP)
    el = jnp.where(in_grp, lg, NEG_BIG)
    t1 = jnp.max(el, axis=-1, keepdims=True)
    i1 = jnp.min(jnp.where(el == t1, lane_f, 1e9), axis=-1, keepdims=True)
    el2 = jnp.where(lane_f == i1, NEG_BIG, el)
    t2 = jnp.max(el2, axis=-1, keepdims=True)
    i2 = jnp.min(jnp.where(el2 == t2, lane_f, 1e9), axis=-1, keepdims=True)
    s = jnp.exp(t2 - t1)
    w1 = p_g / (1.0 + s)
    w2 = p_g * s / (1.0 + s)

    oh1 = (lane_f == i1)
    oh2 = (lane_f == i2)
    ohf = jnp.where(oh1 | oh2, 1.0, 0.0)
    before = _dot(tri_ref[...], ohf.astype(BF16))
    cnt = jnp.sum(ohf, axis=0, keepdims=True)
    seg = jnp.floor((cnt + (SUBLANES - 1.0)) * (1.0 / SUBLANES)) * SUBLANES
    e_row = lax.broadcasted_iota(jnp.int32, (LANES, LANES), 0)
    e_col = lax.broadcasted_iota(jnp.int32, (LANES, LANES), 1)
    earlier = (e_row < e_col).astype(BF16)
    seg_off = _dot(jnp.broadcast_to(seg, (SUBLANES, LANES)).astype(BF16), earlier)[0:1]
    where_to = before + seg_off
    lpos1 = jnp.sum(jnp.where(oh1, where_to, 0.0), axis=-1, keepdims=True)
    lpos2 = jnp.sum(jnp.where(oh2, where_to, 0.0), axis=-1, keepdims=True)

    cols = [w1, w2, lpos1, lpos2]
    for wk in (w1, w2):
        hi = wk.astype(BF16).astype(F32)
        cols += [hi, wk - hi]
    cols.append(jnp.ones_like(w1))
    packed = jnp.zeros((tm, LANES), F32)
    for k, col in enumerate(cols):
        packed = jnp.where(lane == k, col, packed)
    route_ref[rows, :] = packed

    row = lax.broadcasted_iota(jnp.int32, (tm, LANES), 0)
    on_diag = (row % LANES) == lane
    per = tm // LANES
    for qi, col in enumerate((lpos1, lpos2)):
        picked = jnp.where(on_diag, col, 0.0)
        dense = jnp.sum(picked.reshape(per, LANES, LANES), axis=1).astype(jnp.int32)
        slots_ref[0, qi * per:(qi + 1) * per, :] = dense
    tbl_row = lax.broadcasted_iota(jnp.int32, (SUBLANES, LANES), 0)
    seg_ref[0] = jnp.where(tbl_row == 0, seg, jnp.where(tbl_row == 1, seg_off, 0.0)).astype(jnp.int32)


def _outproj_router(yr, ys, x2, mod3, npost, npre, wo_r, wo_s, w_router, b_router, seq_len):
    T, d = x2.shape
    tm = TM_OUT
    per_seq = seq_len // tm
    rw = yr.shape[1]
    tri = (jnp.arange(tm)[:, None] > jnp.arange(tm)[None, :]).astype(BF16)
    tok = lambda w: pl.BlockSpec((tm, w), lambda i: (i, 0))
    modv = lambda k: pl.BlockSpec((1, 1, d), lambda i: (i // per_seq, 0, k))
    tile3 = lambda r: pl.BlockSpec((1, r, LANES), lambda i: (i, 0, 0))
    slot_rows = TOP_K * (tm // LANES)
    return pl.pallas_call(
        _outproj_router_kernel,
        grid=(T // tm,),
        in_specs=[
            tok(rw), tok(rw), tok(d), modv(2), modv(3), modv(4),
            _const_spec((1, d)), _const_spec((1, d)),
            _const_spec(wo_r.shape), _const_spec(wo_s.shape), _const_spec(w_router.shape), _const_spec((1, LANES)),
            _const_spec((tm, tm)),
        ],
        out_specs=[tok(d), tok(d), tok(LANES), tile3(slot_rows), tile3(SUBLANES)],
        out_shape=[jax.ShapeDtypeStruct((T, d), F32), jax.ShapeDtypeStruct((T, d), F32),
                   jax.ShapeDtypeStruct((T, LANES), F32),
                   jax.ShapeDtypeStruct((T // tm, slot_rows, LANES), jnp.int32),
                   jax.ShapeDtypeStruct((T // tm, SUBLANES, LANES), jnp.int32)],
        scratch_shapes=[pltpu.VMEM((d, 2 * LANES), BF16)],
        compiler_params=pltpu.CompilerParams(dimension_semantics=("arbitrary",),
                                             vmem_limit_bytes=VMEM_LIMIT),
        name="outproj_router",
    )(yr, ys, x2, mod3, mod3, mod3, npost, npre, wo_r, wo_s, w_router, b_router, tri)


def _expert_kernel(be_ref, first_ref, slot_ref, next_ref, nused_ref, xs_ref, wg_hbm, wu_hbm, wd_hbm, y_ref,
                   wg_f, wu_f, wd_f, wg_b, wu_b, wd_b, sem):
    i = pl.program_id(0)

    def fetch(e, s):
        return [pltpu.make_async_copy(src.at[e], dst.at[s], sem.at[s, k])
                for k, (src, dst) in enumerate(((wg_hbm, wg_f), (wu_hbm, wu_f), (wd_hbm, wd_f)))]

    @pl.when(i == 0)
    def _():
        for c in fetch(be_ref[0], slot_ref[0]):
            c.start()

    @pl.when(first_ref[i] == 1)
    def _():
        s = slot_ref[i]

        @pl.when(next_ref[i] >= 0)
        def _():
            for c in fetch(next_ref[i], 1 - s):
                c.start()

        for c in fetch(be_ref[i], s):
            c.wait()
        wg_b[...] = wg_f[s].astype(BF16)
        wu_b[...] = wu_f[s].astype(BF16)
        wd_b[...] = wd_f[s].astype(BF16)

    @pl.when(i < nused_ref[0])
    def _():
        d = y_ref.shape[1]
        side = xs_ref[:, d:d + LANES]
        second = side[:, 8:9] == 2.0
        unscale = jnp.where(second, 0.5, 1.0)
        weight = jnp.where(second, side[:, 6:7] + side[:, 7:8], side[:, 4:5] + side[:, 5:6]) * unscale
        xb = (xs_ref[:, 0:d] * unscale).astype(BF16)
        hid = (_silu(_dot(xb, wg_b[...])) * _dot(xb, wu_b[...])).astype(BF16)
        y_ref[...] = (_dot(hid, wd_b[...]) * weight).astype(BF16).astype(F32)

    @pl.when(i >= nused_ref[0])
    def _():
        y_ref[...] = jnp.zeros_like(y_ref)


def _expert_plan(padded, pad_end, n_blocks, mb):
    n_used = (pad_end[-1:] // mb).astype(jnp.int32)
    blk_start = jnp.arange(n_blocks, dtype=jnp.int32) * mb
    experts = jnp.arange(N_EXPERTS, dtype=jnp.int32)
    blk_expert = jnp.minimum(jnp.sum((pad_end[None, :] <= blk_start[:, None]).astype(jnp.int32), axis=1),
                             N_EXPERTS - 1)
    prev = jnp.concatenate([jnp.full((1,), -1, jnp.int32), blk_expert[:-1]])
    first = ((blk_start < pad_end[-1]) & (blk_expert != prev)).astype(jnp.int32)
    slot = (jnp.cumsum(first) - 1) % 2
    later = jnp.where((padded > 0)[None, :] & (experts[None, :] > experts[:, None]), experts[None, :], N_EXPERTS)
    next_nonempty = jnp.min(later, axis=1)
    next_nonempty = jnp.where(next_nonempty == N_EXPERTS, -1, next_nonempty)
    nxt = jnp.sum(jnp.where(blk_expert[:, None] == experts[None, :], next_nonempty[None, :], 0), axis=1)
    return [a.astype(jnp.int32) for a in (blk_expert, first, slot, nxt, n_used)]


def _experts(plan, xs, w_gate, w_up, w_down):
    rows, dp = xs.shape
    n_exp, d, de = w_gate.shape
    assert dp == d + LANES
    mb = MB_EXPERT
    cap = rows // mb * mb
    grid_spec = pltpu.PrefetchScalarGridSpec(
        num_scalar_prefetch=len(plan),
        grid=(cap // mb,),
        in_specs=[
            pl.BlockSpec((mb, dp), lambda i, be, fi, sl, nx, nu: (jnp.minimum(i, nu[0] - 1), 0)),
            pl.BlockSpec(memory_space=pl.ANY), pl.BlockSpec(memory_space=pl.ANY), pl.BlockSpec(memory_space=pl.ANY),
        ],
        out_specs=pl.BlockSpec((mb, d), lambda i, be, fi, sl, nx, nu: (i, 0)),
        scratch_shapes=[pltpu.VMEM((2, d, de), F32), pltpu.VMEM((2, d, de), F32), pltpu.VMEM((2, de, d), F32),
                        pltpu.VMEM((d, de), BF16), pltpu.VMEM((d, de), BF16), pltpu.VMEM((de, d), BF16),
                        pltpu.SemaphoreType.DMA((2, 3))],
    )
    return pl.pallas_call(
        _expert_kernel,
        grid_spec=grid_spec,
        out_shape=jax.ShapeDtypeStruct((cap, d), F32),
        compiler_params=pltpu.CompilerParams(dimension_semantics=("arbitrary",),
                                             vmem_limit_bytes=VMEM_LIMIT),
        name="experts",
    )(*plan, xs, w_gate, w_up, w_down)


def _segment_pieces(tile, segrow_ref, seglen_ref, segoff_ref, act):
    def per_expert(e, carry):
        idx = tile * N_EXPERTS + e
        g0 = segrow_ref[idx]
        l0 = segoff_ref[idx]

        def piece(j, c2):
            act(pl.multiple_of(l0 + j * SUBLANES, SUBLANES), pl.multiple_of(g0 + j * SUBLANES, SUBLANES))
            return c2

        lax.fori_loop(0, seglen_ref[idx], piece, 0)
        return carry

    lax.fori_loop(0, N_EXPERTS, per_expert, 0)


def _wait_rows(total, row_copy):
    size = SUBLANES
    while size <= _stage_rows(max(TD_DISPATCH, TF_COMBINE)):
        @pl.when((total & size) != 0)
        def _(size=size):
            row_copy(size).wait()
        size *= 2


def _stage_rows(tile_tokens):
    return TOP_K * tile_tokens + N_EXPERTS * SUBLANES


def _dispatch_seg_kernel(pad_end_ref, zero_from_ref, segrow_ref, seglen_ref, segoff_ref, tilerows_ref,
                         lpos_ref, route_ref, h_ref, xs_hbm, zbuf, stage, sem, zsem):
    i = pl.program_id(0)
    nt = pl.num_programs(0)
    td, d = h_ref.shape
    sr = stage.shape[1]
    per = td // LANES

    def zero_fills(act):
        def fill(row, n):
            act(pltpu.make_async_copy(zbuf.at[pl.ds(0, n)], xs_hbm.at[pl.ds(pl.multiple_of(row, SUBLANES), n)], zsem))

        def region(e, carry):
            row = zero_from_ref[e]
            left = pad_end_ref[e] - row
            size = ZERO_PIECE
            while size >= SUBLANES:
                take = (left & size) != 0

                @pl.when(take)
                def _(row=row, size=size):
                    fill(row, size)

                row = row + jnp.where(take, size, 0)
                size //= 2
            return carry

        def tail(p, carry):
            @pl.when(p * ZERO_PIECE >= pad_end_ref[N_EXPERTS - 1])
            def _():
                fill(p * ZERO_PIECE, ZERO_PIECE)
            return carry

        lax.fori_loop(0, N_EXPERTS, region, 0)
        lax.fori_loop(0, xs_hbm.shape[0] // ZERO_PIECE, tail, 0)

    @pl.when(i == 0)
    def _():
        zbuf[...] = jnp.zeros_like(zbuf)
        zero_fills(lambda c: c.start())

    def shipped(tile):
        _wait_rows(tilerows_ref[tile], lambda n: pltpu.make_async_copy(
            stage.at[tile % 2, pl.ds(0, n)], xs_hbm.at[pl.ds(0, n)], sem.at[tile % 2]))

    @pl.when(i >= 2)
    def _():
        shipped(i - 2)

    srow = lax.broadcasted_iota(jnp.int32, (sr, LANES), 0)
    place = jnp.concatenate(
        [jnp.where(srow == lpos_ref[0, cb:cb + 1, :], 1.0,
                   jnp.where(srow == lpos_ref[0, per + cb:per + cb + 1, :], 2.0, 0.0)) for cb in range(per)],
        axis=1).astype(BF16)
    slot = i % 2
    stage[slot] = _dot(place, jnp.concatenate([h_ref[...].astype(BF16), route_ref[...].astype(BF16)], axis=1))
    _segment_pieces(i, segrow_ref, seglen_ref, segoff_ref, lambda lr, gr: pltpu.make_async_copy(
        stage.at[slot, pl.ds(lr, SUBLANES)], xs_hbm.at[pl.ds(gr, SUBLANES)], sem.at[slot]).start())

    @pl.when(i == nt - 1)
    def _():
        @pl.when(i >= 1)
        def _():
            shipped(i - 1)
        shipped(i)
        zero_fills(lambda c: c.wait())


def _dispatch_seg(plan, lpos, route, h2, cap):
    T, d = h2.shape
    td = TD_DISPATCH
    nt = T // td
    width = d + LANES
    sr = _stage_rows(td)
    n_pre = len(plan)
    grid_spec = pltpu.PrefetchScalarGridSpec(
        num_scalar_prefetch=n_pre,
        grid=(nt,),
        in_specs=[
            pl.BlockSpec((1, lpos.shape[1], LANES), lambda i, *_: (i, 0, 0)),
            pl.BlockSpec((td, LANES), lambda i, *_: (i, 0)),
            pl.BlockSpec((td, d), lambda i, *_: (i, 0)),
        ],
        out_specs=pl.BlockSpec(memory_space=pl.ANY),
        scratch_shapes=[pltpu.VMEM((ZERO_PIECE, width), F32), pltpu.VMEM((2, sr, width), F32),
                        pltpu.SemaphoreType.DMA((2,)), pltpu.SemaphoreType.DMA(())],
    )
    return pl.pallas_call(
        _dispatch_seg_kernel,
        grid_spec=grid_spec,
        out_shape=jax.ShapeDtypeStruct((cap, width), F32),
        compiler_params=pltpu.CompilerParams(dimension_semantics=("arbitrary",), vmem_limit_bytes=VMEM_LIMIT),
        name="dispatch",
    )(*plan, lpos, route, h2)


def _combine_seg_kernel(segrow_ref, seglen_ref, segoff_ref, tilerows_ref, route_ref, x1_ref, g2_ref, nw_ref,
                        yb_hbm, o_ref, stage, sem):
    i = pl.program_id(0)
    nt = pl.num_programs(0)
    tf = x1_ref.shape[0]
    sr = stage.shape[1]

    def fetch(tile, slot):
        _segment_pieces(tile, segrow_ref, seglen_ref, segoff_ref, lambda lr, gr: pltpu.make_async_copy(
            yb_hbm.at[pl.ds(gr, SUBLANES)], stage.at[slot, pl.ds(lr, SUBLANES)], sem.at[slot]).start())

    @pl.when(i == 0)
    def _():
        stage[...] = jnp.zeros_like(stage)
        fetch(i, 0)

    for slot in range(2):
        @pl.when((i + 1 < nt) & (i % 2 != slot))
        def _(slot=slot):
            fetch(i + 1, slot)

    _wait_rows(tilerows_ref[i], lambda n: pltpu.make_async_copy(
        yb_hbm.at[pl.ds(0, n)], stage.at[i % 2, pl.ds(0, n)], sem.at[i % 2]))

    route = route_ref[...]
    scol = lax.broadcasted_iota(jnp.int32, (tf, sr), 1).astype(F32)
    pick = jnp.where((scol == route[:, 2:3]) | (scol == route[:, 3:4]), 1.0, 0.0).astype(BF16)
    out = _dot(pick, stage[i % 2].astype(BF16))
    ms = jnp.mean(out * out, axis=-1, keepdims=True)
    o_ref[...] = x1_ref[...] + g2_ref[0] * ((out * lax.rsqrt(ms + EPS)) * nw_ref[...])


def _combine_seg(seg_plan, route, x1, mod3, nw, yb, seq_len):
    T, d = x1.shape
    tf = TF_COMBINE
    nt = T // tf
    per_seq = seq_len // tf
    grid_spec = pltpu.PrefetchScalarGridSpec(
        num_scalar_prefetch=len(seg_plan),
        grid=(nt,),
        in_specs=[
            pl.BlockSpec((tf, LANES), lambda i, *_: (i, 0)),
            pl.BlockSpec((tf, d), lambda i, *_: (i, 0)),
            pl.BlockSpec((1, 1, d), lambda i, *_: (i // per_seq, 0, 5)),
            pl.BlockSpec((1, d), lambda i, *_: (0, 0)),
            pl.BlockSpec(memory_space=pl.ANY),
        ],
        out_specs=pl.BlockSpec((tf, d), lambda i, *_: (i, 0)),
        scratch_shapes=[pltpu.VMEM((2, _stage_rows(tf), d), F32), pltpu.SemaphoreType.DMA((2,))],
    )
    return pl.pallas_call(
        _combine_seg_kernel,
        grid_spec=grid_spec,
        out_shape=jax.ShapeDtypeStruct((T, d), F32),
        compiler_params=pltpu.CompilerParams(dimension_semantics=("arbitrary",), vmem_limit_bytes=VMEM_LIMIT),
        name="combine",
    )(*seg_plan, route, x1, mod3, nw, yb)


def _rope_tables(L, n_heads):
    quarter = RET_DK // 4
    freqs = ROPE_BASE ** (-jnp.arange(quarter, dtype=F32) / quarter)
    t = jnp.arange(L)
    ang_r = (t // GRID_W).astype(F32)[:, None] * freqs
    ang_c = (t % GRID_W).astype(F32)[:, None] * freqs
    cos = jnp.concatenate([jnp.cos(ang_r)] * 2 + [jnp.cos(ang_c)] * 2, axis=-1)
    sin = jnp.concatenate([-jnp.sin(ang_r), jnp.sin(ang_r), -jnp.sin(ang_c), jnp.sin(ang_c)], axis=-1)
    return jnp.tile(cos, (1, n_heads)), jnp.tile(sin, (1, n_heads))


def _lane_pad(v, width=LANES):
    return jnp.pad(v, [(0, 0)] * (v.ndim - 1) + [(0, width - v.shape[-1])])


def kernel(x, c, ctx, c_ctx, w_mod, b_mod, norm_pre_mix, norm_post_mix, norm_pre_ffn, norm_post_ffn, w_in, w_out, ret_decay_f, ret_decay_b, ret_gn_w, ssd_conv_w, ssd_conv_b, ssd_dt_bias_f, ssd_dt_bias_b, ssd_a_log_f, ssd_a_log_b, ssd_d, ssd_norm_w, moe_w_rg, moe_b_rg, moe_w_re, moe_b_re, moe_w_gate, moe_w_up, moe_w_down):
    b, L, d = x.shape
    assert w_mod.shape[0] == 1, "single layer: context outputs are never needed"
    assert TM_OUT == TD_DISPATCH == TF_COMBINE, "router, dispatch and combine share one slot-row layout"
    rw = RET_HEADS * RET_DK
    nconv = SSD_WIDTH + 2 * SSD_GROUPS * SSD_STATE
    T = b * L

    mod_rows = -(-(b + 1) // SUBLANES) * SUBLANES
    c_all = jnp.zeros((mod_rows, d), F32).at[:b].set(c).at[b].set(c_ctx)
    mod3 = _modulation(c_all, w_mod[0], b_mod[0]).reshape(mod_rows, 1, 6 * d)

    wi = w_in[0]
    o = 0
    wq = wi[:, o:o + rw]; o += rw
    wk = wi[:, o:o + rw]; o += rw
    wv = wi[:, o:o + rw]; o += rw
    wg = wi[:, o:o + rw]; o += rw
    wz = wi[:, o:o + SSD_WIDTH]; o += SSD_WIDTH
    wxbc = wi[:, o:o + nconv].astype(BF16); o += nconv
    wdt = _lane_pad(wi[:, o:o + 2 * SSD_HEADS]).astype(BF16)
    wqk = jnp.concatenate([wq, wk], axis=1).astype(BF16)
    wvgz = jnp.concatenate([wv, wg, wz], axis=1).astype(BF16)
    cos_t, sin_t = _rope_tables(L, RET_HEADS)
    nw1 = norm_pre_mix[0].reshape(1, d)

    q, k, v, g, z, xbc, dt = _inproj(x, mod3, nw1, wqk, wvgz, wxbc, wdt, cos_t, sin_t)
    kc, vc, xbcc, dtc = _inproj_ctx(ctx, mod3, b, nw1, wk.astype(BF16), wv.astype(BF16), wxbc, wdt)

    conv_w8 = jnp.pad(ssd_conv_w[0], ((0, SUBLANES - SSD_CONV), (0, 0)))
    dt_bias = _lane_pad(jnp.concatenate([ssd_dt_bias_f[0], ssd_dt_bias_b[0]])[None, :])
    a_log = _lane_pad(jnp.concatenate([ssd_a_log_f[0], ssd_a_log_b[0]])[None, :])
    d_skip = jnp.repeat(ssd_d[0], SSD_HEADDIM)[None, :]
    ys = _ssd(xbc, z, dt, xbcc, dtc, conv_w8, ssd_conv_b[0][None, :], dt_bias, a_log, d_skip,
              ssd_norm_w[0][None, :])

    yr = _retention(q, k, v, g, kc, vc,
                    jnp.repeat(ret_decay_f[0], RET_DK)[None, :], jnp.repeat(ret_decay_b[0], RET_DK)[None, :],
                    ret_gn_w[0][None, :])

    wo = w_out[0].astype(BF16)
    w_router = _lane_pad(jnp.concatenate(
        [jnp.transpose(moe_w_re[0], (1, 0, 2)).reshape(d, N_EXPERTS), moe_w_rg[0]], axis=1))
    b_router = _lane_pad(jnp.concatenate([moe_b_re[0].reshape(-1), moe_b_rg[0]])[None, :])
    x1, h2, route, lpos, seg = _outproj_router(
        yr.reshape(T, rw), ys.reshape(T, SSD_WIDTH), x.reshape(T, d), mod3,
        norm_post_mix[0][None, :], norm_pre_ffn[0][None, :], wo[:rw], wo[rw:], w_router, b_router, L)

    mb = MB_EXPERT
    nt = T // TM_OUT
    n_blocks = -(-(T * TOP_K + nt * N_EXPERTS * (SUBLANES - 1) + N_EXPERTS * (mb - 1)) // mb)
    seg_len = seg[:, 0, :N_EXPERTS]
    seg_off = seg[:, 1, :N_EXPERTS]
    used = jnp.sum(seg_len, axis=0)
    padded = (used + mb - 1) // mb * mb
    pad_end = jnp.cumsum(padded)
    pad_start = pad_end - padded
    seg_row = pad_start[None, :] + jnp.cumsum(seg_len, axis=0) - seg_len
    seg_plan = [a.reshape(-1).astype(jnp.int32)
                for a in (seg_row, seg_len // SUBLANES, seg_off, jnp.sum(seg_len, axis=1))]
    zero_from = (pad_start + used).astype(jnp.int32)

    xs = _dispatch_seg([pad_end.astype(jnp.int32), zero_from] + seg_plan, lpos, route, h2, n_blocks * mb)
    yb = _experts(_expert_plan(padded, pad_end, n_blocks, mb), xs, moe_w_gate[0], moe_w_up[0], moe_w_down[0])
    out = _combine_seg(seg_plan, route, x1, mod3, norm_post_ffn[0][None, :], yb, L)
    return out.reshape(b, L, d)
```

```python
import jax
import jax.numpy as jnp
from jax import lax
from jax.experimental import pallas as pl
from jax.experimental.pallas import tpu as pltpu

F32 = jnp.float32
BF16 = jnp.bfloat16

LANES = 128
SUBLANES = 8
V7X_VMEM_BYTES = 64 * 1024 * 1024
VMEM_LIMIT = V7X_VMEM_BYTES * 3 // 4
VMEM_LIMIT_SSD = V7X_VMEM_BYTES * 7 // 8

EPS = 1e-6
CHUNK = 128
GRID_W = 64
RET_HEADS = 4
RET_DK = 128
ROPE_BASE = 10000.0
SSD_HEADS = 8
SSD_HEADDIM = 64
SSD_GROUPS = 2
SSD_STATE = 128
SSD_WIDTH = SSD_HEADS * SSD_HEADDIM
SSD_CONV = 5
SSD_PAIRS = SSD_WIDTH // LANES
MOE_GROUPS = 4
EXPERTS_PER_GROUP = 8
N_EXPERTS = MOE_GROUPS * EXPERTS_PER_GROUP
TOP_K = 2
CONV_HALO = SUBLANES

TM_PROJ = 512
TM_OUT = 512
TD_DISPATCH = TM_OUT
MB_EXPERT = 512
ZERO_PIECE = MB_EXPERT // 2
TF_COMBINE = TM_OUT
RET_UNROLL = 2
NEG_BIG = -1e30


def _silu(v):
    return v * jax.nn.sigmoid(v)


def _dot(a, b):
    return jnp.dot(a, b, preferred_element_type=F32)


def _dot_tn(a, b):
    return lax.dot_general(a, b, (((0,), (0,)), ((), ())), preferred_element_type=F32)


def _dot_nt(a, b):
    return lax.dot_general(a, b, (((1,), (1,)), ((), ())), preferred_element_type=F32)


def _mod_kernel(c_ref, w_ref, b_ref, o_ref):
    a = _silu(c_ref[...])
    w = w_ref[...]
    a_hi = a.astype(BF16)
    a_lo = (a - a_hi.astype(F32)).astype(BF16)
    w_hi = w.astype(BF16)
    w_lo = (w - w_hi.astype(F32)).astype(BF16)
    o_ref[...] = _dot(a_hi, w_hi) + _dot(a_lo, w_hi) + _dot(a_hi, w_lo) + b_ref[...]


def _modulation(c_all, w_mod, b_mod):
    rows, d = c_all.shape
    n = w_mod.shape[1]
    return pl.pallas_call(
        _mod_kernel,
        grid=(n // d,),
        in_specs=[
            pl.BlockSpec((rows, d), lambda j: (0, 0)),
            pl.BlockSpec((d, d), lambda j: (0, j)),
            pl.BlockSpec((1, d), lambda j: (0, j)),
        ],
        out_specs=pl.BlockSpec((rows, d), lambda j: (0, j)),
        out_shape=jax.ShapeDtypeStruct((rows, n), F32),
        name="modulation",
    )(c_all, w_mod, b_mod.reshape(1, n))


def _norm_mod(x, nw, sc, sh):
    ms = jnp.mean(x * x, axis=-1, keepdims=True)
    return (x * lax.rsqrt(ms + EPS)) * (nw * (1.0 + sc)) + sh


def _rope(t, cos, sin_signed, first_half):
    width = t.shape[-1]
    quarter = RET_DK // 4
    swapped = jnp.where(first_half, pltpu.roll(t, width - quarter, 1), pltpu.roll(t, quarter, 1))
    return t * cos + swapped * sin_signed


def _inproj_kernel(x_ref, sh_ref, sc_ref, nw_ref, wqk_ref, wvgz_ref, wxbc_ref, wdt_ref, cos_ref, sin_ref,
                   q_ref, k_ref, v_ref, g_ref, z_ref, xbc_ref, dt_ref):
    hb = _norm_mod(x_ref[0], nw_ref[...], sc_ref[0], sh_ref[0]).astype(BF16)
    rw = q_ref.shape[-1]
    qk = _dot(hb, wqk_ref[...])
    cos = cos_ref[...]
    sin = sin_ref[...]
    lane = lax.broadcasted_iota(jnp.int32, cos.shape, 1)
    first_half = (lane % (RET_DK // 2)) < (RET_DK // 4)
    q_ref[0] = _rope(qk[:, :rw], cos, sin, first_half).astype(BF16)
    k_ref[0] = (_rope(qk[:, rw:], cos, sin, first_half) * (RET_DK ** -0.5)).astype(BF16)
    vgz = _dot(hb, wvgz_ref[...])
    v_ref[0] = vgz[:, :rw].astype(BF16)
    g_ref[0] = vgz[:, rw:2 * rw].astype(BF16)
    z_ref[0] = vgz[:, 2 * rw:].astype(BF16)
    xbc_ref[0] = _dot(hb, wxbc_ref[...]).astype(BF16)
    dt_ref[0] = _dot(hb, wdt_ref[...])


def _inproj_ctx_kernel(x_ref, sh_ref, sc_ref, nw_ref, wk_ref, wv_ref, wxbc_ref, wdt_ref,
                       k_ref, v_ref, xbc_ref, dt_ref):
    hb = _norm_mod(x_ref[0], nw_ref[...], sc_ref[0], sh_ref[0]).astype(BF16)
    k_ref[0] = (_dot(hb, wk_ref[...]) * (RET_DK ** -0.5)).astype(BF16)
    v_ref[0] = _dot(hb, wv_ref[...]).astype(BF16)
    xbc_ref[0] = _dot(hb, wxbc_ref[...]).astype(BF16)
    dt_ref[0] = _dot(hb, wdt_ref[...])


def _const_spec(shape):
    nd = len(shape)
    return pl.BlockSpec(shape, lambda *_: (0,) * nd)


def _inproj(x, mod3, nw, wqk, wvgz, wxbc, wdt, cos_t, sin_t):
    b, L, d = x.shape
    tm = min(TM_PROJ, L)
    rw = wqk.shape[1] // 2
    tok = lambda w: pl.BlockSpec((1, tm, w), lambda i, j: (i, j, 0))
    out_bf = lambda w: jax.ShapeDtypeStruct((b, L, w), BF16)
    return pl.pallas_call(
        _inproj_kernel,
        grid=(b, L // tm),
        in_specs=[
            tok(d),
            pl.BlockSpec((1, 1, d), lambda i, j: (i, 0, 0)),
            pl.BlockSpec((1, 1, d), lambda i, j: (i, 0, 1)),
            _const_spec((1, d)),
            _const_spec(wqk.shape), _const_spec(wvgz.shape), _const_spec(wxbc.shape), _const_spec(wdt.shape),
            pl.BlockSpec((tm, rw), lambda i, j: (j, 0)),
            pl.BlockSpec((tm, rw), lambda i, j: (j, 0)),
        ],
        out_specs=[tok(rw), tok(rw), tok(rw), tok(rw), tok(rw), tok(wxbc.shape[1]), tok(LANES)],
        out_shape=[out_bf(rw), out_bf(rw), out_bf(rw), out_bf(rw), out_bf(rw), out_bf(wxbc.shape[1]),
                   jax.ShapeDtypeStruct((b, L, LANES), F32)],
        compiler_params=pltpu.CompilerParams(vmem_limit_bytes=VMEM_LIMIT),
        name="inproj",
    )(x, mod3, mod3, nw, wqk, wvgz, wxbc, wdt, cos_t, sin_t)


def _inproj_ctx(ctx, mod3, ctx_row, nw, wk, wv, wxbc, wdt):
    b, L, d = ctx.shape
    tm = min(TM_PROJ, L)
    rw = wk.shape[1]
    tok = lambda w: pl.BlockSpec((1, tm, w), lambda i, j: (i, j, 0))
    out_bf = lambda w: jax.ShapeDtypeStruct((b, L, w), BF16)
    return pl.pallas_call(
        _inproj_ctx_kernel,
        grid=(b, L // tm),
        in_specs=[
            tok(d),
            pl.BlockSpec((1, 1, d), lambda i, j: (ctx_row, 0, 0)),
            pl.BlockSpec((1, 1, d), lambda i, j: (ctx_row, 0, 1)),
            _const_spec((1, d)),
            _const_spec(wk.shape), _const_spec(wv.shape), _const_spec(wxbc.shape), _const_spec(wdt.shape),
        ],
        out_specs=[tok(rw), tok(rw), tok(wxbc.shape[1]), tok(LANES)],
        out_shape=[out_bf(rw), out_bf(rw), out_bf(wxbc.shape[1]), jax.ShapeDtypeStruct((b, L, LANES), F32)],
        compiler_params=pltpu.CompilerParams(vmem_limit_bytes=VMEM_LIMIT),
        name="inproj_ctx",
    )(ctx, mod3, mod3, nw, wk, wv, wxbc, wdt)


def _ssd_kernel(xbc_ref, z_ref, dt_ref, xbcc_ref, dtc_ref, cw_ref, cb_ref, dtb_ref, alog_ref, dsk_ref, nw_ref,
                y_ref,
                xpad, xpadc, u, uc, dtv, dav, dtcv, dacv, sf_scr, kb_scr, acum, ecum, dec_scr,
                arow_scr, erow_scr, dtrow_scr):
    L = xbc_ref.shape[1]
    Lc = xbcc_ref.shape[1]
    nch = L // CHUNK
    nchc = Lc // CHUNK
    win = CHUNK + 2 * CONV_HALO
    nconv = xbc_ref.shape[2]
    nh = SSD_HEADS

    def conv_pass(src_ref, pad_ref, dst_ref, n_chunks, length):
        zeros = jnp.zeros((CONV_HALO, nconv), F32)
        pad_ref[0:CONV_HALO, :] = zeros
        pad_ref[CONV_HALO + length:2 * CONV_HALO + length, :] = zeros
        pad_ref[CONV_HALO:CONV_HALO + length, :] = src_ref[0].astype(F32)

        def chunk(c, carry):
            base = pl.multiple_of(c * CHUNK, CHUNK)
            for cb_i in range(nconv // LANES):
                cols = slice(cb_i * LANES, (cb_i + 1) * LANES)
                w = pad_ref[pl.ds(base, win), cols]
                acc = cb_ref[:, cols] + w[CONV_HALO:CONV_HALO + CHUNK] * cw_ref[SSD_CONV // 2:SSD_CONV // 2 + 1, cols]
                for j in range(SSD_CONV):
                    if j == SSD_CONV // 2:
                        continue
                    shifted = pltpu.roll(w, (SSD_CONV // 2 - j) % win, 0)
                    acc = acc + shifted[CONV_HALO:CONV_HALO + CHUNK] * cw_ref[j:j + 1, cols]
                dst_ref[pl.ds(base, CHUNK), cols] = _silu(acc).astype(BF16)
            return carry

        lax.fori_loop(0, n_chunks, chunk, 0)

    conv_pass(xbcc_ref, xpadc, uc, nchc, Lc)
    conv_pass(xbc_ref, xpad, u, nch, L)

    a_neg = -jnp.exp(alog_ref[...])
    dtv[...] = jax.nn.softplus(dt_ref[0] + dtb_ref[...])
    dav[...] = dtv[...] * a_neg
    dtcv[...] = jax.nn.softplus(dtc_ref[0] + dtb_ref[...])
    dacv[...] = dtcv[...] * a_neg

    row_i = lax.broadcasted_iota(jnp.int32, (CHUNK, CHUNK), 0)
    col_i = lax.broadcasted_iota(jnp.int32, (CHUNK, CHUNK), 1)
    causal = col_i <= row_i
    lo_half = col_i < SSD_HEADDIM
    fwd_lane = col_i < nh
    head_of = lax.broadcasted_iota(jnp.int32, (CHUNK, SSD_WIDTH), 1) // SSD_HEADDIM
    src_col = lax.broadcasted_iota(jnp.int32, (CHUNK, SSD_WIDTH), 0)
    exp_f = (head_of == src_col).astype(BF16)
    exp_b = (head_of == src_col - nh).astype(BF16)
    exp_fb = jnp.concatenate([exp_f, exp_b], axis=1)

    def split3(v):
        hi = v.astype(BF16)
        r1 = v - hi.astype(F32)
        mid = r1.astype(BF16)
        return hi, mid, (r1 - mid.astype(F32)).astype(BF16)

    def times_onehot(v, m, passes=3):
        parts = split3(v)[:passes]
        acc = _dot(parts[0], m)
        for part in parts[1:]:
            acc = acc + _dot(part, m)
        return acc

    def colb(mat, r):
        return jnp.broadcast_to(mat[:, r:r + 1], (CHUNK, CHUNK))

    def pair_sel(a, b_):
        return jnp.where(lo_half, a, b_)

    gw = 2 * LANES

    def chunk_terms(u_ref, dt_s, da_s, base):
        dt = dt_s[pl.ds(base, CHUNK), :]
        da = da_s[pl.ds(base, CHUNK), :]
        acol = da
        for step in (1, 2, 4, 8, 16, 32, 64):
            acol = acol + jnp.where(row_i >= step, pltpu.roll(acol, step, 0), 0.0)
        ecol = acol - da
        last = acol[CHUNK - 1:CHUNK, :]
        wgt = jnp.where(fwd_lane, jnp.exp(last - acol), jnp.exp(ecol)) * dt
        scale = jnp.where(fwd_lane, jnp.exp(acol), jnp.exp(last - ecol))
        wide = times_onehot(jnp.concatenate([wgt, scale], axis=0), exp_fb, passes=1)
        dec = times_onehot(jnp.broadcast_to(jnp.exp(last), (SUBLANES, LANES)), exp_fb)[0:1]
        xs = u_ref[pl.ds(base, CHUNK), 0:SSD_WIDTH].astype(F32)
        kmats = []
        for g in range(SSD_GROUPS):
            xw = jnp.concatenate([xs[:, g * gw:(g + 1) * gw] * wide[:CHUNK, g * gw:(g + 1) * gw],
                                  xs[:, g * gw:(g + 1) * gw] * wide[:CHUNK, SSD_WIDTH + g * gw:SSD_WIDTH + (g + 1) * gw]],
                                 axis=1).astype(BF16)
            bm = u_ref[pl.ds(base, CHUNK), SSD_WIDTH + g * SSD_STATE:SSD_WIDTH + (g + 1) * SSD_STATE]
            kmats.append(_dot_tn(bm, xw))
        return dt, acol, ecol, wide[CHUNK:], dec, kmats

    def advance(s, dec, kmats, backward):
        off = SSD_WIDTH if backward else 0
        koff = gw if backward else 0
        return [dec[:, off + g * gw:off + (g + 1) * gw] * s[g] + kmats[g][:, koff:koff + gw]
                for g in range(SSD_GROUPS)]

    ctx_terms = [chunk_terms(uc, dtcv, dacv, c * CHUNK) for c in range(nchc)]
    s_f0 = [jnp.zeros((SSD_STATE, gw), F32) for _ in range(SSD_GROUPS)]
    for c in range(nchc):
        s_f0 = advance(s_f0, ctx_terms[c][4], ctx_terms[c][5], False)
    s_b0 = [jnp.zeros((SSD_STATE, gw), F32) for _ in range(SSD_GROUPS)]
    for c in reversed(range(nchc)):
        s_b0 = advance(s_b0, ctx_terms[c][4], ctx_terms[c][5], True)

    def prep(c, carry):
        base = pl.multiple_of(c * CHUNK, CHUNK)
        dt, acol, ecol, scale, dec, kmats = chunk_terms(u, dtv, dav, base)
        acum[pl.ds(base, CHUNK), :] = acol
        ecum[pl.ds(base, CHUNK), :] = ecol
        hrow = pl.ds(pl.multiple_of(c * 2 * nh, 2 * nh), 2 * nh)
        arow_scr[hrow, :] = acol.T[:2 * nh]
        erow_scr[hrow, :] = ecol.T[:2 * nh]
        dtrow_scr[hrow, :] = dt.T[:2 * nh]
        xpad[pl.ds(base, CHUNK), :] = scale
        dec_scr[pl.ds(pl.multiple_of(c * SUBLANES, SUBLANES), SUBLANES), :] = jnp.broadcast_to(dec, (SUBLANES, 2 * SSD_WIDTH))
        for g in range(SSD_GROUPS):
            sf_scr[c, g] = kmats[g][:, :gw]
            kb_scr[c, g] = kmats[g][:, gw:]
        return carry

    lax.fori_loop(0, nch, prep, 0)

    def chunk_dec(c):
        return dec_scr[pl.ds(pl.multiple_of(c * SUBLANES, SUBLANES), 1), :]

    def fwd(c, s_old):
        dec = chunk_dec(c)
        new = []
        for g in range(SSD_GROUPS):
            new.append(dec[:, g * gw:(g + 1) * gw] * s_old[g] + sf_scr[c, g])
            sf_scr[c, g] = s_old[g]
        return tuple(new)

    lax.fori_loop(0, nch, fwd, tuple(s_f0))

    def bwd(i, s_b):
        c = nch - 1 - i
        base = pl.multiple_of(c * CHUNK, CHUNK)
        acol = acum[pl.ds(base, CHUNK), :]
        ecol = ecum[pl.ds(base, CHUNK), :]
        hrow = pl.ds(pl.multiple_of(c * 2 * nh, 2 * nh), 2 * nh)
        arow = arow_scr[hrow, :]
        erow = erow_scr[hrow, :]
        dt_t = dtrow_scr[hrow, :]
        scale = xpad[pl.ds(base, CHUNK), :]
        ys = []
        for g in range(SSD_GROUPS):
            bm = u[pl.ds(base, CHUNK), SSD_WIDTH + g * SSD_STATE:SSD_WIDTH + (g + 1) * SSD_STATE]
            cm = u[pl.ds(base, CHUNK), SSD_WIDTH + (SSD_GROUPS + g) * SSD_STATE:SSD_WIDTH + (SSD_GROUPS + g + 1) * SSD_STATE]
            cbm = _dot_nt(cm, bm)
            cs_f = _dot(cm, sf_scr[c, g].astype(BF16))
            cs_b = _dot(cm, s_b[g].astype(BF16))
            for pp in range(SSD_PAIRS // SSD_GROUPS):
                p = g * (SSD_PAIRS // SSD_GROUPS) + pp
                xs_b = u[pl.ds(base, CHUNK), p * LANES:(p + 1) * LANES]
                y_h = []
                for hh in range(2):
                    r = 2 * p + hh
                    arg = jnp.where(causal, colb(acol, r) - arow[r:r + 1, :],
                                    erow[nh + r:nh + r + 1, :] - colb(ecol, nh + r))
                    coef = jnp.where(causal, dt_t[r:r + 1, :], dt_t[nh + r:nh + r + 1, :])
                    gm = (cbm * (jnp.exp(arg) * coef)).astype(BF16)
                    y_h.append(_dot(gm, xs_b))
                sl = slice(pp * LANES, (pp + 1) * LANES)
                wl = slice(p * LANES, (p + 1) * LANES)
                wlb = slice(SSD_WIDTH + p * LANES, SSD_WIDTH + (p + 1) * LANES)
                ys.append(pair_sel(y_h[0], y_h[1]) + cs_f[:, sl] * scale[:, wl] + cs_b[:, sl] * scale[:, wlb]
                          + dsk_ref[:, wl] * xs_b.astype(F32))
        y = jnp.concatenate(ys, axis=1)
        y = y * _silu(z_ref[0, pl.ds(base, CHUNK), :].astype(F32))
        ms = jnp.mean(y * y, axis=-1, keepdims=True)
        y_ref[0, pl.ds(base, CHUNK), :] = ((y * lax.rsqrt(ms + EPS)) * nw_ref[...]).astype(BF16)
        dec = chunk_dec(c)
        return tuple(dec[:, SSD_WIDTH + g * gw:SSD_WIDTH + (g + 1) * gw] * s_b[g] + kb_scr[c, g]
                     for g in range(SSD_GROUPS))

    lax.fori_loop(0, nch, bwd, tuple(s_b0))


def _ssd(xbc, z, dt, xbcc, dtc, conv_w8, conv_b, dt_bias, a_log, d_skip, norm_w):
    b, L, nconv = xbc.shape
    Lc = xbcc.shape[1]
    nch = L // CHUNK
    per_b = lambda n, w: pl.BlockSpec((1, n, w), lambda i: (i, 0, 0))
    return pl.pallas_call(
        _ssd_kernel,
        grid=(b,),
        in_specs=[
            per_b(L, nconv), per_b(L, SSD_WIDTH), per_b(L, LANES), per_b(Lc, nconv), per_b(Lc, LANES),
            _const_spec(conv_w8.shape), _const_spec(conv_b.shape), _const_spec(dt_bias.shape),
            _const_spec(a_log.shape), _const_spec(d_skip.shape), _const_spec(norm_w.shape),
        ],
        out_specs=per_b(L, SSD_WIDTH),
        out_shape=jax.ShapeDtypeStruct((b, L, SSD_WIDTH), BF16),
        scratch_shapes=[
            pltpu.VMEM((L + 2 * CONV_HALO, nconv), F32),
            pltpu.VMEM((Lc + 2 * CONV_HALO, nconv), F32),
            pltpu.VMEM((L, nconv), BF16),
            pltpu.VMEM((Lc, nconv), BF16),
            pltpu.VMEM((L, LANES), F32), pltpu.VMEM((L, LANES), F32),
            pltpu.VMEM((Lc, LANES), F32), pltpu.VMEM((Lc, LANES), F32),
            pltpu.VMEM((nch, SSD_GROUPS, SSD_STATE, 2 * LANES), F32),
            pltpu.VMEM((nch, SSD_GROUPS, SSD_STATE, 2 * LANES), F32),
            pltpu.VMEM((L, LANES), F32), pltpu.VMEM((L, LANES), F32),
            pltpu.VMEM((nch * SUBLANES, 2 * SSD_WIDTH), F32),
            pltpu.VMEM((nch * 2 * SSD_HEADS, CHUNK), F32), pltpu.VMEM((nch * 2 * SSD_HEADS, CHUNK), F32),
            pltpu.VMEM((nch * 2 * SSD_HEADS, CHUNK), F32),
        ],
        compiler_params=pltpu.CompilerParams(vmem_limit_bytes=VMEM_LIMIT_SSD),
        name="ssd",
    )(xbc, z, dt, xbcc, dtc, conv_w8, conv_b, dt_bias, a_log, d_skip, norm_w)


def _ret_kernel(q_ref, k_ref, v_ref, g_ref, kc_ref, vc_ref, df_ref, db_ref, gn_ref, y_ref, sf_scr):
    L = q_ref.shape[1]
    Lc = kc_ref.shape[1]
    nch = L // CHUNK
    dk = RET_DK
    row_i = lax.broadcasted_iota(jnp.int32, (CHUNK, dk), 0).astype(F32)
    col_i = lax.broadcasted_iota(jnp.int32, (CHUNK, dk), 1).astype(F32)
    rel = row_i - col_i
    crow = lax.broadcasted_iota(jnp.int32, (Lc, dk), 0).astype(F32)

    heads = []
    s_f0 = []
    s_b0 = []
    for h in range(RET_HEADS):
        cols = slice(h * dk, (h + 1) * dk)
        lg_f = -jnp.exp(df_ref[:, cols])
        lg_b = -jnp.exp(db_ref[:, cols])
        heads.append(dict(
            cols=cols,
            dmat=jnp.where(rel >= 0, jnp.exp(jnp.maximum(rel, 0.0) * lg_f), jnp.exp(jnp.maximum(-rel, 0.0) * lg_b)),
            dq_f=jnp.exp((row_i + 1.0) * lg_f),
            dq_b=jnp.exp((CHUNK - row_i) * lg_b),
            dk_f=jnp.exp((CHUNK - 1.0 - row_i) * lg_f),
            dk_b=jnp.exp(row_i * lg_b),
            dc_f=jnp.exp(CHUNK * lg_f),
            dc_b=jnp.exp(CHUNK * lg_b),
        ))
        kc = kc_ref[0, :, cols].astype(F32)
        vc = vc_ref[0, :, cols]
        s_f0.append(_dot_tn((kc * jnp.exp((Lc - 1.0 - crow) * lg_f)).astype(BF16), vc))
        s_b0.append(_dot_tn((kc * jnp.exp(crow * lg_b)).astype(BF16), vc))

    def fwd(c, s_f):
        base = pl.multiple_of(c * CHUNK, CHUNK)
        new = []
        for h, hd in enumerate(heads):
            sf_scr[c, h] = s_f[h]
            kk = k_ref[0, pl.ds(base, CHUNK), hd["cols"]].astype(F32)
            vv = v_ref[0, pl.ds(base, CHUNK), hd["cols"]]
            new.append(hd["dc_f"] * s_f[h] + _dot_tn((kk * hd["dk_f"]).astype(BF16), vv))
        return tuple(new)

    lax.fori_loop(0, nch, fwd, tuple(s_f0), unroll=RET_UNROLL)

    def bwd(i, s_bs):
        c = nch - 1 - i
        base = pl.multiple_of(c * CHUNK, CHUNK)
        new = []
        for h, hd in enumerate(heads):
            qq = q_ref[0, pl.ds(base, CHUNK), hd["cols"]]
            kk = k_ref[0, pl.ds(base, CHUNK), hd["cols"]]
            vv = v_ref[0, pl.ds(base, CHUNK), hd["cols"]]
            s_b = s_bs[h]
            scores = (_dot_nt(qq, kk) * hd["dmat"]).astype(BF16)
            y = (_dot(scores, vv)
                 + _dot(qq, sf_scr[c, h].astype(BF16)) * hd["dq_f"]
                 + _dot(qq, s_b.astype(BF16)) * hd["dq_b"])
            mu = jnp.mean(y, axis=-1, keepdims=True)
            yc = y - mu
            var = jnp.mean(yc * yc, axis=-1, keepdims=True)
            yn = (yc * lax.rsqrt(var + EPS)) * gn_ref[:, hd["cols"]]
            gate = _silu(g_ref[0, pl.ds(base, CHUNK), hd["cols"]].astype(F32))
            y_ref[0, pl.ds(base, CHUNK), hd["cols"]] = (yn * gate).astype(BF16)
            new.append(hd["dc_b"] * s_b + _dot_tn((kk.astype(F32) * hd["dk_b"]).astype(BF16), vv))
        return tuple(new)

    lax.fori_loop(0, nch, bwd, tuple(s_b0), unroll=RET_UNROLL)


def _retention(q, k, v, g, kc, vc, decay_f, decay_b, gn_w):
    b, L, w = q.shape
    Lc = kc.shape[1]
    nch = L // CHUNK
    per_b = lambda n: pl.BlockSpec((1, n, w), lambda i: (i, 0, 0))
    return pl.pallas_call(
        _ret_kernel,
        grid=(b,),
        in_specs=[per_b(L), per_b(L), per_b(L), per_b(L), per_b(Lc), per_b(Lc),
                  _const_spec((1, w)), _const_spec((1, w)), _const_spec((1, w))],
        out_specs=per_b(L),
        out_shape=jax.ShapeDtypeStruct((b, L, w), BF16),
        scratch_shapes=[
            pltpu.VMEM((nch, RET_HEADS, RET_DK, RET_DK), F32),
        ],
        compiler_params=pltpu.CompilerParams(vmem_limit_bytes=VMEM_LIMIT),
        name="retention",
    )(q, k, v, g, kc, vc, decay_f, decay_b, gn_w)


def _outproj_router_kernel(yr_ref, ys_ref, x_ref, g1_ref, sh2_ref, sc2_ref, npost_ref, npre_ref,
                           wor_ref, wos_ref, wr_ref, br_ref, tri_ref,
                           x1_ref, h2_ref, route_ref, slots_ref, seg_ref,
                           wcat):
    i = pl.program_id(0)

    @pl.when(i == 0)
    def _():
        wr = wr_ref[...]
        hi = wr.astype(BF16)
        wcat[:, :LANES] = hi
        wcat[:, LANES:] = (wr - hi.astype(F32)).astype(BF16)

    _route_tile(yr_ref, ys_ref, x_ref, g1_ref, sh2_ref, sc2_ref, npost_ref, npre_ref, wor_ref, wos_ref, br_ref,
                tri_ref, x1_ref, h2_ref, route_ref, slots_ref, seg_ref, wcat)


def _route_tile(yr_ref, ys_ref, x_ref, g1_ref, sh2_ref, sc2_ref, npost_ref, npre_ref, wor_ref, wos_ref, br_ref,
                tri_ref, x1_ref, h2_ref, route_ref, slots_ref, seg_ref, wcat):
    tm = x_ref.shape[0]
    rows = slice(0, tm)
    y = _dot(yr_ref[rows, :], wor_ref[...]) + _dot(ys_ref[rows, :], wos_ref[...])
    ms = jnp.mean(y * y, axis=-1, keepdims=True)
    x1 = x_ref[rows, :] + (y * lax.rsqrt(ms + EPS)) * (g1_ref[0] * npost_ref[...])
    x1_ref[rows, :] = x1
    h2 = _norm_mod(x1, npre_ref[...], sc2_ref[0], sh2_ref[0])
    h2_ref[rows, :] = h2

    h_hi = h2.astype(BF16)
    h_lo = (h2 - h_hi.astype(F32)).astype(BF16)
    both = _dot(h_hi, wcat[...])
    lg = both[:, :LANES] + both[:, LANES:] + _dot(h_lo, wcat[:, :LANES]) + br_ref[...]

    lane = lax.broadcasted_iota(jnp.int32, (tm, LANES), 1)
    lane_f = lane.astype(F32)
    is_grp = (lane >= N_EXPERTS) & (lane < N_EXPERTS + MOE_GROUPS)
    gl = jnp.where(is_grp, lg, NEG_BIG)
    mg = jnp.max(gl, axis=-1, keepdims=True)
    grp_lane = jnp.min(jnp.where(gl == mg, lane_f, 1e9), axis=-1, keepdims=True)
    p_g = 1.0 / jnp.sum(jnp.where(is_grp, jnp.exp(gl - mg), 0.0), axis=-1, keepdims=True)
    first = (grp_lane - N_EXPERTS) * EXPERTS_PER_GROUP
    in_grp = (lane_f >= first) & (lane_f < first + EXPERTS_PER_GROUP)
    el = jnp.where(in_grp, lg, NEG_BIG)
    t1 = jnp.max(el, axis=-1, keepdims=True)
    i1 = jnp.min(jnp.where(el == t1, lane_f, 1e9), axis=-1, keepdims=True)
    el2 = jnp.where(lane_f == i1, NEG_BIG, el)
    t2 = jnp.max(el2, axis=-1, keepdims=True)
    i2 = jnp.min(jnp.where(el2 == t2, lane_f, 1e9), axis=-1, keepdims=True)
    s = jnp.exp(t2 - t1)
    w1 = p_g / (1.0 + s)
    w2 = p_g * s / (1.0 + s)

    oh1 = (lane_f == i1)
    oh2 = (lane_f == i2)
    ohf = jnp.where(oh1 | oh2, 1.0, 0.0)
    before = _dot(tri_ref[...], ohf.astype(BF16))
    cnt = jnp.sum(ohf, axis=0, keepdims=True)
    seg = jnp.floor((cnt + (SUBLANES - 1.0)) * (1.0 / SUBLANES)) * SUBLANES
    e_row = lax.broadcasted_iota(jnp.int32, (LANES, LANES), 0)
    e_col = lax.broadcasted_iota(jnp.int32, (LANES, LANES), 1)
    earlier = (e_row < e_col).astype(BF16)
    seg_off = _dot(jnp.broadcast_to(seg, (SUBLANES, LANES)).astype(BF16), earlier)[0:1]
    where_to = before + seg_off
    lpos1 = jnp.sum(jnp.where(oh1, where_to, 0.0), axis=-1, keepdims=True)
    lpos2 = jnp.sum(jnp.where(oh2, where_to, 0.0), axis=-1, keepdims=True)

    cols = [w1, w2, lpos1, lpos2]
    for wk in (w1, w2):
        hi = wk.astype(BF16).astype(F32)
        cols += [hi, wk - hi]
    cols.append(jnp.ones_like(w1))
    packed = jnp.zeros((tm, LANES), F32)
    for k, col in enumerate(cols):
        packed = jnp.where(lane == k, col, packed)
    route_ref[rows, :] = packed

    row = lax.broadcasted_iota(jnp.int32, (tm, LANES), 0)
    on_diag = (row % LANES) == lane
    per = tm // LANES
    for qi, col in enumerate((lpos1, lpos2)):
        picked = jnp.where(on_diag, col, 0.0)
        dense = jnp.sum(picked.reshape(per, LANES, LANES), axis=1).astype(jnp.int32)
        slots_ref[0, qi * per:(qi + 1) * per, :] = dense
    tbl_row = lax.broadcasted_iota(jnp.int32, (SUBLANES, LANES), 0)
    seg_ref[0] = jnp.where(tbl_row == 0, seg, jnp.where(tbl_row == 1, seg_off, 0.0)).astype(jnp.int32)


def _outproj_router(yr, ys, x2, mod3, npost, npre, wo_r, wo_s, w_router, b_router, seq_len):
    T, d = x2.shape
    tm = TM_OUT
    per_seq = seq_len // tm
    rw = yr.shape[1]
    tri = (jnp.arange(tm)[:, None] > jnp.arange(tm)[None, :]).astype(BF16)
    tok = lambda w: pl.BlockSpec((tm, w), lambda i: (i, 0))
    modv = lambda k: pl.BlockSpec((1, 1, d), lambda i: (i // per_seq, 0, k))
    tile3 = lambda r: pl.BlockSpec((1, r, LANES), lambda i: (i, 0, 0))
    slot_rows = TOP_K * (tm // LANES)
    return pl.pallas_call(
        _outproj_router_kernel,
        grid=(T // tm,),
        in_specs=[
            tok(rw), tok(rw), tok(d), modv(2), modv(3), modv(4),
            _const_spec((1, d)), _const_spec((1, d)),
            _const_spec(wo_r.shape), _const_spec(wo_s.shape), _const_spec(w_router.shape), _const_spec((1, LANES)),
            _const_spec((tm, tm)),
        ],
        out_specs=[tok(d), tok(d), tok(LANES), tile3(slot_rows), tile3(SUBLANES)],
        out_shape=[jax.ShapeDtypeStruct((T, d), F32), jax.ShapeDtypeStruct((T, d), F32),
                   jax.ShapeDtypeStruct((T, LANES), F32),
                   jax.ShapeDtypeStruct((T // tm, slot_rows, LANES), jnp.int32),
                   jax.ShapeDtypeStruct((T // tm, SUBLANES, LANES), jnp.int32)],
        scratch_shapes=[pltpu.VMEM((d, 2 * LANES), BF16)],
        compiler_params=pltpu.CompilerParams(dimension_semantics=("arbitrary",),
                                             vmem_limit_bytes=VMEM_LIMIT),
        name="outproj_router",
    )(yr, ys, x2, mod3, mod3, mod3, npost, npre, wo_r, wo_s, w_router, b_router, tri)


def _expert_kernel(be_ref, first_ref, slot_ref, next_ref, nused_ref, xs_ref, wg_hbm, wu_hbm, wd_hbm, y_ref,
                   wg_f, wu_f, wd_f, wg_b, wu_b, wd_b, sem):
    i = pl.program_id(0)

    def fetch(e, s):
        return [pltpu.make_async_copy(src.at[e], dst.at[s], sem.at[s, k])
                for k, (src, dst) in enumerate(((wg_hbm, wg_f), (wu_hbm, wu_f), (wd_hbm, wd_f)))]

    @pl.when(i == 0)
    def _():
        for c in fetch(be_ref[0], slot_ref[0]):
            c.start()

    @pl.when(first_ref[i] == 1)
    def _():
        s = slot_ref[i]

        @pl.when(next_ref[i] >= 0)
        def _():
            for c in fetch(next_ref[i], 1 - s):
                c.start()

        for c in fetch(be_ref[i], s):
            c.wait()
        wg_b[...] = wg_f[s].astype(BF16)
        wu_b[...] = wu_f[s].astype(BF16)
        wd_b[...] = wd_f[s].astype(BF16)

    @pl.when(i < nused_ref[0])
    def _():
        d = y_ref.shape[1]
        side = xs_ref[:, d:d + LANES]
        second = side[:, 8:9] == 2.0
        unscale = jnp.where(second, 0.5, 1.0)
        weight = jnp.where(second, side[:, 6:7] + side[:, 7:8], side[:, 4:5] + side[:, 5:6]) * unscale
        xb = (xs_ref[:, 0:d] * unscale).astype(BF16)
        hid = (_silu(_dot(xb, wg_b[...])) * _dot(xb, wu_b[...])).astype(BF16)
        y_ref[...] = (_dot(hid, wd_b[...]) * weight).astype(BF16).astype(F32)

    @pl.when(i >= nused_ref[0])
    def _():
        y_ref[...] = jnp.zeros_like(y_ref)


def _expert_plan(padded, pad_end, n_blocks, mb):
    n_used = (pad_end[-1:] // mb).astype(jnp.int32)
    blk_start = jnp.arange(n_blocks, dtype=jnp.int32) * mb
    experts = jnp.arange(N_EXPERTS, dtype=jnp.int32)
    blk_expert = jnp.minimum(jnp.sum((pad_end[None, :] <= blk_start[:, None]).astype(jnp.int32), axis=1),
                             N_EXPERTS - 1)
    prev = jnp.concatenate([jnp.full((1,), -1, jnp.int32), blk_expert[:-1]])
    first = ((blk_start < pad_end[-1]) & (blk_expert != prev)).astype(jnp.int32)
    slot = (jnp.cumsum(first) - 1) % 2
    later = jnp.where((padded > 0)[None, :] & (experts[None, :] > experts[:, None]), experts[None, :], N_EXPERTS)
    next_nonempty = jnp.min(later, axis=1)
    next_nonempty = jnp.where(next_nonempty == N_EXPERTS, -1, next_nonempty)
    nxt = jnp.sum(jnp.where(blk_expert[:, None] == experts[None, :], next_nonempty[None, :], 0), axis=1)
    return [a.astype(jnp.int32) for a in (blk_expert, first, slot, nxt, n_used)]


def _experts(plan, xs, w_gate, w_up, w_down):
    rows, dp = xs.shape
    n_exp, d, de = w_gate.shape
    assert dp == d + LANES
    mb = MB_EXPERT
    cap = rows // mb * mb
    grid_spec = pltpu.PrefetchScalarGridSpec(
        num_scalar_prefetch=len(plan),
        grid=(cap // mb,),
        in_specs=[
            pl.BlockSpec((mb, dp), lambda i, be, fi, sl, nx, nu: (jnp.minimum(i, nu[0] - 1), 0)),
            pl.BlockSpec(memory_space=pl.ANY), pl.BlockSpec(memory_space=pl.ANY), pl.BlockSpec(memory_space=pl.ANY),
        ],
        out_specs=pl.BlockSpec((mb, d), lambda i, be, fi, sl, nx, nu: (i, 0)),
        scratch_shapes=[pltpu.VMEM((2, d, de), F32), pltpu.VMEM((2, d, de), F32), pltpu.VMEM((2, de, d), F32),
                        pltpu.VMEM((d, de), BF16), pltpu.VMEM((d, de), BF16), pltpu.VMEM((de, d), BF16),
                        pltpu.SemaphoreType.DMA((2, 3))],
    )
    return pl.pallas_call(
        _expert_kernel,
        grid_spec=grid_spec,
        out_shape=jax.ShapeDtypeStruct((cap, d), F32),
        compiler_params=pltpu.CompilerParams(dimension_semantics=("arbitrary",),
                                             vmem_limit_bytes=VMEM_LIMIT),
        name="experts",
    )(*plan, xs, w_gate, w_up, w_down)


def _segment_pieces(tile, segrow_ref, seglen_ref, segoff_ref, act):
    def per_expert(e, carry):
        idx = tile * N_EXPERTS + e
        g0 = segrow_ref[idx]
        l0 = segoff_ref[idx]

        def piece(j, c2):
            act(pl.multiple_of(l0 + j * SUBLANES, SUBLANES), pl.multiple_of(g0 + j * SUBLANES, SUBLANES))
            return c2

        lax.fori_loop(0, seglen_ref[idx], piece, 0)
        return carry

    lax.fori_loop(0, N_EXPERTS, per_expert, 0)


def _wait_rows(total, row_copy):
    size = SUBLANES
    while size <= _stage_rows(max(TD_DISPATCH, TF_COMBINE)):
        @pl.when((total & size) != 0)
        def _(size=size):
            row_copy(size).wait()
        size *= 2


def _stage_rows(tile_tokens):
    return TOP_K * tile_tokens + N_EXPERTS * SUBLANES


def _dispatch_seg_kernel(pad_end_ref, zero_from_ref, segrow_ref, seglen_ref, segoff_ref, tilerows_ref,
                         lpos_ref, route_ref, h_ref, xs_hbm, zbuf, stage, sem, zsem):
    i = pl.program_id(0)
    nt = pl.num_programs(0)
    td, d = h_ref.shape
    sr = stage.shape[1]
    per = td // LANES

    def zero_fills(act):
        def fill(row, n):
            act(pltpu.make_async_copy(zbuf.at[pl.ds(0, n)], xs_hbm.at[pl.ds(pl.multiple_of(row, SUBLANES), n)], zsem))

        def region(e, carry):
            row = zero_from_ref[e]
            left = pad_end_ref[e] - row
            size = ZERO_PIECE
            while size >= SUBLANES:
                take = (left & size) != 0

                @pl.when(take)
                def _(row=row, size=size):
                    fill(row, size)

                row = row + jnp.where(take, size, 0)
                size //= 2
            return carry

        def tail(p, carry):
            @pl.when(p * ZERO_PIECE >= pad_end_ref[N_EXPERTS - 1])
            def _():
                fill(p * ZERO_PIECE, ZERO_PIECE)
            return carry

        lax.fori_loop(0, N_EXPERTS, region, 0)
        lax.fori_loop(0, xs_hbm.shape[0] // ZERO_PIECE, tail, 0)

    @pl.when(i == 0)
    def _():
        zbuf[...] = jnp.zeros_like(zbuf)
        zero_fills(lambda c: c.start())

    def shipped(tile):
        _wait_rows(tilerows_ref[tile], lambda n: pltpu.make_async_copy(
            stage.at[tile % 2, pl.ds(0, n)], xs_hbm.at[pl.ds(0, n)], sem.at[tile % 2]))

    @pl.when(i >= 2)
    def _():
        shipped(i - 2)

    srow = lax.broadcasted_iota(jnp.int32, (sr, LANES), 0)
    place = jnp.concatenate(
        [jnp.where(srow == lpos_ref[0, cb:cb + 1, :], 1.0,
                   jnp.where(srow == lpos_ref[0, per + cb:per + cb + 1, :], 2.0, 0.0)) for cb in range(per)],
        axis=1).astype(BF16)
    slot = i % 2
    stage[slot] = _dot(place, jnp.concatenate([h_ref[...].astype(BF16), route_ref[...].astype(BF16)], axis=1))
    _segment_pieces(i, segrow_ref, seglen_ref, segoff_ref, lambda lr, gr: pltpu.make_async_copy(
        stage.at[slot, pl.ds(lr, SUBLANES)], xs_hbm.at[pl.ds(gr, SUBLANES)], sem.at[slot]).start())

    @pl.when(i == nt - 1)
    def _():
        @pl.when(i >= 1)
        def _():
            shipped(i - 1)
        shipped(i)
        zero_fills(lambda c: c.wait())


def _dispatch_seg(plan, lpos, route, h2, cap):
    T, d = h2.shape
    td = TD_DISPATCH
    nt = T // td
    width = d + LANES
    sr = _stage_rows(td)
    n_pre = len(plan)
    grid_spec = pltpu.PrefetchScalarGridSpec(
        num_scalar_prefetch=n_pre,
        grid=(nt,),
        in_specs=[
            pl.BlockSpec((1, lpos.shape[1], LANES), lambda i, *_: (i, 0, 0)),
            pl.BlockSpec((td, LANES), lambda i, *_: (i, 0)),
            pl.BlockSpec((td, d), lambda i, *_: (i, 0)),
        ],
        out_specs=pl.BlockSpec(memory_space=pl.ANY),
        scratch_shapes=[pltpu.VMEM((ZERO_PIECE, width), F32), pltpu.VMEM((2, sr, width), F32),
                        pltpu.SemaphoreType.DMA((2,)), pltpu.SemaphoreType.DMA(())],
    )
    return pl.pallas_call(
        _dispatch_seg_kernel,
        grid_spec=grid_spec,
        out_shape=jax.ShapeDtypeStruct((cap, width), F32),
        compiler_params=pltpu.CompilerParams(dimension_semantics=("arbitrary",), vmem_limit_bytes=VMEM_LIMIT),
        name="dispatch",
    )(*plan, lpos, route, h2)


def _combine_seg_kernel(segrow_ref, seglen_ref, segoff_ref, tilerows_ref, route_ref, x1_ref, g2_ref, nw_ref,
                        yb_hbm, o_ref, stage, sem):
    i = pl.program_id(0)
    nt = pl.num_programs(0)
    tf = x1_ref.shape[0]
    sr = stage.shape[1]

    def fetch(tile, slot):
        _segment_pieces(tile, segrow_ref, seglen_ref, segoff_ref, lambda lr, gr: pltpu.make_async_copy(
            yb_hbm.at[pl.ds(gr, SUBLANES)], stage.at[slot, pl.ds(lr, SUBLANES)], sem.at[slot]).start())

    @pl.when(i == 0)
    def _():
        stage[...] = jnp.zeros_like(stage)
        fetch(i, 0)

    for slot in range(2):
        @pl.when((i + 1 < nt) & (i % 2 != slot))
        def _(slot=slot):
            fetch(i + 1, slot)

    _wait_rows(tilerows_ref[i], lambda n: pltpu.make_async_copy(
        yb_hbm.at[pl.ds(0, n)], stage.at[i % 2, pl.ds(0, n)], sem.at[i % 2]))

    local_row = route_ref[:, 2:2 + TOP_K].astype(jnp.int32)
    scol = lax.broadcasted_iota(jnp.int32, (tf, sr), 1)
    pick = jnp.where((scol == local_row[:, 0:1]) | (scol == local_row[:, 1:2]), 1.0, 0.0).astype(BF16)
    out = _dot(pick, stage[i % 2].astype(BF16))
    ms = jnp.mean(out * out, axis=-1, keepdims=True)
    o_ref[...] = x1_ref[...] + g2_ref[0] * ((out * lax.rsqrt(ms + EPS)) * nw_ref[...])


def _combine_seg(seg_plan, route, x1, mod3, nw, yb, seq_len):
    T, d = x1.shape
    tf = TF_COMBINE
    nt = T // tf
    per_seq = seq_len // tf
    grid_spec = pltpu.PrefetchScalarGridSpec(
        num_scalar_prefetch=len(seg_plan),
        grid=(nt,),
        in_specs=[
            pl.BlockSpec((tf, LANES), lambda i, *_: (i, 0)),
            pl.BlockSpec((tf, d), lambda i, *_: (i, 0)),
            pl.BlockSpec((1, 1, d), lambda i, *_: (i // per_seq, 0, 5)),
            pl.BlockSpec((1, d), lambda i, *_: (0, 0)),
            pl.BlockSpec(memory_space=pl.ANY),
        ],
        out_specs=pl.BlockSpec((tf, d), lambda i, *_: (i, 0)),
        scratch_shapes=[pltpu.VMEM((2, _stage_rows(tf), d), F32), pltpu.SemaphoreType.DMA((2,))],
    )
    return pl.pallas_call(
        _combine_seg_kernel,
        grid_spec=grid_spec,
        out_shape=jax.ShapeDtypeStruct((T, d), F32),
        compiler_params=pltpu.CompilerParams(dimension_semantics=("arbitrary",), vmem_limit_bytes=VMEM_LIMIT),
        name="combine",
    )(*seg_plan, route, x1, mod3, nw, yb)


def _rope_tables(L, n_heads):
    quarter = RET_DK // 4
    freqs = ROPE_BASE ** (-jnp.arange(quarter, dtype=F32) / quarter)
    t = jnp.arange(L)
    ang_r = (t // GRID_W).astype(F32)[:, None] * freqs
    ang_c = (t % GRID_W).astype(F32)[:, None] * freqs
    cos = jnp.concatenate([jnp.cos(ang_r)] * 2 + [jnp.cos(ang_c)] * 2, axis=-1)
    sin = jnp.concatenate([-jnp.sin(ang_r), jnp.sin(ang_r), -jnp.sin(ang_c), jnp.sin(ang_c)], axis=-1)
    return jnp.tile(cos, (1, n_heads)), jnp.tile(sin, (1, n_heads))


def _lane_pad(v, width=LANES):
    return jnp.pad(v, [(0, 0)] * (v.ndim - 1) + [(0, width - v.shape[-1])])


def kernel(x, c, ctx, c_ctx, w_mod, b_mod, norm_pre_mix, norm_post_mix, norm_pre_ffn, norm_post_ffn, w_in, w_out, ret_decay_f, ret_decay_b, ret_gn_w, ssd_conv_w, ssd_conv_b, ssd_dt_bias_f, ssd_dt_bias_b, ssd_a_log_f, ssd_a_log_b, ssd_d, ssd_norm_w, moe_w_rg, moe_b_rg, moe_w_re, moe_b_re, moe_w_gate, moe_w_up, moe_w_down):
    b, L, d = x.shape
    assert w_mod.shape[0] == 1, "single layer: context outputs are never needed"
    assert TM_OUT == TD_DISPATCH == TF_COMBINE, "router, dispatch and combine share one slot-row layout"
    rw = RET_HEADS * RET_DK
    nconv = SSD_WIDTH + 2 * SSD_GROUPS * SSD_STATE
    T = b * L

    mod_rows = -(-(b + 1) // SUBLANES) * SUBLANES
    c_all = jnp.zeros((mod_rows, d), F32).at[:b].set(c).at[b].set(c_ctx)
    mod3 = _modulation(c_all, w_mod[0], b_mod[0]).reshape(mod_rows, 1, 6 * d)

    wi = w_in[0]
    o = 0
    wq = wi[:, o:o + rw]; o += rw
    wk = wi[:, o:o + rw]; o += rw
    wv = wi[:, o:o + rw]; o += rw
    wg = wi[:, o:o + rw]; o += rw
    wz = wi[:, o:o + SSD_WIDTH]; o += SSD_WIDTH
    wxbc = wi[:, o:o + nconv].astype(BF16); o += nconv
    wdt = _lane_pad(wi[:, o:o + 2 * SSD_HEADS]).astype(BF16)
    wqk = jnp.concatenate([wq, wk], axis=1).astype(BF16)
    wvgz = jnp.concatenate([wv, wg, wz], axis=1).astype(BF16)
    cos_t, sin_t = _rope_tables(L, RET_HEADS)
    nw1 = norm_pre_mix[0].reshape(1, d)

    q, k, v, g, z, xbc, dt = _inproj(x, mod3, nw1, wqk, wvgz, wxbc, wdt, cos_t, sin_t)
    kc, vc, xbcc, dtc = _inproj_ctx(ctx, mod3, b, nw1, wk.astype(BF16), wv.astype(BF16), wxbc, wdt)

    conv_w8 = jnp.pad(ssd_conv_w[0], ((0, SUBLANES - SSD_CONV), (0, 0)))
    dt_bias = _lane_pad(jnp.concatenate([ssd_dt_bias_f[0], ssd_dt_bias_b[0]])[None, :])
    a_log = _lane_pad(jnp.concatenate([ssd_a_log_f[0], ssd_a_log_b[0]])[None, :])
    d_skip = jnp.repeat(ssd_d[0], SSD_HEADDIM)[None, :]
    ys = _ssd(xbc, z, dt, xbcc, dtc, conv_w8, ssd_conv_b[0][None, :], dt_bias, a_log, d_skip,
              ssd_norm_w[0][None, :])

    yr = _retention(q, k, v, g, kc, vc,
                    jnp.repeat(ret_decay_f[0], RET_DK)[None, :], jnp.repeat(ret_decay_b[0], RET_DK)[None, :],
                    ret_gn_w[0][None, :])

    wo = w_out[0].astype(BF16)
    w_router = _lane_pad(jnp.concatenate(
        [jnp.transpose(moe_w_re[0], (1, 0, 2)).reshape(d, N_EXPERTS), moe_w_rg[0]], axis=1))
    b_router = _lane_pad(jnp.concatenate([moe_b_re[0].reshape(-1), moe_b_rg[0]])[None, :])
    x1, h2, route, lpos, seg = _outproj_router(
        yr.reshape(T, rw), ys.reshape(T, SSD_WIDTH), x.reshape(T, d), mod3,
        norm_post_mix[0][None, :], norm_pre_ffn[0][None, :], wo[:rw], wo[rw:], w_router, b_router, L)

    mb = MB_EXPERT
    nt = T // TM_OUT
    n_blocks = -(-(T * TOP_K + nt * N_EXPERTS * (SUBLANES - 1) + N_EXPERTS * (mb - 1)) // mb)
    seg_len = seg[:, 0, :N_EXPERTS]
    seg_off = seg[:, 1, :N_EXPERTS]
    used = jnp.sum(seg_len, axis=0)
    padded = (used + mb - 1) // mb * mb
    pad_end = jnp.cumsum(padded)
    pad_start = pad_end - padded
    seg_row = pad_start[None, :] + jnp.cumsum(seg_len, axis=0) - seg_len
    seg_plan = [a.reshape(-1).astype(jnp.int32)
                for a in (seg_row, seg_len // SUBLANES, seg_off, jnp.sum(seg_len, axis=1))]
    zero_from = (pad_start + used).astype(jnp.int32)

    xs = _dispatch_seg([pad_end.astype(jnp.int32), zero_from] + seg_plan, lpos, route, h2, n_blocks * mb)
    yb = _experts(_expert_plan(padded, pad_end, n_blocks, mb), xs, moe_w_gate[0], moe_w_up[0], moe_w_down[0])
    out = _combine_seg(seg_plan, route, x1, mod3, norm_post_ffn[0][None, :], yb, L)
    return out.reshape(b, L, d)
```

```python
import jax
import jax.numpy as jnp
from jax import lax
from jax.experimental import pallas as pl
from jax.experimental.pallas import tpu as pltpu

F32 = jnp.float32
BF16 = jnp.bfloat16

LANES = 128
SUBLANES = 8
BF16_TILE_ROWS = 16
V7X_VMEM_BYTES = 64 * 1024 * 1024
VMEM_LIMIT = V7X_VMEM_BYTES * 3 // 4
VMEM_LIMIT_SSD = V7X_VMEM_BYTES * 7 // 8

EPS = 1e-6
CHUNK = 128
GRID_W = 64
RET_HEADS = 4
RET_DK = 128
ROPE_BASE = 10000.0
SSD_HEADS = 8
SSD_HEADDIM = 64
SSD_GROUPS = 2
SSD_STATE = 128
SSD_WIDTH = SSD_HEADS * SSD_HEADDIM
SSD_CONV = 5
SSD_PAIRS = SSD_WIDTH // LANES
MOE_GROUPS = 4
EXPERTS_PER_GROUP = 8
N_EXPERTS = MOE_GROUPS * EXPERTS_PER_GROUP
TOP_K = 2
CONV_HALO = SUBLANES

TM_PROJ = 512
TM_OUT = 512
TD_DISPATCH = TM_OUT
MB_EXPERT = 512
ZERO_PIECE = MB_EXPERT // 2
TF_COMBINE = TM_OUT
SEG_ALIGN = BF16_TILE_ROWS
RET_UNROLL = 2
NEG_BIG = -1e30


def _silu(v):
    return v * jax.nn.sigmoid(v)


def _dot(a, b):
    return jnp.dot(a, b, preferred_element_type=F32)


def _dot_tn(a, b):
    return lax.dot_general(a, b, (((0,), (0,)), ((), ())), preferred_element_type=F32)


def _dot_nt(a, b):
    return lax.dot_general(a, b, (((1,), (1,)), ((), ())), preferred_element_type=F32)


def _mod_kernel(c_ref, w_ref, b_ref, o_ref):
    a = _silu(c_ref[...])
    w = w_ref[...]
    a_hi = a.astype(BF16)
    a_lo = (a - a_hi.astype(F32)).astype(BF16)
    w_hi = w.astype(BF16)
    w_lo = (w - w_hi.astype(F32)).astype(BF16)
    o_ref[...] = _dot(a_hi, w_hi) + _dot(a_lo, w_hi) + _dot(a_hi, w_lo) + b_ref[...]


def _modulation(c_all, w_mod, b_mod):
    rows, d = c_all.shape
    n = w_mod.shape[1]
    return pl.pallas_call(
        _mod_kernel,
        grid=(n // d,),
        in_specs=[
            pl.BlockSpec((rows, d), lambda j: (0, 0)),
            pl.BlockSpec((d, d), lambda j: (0, j)),
            pl.BlockSpec((1, d), lambda j: (0, j)),
        ],
        out_specs=pl.BlockSpec((rows, d), lambda j: (0, j)),
        out_shape=jax.ShapeDtypeStruct((rows, n), F32),
        name="modulation",
    )(c_all, w_mod, b_mod.reshape(1, n))


def _norm_mod(x, nw, sc, sh):
    ms = jnp.mean(x * x, axis=-1, keepdims=True)
    return (x * lax.rsqrt(ms + EPS)) * (nw * (1.0 + sc)) + sh


def _rope(t, cos, sin_signed, first_half):
    width = t.shape[-1]
    quarter = RET_DK // 4
    swapped = jnp.where(first_half, pltpu.roll(t, width - quarter, 1), pltpu.roll(t, quarter, 1))
    return t * cos + swapped * sin_signed


def _inproj_kernel(x_ref, sh_ref, sc_ref, nw_ref, wqk_ref, wvgz_ref, wxbc_ref, wdt_ref, cos_ref, sin_ref,
                   q_ref, k_ref, v_ref, g_ref, z_ref, xbc_ref, dt_ref):
    hb = _norm_mod(x_ref[0], nw_ref[...], sc_ref[0], sh_ref[0]).astype(BF16)
    rw = q_ref.shape[-1]
    qk = _dot(hb, wqk_ref[...])
    cos = cos_ref[...]
    sin = sin_ref[...]
    lane = lax.broadcasted_iota(jnp.int32, cos.shape, 1)
    first_half = (lane % (RET_DK // 2)) < (RET_DK // 4)
    q_ref[0] = _rope(qk[:, :rw], cos, sin, first_half).astype(BF16)
    k_ref[0] = (_rope(qk[:, rw:], cos, sin, first_half) * (RET_DK ** -0.5)).astype(BF16)
    vgz = _dot(hb, wvgz_ref[...])
    v_ref[0] = vgz[:, :rw].astype(BF16)
    g_ref[0] = vgz[:, rw:2 * rw].astype(BF16)
    z_ref[0] = vgz[:, 2 * rw:].astype(BF16)
    xbc_ref[0] = _dot(hb, wxbc_ref[...]).astype(BF16)
    dt_ref[0] = _dot(hb, wdt_ref[...])


def _inproj_ctx_kernel(x_ref, sh_ref, sc_ref, nw_ref, wk_ref, wv_ref, wxbc_ref, wdt_ref,
                       k_ref, v_ref, xbc_ref, dt_ref):
    hb = _norm_mod(x_ref[0], nw_ref[...], sc_ref[0], sh_ref[0]).astype(BF16)
    k_ref[0] = (_dot(hb, wk_ref[...]) * (RET_DK ** -0.5)).astype(BF16)
    v_ref[0] = _dot(hb, wv_ref[...]).astype(BF16)
    xbc_ref[0] = _dot(hb, wxbc_ref[...]).astype(BF16)
    dt_ref[0] = _dot(hb, wdt_ref[...])


def _const_spec(shape):
    nd = len(shape)
    return pl.BlockSpec(shape, lambda *_: (0,) * nd)


def _inproj(x, mod3, nw, wqk, wvgz, wxbc, wdt, cos_t, sin_t):
    b, L, d = x.shape
    tm = min(TM_PROJ, L)
    rw = wqk.shape[1] // 2
    tok = lambda w: pl.BlockSpec((1, tm, w), lambda i, j: (i, j, 0))
    out_bf = lambda w: jax.ShapeDtypeStruct((b, L, w), BF16)
    return pl.pallas_call(
        _inproj_kernel,
        grid=(b, L // tm),
        in_specs=[
            tok(d),
            pl.BlockSpec((1, 1, d), lambda i, j: (i, 0, 0)),
            pl.BlockSpec((1, 1, d), lambda i, j: (i, 0, 1)),
            _const_spec((1, d)),
            _const_spec(wqk.shape), _const_spec(wvgz.shape), _const_spec(wxbc.shape), _const_spec(wdt.shape),
            pl.BlockSpec((tm, rw), lambda i, j: (j, 0)),
            pl.BlockSpec((tm, rw), lambda i, j: (j, 0)),
        ],
        out_specs=[tok(rw), tok(rw), tok(rw), tok(rw), tok(rw), tok(wxbc.shape[1]), tok(LANES)],
        out_shape=[out_bf(rw), out_bf(rw), out_bf(rw), out_bf(rw), out_bf(rw), out_bf(wxbc.shape[1]),
                   jax.ShapeDtypeStruct((b, L, LANES), F32)],
        compiler_params=pltpu.CompilerParams(vmem_limit_bytes=VMEM_LIMIT),
        name="inproj",
    )(x, mod3, mod3, nw, wqk, wvgz, wxbc, wdt, cos_t, sin_t)


def _inproj_ctx(ctx, mod3, ctx_row, nw, wk, wv, wxbc, wdt):
    b, L, d = ctx.shape
    tm = min(TM_PROJ, L)
    rw = wk.shape[1]
    tok = lambda w: pl.BlockSpec((1, tm, w), lambda i, j: (i, j, 0))
    out_bf = lambda w: jax.ShapeDtypeStruct((b, L, w), BF16)
    return pl.pallas_call(
        _inproj_ctx_kernel,
        grid=(b, L // tm),
        in_specs=[
            tok(d),
            pl.BlockSpec((1, 1, d), lambda i, j: (ctx_row, 0, 0)),
            pl.BlockSpec((1, 1, d), lambda i, j: (ctx_row, 0, 1)),
            _const_spec((1, d)),
            _const_spec(wk.shape), _const_spec(wv.shape), _const_spec(wxbc.shape), _const_spec(wdt.shape),
        ],
        out_specs=[tok(rw), tok(rw), tok(wxbc.shape[1]), tok(LANES)],
        out_shape=[out_bf(rw), out_bf(rw), out_bf(wxbc.shape[1]), jax.ShapeDtypeStruct((b, L, LANES), F32)],
        compiler_params=pltpu.CompilerParams(vmem_limit_bytes=VMEM_LIMIT),
        name="inproj_ctx",
    )(ctx, mod3, mod3, nw, wk, wv, wxbc, wdt)


def _ssd_kernel(xbc_ref, z_ref, dt_ref, xbcc_ref, dtc_ref, cw_ref, cb_ref, dtb_ref, alog_ref, dsk_ref, nw_ref,
                y_ref,
                xpad, xpadc, u, uc, dtv, dav, dtcv, dacv, sf_scr, kb_scr, acum, ecum, dec_scr,
                arow_scr, erow_scr, dtrow_scr):
    L = xbc_ref.shape[1]
    Lc = xbcc_ref.shape[1]
    nch = L // CHUNK
    nchc = Lc // CHUNK
    win = CHUNK + 2 * CONV_HALO
    nconv = xbc_ref.shape[2]
    nh = SSD_HEADS

    def conv_pass(src_ref, pad_ref, dst_ref, n_chunks, length):
        zeros = jnp.zeros((CONV_HALO, nconv), F32)
        pad_ref[0:CONV_HALO, :] = zeros
        pad_ref[CONV_HALO + length:2 * CONV_HALO + length, :] = zeros
        pad_ref[CONV_HALO:CONV_HALO + length, :] = src_ref[0].astype(F32)

        def chunk(c, carry):
            base = pl.multiple_of(c * CHUNK, CHUNK)
            for cb_i in range(nconv // LANES):
                cols = slice(cb_i * LANES, (cb_i + 1) * LANES)
                w = pad_ref[pl.ds(base, win), cols]
                acc = cb_ref[:, cols] + w[CONV_HALO:CONV_HALO + CHUNK] * cw_ref[SSD_CONV // 2:SSD_CONV // 2 + 1, cols]
                for j in range(SSD_CONV):
                    if j == SSD_CONV // 2:
                        continue
                    shifted = pltpu.roll(w, (SSD_CONV // 2 - j) % win, 0)
                    acc = acc + shifted[CONV_HALO:CONV_HALO + CHUNK] * cw_ref[j:j + 1, cols]
                dst_ref[pl.ds(base, CHUNK), cols] = _silu(acc).astype(BF16)
            return carry

        lax.fori_loop(0, n_chunks, chunk, 0)

    conv_pass(xbcc_ref, xpadc, uc, nchc, Lc)
    conv_pass(xbc_ref, xpad, u, nch, L)

    a_neg = -jnp.exp(alog_ref[...])
    dtv[...] = jax.nn.softplus(dt_ref[0] + dtb_ref[...])
    dav[...] = dtv[...] * a_neg
    dtcv[...] = jax.nn.softplus(dtc_ref[0] + dtb_ref[...])
    dacv[...] = dtcv[...] * a_neg

    row_i = lax.broadcasted_iota(jnp.int32, (CHUNK, CHUNK), 0)
    col_i = lax.broadcasted_iota(jnp.int32, (CHUNK, CHUNK), 1)
    causal = col_i <= row_i
    lo_half = col_i < SSD_HEADDIM
    fwd_lane = col_i < nh
    head_of = lax.broadcasted_iota(jnp.int32, (CHUNK, SSD_WIDTH), 1) // SSD_HEADDIM
    src_col = lax.broadcasted_iota(jnp.int32, (CHUNK, SSD_WIDTH), 0)
    exp_f = (head_of == src_col).astype(BF16)
    exp_b = (head_of == src_col - nh).astype(BF16)
    exp_fb = jnp.concatenate([exp_f, exp_b], axis=1)

    def split3(v):
        hi = v.astype(BF16)
        r1 = v - hi.astype(F32)
        mid = r1.astype(BF16)
        return hi, mid, (r1 - mid.astype(F32)).astype(BF16)

    def times_onehot(v, m, passes=3):
        parts = split3(v)[:passes]
        acc = _dot(parts[0], m)
        for part in parts[1:]:
            acc = acc + _dot(part, m)
        return acc

    def colb(mat, r):
        return jnp.broadcast_to(mat[:, r:r + 1], (CHUNK, CHUNK))

    def pair_sel(a, b_):
        return jnp.where(lo_half, a, b_)

    gw = 2 * LANES

    def chunk_terms(u_ref, dt_s, da_s, base):
        dt = dt_s[pl.ds(base, CHUNK), :]
        da = da_s[pl.ds(base, CHUNK), :]
        acol = da
        for step in (1, 2, 4, 8, 16, 32, 64):
            acol = acol + jnp.where(row_i >= step, pltpu.roll(acol, step, 0), 0.0)
        ecol = acol - da
        last = acol[CHUNK - 1:CHUNK, :]
        wgt = jnp.where(fwd_lane, jnp.exp(last - acol), jnp.exp(ecol)) * dt
        scale = jnp.where(fwd_lane, jnp.exp(acol), jnp.exp(last - ecol))
        wide = times_onehot(jnp.concatenate([wgt, scale], axis=0), exp_fb, passes=1)
        dec = times_onehot(jnp.broadcast_to(jnp.exp(last), (SUBLANES, LANES)), exp_fb)[0:1]
        xs = u_ref[pl.ds(base, CHUNK), 0:SSD_WIDTH].astype(F32)
        kmats = []
        for g in range(SSD_GROUPS):
            xw = jnp.concatenate([xs[:, g * gw:(g + 1) * gw] * wide[:CHUNK, g * gw:(g + 1) * gw],
                                  xs[:, g * gw:(g + 1) * gw] * wide[:CHUNK, SSD_WIDTH + g * gw:SSD_WIDTH + (g + 1) * gw]],
                                 axis=1).astype(BF16)
            bm = u_ref[pl.ds(base, CHUNK), SSD_WIDTH + g * SSD_STATE:SSD_WIDTH + (g + 1) * SSD_STATE]
            kmats.append(_dot_tn(bm, xw))
        return dt, acol, ecol, wide[CHUNK:], dec, kmats

    def advance(s, dec, kmats, backward):
        off = SSD_WIDTH if backward else 0
        koff = gw if backward else 0
        return [dec[:, off + g * gw:off + (g + 1) * gw] * s[g] + kmats[g][:, koff:koff + gw]
                for g in range(SSD_GROUPS)]

    ctx_terms = [chunk_terms(uc, dtcv, dacv, c * CHUNK) for c in range(nchc)]
    s_f0 = [jnp.zeros((SSD_STATE, gw), F32) for _ in range(SSD_GROUPS)]
    for c in range(nchc):
        s_f0 = advance(s_f0, ctx_terms[c][4], ctx_terms[c][5], False)
    s_b0 = [jnp.zeros((SSD_STATE, gw), F32) for _ in range(SSD_GROUPS)]
    for c in reversed(range(nchc)):
        s_b0 = advance(s_b0, ctx_terms[c][4], ctx_terms[c][5], True)

    def prep(c, carry):
        base = pl.multiple_of(c * CHUNK, CHUNK)
        dt, acol, ecol, scale, dec, kmats = chunk_terms(u, dtv, dav, base)
        acum[pl.ds(base, CHUNK), :] = acol
        ecum[pl.ds(base, CHUNK), :] = ecol
        hrow = pl.ds(pl.multiple_of(c * 2 * nh, 2 * nh), 2 * nh)
        arow_scr[hrow, :] = acol.T[:2 * nh]
        erow_scr[hrow, :] = ecol.T[:2 * nh]
        dtrow_scr[hrow, :] = dt.T[:2 * nh]
        xpad[pl.ds(base, CHUNK), :] = scale
        dec_scr[pl.ds(pl.multiple_of(c * SUBLANES, SUBLANES), SUBLANES), :] = jnp.broadcast_to(dec, (SUBLANES, 2 * SSD_WIDTH))
        for g in range(SSD_GROUPS):
            sf_scr[c, g] = kmats[g][:, :gw]
            kb_scr[c, g] = kmats[g][:, gw:]
        return carry

    lax.fori_loop(0, nch, prep, 0)

    def chunk_dec(c):
        return dec_scr[pl.ds(pl.multiple_of(c * SUBLANES, SUBLANES), 1), :]

    def fwd(c, s_old):
        dec = chunk_dec(c)
        new = []
        for g in range(SSD_GROUPS):
            new.append(dec[:, g * gw:(g + 1) * gw] * s_old[g] + sf_scr[c, g])
            sf_scr[c, g] = s_old[g]
        return tuple(new)

    lax.fori_loop(0, nch, fwd, tuple(s_f0))

    def bwd(i, s_b):
        c = nch - 1 - i
        base = pl.multiple_of(c * CHUNK, CHUNK)
        acol = acum[pl.ds(base, CHUNK), :]
        ecol = ecum[pl.ds(base, CHUNK), :]
        hrow = pl.ds(pl.multiple_of(c * 2 * nh, 2 * nh), 2 * nh)
        arow = arow_scr[hrow, :]
        erow = erow_scr[hrow, :]
        dt_t = dtrow_scr[hrow, :]
        scale = xpad[pl.ds(base, CHUNK), :]
        ys = []
        for g in range(SSD_GROUPS):
            bm = u[pl.ds(base, CHUNK), SSD_WIDTH + g * SSD_STATE:SSD_WIDTH + (g + 1) * SSD_STATE]
            cm = u[pl.ds(base, CHUNK), SSD_WIDTH + (SSD_GROUPS + g) * SSD_STATE:SSD_WIDTH + (SSD_GROUPS + g + 1) * SSD_STATE]
            cbm = _dot_nt(cm, bm)
            cs_f = _dot(cm, sf_scr[c, g].astype(BF16))
            cs_b = _dot(cm, s_b[g].astype(BF16))
            for pp in range(SSD_PAIRS // SSD_GROUPS):
                p = g * (SSD_PAIRS // SSD_GROUPS) + pp
                xs_b = u[pl.ds(base, CHUNK), p * LANES:(p + 1) * LANES]
                y_h = []
                for hh in range(2):
                    r = 2 * p + hh
                    arg = jnp.where(causal, colb(acol, r) - arow[r:r + 1, :],
                                    erow[nh + r:nh + r + 1, :] - colb(ecol, nh + r))
                    coef = jnp.where(causal, dt_t[r:r + 1, :], dt_t[nh + r:nh + r + 1, :])
                    gm = (cbm * (jnp.exp(arg) * coef)).astype(BF16)
                    y_h.append(_dot(gm, xs_b))
                sl = slice(pp * LANES, (pp + 1) * LANES)
                wl = slice(p * LANES, (p + 1) * LANES)
                wlb = slice(SSD_WIDTH + p * LANES, SSD_WIDTH + (p + 1) * LANES)
                ys.append(pair_sel(y_h[0], y_h[1]) + cs_f[:, sl] * scale[:, wl] + cs_b[:, sl] * scale[:, wlb]
                          + dsk_ref[:, wl] * xs_b.astype(F32))
        y = jnp.concatenate(ys, axis=1)
        y = y * _silu(z_ref[0, pl.ds(base, CHUNK), :].astype(F32))
        ms = jnp.mean(y * y, axis=-1, keepdims=True)
        y_ref[0, pl.ds(base, CHUNK), :] = ((y * lax.rsqrt(ms + EPS)) * nw_ref[...]).astype(BF16)
        dec = chunk_dec(c)
        return tuple(dec[:, SSD_WIDTH + g * gw:SSD_WIDTH + (g + 1) * gw] * s_b[g] + kb_scr[c, g]
                     for g in range(SSD_GROUPS))

    lax.fori_loop(0, nch, bwd, tuple(s_b0))


def _ssd(xbc, z, dt, xbcc, dtc, conv_w8, conv_b, dt_bias, a_log, d_skip, norm_w):
    b, L, nconv = xbc.shape
    Lc = xbcc.shape[1]
    nch = L // CHUNK
    per_b = lambda n, w: pl.BlockSpec((1, n, w), lambda i: (i, 0, 0))
    return pl.pallas_call(
        _ssd_kernel,
        grid=(b,),
        in_specs=[
            per_b(L, nconv), per_b(L, SSD_WIDTH), per_b(L, LANES), per_b(Lc, nconv), per_b(Lc, LANES),
            _const_spec(conv_w8.shape), _const_spec(conv_b.shape), _const_spec(dt_bias.shape),
            _const_spec(a_log.shape), _const_spec(d_skip.shape), _const_spec(norm_w.shape),
        ],
        out_specs=per_b(L, SSD_WIDTH),
        out_shape=jax.ShapeDtypeStruct((b, L, SSD_WIDTH), BF16),
        scratch_shapes=[
            pltpu.VMEM((L + 2 * CONV_HALO, nconv), F32),
            pltpu.VMEM((Lc + 2 * CONV_HALO, nconv), F32),
            pltpu.VMEM((L, nconv), BF16),
            pltpu.VMEM((Lc, nconv), BF16),
            pltpu.VMEM((L, LANES), F32), pltpu.VMEM((L, LANES), F32),
            pltpu.VMEM((Lc, LANES), F32), pltpu.VMEM((Lc, LANES), F32),
            pltpu.VMEM((nch, SSD_GROUPS, SSD_STATE, 2 * LANES), F32),
            pltpu.VMEM((nch, SSD_GROUPS, SSD_STATE, 2 * LANES), F32),
            pltpu.VMEM((L, LANES), F32), pltpu.VMEM((L, LANES), F32),
            pltpu.VMEM((nch * SUBLANES, 2 * SSD_WIDTH), F32),
            pltpu.VMEM((nch * 2 * SSD_HEADS, CHUNK), F32), pltpu.VMEM((nch * 2 * SSD_HEADS, CHUNK), F32),
            pltpu.VMEM((nch * 2 * SSD_HEADS, CHUNK), F32),
        ],
        compiler_params=pltpu.CompilerParams(vmem_limit_bytes=VMEM_LIMIT_SSD),
        name="ssd",
    )(xbc, z, dt, xbcc, dtc, conv_w8, conv_b, dt_bias, a_log, d_skip, norm_w)


def _ret_kernel(q_ref, k_ref, v_ref, g_ref, kc_ref, vc_ref, df_ref, db_ref, gn_ref, y_ref, sf_scr):
    L = q_ref.shape[1]
    Lc = kc_ref.shape[1]
    nch = L // CHUNK
    dk = RET_DK
    row_i = lax.broadcasted_iota(jnp.int32, (CHUNK, dk), 0).astype(F32)
    col_i = lax.broadcasted_iota(jnp.int32, (CHUNK, dk), 1).astype(F32)
    rel = row_i - col_i
    crow = lax.broadcasted_iota(jnp.int32, (Lc, dk), 0).astype(F32)

    heads = []
    s_f0 = []
    s_b0 = []
    for h in range(RET_HEADS):
        cols = slice(h * dk, (h + 1) * dk)
        lg_f = -jnp.exp(df_ref[:, cols])
        lg_b = -jnp.exp(db_ref[:, cols])
        heads.append(dict(
            cols=cols,
            dmat=jnp.where(rel >= 0, jnp.exp(jnp.maximum(rel, 0.0) * lg_f), jnp.exp(jnp.maximum(-rel, 0.0) * lg_b)),
            dq_f=jnp.exp((row_i + 1.0) * lg_f),
            dq_b=jnp.exp((CHUNK - row_i) * lg_b),
            dk_f=jnp.exp((CHUNK - 1.0 - row_i) * lg_f),
            dk_b=jnp.exp(row_i * lg_b),
            dc_f=jnp.exp(CHUNK * lg_f),
            dc_b=jnp.exp(CHUNK * lg_b),
        ))
        kc = kc_ref[0, :, cols].astype(F32)
        vc = vc_ref[0, :, cols]
        s_f0.append(_dot_tn((kc * jnp.exp((Lc - 1.0 - crow) * lg_f)).astype(BF16), vc))
        s_b0.append(_dot_tn((kc * jnp.exp(crow * lg_b)).astype(BF16), vc))

    def fwd(c, s_f):
        base = pl.multiple_of(c * CHUNK, CHUNK)
        new = []
        for h, hd in enumerate(heads):
            sf_scr[c, h] = s_f[h]
            kk = k_ref[0, pl.ds(base, CHUNK), hd["cols"]].astype(F32)
            vv = v_ref[0, pl.ds(base, CHUNK), hd["cols"]]
            new.append(hd["dc_f"] * s_f[h] + _dot_tn((kk * hd["dk_f"]).astype(BF16), vv))
        return tuple(new)

    lax.fori_loop(0, nch, fwd, tuple(s_f0), unroll=RET_UNROLL)

    def bwd(i, s_bs):
        c = nch - 1 - i
        base = pl.multiple_of(c * CHUNK, CHUNK)
        new = []
        for h, hd in enumerate(heads):
            qq = q_ref[0, pl.ds(base, CHUNK), hd["cols"]]
            kk = k_ref[0, pl.ds(base, CHUNK), hd["cols"]]
            vv = v_ref[0, pl.ds(base, CHUNK), hd["cols"]]
            s_b = s_bs[h]
            scores = (_dot_nt(qq, kk) * hd["dmat"]).astype(BF16)
            y = (_dot(scores, vv)
                 + _dot(qq, sf_scr[c, h].astype(BF16)) * hd["dq_f"]
                 + _dot(qq, s_b.astype(BF16)) * hd["dq_b"])
            mu = jnp.mean(y, axis=-1, keepdims=True)
            yc = y - mu
            var = jnp.mean(yc * yc, axis=-1, keepdims=True)
            yn = (yc * lax.rsqrt(var + EPS)) * gn_ref[:, hd["cols"]]
            gate = _silu(g_ref[0, pl.ds(base, CHUNK), hd["cols"]].astype(F32))
            y_ref[0, pl.ds(base, CHUNK), hd["cols"]] = (yn * gate).astype(BF16)
            new.append(hd["dc_b"] * s_b + _dot_tn((kk.astype(F32) * hd["dk_b"]).astype(BF16), vv))
        return tuple(new)

    lax.fori_loop(0, nch, bwd, tuple(s_b0), unroll=RET_UNROLL)


def _retention(q, k, v, g, kc, vc, decay_f, decay_b, gn_w):
    b, L, w = q.shape
    Lc = kc.shape[1]
    nch = L // CHUNK
    per_b = lambda n: pl.BlockSpec((1, n, w), lambda i: (i, 0, 0))
    return pl.pallas_call(
        _ret_kernel,
        grid=(b,),
        in_specs=[per_b(L), per_b(L), per_b(L), per_b(L), per_b(Lc), per_b(Lc),
                  _const_spec((1, w)), _const_spec((1, w)), _const_spec((1, w))],
        out_specs=per_b(L),
        out_shape=jax.ShapeDtypeStruct((b, L, w), BF16),
        scratch_shapes=[
            pltpu.VMEM((nch, RET_HEADS, RET_DK, RET_DK), F32),
        ],
        compiler_params=pltpu.CompilerParams(vmem_limit_bytes=VMEM_LIMIT),
        name="retention",
    )(q, k, v, g, kc, vc, decay_f, decay_b, gn_w)


def _outproj_router_kernel(yr_ref, ys_ref, x_ref, g1_ref, sh2_ref, sc2_ref, npost_ref, npre_ref,
                           wor_ref, wos_ref, wr_ref, br_ref, tri_ref,
                           x1_ref, h2_ref, route_ref, slots_ref, seg_ref,
                           wcat):
    i = pl.program_id(0)

    @pl.when(i == 0)
    def _():
        wr = wr_ref[...]
        hi = wr.astype(BF16)
        wcat[:, :LANES] = hi
        wcat[:, LANES:] = (wr - hi.astype(F32)).astype(BF16)

    _route_tile(yr_ref, ys_ref, x_ref, g1_ref, sh2_ref, sc2_ref, npost_ref, npre_ref, wor_ref, wos_ref, br_ref,
                tri_ref, x1_ref, h2_ref, route_ref, slots_ref, seg_ref, wcat)


def _route_tile(yr_ref, ys_ref, x_ref, g1_ref, sh2_ref, sc2_ref, npost_ref, npre_ref, wor_ref, wos_ref, br_ref,
                tri_ref, x1_ref, h2_ref, route_ref, slots_ref, seg_ref, wcat):
    tm = x_ref.shape[0]
    rows = slice(0, tm)
    y = _dot(yr_ref[rows, :], wor_ref[...]) + _dot(ys_ref[rows, :], wos_ref[...])
    ms = jnp.mean(y * y, axis=-1, keepdims=True)
    x1 = x_ref[rows, :] + (y * lax.rsqrt(ms + EPS)) * (g1_ref[0] * npost_ref[...])
    x1_ref[rows, :] = x1
    h2 = _norm_mod(x1, npre_ref[...], sc2_ref[0], sh2_ref[0])
    h2_ref[rows, :] = h2

    h_hi = h2.astype(BF16)
    h_lo = (h2 - h_hi.astype(F32)).astype(BF16)
    both = _dot(h_hi, wcat[...])
    lg = both[:, :LANES] + both[:, LANES:] + _dot(h_lo, wcat[:, :LANES]) + br_ref[...]

    lane = lax.broadcasted_iota(jnp.int32, (tm, LANES), 1)
    lane_f = lane.astype(F32)
    is_grp = (lane >= N_EXPERTS) & (lane < N_EXPERTS + MOE_GROUPS)
    gl = jnp.where(is_grp, lg, NEG_BIG)
    mg = jnp.max(gl, axis=-1, keepdims=True)
    grp_lane = jnp.min(jnp.where(gl == mg, lane_f, 1e9), axis=-1, keepdims=True)
    p_g = 1.0 / jnp.sum(jnp.where(is_grp, jnp.exp(gl - mg), 0.0), axis=-1, keepdims=True)
    first = (grp_lane - N_EXPERTS) * EXPERTS_PER_GROUP
    in_grp = (lane_f >= first) & (lane_f < first + EXPERTS_PER_GROUP)
    el = jnp.where(in_grp, lg, NEG_BIG)
    t1 = jnp.max(el, axis=-1, keepdims=True)
    i1 = jnp.min(jnp.where(el == t1, lane_f, 1e9), axis=-1, keepdims=True)
    el2 = jnp.where(lane_f == i1, NEG_BIG, el)
    t2 = jnp.max(el2, axis=-1, keepdims=True)
    i2 = jnp.min(jnp.where(el2 == t2, lane_f, 1e9), axis=-1, keepdims=True)
    s = jnp.exp(t2 - t1)
    w1 = p_g / (1.0 + s)
    w2 = p_g * s / (1.0 + s)

    oh1 = (lane_f == i1)
    oh2 = (lane_f == i2)
    ohf = jnp.where(oh1 | oh2, 1.0, 0.0)
    before = _dot(tri_ref[...], ohf.astype(BF16))
    cnt = jnp.sum(ohf, axis=0, keepdims=True)
    seg = jnp.floor((cnt + (SEG_ALIGN - 1.0)) * (1.0 / SEG_ALIGN)) * SEG_ALIGN
    e_row = lax.broadcasted_iota(jnp.int32, (LANES, LANES), 0)
    e_col = lax.broadcasted_iota(jnp.int32, (LANES, LANES), 1)
    earlier = (e_row < e_col).astype(BF16)
    seg_off = _dot(jnp.broadcast_to(seg, (SUBLANES, LANES)).astype(BF16), earlier)[0:1]
    where_to = before + seg_off
    lpos1 = jnp.sum(jnp.where(oh1, where_to, 0.0), axis=-1, keepdims=True)
    lpos2 = jnp.sum(jnp.where(oh2, where_to, 0.0), axis=-1, keepdims=True)

    cols = [w1, w2, lpos1, lpos2]
    for wk in (w1, w2):
        hi = wk.astype(BF16).astype(F32)
        cols += [hi, wk - hi]
    cols.append(jnp.ones_like(w1))
    packed = jnp.zeros((tm, LANES), F32)
    for k, col in enumerate(cols):
        packed = jnp.where(lane == k, col, packed)
    route_ref[rows, :] = packed

    row = lax.broadcasted_iota(jnp.int32, (tm, LANES), 0)
    on_diag = (row % LANES) == lane
    per = tm // LANES
    for qi, col in enumerate((lpos1, lpos2)):
        picked = jnp.where(on_diag, col, 0.0)
        dense = jnp.sum(picked.reshape(per, LANES, LANES), axis=1).astype(jnp.int32)
        slots_ref[0, qi * per:(qi + 1) * per, :] = dense
    tbl_row = lax.broadcasted_iota(jnp.int32, (SUBLANES, LANES), 0)
    seg_ref[0] = jnp.where(tbl_row == 0, seg, jnp.where(tbl_row == 1, seg_off, 0.0)).astype(jnp.int32)


def _outproj_router(yr, ys, x2, mod3, npost, npre, wo_r, wo_s, w_router, b_router, seq_len):
    T, d = x2.shape
    tm = TM_OUT
    per_seq = seq_len // tm
    rw = yr.shape[1]
    tri = (jnp.arange(tm)[:, None] > jnp.arange(tm)[None, :]).astype(BF16)
    tok = lambda w: pl.BlockSpec((tm, w), lambda i: (i, 0))
    modv = lambda k: pl.BlockSpec((1, 1, d), lambda i: (i // per_seq, 0, k))
    tile3 = lambda r: pl.BlockSpec((1, r, LANES), lambda i: (i, 0, 0))
    slot_rows = TOP_K * (tm // LANES)
    return pl.pallas_call(
        _outproj_router_kernel,
        grid=(T // tm,),
        in_specs=[
            tok(rw), tok(rw), tok(d), modv(2), modv(3), modv(4),
            _const_spec((1, d)), _const_spec((1, d)),
            _const_spec(wo_r.shape), _const_spec(wo_s.shape), _const_spec(w_router.shape), _const_spec((1, LANES)),
            _const_spec((tm, tm)),
        ],
        out_specs=[tok(d), tok(d), tok(LANES), tile3(slot_rows), tile3(SUBLANES)],
        out_shape=[jax.ShapeDtypeStruct((T, d), F32), jax.ShapeDtypeStruct((T, d), F32),
                   jax.ShapeDtypeStruct((T, LANES), F32),
                   jax.ShapeDtypeStruct((T // tm, slot_rows, LANES), jnp.int32),
                   jax.ShapeDtypeStruct((T // tm, SUBLANES, LANES), jnp.int32)],
        scratch_shapes=[pltpu.VMEM((d, 2 * LANES), BF16)],
        compiler_params=pltpu.CompilerParams(dimension_semantics=("arbitrary",),
                                             vmem_limit_bytes=VMEM_LIMIT),
        name="outproj_router",
    )(yr, ys, x2, mod3, mod3, mod3, npost, npre, wo_r, wo_s, w_router, b_router, tri)


def _expert_kernel(be_ref, first_ref, slot_ref, next_ref, nused_ref, xs_ref, wg_hbm, wu_hbm, wd_hbm, y_ref,
                   wg_f, wu_f, wd_f, wg_b, wu_b, wd_b, sem):
    i = pl.program_id(0)

    def fetch(e, s):
        return [pltpu.make_async_copy(src.at[e], dst.at[s], sem.at[s, k])
                for k, (src, dst) in enumerate(((wg_hbm, wg_f), (wu_hbm, wu_f), (wd_hbm, wd_f)))]

    @pl.when(i == 0)
    def _():
        for c in fetch(be_ref[0], slot_ref[0]):
            c.start()

    @pl.when(first_ref[i] == 1)
    def _():
        s = slot_ref[i]

        @pl.when(next_ref[i] >= 0)
        def _():
            for c in fetch(next_ref[i], 1 - s):
                c.start()

        for c in fetch(be_ref[i], s):
            c.wait()
        wg_b[...] = wg_f[s].astype(BF16)
        wu_b[...] = wu_f[s].astype(BF16)
        wd_b[...] = wd_f[s].astype(BF16)

    @pl.when(i < nused_ref[0])
    def _():
        d = y_ref.shape[1]
        side = xs_ref[:, d:d + LANES].astype(F32)
        second = side[:, 8:9] == 2.0
        unscale = jnp.where(second, 0.5, 1.0)
        weight = jnp.where(second, side[:, 6:7] + side[:, 7:8], side[:, 4:5] + side[:, 5:6]) * unscale
        xb = xs_ref[:, 0:d] * unscale.astype(BF16)
        hid = (_silu(_dot(xb, wg_b[...])) * _dot(xb, wu_b[...])).astype(BF16)
        y_ref[...] = (_dot(hid, wd_b[...]) * weight).astype(BF16)

    @pl.when(i >= nused_ref[0])
    def _():
        y_ref[...] = jnp.zeros_like(y_ref)


def _expert_plan(padded, pad_end, n_blocks, mb):
    n_used = (pad_end[-1:] // mb).astype(jnp.int32)
    blk_start = jnp.arange(n_blocks, dtype=jnp.int32) * mb
    experts = jnp.arange(N_EXPERTS, dtype=jnp.int32)
    blk_expert = jnp.minimum(jnp.sum((pad_end[None, :] <= blk_start[:, None]).astype(jnp.int32), axis=1),
                             N_EXPERTS - 1)
    prev = jnp.concatenate([jnp.full((1,), -1, jnp.int32), blk_expert[:-1]])
    first = ((blk_start < pad_end[-1]) & (blk_expert != prev)).astype(jnp.int32)
    slot = (jnp.cumsum(first) - 1) % 2
    later = jnp.where((padded > 0)[None, :] & (experts[None, :] > experts[:, None]), experts[None, :], N_EXPERTS)
    next_nonempty = jnp.min(later, axis=1)
    next_nonempty = jnp.where(next_nonempty == N_EXPERTS, -1, next_nonempty)
    nxt = jnp.sum(jnp.where(blk_expert[:, None] == experts[None, :], next_nonempty[None, :], 0), axis=1)
    return [a.astype(jnp.int32) for a in (blk_expert, first, slot, nxt, n_used)]


def _experts(plan, xs, w_gate, w_up, w_down):
    rows, dp = xs.shape
    n_exp, d, de = w_gate.shape
    assert dp == d + LANES
    mb = MB_EXPERT
    cap = rows // mb * mb
    grid_spec = pltpu.PrefetchScalarGridSpec(
        num_scalar_prefetch=len(plan),
        grid=(cap // mb,),
        in_specs=[
            pl.BlockSpec((mb, dp), lambda i, be, fi, sl, nx, nu: (jnp.minimum(i, nu[0] - 1), 0)),
            pl.BlockSpec(memory_space=pl.ANY), pl.BlockSpec(memory_space=pl.ANY), pl.BlockSpec(memory_space=pl.ANY),
        ],
        out_specs=pl.BlockSpec((mb, d), lambda i, be, fi, sl, nx, nu: (i, 0)),
        scratch_shapes=[pltpu.VMEM((2, d, de), F32), pltpu.VMEM((2, d, de), F32), pltpu.VMEM((2, de, d), F32),
                        pltpu.VMEM((d, de), BF16), pltpu.VMEM((d, de), BF16), pltpu.VMEM((de, d), BF16),
                        pltpu.SemaphoreType.DMA((2, 3))],
    )
    return pl.pallas_call(
        _expert_kernel,
        grid_spec=grid_spec,
        out_shape=jax.ShapeDtypeStruct((cap, d), BF16),
        compiler_params=pltpu.CompilerParams(dimension_semantics=("arbitrary",),
                                             vmem_limit_bytes=VMEM_LIMIT),
        name="experts",
    )(*plan, xs, w_gate, w_up, w_down)


def _segment_pieces(tile, segrow_ref, seglen_ref, segoff_ref, act):
    def per_expert(e, carry):
        idx = tile * N_EXPERTS + e
        g0 = segrow_ref[idx]
        l0 = segoff_ref[idx]

        def piece(j, c2):
            act(pl.multiple_of(l0 + j * SEG_ALIGN, SEG_ALIGN), pl.multiple_of(g0 + j * SEG_ALIGN, SEG_ALIGN))
            return c2

        lax.fori_loop(0, seglen_ref[idx], piece, 0)
        return carry

    lax.fori_loop(0, N_EXPERTS, per_expert, 0)


def _wait_rows(total, row_copy):
    size = SEG_ALIGN
    while size <= _stage_rows(max(TD_DISPATCH, TF_COMBINE)):
        @pl.when((total & size) != 0)
        def _(size=size):
            row_copy(size).wait()
        size *= 2


def _stage_rows(tile_tokens):
    return TOP_K * tile_tokens + N_EXPERTS * SEG_ALIGN


def _dispatch_seg_kernel(pad_end_ref, zero_from_ref, segrow_ref, seglen_ref, segoff_ref, tilerows_ref,
                         lpos_ref, route_ref, h_ref, xs_hbm, zbuf, stage, sem, zsem):
    i = pl.program_id(0)
    nt = pl.num_programs(0)
    td, d = h_ref.shape
    sr = stage.shape[1]
    per = td // LANES

    def zero_fills(act):
        def fill(row, n):
            act(pltpu.make_async_copy(zbuf.at[pl.ds(0, n)], xs_hbm.at[pl.ds(pl.multiple_of(row, SEG_ALIGN), n)], zsem))

        def region(e, carry):
            row = zero_from_ref[e]
            left = pad_end_ref[e] - row
            size = ZERO_PIECE
            while size >= SEG_ALIGN:
                take = (left & size) != 0

                @pl.when(take)
                def _(row=row, size=size):
                    fill(row, size)

                row = row + jnp.where(take, size, 0)
                size //= 2
            return carry

        def tail(p, carry):
            @pl.when(p * ZERO_PIECE >= pad_end_ref[N_EXPERTS - 1])
            def _():
                fill(p * ZERO_PIECE, ZERO_PIECE)
            return carry

        lax.fori_loop(0, N_EXPERTS, region, 0)
        lax.fori_loop(0, xs_hbm.shape[0] // ZERO_PIECE, tail, 0)

    @pl.when(i == 0)
    def _():
        zbuf[...] = jnp.zeros_like(zbuf)
        zero_fills(lambda c: c.start())

    def shipped(tile):
        _wait_rows(tilerows_ref[tile], lambda n: pltpu.make_async_copy(
            stage.at[tile % 2, pl.ds(0, n)], xs_hbm.at[pl.ds(0, n)], sem.at[tile % 2]))

    @pl.when(i >= 2)
    def _():
        shipped(i - 2)

    srow = lax.broadcasted_iota(jnp.int32, (sr, LANES), 0)
    place = jnp.concatenate(
        [jnp.where(srow == lpos_ref[0, cb:cb + 1, :], 1.0,
                   jnp.where(srow == lpos_ref[0, per + cb:per + cb + 1, :], 2.0, 0.0)) for cb in range(per)],
        axis=1).astype(BF16)
    slot = i % 2
    stage[slot] = _dot(place, jnp.concatenate([h_ref[...].astype(BF16), route_ref[...].astype(BF16)],
                                              axis=1)).astype(BF16)
    _segment_pieces(i, segrow_ref, seglen_ref, segoff_ref, lambda lr, gr: pltpu.make_async_copy(
        stage.at[slot, pl.ds(lr, SEG_ALIGN)], xs_hbm.at[pl.ds(gr, SEG_ALIGN)], sem.at[slot]).start())

    @pl.when(i == nt - 1)
    def _():
        @pl.when(i >= 1)
        def _():
            shipped(i - 1)
        shipped(i)
        zero_fills(lambda c: c.wait())


def _dispatch_seg(plan, lpos, route, h2, cap):
    T, d = h2.shape
    td = TD_DISPATCH
    nt = T // td
    width = d + LANES
    sr = _stage_rows(td)
    n_pre = len(plan)
    grid_spec = pltpu.PrefetchScalarGridSpec(
        num_scalar_prefetch=n_pre,
        grid=(nt,),
        in_specs=[
            pl.BlockSpec((1, lpos.shape[1], LANES), lambda i, *_: (i, 0, 0)),
            pl.BlockSpec((td, LANES), lambda i, *_: (i, 0)),
            pl.BlockSpec((td, d), lambda i, *_: (i, 0)),
        ],
        out_specs=pl.BlockSpec(memory_space=pl.ANY),
        scratch_shapes=[pltpu.VMEM((ZERO_PIECE, width), BF16), pltpu.VMEM((2, sr, width), BF16),
                        pltpu.SemaphoreType.DMA((2,)), pltpu.SemaphoreType.DMA(())],
    )
    return pl.pallas_call(
        _dispatch_seg_kernel,
        grid_spec=grid_spec,
        out_shape=jax.ShapeDtypeStruct((cap, width), BF16),
        compiler_params=pltpu.CompilerParams(dimension_semantics=("arbitrary",), vmem_limit_bytes=VMEM_LIMIT),
        name="dispatch",
    )(*plan, lpos, route, h2)


def _combine_seg_kernel(segrow_ref, seglen_ref, segoff_ref, tilerows_ref, route_ref, x1_ref, g2_ref, nw_ref,
                        yb_hbm, o_ref, stage, sem):
    i = pl.program_id(0)
    nt = pl.num_programs(0)
    tf = x1_ref.shape[0]
    sr = stage.shape[1]

    def fetch(tile, slot):
        _segment_pieces(tile, segrow_ref, seglen_ref, segoff_ref, lambda lr, gr: pltpu.make_async_copy(
            yb_hbm.at[pl.ds(gr, SEG_ALIGN)], stage.at[slot, pl.ds(lr, SEG_ALIGN)], sem.at[slot]).start())

    @pl.when(i == 0)
    def _():
        stage[...] = jnp.zeros_like(stage)
        fetch(i, 0)

    for slot in range(2):
        @pl.when((i + 1 < nt) & (i % 2 != slot))
        def _(slot=slot):
            fetch(i + 1, slot)

    _wait_rows(tilerows_ref[i], lambda n: pltpu.make_async_copy(
        yb_hbm.at[pl.ds(0, n)], stage.at[i % 2, pl.ds(0, n)], sem.at[i % 2]))

    local_row = route_ref[:, 2:2 + TOP_K].astype(jnp.int32)
    scol = lax.broadcasted_iota(jnp.int32, (tf, sr), 1)
    pick = jnp.where((scol == local_row[:, 0:1]) | (scol == local_row[:, 1:2]), 1.0, 0.0).astype(BF16)
    out = _dot(pick, stage[i % 2])
    ms = jnp.mean(out * out, axis=-1, keepdims=True)
    o_ref[...] = x1_ref[...] + g2_ref[0] * ((out * lax.rsqrt(ms + EPS)) * nw_ref[...])


def _combine_seg(seg_plan, route, x1, mod3, nw, yb, seq_len):
    T, d = x1.shape
    tf = TF_COMBINE
    nt = T // tf
    per_seq = seq_len // tf
    grid_spec = pltpu.PrefetchScalarGridSpec(
        num_scalar_prefetch=len(seg_plan),
        grid=(nt,),
        in_specs=[
            pl.BlockSpec((tf, LANES), lambda i, *_: (i, 0)),
            pl.BlockSpec((tf, d), lambda i, *_: (i, 0)),
            pl.BlockSpec((1, 1, d), lambda i, *_: (i // per_seq, 0, 5)),
            pl.BlockSpec((1, d), lambda i, *_: (0, 0)),
            pl.BlockSpec(memory_space=pl.ANY),
        ],
        out_specs=pl.BlockSpec((tf, d), lambda i, *_: (i, 0)),
        scratch_shapes=[pltpu.VMEM((2, _stage_rows(tf), d), yb.dtype), pltpu.SemaphoreType.DMA((2,))],
    )
    return pl.pallas_call(
        _combine_seg_kernel,
        grid_spec=grid_spec,
        out_shape=jax.ShapeDtypeStruct((T, d), F32),
        compiler_params=pltpu.CompilerParams(dimension_semantics=("arbitrary",), vmem_limit_bytes=VMEM_LIMIT),
        name="combine",
    )(*seg_plan, route, x1, mod3, nw, yb)


def _rope_tables(L, n_heads):
    quarter = RET_DK // 4
    freqs = ROPE_BASE ** (-jnp.arange(quarter, dtype=F32) / quarter)
    t = jnp.arange(L)
    ang_r = (t // GRID_W).astype(F32)[:, None] * freqs
    ang_c = (t % GRID_W).astype(F32)[:, None] * freqs
    cos = jnp.concatenate([jnp.cos(ang_r)] * 2 + [jnp.cos(ang_c)] * 2, axis=-1)
    sin = jnp.concatenate([-jnp.sin(ang_r), jnp.sin(ang_r), -jnp.sin(ang_c), jnp.sin(ang_c)], axis=-1)
    return jnp.tile(cos, (1, n_heads)), jnp.tile(sin, (1, n_heads))


def _lane_pad(v, width=LANES):
    return jnp.pad(v, [(0, 0)] * (v.ndim - 1) + [(0, width - v.shape[-1])])


def kernel(x, c, ctx, c_ctx, w_mod, b_mod, norm_pre_mix, norm_post_mix, norm_pre_ffn, norm_post_ffn, w_in, w_out, ret_decay_f, ret_decay_b, ret_gn_w, ssd_conv_w, ssd_conv_b, ssd_dt_bias_f, ssd_dt_bias_b, ssd_a_log_f, ssd_a_log_b, ssd_d, ssd_norm_w, moe_w_rg, moe_b_rg, moe_w_re, moe_b_re, moe_w_gate, moe_w_up, moe_w_down):
    b, L, d = x.shape
    assert w_mod.shape[0] == 1, "single layer: context outputs are never needed"
    assert TM_OUT == TD_DISPATCH == TF_COMBINE, "router, dispatch and combine share one slot-row layout"
    rw = RET_HEADS * RET_DK
    nconv = SSD_WIDTH + 2 * SSD_GROUPS * SSD_STATE
    T = b * L

    mod_rows = -(-(b + 1) // SUBLANES) * SUBLANES
    c_all = jnp.zeros((mod_rows, d), F32).at[:b].set(c).at[b].set(c_ctx)
    mod3 = _modulation(c_all, w_mod[0], b_mod[0]).reshape(mod_rows, 1, 6 * d)

    wi = w_in[0]
    o = 0
    wq = wi[:, o:o + rw]; o += rw
    wk = wi[:, o:o + rw]; o += rw
    wv = wi[:, o:o + rw]; o += rw
    wg = wi[:, o:o + rw]; o += rw
    wz = wi[:, o:o + SSD_WIDTH]; o += SSD_WIDTH
    wxbc = wi[:, o:o + nconv].astype(BF16); o += nconv
    wdt = _lane_pad(wi[:, o:o + 2 * SSD_HEADS]).astype(BF16)
    wqk = jnp.concatenate([wq, wk], axis=1).astype(BF16)
    wvgz = jnp.concatenate([wv, wg, wz], axis=1).astype(BF16)
    cos_t, sin_t = _rope_tables(L, RET_HEADS)
    nw1 = norm_pre_mix[0].reshape(1, d)

    q, k, v, g, z, xbc, dt = _inproj(x, mod3, nw1, wqk, wvgz, wxbc, wdt, cos_t, sin_t)
    kc, vc, xbcc, dtc = _inproj_ctx(ctx, mod3, b, nw1, wk.astype(BF16), wv.astype(BF16), wxbc, wdt)

    conv_w8 = jnp.pad(ssd_conv_w[0], ((0, SUBLANES - SSD_CONV), (0, 0)))
    dt_bias = _lane_pad(jnp.concatenate([ssd_dt_bias_f[0], ssd_dt_bias_b[0]])[None, :])
    a_log = _lane_pad(jnp.concatenate([ssd_a_log_f[0], ssd_a_log_b[0]])[None, :])
    d_skip = jnp.repeat(ssd_d[0], SSD_HEADDIM)[None, :]
    ys = _ssd(xbc, z, dt, xbcc, dtc, conv_w8, ssd_conv_b[0][None, :], dt_bias, a_log, d_skip,
              ssd_norm_w[0][None, :])

    yr = _retention(q, k, v, g, kc, vc,
                    jnp.repeat(ret_decay_f[0], RET_DK)[None, :], jnp.repeat(ret_decay_b[0], RET_DK)[None, :],
                    ret_gn_w[0][None, :])

    wo = w_out[0].astype(BF16)
    w_router = _lane_pad(jnp.concatenate(
        [jnp.transpose(moe_w_re[0], (1, 0, 2)).reshape(d, N_EXPERTS), moe_w_rg[0]], axis=1))
    b_router = _lane_pad(jnp.concatenate([moe_b_re[0].reshape(-1), moe_b_rg[0]])[None, :])
    x1, h2, route, lpos, seg = _outproj_router(
        yr.reshape(T, rw), ys.reshape(T, SSD_WIDTH), x.reshape(T, d), mod3,
        norm_post_mix[0][None, :], norm_pre_ffn[0][None, :], wo[:rw], wo[rw:], w_router, b_router, L)

    mb = MB_EXPERT
    nt = T // TM_OUT
    n_blocks = -(-(T * TOP_K + nt * N_EXPERTS * (SEG_ALIGN - 1) + N_EXPERTS * (mb - 1)) // mb)
    seg_len = seg[:, 0, :N_EXPERTS]
    seg_off = seg[:, 1, :N_EXPERTS]
    used = jnp.sum(seg_len, axis=0)
    padded = (used + mb - 1) // mb * mb
    pad_end = jnp.cumsum(padded)
    pad_start = pad_end - padded
    seg_row = pad_start[None, :] + jnp.cumsum(seg_len, axis=0) - seg_len
    seg_plan = [a.reshape(-1).astype(jnp.int32)
                for a in (seg_row, seg_len // SEG_ALIGN, seg_off, jnp.sum(seg_len, axis=1))]
    zero_from = (pad_start + used).astype(jnp.int32)

    xs = _dispatch_seg([pad_end.astype(jnp.int32), zero_from] + seg_plan, lpos, route, h2, n_blocks * mb)
    yb = _experts(_expert_plan(padded, pad_end, n_blocks, mb), xs, moe_w_gate[0], moe_w_up[0], moe_w_down[0])
    out = _combine_seg(seg_plan, route, x1, mod3, norm_post_ffn[0][None, :], yb, L)
    return out.reshape(b, L, d)
```

```python
import jax
import jax.numpy as jnp
from jax import lax
from jax.experimental import pallas as pl
from jax.experimental.pallas import tpu as pltpu

F32 = jnp.float32
BF16 = jnp.bfloat16

LANES = 128
SUBLANES = 8
BF16_TILE_ROWS = 16
V7X_VMEM_BYTES = 64 * 1024 * 1024
VMEM_LIMIT = V7X_VMEM_BYTES * 3 // 4
VMEM_LIMIT_SSD = V7X_VMEM_BYTES * 7 // 8

EPS = 1e-6
CHUNK = 128
GRID_W = 64
RET_HEADS = 4
RET_DK = 128
ROPE_BASE = 10000.0
SSD_HEADS = 8
SSD_HEADDIM = 64
SSD_GROUPS = 2
SSD_STATE = 128
SSD_WIDTH = SSD_HEADS * SSD_HEADDIM
SSD_CONV = 5
SSD_PAIRS = SSD_WIDTH // LANES
MOE_GROUPS = 4
EXPERTS_PER_GROUP = 8
N_EXPERTS = MOE_GROUPS * EXPERTS_PER_GROUP
TOP_K = 2
CONV_HALO = SUBLANES

TM_PROJ = 512
TM_OUT = 512
TD_DISPATCH = TM_OUT
MB_EXPERT = 512
EXPERT_BLOCKS_PER_STEP = 2
ZERO_PIECE = MB_EXPERT // 2
TF_COMBINE = TM_OUT
SEG_ALIGN = BF16_TILE_ROWS
RET_UNROLL = 2
NEG_BIG = -1e30


def _silu(v):
    return v * jax.nn.sigmoid(v)


def _dot(a, b):
    return jnp.dot(a, b, preferred_element_type=F32)


def _dot_tn(a, b):
    return lax.dot_general(a, b, (((0,), (0,)), ((), ())), preferred_element_type=F32)


def _dot_nt(a, b):
    return lax.dot_general(a, b, (((1,), (1,)), ((), ())), preferred_element_type=F32)


def _mod_kernel(c_ref, w_ref, b_ref, o_ref):
    a = _silu(c_ref[...])
    w = w_ref[...]
    a_hi = a.astype(BF16)
    a_lo = (a - a_hi.astype(F32)).astype(BF16)
    w_hi = w.astype(BF16)
    w_lo = (w - w_hi.astype(F32)).astype(BF16)
    o_ref[...] = _dot(a_hi, w_hi) + _dot(a_lo, w_hi) + _dot(a_hi, w_lo) + b_ref[...]


def _modulation(c_all, w_mod, b_mod):
    rows, d = c_all.shape
    n = w_mod.shape[1]
    return pl.pallas_call(
        _mod_kernel,
        grid=(n // d,),
        in_specs=[
            pl.BlockSpec((rows, d), lambda j: (0, 0)),
            pl.BlockSpec((d, d), lambda j: (0, j)),
            pl.BlockSpec((1, d), lambda j: (0, j)),
        ],
        out_specs=pl.BlockSpec((rows, d), lambda j: (0, j)),
        out_shape=jax.ShapeDtypeStruct((rows, n), F32),
        name="modulation",
    )(c_all, w_mod, b_mod.reshape(1, n))


def _norm_mod(x, nw, sc, sh):
    ms = jnp.mean(x * x, axis=-1, keepdims=True)
    return (x * lax.rsqrt(ms + EPS)) * (nw * (1.0 + sc)) + sh


def _rope(t, cos, sin_signed, first_half):
    width = t.shape[-1]
    quarter = RET_DK // 4
    swapped = jnp.where(first_half, pltpu.roll(t, width - quarter, 1), pltpu.roll(t, quarter, 1))
    return t * cos + swapped * sin_signed


def _inproj_kernel(x_ref, sh_ref, sc_ref, nw_ref, wqk_ref, wvgz_ref, wxbc_ref, wdt_ref, cos_ref, sin_ref,
                   q_ref, k_ref, v_ref, g_ref, z_ref, xbc_ref, dt_ref):
    hb = _norm_mod(x_ref[0], nw_ref[...], sc_ref[0], sh_ref[0]).astype(BF16)
    rw = q_ref.shape[-1]
    qk = _dot(hb, wqk_ref[...])
    cos = cos_ref[...]
    sin = sin_ref[...]
    lane = lax.broadcasted_iota(jnp.int32, cos.shape, 1)
    first_half = (lane % (RET_DK // 2)) < (RET_DK // 4)
    q_ref[0] = _rope(qk[:, :rw], cos, sin, first_half).astype(BF16)
    k_ref[0] = (_rope(qk[:, rw:], cos, sin, first_half) * (RET_DK ** -0.5)).astype(BF16)
    vgz = _dot(hb, wvgz_ref[...])
    v_ref[0] = vgz[:, :rw].astype(BF16)
    g_ref[0] = vgz[:, rw:2 * rw].astype(BF16)
    z_ref[0] = vgz[:, 2 * rw:].astype(BF16)
    xbc_ref[0] = _dot(hb, wxbc_ref[...]).astype(BF16)
    dt_ref[0] = _dot(hb, wdt_ref[...])


def _inproj_ctx_kernel(x_ref, sh_ref, sc_ref, nw_ref, wk_ref, wv_ref, wxbc_ref, wdt_ref,
                       k_ref, v_ref, xbc_ref, dt_ref):
    hb = _norm_mod(x_ref[0], nw_ref[...], sc_ref[0], sh_ref[0]).astype(BF16)
    k_ref[0] = (_dot(hb, wk_ref[...]) * (RET_DK ** -0.5)).astype(BF16)
    v_ref[0] = _dot(hb, wv_ref[...]).astype(BF16)
    xbc_ref[0] = _dot(hb, wxbc_ref[...]).astype(BF16)
    dt_ref[0] = _dot(hb, wdt_ref[...])


def _const_spec(shape):
    nd = len(shape)
    return pl.BlockSpec(shape, lambda *_: (0,) * nd)


def _inproj(x, mod3, nw, wqk, wvgz, wxbc, wdt, cos_t, sin_t):
    b, L, d = x.shape
    tm = min(TM_PROJ, L)
    rw = wqk.shape[1] // 2
    tok = lambda w: pl.BlockSpec((1, tm, w), lambda i, j: (i, j, 0))
    out_bf = lambda w: jax.ShapeDtypeStruct((b, L, w), BF16)
    return pl.pallas_call(
        _inproj_kernel,
        grid=(b, L // tm),
        in_specs=[
            tok(d),
            pl.BlockSpec((1, 1, d), lambda i, j: (i, 0, 0)),
            pl.BlockSpec((1, 1, d), lambda i, j: (i, 0, 1)),
            _const_spec((1, d)),
            _const_spec(wqk.shape), _const_spec(wvgz.shape), _const_spec(wxbc.shape), _const_spec(wdt.shape),
            pl.BlockSpec((tm, rw), lambda i, j: (j, 0)),
            pl.BlockSpec((tm, rw), lambda i, j: (j, 0)),
        ],
        out_specs=[tok(rw), tok(rw), tok(rw), tok(rw), tok(rw), tok(wxbc.shape[1]), tok(LANES)],
        out_shape=[out_bf(rw), out_bf(rw), out_bf(rw), out_bf(rw), out_bf(rw), out_bf(wxbc.shape[1]),
                   jax.ShapeDtypeStruct((b, L, LANES), F32)],
        compiler_params=pltpu.CompilerParams(vmem_limit_bytes=VMEM_LIMIT),
        name="inproj",
    )(x, mod3, mod3, nw, wqk, wvgz, wxbc, wdt, cos_t, sin_t)


def _inproj_ctx(ctx, mod3, ctx_row, nw, wk, wv, wxbc, wdt):
    b, L, d = ctx.shape
    tm = min(TM_PROJ, L)
    rw = wk.shape[1]
    tok = lambda w: pl.BlockSpec((1, tm, w), lambda i, j: (i, j, 0))
    out_bf = lambda w: jax.ShapeDtypeStruct((b, L, w), BF16)
    return pl.pallas_call(
        _inproj_ctx_kernel,
        grid=(b, L // tm),
        in_specs=[
            tok(d),
            pl.BlockSpec((1, 1, d), lambda i, j: (ctx_row, 0, 0)),
            pl.BlockSpec((1, 1, d), lambda i, j: (ctx_row, 0, 1)),
            _const_spec((1, d)),
            _const_spec(wk.shape), _const_spec(wv.shape), _const_spec(wxbc.shape), _const_spec(wdt.shape),
        ],
        out_specs=[tok(rw), tok(rw), tok(wxbc.shape[1]), tok(LANES)],
        out_shape=[out_bf(rw), out_bf(rw), out_bf(wxbc.shape[1]), jax.ShapeDtypeStruct((b, L, LANES), F32)],
        compiler_params=pltpu.CompilerParams(vmem_limit_bytes=VMEM_LIMIT),
        name="inproj_ctx",
    )(ctx, mod3, mod3, nw, wk, wv, wxbc, wdt)


def _ssd_kernel(xbc_ref, z_ref, dt_ref, xbcc_ref, dtc_ref, cw_ref, cb_ref, dtb_ref, alog_ref, dsk_ref, nw_ref,
                y_ref,
                xpad, xpadc, u, uc, dtv, dav, dtcv, dacv, sf_scr, kb_scr, acum, ecum, dec_scr,
                arow_scr, erow_scr, dtrow_scr):
    L = xbc_ref.shape[1]
    Lc = xbcc_ref.shape[1]
    nch = L // CHUNK
    nchc = Lc // CHUNK
    win = CHUNK + 2 * CONV_HALO
    nconv = xbc_ref.shape[2]
    nh = SSD_HEADS

    def conv_pass(src_ref, pad_ref, dst_ref, n_chunks, length):
        zeros = jnp.zeros((CONV_HALO, nconv), F32)
        pad_ref[0:CONV_HALO, :] = zeros
        pad_ref[CONV_HALO + length:2 * CONV_HALO + length, :] = zeros
        pad_ref[CONV_HALO:CONV_HALO + length, :] = src_ref[0].astype(F32)

        def chunk(c, carry):
            base = pl.multiple_of(c * CHUNK, CHUNK)
            for cb_i in range(nconv // LANES):
                cols = slice(cb_i * LANES, (cb_i + 1) * LANES)
                w = pad_ref[pl.ds(base, win), cols]
                acc = cb_ref[:, cols] + w[CONV_HALO:CONV_HALO + CHUNK] * cw_ref[SSD_CONV // 2:SSD_CONV // 2 + 1, cols]
                for j in range(SSD_CONV):
                    if j == SSD_CONV // 2:
                        continue
                    shifted = pltpu.roll(w, (SSD_CONV // 2 - j) % win, 0)
                    acc = acc + shifted[CONV_HALO:CONV_HALO + CHUNK] * cw_ref[j:j + 1, cols]
                dst_ref[pl.ds(base, CHUNK), cols] = _silu(acc).astype(BF16)
            return carry

        lax.fori_loop(0, n_chunks, chunk, 0)

    conv_pass(xbcc_ref, xpadc, uc, nchc, Lc)
    conv_pass(xbc_ref, xpad, u, nch, L)

    a_neg = -jnp.exp(alog_ref[...])
    dtv[...] = jax.nn.softplus(dt_ref[0] + dtb_ref[...])
    dav[...] = dtv[...] * a_neg
    dtcv[...] = jax.nn.softplus(dtc_ref[0] + dtb_ref[...])
    dacv[...] = dtcv[...] * a_neg

    row_i = lax.broadcasted_iota(jnp.int32, (CHUNK, CHUNK), 0)
    col_i = lax.broadcasted_iota(jnp.int32, (CHUNK, CHUNK), 1)
    causal = col_i <= row_i
    lo_half = col_i < SSD_HEADDIM
    fwd_lane = col_i < nh
    head_of = lax.broadcasted_iota(jnp.int32, (CHUNK, SSD_WIDTH), 1) // SSD_HEADDIM
    src_col = lax.broadcasted_iota(jnp.int32, (CHUNK, SSD_WIDTH), 0)
    exp_f = (head_of == src_col).astype(BF16)
    exp_b = (head_of == src_col - nh).astype(BF16)
    exp_fb = jnp.concatenate([exp_f, exp_b], axis=1)

    def split3(v):
        hi = v.astype(BF16)
        r1 = v - hi.astype(F32)
        mid = r1.astype(BF16)
        return hi, mid, (r1 - mid.astype(F32)).astype(BF16)

    def times_onehot(v, m, passes=3):
        parts = split3(v)[:passes]
        acc = _dot(parts[0], m)
        for part in parts[1:]:
            acc = acc + _dot(part, m)
        return acc

    def colb(mat, r):
        return jnp.broadcast_to(mat[:, r:r + 1], (CHUNK, CHUNK))

    def pair_sel(a, b_):
        return jnp.where(lo_half, a, b_)

    gw = 2 * LANES

    def chunk_terms(u_ref, dt_s, da_s, base):
        dt = dt_s[pl.ds(base, CHUNK), :]
        da = da_s[pl.ds(base, CHUNK), :]
        acol = da
        for step in (1, 2, 4, 8, 16, 32, 64):
            acol = acol + jnp.where(row_i >= step, pltpu.roll(acol, step, 0), 0.0)
        ecol = acol - da
        last = acol[CHUNK - 1:CHUNK, :]
        wgt = jnp.where(fwd_lane, jnp.exp(last - acol), jnp.exp(ecol)) * dt
        scale = jnp.where(fwd_lane, jnp.exp(acol), jnp.exp(last - ecol))
        wide = times_onehot(jnp.concatenate([wgt, scale], axis=0), exp_fb, passes=1)
        dec = times_onehot(jnp.broadcast_to(jnp.exp(last), (SUBLANES, LANES)), exp_fb)[0:1]
        xs = u_ref[pl.ds(base, CHUNK), 0:SSD_WIDTH].astype(F32)
        kmats = []
        for g in range(SSD_GROUPS):
            xw = jnp.concatenate([xs[:, g * gw:(g + 1) * gw] * wide[:CHUNK, g * gw:(g + 1) * gw],
                                  xs[:, g * gw:(g + 1) * gw] * wide[:CHUNK, SSD_WIDTH + g * gw:SSD_WIDTH + (g + 1) * gw]],
                                 axis=1).astype(BF16)
            bm = u_ref[pl.ds(base, CHUNK), SSD_WIDTH + g * SSD_STATE:SSD_WIDTH + (g + 1) * SSD_STATE]
            kmats.append(_dot_tn(bm, xw))
        return dt, acol, ecol, wide[CHUNK:], dec, kmats

    def advance(s, dec, kmats, backward):
        off = SSD_WIDTH if backward else 0
        koff = gw if backward else 0
        return [dec[:, off + g * gw:off + (g + 1) * gw] * s[g] + kmats[g][:, koff:koff + gw]
                for g in range(SSD_GROUPS)]

    ctx_terms = [chunk_terms(uc, dtcv, dacv, c * CHUNK) for c in range(nchc)]
    s_f0 = [jnp.zeros((SSD_STATE, gw), F32) for _ in range(SSD_GROUPS)]
    for c in range(nchc):
        s_f0 = advance(s_f0, ctx_terms[c][4], ctx_terms[c][5], False)
    s_b0 = [jnp.zeros((SSD_STATE, gw), F32) for _ in range(SSD_GROUPS)]
    for c in reversed(range(nchc)):
        s_b0 = advance(s_b0, ctx_terms[c][4], ctx_terms[c][5], True)

    def prep(c, carry):
        base = pl.multiple_of(c * CHUNK, CHUNK)
        dt, acol, ecol, scale, dec, kmats = chunk_terms(u, dtv, dav, base)
        acum[pl.ds(base, CHUNK), :] = acol
        ecum[pl.ds(base, CHUNK), :] = ecol
        hrow = pl.ds(pl.multiple_of(c * 2 * nh, 2 * nh), 2 * nh)
        arow_scr[hrow, :] = acol.T[:2 * nh]
        erow_scr[hrow, :] = ecol.T[:2 * nh]
        dtrow_scr[hrow, :] = dt.T[:2 * nh]
        xpad[pl.ds(base, CHUNK), :] = scale
        dec_scr[pl.ds(pl.multiple_of(c * SUBLANES, SUBLANES), SUBLANES), :] = jnp.broadcast_to(dec, (SUBLANES, 2 * SSD_WIDTH))
        for g in range(SSD_GROUPS):
            sf_scr[c, g] = kmats[g][:, :gw]
            kb_scr[c, g] = kmats[g][:, gw:]
        return carry

    lax.fori_loop(0, nch, prep, 0)

    def chunk_dec(c):
        return dec_scr[pl.ds(pl.multiple_of(c * SUBLANES, SUBLANES), 1), :]

    def fwd(c, s_old):
        dec = chunk_dec(c)
        new = []
        for g in range(SSD_GROUPS):
            new.append(dec[:, g * gw:(g + 1) * gw] * s_old[g] + sf_scr[c, g])
            sf_scr[c, g] = s_old[g]
        return tuple(new)

    lax.fori_loop(0, nch, fwd, tuple(s_f0))

    def bwd(i, s_b):
        c = nch - 1 - i
        base = pl.multiple_of(c * CHUNK, CHUNK)
        acol = acum[pl.ds(base, CHUNK), :]
        ecol = ecum[pl.ds(base, CHUNK), :]
        hrow = pl.ds(pl.multiple_of(c * 2 * nh, 2 * nh), 2 * nh)
        arow = arow_scr[hrow, :]
        erow = erow_scr[hrow, :]
        dt_t = dtrow_scr[hrow, :]
        scale = xpad[pl.ds(base, CHUNK), :]
        ys = []
        for g in range(SSD_GROUPS):
            bm = u[pl.ds(base, CHUNK), SSD_WIDTH + g * SSD_STATE:SSD_WIDTH + (g + 1) * SSD_STATE]
            cm = u[pl.ds(base, CHUNK), SSD_WIDTH + (SSD_GROUPS + g) * SSD_STATE:SSD_WIDTH + (SSD_GROUPS + g + 1) * SSD_STATE]
            cbm = _dot_nt(cm, bm)
            cs_f = _dot(cm, sf_scr[c, g].astype(BF16))
            cs_b = _dot(cm, s_b[g].astype(BF16))
            for pp in range(SSD_PAIRS // SSD_GROUPS):
                p = g * (SSD_PAIRS // SSD_GROUPS) + pp
                xs_b = u[pl.ds(base, CHUNK), p * LANES:(p + 1) * LANES]
                y_h = []
                for hh in range(2):
                    r = 2 * p + hh
                    arg = jnp.where(causal, colb(acol, r) - arow[r:r + 1, :],
                                    erow[nh + r:nh + r + 1, :] - colb(ecol, nh + r))
                    coef = jnp.where(causal, dt_t[r:r + 1, :], dt_t[nh + r:nh + r + 1, :])
                    gm = (cbm * (jnp.exp(arg) * coef)).astype(BF16)
                    y_h.append(_dot(gm, xs_b))
                sl = slice(pp * LANES, (pp + 1) * LANES)
                wl = slice(p * LANES, (p + 1) * LANES)
                wlb = slice(SSD_WIDTH + p * LANES, SSD_WIDTH + (p + 1) * LANES)
                ys.append(pair_sel(y_h[0], y_h[1]) + cs_f[:, sl] * scale[:, wl] + cs_b[:, sl] * scale[:, wlb]
                          + dsk_ref[:, wl] * xs_b.astype(F32))
        y = jnp.concatenate(ys, axis=1)
        y = y * _silu(z_ref[0, pl.ds(base, CHUNK), :].astype(F32))
        ms = jnp.mean(y * y, axis=-1, keepdims=True)
        y_ref[0, pl.ds(base, CHUNK), :] = ((y * lax.rsqrt(ms + EPS)) * nw_ref[...]).astype(BF16)
        dec = chunk_dec(c)
        return tuple(dec[:, SSD_WIDTH + g * gw:SSD_WIDTH + (g + 1) * gw] * s_b[g] + kb_scr[c, g]
                     for g in range(SSD_GROUPS))

    lax.fori_loop(0, nch, bwd, tuple(s_b0))


def _ssd(xbc, z, dt, xbcc, dtc, conv_w8, conv_b, dt_bias, a_log, d_skip, norm_w):
    b, L, nconv = xbc.shape
    Lc = xbcc.shape[1]
    nch = L // CHUNK
    per_b = lambda n, w: pl.BlockSpec((1, n, w), lambda i: (i, 0, 0))
    return pl.pallas_call(
        _ssd_kernel,
        grid=(b,),
        in_specs=[
            per_b(L, nconv), per_b(L, SSD_WIDTH), per_b(L, LANES), per_b(Lc, nconv), per_b(Lc, LANES),
            _const_spec(conv_w8.shape), _const_spec(conv_b.shape), _const_spec(dt_bias.shape),
            _const_spec(a_log.shape), _const_spec(d_skip.shape), _const_spec(norm_w.shape),
        ],
        out_specs=per_b(L, SSD_WIDTH),
        out_shape=jax.ShapeDtypeStruct((b, L, SSD_WIDTH), BF16),
        scratch_shapes=[
            pltpu.VMEM((L + 2 * CONV_HALO, nconv), F32),
            pltpu.VMEM((Lc + 2 * CONV_HALO, nconv), F32),
            pltpu.VMEM((L, nconv), BF16),
            pltpu.VMEM((Lc, nconv), BF16),
            pltpu.VMEM((L, LANES), F32), pltpu.VMEM((L, LANES), F32),
            pltpu.VMEM((Lc, LANES), F32), pltpu.VMEM((Lc, LANES), F32),
            pltpu.VMEM((nch, SSD_GROUPS, SSD_STATE, 2 * LANES), F32),
            pltpu.VMEM((nch, SSD_GROUPS, SSD_STATE, 2 * LANES), F32),
            pltpu.VMEM((L, LANES), F32), pltpu.VMEM((L, LANES), F32),
            pltpu.VMEM((nch * SUBLANES, 2 * SSD_WIDTH), F32),
            pltpu.VMEM((nch * 2 * SSD_HEADS, CHUNK), F32), pltpu.VMEM((nch * 2 * SSD_HEADS, CHUNK), F32),
            pltpu.VMEM((nch * 2 * SSD_HEADS, CHUNK), F32),
        ],
        compiler_params=pltpu.CompilerParams(vmem_limit_bytes=VMEM_LIMIT_SSD),
        name="ssd",
    )(xbc, z, dt, xbcc, dtc, conv_w8, conv_b, dt_bias, a_log, d_skip, norm_w)


def _ret_kernel(q_ref, k_ref, v_ref, g_ref, kc_ref, vc_ref, df_ref, db_ref, gn_ref, y_ref, sf_scr):
    L = q_ref.shape[1]
    Lc = kc_ref.shape[1]
    nch = L // CHUNK
    dk = RET_DK
    row_i = lax.broadcasted_iota(jnp.int32, (CHUNK, dk), 0).astype(F32)
    col_i = lax.broadcasted_iota(jnp.int32, (CHUNK, dk), 1).astype(F32)
    rel = row_i - col_i
    crow = lax.broadcasted_iota(jnp.int32, (Lc, dk), 0).astype(F32)

    heads = []
    s_f0 = []
    s_b0 = []
    for h in range(RET_HEADS):
        cols = slice(h * dk, (h + 1) * dk)
        lg_f = -jnp.exp(df_ref[:, cols])
        lg_b = -jnp.exp(db_ref[:, cols])
        heads.append(dict(
            cols=cols,
            dmat=jnp.where(rel >= 0, jnp.exp(jnp.maximum(rel, 0.0) * lg_f), jnp.exp(jnp.maximum(-rel, 0.0) * lg_b)),
            dq_f=jnp.exp((row_i + 1.0) * lg_f),
            dq_b=jnp.exp((CHUNK - row_i) * lg_b),
            dk_f=jnp.exp((CHUNK - 1.0 - row_i) * lg_f),
            dk_b=jnp.exp(row_i * lg_b),
            dc_f=jnp.exp(CHUNK * lg_f),
            dc_b=jnp.exp(CHUNK * lg_b),
        ))
        kc = kc_ref[0, :, cols].astype(F32)
        vc = vc_ref[0, :, cols]
        s_f0.append(_dot_tn((kc * jnp.exp((Lc - 1.0 - crow) * lg_f)).astype(BF16), vc))
        s_b0.append(_dot_tn((kc * jnp.exp(crow * lg_b)).astype(BF16), vc))

    def fwd(c, s_f):
        base = pl.multiple_of(c * CHUNK, CHUNK)
        new = []
        for h, hd in enumerate(heads):
            sf_scr[c, h] = s_f[h]
            kk = k_ref[0, pl.ds(base, CHUNK), hd["cols"]].astype(F32)
            vv = v_ref[0, pl.ds(base, CHUNK), hd["cols"]]
            new.append(hd["dc_f"] * s_f[h] + _dot_tn((kk * hd["dk_f"]).astype(BF16), vv))
        return tuple(new)

    lax.fori_loop(0, nch, fwd, tuple(s_f0), unroll=RET_UNROLL)

    def bwd(i, s_bs):
        c = nch - 1 - i
        base = pl.multiple_of(c * CHUNK, CHUNK)
        new = []
        for h, hd in enumerate(heads):
            qq = q_ref[0, pl.ds(base, CHUNK), hd["cols"]]
            kk = k_ref[0, pl.ds(base, CHUNK), hd["cols"]]
            vv = v_ref[0, pl.ds(base, CHUNK), hd["cols"]]
            s_b = s_bs[h]
            scores = (_dot_nt(qq, kk) * hd["dmat"]).astype(BF16)
            y = (_dot(scores, vv)
                 + _dot(qq, sf_scr[c, h].astype(BF16)) * hd["dq_f"]
                 + _dot(qq, s_b.astype(BF16)) * hd["dq_b"])
            mu = jnp.mean(y, axis=-1, keepdims=True)
            yc = y - mu
            var = jnp.mean(yc * yc, axis=-1, keepdims=True)
            yn = (yc * lax.rsqrt(var + EPS)) * gn_ref[:, hd["cols"]]
            gate = _silu(g_ref[0, pl.ds(base, CHUNK), hd["cols"]].astype(F32))
            y_ref[0, pl.ds(base, CHUNK), hd["cols"]] = (yn * gate).astype(BF16)
            new.append(hd["dc_b"] * s_b + _dot_tn((kk.astype(F32) * hd["dk_b"]).astype(BF16), vv))
        return tuple(new)

    lax.fori_loop(0, nch, bwd, tuple(s_b0), unroll=RET_UNROLL)


def _retention(q, k, v, g, kc, vc, decay_f, decay_b, gn_w):
    b, L, w = q.shape
    Lc = kc.shape[1]
    nch = L // CHUNK
    per_b = lambda n: pl.BlockSpec((1, n, w), lambda i: (i, 0, 0))
    return pl.pallas_call(
        _ret_kernel,
        grid=(b,),
        in_specs=[per_b(L), per_b(L), per_b(L), per_b(L), per_b(Lc), per_b(Lc),
                  _const_spec((1, w)), _const_spec((1, w)), _const_spec((1, w))],
        out_specs=per_b(L),
        out_shape=jax.ShapeDtypeStruct((b, L, w), BF16),
        scratch_shapes=[
            pltpu.VMEM((nch, RET_HEADS, RET_DK, RET_DK), F32),
        ],
        compiler_params=pltpu.CompilerParams(vmem_limit_bytes=VMEM_LIMIT),
        name="retention",
    )(q, k, v, g, kc, vc, decay_f, decay_b, gn_w)


def _outproj_router_kernel(yr_ref, ys_ref, x_ref, g1_ref, sh2_ref, sc2_ref, npost_ref, npre_ref,
                           wor_ref, wos_ref, wr_ref, br_ref, tri_ref,
                           x1_ref, h2_ref, route_ref, slots_ref, seg_ref,
                           wcat):
    i = pl.program_id(0)

    @pl.when(i == 0)
    def _():
        wr = wr_ref[...]
        hi = wr.astype(BF16)
        wcat[:, :LANES] = hi
        wcat[:, LANES:] = (wr - hi.astype(F32)).astype(BF16)

    _route_tile(yr_ref, ys_ref, x_ref, g1_ref, sh2_ref, sc2_ref, npost_ref, npre_ref, wor_ref, wos_ref, br_ref,
                tri_ref, x1_ref, h2_ref, route_ref, slots_ref, seg_ref, wcat)


def _route_tile(yr_ref, ys_ref, x_ref, g1_ref, sh2_ref, sc2_ref, npost_ref, npre_ref, wor_ref, wos_ref, br_ref,
                tri_ref, x1_ref, h2_ref, route_ref, slots_ref, seg_ref, wcat):
    tm = x_ref.shape[0]
    rows = slice(0, tm)
    y = _dot(yr_ref[rows, :], wor_ref[...]) + _dot(ys_ref[rows, :], wos_ref[...])
    ms = jnp.mean(y * y, axis=-1, keepdims=True)
    x1 = x_ref[rows, :] + (y * lax.rsqrt(ms + EPS)) * (g1_ref[0] * npost_ref[...])
    x1_ref[rows, :] = x1
    h2 = _norm_mod(x1, npre_ref[...], sc2_ref[0], sh2_ref[0])
    h2_ref[rows, :] = h2.astype(BF16)

    h_hi = h2.astype(BF16)
    h_lo = (h2 - h_hi.astype(F32)).astype(BF16)
    both = _dot(h_hi, wcat[...])
    lg = both[:, :LANES] + both[:, LANES:] + _dot(h_lo, wcat[:, :LANES]) + br_ref[...]

    lane = lax.broadcasted_iota(jnp.int32, (tm, LANES), 1)
    lane_f = lane.astype(F32)
    is_grp = (lane >= N_EXPERTS) & (lane < N_EXPERTS + MOE_GROUPS)
    gl = jnp.where(is_grp, lg, NEG_BIG)
    mg = jnp.max(gl, axis=-1, keepdims=True)
    grp_lane = jnp.min(jnp.where(gl == mg, lane_f, 1e9), axis=-1, keepdims=True)
    p_g = 1.0 / jnp.sum(jnp.where(is_grp, jnp.exp(gl - mg), 0.0), axis=-1, keepdims=True)
    first = (grp_lane - N_EXPERTS) * EXPERTS_PER_GROUP
    in_grp = (lane_f >= first) & (lane_f < first + EXPERTS_PER_GROUP)
    el = jnp.where(in_grp, lg, NEG_BIG)
    t1 = jnp.max(el, axis=-1, keepdims=True)
    i1 = jnp.min(jnp.where(el == t1, lane_f, 1e9), axis=-1, keepdims=True)
    el2 = jnp.where(lane_f == i1, NEG_BIG, el)
    t2 = jnp.max(el2, axis=-1, keepdims=True)
    i2 = jnp.min(jnp.where(el2 == t2, lane_f, 1e9), axis=-1, keepdims=True)
    s = jnp.exp(t2 - t1)
    w1 = p_g / (1.0 + s)
    w2 = p_g * s / (1.0 + s)

    oh1 = (lane_f == i1)
    oh2 = (lane_f == i2)
    ohf = jnp.where(oh1 | oh2, 1.0, 0.0)
    before = _dot(tri_ref[...], ohf.astype(BF16))
    cnt = jnp.sum(ohf, axis=0, keepdims=True)
    seg = jnp.floor((cnt + (SEG_ALIGN - 1.0)) * (1.0 / SEG_ALIGN)) * SEG_ALIGN
    e_row = lax.broadcasted_iota(jnp.int32, (LANES, LANES), 0)
    e_col = lax.broadcasted_iota(jnp.int32, (LANES, LANES), 1)
    earlier = (e_row < e_col).astype(BF16)
    seg_off = _dot(jnp.broadcast_to(seg, (SUBLANES, LANES)).astype(BF16), earlier)[0:1]
    where_to = before + seg_off
    lpos1 = jnp.sum(jnp.where(oh1, where_to, 0.0), axis=-1, keepdims=True)
    lpos2 = jnp.sum(jnp.where(oh2, where_to, 0.0), axis=-1, keepdims=True)

    cols = [w1, w2, lpos1, lpos2]
    for wk in (w1, w2):
        hi = wk.astype(BF16).astype(F32)
        cols += [hi, wk - hi]
    cols.append(jnp.ones_like(w1))
    packed = jnp.zeros((tm, LANES), F32)
    for k, col in enumerate(cols):
        packed = jnp.where(lane == k, col, packed)
    route_ref[rows, :] = packed

    row = lax.broadcasted_iota(jnp.int32, (tm, LANES), 0)
    on_diag = (row % LANES) == lane
    per = tm // LANES
    for qi, col in enumerate((lpos1, lpos2)):
        picked = jnp.where(on_diag, col, 0.0)
        dense = jnp.sum(picked.reshape(per, LANES, LANES), axis=1).astype(jnp.int32)
        slots_ref[0, qi * per:(qi + 1) * per, :] = dense
    tbl_row = lax.broadcasted_iota(jnp.int32, (SUBLANES, LANES), 0)
    seg_ref[0] = jnp.where(tbl_row == 0, seg, jnp.where(tbl_row == 1, seg_off, 0.0)).astype(jnp.int32)


def _outproj_router(yr, ys, x2, mod3, npost, npre, wo_r, wo_s, w_router, b_router, seq_len):
    T, d = x2.shape
    tm = TM_OUT
    per_seq = seq_len // tm
    rw = yr.shape[1]
    tri = (jnp.arange(tm)[:, None] > jnp.arange(tm)[None, :]).astype(BF16)
    tok = lambda w: pl.BlockSpec((tm, w), lambda i: (i, 0))
    modv = lambda k: pl.BlockSpec((1, 1, d), lambda i: (i // per_seq, 0, k))
    tile3 = lambda r: pl.BlockSpec((1, r, LANES), lambda i: (i, 0, 0))
    slot_rows = TOP_K * (tm // LANES)
    return pl.pallas_call(
        _outproj_router_kernel,
        grid=(T // tm,),
        in_specs=[
            tok(rw), tok(rw), tok(d), modv(2), modv(3), modv(4),
            _const_spec((1, d)), _const_spec((1, d)),
            _const_spec(wo_r.shape), _const_spec(wo_s.shape), _const_spec(w_router.shape), _const_spec((1, LANES)),
            _const_spec((tm, tm)),
        ],
        out_specs=[tok(d), tok(d), tok(LANES), tile3(slot_rows), tile3(SUBLANES)],
        out_shape=[jax.ShapeDtypeStruct((T, d), F32), jax.ShapeDtypeStruct((T, d), BF16),
                   jax.ShapeDtypeStruct((T, LANES), F32),
                   jax.ShapeDtypeStruct((T // tm, slot_rows, LANES), jnp.int32),
                   jax.ShapeDtypeStruct((T // tm, SUBLANES, LANES), jnp.int32)],
        scratch_shapes=[pltpu.VMEM((d, 2 * LANES), BF16)],
        compiler_params=pltpu.CompilerParams(dimension_semantics=("arbitrary",),
                                             vmem_limit_bytes=VMEM_LIMIT),
        name="outproj_router",
    )(yr, ys, x2, mod3, mod3, mod3, npost, npre, wo_r, wo_s, w_router, b_router, tri)


def _expert_kernel(be_ref, first_ref, slot_ref, next_ref, nused_ref, xs_ref, wg_hbm, wu_hbm, wd_hbm, y_ref,
                   wg_f, wu_f, wd_f, wg_b, wu_b, wd_b, sem):
    def fetch(e, s):
        return [pltpu.make_async_copy(src.at[e], dst.at[s], sem.at[s, k])
                for k, (src, dst) in enumerate(((wg_hbm, wg_f), (wu_hbm, wu_f), (wd_hbm, wd_f)))]

    @pl.when(pl.program_id(0) == 0)
    def _():
        for c in fetch(be_ref[0], slot_ref[0]):
            c.start()

    for sb in range(EXPERT_BLOCKS_PER_STEP):
        _expert_block(pl.program_id(0) * EXPERT_BLOCKS_PER_STEP + sb, slice(sb * MB_EXPERT, (sb + 1) * MB_EXPERT),
                      fetch, be_ref, first_ref, slot_ref, next_ref, nused_ref, xs_ref, y_ref,
                      wg_f, wu_f, wd_f, wg_b, wu_b, wd_b)


def _expert_block(i, rows, fetch, be_ref, first_ref, slot_ref, next_ref, nused_ref, xs_ref, y_ref,
                  wg_f, wu_f, wd_f, wg_b, wu_b, wd_b):
    @pl.when(first_ref[i] == 1)
    def _():
        s = slot_ref[i]

        @pl.when(next_ref[i] >= 0)
        def _():
            for c in fetch(next_ref[i], 1 - s):
                c.start()

        for c in fetch(be_ref[i], s):
            c.wait()
        wg_b[...] = wg_f[s].astype(BF16)
        wu_b[...] = wu_f[s].astype(BF16)
        wd_b[...] = wd_f[s].astype(BF16)

    @pl.when(i < nused_ref[0])
    def _():
        d = y_ref.shape[1]
        side = xs_ref[rows, d:d + LANES].astype(F32)
        second = side[:, 8:9] == 2.0
        unscale = jnp.where(second, 0.5, 1.0)
        weight = jnp.where(second, side[:, 6:7] + side[:, 7:8], side[:, 4:5] + side[:, 5:6]) * unscale
        xb = xs_ref[rows, 0:d] * unscale.astype(BF16)
        hid = (_silu(_dot(xb, wg_b[...])) * _dot(xb, wu_b[...])).astype(BF16)
        y_ref[rows, :] = (_dot(hid, wd_b[...]) * weight).astype(BF16)

    @pl.when(i >= nused_ref[0])
    def _():
        y_ref[rows, :] = jnp.zeros((rows.stop - rows.start, y_ref.shape[1]), y_ref.dtype)


def _expert_plan(padded, pad_end, n_blocks, mb):
    n_used = (pad_end[-1:] // mb).astype(jnp.int32)
    blk_start = jnp.arange(n_blocks, dtype=jnp.int32) * mb
    experts = jnp.arange(N_EXPERTS, dtype=jnp.int32)
    blk_expert = jnp.minimum(jnp.sum((pad_end[None, :] <= blk_start[:, None]).astype(jnp.int32), axis=1),
                             N_EXPERTS - 1)
    prev = jnp.concatenate([jnp.full((1,), -1, jnp.int32), blk_expert[:-1]])
    first = ((blk_start < pad_end[-1]) & (blk_expert != prev)).astype(jnp.int32)
    slot = (jnp.cumsum(first) - 1) % 2
    later = jnp.where((padded > 0)[None, :] & (experts[None, :] > experts[:, None]), experts[None, :], N_EXPERTS)
    next_nonempty = jnp.min(later, axis=1)
    next_nonempty = jnp.where(next_nonempty == N_EXPERTS, -1, next_nonempty)
    nxt = jnp.sum(jnp.where(blk_expert[:, None] == experts[None, :], next_nonempty[None, :], 0), axis=1)
    return [a.astype(jnp.int32) for a in (blk_expert, first, slot, nxt, n_used)]


def _experts(plan, xs, w_gate, w_up, w_down):
    cap, dp = xs.shape
    n_exp, d, de = w_gate.shape
    step_rows = MB_EXPERT * EXPERT_BLOCKS_PER_STEP
    assert dp == d + LANES and cap % step_rows == 0
    grid_spec = pltpu.PrefetchScalarGridSpec(
        num_scalar_prefetch=len(plan),
        grid=(cap // step_rows,),
        in_specs=[
            pl.BlockSpec((step_rows, dp), lambda i, be, fi, sl, nx, nu: (
                jnp.minimum(i, (nu[0] - 1) // EXPERT_BLOCKS_PER_STEP), 0)),
            pl.BlockSpec(memory_space=pl.ANY), pl.BlockSpec(memory_space=pl.ANY), pl.BlockSpec(memory_space=pl.ANY),
        ],
        out_specs=pl.BlockSpec((step_rows, d), lambda i, be, fi, sl, nx, nu: (i, 0)),
        scratch_shapes=[pltpu.VMEM((2, d, de), F32), pltpu.VMEM((2, d, de), F32), pltpu.VMEM((2, de, d), F32),
                        pltpu.VMEM((d, de), BF16), pltpu.VMEM((d, de), BF16), pltpu.VMEM((de, d), BF16),
                        pltpu.SemaphoreType.DMA((2, 3))],
    )
    return pl.pallas_call(
        _expert_kernel,
        grid_spec=grid_spec,
        out_shape=jax.ShapeDtypeStruct((cap, d), BF16),
        compiler_params=pltpu.CompilerParams(dimension_semantics=("arbitrary",),
                                             vmem_limit_bytes=VMEM_LIMIT),
        name="experts",
    )(*plan, xs, w_gate, w_up, w_down)


def _segment_pieces(tile, segrow_ref, seglen_ref, segoff_ref, act):
    def per_expert(e, carry):
        idx = tile * N_EXPERTS + e
        g0 = segrow_ref[idx]
        l0 = segoff_ref[idx]

        def piece(j, c2):
            act(pl.multiple_of(l0 + j * SEG_ALIGN, SEG_ALIGN), pl.multiple_of(g0 + j * SEG_ALIGN, SEG_ALIGN))
            return c2

        lax.fori_loop(0, seglen_ref[idx], piece, 0)
        return carry

    lax.fori_loop(0, N_EXPERTS, per_expert, 0)


def _wait_rows(total, row_copy):
    size = SEG_ALIGN
    while size <= _stage_rows(max(TD_DISPATCH, TF_COMBINE)):
        @pl.when((total & size) != 0)
        def _(size=size):
            row_copy(size).wait()
        size *= 2


def _stage_rows(tile_tokens):
    return TOP_K * tile_tokens + N_EXPERTS * SEG_ALIGN


def _dispatch_seg_kernel(pad_end_ref, zero_from_ref, segrow_ref, seglen_ref, segoff_ref, tilerows_ref,
                         lpos_ref, route_ref, h_ref, xs_hbm, zbuf, stage, sem, zsem):
    i = pl.program_id(0)
    nt = pl.num_programs(0)
    td, d = h_ref.shape
    sr = stage.shape[1]
    per = td // LANES

    def zero_fills(act):
        def fill(row, n):
            act(pltpu.make_async_copy(zbuf.at[pl.ds(0, n)], xs_hbm.at[pl.ds(pl.multiple_of(row, SEG_ALIGN), n)], zsem))

        def region(e, carry):
            row = zero_from_ref[e]
            left = pad_end_ref[e] - row
            size = ZERO_PIECE
            while size >= SEG_ALIGN:
                take = (left & size) != 0

                @pl.when(take)
                def _(row=row, size=size):
                    fill(row, size)

                row = row + jnp.where(take, size, 0)
                size //= 2
            return carry

        def tail(p, carry):
            @pl.when(p * ZERO_PIECE >= pad_end_ref[N_EXPERTS - 1])
            def _():
                fill(p * ZERO_PIECE, ZERO_PIECE)
            return carry

        lax.fori_loop(0, N_EXPERTS, region, 0)
        lax.fori_loop(0, xs_hbm.shape[0] // ZERO_PIECE, tail, 0)

    @pl.when(i == 0)
    def _():
        zbuf[...] = jnp.zeros_like(zbuf)
        zero_fills(lambda c: c.start())

    def shipped(tile):
        _wait_rows(tilerows_ref[tile], lambda n: pltpu.make_async_copy(
            stage.at[tile % 2, pl.ds(0, n)], xs_hbm.at[pl.ds(0, n)], sem.at[tile % 2]))

    @pl.when(i >= 2)
    def _():
        shipped(i - 2)

    srow = lax.broadcasted_iota(jnp.int32, (sr, LANES), 0)
    place = jnp.concatenate(
        [jnp.where(srow == lpos_ref[0, cb:cb + 1, :], 1.0,
                   jnp.where(srow == lpos_ref[0, per + cb:per + cb + 1, :], 2.0, 0.0)) for cb in range(per)],
        axis=1).astype(BF16)
    slot = i % 2
    stage[slot] = _dot(place, jnp.concatenate([h_ref[...], route_ref[...].astype(BF16)],
                                              axis=1)).astype(BF16)
    _segment_pieces(i, segrow_ref, seglen_ref, segoff_ref, lambda lr, gr: pltpu.make_async_copy(
        stage.at[slot, pl.ds(lr, SEG_ALIGN)], xs_hbm.at[pl.ds(gr, SEG_ALIGN)], sem.at[slot]).start())

    @pl.when(i == nt - 1)
    def _():
        @pl.when(i >= 1)
        def _():
            shipped(i - 1)
        shipped(i)
        zero_fills(lambda c: c.wait())


def _dispatch_seg(plan, lpos, route, h2, cap):
    T, d = h2.shape
    td = TD_DISPATCH
    nt = T // td
    width = d + LANES
    sr = _stage_rows(td)
    n_pre = len(plan)
    grid_spec = pltpu.PrefetchScalarGridSpec(
        num_scalar_prefetch=n_pre,
        grid=(nt,),
        in_specs=[
            pl.BlockSpec((1, lpos.shape[1], LANES), lambda i, *_: (i, 0, 0)),
            pl.BlockSpec((td, LANES), lambda i, *_: (i, 0)),
            pl.BlockSpec((td, d), lambda i, *_: (i, 0)),
        ],
        out_specs=pl.BlockSpec(memory_space=pl.ANY),
        scratch_shapes=[pltpu.VMEM((ZERO_PIECE, width), BF16), pltpu.VMEM((2, sr, width), BF16),
                        pltpu.SemaphoreType.DMA((2,)), pltpu.SemaphoreType.DMA(())],
    )
    return pl.pallas_call(
        _dispatch_seg_kernel,
        grid_spec=grid_spec,
        out_shape=jax.ShapeDtypeStruct((cap, width), BF16),
        compiler_params=pltpu.CompilerParams(dimension_semantics=("arbitrary",), vmem_limit_bytes=VMEM_LIMIT),
        name="dispatch",
    )(*plan, lpos, route, h2)


def _combine_seg_kernel(segrow_ref, seglen_ref, segoff_ref, tilerows_ref, route_ref, x1_ref, g2_ref, nw_ref,
                        yb_hbm, o_ref, stage, sem):
    i = pl.program_id(0)
    nt = pl.num_programs(0)
    tf = x1_ref.shape[0]
    sr = stage.shape[1]

    def fetch(tile, slot):
        _segment_pieces(tile, segrow_ref, seglen_ref, segoff_ref, lambda lr, gr: pltpu.make_async_copy(
            yb_hbm.at[pl.ds(gr, SEG_ALIGN)], stage.at[slot, pl.ds(lr, SEG_ALIGN)], sem.at[slot]).start())

    @pl.when(i == 0)
    def _():
        stage[...] = jnp.zeros_like(stage)
        fetch(i, 0)

    for slot in range(2):
        @pl.when((i + 1 < nt) & (i % 2 != slot))
        def _(slot=slot):
            fetch(i + 1, slot)

    _wait_rows(tilerows_ref[i], lambda n: pltpu.make_async_copy(
        yb_hbm.at[pl.ds(0, n)], stage.at[i % 2, pl.ds(0, n)], sem.at[i % 2]))

    local_row = route_ref[:, 2:2 + TOP_K].astype(jnp.int32)
    scol = lax.broadcasted_iota(jnp.int32, (tf, sr), 1)
    pick = jnp.where((scol == local_row[:, 0:1]) | (scol == local_row[:, 1:2]), 1.0, 0.0).astype(BF16)
    out = _dot(pick, stage[i % 2])
    ms = jnp.mean(out * out, axis=-1, keepdims=True)
    o_ref[...] = x1_ref[...] + g2_ref[0] * ((out * lax.rsqrt(ms + EPS)) * nw_ref[...])


def _combine_seg(seg_plan, route, x1, mod3, nw, yb, seq_len):
    T, d = x1.shape
    tf = TF_COMBINE
    nt = T // tf
    per_seq = seq_len // tf
    grid_spec = pltpu.PrefetchScalarGridSpec(
        num_scalar_prefetch=len(seg_plan),
        grid=(nt,),
        in_specs=[
            pl.BlockSpec((tf, LANES), lambda i, *_: (i, 0)),
            pl.BlockSpec((tf, d), lambda i, *_: (i, 0)),
            pl.BlockSpec((1, 1, d), lambda i, *_: (i // per_seq, 0, 5)),
            pl.BlockSpec((1, d), lambda i, *_: (0, 0)),
            pl.BlockSpec(memory_space=pl.ANY),
        ],
        out_specs=pl.BlockSpec((tf, d), lambda i, *_: (i, 0)),
        scratch_shapes=[pltpu.VMEM((2, _stage_rows(tf), d), yb.dtype), pltpu.SemaphoreType.DMA((2,))],
    )
    return pl.pallas_call(
        _combine_seg_kernel,
        grid_spec=grid_spec,
        out_shape=jax.ShapeDtypeStruct((T, d), F32),
        compiler_params=pltpu.CompilerParams(dimension_semantics=("arbitrary",), vmem_limit_bytes=VMEM_LIMIT),
        name="combine",
    )(*seg_plan, route, x1, mod3, nw, yb)


def _rope_tables(L, n_heads):
    quarter = RET_DK // 4
    freqs = ROPE_BASE ** (-jnp.arange(quarter, dtype=F32) / quarter)
    t = jnp.arange(L)
    ang_r = (t // GRID_W).astype(F32)[:, None] * freqs
    ang_c = (t % GRID_W).astype(F32)[:, None] * freqs
    cos = jnp.concatenate([jnp.cos(ang_r)] * 2 + [jnp.cos(ang_c)] * 2, axis=-1)
    sin = jnp.concatenate([-jnp.sin(ang_r), jnp.sin(ang_r), -jnp.sin(ang_c), jnp.sin(ang_c)], axis=-1)
    return jnp.tile(cos, (1, n_heads)), jnp.tile(sin, (1, n_heads))


def _lane_pad(v, width=LANES):
    return jnp.pad(v, [(0, 0)] * (v.ndim - 1) + [(0, width - v.shape[-1])])


def kernel(x, c, ctx, c_ctx, w_mod, b_mod, norm_pre_mix, norm_post_mix, norm_pre_ffn, norm_post_ffn, w_in, w_out, ret_decay_f, ret_decay_b, ret_gn_w, ssd_conv_w, ssd_conv_b, ssd_dt_bias_f, ssd_dt_bias_b, ssd_a_log_f, ssd_a_log_b, ssd_d, ssd_norm_w, moe_w_rg, moe_b_rg, moe_w_re, moe_b_re, moe_w_gate, moe_w_up, moe_w_down):
    b, L, d = x.shape
    assert w_mod.shape[0] == 1, "single layer: context outputs are never needed"
    assert TM_OUT == TD_DISPATCH == TF_COMBINE, "router, dispatch and combine share one slot-row layout"
    rw = RET_HEADS * RET_DK
    nconv = SSD_WIDTH + 2 * SSD_GROUPS * SSD_STATE
    T = b * L

    mod_rows = -(-(b + 1) // SUBLANES) * SUBLANES
    c_all = jnp.zeros((mod_rows, d), F32).at[:b].set(c).at[b].set(c_ctx)
    mod3 = _modulation(c_all, w_mod[0], b_mod[0]).reshape(mod_rows, 1, 6 * d)

    wi = w_in[0]
    o = 0
    wq = wi[:, o:o + rw]; o += rw
    wk = wi[:, o:o + rw]; o += rw
    wv = wi[:, o:o + rw]; o += rw
    wg = wi[:, o:o + rw]; o += rw
    wz = wi[:, o:o + SSD_WIDTH]; o += SSD_WIDTH
    wxbc = wi[:, o:o + nconv].astype(BF16); o += nconv
    wdt = _lane_pad(wi[:, o:o + 2 * SSD_HEADS]).astype(BF16)
    wqk = jnp.concatenate([wq, wk], axis=1).astype(BF16)
    wvgz = jnp.concatenate([wv, wg, wz], axis=1).astype(BF16)
    cos_t, sin_t = _rope_tables(L, RET_HEADS)
    nw1 = norm_pre_mix[0].reshape(1, d)

    q, k, v, g, z, xbc, dt = _inproj(x, mod3, nw1, wqk, wvgz, wxbc, wdt, cos_t, sin_t)
    kc, vc, xbcc, dtc = _inproj_ctx(ctx, mod3, b, nw1, wk.astype(BF16), wv.astype(BF16), wxbc, wdt)

    conv_w8 = jnp.pad(ssd_conv_w[0], ((0, SUBLANES - SSD_CONV), (0, 0)))
    dt_bias = _lane_pad(jnp.concatenate([ssd_dt_bias_f[0], ssd_dt_bias_b[0]])[None, :])
    a_log = _lane_pad(jnp.concatenate([ssd_a_log_f[0], ssd_a_log_b[0]])[None, :])
    d_skip = jnp.repeat(ssd_d[0], SSD_HEADDIM)[None, :]
    ys = _ssd(xbc, z, dt, xbcc, dtc, conv_w8, ssd_conv_b[0][None, :], dt_bias, a_log, d_skip,
              ssd_norm_w[0][None, :])

    yr = _retention(q, k, v, g, kc, vc,
                    jnp.repeat(ret_decay_f[0], RET_DK)[None, :], jnp.repeat(ret_decay_b[0], RET_DK)[None, :],
                    ret_gn_w[0][None, :])

    wo = w_out[0].astype(BF16)
    w_router = _lane_pad(jnp.concatenate(
        [jnp.transpose(moe_w_re[0], (1, 0, 2)).reshape(d, N_EXPERTS), moe_w_rg[0]], axis=1))
    b_router = _lane_pad(jnp.concatenate([moe_b_re[0].reshape(-1), moe_b_rg[0]])[None, :])
    x1, h2, route, lpos, seg = _outproj_router(
        yr.reshape(T, rw), ys.reshape(T, SSD_WIDTH), x.reshape(T, d), mod3,
        norm_post_mix[0][None, :], norm_pre_ffn[0][None, :], wo[:rw], wo[rw:], w_router, b_router, L)

    mb = MB_EXPERT
    nt = T // TM_OUT
    n_blocks = -(-(T * TOP_K + nt * N_EXPERTS * (SEG_ALIGN - 1) + N_EXPERTS * (mb - 1)) // mb)
    n_blocks = -(-n_blocks // EXPERT_BLOCKS_PER_STEP) * EXPERT_BLOCKS_PER_STEP
    seg_len = seg[:, 0, :N_EXPERTS]
    seg_off = seg[:, 1, :N_EXPERTS]
    used = jnp.sum(seg_len, axis=0)
    padded = (used + mb - 1) // mb * mb
    pad_end = jnp.cumsum(padded)
    pad_start = pad_end - padded
    seg_row = pad_start[None, :] + jnp.cumsum(seg_len, axis=0) - seg_len
    seg_plan = [a.reshape(-1).astype(jnp.int32)
                for a in (seg_row, seg_len // SEG_ALIGN, seg_off, jnp.sum(seg_len, axis=1))]
    zero_from = (pad_start + used).astype(jnp.int32)

    xs = _dispatch_seg([pad_end.astype(jnp.int32), zero_from] + seg_plan, lpos, route, h2, n_blocks * mb)
    yb = _experts(_expert_plan(padded, pad_end, n_blocks, mb), xs, moe_w_gate[0], moe_w_up[0], moe_w_down[0])
    out = _combine_seg(seg_plan, route, x1, mod3, norm_post_ffn[0][None, :], yb, L)
    return out.reshape(b, L, d)
```

```python
import jax
import jax.numpy as jnp
from jax import lax
from jax.experimental import pallas as pl
from jax.experimental.pallas import tpu as pltpu

F32 = jnp.float32
BF16 = jnp.bfloat16

LANES = 128
SUBLANES = 8
BF16_TILE_ROWS = 16
V7X_VMEM_BYTES = 64 * 1024 * 1024
VMEM_LIMIT = V7X_VMEM_BYTES * 3 // 4
VMEM_LIMIT_SSD = V7X_VMEM_BYTES * 7 // 8

EPS = 1e-6
CHUNK = 128
GRID_W = 64
RET_HEADS = 4
RET_DK = 128
ROPE_BASE = 10000.0
SSD_HEADS = 8
SSD_HEADDIM = 64
SSD_GROUPS = 2
SSD_STATE = 128
SSD_WIDTH = SSD_HEADS * SSD_HEADDIM
SSD_CONV = 5
SSD_PAIRS = SSD_WIDTH // LANES
MOE_GROUPS = 4
EXPERTS_PER_GROUP = 8
N_EXPERTS = MOE_GROUPS * EXPERTS_PER_GROUP
TOP_K = 2
CONV_HALO = SUBLANES

TM_PROJ = 512
TM_OUT = 512
TD_DISPATCH = TM_OUT
MB_EXPERT = 512
EXPERT_BLOCKS_PER_STEP = 4
ZERO_PIECE = MB_EXPERT // 2
TF_COMBINE = TM_OUT
SEG_ALIGN = BF16_TILE_ROWS
COMMON_STAGE_ROWS = TOP_K * TM_OUT + N_EXPERTS * 10
RET_UNROLL = 2
NEG_BIG = -1e30


def _silu(v):
    return v * jax.nn.sigmoid(v)


def _dot(a, b):
    return jnp.dot(a, b, preferred_element_type=F32)


def _dot_tn(a, b):
    return lax.dot_general(a, b, (((0,), (0,)), ((), ())), preferred_element_type=F32)


def _dot_nt(a, b):
    return lax.dot_general(a, b, (((1,), (1,)), ((), ())), preferred_element_type=F32)


def _mod_kernel(c_ref, w_ref, b_ref, o_ref):
    a = _silu(c_ref[...])
    w = w_ref[...]
    a_hi = a.astype(BF16)
    a_lo = (a - a_hi.astype(F32)).astype(BF16)
    w_hi = w.astype(BF16)
    w_lo = (w - w_hi.astype(F32)).astype(BF16)
    o_ref[...] = _dot(a_hi, w_hi) + _dot(a_lo, w_hi) + _dot(a_hi, w_lo) + b_ref[...]


def _modulation(c_all, w_mod, b_mod):
    rows, d = c_all.shape
    n = w_mod.shape[1]
    return pl.pallas_call(
        _mod_kernel,
        grid=(n // d,),
        in_specs=[
            pl.BlockSpec((rows, d), lambda j: (0, 0)),
            pl.BlockSpec((d, d), lambda j: (0, j)),
            pl.BlockSpec((1, d), lambda j: (0, j)),
        ],
        out_specs=pl.BlockSpec((rows, d), lambda j: (0, j)),
        out_shape=jax.ShapeDtypeStruct((rows, n), F32),
        name="modulation",
    )(c_all, w_mod, b_mod.reshape(1, n))


def _norm_mod(x, nw, sc, sh):
    ms = jnp.mean(x * x, axis=-1, keepdims=True)
    return (x * lax.rsqrt(ms + EPS)) * (nw * (1.0 + sc)) + sh


def _rope(t, cos, sin_signed, first_half):
    width = t.shape[-1]
    quarter = RET_DK // 4
    swapped = jnp.where(first_half, pltpu.roll(t, width - quarter, 1), pltpu.roll(t, quarter, 1))
    return t * cos + swapped * sin_signed


def _inproj_kernel(x_ref, sh_ref, sc_ref, nw_ref, wqk_ref, wvgz_ref, wxbc_ref, wdt_ref, cos_ref, sin_ref,
                   q_ref, k_ref, v_ref, g_ref, z_ref, xbc_ref, dt_ref):
    hb = _norm_mod(x_ref[0], nw_ref[...], sc_ref[0], sh_ref[0]).astype(BF16)
    rw = q_ref.shape[-1]
    qk = _dot(hb, wqk_ref[...])
    cos = cos_ref[...]
    sin = sin_ref[...]
    lane = lax.broadcasted_iota(jnp.int32, cos.shape, 1)
    first_half = (lane % (RET_DK // 2)) < (RET_DK // 4)
    q_ref[0] = _rope(qk[:, :rw], cos, sin, first_half).astype(BF16)
    k_ref[0] = (_rope(qk[:, rw:], cos, sin, first_half) * (RET_DK ** -0.5)).astype(BF16)
    vgz = _dot(hb, wvgz_ref[...])
    v_ref[0] = vgz[:, :rw].astype(BF16)
    g_ref[0] = vgz[:, rw:2 * rw].astype(BF16)
    z_ref[0] = vgz[:, 2 * rw:].astype(BF16)
    xbc_ref[0] = _dot(hb, wxbc_ref[...]).astype(BF16)
    dt_ref[0] = _dot(hb, wdt_ref[...])


def _inproj_ctx_kernel(x_ref, sh_ref, sc_ref, nw_ref, wk_ref, wv_ref, wxbc_ref, wdt_ref,
                       k_ref, v_ref, xbc_ref, dt_ref):
    hb = _norm_mod(x_ref[0], nw_ref[...], sc_ref[0], sh_ref[0]).astype(BF16)
    k_ref[0] = (_dot(hb, wk_ref[...]) * (RET_DK ** -0.5)).astype(BF16)
    v_ref[0] = _dot(hb, wv_ref[...]).astype(BF16)
    xbc_ref[0] = _dot(hb, wxbc_ref[...]).astype(BF16)
    dt_ref[0] = _dot(hb, wdt_ref[...])


def _const_spec(shape):
    nd = len(shape)
    return pl.BlockSpec(shape, lambda *_: (0,) * nd)


def _inproj(x, mod3, nw, wqk, wvgz, wxbc, wdt, cos_t, sin_t):
    b, L, d = x.shape
    tm = min(TM_PROJ, L)
    rw = wqk.shape[1] // 2
    tok = lambda w: pl.BlockSpec((1, tm, w), lambda i, j: (i, j, 0))
    out_bf = lambda w: jax.ShapeDtypeStruct((b, L, w), BF16)
    return pl.pallas_call(
        _inproj_kernel,
        grid=(b, L // tm),
        in_specs=[
            tok(d),
            pl.BlockSpec((1, 1, d), lambda i, j: (i, 0, 0)),
            pl.BlockSpec((1, 1, d), lambda i, j: (i, 0, 1)),
            _const_spec((1, d)),
            _const_spec(wqk.shape), _const_spec(wvgz.shape), _const_spec(wxbc.shape), _const_spec(wdt.shape),
            pl.BlockSpec((tm, rw), lambda i, j: (j, 0)),
            pl.BlockSpec((tm, rw), lambda i, j: (j, 0)),
        ],
        out_specs=[tok(rw), tok(rw), tok(rw), tok(rw), tok(rw), tok(wxbc.shape[1]), tok(LANES)],
        out_shape=[out_bf(rw), out_bf(rw), out_bf(rw), out_bf(rw), out_bf(rw), out_bf(wxbc.shape[1]),
                   jax.ShapeDtypeStruct((b, L, LANES), F32)],
        compiler_params=pltpu.CompilerParams(vmem_limit_bytes=VMEM_LIMIT),
        name="inproj",
    )(x, mod3, mod3, nw, wqk, wvgz, wxbc, wdt, cos_t, sin_t)


def _inproj_ctx(ctx, mod3, ctx_row, nw, wk, wv, wxbc, wdt):
    b, L, d = ctx.shape
    tm = min(TM_PROJ, L)
    rw = wk.shape[1]
    tok = lambda w: pl.BlockSpec((1, tm, w), lambda i, j: (i, j, 0))
    out_bf = lambda w: jax.ShapeDtypeStruct((b, L, w), BF16)
    return pl.pallas_call(
        _inproj_ctx_kernel,
        grid=(b, L // tm),
        in_specs=[
            tok(d),
            pl.BlockSpec((1, 1, d), lambda i, j: (ctx_row, 0, 0)),
            pl.BlockSpec((1, 1, d), lambda i, j: (ctx_row, 0, 1)),
            _const_spec((1, d)),
            _const_spec(wk.shape), _const_spec(wv.shape), _const_spec(wxbc.shape), _const_spec(wdt.shape),
        ],
        out_specs=[tok(rw), tok(rw), tok(wxbc.shape[1]), tok(LANES)],
        out_shape=[out_bf(rw), out_bf(rw), out_bf(wxbc.shape[1]), jax.ShapeDtypeStruct((b, L, LANES), F32)],
        compiler_params=pltpu.CompilerParams(vmem_limit_bytes=VMEM_LIMIT),
        name="inproj_ctx",
    )(ctx, mod3, mod3, nw, wk, wv, wxbc, wdt)


def _ssd_kernel(xbc_ref, z_ref, dt_ref, xbcc_ref, dtc_ref, cw_ref, cb_ref, dtb_ref, alog_ref, dsk_ref, nw_ref,
                y_ref,
                xpad, xpadc, u, uc, dtv, dav, dtcv, dacv, sf_scr, kb_scr, acum, ecum, dec_scr,
                arow_scr, erow_scr, dtrow_scr):
    L = xbc_ref.shape[1]
    Lc = xbcc_ref.shape[1]
    nch = L // CHUNK
    nchc = Lc // CHUNK
    win = CHUNK + 2 * CONV_HALO
    nconv = xbc_ref.shape[2]
    nh = SSD_HEADS

    def conv_pass(src_ref, pad_ref, dst_ref, n_chunks, length):
        zeros = jnp.zeros((CONV_HALO, nconv), F32)
        pad_ref[0:CONV_HALO, :] = zeros
        pad_ref[CONV_HALO + length:2 * CONV_HALO + length, :] = zeros
        pad_ref[CONV_HALO:CONV_HALO + length, :] = src_ref[0].astype(F32)

        def chunk(c, carry):
            base = pl.multiple_of(c * CHUNK, CHUNK)
            for cb_i in range(nconv // LANES):
                cols = slice(cb_i * LANES, (cb_i + 1) * LANES)
                w = pad_ref[pl.ds(base, win), cols]
                acc = cb_ref[:, cols] + w[CONV_HALO:CONV_HALO + CHUNK] * cw_ref[SSD_CONV // 2:SSD_CONV // 2 + 1, cols]
                for j in range(SSD_CONV):
                    if j == SSD_CONV // 2:
                        continue
                    shifted = pltpu.roll(w, (SSD_CONV // 2 - j) % win, 0)
                    acc = acc + shifted[CONV_HALO:CONV_HALO + CHUNK] * cw_ref[j:j + 1, cols]
                dst_ref[pl.ds(base, CHUNK), cols] = _silu(acc).astype(BF16)
            return carry

        lax.fori_loop(0, n_chunks, chunk, 0)

    conv_pass(xbcc_ref, xpadc, uc, nchc, Lc)
    conv_pass(xbc_ref, xpad, u, nch, L)

    a_neg = -jnp.exp(alog_ref[...])
    dtv[...] = jax.nn.softplus(dt_ref[0] + dtb_ref[...])
    dav[...] = dtv[...] * a_neg
    dtcv[...] = jax.nn.softplus(dtc_ref[0] + dtb_ref[...])
    dacv[...] = dtcv[...] * a_neg

    row_i = lax.broadcasted_iota(jnp.int32, (CHUNK, CHUNK), 0)
    col_i = lax.broadcasted_iota(jnp.int32, (CHUNK, CHUNK), 1)
    causal = col_i <= row_i
    lo_half = col_i < SSD_HEADDIM
    fwd_lane = col_i < nh
    head_of = lax.broadcasted_iota(jnp.int32, (CHUNK, SSD_WIDTH), 1) // SSD_HEADDIM
    src_col = lax.broadcasted_iota(jnp.int32, (CHUNK, SSD_WIDTH), 0)
    exp_f = (head_of == src_col).astype(BF16)
    exp_b = (head_of == src_col - nh).astype(BF16)
    exp_fb = jnp.concatenate([exp_f, exp_b], axis=1)

    def split3(v):
        hi = v.astype(BF16)
        r1 = v - hi.astype(F32)
        mid = r1.astype(BF16)
        return hi, mid, (r1 - mid.astype(F32)).astype(BF16)

    def times_onehot(v, m, passes=3):
        parts = split3(v)[:passes]
        acc = _dot(parts[0], m)
        for part in parts[1:]:
            acc = acc + _dot(part, m)
        return acc

    def colb(mat, r):
        return jnp.broadcast_to(mat[:, r:r + 1], (CHUNK, CHUNK))

    def pair_sel(a, b_):
        return jnp.where(lo_half, a, b_)

    gw = 2 * LANES

    def chunk_terms(u_ref, dt_s, da_s, base):
        dt = dt_s[pl.ds(base, CHUNK), :]
        da = da_s[pl.ds(base, CHUNK), :]
        acol = da
        for step in (1, 2, 4, 8, 16, 32, 64):
            acol = acol + jnp.where(row_i >= step, pltpu.roll(acol, step, 0), 0.0)
        ecol = acol - da
        last = acol[CHUNK - 1:CHUNK, :]
        wgt = jnp.where(fwd_lane, jnp.exp(last - acol), jnp.exp(ecol)) * dt
        scale = jnp.where(fwd_lane, jnp.exp(acol), jnp.exp(last - ecol))
        wide = times_onehot(jnp.concatenate([wgt, scale], axis=0), exp_fb, passes=1)
        dec = times_onehot(jnp.broadcast_to(jnp.exp(last), (SUBLANES, LANES)), exp_fb)[0:1]
        xs = u_ref[pl.ds(base, CHUNK), 0:SSD_WIDTH].astype(F32)
        kmats = []
        for g in range(SSD_GROUPS):
            xw = jnp.concatenate([xs[:, g * gw:(g + 1) * gw] * wide[:CHUNK, g * gw:(g + 1) * gw],
                                  xs[:, g * gw:(g + 1) * gw] * wide[:CHUNK, SSD_WIDTH + g * gw:SSD_WIDTH + (g + 1) * gw]],
                                 axis=1).astype(BF16)
            bm = u_ref[pl.ds(base, CHUNK), SSD_WIDTH + g * SSD_STATE:SSD_WIDTH + (g + 1) * SSD_STATE]
            kmats.append(_dot_tn(bm, xw))
        return dt, acol, ecol, wide[CHUNK:], dec, kmats

    def advance(s, dec, kmats, backward):
        off = SSD_WIDTH if backward else 0
        koff = gw if backward else 0
        return [dec[:, off + g * gw:off + (g + 1) * gw] * s[g] + kmats[g][:, koff:koff + gw]
                for g in range(SSD_GROUPS)]

    ctx_terms = [chunk_terms(uc, dtcv, dacv, c * CHUNK) for c in range(nchc)]
    s_f0 = [jnp.zeros((SSD_STATE, gw), F32) for _ in range(SSD_GROUPS)]
    for c in range(nchc):
        s_f0 = advance(s_f0, ctx_terms[c][4], ctx_terms[c][5], False)
    s_b0 = [jnp.zeros((SSD_STATE, gw), F32) for _ in range(SSD_GROUPS)]
    for c in reversed(range(nchc)):
        s_b0 = advance(s_b0, ctx_terms[c][4], ctx_terms[c][5], True)

    def prep(c, carry):
        base = pl.multiple_of(c * CHUNK, CHUNK)
        dt, acol, ecol, scale, dec, kmats = chunk_terms(u, dtv, dav, base)
        acum[pl.ds(base, CHUNK), :] = acol
        ecum[pl.ds(base, CHUNK), :] = ecol
        hrow = pl.ds(pl.multiple_of(c * 2 * nh, 2 * nh), 2 * nh)
        arow_scr[hrow, :] = acol.T[:2 * nh]
        erow_scr[hrow, :] = ecol.T[:2 * nh]
        dtrow_scr[hrow, :] = dt.T[:2 * nh]
        xpad[pl.ds(base, CHUNK), :] = scale
        dec_scr[pl.ds(pl.multiple_of(c * SUBLANES, SUBLANES), SUBLANES), :] = jnp.broadcast_to(dec, (SUBLANES, 2 * SSD_WIDTH))
        for g in range(SSD_GROUPS):
            sf_scr[c, g] = kmats[g][:, :gw]
            kb_scr[c, g] = kmats[g][:, gw:]
        return carry

    lax.fori_loop(0, nch, prep, 0)

    def chunk_dec(c):
        return dec_scr[pl.ds(pl.multiple_of(c * SUBLANES, SUBLANES), 1), :]

    def fwd(c, s_old):
        dec = chunk_dec(c)
        new = []
        for g in range(SSD_GROUPS):
            new.append(dec[:, g * gw:(g + 1) * gw] * s_old[g] + sf_scr[c, g])
            sf_scr[c, g] = s_old[g]
        return tuple(new)

    lax.fori_loop(0, nch, fwd, tuple(s_f0))

    def bwd(i, s_b):
        c = nch - 1 - i
        base = pl.multiple_of(c * CHUNK, CHUNK)
        acol = acum[pl.ds(base, CHUNK), :]
        ecol = ecum[pl.ds(base, CHUNK), :]
        hrow = pl.ds(pl.multiple_of(c * 2 * nh, 2 * nh), 2 * nh)
        arow = arow_scr[hrow, :]
        erow = erow_scr[hrow, :]
        dt_t = dtrow_scr[hrow, :]
        scale = xpad[pl.ds(base, CHUNK), :]
        ys = []
        for g in range(SSD_GROUPS):
            bm = u[pl.ds(base, CHUNK), SSD_WIDTH + g * SSD_STATE:SSD_WIDTH + (g + 1) * SSD_STATE]
            cm = u[pl.ds(base, CHUNK), SSD_WIDTH + (SSD_GROUPS + g) * SSD_STATE:SSD_WIDTH + (SSD_GROUPS + g + 1) * SSD_STATE]
            cbm = _dot_nt(cm, bm)
            cs_f = _dot(cm, sf_scr[c, g].astype(BF16))
            cs_b = _dot(cm, s_b[g].astype(BF16))
            for pp in range(SSD_PAIRS // SSD_GROUPS):
                p = g * (SSD_PAIRS // SSD_GROUPS) + pp
                xs_b = u[pl.ds(base, CHUNK), p * LANES:(p + 1) * LANES]
                y_h = []
                for hh in range(2):
                    r = 2 * p + hh
                    arg = jnp.where(causal, colb(acol, r) - arow[r:r + 1, :],
                                    erow[nh + r:nh + r + 1, :] - colb(ecol, nh + r))
                    coef = jnp.where(causal, dt_t[r:r + 1, :], dt_t[nh + r:nh + r + 1, :])
                    gm = (cbm * (jnp.exp(arg) * coef)).astype(BF16)
                    y_h.append(_dot(gm, xs_b))
                sl = slice(pp * LANES, (pp + 1) * LANES)
                wl = slice(p * LANES, (p + 1) * LANES)
                wlb = slice(SSD_WIDTH + p * LANES, SSD_WIDTH + (p + 1) * LANES)
                ys.append(pair_sel(y_h[0], y_h[1]) + cs_f[:, sl] * scale[:, wl] + cs_b[:, sl] * scale[:, wlb]
                          + dsk_ref[:, wl] * xs_b.astype(F32))
        y = jnp.concatenate(ys, axis=1)
        y = y * _silu(z_ref[0, pl.ds(base, CHUNK), :].astype(F32))
        ms = jnp.mean(y * y, axis=-1, keepdims=True)
        y_ref[0, pl.ds(base, CHUNK), :] = ((y * lax.rsqrt(ms + EPS)) * nw_ref[...]).astype(BF16)
        dec = chunk_dec(c)
        return tuple(dec[:, SSD_WIDTH + g * gw:SSD_WIDTH + (g + 1) * gw] * s_b[g] + kb_scr[c, g]
                     for g in range(SSD_GROUPS))

    lax.fori_loop(0, nch, bwd, tuple(s_b0))


def _ssd(xbc, z, dt, xbcc, dtc, conv_w8, conv_b, dt_bias, a_log, d_skip, norm_w):
    b, L, nconv = xbc.shape
    Lc = xbcc.shape[1]
    nch = L // CHUNK
    per_b = lambda n, w: pl.BlockSpec((1, n, w), lambda i: (i, 0, 0))
    return pl.pallas_call(
        _ssd_kernel,
        grid=(b,),
        in_specs=[
            per_b(L, nconv), per_b(L, SSD_WIDTH), per_b(L, LANES), per_b(Lc, nconv), per_b(Lc, LANES),
            _const_spec(conv_w8.shape), _const_spec(conv_b.shape), _const_spec(dt_bias.shape),
            _const_spec(a_log.shape), _const_spec(d_skip.shape), _const_spec(norm_w.shape),
        ],
        out_specs=per_b(L, SSD_WIDTH),
        out_shape=jax.ShapeDtypeStruct((b, L, SSD_WIDTH), BF16),
        scratch_shapes=[
            pltpu.VMEM((L + 2 * CONV_HALO, nconv), F32),
            pltpu.VMEM((Lc + 2 * CONV_HALO, nconv), F32),
            pltpu.VMEM((L, nconv), BF16),
            pltpu.VMEM((Lc, nconv), BF16),
            pltpu.VMEM((L, LANES), F32), pltpu.VMEM((L, LANES), F32),
            pltpu.VMEM((Lc, LANES), F32), pltpu.VMEM((Lc, LANES), F32),
            pltpu.VMEM((nch, SSD_GROUPS, SSD_STATE, 2 * LANES), F32),
            pltpu.VMEM((nch, SSD_GROUPS, SSD_STATE, 2 * LANES), F32),
            pltpu.VMEM((L, LANES), F32), pltpu.VMEM((L, LANES), F32),
            pltpu.VMEM((nch * SUBLANES, 2 * SSD_WIDTH), F32),
            pltpu.VMEM((nch * 2 * SSD_HEADS, CHUNK), F32), pltpu.VMEM((nch * 2 * SSD_HEADS, CHUNK), F32),
            pltpu.VMEM((nch * 2 * SSD_HEADS, CHUNK), F32),
        ],
        compiler_params=pltpu.CompilerParams(vmem_limit_bytes=VMEM_LIMIT_SSD),
        name="ssd",
    )(xbc, z, dt, xbcc, dtc, conv_w8, conv_b, dt_bias, a_log, d_skip, norm_w)


def _ret_kernel(q_ref, k_ref, v_ref, g_ref, kc_ref, vc_ref, df_ref, db_ref, gn_ref, y_ref, sf_scr):
    L = q_ref.shape[1]
    Lc = kc_ref.shape[1]
    nch = L // CHUNK
    dk = RET_DK
    row_i = lax.broadcasted_iota(jnp.int32, (CHUNK, dk), 0).astype(F32)
    col_i = lax.broadcasted_iota(jnp.int32, (CHUNK, dk), 1).astype(F32)
    rel = row_i - col_i
    crow = lax.broadcasted_iota(jnp.int32, (Lc, dk), 0).astype(F32)

    heads = []
    s_f0 = []
    s_b0 = []
    for h in range(RET_HEADS):
        cols = slice(h * dk, (h + 1) * dk)
        lg_f = -jnp.exp(df_ref[:, cols])
        lg_b = -jnp.exp(db_ref[:, cols])
        heads.append(dict(
            cols=cols,
            dmat=jnp.where(rel >= 0, jnp.exp(jnp.maximum(rel, 0.0) * lg_f), jnp.exp(jnp.maximum(-rel, 0.0) * lg_b)),
            dq_f=jnp.exp((row_i + 1.0) * lg_f),
            dq_b=jnp.exp((CHUNK - row_i) * lg_b),
            dk_f=jnp.exp((CHUNK - 1.0 - row_i) * lg_f),
            dk_b=jnp.exp(row_i * lg_b),
            dc_f=jnp.exp(CHUNK * lg_f),
            dc_b=jnp.exp(CHUNK * lg_b),
        ))
        kc = kc_ref[0, :, cols].astype(F32)
        vc = vc_ref[0, :, cols]
        s_f0.append(_dot_tn((kc * jnp.exp((Lc - 1.0 - crow) * lg_f)).astype(BF16), vc))
        s_b0.append(_dot_tn((kc * jnp.exp(crow * lg_b)).astype(BF16), vc))

    def fwd(c, s_f):
        base = pl.multiple_of(c * CHUNK, CHUNK)
        new = []
        for h, hd in enumerate(heads):
            sf_scr[c, h] = s_f[h]
            kk = k_ref[0, pl.ds(base, CHUNK), hd["cols"]].astype(F32)
            vv = v_ref[0, pl.ds(base, CHUNK), hd["cols"]]
            new.append(hd["dc_f"] * s_f[h] + _dot_tn((kk * hd["dk_f"]).astype(BF16), vv))
        return tuple(new)

    lax.fori_loop(0, nch, fwd, tuple(s_f0), unroll=RET_UNROLL)

    def bwd(i, s_bs):
        c = nch - 1 - i
        base = pl.multiple_of(c * CHUNK, CHUNK)
        new = []
        for h, hd in enumerate(heads):
            qq = q_ref[0, pl.ds(base, CHUNK), hd["cols"]]
            kk = k_ref[0, pl.ds(base, CHUNK), hd["cols"]]
            vv = v_ref[0, pl.ds(base, CHUNK), hd["cols"]]
            s_b = s_bs[h]
            scores = (_dot_nt(qq, kk) * hd["dmat"]).astype(BF16)
            y = (_dot(scores, vv)
                 + _dot(qq, sf_scr[c, h].astype(BF16)) * hd["dq_f"]
                 + _dot(qq, s_b.astype(BF16)) * hd["dq_b"])
            mu = jnp.mean(y, axis=-1, keepdims=True)
            yc = y - mu
            var = jnp.mean(yc * yc, axis=-1, keepdims=True)
            yn = (yc * lax.rsqrt(var + EPS)) * gn_ref[:, hd["cols"]]
            gate = _silu(g_ref[0, pl.ds(base, CHUNK), hd["cols"]].astype(F32))
            y_ref[0, pl.ds(base, CHUNK), hd["cols"]] = (yn * gate).astype(BF16)
            new.append(hd["dc_b"] * s_b + _dot_tn((kk.astype(F32) * hd["dk_b"]).astype(BF16), vv))
        return tuple(new)

    lax.fori_loop(0, nch, bwd, tuple(s_b0), unroll=RET_UNROLL)


def _retention(q, k, v, g, kc, vc, decay_f, decay_b, gn_w):
    b, L, w = q.shape
    Lc = kc.shape[1]
    nch = L // CHUNK
    per_b = lambda n: pl.BlockSpec((1, n, w), lambda i: (i, 0, 0))
    return pl.pallas_call(
        _ret_kernel,
        grid=(b,),
        in_specs=[per_b(L), per_b(L), per_b(L), per_b(L), per_b(Lc), per_b(Lc),
                  _const_spec((1, w)), _const_spec((1, w)), _const_spec((1, w))],
        out_specs=per_b(L),
        out_shape=jax.ShapeDtypeStruct((b, L, w), BF16),
        scratch_shapes=[
            pltpu.VMEM((nch, RET_HEADS, RET_DK, RET_DK), F32),
        ],
        compiler_params=pltpu.CompilerParams(vmem_limit_bytes=VMEM_LIMIT),
        name="retention",
    )(q, k, v, g, kc, vc, decay_f, decay_b, gn_w)


def _outproj_router_kernel(yr_ref, ys_ref, x_ref, g1_ref, sh2_ref, sc2_ref, npost_ref, npre_ref,
                           wor_ref, wos_ref, wr_ref, br_ref, tri_ref,
                           x1_ref, h2_ref, route_ref, slots_ref, seg_ref,
                           wcat):
    i = pl.program_id(0)

    @pl.when(i == 0)
    def _():
        wr = wr_ref[...]
        hi = wr.astype(BF16)
        wcat[:, :LANES] = hi
        wcat[:, LANES:] = (wr - hi.astype(F32)).astype(BF16)

    _route_tile(yr_ref, ys_ref, x_ref, g1_ref, sh2_ref, sc2_ref, npost_ref, npre_ref, wor_ref, wos_ref, br_ref,
                tri_ref, x1_ref, h2_ref, route_ref, slots_ref, seg_ref, wcat)


def _route_tile(yr_ref, ys_ref, x_ref, g1_ref, sh2_ref, sc2_ref, npost_ref, npre_ref, wor_ref, wos_ref, br_ref,
                tri_ref, x1_ref, h2_ref, route_ref, slots_ref, seg_ref, wcat):
    tm = x_ref.shape[0]
    rows = slice(0, tm)
    y = _dot(yr_ref[rows, :], wor_ref[...]) + _dot(ys_ref[rows, :], wos_ref[...])
    ms = jnp.mean(y * y, axis=-1, keepdims=True)
    x1 = x_ref[rows, :] + (y * lax.rsqrt(ms + EPS)) * (g1_ref[0] * npost_ref[...])
    x1_ref[rows, :] = x1
    h2 = _norm_mod(x1, npre_ref[...], sc2_ref[0], sh2_ref[0])
    h2_ref[rows, :] = h2.astype(BF16)

    h_hi = h2.astype(BF16)
    h_lo = (h2 - h_hi.astype(F32)).astype(BF16)
    both = _dot(h_hi, wcat[...])
    lg = both[:, :LANES] + both[:, LANES:] + _dot(h_lo, wcat[:, :LANES]) + br_ref[...]

    lane = lax.broadcasted_iota(jnp.int32, (tm, LANES), 1)
    lane_f = lane.astype(F32)
    is_grp = (lane >= N_EXPERTS) & (lane < N_EXPERTS + MOE_GROUPS)
    gl = jnp.where(is_grp, lg, NEG_BIG)
    mg = jnp.max(gl, axis=-1, keepdims=True)
    grp_lane = jnp.min(jnp.where(gl == mg, lane_f, 1e9), axis=-1, keepdims=True)
    p_g = 1.0 / jnp.sum(jnp.where(is_grp, jnp.exp(gl - mg), 0.0), axis=-1, keepdims=True)
    first = (grp_lane - N_EXPERTS) * EXPERTS_PER_GROUP
    in_grp = (lane_f >= first) & (lane_f < first + EXPERTS_PER_GROUP)
    el = jnp.where(in_grp, lg, NEG_BIG)
    t1 = jnp.max(el, axis=-1, keepdims=True)
    i1 = jnp.min(jnp.where(el == t1, lane_f, 1e9), axis=-1, keepdims=True)
    el2 = jnp.where(lane_f == i1, NEG_BIG, el)
    t2 = jnp.max(el2, axis=-1, keepdims=True)
    i2 = jnp.min(jnp.where(el2 == t2, lane_f, 1e9), axis=-1, keepdims=True)
    s = jnp.exp(t2 - t1)
    w1 = p_g / (1.0 + s)
    w2 = p_g * s / (1.0 + s)

    oh1 = (lane_f == i1)
    oh2 = (lane_f == i2)
    ohf = jnp.where(oh1 | oh2, 1.0, 0.0)
    before = _dot(tri_ref[...], ohf.astype(BF16))
    cnt = jnp.sum(ohf, axis=0, keepdims=True)
    seg = jnp.floor((cnt + (SEG_ALIGN - 1.0)) * (1.0 / SEG_ALIGN)) * SEG_ALIGN
    e_row = lax.broadcasted_iota(jnp.int32, (LANES, LANES), 0)
    e_col = lax.broadcasted_iota(jnp.int32, (LANES, LANES), 1)
    earlier = (e_row < e_col).astype(BF16)
    seg_off = _dot(jnp.broadcast_to(seg, (SUBLANES, LANES)).astype(BF16), earlier)[0:1]
    where_to = before + seg_off
    lpos1 = jnp.sum(jnp.where(oh1, where_to, 0.0), axis=-1, keepdims=True)
    lpos2 = jnp.sum(jnp.where(oh2, where_to, 0.0), axis=-1, keepdims=True)

    cols = [w1, w2, lpos1, lpos2]
    for wk in (w1, w2):
        hi = wk.astype(BF16).astype(F32)
        cols += [hi, wk - hi]
    cols.append(jnp.ones_like(w1))
    packed = jnp.zeros((tm, LANES), F32)
    for k, col in enumerate(cols):
        packed = jnp.where(lane == k, col, packed)
    route_ref[rows, :] = packed

    row = lax.broadcasted_iota(jnp.int32, (tm, LANES), 0)
    on_diag = (row % LANES) == lane
    per = tm // LANES
    for qi, col in enumerate((lpos1, lpos2)):
        picked = jnp.where(on_diag, col, 0.0)
        dense = jnp.sum(picked.reshape(per, LANES, LANES), axis=1).astype(jnp.int32)
        slots_ref[0, qi * per:(qi + 1) * per, :] = dense
    tbl_row = lax.broadcasted_iota(jnp.int32, (SUBLANES, LANES), 0)
    seg_ref[0] = jnp.where(tbl_row == 0, seg, jnp.where(tbl_row == 1, seg_off, 0.0)).astype(jnp.int32)


def _outproj_router(yr, ys, x2, mod3, npost, npre, wo_r, wo_s, w_router, b_router, seq_len):
    T, d = x2.shape
    tm = TM_OUT
    per_seq = seq_len // tm
    rw = yr.shape[1]
    tri = (jnp.arange(tm)[:, None] > jnp.arange(tm)[None, :]).astype(BF16)
    tok = lambda w: pl.BlockSpec((tm, w), lambda i: (i, 0))
    modv = lambda k: pl.BlockSpec((1, 1, d), lambda i: (i // per_seq, 0, k))
    tile3 = lambda r: pl.BlockSpec((1, r, LANES), lambda i: (i, 0, 0))
    slot_rows = TOP_K * (tm // LANES)
    return pl.pallas_call(
        _outproj_router_kernel,
        grid=(T // tm,),
        in_specs=[
            tok(rw), tok(rw), tok(d), modv(2), modv(3), modv(4),
            _const_spec((1, d)), _const_spec((1, d)),
            _const_spec(wo_r.shape), _const_spec(wo_s.shape), _const_spec(w_router.shape), _const_spec((1, LANES)),
            _const_spec((tm, tm)),
        ],
        out_specs=[tok(d), tok(d), tok(LANES), tile3(slot_rows), tile3(SUBLANES)],
        out_shape=[jax.ShapeDtypeStruct((T, d), F32), jax.ShapeDtypeStruct((T, d), BF16),
                   jax.ShapeDtypeStruct((T, LANES), F32),
                   jax.ShapeDtypeStruct((T // tm, slot_rows, LANES), jnp.int32),
                   jax.ShapeDtypeStruct((T // tm, SUBLANES, LANES), jnp.int32)],
        scratch_shapes=[pltpu.VMEM((d, 2 * LANES), BF16)],
        compiler_params=pltpu.CompilerParams(dimension_semantics=("arbitrary",),
                                             vmem_limit_bytes=VMEM_LIMIT),
        name="outproj_router",
    )(yr, ys, x2, mod3, mod3, mod3, npost, npre, wo_r, wo_s, w_router, b_router, tri)


def _expert_kernel(be_ref, first_ref, slot_ref, next_ref, nused_ref, xs_ref, wg_hbm, wu_hbm, wd_hbm, y_ref,
                   wg_f, wu_f, wd_f, wg_b, wu_b, wd_b, sem):
    def fetch(e, s):
        return [pltpu.make_async_copy(src.at[e], dst.at[s], sem.at[s, k])
                for k, (src, dst) in enumerate(((wg_hbm, wg_f), (wu_hbm, wu_f), (wd_hbm, wd_f)))]

    @pl.when(pl.program_id(0) == 0)
    def _():
        for c in fetch(be_ref[0], slot_ref[0]):
            c.start()

    for sb in range(EXPERT_BLOCKS_PER_STEP):
        _expert_block(pl.program_id(0) * EXPERT_BLOCKS_PER_STEP + sb, slice(sb * MB_EXPERT, (sb + 1) * MB_EXPERT),
                      fetch, be_ref, first_ref, slot_ref, next_ref, nused_ref, xs_ref, y_ref,
                      wg_f, wu_f, wd_f, wg_b, wu_b, wd_b)


def _expert_block(i, rows, fetch, be_ref, first_ref, slot_ref, next_ref, nused_ref, xs_ref, y_ref,
                  wg_f, wu_f, wd_f, wg_b, wu_b, wd_b):
    @pl.when(first_ref[i] == 1)
    def _():
        s = slot_ref[i]

        @pl.when(next_ref[i] >= 0)
        def _():
            for c in fetch(next_ref[i], 1 - s):
                c.start()

        for c in fetch(be_ref[i], s):
            c.wait()
        wg_b[...] = wg_f[s].astype(BF16)
        wu_b[...] = wu_f[s].astype(BF16)
        wd_b[...] = wd_f[s].astype(BF16)

    @pl.when(i < nused_ref[0])
    def _():
        d = y_ref.shape[1]
        side = xs_ref[rows, d:d + LANES].astype(F32)
        second = side[:, 8:9] == 2.0
        unscale = jnp.where(second, 0.5, 1.0)
        weight = jnp.where(second, side[:, 6:7] + side[:, 7:8], side[:, 4:5] + side[:, 5:6]) * unscale
        xb = xs_ref[rows, 0:d] * unscale.astype(BF16)
        hid = (_silu(_dot(xb, wg_b[...])) * _dot(xb, wu_b[...])).astype(BF16)
        y_ref[rows, :] = (_dot(hid, wd_b[...]) * weight).astype(BF16)

    @pl.when(i >= nused_ref[0])
    def _():
        y_ref[rows, :] = jnp.zeros((rows.stop - rows.start, y_ref.shape[1]), y_ref.dtype)


def _expert_plan(padded, pad_end, n_blocks, mb):
    n_used = (pad_end[-1:] // mb).astype(jnp.int32)
    blk_start = jnp.arange(n_blocks, dtype=jnp.int32) * mb
    experts = jnp.arange(N_EXPERTS, dtype=jnp.int32)
    blk_expert = jnp.minimum(jnp.sum((pad_end[None, :] <= blk_start[:, None]).astype(jnp.int32), axis=1),
                             N_EXPERTS - 1)
    prev = jnp.concatenate([jnp.full((1,), -1, jnp.int32), blk_expert[:-1]])
    first = ((blk_start < pad_end[-1]) & (blk_expert != prev)).astype(jnp.int32)
    slot = (jnp.cumsum(first) - 1) % 2
    later = jnp.where((padded > 0)[None, :] & (experts[None, :] > experts[:, None]), experts[None, :], N_EXPERTS)
    next_nonempty = jnp.min(later, axis=1)
    next_nonempty = jnp.where(next_nonempty == N_EXPERTS, -1, next_nonempty)
    nxt = jnp.sum(jnp.where(blk_expert[:, None] == experts[None, :], next_nonempty[None, :], 0), axis=1)
    return [a.astype(jnp.int32) for a in (blk_expert, first, slot, nxt, n_used)]


def _experts(plan, xs, w_gate, w_up, w_down):
    cap, dp = xs.shape
    n_exp, d, de = w_gate.shape
    step_rows = MB_EXPERT * EXPERT_BLOCKS_PER_STEP
    assert dp == d + LANES and cap % step_rows == 0
    grid_spec = pltpu.PrefetchScalarGridSpec(
        num_scalar_prefetch=len(plan),
        grid=(cap // step_rows,),
        in_specs=[
            pl.BlockSpec((step_rows, dp), lambda i, be, fi, sl, nx, nu: (
                jnp.minimum(i, (nu[0] - 1) // EXPERT_BLOCKS_PER_STEP), 0)),
            pl.BlockSpec(memory_space=pl.ANY), pl.BlockSpec(memory_space=pl.ANY), pl.BlockSpec(memory_space=pl.ANY),
        ],
        out_specs=pl.BlockSpec((step_rows, d), lambda i, be, fi, sl, nx, nu: (i, 0)),
        scratch_shapes=[pltpu.VMEM((2, d, de), F32), pltpu.VMEM((2, d, de), F32), pltpu.VMEM((2, de, d), F32),
                        pltpu.VMEM((d, de), BF16), pltpu.VMEM((d, de), BF16), pltpu.VMEM((de, d), BF16),
                        pltpu.SemaphoreType.DMA((2, 3))],
    )
    return pl.pallas_call(
        _expert_kernel,
        grid_spec=grid_spec,
        out_shape=jax.ShapeDtypeStruct((cap, d), BF16),
        compiler_params=pltpu.CompilerParams(dimension_semantics=("arbitrary",),
                                             vmem_limit_bytes=VMEM_LIMIT),
        name="experts",
    )(*plan, xs, w_gate, w_up, w_down)


def _segment_pieces(tile, segrow_ref, seglen_ref, segoff_ref, act):
    def per_expert(e, carry):
        idx = tile * N_EXPERTS + e
        g0 = segrow_ref[idx]
        l0 = segoff_ref[idx]

        def piece(j, c2):
            act(pl.multiple_of(l0 + j * SEG_ALIGN, SEG_ALIGN), pl.multiple_of(g0 + j * SEG_ALIGN, SEG_ALIGN))
            return c2

        lax.fori_loop(0, seglen_ref[idx], piece, 0)
        return carry

    lax.fori_loop(0, N_EXPERTS, per_expert, 0)


def _wait_rows(total, row_copy):
    size = SEG_ALIGN
    while size <= _stage_rows(max(TD_DISPATCH, TF_COMBINE)):
        @pl.when((total & size) != 0)
        def _(size=size):
            row_copy(size).wait()
        size *= 2


def _for_tile_rows(tile_rows, max_rows, body):
    @pl.when(tile_rows <= COMMON_STAGE_ROWS)
    def _():
        body(COMMON_STAGE_ROWS)

    @pl.when(tile_rows > COMMON_STAGE_ROWS)
    def _():
        body(max_rows)


def _stage_rows(tile_tokens):
    return TOP_K * tile_tokens + N_EXPERTS * SEG_ALIGN


def _dispatch_seg_kernel(pad_end_ref, zero_from_ref, segrow_ref, seglen_ref, segoff_ref, tilerows_ref,
                         lpos_ref, route_ref, h_ref, xs_hbm, zbuf, stage, sem, zsem):
    i = pl.program_id(0)
    nt = pl.num_programs(0)
    td, d = h_ref.shape
    sr = stage.shape[1]
    per = td // LANES

    def zero_fills(act):
        def fill(row, n):
            act(pltpu.make_async_copy(zbuf.at[pl.ds(0, n)], xs_hbm.at[pl.ds(pl.multiple_of(row, SEG_ALIGN), n)], zsem))

        def region(e, carry):
            row = zero_from_ref[e]
            left = pad_end_ref[e] - row
            size = ZERO_PIECE
            while size >= SEG_ALIGN:
                take = (left & size) != 0

                @pl.when(take)
                def _(row=row, size=size):
                    fill(row, size)

                row = row + jnp.where(take, size, 0)
                size //= 2
            return carry

        def tail(p, carry):
            @pl.when(p * ZERO_PIECE >= pad_end_ref[N_EXPERTS - 1])
            def _():
                fill(p * ZERO_PIECE, ZERO_PIECE)
            return carry

        lax.fori_loop(0, N_EXPERTS, region, 0)
        lax.fori_loop(0, xs_hbm.shape[0] // ZERO_PIECE, tail, 0)

    @pl.when(i == 0)
    def _():
        zbuf[...] = jnp.zeros_like(zbuf)
        zero_fills(lambda c: c.start())

    def shipped(tile):
        _wait_rows(tilerows_ref[tile], lambda n: pltpu.make_async_copy(
            stage.at[tile % 2, pl.ds(0, n)], xs_hbm.at[pl.ds(0, n)], sem.at[tile % 2]))

    @pl.when(i >= 2)
    def _():
        shipped(i - 2)

    slot = i % 2

    def sort_rows(n_rows):
        srow = lax.broadcasted_iota(jnp.int32, (n_rows, LANES), 0)
        place = jnp.concatenate(
            [jnp.where(srow == lpos_ref[0, cb:cb + 1, :], 1.0,
                       jnp.where(srow == lpos_ref[0, per + cb:per + cb + 1, :], 2.0, 0.0)) for cb in range(per)],
            axis=1).astype(BF16)
        stage[slot, 0:n_rows, :] = _dot(place, jnp.concatenate([h_ref[...], route_ref[...].astype(BF16)],
                                                               axis=1)).astype(BF16)

    _for_tile_rows(tilerows_ref[i], sr, sort_rows)
    _segment_pieces(i, segrow_ref, seglen_ref, segoff_ref, lambda lr, gr: pltpu.make_async_copy(
        stage.at[slot, pl.ds(lr, SEG_ALIGN)], xs_hbm.at[pl.ds(gr, SEG_ALIGN)], sem.at[slot]).start())

    @pl.when(i == nt - 1)
    def _():
        @pl.when(i >= 1)
        def _():
            shipped(i - 1)
        shipped(i)
        zero_fills(lambda c: c.wait())


def _dispatch_seg(plan, lpos, route, h2, cap):
    T, d = h2.shape
    td = TD_DISPATCH
    nt = T // td
    width = d + LANES
    sr = _stage_rows(td)
    n_pre = len(plan)
    grid_spec = pltpu.PrefetchScalarGridSpec(
        num_scalar_prefetch=n_pre,
        grid=(nt,),
        in_specs=[
            pl.BlockSpec((1, lpos.shape[1], LANES), lambda i, *_: (i, 0, 0)),
            pl.BlockSpec((td, LANES), lambda i, *_: (i, 0)),
            pl.BlockSpec((td, d), lambda i, *_: (i, 0)),
        ],
        out_specs=pl.BlockSpec(memory_space=pl.ANY),
        scratch_shapes=[pltpu.VMEM((ZERO_PIECE, width), BF16), pltpu.VMEM((2, sr, width), BF16),
                        pltpu.SemaphoreType.DMA((2,)), pltpu.SemaphoreType.DMA(())],
    )
    return pl.pallas_call(
        _dispatch_seg_kernel,
        grid_spec=grid_spec,
        out_shape=jax.ShapeDtypeStruct((cap, width), BF16),
        compiler_params=pltpu.CompilerParams(dimension_semantics=("arbitrary",), vmem_limit_bytes=VMEM_LIMIT),
        name="dispatch",
    )(*plan, lpos, route, h2)


def _combine_seg_kernel(segrow_ref, seglen_ref, segoff_ref, tilerows_ref, route_ref, x1_ref, g2_ref, nw_ref,
                        yb_hbm, o_ref, stage, sem):
    i = pl.program_id(0)
    nt = pl.num_programs(0)
    tf = x1_ref.shape[0]
    sr = stage.shape[1]

    def fetch(tile, slot):
        _segment_pieces(tile, segrow_ref, seglen_ref, segoff_ref, lambda lr, gr: pltpu.make_async_copy(
            yb_hbm.at[pl.ds(gr, SEG_ALIGN)], stage.at[slot, pl.ds(lr, SEG_ALIGN)], sem.at[slot]).start())

    @pl.when(i == 0)
    def _():
        stage[...] = jnp.zeros_like(stage)
        fetch(i, 0)

    for slot in range(2):
        @pl.when((i + 1 < nt) & (i % 2 != slot))
        def _(slot=slot):
            fetch(i + 1, slot)

    _wait_rows(tilerows_ref[i], lambda n: pltpu.make_async_copy(
        yb_hbm.at[pl.ds(0, n)], stage.at[i % 2, pl.ds(0, n)], sem.at[i % 2]))

    local_row = route_ref[:, 2:2 + TOP_K].astype(jnp.int32)

    def unsort_rows(n_rows):
        scol = lax.broadcasted_iota(jnp.int32, (tf, n_rows), 1)
        pick = jnp.where((scol == local_row[:, 0:1]) | (scol == local_row[:, 1:2]), 1.0, 0.0).astype(BF16)
        out = _dot(pick, stage[i % 2, 0:n_rows, :])
        ms = jnp.mean(out * out, axis=-1, keepdims=True)
        o_ref[...] = x1_ref[...] + g2_ref[0] * ((out * lax.rsqrt(ms + EPS)) * nw_ref[...])

    _for_tile_rows(tilerows_ref[i], sr, unsort_rows)


def _combine_seg(seg_plan, route, x1, mod3, nw, yb, seq_len):
    T, d = x1.shape
    tf = TF_COMBINE
    nt = T // tf
    per_seq = seq_len // tf
    grid_spec = pltpu.PrefetchScalarGridSpec(
        num_scalar_prefetch=len(seg_plan),
        grid=(nt,),
        in_specs=[
            pl.BlockSpec((tf, LANES), lambda i, *_: (i, 0)),
            pl.BlockSpec((tf, d), lambda i, *_: (i, 0)),
            pl.BlockSpec((1, 1, d), lambda i, *_: (i // per_seq, 0, 5)),
            pl.BlockSpec((1, d), lambda i, *_: (0, 0)),
            pl.BlockSpec(memory_space=pl.ANY),
        ],
        out_specs=pl.BlockSpec((tf, d), lambda i, *_: (i, 0)),
        scratch_shapes=[pltpu.VMEM((2, _stage_rows(tf), d), yb.dtype), pltpu.SemaphoreType.DMA((2,))],
    )
    return pl.pallas_call(
        _combine_seg_kernel,
        grid_spec=grid_spec,
        out_shape=jax.ShapeDtypeStruct((T, d), F32),
        compiler_params=pltpu.CompilerParams(dimension_semantics=("arbitrary",), vmem_limit_bytes=VMEM_LIMIT),
        name="combine",
    )(*seg_plan, route, x1, mod3, nw, yb)


def _rope_tables(L, n_heads):
    quarter = RET_DK // 4
    freqs = ROPE_BASE ** (-jnp.arange(quarter, dtype=F32) / quarter)
    t = jnp.arange(L)
    ang_r = (t // GRID_W).astype(F32)[:, None] * freqs
    ang_c = (t % GRID_W).astype(F32)[:, None] * freqs
    cos = jnp.concatenate([jnp.cos(ang_r)] * 2 + [jnp.cos(ang_c)] * 2, axis=-1)
    sin = jnp.concatenate([-jnp.sin(ang_r), jnp.sin(ang_r), -jnp.sin(ang_c), jnp.sin(ang_c)], axis=-1)
    return jnp.tile(cos, (1, n_heads)), jnp.tile(sin, (1, n_heads))


def _lane_pad(v, width=LANES):
    return jnp.pad(v, [(0, 0)] * (v.ndim - 1) + [(0, width - v.shape[-1])])


def kernel(x, c, ctx, c_ctx, w_mod, b_mod, norm_pre_mix, norm_post_mix, norm_pre_ffn, norm_post_ffn, w_in, w_out, ret_decay_f, ret_decay_b, ret_gn_w, ssd_conv_w, ssd_conv_b, ssd_dt_bias_f, ssd_dt_bias_b, ssd_a_log_f, ssd_a_log_b, ssd_d, ssd_norm_w, moe_w_rg, moe_b_rg, moe_w_re, moe_b_re, moe_w_gate, moe_w_up, moe_w_down):
    b, L, d = x.shape
    assert w_mod.shape[0] == 1, "single layer: context outputs are never needed"
    assert TM_OUT == TD_DISPATCH == TF_COMBINE, "router, dispatch and combine share one slot-row layout"
    rw = RET_HEADS * RET_DK
    nconv = SSD_WIDTH + 2 * SSD_GROUPS * SSD_STATE
    T = b * L

    mod_rows = -(-(b + 1) // SUBLANES) * SUBLANES
    c_all = jnp.zeros((mod_rows, d), F32).at[:b].set(c).at[b].set(c_ctx)
    mod3 = _modulation(c_all, w_mod[0], b_mod[0]).reshape(mod_rows, 1, 6 * d)

    wi = w_in[0]
    o = 0
    wq = wi[:, o:o + rw]; o += rw
    wk = wi[:, o:o + rw]; o += rw
    wv = wi[:, o:o + rw]; o += rw
    wg = wi[:, o:o + rw]; o += rw
    wz = wi[:, o:o + SSD_WIDTH]; o += SSD_WIDTH
    wxbc = wi[:, o:o + nconv].astype(BF16); o += nconv
    wdt = _lane_pad(wi[:, o:o + 2 * SSD_HEADS]).astype(BF16)
    wqk = jnp.concatenate([wq, wk], axis=1).astype(BF16)
    wvgz = jnp.concatenate([wv, wg, wz], axis=1).astype(BF16)
    cos_t, sin_t = _rope_tables(L, RET_HEADS)
    nw1 = norm_pre_mix[0].reshape(1, d)

    q, k, v, g, z, xbc, dt = _inproj(x, mod3, nw1, wqk, wvgz, wxbc, wdt, cos_t, sin_t)
    kc, vc, xbcc, dtc = _inproj_ctx(ctx, mod3, b, nw1, wk.astype(BF16), wv.astype(BF16), wxbc, wdt)

    conv_w8 = jnp.pad(ssd_conv_w[0], ((0, SUBLANES - SSD_CONV), (0, 0)))
    dt_bias = _lane_pad(jnp.concatenate([ssd_dt_bias_f[0], ssd_dt_bias_b[0]])[None, :])
    a_log = _lane_pad(jnp.concatenate([ssd_a_log_f[0], ssd_a_log_b[0]])[None, :])
    d_skip = jnp.repeat(ssd_d[0], SSD_HEADDIM)[None, :]
    ys = _ssd(xbc, z, dt, xbcc, dtc, conv_w8, ssd_conv_b[0][None, :], dt_bias, a_log, d_skip,
              ssd_norm_w[0][None, :])

    yr = _retention(q, k, v, g, kc, vc,
                    jnp.repeat(ret_decay_f[0], RET_DK)[None, :], jnp.repeat(ret_decay_b[0], RET_DK)[None, :],
                    ret_gn_w[0][None, :])

    wo = w_out[0].astype(BF16)
    w_router = _lane_pad(jnp.concatenate(
        [jnp.transpose(moe_w_re[0], (1, 0, 2)).reshape(d, N_EXPERTS), moe_w_rg[0]], axis=1))
    b_router = _lane_pad(jnp.concatenate([moe_b_re[0].reshape(-1), moe_b_rg[0]])[None, :])
    x1, h2, route, lpos, seg = _outproj_router(
        yr.reshape(T, rw), ys.reshape(T, SSD_WIDTH), x.reshape(T, d), mod3,
        norm_post_mix[0][None, :], norm_pre_ffn[0][None, :], wo[:rw], wo[rw:], w_router, b_router, L)

    mb = MB_EXPERT
    nt = T // TM_OUT
    n_blocks = -(-(T * TOP_K + nt * N_EXPERTS * (SEG_ALIGN - 1) + N_EXPERTS * (mb - 1)) // mb)
    n_blocks = -(-n_blocks // EXPERT_BLOCKS_PER_STEP) * EXPERT_BLOCKS_PER_STEP
    seg_len = seg[:, 0, :N_EXPERTS]
    seg_off = seg[:, 1, :N_EXPERTS]
    used = jnp.sum(seg_len, axis=0)
    padded = (used + mb - 1) // mb * mb
    pad_end = jnp.cumsum(padded)
    pad_start = pad_end - padded
    seg_row = pad_start[None, :] + jnp.cumsum(seg_len, axis=0) - seg_len
    seg_plan = [a.reshape(-1).astype(jnp.int32)
                for a in (seg_row, seg_len // SEG_ALIGN, seg_off, jnp.sum(seg_len, axis=1))]
    zero_from = (pad_start + used).astype(jnp.int32)

    xs = _dispatch_seg([pad_end.astype(jnp.int32), zero_from] + seg_plan, lpos, route, h2, n_blocks * mb)
    yb = _experts(_expert_plan(padded, pad_end, n_blocks, mb), xs, moe_w_gate[0], moe_w_up[0], moe_w_down[0])
    out = _combine_seg(seg_plan, route, x1, mod3, norm_post_ffn[0][None, :], yb, L)
    return out.reshape(b, L, d)
```

```python
import jax
import jax.numpy as jnp
from jax import lax
from jax.experimental import pallas as pl
from jax.experimental.pallas import tpu as pltpu

F32 = jnp.float32
BF16 = jnp.bfloat16

LANES = 128
SUBLANES = 8
BF16_TILE_ROWS = 16
V7X_VMEM_BYTES = 64 * 1024 * 1024
VMEM_LIMIT = V7X_VMEM_BYTES * 3 // 4
VMEM_LIMIT_SSD = V7X_VMEM_BYTES * 7 // 8

EPS = 1e-6
CHUNK = 128
GRID_W = 64
RET_HEADS = 4
RET_DK = 128
ROPE_BASE = 10000.0
SSD_HEADS = 8
SSD_HEADDIM = 64
SSD_GROUPS = 2
SSD_STATE = 128
SSD_WIDTH = SSD_HEADS * SSD_HEADDIM
SSD_CONV = 5
SSD_PAIRS = SSD_WIDTH // LANES
MOE_GROUPS = 4
EXPERTS_PER_GROUP = 8
N_EXPERTS = MOE_GROUPS * EXPERTS_PER_GROUP
TOP_K = 2
CONV_HALO = SUBLANES

TM_PROJ = 512
TM_OUT = 512
TD_DISPATCH = TM_OUT
MB_EXPERT = 512
EXPERT_BLOCKS_PER_STEP = 4
ZERO_PIECE = MB_EXPERT // 2
TF_COMBINE = TM_OUT
SEG_ALIGN = BF16_TILE_ROWS
COMMON_STAGE_ROWS = TOP_K * TM_OUT + N_EXPERTS * 10
RET_UNROLL = 2
NEG_BIG = -1e30


def _silu(v):
    return v * jax.nn.sigmoid(v)


def _dot(a, b):
    return jnp.dot(a, b, preferred_element_type=F32)


def _dot_tn(a, b):
    return lax.dot_general(a, b, (((0,), (0,)), ((), ())), preferred_element_type=F32)


def _dot_nt(a, b):
    return lax.dot_general(a, b, (((1,), (1,)), ((), ())), preferred_element_type=F32)


def _mod_kernel(c_ref, w_ref, b_ref, o_ref):
    a = _silu(c_ref[...])
    w = w_ref[...]
    a_hi = a.astype(BF16)
    a_lo = (a - a_hi.astype(F32)).astype(BF16)
    w_hi = w.astype(BF16)
    w_lo = (w - w_hi.astype(F32)).astype(BF16)
    o_ref[...] = _dot(a_hi, w_hi) + _dot(a_lo, w_hi) + _dot(a_hi, w_lo) + b_ref[...]


def _modulation(c_all, w_mod, b_mod):
    rows, d = c_all.shape
    n = w_mod.shape[1]
    return pl.pallas_call(
        _mod_kernel,
        grid=(n // d,),
        in_specs=[
            pl.BlockSpec((rows, d), lambda j: (0, 0)),
            pl.BlockSpec((d, d), lambda j: (0, j)),
            pl.BlockSpec((1, d), lambda j: (0, j)),
        ],
        out_specs=pl.BlockSpec((rows, d), lambda j: (0, j)),
        out_shape=jax.ShapeDtypeStruct((rows, n), F32),
        name="modulation",
    )(c_all, w_mod, b_mod.reshape(1, n))


def _norm_mod(x, nw, sc, sh):
    ms = jnp.mean(x * x, axis=-1, keepdims=True)
    return (x * lax.rsqrt(ms + EPS)) * (nw * (1.0 + sc)) + sh


def _rope(t, cos, sin_signed, first_half):
    width = t.shape[-1]
    quarter = RET_DK // 4
    swapped = jnp.where(first_half, pltpu.roll(t, width - quarter, 1), pltpu.roll(t, quarter, 1))
    return t * cos + swapped * sin_signed


def _inproj_kernel(x_ref, sh_ref, sc_ref, nw_ref, wqk_ref, wvgz_ref, wxbc_ref, wdt_ref, cos_ref, sin_ref,
                   q_ref, k_ref, v_ref, g_ref, z_ref, xbc_ref, dt_ref):
    hb = _norm_mod(x_ref[0], nw_ref[...], sc_ref[0], sh_ref[0]).astype(BF16)
    rw = q_ref.shape[-1]
    qk = _dot(hb, wqk_ref[...])
    cos = cos_ref[...]
    sin = sin_ref[...]
    lane = lax.broadcasted_iota(jnp.int32, cos.shape, 1)
    first_half = (lane % (RET_DK // 2)) < (RET_DK // 4)
    q_ref[0] = _rope(qk[:, :rw], cos, sin, first_half).astype(BF16)
    k_ref[0] = (_rope(qk[:, rw:], cos, sin, first_half) * (RET_DK ** -0.5)).astype(BF16)
    vgz = _dot(hb, wvgz_ref[...])
    v_ref[0] = vgz[:, :rw].astype(BF16)
    g_ref[0] = vgz[:, rw:2 * rw].astype(BF16)
    z_ref[0] = vgz[:, 2 * rw:].astype(BF16)
    xbc_ref[0] = _dot(hb, wxbc_ref[...]).astype(BF16)
    dt_ref[0] = _dot(hb, wdt_ref[...])


def _inproj_ctx_kernel(x_ref, sh_ref, sc_ref, nw_ref, wk_ref, wv_ref, wxbc_ref, wdt_ref,
                       k_ref, v_ref, xbc_ref, dt_ref):
    hb = _norm_mod(x_ref[0], nw_ref[...], sc_ref[0], sh_ref[0]).astype(BF16)
    k_ref[0] = (_dot(hb, wk_ref[...]) * (RET_DK ** -0.5)).astype(BF16)
    v_ref[0] = _dot(hb, wv_ref[...]).astype(BF16)
    xbc_ref[0] = _dot(hb, wxbc_ref[...]).astype(BF16)
    dt_ref[0] = _dot(hb, wdt_ref[...])


def _const_spec(shape):
    nd = len(shape)
    return pl.BlockSpec(shape, lambda *_: (0,) * nd)


def _inproj(x, mod3, nw, wqk, wvgz, wxbc, wdt, cos_t, sin_t):
    b, L, d = x.shape
    tm = min(TM_PROJ, L)
    rw = wqk.shape[1] // 2
    tok = lambda w: pl.BlockSpec((1, tm, w), lambda i, j: (i, j, 0))
    out_bf = lambda w: jax.ShapeDtypeStruct((b, L, w), BF16)
    return pl.pallas_call(
        _inproj_kernel,
        grid=(b, L // tm),
        in_specs=[
            tok(d),
            pl.BlockSpec((1, 1, d), lambda i, j: (i, 0, 0)),
            pl.BlockSpec((1, 1, d), lambda i, j: (i, 0, 1)),
            _const_spec((1, d)),
            _const_spec(wqk.shape), _const_spec(wvgz.shape), _const_spec(wxbc.shape), _const_spec(wdt.shape),
            pl.BlockSpec((tm, rw), lambda i, j: (j, 0)),
            pl.BlockSpec((tm, rw), lambda i, j: (j, 0)),
        ],
        out_specs=[tok(rw), tok(rw), tok(rw), tok(rw), tok(rw), tok(wxbc.shape[1]), tok(LANES)],
        out_shape=[out_bf(rw), out_bf(rw), out_bf(rw), out_bf(rw), out_bf(rw), out_bf(wxbc.shape[1]),
                   jax.ShapeDtypeStruct((b, L, LANES), F32)],
        compiler_params=pltpu.CompilerParams(vmem_limit_bytes=VMEM_LIMIT),
        name="inproj",
    )(x, mod3, mod3, nw, wqk, wvgz, wxbc, wdt, cos_t, sin_t)


def _inproj_ctx(ctx, mod3, ctx_row, nw, wk, wv, wxbc, wdt):
    b, L, d = ctx.shape
    tm = min(TM_PROJ, L)
    rw = wk.shape[1]
    tok = lambda w: pl.BlockSpec((1, tm, w), lambda i, j: (i, j, 0))
    out_bf = lambda w: jax.ShapeDtypeStruct((b, L, w), BF16)
    return pl.pallas_call(
        _inproj_ctx_kernel,
        grid=(b, L // tm),
        in_specs=[
            tok(d),
            pl.BlockSpec((1, 1, d), lambda i, j: (ctx_row, 0, 0)),
            pl.BlockSpec((1, 1, d), lambda i, j: (ctx_row, 0, 1)),
            _const_spec((1, d)),
            _const_spec(wk.shape), _const_spec(wv.shape), _const_spec(wxbc.shape), _const_spec(wdt.shape),
        ],
        out_specs=[tok(rw), tok(rw), tok(wxbc.shape[1]), tok(LANES)],
        out_shape=[out_bf(rw), out_bf(rw), out_bf(wxbc.shape[1]), jax.ShapeDtypeStruct((b, L, LANES), F32)],
        compiler_params=pltpu.CompilerParams(vmem_limit_bytes=VMEM_LIMIT),
        name="inproj_ctx",
    )(ctx, mod3, mod3, nw, wk, wv, wxbc, wdt)


def _ssd_kernel(xbc_ref, z_ref, dt_ref, xbcc_ref, dtc_ref, cw_ref, cb_ref, dtb_ref, alog_ref, dsk_ref, nw_ref,
                y_ref,
                xpad, xpadc, u, uc, dtv, dav, dtcv, dacv, sf_scr, kb_scr, acum, ecum, dec_scr,
                arow_scr, erow_scr, dtrow_scr):
    L = xbc_ref.shape[1]
    Lc = xbcc_ref.shape[1]
    nch = L // CHUNK
    nchc = Lc // CHUNK
    win = CHUNK + 2 * CONV_HALO
    nconv = xbc_ref.shape[2]
    nh = SSD_HEADS

    def conv_pass(src_ref, pad_ref, dst_ref, n_chunks, length):
        zeros = jnp.zeros((CONV_HALO, nconv), F32)
        pad_ref[0:CONV_HALO, :] = zeros
        pad_ref[CONV_HALO + length:2 * CONV_HALO + length, :] = zeros
        pad_ref[CONV_HALO:CONV_HALO + length, :] = src_ref[0].astype(F32)

        def chunk(c, carry):
            base = pl.multiple_of(c * CHUNK, CHUNK)
            for cb_i in range(nconv // LANES):
                cols = slice(cb_i * LANES, (cb_i + 1) * LANES)
                w = pad_ref[pl.ds(base, win), cols]
                acc = cb_ref[:, cols] + w[CONV_HALO:CONV_HALO + CHUNK] * cw_ref[SSD_CONV // 2:SSD_CONV // 2 + 1, cols]
                for j in range(SSD_CONV):
                    if j == SSD_CONV // 2:
                        continue
                    shifted = pltpu.roll(w, (SSD_CONV // 2 - j) % win, 0)
                    acc = acc + shifted[CONV_HALO:CONV_HALO + CHUNK] * cw_ref[j:j + 1, cols]
                dst_ref[pl.ds(base, CHUNK), cols] = _silu(acc).astype(BF16)
            return carry

        lax.fori_loop(0, n_chunks, chunk, 0)

    conv_pass(xbcc_ref, xpadc, uc, nchc, Lc)
    conv_pass(xbc_ref, xpad, u, nch, L)

    a_neg = -jnp.exp(alog_ref[...])
    dtv[...] = jax.nn.softplus(dt_ref[0] + dtb_ref[...])
    dav[...] = dtv[...] * a_neg
    dtcv[...] = jax.nn.softplus(dtc_ref[0] + dtb_ref[...])
    dacv[...] = dtcv[...] * a_neg

    row_i = lax.broadcasted_iota(jnp.int32, (CHUNK, CHUNK), 0)
    col_i = lax.broadcasted_iota(jnp.int32, (CHUNK, CHUNK), 1)
    causal = col_i <= row_i
    lo_half = col_i < SSD_HEADDIM
    fwd_lane = col_i < nh
    head_of = lax.broadcasted_iota(jnp.int32, (CHUNK, SSD_WIDTH), 1) // SSD_HEADDIM
    src_col = lax.broadcasted_iota(jnp.int32, (CHUNK, SSD_WIDTH), 0)
    exp_f = (head_of == src_col).astype(BF16)
    exp_b = (head_of == src_col - nh).astype(BF16)
    exp_fb = jnp.concatenate([exp_f, exp_b], axis=1)

    def split3(v):
        hi = v.astype(BF16)
        r1 = v - hi.astype(F32)
        mid = r1.astype(BF16)
        return hi, mid, (r1 - mid.astype(F32)).astype(BF16)

    def times_onehot(v, m, passes=3):
        parts = split3(v)[:passes]
        acc = _dot(parts[0], m)
        for part in parts[1:]:
            acc = acc + _dot(part, m)
        return acc

    def colb(mat, r):
        return jnp.broadcast_to(mat[:, r:r + 1], (CHUNK, CHUNK))

    def pair_sel(a, b_):
        return jnp.where(lo_half, a, b_)

    gw = 2 * LANES

    def chunk_terms(u_ref, dt_s, da_s, base):
        dt = dt_s[pl.ds(base, CHUNK), :]
        da = da_s[pl.ds(base, CHUNK), :]
        acol = da
        for step in (1, 2, 4, 8, 16, 32, 64):
            acol = acol + jnp.where(row_i >= step, pltpu.roll(acol, step, 0), 0.0)
        ecol = acol - da
        last = acol[CHUNK - 1:CHUNK, :]
        wgt = jnp.where(fwd_lane, jnp.exp(last - acol), jnp.exp(ecol)) * dt
        scale = jnp.where(fwd_lane, jnp.exp(acol), jnp.exp(last - ecol))
        wide = times_onehot(jnp.concatenate([wgt, scale], axis=0), exp_fb, passes=1)
        dec = times_onehot(jnp.broadcast_to(jnp.exp(last), (SUBLANES, LANES)), exp_fb)[0:1]
        xs = u_ref[pl.ds(base, CHUNK), 0:SSD_WIDTH].astype(F32)
        kmats = []
        for g in range(SSD_GROUPS):
            xw = jnp.concatenate([xs[:, g * gw:(g + 1) * gw] * wide[:CHUNK, g * gw:(g + 1) * gw],
                                  xs[:, g * gw:(g + 1) * gw] * wide[:CHUNK, SSD_WIDTH + g * gw:SSD_WIDTH + (g + 1) * gw]],
                                 axis=1).astype(BF16)
            bm = u_ref[pl.ds(base, CHUNK), SSD_WIDTH + g * SSD_STATE:SSD_WIDTH + (g + 1) * SSD_STATE]
            kmats.append(_dot_tn(bm, xw))
        return dt, acol, ecol, wide[CHUNK:], dec, kmats

    def advance(s, dec, kmats, backward):
        off = SSD_WIDTH if backward else 0
        koff = gw if backward else 0
        return [dec[:, off + g * gw:off + (g + 1) * gw] * s[g] + kmats[g][:, koff:koff + gw]
                for g in range(SSD_GROUPS)]

    ctx_terms = [chunk_terms(uc, dtcv, dacv, c * CHUNK) for c in range(nchc)]
    s_f0 = [jnp.zeros((SSD_STATE, gw), F32) for _ in range(SSD_GROUPS)]
    for c in range(nchc):
        s_f0 = advance(s_f0, ctx_terms[c][4], ctx_terms[c][5], False)
    s_b0 = [jnp.zeros((SSD_STATE, gw), F32) for _ in range(SSD_GROUPS)]
    for c in reversed(range(nchc)):
        s_b0 = advance(s_b0, ctx_terms[c][4], ctx_terms[c][5], True)

    def prep(c, carry):
        base = pl.multiple_of(c * CHUNK, CHUNK)
        dt, acol, ecol, scale, dec, kmats = chunk_terms(u, dtv, dav, base)
        acum[pl.ds(base, CHUNK), :] = acol
        ecum[pl.ds(base, CHUNK), :] = ecol
        hrow = pl.ds(pl.multiple_of(c * 2 * nh, 2 * nh), 2 * nh)
        arow_scr[hrow, :] = acol.T[:2 * nh]
        erow_scr[hrow, :] = ecol.T[:2 * nh]
        dtrow_scr[hrow, :] = dt.T[:2 * nh]
        xpad[pl.ds(base, CHUNK), :] = scale
        dec_scr[pl.ds(pl.multiple_of(c * SUBLANES, SUBLANES), SUBLANES), :] = jnp.broadcast_to(dec, (SUBLANES, 2 * SSD_WIDTH))
        for g in range(SSD_GROUPS):
            sf_scr[c, g] = kmats[g][:, :gw]
            kb_scr[c, g] = kmats[g][:, gw:]
        return carry

    lax.fori_loop(0, nch, prep, 0)

    def chunk_dec(c):
        return dec_scr[pl.ds(pl.multiple_of(c * SUBLANES, SUBLANES), 1), :]

    def fwd(c, s_old):
        dec = chunk_dec(c)
        new = []
        for g in range(SSD_GROUPS):
            new.append(dec[:, g * gw:(g + 1) * gw] * s_old[g] + sf_scr[c, g])
            sf_scr[c, g] = s_old[g]
        return tuple(new)

    lax.fori_loop(0, nch, fwd, tuple(s_f0))

    def bwd(i, s_b):
        c = nch - 1 - i
        base = pl.multiple_of(c * CHUNK, CHUNK)
        acol = acum[pl.ds(base, CHUNK), :]
        ecol = ecum[pl.ds(base, CHUNK), :]
        hrow = pl.ds(pl.multiple_of(c * 2 * nh, 2 * nh), 2 * nh)
        arow = arow_scr[hrow, :]
        erow = erow_scr[hrow, :]
        dt_t = dtrow_scr[hrow, :]
        scale = xpad[pl.ds(base, CHUNK), :]
        ys = []
        for g in range(SSD_GROUPS):
            bm = u[pl.ds(base, CHUNK), SSD_WIDTH + g * SSD_STATE:SSD_WIDTH + (g + 1) * SSD_STATE]
            cm = u[pl.ds(base, CHUNK), SSD_WIDTH + (SSD_GROUPS + g) * SSD_STATE:SSD_WIDTH + (SSD_GROUPS + g + 1) * SSD_STATE]
            cbm = _dot_nt(cm, bm)
            cs_f = _dot(cm, sf_scr[c, g].astype(BF16))
            cs_b = _dot(cm, s_b[g].astype(BF16))
            for pp in range(SSD_PAIRS // SSD_GROUPS):
                p = g * (SSD_PAIRS // SSD_GROUPS) + pp
                xs_b = u[pl.ds(base, CHUNK), p * LANES:(p + 1) * LANES]
                y_h = []
                for hh in range(2):
                    r = 2 * p + hh
                    arg = jnp.where(causal, colb(acol, r) - arow[r:r + 1, :],
                                    erow[nh + r:nh + r + 1, :] - colb(ecol, nh + r))
                    coef = jnp.where(causal, dt_t[r:r + 1, :], dt_t[nh + r:nh + r + 1, :])
                    gm = (cbm * (jnp.exp(arg) * coef)).astype(BF16)
                    y_h.append(_dot(gm, xs_b))
                sl = slice(pp * LANES, (pp + 1) * LANES)
                wl = slice(p * LANES, (p + 1) * LANES)
                wlb = slice(SSD_WIDTH + p * LANES, SSD_WIDTH + (p + 1) * LANES)
                ys.append(pair_sel(y_h[0], y_h[1]) + cs_f[:, sl] * scale[:, wl] + cs_b[:, sl] * scale[:, wlb]
                          + dsk_ref[:, wl] * xs_b.astype(F32))
        y = jnp.concatenate(ys, axis=1)
        y = y * _silu(z_ref[0, pl.ds(base, CHUNK), :].astype(F32))
        ms = jnp.mean(y * y, axis=-1, keepdims=True)
        y_ref[0, pl.ds(base, CHUNK), :] = ((y * lax.rsqrt(ms + EPS)) * nw_ref[...]).astype(BF16)
        dec = chunk_dec(c)
        return tuple(dec[:, SSD_WIDTH + g * gw:SSD_WIDTH + (g + 1) * gw] * s_b[g] + kb_scr[c, g]
                     for g in range(SSD_GROUPS))

    lax.fori_loop(0, nch, bwd, tuple(s_b0))


def _ssd(xbc, z, dt, xbcc, dtc, conv_w8, conv_b, dt_bias, a_log, d_skip, norm_w):
    b, L, nconv = xbc.shape
    Lc = xbcc.shape[1]
    nch = L // CHUNK
    per_b = lambda n, w: pl.BlockSpec((1, n, w), lambda i: (i, 0, 0))
    return pl.pallas_call(
        _ssd_kernel,
        grid=(b,),
        in_specs=[
            per_b(L, nconv), per_b(L, SSD_WIDTH), per_b(L, LANES), per_b(Lc, nconv), per_b(Lc, LANES),
            _const_spec(conv_w8.shape), _const_spec(conv_b.shape), _const_spec(dt_bias.shape),
            _const_spec(a_log.shape), _const_spec(d_skip.shape), _const_spec(norm_w.shape),
        ],
        out_specs=per_b(L, SSD_WIDTH),
        out_shape=jax.ShapeDtypeStruct((b, L, SSD_WIDTH), BF16),
        scratch_shapes=[
            pltpu.VMEM((L + 2 * CONV_HALO, nconv), F32),
            pltpu.VMEM((Lc + 2 * CONV_HALO, nconv), F32),
            pltpu.VMEM((L, nconv), BF16),
            pltpu.VMEM((Lc, nconv), BF16),
            pltpu.VMEM((L, LANES), F32), pltpu.VMEM((L, LANES), F32),
            pltpu.VMEM((Lc, LANES), F32), pltpu.VMEM((Lc, LANES), F32),
            pltpu.VMEM((nch, SSD_GROUPS, SSD_STATE, 2 * LANES), F32),
            pltpu.VMEM((nch, SSD_GROUPS, SSD_STATE, 2 * LANES), F32),
            pltpu.VMEM((L, LANES), F32), pltpu.VMEM((L, LANES), F32),
            pltpu.VMEM((nch * SUBLANES, 2 * SSD_WIDTH), F32),
            pltpu.VMEM((nch * 2 * SSD_HEADS, CHUNK), F32), pltpu.VMEM((nch * 2 * SSD_HEADS, CHUNK), F32),
            pltpu.VMEM((nch * 2 * SSD_HEADS, CHUNK), F32),
        ],
        compiler_params=pltpu.CompilerParams(vmem_limit_bytes=VMEM_LIMIT_SSD),
        name="ssd",
    )(xbc, z, dt, xbcc, dtc, conv_w8, conv_b, dt_bias, a_log, d_skip, norm_w)


def _ret_kernel(q_ref, k_ref, v_ref, g_ref, kc_ref, vc_ref, df_ref, db_ref, gn_ref, y_ref, sf_scr):
    L = q_ref.shape[1]
    Lc = kc_ref.shape[1]
    nch = L // CHUNK
    dk = RET_DK
    row_i = lax.broadcasted_iota(jnp.int32, (CHUNK, dk), 0).astype(F32)
    col_i = lax.broadcasted_iota(jnp.int32, (CHUNK, dk), 1).astype(F32)
    rel = row_i - col_i
    crow = lax.broadcasted_iota(jnp.int32, (Lc, dk), 0).astype(F32)

    heads = []
    s_f0 = []
    s_b0 = []
    for h in range(RET_HEADS):
        cols = slice(h * dk, (h + 1) * dk)
        lg_f = -jnp.exp(df_ref[:, cols])
        lg_b = -jnp.exp(db_ref[:, cols])
        heads.append(dict(
            cols=cols,
            dmat=jnp.where(rel >= 0, jnp.exp(jnp.maximum(rel, 0.0) * lg_f), jnp.exp(jnp.maximum(-rel, 0.0) * lg_b)),
            dq_f=jnp.exp((row_i + 1.0) * lg_f),
            dq_b=jnp.exp((CHUNK - row_i) * lg_b),
            dk_f=jnp.exp((CHUNK - 1.0 - row_i) * lg_f),
            dk_b=jnp.exp(row_i * lg_b),
            dc_f=jnp.exp(CHUNK * lg_f),
            dc_b=jnp.exp(CHUNK * lg_b),
        ))
        kc = kc_ref[0, :, cols].astype(F32)
        vc = vc_ref[0, :, cols]
        s_f0.append(_dot_tn((kc * jnp.exp((Lc - 1.0 - crow) * lg_f)).astype(BF16), vc))
        s_b0.append(_dot_tn((kc * jnp.exp(crow * lg_b)).astype(BF16), vc))

    def fwd(c, s_f):
        base = pl.multiple_of(c * CHUNK, CHUNK)
        new = []
        for h, hd in enumerate(heads):
            sf_scr[c, h] = s_f[h]
            kk = k_ref[0, pl.ds(base, CHUNK), hd["cols"]].astype(F32)
            vv = v_ref[0, pl.ds(base, CHUNK), hd["cols"]]
            new.append(hd["dc_f"] * s_f[h] + _dot_tn((kk * hd["dk_f"]).astype(BF16), vv))
        return tuple(new)

    lax.fori_loop(0, nch, fwd, tuple(s_f0), unroll=RET_UNROLL)

    def bwd(i, s_bs):
        c = nch - 1 - i
        base = pl.multiple_of(c * CHUNK, CHUNK)
        new = []
        for h, hd in enumerate(heads):
            qq = q_ref[0, pl.ds(base, CHUNK), hd["cols"]]
            kk = k_ref[0, pl.ds(base, CHUNK), hd["cols"]]
            vv = v_ref[0, pl.ds(base, CHUNK), hd["cols"]]
            s_b = s_bs[h]
            scores = (_dot_nt(qq, kk) * hd["dmat"]).astype(BF16)
            y = (_dot(scores, vv)
                 + _dot(qq, sf_scr[c, h].astype(BF16)) * hd["dq_f"]
                 + _dot(qq, s_b.astype(BF16)) * hd["dq_b"])
            mu = jnp.mean(y, axis=-1, keepdims=True)
            yc = y - mu
            var = jnp.mean(yc * yc, axis=-1, keepdims=True)
            yn = (yc * lax.rsqrt(var + EPS)) * gn_ref[:, hd["cols"]]
            gate = _silu(g_ref[0, pl.ds(base, CHUNK), hd["cols"]].astype(F32))
            y_ref[0, pl.ds(base, CHUNK), hd["cols"]] = (yn * gate).astype(BF16)
            new.append(hd["dc_b"] * s_b + _dot_tn((kk.astype(F32) * hd["dk_b"]).astype(BF16), vv))
        return tuple(new)

    lax.fori_loop(0, nch, bwd, tuple(s_b0), unroll=RET_UNROLL)


def _retention(q, k, v, g, kc, vc, decay_f, decay_b, gn_w):
    b, L, w = q.shape
    Lc = kc.shape[1]
    nch = L // CHUNK
    per_b = lambda n: pl.BlockSpec((1, n, w), lambda i: (i, 0, 0))
    return pl.pallas_call(
        _ret_kernel,
        grid=(b,),
        in_specs=[per_b(L), per_b(L), per_b(L), per_b(L), per_b(Lc), per_b(Lc),
                  _const_spec((1, w)), _const_spec((1, w)), _const_spec((1, w))],
        out_specs=per_b(L),
        out_shape=jax.ShapeDtypeStruct((b, L, w), BF16),
        scratch_shapes=[
            pltpu.VMEM((nch, RET_HEADS, RET_DK, RET_DK), F32),
        ],
        compiler_params=pltpu.CompilerParams(vmem_limit_bytes=VMEM_LIMIT),
        name="retention",
    )(q, k, v, g, kc, vc, decay_f, decay_b, gn_w)


def _outproj_router_kernel(yr_ref, ys_ref, x_ref, g1_ref, sh2_ref, sc2_ref, npost_ref, npre_ref,
                           wor_ref, wos_ref, wr_ref, br_ref, tri_ref,
                           x1_ref, h2_ref, route_ref, slots_ref, seg_ref,
                           wcat):
    i = pl.program_id(0)

    @pl.when(i == 0)
    def _():
        wr = wr_ref[...]
        hi = wr.astype(BF16)
        wcat[:, :LANES] = hi
        wcat[:, LANES:] = (wr - hi.astype(F32)).astype(BF16)

    _route_tile(yr_ref, ys_ref, x_ref, g1_ref, sh2_ref, sc2_ref, npost_ref, npre_ref, wor_ref, wos_ref, br_ref,
                tri_ref, x1_ref, h2_ref, route_ref, slots_ref, seg_ref, wcat)


def _route_tile(yr_ref, ys_ref, x_ref, g1_ref, sh2_ref, sc2_ref, npost_ref, npre_ref, wor_ref, wos_ref, br_ref,
                tri_ref, x1_ref, h2_ref, route_ref, slots_ref, seg_ref, wcat):
    tm = x_ref.shape[0]
    rows = slice(0, tm)
    y = _dot(yr_ref[rows, :], wor_ref[...]) + _dot(ys_ref[rows, :], wos_ref[...])
    ms = jnp.mean(y * y, axis=-1, keepdims=True)
    x1 = x_ref[rows, :] + (y * lax.rsqrt(ms + EPS)) * (g1_ref[0] * npost_ref[...])
    x1_ref[rows, :] = x1
    h2 = _norm_mod(x1, npre_ref[...], sc2_ref[0], sh2_ref[0])
    d = h2.shape[1]
    h2_ref[rows, 0:d] = h2.astype(BF16)

    h_hi = h2.astype(BF16)
    h_lo = (h2 - h_hi.astype(F32)).astype(BF16)
    both = _dot(h_hi, wcat[...])
    lg = both[:, :LANES] + both[:, LANES:] + _dot(h_lo, wcat[:, :LANES]) + br_ref[...]

    lane = lax.broadcasted_iota(jnp.int32, (tm, LANES), 1)
    lane_f = lane.astype(F32)
    is_grp = (lane >= N_EXPERTS) & (lane < N_EXPERTS + MOE_GROUPS)
    gl = jnp.where(is_grp, lg, NEG_BIG)
    mg = jnp.max(gl, axis=-1, keepdims=True)
    grp_lane = jnp.min(jnp.where(gl == mg, lane_f, 1e9), axis=-1, keepdims=True)
    p_g = 1.0 / jnp.sum(jnp.where(is_grp, jnp.exp(gl - mg), 0.0), axis=-1, keepdims=True)
    first = (grp_lane - N_EXPERTS) * EXPERTS_PER_GROUP
    in_grp = (lane_f >= first) & (lane_f < first + EXPERTS_PER_GROUP)
    el = jnp.where(in_grp, lg, NEG_BIG)
    t1 = jnp.max(el, axis=-1, keepdims=True)
    i1 = jnp.min(jnp.where(el == t1, lane_f, 1e9), axis=-1, keepdims=True)
    el2 = jnp.where(lane_f == i1, NEG_BIG, el)
    t2 = jnp.max(el2, axis=-1, keepdims=True)
    i2 = jnp.min(jnp.where(el2 == t2, lane_f, 1e9), axis=-1, keepdims=True)
    s = jnp.exp(t2 - t1)
    w1 = p_g / (1.0 + s)
    w2 = p_g * s / (1.0 + s)

    oh1 = (lane_f == i1)
    oh2 = (lane_f == i2)
    ohf = jnp.where(oh1 | oh2, 1.0, 0.0)
    before = _dot(tri_ref[...], ohf.astype(BF16))
    cnt = jnp.sum(ohf, axis=0, keepdims=True)
    seg = jnp.floor((cnt + (SEG_ALIGN - 1.0)) * (1.0 / SEG_ALIGN)) * SEG_ALIGN
    e_row = lax.broadcasted_iota(jnp.int32, (LANES, LANES), 0)
    e_col = lax.broadcasted_iota(jnp.int32, (LANES, LANES), 1)
    earlier = (e_row < e_col).astype(BF16)
    seg_off = _dot(jnp.broadcast_to(seg, (SUBLANES, LANES)).astype(BF16), earlier)[0:1]
    where_to = before + seg_off
    lpos1 = jnp.sum(jnp.where(oh1, where_to, 0.0), axis=-1, keepdims=True)
    lpos2 = jnp.sum(jnp.where(oh2, where_to, 0.0), axis=-1, keepdims=True)

    cols = [w1, w2, lpos1, lpos2]
    for wk in (w1, w2):
        hi = wk.astype(BF16).astype(F32)
        cols += [hi, wk - hi]
    cols.append(jnp.ones_like(w1))
    packed = jnp.zeros((tm, LANES), F32)
    for k, col in enumerate(cols):
        packed = jnp.where(lane == k, col, packed)
    route_ref[rows, :] = packed
    h2_ref[rows, d:d + LANES] = packed.astype(BF16)

    row = lax.broadcasted_iota(jnp.int32, (tm, LANES), 0)
    on_diag = (row % LANES) == lane
    per = tm // LANES
    for qi, col in enumerate((lpos1, lpos2)):
        picked = jnp.where(on_diag, col, 0.0)
        dense = jnp.sum(picked.reshape(per, LANES, LANES), axis=1).astype(jnp.int32)
        slots_ref[0, qi * per:(qi + 1) * per, :] = dense
    tbl_row = lax.broadcasted_iota(jnp.int32, (SUBLANES, LANES), 0)
    seg_ref[0] = jnp.where(tbl_row == 0, seg, jnp.where(tbl_row == 1, seg_off, 0.0)).astype(jnp.int32)


def _outproj_router(yr, ys, x2, mod3, npost, npre, wo_r, wo_s, w_router, b_router, seq_len):
    T, d = x2.shape
    tm = TM_OUT
    per_seq = seq_len // tm
    rw = yr.shape[1]
    tri = (jnp.arange(tm)[:, None] > jnp.arange(tm)[None, :]).astype(BF16)
    tok = lambda w: pl.BlockSpec((tm, w), lambda i: (i, 0))
    modv = lambda k: pl.BlockSpec((1, 1, d), lambda i: (i // per_seq, 0, k))
    tile3 = lambda r: pl.BlockSpec((1, r, LANES), lambda i: (i, 0, 0))
    slot_rows = TOP_K * (tm // LANES)
    return pl.pallas_call(
        _outproj_router_kernel,
        grid=(T // tm,),
        in_specs=[
            tok(rw), tok(rw), tok(d), modv(2), modv(3), modv(4),
            _const_spec((1, d)), _const_spec((1, d)),
            _const_spec(wo_r.shape), _const_spec(wo_s.shape), _const_spec(w_router.shape), _const_spec((1, LANES)),
            _const_spec((tm, tm)),
        ],
        out_specs=[tok(d), tok(d + LANES), tok(LANES), tile3(slot_rows), tile3(SUBLANES)],
        out_shape=[jax.ShapeDtypeStruct((T, d), F32), jax.ShapeDtypeStruct((T, d + LANES), BF16),
                   jax.ShapeDtypeStruct((T, LANES), F32),
                   jax.ShapeDtypeStruct((T // tm, slot_rows, LANES), jnp.int32),
                   jax.ShapeDtypeStruct((T // tm, SUBLANES, LANES), jnp.int32)],
        scratch_shapes=[pltpu.VMEM((d, 2 * LANES), BF16)],
        compiler_params=pltpu.CompilerParams(dimension_semantics=("arbitrary",),
                                             vmem_limit_bytes=VMEM_LIMIT),
        name="outproj_router",
    )(yr, ys, x2, mod3, mod3, mod3, npost, npre, wo_r, wo_s, w_router, b_router, tri)


def _expert_kernel(be_ref, first_ref, slot_ref, next_ref, nused_ref, xs_ref, wg_hbm, wu_hbm, wd_hbm, y_ref,
                   wg_f, wu_f, wd_f, wg_b, wu_b, wd_b, sem):
    def fetch(e, s):
        return [pltpu.make_async_copy(src.at[e], dst.at[s], sem.at[s, k])
                for k, (src, dst) in enumerate(((wg_hbm, wg_f), (wu_hbm, wu_f), (wd_hbm, wd_f)))]

    @pl.when(pl.program_id(0) == 0)
    def _():
        for c in fetch(be_ref[0], slot_ref[0]):
            c.start()

    for sb in range(EXPERT_BLOCKS_PER_STEP):
        _expert_block(pl.program_id(0) * EXPERT_BLOCKS_PER_STEP + sb, slice(sb * MB_EXPERT, (sb + 1) * MB_EXPERT),
                      fetch, be_ref, first_ref, slot_ref, next_ref, nused_ref, xs_ref, y_ref,
                      wg_f, wu_f, wd_f, wg_b, wu_b, wd_b)


def _expert_block(i, rows, fetch, be_ref, first_ref, slot_ref, next_ref, nused_ref, xs_ref, y_ref,
                  wg_f, wu_f, wd_f, wg_b, wu_b, wd_b):
    @pl.when(first_ref[i] == 1)
    def _():
        s = slot_ref[i]

        @pl.when(next_ref[i] >= 0)
        def _():
            for c in fetch(next_ref[i], 1 - s):
                c.start()

        for c in fetch(be_ref[i], s):
            c.wait()
        wg_b[...] = wg_f[s].astype(BF16)
        wu_b[...] = wu_f[s].astype(BF16)
        wd_b[...] = wd_f[s].astype(BF16)

    @pl.when(i < nused_ref[0])
    def _():
        d = y_ref.shape[1]
        side = xs_ref[rows, d:d + LANES].astype(F32)
        second = side[:, 8:9] == 2.0
        unscale = jnp.where(second, 0.5, 1.0)
        weight = jnp.where(second, side[:, 6:7] + side[:, 7:8], side[:, 4:5] + side[:, 5:6]) * unscale
        xb = xs_ref[rows, 0:d] * unscale.astype(BF16)
        hid = (_silu(_dot(xb, wg_b[...])) * _dot(xb, wu_b[...])).astype(BF16)
        y_ref[rows, :] = (_dot(hid, wd_b[...]) * weight).astype(BF16)

    @pl.when(i >= nused_ref[0])
    def _():
        y_ref[rows, :] = jnp.zeros((rows.stop - rows.start, y_ref.shape[1]), y_ref.dtype)


def _expert_plan(padded, pad_end, n_blocks, mb):
    n_used = (pad_end[-1:] // mb).astype(jnp.int32)
    blk_start = jnp.arange(n_blocks, dtype=jnp.int32) * mb
    experts = jnp.arange(N_EXPERTS, dtype=jnp.int32)
    blk_expert = jnp.minimum(jnp.sum((pad_end[None, :] <= blk_start[:, None]).astype(jnp.int32), axis=1),
                             N_EXPERTS - 1)
    prev = jnp.concatenate([jnp.full((1,), -1, jnp.int32), blk_expert[:-1]])
    first = ((blk_start < pad_end[-1]) & (blk_expert != prev)).astype(jnp.int32)
    slot = (jnp.cumsum(first) - 1) % 2
    later = jnp.where((padded > 0)[None, :] & (experts[None, :] > experts[:, None]), experts[None, :], N_EXPERTS)
    next_nonempty = jnp.min(later, axis=1)
    next_nonempty = jnp.where(next_nonempty == N_EXPERTS, -1, next_nonempty)
    nxt = jnp.sum(jnp.where(blk_expert[:, None] == experts[None, :], next_nonempty[None, :], 0), axis=1)
    return [a.astype(jnp.int32) for a in (blk_expert, first, slot, nxt, n_used)]


def _experts(plan, xs, w_gate, w_up, w_down):
    cap, dp = xs.shape
    n_exp, d, de = w_gate.shape
    step_rows = MB_EXPERT * EXPERT_BLOCKS_PER_STEP
    assert dp == d + LANES and cap % step_rows == 0
    grid_spec = pltpu.PrefetchScalarGridSpec(
        num_scalar_prefetch=len(plan),
        grid=(cap // step_rows,),
        in_specs=[
            pl.BlockSpec((step_rows, dp), lambda i, be, fi, sl, nx, nu: (
                jnp.minimum(i, (nu[0] - 1) // EXPERT_BLOCKS_PER_STEP), 0)),
            pl.BlockSpec(memory_space=pl.ANY), pl.BlockSpec(memory_space=pl.ANY), pl.BlockSpec(memory_space=pl.ANY),
        ],
        out_specs=pl.BlockSpec((step_rows, d), lambda i, be, fi, sl, nx, nu: (i, 0)),
        scratch_shapes=[pltpu.VMEM((2, d, de), F32), pltpu.VMEM((2, d, de), F32), pltpu.VMEM((2, de, d), F32),
                        pltpu.VMEM((d, de), BF16), pltpu.VMEM((d, de), BF16), pltpu.VMEM((de, d), BF16),
                        pltpu.SemaphoreType.DMA((2, 3))],
    )
    return pl.pallas_call(
        _expert_kernel,
        grid_spec=grid_spec,
        out_shape=jax.ShapeDtypeStruct((cap, d), BF16),
        compiler_params=pltpu.CompilerParams(dimension_semantics=("arbitrary",),
                                             vmem_limit_bytes=VMEM_LIMIT),
        name="experts",
    )(*plan, xs, w_gate, w_up, w_down)


def _segment_pieces(tile, segrow_ref, seglen_ref, segoff_ref, act):
    def per_expert(e, carry):
        idx = tile * N_EXPERTS + e
        g0 = segrow_ref[idx]
        l0 = segoff_ref[idx]

        def piece(j, c2):
            act(pl.multiple_of(l0 + j * SEG_ALIGN, SEG_ALIGN), pl.multiple_of(g0 + j * SEG_ALIGN, SEG_ALIGN))
            return c2

        lax.fori_loop(0, seglen_ref[idx], piece, 0)
        return carry

    lax.fori_loop(0, N_EXPERTS, per_expert, 0)


def _wait_rows(total, row_copy):
    size = SEG_ALIGN
    while size <= _stage_rows(max(TD_DISPATCH, TF_COMBINE)):
        @pl.when((total & size) != 0)
        def _(size=size):
            row_copy(size).wait()
        size *= 2


def _for_tile_rows(tile_rows, max_rows, body):
    @pl.when(tile_rows <= COMMON_STAGE_ROWS)
    def _():
        body(COMMON_STAGE_ROWS)

    @pl.when(tile_rows > COMMON_STAGE_ROWS)
    def _():
        body(max_rows)


def _stage_rows(tile_tokens):
    return TOP_K * tile_tokens + N_EXPERTS * SEG_ALIGN


def _dispatch_seg_kernel(pad_end_ref, zero_from_ref, segrow_ref, seglen_ref, segoff_ref, tilerows_ref,
                         lpos_ref, h_ref, xs_hbm, zbuf, stage, sem, zsem):
    i = pl.program_id(0)
    nt = pl.num_programs(0)
    td = h_ref.shape[0]
    sr = stage.shape[1]
    per = td // LANES

    def zero_fills(act):
        def fill(row, n):
            act(pltpu.make_async_copy(zbuf.at[pl.ds(0, n)], xs_hbm.at[pl.ds(pl.multiple_of(row, SEG_ALIGN), n)], zsem))

        def region(e, carry):
            row = zero_from_ref[e]
            left = pad_end_ref[e] - row
            size = ZERO_PIECE
            while size >= SEG_ALIGN:
                take = (left & size) != 0

                @pl.when(take)
                def _(row=row, size=size):
                    fill(row, size)

                row = row + jnp.where(take, size, 0)
                size //= 2
            return carry

        def tail(p, carry):
            @pl.when(p * ZERO_PIECE >= pad_end_ref[N_EXPERTS - 1])
            def _():
                fill(p * ZERO_PIECE, ZERO_PIECE)
            return carry

        lax.fori_loop(0, N_EXPERTS, region, 0)
        lax.fori_loop(0, xs_hbm.shape[0] // ZERO_PIECE, tail, 0)

    @pl.when(i == 0)
    def _():
        zbuf[...] = jnp.zeros_like(zbuf)
        zero_fills(lambda c: c.start())

    def shipped(tile):
        _wait_rows(tilerows_ref[tile], lambda n: pltpu.make_async_copy(
            stage.at[tile % 2, pl.ds(0, n)], xs_hbm.at[pl.ds(0, n)], sem.at[tile % 2]))

    @pl.when(i >= 2)
    def _():
        shipped(i - 2)

    slot = i % 2

    def sort_rows(n_rows):
        srow = lax.broadcasted_iota(jnp.int32, (n_rows, LANES), 0)
        place = jnp.concatenate(
            [jnp.where(srow == lpos_ref[0, cb:cb + 1, :], 1.0,
                       jnp.where(srow == lpos_ref[0, per + cb:per + cb + 1, :], 2.0, 0.0)) for cb in range(per)],
            axis=1).astype(BF16)
        stage[slot, 0:n_rows, :] = _dot(place, h_ref[...]).astype(BF16)

    _for_tile_rows(tilerows_ref[i], sr, sort_rows)
    _segment_pieces(i, segrow_ref, seglen_ref, segoff_ref, lambda lr, gr: pltpu.make_async_copy(
        stage.at[slot, pl.ds(lr, SEG_ALIGN)], xs_hbm.at[pl.ds(gr, SEG_ALIGN)], sem.at[slot]).start())

    @pl.when(i == nt - 1)
    def _():
        @pl.when(i >= 1)
        def _():
            shipped(i - 1)
        shipped(i)
        zero_fills(lambda c: c.wait())


def _dispatch_seg(plan, lpos, h2, cap):
    T, width = h2.shape
    td = TD_DISPATCH
    nt = T // td
    sr = _stage_rows(td)
    n_pre = len(plan)
    grid_spec = pltpu.PrefetchScalarGridSpec(
        num_scalar_prefetch=n_pre,
        grid=(nt,),
        in_specs=[
            pl.BlockSpec((1, lpos.shape[1], LANES), lambda i, *_: (i, 0, 0)),
            pl.BlockSpec((td, width), lambda i, *_: (i, 0)),
        ],
        out_specs=pl.BlockSpec(memory_space=pl.ANY),
        scratch_shapes=[pltpu.VMEM((ZERO_PIECE, width), BF16), pltpu.VMEM((2, sr, width), BF16),
                        pltpu.SemaphoreType.DMA((2,)), pltpu.SemaphoreType.DMA(())],
    )
    return pl.pallas_call(
        _dispatch_seg_kernel,
        grid_spec=grid_spec,
        out_shape=jax.ShapeDtypeStruct((cap, width), BF16),
        compiler_params=pltpu.CompilerParams(dimension_semantics=("arbitrary",), vmem_limit_bytes=VMEM_LIMIT),
        name="dispatch",
    )(*plan, lpos, h2)


def _combine_seg_kernel(segrow_ref, seglen_ref, segoff_ref, tilerows_ref, route_ref, x1_ref, g2_ref, nw_ref,
                        yb_hbm, o_ref, stage, sem):
    i = pl.program_id(0)
    nt = pl.num_programs(0)
    tf = x1_ref.shape[0]
    sr = stage.shape[1]

    def fetch(tile, slot):
        _segment_pieces(tile, segrow_ref, seglen_ref, segoff_ref, lambda lr, gr: pltpu.make_async_copy(
            yb_hbm.at[pl.ds(gr, SEG_ALIGN)], stage.at[slot, pl.ds(lr, SEG_ALIGN)], sem.at[slot]).start())

    @pl.when(i == 0)
    def _():
        stage[...] = jnp.zeros_like(stage)
        fetch(i, 0)

    for slot in range(2):
        @pl.when((i + 1 < nt) & (i % 2 != slot))
        def _(slot=slot):
            fetch(i + 1, slot)

    _wait_rows(tilerows_ref[i], lambda n: pltpu.make_async_copy(
        yb_hbm.at[pl.ds(0, n)], stage.at[i % 2, pl.ds(0, n)], sem.at[i % 2]))

    local_row = route_ref[:, 2:2 + TOP_K].astype(jnp.int32)

    def unsort_rows(n_rows):
        scol = lax.broadcasted_iota(jnp.int32, (tf, n_rows), 1)
        pick = jnp.where((scol == local_row[:, 0:1]) | (scol == local_row[:, 1:2]), 1.0, 0.0).astype(BF16)
        out = _dot(pick, stage[i % 2, 0:n_rows, :])
        ms = jnp.mean(out * out, axis=-1, keepdims=True)
        o_ref[...] = x1_ref[...] + g2_ref[0] * ((out * lax.rsqrt(ms + EPS)) * nw_ref[...])

    _for_tile_rows(tilerows_ref[i], sr, unsort_rows)


def _combine_seg(seg_plan, route, x1, mod3, nw, yb, seq_len):
    T, d = x1.shape
    tf = TF_COMBINE
    nt = T // tf
    per_seq = seq_len // tf
    grid_spec = pltpu.PrefetchScalarGridSpec(
        num_scalar_prefetch=len(seg_plan),
        grid=(nt,),
        in_specs=[
            pl.BlockSpec((tf, LANES), lambda i, *_: (i, 0)),
            pl.BlockSpec((tf, d), lambda i, *_: (i, 0)),
            pl.BlockSpec((1, 1, d), lambda i, *_: (i // per_seq, 0, 5)),
            pl.BlockSpec((1, d), lambda i, *_: (0, 0)),
            pl.BlockSpec(memory_space=pl.ANY),
        ],
        out_specs=pl.BlockSpec((tf, d), lambda i, *_: (i, 0)),
        scratch_shapes=[pltpu.VMEM((2, _stage_rows(tf), d), yb.dtype), pltpu.SemaphoreType.DMA((2,))],
    )
    return pl.pallas_call(
        _combine_seg_kernel,
        grid_spec=grid_spec,
        out_shape=jax.ShapeDtypeStruct((T, d), F32),
        compiler_params=pltpu.CompilerParams(dimension_semantics=("arbitrary",), vmem_limit_bytes=VMEM_LIMIT),
        name="combine",
    )(*seg_plan, route, x1, mod3, nw, yb)


def _rope_tables(L, n_heads):
    quarter = RET_DK // 4
    freqs = ROPE_BASE ** (-jnp.arange(quarter, dtype=F32) / quarter)
    t = jnp.arange(L)
    ang_r = (t // GRID_W).astype(F32)[:, None] * freqs
    ang_c = (t % GRID_W).astype(F32)[:, None] * freqs
    cos = jnp.concatenate([jnp.cos(ang_r)] * 2 + [jnp.cos(ang_c)] * 2, axis=-1)
    sin = jnp.concatenate([-jnp.sin(ang_r), jnp.sin(ang_r), -jnp.sin(ang_c), jnp.sin(ang_c)], axis=-1)
    return jnp.tile(cos, (1, n_heads)), jnp.tile(sin, (1, n_heads))


def _lane_pad(v, width=LANES):
    return jnp.pad(v, [(0, 0)] * (v.ndim - 1) + [(0, width - v.shape[-1])])


def kernel(x, c, ctx, c_ctx, w_mod, b_mod, norm_pre_mix, norm_post_mix, norm_pre_ffn, norm_post_ffn, w_in, w_out, ret_decay_f, ret_decay_b, ret_gn_w, ssd_conv_w, ssd_conv_b, ssd_dt_bias_f, ssd_dt_bias_b, ssd_a_log_f, ssd_a_log_b, ssd_d, ssd_norm_w, moe_w_rg, moe_b_rg, moe_w_re, moe_b_re, moe_w_gate, moe_w_up, moe_w_down):
    b, L, d = x.shape
    assert w_mod.shape[0] == 1, "single layer: context outputs are never needed"
    assert TM_OUT == TD_DISPATCH == TF_COMBINE, "router, dispatch and combine share one slot-row layout"
    rw = RET_HEADS * RET_DK
    nconv = SSD_WIDTH + 2 * SSD_GROUPS * SSD_STATE
    T = b * L

    mod_rows = -(-(b + 1) // SUBLANES) * SUBLANES
    c_all = jnp.zeros((mod_rows, d), F32).at[:b].set(c).at[b].set(c_ctx)
    mod3 = _modulation(c_all, w_mod[0], b_mod[0]).reshape(mod_rows, 1, 6 * d)

    wi = w_in[0]
    o = 0
    wq = wi[:, o:o + rw]; o += rw
    wk = wi[:, o:o + rw]; o += rw
    wv = wi[:, o:o + rw]; o += rw
    wg = wi[:, o:o + rw]; o += rw
    wz = wi[:, o:o + SSD_WIDTH]; o += SSD_WIDTH
    wxbc = wi[:, o:o + nconv].astype(BF16); o += nconv
    wdt = _lane_pad(wi[:, o:o + 2 * SSD_HEADS]).astype(BF16)
    wqk = jnp.concatenate([wq, wk], axis=1).astype(BF16)
    wvgz = jnp.concatenate([wv, wg, wz], axis=1).astype(BF16)
    cos_t, sin_t = _rope_tables(L, RET_HEADS)
    nw1 = norm_pre_mix[0].reshape(1, d)

    q, k, v, g, z, xbc, dt = _inproj(x, mod3, nw1, wqk, wvgz, wxbc, wdt, cos_t, sin_t)
    kc, vc, xbcc, dtc = _inproj_ctx(ctx, mod3, b, nw1, wk.astype(BF16), wv.astype(BF16), wxbc, wdt)

    conv_w8 = jnp.pad(ssd_conv_w[0], ((0, SUBLANES - SSD_CONV), (0, 0)))
    dt_bias = _lane_pad(jnp.concatenate([ssd_dt_bias_f[0], ssd_dt_bias_b[0]])[None, :])
    a_log = _lane_pad(jnp.concatenate([ssd_a_log_f[0], ssd_a_log_b[0]])[None, :])
    d_skip = jnp.repeat(ssd_d[0], SSD_HEADDIM)[None, :]
    ys = _ssd(xbc, z, dt, xbcc, dtc, conv_w8, ssd_conv_b[0][None, :], dt_bias, a_log, d_skip,
              ssd_norm_w[0][None, :])

    yr = _retention(q, k, v, g, kc, vc,
                    jnp.repeat(ret_decay_f[0], RET_DK)[None, :], jnp.repeat(ret_decay_b[0], RET_DK)[None, :],
                    ret_gn_w[0][None, :])

    wo = w_out[0].astype(BF16)
    w_router = _lane_pad(jnp.concatenate(
        [jnp.transpose(moe_w_re[0], (1, 0, 2)).reshape(d, N_EXPERTS), moe_w_rg[0]], axis=1))
    b_router = _lane_pad(jnp.concatenate([moe_b_re[0].reshape(-1), moe_b_rg[0]])[None, :])
    x1, h2, route, lpos, seg = _outproj_router(
        yr.reshape(T, rw), ys.reshape(T, SSD_WIDTH), x.reshape(T, d), mod3,
        norm_post_mix[0][None, :], norm_pre_ffn[0][None, :], wo[:rw], wo[rw:], w_router, b_router, L)

    mb = MB_EXPERT
    nt = T // TM_OUT
    n_blocks = -(-(T * TOP_K + nt * N_EXPERTS * (SEG_ALIGN - 1) + N_EXPERTS * (mb - 1)) // mb)
    n_blocks = -(-n_blocks // EXPERT_BLOCKS_PER_STEP) * EXPERT_BLOCKS_PER_STEP
    seg_len = seg[:, 0, :N_EXPERTS]
    seg_off = seg[:, 1, :N_EXPERTS]
    used = jnp.sum(seg_len, axis=0)
    padded = (used + mb - 1) // mb * mb
    pad_end = jnp.cumsum(padded)
    pad_start = pad_end - padded
    seg_row = pad_start[None, :] + jnp.cumsum(seg_len, axis=0) - seg_len
    seg_plan = [a.reshape(-1).astype(jnp.int32)
                for a in (seg_row, seg_len // SEG_ALIGN, seg_off, jnp.sum(seg_len, axis=1))]
    zero_from = (pad_start + used).astype(jnp.int32)

    xs = _dispatch_seg([pad_end.astype(jnp.int32), zero_from] + seg_plan, lpos, h2, n_blocks * mb)
    yb = _experts(_expert_plan(padded, pad_end, n_blocks, mb), xs, moe_w_gate[0], moe_w_up[0], moe_w_down[0])
    out = _combine_seg(seg_plan, route, x1, mod3, norm_post_ffn[0][None, :], yb, L)
    return out.reshape(b, L, d)
```

```python
import jax
import jax.numpy as jnp
from jax import lax
from jax.experimental import pallas as pl
from jax.experimental.pallas import tpu as pltpu

F32 = jnp.float32
BF16 = jnp.bfloat16

LANES = 128
SUBLANES = 8
BF16_TILE_ROWS = 16
V7X_VMEM_BYTES = 64 * 1024 * 1024
VMEM_LIMIT = V7X_VMEM_BYTES * 3 // 4
VMEM_LIMIT_SSD = V7X_VMEM_BYTES * 7 // 8

EPS = 1e-6
CHUNK = 128
GRID_W = 64
RET_HEADS = 4
RET_DK = 128
ROPE_BASE = 10000.0
SSD_HEADS = 8
SSD_HEADDIM = 64
SSD_GROUPS = 2
SSD_STATE = 128
SSD_WIDTH = SSD_HEADS * SSD_HEADDIM
SSD_CONV = 5
SSD_PAIRS = SSD_WIDTH // LANES
MOE_GROUPS = 4
EXPERTS_PER_GROUP = 8
N_EXPERTS = MOE_GROUPS * EXPERTS_PER_GROUP
TOP_K = 2
CONV_HALO = SUBLANES

TM_PROJ = 512
TM_OUT = 512
TD_DISPATCH = TM_OUT
MB_EXPERT = 512
EXPERT_BLOCKS_PER_STEP = 4
ZERO_PIECE = MB_EXPERT // 2
TF_COMBINE = TM_OUT
SEG_ALIGN = BF16_TILE_ROWS
COMMON_STAGE_ROWS = TOP_K * TM_OUT + N_EXPERTS * 10
RET_UNROLL = 8
SSD_PREP_UNROLL = 8
SSD_OUT_UNROLL = 2
NEG_BIG = -1e30


def _silu(v):
    return v * jax.nn.sigmoid(v)


def _dot(a, b):
    return jnp.dot(a, b, preferred_element_type=F32)


def _dot_tn(a, b):
    return lax.dot_general(a, b, (((0,), (0,)), ((), ())), preferred_element_type=F32)


def _dot_nt(a, b):
    return lax.dot_general(a, b, (((1,), (1,)), ((), ())), preferred_element_type=F32)


def _mod_kernel(c_ref, w_ref, b_ref, o_ref):
    a = _silu(c_ref[...])
    w = w_ref[...]
    a_hi = a.astype(BF16)
    a_lo = (a - a_hi.astype(F32)).astype(BF16)
    w_hi = w.astype(BF16)
    w_lo = (w - w_hi.astype(F32)).astype(BF16)
    o_ref[...] = _dot(a_hi, w_hi) + _dot(a_lo, w_hi) + _dot(a_hi, w_lo) + b_ref[...]


def _modulation(c_all, w_mod, b_mod):
    rows, d = c_all.shape
    n = w_mod.shape[1]
    return pl.pallas_call(
        _mod_kernel,
        grid=(n // d,),
        in_specs=[
            pl.BlockSpec((rows, d), lambda j: (0, 0)),
            pl.BlockSpec((d, d), lambda j: (0, j)),
            pl.BlockSpec((1, d), lambda j: (0, j)),
        ],
        out_specs=pl.BlockSpec((rows, d), lambda j: (0, j)),
        out_shape=jax.ShapeDtypeStruct((rows, n), F32),
        name="modulation",
    )(c_all, w_mod, b_mod.reshape(1, n))


def _norm_mod(x, nw, sc, sh):
    ms = jnp.mean(x * x, axis=-1, keepdims=True)
    return (x * lax.rsqrt(ms + EPS)) * (nw * (1.0 + sc)) + sh


def _rope(t, cos, sin_signed, first_half):
    width = t.shape[-1]
    quarter = RET_DK // 4
    swapped = jnp.where(first_half, pltpu.roll(t, width - quarter, 1), pltpu.roll(t, quarter, 1))
    return t * cos + swapped * sin_signed


def _inproj_kernel(x_ref, sh_ref, sc_ref, nw_ref, wqk_ref, wvgz_ref, wxbc_ref, wdt_ref, cos_ref, sin_ref,
                   q_ref, k_ref, v_ref, g_ref, z_ref, xbc_ref, dt_ref):
    hb = _norm_mod(x_ref[0], nw_ref[...], sc_ref[0], sh_ref[0]).astype(BF16)
    rw = q_ref.shape[-1]
    qk = _dot(hb, wqk_ref[...])
    cos = cos_ref[...]
    sin = sin_ref[...]
    lane = lax.broadcasted_iota(jnp.int32, cos.shape, 1)
    first_half = (lane % (RET_DK // 2)) < (RET_DK // 4)
    q_ref[0] = _rope(qk[:, :rw], cos, sin, first_half).astype(BF16)
    k_ref[0] = (_rope(qk[:, rw:], cos, sin, first_half) * (RET_DK ** -0.5)).astype(BF16)
    vgz = _dot(hb, wvgz_ref[...])
    v_ref[0] = vgz[:, :rw].astype(BF16)
    g_ref[0] = vgz[:, rw:2 * rw].astype(BF16)
    z_ref[0] = vgz[:, 2 * rw:].astype(BF16)
    xbc_ref[0] = _dot(hb, wxbc_ref[...]).astype(BF16)
    dt_ref[0] = _dot(hb, wdt_ref[...])


def _inproj_ctx_kernel(x_ref, sh_ref, sc_ref, nw_ref, wk_ref, wv_ref, wxbc_ref, wdt_ref,
                       k_ref, v_ref, xbc_ref, dt_ref):
    hb = _norm_mod(x_ref[0], nw_ref[...], sc_ref[0], sh_ref[0]).astype(BF16)
    k_ref[0] = (_dot(hb, wk_ref[...]) * (RET_DK ** -0.5)).astype(BF16)
    v_ref[0] = _dot(hb, wv_ref[...]).astype(BF16)
    xbc_ref[0] = _dot(hb, wxbc_ref[...]).astype(BF16)
    dt_ref[0] = _dot(hb, wdt_ref[...])


def _const_spec(shape):
    nd = len(shape)
    return pl.BlockSpec(shape, lambda *_: (0,) * nd)


def _inproj(x, mod3, nw, wqk, wvgz, wxbc, wdt, cos_t, sin_t):
    b, L, d = x.shape
    tm = min(TM_PROJ, L)
    rw = wqk.shape[1] // 2
    tok = lambda w: pl.BlockSpec((1, tm, w), lambda i, j: (i, j, 0))
    out_bf = lambda w: jax.ShapeDtypeStruct((b, L, w), BF16)
    return pl.pallas_call(
        _inproj_kernel,
        grid=(b, L // tm),
        in_specs=[
            tok(d),
            pl.BlockSpec((1, 1, d), lambda i, j: (i, 0, 0)),
            pl.BlockSpec((1, 1, d), lambda i, j: (i, 0, 1)),
            _const_spec((1, d)),
            _const_spec(wqk.shape), _const_spec(wvgz.shape), _const_spec(wxbc.shape), _const_spec(wdt.shape),
            pl.BlockSpec((tm, rw), lambda i, j: (j, 0)),
            pl.BlockSpec((tm, rw), lambda i, j: (j, 0)),
        ],
        out_specs=[tok(rw), tok(rw), tok(rw), tok(rw), tok(rw), tok(wxbc.shape[1]), tok(LANES)],
        out_shape=[out_bf(rw), out_bf(rw), out_bf(rw), out_bf(rw), out_bf(rw), out_bf(wxbc.shape[1]),
                   jax.ShapeDtypeStruct((b, L, LANES), F32)],
        compiler_params=pltpu.CompilerParams(vmem_limit_bytes=VMEM_LIMIT),
        name="inproj",
    )(x, mod3, mod3, nw, wqk, wvgz, wxbc, wdt, cos_t, sin_t)


def _inproj_ctx(ctx, mod3, ctx_row, nw, wk, wv, wxbc, wdt):
    b, L, d = ctx.shape
    tm = min(TM_PROJ, L)
    rw = wk.shape[1]
    tok = lambda w: pl.BlockSpec((1, tm, w), lambda i, j: (i, j, 0))
    out_bf = lambda w: jax.ShapeDtypeStruct((b, L, w), BF16)
    return pl.pallas_call(
        _inproj_ctx_kernel,
        grid=(b, L // tm),
        in_specs=[
            tok(d),
            pl.BlockSpec((1, 1, d), lambda i, j: (ctx_row, 0, 0)),
            pl.BlockSpec((1, 1, d), lambda i, j: (ctx_row, 0, 1)),
            _const_spec((1, d)),
            _const_spec(wk.shape), _const_spec(wv.shape), _const_spec(wxbc.shape), _const_spec(wdt.shape),
        ],
        out_specs=[tok(rw), tok(rw), tok(wxbc.shape[1]), tok(LANES)],
        out_shape=[out_bf(rw), out_bf(rw), out_bf(wxbc.shape[1]), jax.ShapeDtypeStruct((b, L, LANES), F32)],
        compiler_params=pltpu.CompilerParams(vmem_limit_bytes=VMEM_LIMIT),
        name="inproj_ctx",
    )(ctx, mod3, mod3, nw, wk, wv, wxbc, wdt)


def _ssd_kernel(xbc_ref, z_ref, dt_ref, xbcc_ref, dtc_ref, cw_ref, cb_ref, dtb_ref, alog_ref, dsk_ref, nw_ref,
                y_ref,
                xpad, xpadc, u, uc, dtv, dav, dtcv, dacv, sf_scr, kb_scr, acum, ecum, dec_scr,
                arow_scr, erow_scr, dtrow_scr):
    L = xbc_ref.shape[1]
    Lc = xbcc_ref.shape[1]
    nch = L // CHUNK
    nchc = Lc // CHUNK
    win = CHUNK + 2 * CONV_HALO
    nconv = xbc_ref.shape[2]
    nh = SSD_HEADS

    def conv_pass(src_ref, pad_ref, dst_ref, n_chunks, length):
        zeros = jnp.zeros((CONV_HALO, nconv), F32)
        pad_ref[0:CONV_HALO, :] = zeros
        pad_ref[CONV_HALO + length:2 * CONV_HALO + length, :] = zeros
        pad_ref[CONV_HALO:CONV_HALO + length, :] = src_ref[0].astype(F32)

        def chunk(c, carry):
            base = pl.multiple_of(c * CHUNK, CHUNK)
            for cb_i in range(nconv // LANES):
                cols = slice(cb_i * LANES, (cb_i + 1) * LANES)
                w = pad_ref[pl.ds(base, win), cols]
                acc = cb_ref[:, cols] + w[CONV_HALO:CONV_HALO + CHUNK] * cw_ref[SSD_CONV // 2:SSD_CONV // 2 + 1, cols]
                for j in range(SSD_CONV):
                    if j == SSD_CONV // 2:
                        continue
                    shifted = pltpu.roll(w, (SSD_CONV // 2 - j) % win, 0)
                    acc = acc + shifted[CONV_HALO:CONV_HALO + CHUNK] * cw_ref[j:j + 1, cols]
                dst_ref[pl.ds(base, CHUNK), cols] = _silu(acc).astype(BF16)
            return carry

        lax.fori_loop(0, n_chunks, chunk, 0)

    conv_pass(xbcc_ref, xpadc, uc, nchc, Lc)
    conv_pass(xbc_ref, xpad, u, nch, L)

    a_neg = -jnp.exp(alog_ref[...])
    dtv[...] = jax.nn.softplus(dt_ref[0] + dtb_ref[...])
    dav[...] = dtv[...] * a_neg
    dtcv[...] = jax.nn.softplus(dtc_ref[0] + dtb_ref[...])
    dacv[...] = dtcv[...] * a_neg

    row_i = lax.broadcasted_iota(jnp.int32, (CHUNK, CHUNK), 0)
    col_i = lax.broadcasted_iota(jnp.int32, (CHUNK, CHUNK), 1)
    causal = col_i <= row_i
    lo_half = col_i < SSD_HEADDIM
    fwd_lane = col_i < nh
    head_of = lax.broadcasted_iota(jnp.int32, (CHUNK, SSD_WIDTH), 1) // SSD_HEADDIM
    src_col = lax.broadcasted_iota(jnp.int32, (CHUNK, SSD_WIDTH), 0)
    exp_f = (head_of == src_col).astype(BF16)
    exp_b = (head_of == src_col - nh).astype(BF16)
    exp_fb = jnp.concatenate([exp_f, exp_b], axis=1)

    def split3(v):
        hi = v.astype(BF16)
        r1 = v - hi.astype(F32)
        mid = r1.astype(BF16)
        return hi, mid, (r1 - mid.astype(F32)).astype(BF16)

    def times_onehot(v, m, passes=3):
        parts = split3(v)[:passes]
        acc = _dot(parts[0], m)
        for part in parts[1:]:
            acc = acc + _dot(part, m)
        return acc

    def colb(mat, r):
        return jnp.broadcast_to(mat[:, r:r + 1], (CHUNK, CHUNK))

    def pair_sel(a, b_):
        return jnp.where(lo_half, a, b_)

    gw = 2 * LANES

    def chunk_terms(u_ref, dt_s, da_s, base):
        dt = dt_s[pl.ds(base, CHUNK), :]
        da = da_s[pl.ds(base, CHUNK), :]
        acol = da
        for step in (1, 2, 4, 8, 16, 32, 64):
            acol = acol + jnp.where(row_i >= step, pltpu.roll(acol, step, 0), 0.0)
        ecol = acol - da
        last = acol[CHUNK - 1:CHUNK, :]
        wgt = jnp.where(fwd_lane, jnp.exp(last - acol), jnp.exp(ecol)) * dt
        scale = jnp.where(fwd_lane, jnp.exp(acol), jnp.exp(last - ecol))
        wide = times_onehot(jnp.concatenate([wgt, scale], axis=0), exp_fb, passes=1)
        dec = times_onehot(jnp.broadcast_to(jnp.exp(last), (SUBLANES, LANES)), exp_fb)[0:1]
        xs = u_ref[pl.ds(base, CHUNK), 0:SSD_WIDTH].astype(F32)
        kmats = []
        for g in range(SSD_GROUPS):
            xw = jnp.concatenate([xs[:, g * gw:(g + 1) * gw] * wide[:CHUNK, g * gw:(g + 1) * gw],
                                  xs[:, g * gw:(g + 1) * gw] * wide[:CHUNK, SSD_WIDTH + g * gw:SSD_WIDTH + (g + 1) * gw]],
                                 axis=1).astype(BF16)
            bm = u_ref[pl.ds(base, CHUNK), SSD_WIDTH + g * SSD_STATE:SSD_WIDTH + (g + 1) * SSD_STATE]
            kmats.append(_dot_tn(bm, xw))
        return dt, acol, ecol, wide[CHUNK:], dec, kmats

    def advance(s, dec, kmats, backward):
        off = SSD_WIDTH if backward else 0
        koff = gw if backward else 0
        return [dec[:, off + g * gw:off + (g + 1) * gw] * s[g] + kmats[g][:, koff:koff + gw]
                for g in range(SSD_GROUPS)]

    ctx_terms = [chunk_terms(uc, dtcv, dacv, c * CHUNK) for c in range(nchc)]
    s_f0 = [jnp.zeros((SSD_STATE, gw), F32) for _ in range(SSD_GROUPS)]
    for c in range(nchc):
        s_f0 = advance(s_f0, ctx_terms[c][4], ctx_terms[c][5], False)
    s_b0 = [jnp.zeros((SSD_STATE, gw), F32) for _ in range(SSD_GROUPS)]
    for c in reversed(range(nchc)):
        s_b0 = advance(s_b0, ctx_terms[c][4], ctx_terms[c][5], True)

    def prep(c, carry):
        base = pl.multiple_of(c * CHUNK, CHUNK)
        dt, acol, ecol, scale, dec, kmats = chunk_terms(u, dtv, dav, base)
        acum[pl.ds(base, CHUNK), :] = acol
        ecum[pl.ds(base, CHUNK), :] = ecol
        hrow = pl.ds(pl.multiple_of(c * 2 * nh, 2 * nh), 2 * nh)
        arow_scr[hrow, :] = acol.T[:2 * nh]
        erow_scr[hrow, :] = ecol.T[:2 * nh]
        dtrow_scr[hrow, :] = dt.T[:2 * nh]
        xpad[pl.ds(base, CHUNK), :] = scale
        dec_scr[pl.ds(pl.multiple_of(c * SUBLANES, SUBLANES), SUBLANES), :] = jnp.broadcast_to(dec, (SUBLANES, 2 * SSD_WIDTH))
        for g in range(SSD_GROUPS):
            sf_scr[c, g] = kmats[g][:, :gw]
            kb_scr[c, g] = kmats[g][:, gw:]
        return carry

    lax.fori_loop(0, nch, prep, 0, unroll=SSD_PREP_UNROLL)

    def chunk_dec(c):
        return dec_scr[pl.ds(pl.multiple_of(c * SUBLANES, SUBLANES), 1), :]

    def fwd(c, s_old):
        dec = chunk_dec(c)
        new = []
        for g in range(SSD_GROUPS):
            new.append(dec[:, g * gw:(g + 1) * gw] * s_old[g] + sf_scr[c, g])
            sf_scr[c, g] = s_old[g]
        return tuple(new)

    lax.fori_loop(0, nch, fwd, tuple(s_f0))

    def bwd(i, s_b):
        c = nch - 1 - i
        base = pl.multiple_of(c * CHUNK, CHUNK)
        acol = acum[pl.ds(base, CHUNK), :]
        ecol = ecum[pl.ds(base, CHUNK), :]
        hrow = pl.ds(pl.multiple_of(c * 2 * nh, 2 * nh), 2 * nh)
        arow = arow_scr[hrow, :]
        erow = erow_scr[hrow, :]
        dt_t = dtrow_scr[hrow, :]
        scale = xpad[pl.ds(base, CHUNK), :]
        ys = []
        for g in range(SSD_GROUPS):
            bm = u[pl.ds(base, CHUNK), SSD_WIDTH + g * SSD_STATE:SSD_WIDTH + (g + 1) * SSD_STATE]
            cm = u[pl.ds(base, CHUNK), SSD_WIDTH + (SSD_GROUPS + g) * SSD_STATE:SSD_WIDTH + (SSD_GROUPS + g + 1) * SSD_STATE]
            cbm = _dot_nt(cm, bm)
            cs_f = _dot(cm, sf_scr[c, g].astype(BF16))
            cs_b = _dot(cm, s_b[g].astype(BF16))
            for pp in range(SSD_PAIRS // SSD_GROUPS):
                p = g * (SSD_PAIRS // SSD_GROUPS) + pp
                xs_b = u[pl.ds(base, CHUNK), p * LANES:(p + 1) * LANES]
                y_h = []
                for hh in range(2):
                    r = 2 * p + hh
                    arg = jnp.where(causal, colb(acol, r) - arow[r:r + 1, :],
                                    erow[nh + r:nh + r + 1, :] - colb(ecol, nh + r))
                    coef = jnp.where(causal, dt_t[r:r + 1, :], dt_t[nh + r:nh + r + 1, :])
                    gm = (cbm * (jnp.exp(arg) * coef)).astype(BF16)
                    y_h.append(_dot(gm, xs_b))
                sl = slice(pp * LANES, (pp + 1) * LANES)
                wl = slice(p * LANES, (p + 1) * LANES)
                wlb = slice(SSD_WIDTH + p * LANES, SSD_WIDTH + (p + 1) * LANES)
                ys.append(pair_sel(y_h[0], y_h[1]) + cs_f[:, sl] * scale[:, wl] + cs_b[:, sl] * scale[:, wlb]
                          + dsk_ref[:, wl] * xs_b.astype(F32))
        y = jnp.concatenate(ys, axis=1)
        y = y * _silu(z_ref[0, pl.ds(base, CHUNK), :].astype(F32))
        ms = jnp.mean(y * y, axis=-1, keepdims=True)
        y_ref[0, pl.ds(base, CHUNK), :] = ((y * lax.rsqrt(ms + EPS)) * nw_ref[...]).astype(BF16)
        dec = chunk_dec(c)
        return tuple(dec[:, SSD_WIDTH + g * gw:SSD_WIDTH + (g + 1) * gw] * s_b[g] + kb_scr[c, g]
                     for g in range(SSD_GROUPS))

    lax.fori_loop(0, nch, bwd, tuple(s_b0), unroll=SSD_OUT_UNROLL)


def _ssd(xbc, z, dt, xbcc, dtc, conv_w8, conv_b, dt_bias, a_log, d_skip, norm_w):
    b, L, nconv = xbc.shape
    Lc = xbcc.shape[1]
    nch = L // CHUNK
    per_b = lambda n, w: pl.BlockSpec((1, n, w), lambda i: (i, 0, 0))
    return pl.pallas_call(
        _ssd_kernel,
        grid=(b,),
        in_specs=[
            per_b(L, nconv), per_b(L, SSD_WIDTH), per_b(L, LANES), per_b(Lc, nconv), per_b(Lc, LANES),
            _const_spec(conv_w8.shape), _const_spec(conv_b.shape), _const_spec(dt_bias.shape),
            _const_spec(a_log.shape), _const_spec(d_skip.shape), _const_spec(norm_w.shape),
        ],
        out_specs=per_b(L, SSD_WIDTH),
        out_shape=jax.ShapeDtypeStruct((b, L, SSD_WIDTH), BF16),
        scratch_shapes=[
            pltpu.VMEM((L + 2 * CONV_HALO, nconv), F32),
            pltpu.VMEM((Lc + 2 * CONV_HALO, nconv), F32),
            pltpu.VMEM((L, nconv), BF16),
            pltpu.VMEM((Lc, nconv), BF16),
            pltpu.VMEM((L, LANES), F32), pltpu.VMEM((L, LANES), F32),
            pltpu.VMEM((Lc, LANES), F32), pltpu.VMEM((Lc, LANES), F32),
            pltpu.VMEM((nch, SSD_GROUPS, SSD_STATE, 2 * LANES), F32),
            pltpu.VMEM((nch, SSD_GROUPS, SSD_STATE, 2 * LANES), F32),
            pltpu.VMEM((L, LANES), F32), pltpu.VMEM((L, LANES), F32),
            pltpu.VMEM((nch * SUBLANES, 2 * SSD_WIDTH), F32),
            pltpu.VMEM((nch * 2 * SSD_HEADS, CHUNK), F32), pltpu.VMEM((nch * 2 * SSD_HEADS, CHUNK), F32),
            pltpu.VMEM((nch * 2 * SSD_HEADS, CHUNK), F32),
        ],
        compiler_params=pltpu.CompilerParams(vmem_limit_bytes=VMEM_LIMIT_SSD),
        name="ssd",
    )(xbc, z, dt, xbcc, dtc, conv_w8, conv_b, dt_bias, a_log, d_skip, norm_w)


def _ret_kernel(q_ref, k_ref, v_ref, g_ref, kc_ref, vc_ref, df_ref, db_ref, gn_ref, y_ref, sf_scr):
    L = q_ref.shape[1]
    Lc = kc_ref.shape[1]
    nch = L // CHUNK
    dk = RET_DK
    row_i = lax.broadcasted_iota(jnp.int32, (CHUNK, dk), 0).astype(F32)
    col_i = lax.broadcasted_iota(jnp.int32, (CHUNK, dk), 1).astype(F32)
    rel = row_i - col_i
    crow = lax.broadcasted_iota(jnp.int32, (Lc, dk), 0).astype(F32)

    heads = []
    s_f0 = []
    s_b0 = []
    for h in range(RET_HEADS):
        cols = slice(h * dk, (h + 1) * dk)
        lg_f = -jnp.exp(df_ref[:, cols])
        lg_b = -jnp.exp(db_ref[:, cols])
        heads.append(dict(
            cols=cols,
            dmat=jnp.where(rel >= 0, jnp.exp(jnp.maximum(rel, 0.0) * lg_f), jnp.exp(jnp.maximum(-rel, 0.0) * lg_b)),
            dq_f=jnp.exp((row_i + 1.0) * lg_f),
            dq_b=jnp.exp((CHUNK - row_i) * lg_b),
            dk_f=jnp.exp((CHUNK - 1.0 - row_i) * lg_f),
            dk_b=jnp.exp(row_i * lg_b),
            dc_f=jnp.exp(CHUNK * lg_f),
            dc_b=jnp.exp(CHUNK * lg_b),
        ))
        kc = kc_ref[0, :, cols].astype(F32)
        vc = vc_ref[0, :, cols]
        s_f0.append(_dot_tn((kc * jnp.exp((Lc - 1.0 - crow) * lg_f)).astype(BF16), vc))
        s_b0.append(_dot_tn((kc * jnp.exp(crow * lg_b)).astype(BF16), vc))

    def fwd(c, s_f):
        base = pl.multiple_of(c * CHUNK, CHUNK)
        new = []
        for h, hd in enumerate(heads):
            sf_scr[c, h] = s_f[h]
            kk = k_ref[0, pl.ds(base, CHUNK), hd["cols"]].astype(F32)
            vv = v_ref[0, pl.ds(base, CHUNK), hd["cols"]]
            new.append(hd["dc_f"] * s_f[h] + _dot_tn((kk * hd["dk_f"]).astype(BF16), vv))
        return tuple(new)

    lax.fori_loop(0, nch, fwd, tuple(s_f0), unroll=RET_UNROLL)

    def bwd(i, s_bs):
        c = nch - 1 - i
        base = pl.multiple_of(c * CHUNK, CHUNK)
        new = []
        for h, hd in enumerate(heads):
            qq = q_ref[0, pl.ds(base, CHUNK), hd["cols"]]
            kk = k_ref[0, pl.ds(base, CHUNK), hd["cols"]]
            vv = v_ref[0, pl.ds(base, CHUNK), hd["cols"]]
            s_b = s_bs[h]
            scores = (_dot_nt(qq, kk) * hd["dmat"]).astype(BF16)
            y = (_dot(scores, vv)
                 + _dot(qq, sf_scr[c, h].astype(BF16)) * hd["dq_f"]
                 + _dot(qq, s_b.astype(BF16)) * hd["dq_b"])
            mu = jnp.mean(y, axis=-1, keepdims=True)
            yc = y - mu
            var = jnp.mean(yc * yc, axis=-1, keepdims=True)
            yn = (yc * lax.rsqrt(var + EPS)) * gn_ref[:, hd["cols"]]
            gate = _silu(g_ref[0, pl.ds(base, CHUNK), hd["cols"]].astype(F32))
            y_ref[0, pl.ds(base, CHUNK), hd["cols"]] = (yn * gate).astype(BF16)
            new.append(hd["dc_b"] * s_b + _dot_tn((kk.astype(F32) * hd["dk_b"]).astype(BF16), vv))
        return tuple(new)

    lax.fori_loop(0, nch, bwd, tuple(s_b0), unroll=RET_UNROLL)


def _retention(q, k, v, g, kc, vc, decay_f, decay_b, gn_w):
    b, L, w = q.shape
    Lc = kc.shape[1]
    nch = L // CHUNK
    per_b = lambda n: pl.BlockSpec((1, n, w), lambda i: (i, 0, 0))
    return pl.pallas_call(
        _ret_kernel,
        grid=(b,),
        in_specs=[per_b(L), per_b(L), per_b(L), per_b(L), per_b(Lc), per_b(Lc),
                  _const_spec((1, w)), _const_spec((1, w)), _const_spec((1, w))],
        out_specs=per_b(L),
        out_shape=jax.ShapeDtypeStruct((b, L, w), BF16),
        scratch_shapes=[
            pltpu.VMEM((nch, RET_HEADS, RET_DK, RET_DK), F32),
        ],
        compiler_params=pltpu.CompilerParams(vmem_limit_bytes=VMEM_LIMIT),
        name="retention",
    )(q, k, v, g, kc, vc, decay_f, decay_b, gn_w)


def _outproj_router_kernel(yr_ref, ys_ref, x_ref, g1_ref, sh2_ref, sc2_ref, npost_ref, npre_ref,
                           wor_ref, wos_ref, wr_ref, br_ref, tri_ref,
                           x1_ref, h2_ref, route_ref, slots_ref, seg_ref,
                           wcat):
    i = pl.program_id(0)

    @pl.when(i == 0)
    def _():
        wr = wr_ref[...]
        hi = wr.astype(BF16)
        wcat[:, :LANES] = hi
        wcat[:, LANES:] = (wr - hi.astype(F32)).astype(BF16)

    _route_tile(yr_ref, ys_ref, x_ref, g1_ref, sh2_ref, sc2_ref, npost_ref, npre_ref, wor_ref, wos_ref, br_ref,
                tri_ref, x1_ref, h2_ref, route_ref, slots_ref, seg_ref, wcat)


def _route_tile(yr_ref, ys_ref, x_ref, g1_ref, sh2_ref, sc2_ref, npost_ref, npre_ref, wor_ref, wos_ref, br_ref,
                tri_ref, x1_ref, h2_ref, route_ref, slots_ref, seg_ref, wcat):
    tm = x_ref.shape[0]
    rows = slice(0, tm)
    y = _dot(yr_ref[rows, :], wor_ref[...]) + _dot(ys_ref[rows, :], wos_ref[...])
    ms = jnp.mean(y * y, axis=-1, keepdims=True)
    x1 = x_ref[rows, :] + (y * lax.rsqrt(ms + EPS)) * (g1_ref[0] * npost_ref[...])
    x1_ref[rows, :] = x1
    h2 = _norm_mod(x1, npre_ref[...], sc2_ref[0], sh2_ref[0])
    h2_ref[rows, :] = h2.astype(BF16)

    h_hi = h2.astype(BF16)
    h_lo = (h2 - h_hi.astype(F32)).astype(BF16)
    both = _dot(h_hi, wcat[...])
    lg = both[:, :LANES] + both[:, LANES:] + _dot(h_lo, wcat[:, :LANES]) + br_ref[...]

    lane = lax.broadcasted_iota(jnp.int32, (tm, LANES), 1)
    lane_f = lane.astype(F32)
    is_grp = (lane >= N_EXPERTS) & (lane < N_EXPERTS + MOE_GROUPS)
    gl = jnp.where(is_grp, lg, NEG_BIG)
    mg = jnp.max(gl, axis=-1, keepdims=True)
    grp_lane = jnp.min(jnp.where(gl == mg, lane_f, 1e9), axis=-1, keepdims=True)
    p_g = 1.0 / jnp.sum(jnp.where(is_grp, jnp.exp(gl - mg), 0.0), axis=-1, keepdims=True)
    first = (grp_lane - N_EXPERTS) * EXPERTS_PER_GROUP
    in_grp = (lane_f >= first) & (lane_f < first + EXPERTS_PER_GROUP)
    el = jnp.where(in_grp, lg, NEG_BIG)
    t1 = jnp.max(el, axis=-1, keepdims=True)
    i1 = jnp.min(jnp.where(el == t1, lane_f, 1e9), axis=-1, keepdims=True)
    el2 = jnp.where(lane_f == i1, NEG_BIG, el)
    t2 = jnp.max(el2, axis=-1, keepdims=True)
    i2 = jnp.min(jnp.where(el2 == t2, lane_f, 1e9), axis=-1, keepdims=True)
    s = jnp.exp(t2 - t1)
    w1 = p_g / (1.0 + s)
    w2 = p_g * s / (1.0 + s)

    oh1 = (lane_f == i1)
    oh2 = (lane_f == i2)
    ohf = jnp.where(oh1 | oh2, 1.0, 0.0)
    before = _dot(tri_ref[...], ohf.astype(BF16))
    cnt = jnp.sum(ohf, axis=0, keepdims=True)
    seg = jnp.floor((cnt + (SEG_ALIGN - 1.0)) * (1.0 / SEG_ALIGN)) * SEG_ALIGN
    e_row = lax.broadcasted_iota(jnp.int32, (LANES, LANES), 0)
    e_col = lax.broadcasted_iota(jnp.int32, (LANES, LANES), 1)
    earlier = (e_row < e_col).astype(BF16)
    seg_off = _dot(jnp.broadcast_to(seg, (SUBLANES, LANES)).astype(BF16), earlier)[0:1]
    where_to = before + seg_off
    lpos1 = jnp.sum(jnp.where(oh1, where_to, 0.0), axis=-1, keepdims=True)
    lpos2 = jnp.sum(jnp.where(oh2, where_to, 0.0), axis=-1, keepdims=True)

    cols = [w1, w2, lpos1, lpos2]
    for wk in (w1, w2):
        hi = wk.astype(BF16).astype(F32)
        cols += [hi, wk - hi]
    cols.append(jnp.ones_like(w1))
    packed = jnp.zeros((tm, LANES), F32)
    for k, col in enumerate(cols):
        packed = jnp.where(lane == k, col, packed)
    route_ref[rows, :] = packed

    row = lax.broadcasted_iota(jnp.int32, (tm, LANES), 0)
    on_diag = (row % LANES) == lane
    per = tm // LANES
    for qi, col in enumerate((lpos1, lpos2)):
        picked = jnp.where(on_diag, col, 0.0)
        dense = jnp.sum(picked.reshape(per, LANES, LANES), axis=1).astype(jnp.int32)
        slots_ref[0, qi * per:(qi + 1) * per, :] = dense
    tbl_row = lax.broadcasted_iota(jnp.int32, (SUBLANES, LANES), 0)
    seg_ref[0] = jnp.where(tbl_row == 0, seg, jnp.where(tbl_row == 1, seg_off, 0.0)).astype(jnp.int32)


def _outproj_router(yr, ys, x2, mod3, npost, npre, wo_r, wo_s, w_router, b_router, seq_len):
    T, d = x2.shape
    tm = TM_OUT
    per_seq = seq_len // tm
    rw = yr.shape[1]
    tri = (jnp.arange(tm)[:, None] > jnp.arange(tm)[None, :]).astype(BF16)
    tok = lambda w: pl.BlockSpec((tm, w), lambda i: (i, 0))
    modv = lambda k: pl.BlockSpec((1, 1, d), lambda i: (i // per_seq, 0, k))
    tile3 = lambda r: pl.BlockSpec((1, r, LANES), lambda i: (i, 0, 0))
    slot_rows = TOP_K * (tm // LANES)
    return pl.pallas_call(
        _outproj_router_kernel,
        grid=(T // tm,),
        in_specs=[
            tok(rw), tok(rw), tok(d), modv(2), modv(3), modv(4),
            _const_spec((1, d)), _const_spec((1, d)),
            _const_spec(wo_r.shape), _const_spec(wo_s.shape), _const_spec(w_router.shape), _const_spec((1, LANES)),
            _const_spec((tm, tm)),
        ],
        out_specs=[tok(d), tok(d), tok(LANES), tile3(slot_rows), tile3(SUBLANES)],
        out_shape=[jax.ShapeDtypeStruct((T, d), F32), jax.ShapeDtypeStruct((T, d), BF16),
                   jax.ShapeDtypeStruct((T, LANES), F32),
                   jax.ShapeDtypeStruct((T // tm, slot_rows, LANES), jnp.int32),
                   jax.ShapeDtypeStruct((T // tm, SUBLANES, LANES), jnp.int32)],
        scratch_shapes=[pltpu.VMEM((d, 2 * LANES), BF16)],
        compiler_params=pltpu.CompilerParams(dimension_semantics=("arbitrary",),
                                             vmem_limit_bytes=VMEM_LIMIT),
        name="outproj_router",
    )(yr, ys, x2, mod3, mod3, mod3, npost, npre, wo_r, wo_s, w_router, b_router, tri)


def _expert_kernel(be_ref, first_ref, slot_ref, next_ref, nused_ref, xs_ref, wg_hbm, wu_hbm, wd_hbm, y_ref,
                   wg_f, wu_f, wd_f, wg_b, wu_b, wd_b, sem):
    def fetch(e, s):
        return [pltpu.make_async_copy(src.at[e], dst.at[s], sem.at[s, k])
                for k, (src, dst) in enumerate(((wg_hbm, wg_f), (wu_hbm, wu_f), (wd_hbm, wd_f)))]

    @pl.when(pl.program_id(0) == 0)
    def _():
        for c in fetch(be_ref[0], slot_ref[0]):
            c.start()

    for sb in range(EXPERT_BLOCKS_PER_STEP):
        _expert_block(pl.program_id(0) * EXPERT_BLOCKS_PER_STEP + sb, slice(sb * MB_EXPERT, (sb + 1) * MB_EXPERT),
                      fetch, be_ref, first_ref, slot_ref, next_ref, nused_ref, xs_ref, y_ref,
                      wg_f, wu_f, wd_f, wg_b, wu_b, wd_b)


def _expert_block(i, rows, fetch, be_ref, first_ref, slot_ref, next_ref, nused_ref, xs_ref, y_ref,
                  wg_f, wu_f, wd_f, wg_b, wu_b, wd_b):
    @pl.when(first_ref[i] == 1)
    def _():
        s = slot_ref[i]

        @pl.when(next_ref[i] >= 0)
        def _():
            for c in fetch(next_ref[i], 1 - s):
                c.start()

        for c in fetch(be_ref[i], s):
            c.wait()
        wg_b[...] = wg_f[s].astype(BF16)
        wu_b[...] = wu_f[s].astype(BF16)
        wd_b[...] = wd_f[s].astype(BF16)

    @pl.when(i < nused_ref[0])
    def _():
        d = y_ref.shape[1]
        side = xs_ref[rows, d:d + LANES].astype(F32)
        second = side[:, 8:9] == 2.0
        unscale = jnp.where(second, 0.5, 1.0)
        weight = jnp.where(second, side[:, 6:7] + side[:, 7:8], side[:, 4:5] + side[:, 5:6]) * unscale
        xb = xs_ref[rows, 0:d] * unscale.astype(BF16)
        hid = (_silu(_dot(xb, wg_b[...])) * _dot(xb, wu_b[...])).astype(BF16)
        y_ref[rows, :] = (_dot(hid, wd_b[...]) * weight).astype(BF16)

    @pl.when(i >= nused_ref[0])
    def _():
        y_ref[rows, :] = jnp.zeros((rows.stop - rows.start, y_ref.shape[1]), y_ref.dtype)


def _expert_plan(padded, pad_end, n_blocks, mb):
    n_used = (pad_end[-1:] // mb).astype(jnp.int32)
    blk_start = jnp.arange(n_blocks, dtype=jnp.int32) * mb
    experts = jnp.arange(N_EXPERTS, dtype=jnp.int32)
    blk_expert = jnp.minimum(jnp.sum((pad_end[None, :] <= blk_start[:, None]).astype(jnp.int32), axis=1),
                             N_EXPERTS - 1)
    prev = jnp.concatenate([jnp.full((1,), -1, jnp.int32), blk_expert[:-1]])
    first = ((blk_start < pad_end[-1]) & (blk_expert != prev)).astype(jnp.int32)
    slot = (jnp.cumsum(first) - 1) % 2
    later = jnp.where((padded > 0)[None, :] & (experts[None, :] > experts[:, None]), experts[None, :], N_EXPERTS)
    next_nonempty = jnp.min(later, axis=1)
    next_nonempty = jnp.where(next_nonempty == N_EXPERTS, -1, next_nonempty)
    nxt = jnp.sum(jnp.where(blk_expert[:, None] == experts[None, :], next_nonempty[None, :], 0), axis=1)
    return [a.astype(jnp.int32) for a in (blk_expert, first, slot, nxt, n_used)]


def _experts(plan, xs, w_gate, w_up, w_down):
    cap, dp = xs.shape
    n_exp, d, de = w_gate.shape
    step_rows = MB_EXPERT * EXPERT_BLOCKS_PER_STEP
    assert dp == d + LANES and cap % step_rows == 0
    grid_spec = pltpu.PrefetchScalarGridSpec(
        num_scalar_prefetch=len(plan),
        grid=(cap // step_rows,),
        in_specs=[
            pl.BlockSpec((step_rows, dp), lambda i, be, fi, sl, nx, nu: (
                jnp.minimum(i, (nu[0] - 1) // EXPERT_BLOCKS_PER_STEP), 0)),
            pl.BlockSpec(memory_space=pl.ANY), pl.BlockSpec(memory_space=pl.ANY), pl.BlockSpec(memory_space=pl.ANY),
        ],
        out_specs=pl.BlockSpec((step_rows, d), lambda i, be, fi, sl, nx, nu: (i, 0)),
        scratch_shapes=[pltpu.VMEM((2, d, de), F32), pltpu.VMEM((2, d, de), F32), pltpu.VMEM((2, de, d), F32),
                        pltpu.VMEM((d, de), BF16), pltpu.VMEM((d, de), BF16), pltpu.VMEM((de, d), BF16),
                        pltpu.SemaphoreType.DMA((2, 3))],
    )
    return pl.pallas_call(
        _expert_kernel,
        grid_spec=grid_spec,
        out_shape=jax.ShapeDtypeStruct((cap, d), BF16),
        compiler_params=pltpu.CompilerParams(dimension_semantics=("arbitrary",),
                                             vmem_limit_bytes=VMEM_LIMIT),
        name="experts",
    )(*plan, xs, w_gate, w_up, w_down)


def _segment_pieces(tile, segrow_ref, seglen_ref, segoff_ref, act):
    def per_expert(e, carry):
        idx = tile * N_EXPERTS + e
        g0 = segrow_ref[idx]
        l0 = segoff_ref[idx]

        def piece(j, c2):
            act(pl.multiple_of(l0 + j * SEG_ALIGN, SEG_ALIGN), pl.multiple_of(g0 + j * SEG_ALIGN, SEG_ALIGN))
            return c2

        lax.fori_loop(0, seglen_ref[idx], piece, 0)
        return carry

    lax.fori_loop(0, N_EXPERTS, per_expert, 0)


def _wait_rows(total, row_copy):
    size = SEG_ALIGN
    while size <= _stage_rows(max(TD_DISPATCH, TF_COMBINE)):
        @pl.when((total & size) != 0)
        def _(size=size):
            row_copy(size).wait()
        size *= 2


def _for_tile_rows(tile_rows, max_rows, body):
    @pl.when(tile_rows <= COMMON_STAGE_ROWS)
    def _():
        body(COMMON_STAGE_ROWS)

    @pl.when(tile_rows > COMMON_STAGE_ROWS)
    def _():
        body(max_rows)


def _stage_rows(tile_tokens):
    return TOP_K * tile_tokens + N_EXPERTS * SEG_ALIGN


def _dispatch_seg_kernel(pad_end_ref, zero_from_ref, segrow_ref, seglen_ref, segoff_ref, tilerows_ref,
                         lpos_ref, route_ref, h_ref, xs_hbm, zbuf, stage, sem, zsem):
    i = pl.program_id(0)
    nt = pl.num_programs(0)
    td, d = h_ref.shape
    sr = stage.shape[1]
    per = td // LANES

    def zero_fills(act):
        def fill(row, n):
            act(pltpu.make_async_copy(zbuf.at[pl.ds(0, n)], xs_hbm.at[pl.ds(pl.multiple_of(row, SEG_ALIGN), n)], zsem))

        def region(e, carry):
            row = zero_from_ref[e]
            left = pad_end_ref[e] - row
            size = ZERO_PIECE
            while size >= SEG_ALIGN:
                take = (left & size) != 0

                @pl.when(take)
                def _(row=row, size=size):
                    fill(row, size)

                row = row + jnp.where(take, size, 0)
                size //= 2
            return carry

        def tail(p, carry):
            @pl.when(p * ZERO_PIECE >= pad_end_ref[N_EXPERTS - 1])
            def _():
                fill(p * ZERO_PIECE, ZERO_PIECE)
            return carry

        lax.fori_loop(0, N_EXPERTS, region, 0)
        lax.fori_loop(0, xs_hbm.shape[0] // ZERO_PIECE, tail, 0)

    @pl.when(i == 0)
    def _():
        zbuf[...] = jnp.zeros_like(zbuf)
        zero_fills(lambda c: c.start())

    def shipped(tile):
        _wait_rows(tilerows_ref[tile], lambda n: pltpu.make_async_copy(
            stage.at[tile % 2, pl.ds(0, n)], xs_hbm.at[pl.ds(0, n)], sem.at[tile % 2]))

    @pl.when(i >= 2)
    def _():
        shipped(i - 2)

    slot = i % 2

    def sort_rows(n_rows):
        srow = lax.broadcasted_iota(jnp.int32, (n_rows, LANES), 0)
        place = jnp.concatenate(
            [jnp.where(srow == lpos_ref[0, cb:cb + 1, :], 1.0,
                       jnp.where(srow == lpos_ref[0, per + cb:per + cb + 1, :], 2.0, 0.0)) for cb in range(per)],
            axis=1).astype(BF16)
        stage[slot, 0:n_rows, :] = _dot(place, jnp.concatenate([h_ref[...], route_ref[...].astype(BF16)],
                                                               axis=1)).astype(BF16)

    _for_tile_rows(tilerows_ref[i], sr, sort_rows)
    _segment_pieces(i, segrow_ref, seglen_ref, segoff_ref, lambda lr, gr: pltpu.make_async_copy(
        stage.at[slot, pl.ds(lr, SEG_ALIGN)], xs_hbm.at[pl.ds(gr, SEG_ALIGN)], sem.at[slot]).start())

    @pl.when(i == nt - 1)
    def _():
        @pl.when(i >= 1)
        def _():
            shipped(i - 1)
        shipped(i)
        zero_fills(lambda c: c.wait())


def _dispatch_seg(plan, lpos, route, h2, cap):
    T, d = h2.shape
    td = TD_DISPATCH
    nt = T // td
    width = d + LANES
    sr = _stage_rows(td)
    n_pre = len(plan)
    grid_spec = pltpu.PrefetchScalarGridSpec(
        num_scalar_prefetch=n_pre,
        grid=(nt,),
        in_specs=[
            pl.BlockSpec((1, lpos.shape[1], LANES), lambda i, *_: (i, 0, 0)),
            pl.BlockSpec((td, LANES), lambda i, *_: (i, 0)),
            pl.BlockSpec((td, d), lambda i, *_: (i, 0)),
        ],
        out_specs=pl.BlockSpec(memory_space=pl.ANY),
        scratch_shapes=[pltpu.VMEM((ZERO_PIECE, width), BF16), pltpu.VMEM((2, sr, width), BF16),
                        pltpu.SemaphoreType.DMA((2,)), pltpu.SemaphoreType.DMA(())],
    )
    return pl.pallas_call(
        _dispatch_seg_kernel,
        grid_spec=grid_spec,
        out_shape=jax.ShapeDtypeStruct((cap, width), BF16),
        compiler_params=pltpu.CompilerParams(dimension_semantics=("arbitrary",), vmem_limit_bytes=VMEM_LIMIT),
        name="dispatch",
    )(*plan, lpos, route, h2)


def _combine_seg_kernel(segrow_ref, seglen_ref, segoff_ref, tilerows_ref, route_ref, x1_ref, g2_ref, nw_ref,
                        yb_hbm, o_ref, stage, sem):
    i = pl.program_id(0)
    nt = pl.num_programs(0)
    tf = x1_ref.shape[0]
    sr = stage.shape[1]

    def fetch(tile, slot):
        _segment_pieces(tile, segrow_ref, seglen_ref, segoff_ref, lambda lr, gr: pltpu.make_async_copy(
            yb_hbm.at[pl.ds(gr, SEG_ALIGN)], stage.at[slot, pl.ds(lr, SEG_ALIGN)], sem.at[slot]).start())

    @pl.when(i == 0)
    def _():
        stage[...] = jnp.zeros_like(stage)
        fetch(i, 0)

    for slot in range(2):
        @pl.when((i + 1 < nt) & (i % 2 != slot))
        def _(slot=slot):
            fetch(i + 1, slot)

    _wait_rows(tilerows_ref[i], lambda n: pltpu.make_async_copy(
        yb_hbm.at[pl.ds(0, n)], stage.at[i % 2, pl.ds(0, n)], sem.at[i % 2]))

    local_row = route_ref[:, 2:2 + TOP_K].astype(jnp.int32)

    def unsort_rows(n_rows):
        scol = lax.broadcasted_iota(jnp.int32, (tf, n_rows), 1)
        pick = jnp.where((scol == local_row[:, 0:1]) | (scol == local_row[:, 1:2]), 1.0, 0.0).astype(BF16)
        out = _dot(pick, stage[i % 2, 0:n_rows, :])
        ms = jnp.mean(out * out, axis=-1, keepdims=True)
        o_ref[...] = x1_ref[...] + g2_ref[0] * ((out * lax.rsqrt(ms + EPS)) * nw_ref[...])

    _for_tile_rows(tilerows_ref[i], sr, unsort_rows)


def _combine_seg(seg_plan, route, x1, mod3, nw, yb, seq_len):
    T, d = x1.shape
    tf = TF_COMBINE
    nt = T // tf
    per_seq = seq_len // tf
    grid_spec = pltpu.PrefetchScalarGridSpec(
        num_scalar_prefetch=len(seg_plan),
        grid=(nt,),
        in_specs=[
            pl.BlockSpec((tf, LANES), lambda i, *_: (i, 0)),
            pl.BlockSpec((tf, d), lambda i, *_: (i, 0)),
            pl.BlockSpec((1, 1, d), lambda i, *_: (i // per_seq, 0, 5)),
            pl.BlockSpec((1, d), lambda i, *_: (0, 0)),
            pl.BlockSpec(memory_space=pl.ANY),
        ],
        out_specs=pl.BlockSpec((tf, d), lambda i, *_: (i, 0)),
        scratch_shapes=[pltpu.VMEM((2, _stage_rows(tf), d), yb.dtype), pltpu.SemaphoreType.DMA((2,))],
    )
    return pl.pallas_call(
        _combine_seg_kernel,
        grid_spec=grid_spec,
        out_shape=jax.ShapeDtypeStruct((T, d), F32),
        compiler_params=pltpu.CompilerParams(dimension_semantics=("arbitrary",), vmem_limit_bytes=VMEM_LIMIT),
        name="combine",
    )(*seg_plan, route, x1, mod3, nw, yb)


def _rope_tables(L, n_heads):
    quarter = RET_DK // 4
    freqs = ROPE_BASE ** (-jnp.arange(quarter, dtype=F32) / quarter)
    t = jnp.arange(L)
    ang_r = (t // GRID_W).astype(F32)[:, None] * freqs
    ang_c = (t % GRID_W).astype(F32)[:, None] * freqs
    cos = jnp.concatenate([jnp.cos(ang_r)] * 2 + [jnp.cos(ang_c)] * 2, axis=-1)
    sin = jnp.concatenate([-jnp.sin(ang_r), jnp.sin(ang_r), -jnp.sin(ang_c), jnp.sin(ang_c)], axis=-1)
    return jnp.tile(cos, (1, n_heads)), jnp.tile(sin, (1, n_heads))


def _lane_pad(v, width=LANES):
    return jnp.pad(v, [(0, 0)] * (v.ndim - 1) + [(0, width - v.shape[-1])])


def kernel(x, c, ctx, c_ctx, w_mod, b_mod, norm_pre_mix, norm_post_mix, norm_pre_ffn, norm_post_ffn, w_in, w_out, ret_decay_f, ret_decay_b, ret_gn_w, ssd_conv_w, ssd_conv_b, ssd_dt_bias_f, ssd_dt_bias_b, ssd_a_log_f, ssd_a_log_b, ssd_d, ssd_norm_w, moe_w_rg, moe_b_rg, moe_w_re, moe_b_re, moe_w_gate, moe_w_up, moe_w_down):
    b, L, d = x.shape
    assert w_mod.shape[0] == 1, "single layer: context outputs are never needed"
    assert TM_OUT == TD_DISPATCH == TF_COMBINE, "router, dispatch and combine share one slot-row layout"
    rw = RET_HEADS * RET_DK
    nconv = SSD_WIDTH + 2 * SSD_GROUPS * SSD_STATE
    T = b * L

    mod_rows = -(-(b + 1) // SUBLANES) * SUBLANES
    c_all = jnp.zeros((mod_rows, d), F32).at[:b].set(c).at[b].set(c_ctx)
    mod3 = _modulation(c_all, w_mod[0], b_mod[0]).reshape(mod_rows, 1, 6 * d)

    wi = w_in[0]
    o = 0
    wq = wi[:, o:o + rw]; o += rw
    wk = wi[:, o:o + rw]; o += rw
    wv = wi[:, o:o + rw]; o += rw
    wg = wi[:, o:o + rw]; o += rw
    wz = wi[:, o:o + SSD_WIDTH]; o += SSD_WIDTH
    wxbc = wi[:, o:o + nconv].astype(BF16); o += nconv
    wdt = _lane_pad(wi[:, o:o + 2 * SSD_HEADS]).astype(BF16)
    wqk = jnp.concatenate([wq, wk], axis=1).astype(BF16)
    wvgz = jnp.concatenate([wv, wg, wz], axis=1).astype(BF16)
    cos_t, sin_t = _rope_tables(L, RET_HEADS)
    nw1 = norm_pre_mix[0].reshape(1, d)

    q, k, v, g, z, xbc, dt = _inproj(x, mod3, nw1, wqk, wvgz, wxbc, wdt, cos_t, sin_t)
    kc, vc, xbcc, dtc = _inproj_ctx(ctx, mod3, b, nw1, wk.astype(BF16), wv.astype(BF16), wxbc, wdt)

    conv_w8 = jnp.pad(ssd_conv_w[0], ((0, SUBLANES - SSD_CONV), (0, 0)))
    dt_bias = _lane_pad(jnp.concatenate([ssd_dt_bias_f[0], ssd_dt_bias_b[0]])[None, :])
    a_log = _lane_pad(jnp.concatenate([ssd_a_log_f[0], ssd_a_log_b[0]])[None, :])
    d_skip = jnp.repeat(ssd_d[0], SSD_HEADDIM)[None, :]
    ys = _ssd(xbc, z, dt, xbcc, dtc, conv_w8, ssd_conv_b[0][None, :], dt_bias, a_log, d_skip,
              ssd_norm_w[0][None, :])

    yr = _retention(q, k, v, g, kc, vc,
                    jnp.repeat(ret_decay_f[0], RET_DK)[None, :], jnp.repeat(ret_decay_b[0], RET_DK)[None, :],
                    ret_gn_w[0][None, :])

    wo = w_out[0].astype(BF16)
    w_router = _lane_pad(jnp.concatenate(
        [jnp.transpose(moe_w_re[0], (1, 0, 2)).reshape(d, N_EXPERTS), moe_w_rg[0]], axis=1))
    b_router = _lane_pad(jnp.concatenate([moe_b_re[0].reshape(-1), moe_b_rg[0]])[None, :])
    x1, h2, route, lpos, seg = _outproj_router(
        yr.reshape(T, rw), ys.reshape(T, SSD_WIDTH), x.reshape(T, d), mod3,
        norm_post_mix[0][None, :], norm_pre_ffn[0][None, :], wo[:rw], wo[rw:], w_router, b_router, L)

    mb = MB_EXPERT
    nt = T // TM_OUT
    n_blocks = -(-(T * TOP_K + nt * N_EXPERTS * (SEG_ALIGN - 1) + N_EXPERTS * (mb - 1)) // mb)
    n_blocks = -(-n_blocks // EXPERT_BLOCKS_PER_STEP) * EXPERT_BLOCKS_PER_STEP
    seg_len = seg[:, 0, :N_EXPERTS]
    seg_off = seg[:, 1, :N_EXPERTS]
    used = jnp.sum(seg_len, axis=0)
    padded = (used + mb - 1) // mb * mb
    pad_end = jnp.cumsum(padded)
    pad_start = pad_end - padded
    seg_row = pad_start[None, :] + jnp.cumsum(seg_len, axis=0) - seg_len
    seg_plan = [a.reshape(-1).astype(jnp.int32)
                for a in (seg_row, seg_len // SEG_ALIGN, seg_off, jnp.sum(seg_len, axis=1))]
    zero_from = (pad_start + used).astype(jnp.int32)

    xs = _dispatch_seg([pad_end.astype(jnp.int32), zero_from] + seg_plan, lpos, route, h2, n_blocks * mb)
    yb = _experts(_expert_plan(padded, pad_end, n_blocks, mb), xs, moe_w_gate[0], moe_w_up[0], moe_w_down[0])
    out = _combine_seg(seg_plan, route, x1, mod3, norm_post_ffn[0][None, :], yb, L)
    return out.reshape(b, L, d)
```

```python
import jax
import jax.numpy as jnp
from jax import lax
from jax.experimental import pallas as pl
from jax.experimental.pallas import tpu as pltpu

F32 = jnp.float32
BF16 = jnp.bfloat16

LANES = 128
SUBLANES = 8
BF16_TILE_ROWS = 16
V7X_VMEM_BYTES = 64 * 1024 * 1024
VMEM_LIMIT = V7X_VMEM_BYTES * 3 // 4
VMEM_LIMIT_SSD = V7X_VMEM_BYTES * 7 // 8

EPS = 1e-6
CHUNK = 128
GRID_W = 64
RET_HEADS = 4
RET_DK = 128
ROPE_BASE = 10000.0
SSD_HEADS = 8
SSD_HEADDIM = 64
SSD_GROUPS = 2
SSD_STATE = 128
SSD_WIDTH = SSD_HEADS * SSD_HEADDIM
SSD_CONV = 5
SSD_PAIRS = SSD_WIDTH // LANES
MOE_GROUPS = 4
EXPERTS_PER_GROUP = 8
N_EXPERTS = MOE_GROUPS * EXPERTS_PER_GROUP
TOP_K = 2
CONV_HALO = SUBLANES

TM_PROJ = 512
TM_OUT = 512
TD_DISPATCH = TM_OUT
MB_EXPERT = 512
EXPERT_BLOCKS_PER_STEP = 4
ZERO_PIECE = MB_EXPERT // 2
TF_COMBINE = TM_OUT
SEG_ALIGN = BF16_TILE_ROWS
COMMON_STAGE_ROWS = TOP_K * TM_OUT + N_EXPERTS * 10
RET_UNROLL = 8
SSD_PREP_UNROLL = 8
SSD_OUT_UNROLL = 2
NEG_BIG = -1e30


def _silu(v):
    return v * jax.nn.sigmoid(v)


def _dot(a, b):
    return jnp.dot(a, b, preferred_element_type=F32)


def _dot_tn(a, b):
    return lax.dot_general(a, b, (((0,), (0,)), ((), ())), preferred_element_type=F32)


def _dot_nt(a, b):
    return lax.dot_general(a, b, (((1,), (1,)), ((), ())), preferred_element_type=F32)


def _mod_kernel(c_ref, w_ref, b_ref, o_ref):
    a = _silu(c_ref[...])
    w = w_ref[...]
    a_hi = a.astype(BF16)
    a_lo = (a - a_hi.astype(F32)).astype(BF16)
    w_hi = w.astype(BF16)
    w_lo = (w - w_hi.astype(F32)).astype(BF16)
    o_ref[...] = _dot(a_hi, w_hi) + _dot(a_lo, w_hi) + _dot(a_hi, w_lo) + b_ref[...]


def _modulation(c_all, w_mod, b_mod):
    rows, d = c_all.shape
    n = w_mod.shape[1]
    return pl.pallas_call(
        _mod_kernel,
        grid=(n // d,),
        in_specs=[
            pl.BlockSpec((rows, d), lambda j: (0, 0)),
            pl.BlockSpec((d, d), lambda j: (0, j)),
            pl.BlockSpec((1, d), lambda j: (0, j)),
        ],
        out_specs=pl.BlockSpec((rows, d), lambda j: (0, j)),
        out_shape=jax.ShapeDtypeStruct((rows, n), F32),
        name="modulation",
    )(c_all, w_mod, b_mod.reshape(1, n))


def _norm_mod(x, nw, sc, sh):
    ms = jnp.mean(x * x, axis=-1, keepdims=True)
    return (x * lax.rsqrt(ms + EPS)) * (nw * (1.0 + sc)) + sh


def _rope(t, cos, sin_signed, first_half):
    width = t.shape[-1]
    quarter = RET_DK // 4
    swapped = jnp.where(first_half, pltpu.roll(t, width - quarter, 1), pltpu.roll(t, quarter, 1))
    return t * cos + swapped * sin_signed


def _inproj_kernel(x_ref, sh_ref, sc_ref, nw_ref, wqk_ref, wvgz_ref, wxbc_ref, wdt_ref, cos_ref, sin_ref,
                   q_ref, k_ref, v_ref, g_ref, z_ref, xbc_ref, dt_ref):
    hb = _norm_mod(x_ref[0], nw_ref[...], sc_ref[0], sh_ref[0]).astype(BF16)
    rw = q_ref.shape[-1]
    qk = _dot(hb, wqk_ref[...])
    cos = cos_ref[...]
    sin = sin_ref[...]
    lane = lax.broadcasted_iota(jnp.int32, cos.shape, 1)
    first_half = (lane % (RET_DK // 2)) < (RET_DK // 4)
    q_ref[0] = _rope(qk[:, :rw], cos, sin, first_half).astype(BF16)
    k_ref[0] = (_rope(qk[:, rw:], cos, sin, first_half) * (RET_DK ** -0.5)).astype(BF16)
    vgz = _dot(hb, wvgz_ref[...])
    v_ref[0] = vgz[:, :rw].astype(BF16)
    g_ref[0] = vgz[:, rw:2 * rw].astype(BF16)
    z_ref[0] = vgz[:, 2 * rw:].astype(BF16)
    xbc_ref[0] = _dot(hb, wxbc_ref[...]).astype(BF16)
    dt_ref[0] = _dot(hb, wdt_ref[...])


def _inproj_ctx_kernel(x_ref, sh_ref, sc_ref, nw_ref, wk_ref, wv_ref, wxbc_ref, wdt_ref,
                       k_ref, v_ref, xbc_ref, dt_ref):
    hb = _norm_mod(x_ref[0], nw_ref[...], sc_ref[0], sh_ref[0]).astype(BF16)
    k_ref[0] = (_dot(hb, wk_ref[...]) * (RET_DK ** -0.5)).astype(BF16)
    v_ref[0] = _dot(hb, wv_ref[...]).astype(BF16)
    xbc_ref[0] = _dot(hb, wxbc_ref[...]).astype(BF16)
    dt_ref[0] = _dot(hb, wdt_ref[...])


def _const_spec(shape):
    nd = len(shape)
    return pl.BlockSpec(shape, lambda *_: (0,) * nd)


def _inproj(x, mod3, nw, wqk, wvgz, wxbc, wdt, cos_t, sin_t):
    b, L, d = x.shape
    tm = min(TM_PROJ, L)
    rw = wqk.shape[1] // 2
    tok = lambda w: pl.BlockSpec((1, tm, w), lambda i, j: (i, j, 0))
    out_bf = lambda w: jax.ShapeDtypeStruct((b, L, w), BF16)
    return pl.pallas_call(
        _inproj_kernel,
        grid=(b, L // tm),
        in_specs=[
            tok(d),
            pl.BlockSpec((1, 1, d), lambda i, j: (i, 0, 0)),
            pl.BlockSpec((1, 1, d), lambda i, j: (i, 0, 1)),
            _const_spec((1, d)),
            _const_spec(wqk.shape), _const_spec(wvgz.shape), _const_spec(wxbc.shape), _const_spec(wdt.shape),
            pl.BlockSpec((tm, rw), lambda i, j: (j, 0)),
            pl.BlockSpec((tm, rw), lambda i, j: (j, 0)),
        ],
        out_specs=[tok(rw), tok(rw), tok(rw), tok(rw), tok(rw), tok(wxbc.shape[1]), tok(LANES)],
        out_shape=[out_bf(rw), out_bf(rw), out_bf(rw), out_bf(rw), out_bf(rw), out_bf(wxbc.shape[1]),
                   jax.ShapeDtypeStruct((b, L, LANES), F32)],
        compiler_params=pltpu.CompilerParams(vmem_limit_bytes=VMEM_LIMIT),
        name="inproj",
    )(x, mod3, mod3, nw, wqk, wvgz, wxbc, wdt, cos_t, sin_t)


def _inproj_ctx(ctx, mod3, ctx_row, nw, wk, wv, wxbc, wdt):
    b, L, d = ctx.shape
    tm = min(TM_PROJ, L)
    rw = wk.shape[1]
    tok = lambda w: pl.BlockSpec((1, tm, w), lambda i, j: (i, j, 0))
    out_bf = lambda w: jax.ShapeDtypeStruct((b, L, w), BF16)
    return pl.pallas_call(
        _inproj_ctx_kernel,
        grid=(b, L // tm),
        in_specs=[
            tok(d),
            pl.BlockSpec((1, 1, d), lambda i, j: (ctx_row, 0, 0)),
            pl.BlockSpec((1, 1, d), lambda i, j: (ctx_row, 0, 1)),
            _const_spec((1, d)),
            _const_spec(wk.shape), _const_spec(wv.shape), _const_spec(wxbc.shape), _const_spec(wdt.shape),
        ],
        out_specs=[tok(rw), tok(rw), tok(wxbc.shape[1]), tok(LANES)],
        out_shape=[out_bf(rw), out_bf(rw), out_bf(wxbc.shape[1]), jax.ShapeDtypeStruct((b, L, LANES), F32)],
        compiler_params=pltpu.CompilerParams(vmem_limit_bytes=VMEM_LIMIT),
        name="inproj_ctx",
    )(ctx, mod3, mod3, nw, wk, wv, wxbc, wdt)


def _ssd_kernel(xbc_ref, z_ref, dt_ref, xbcc_ref, dtc_ref, cw_ref, cb_ref, dtb_ref, alog_ref, dsk_ref, nw_ref,
                y_ref,
                xpad, xpadc, u, uc, dtv, dav, dtcv, dacv, sf_scr, kb_scr, acum, ecum, dec_scr,
                arow_scr, erow_scr, dtrow_scr):
    L = xbc_ref.shape[1]
    Lc = xbcc_ref.shape[1]
    nch = L // CHUNK
    nchc = Lc // CHUNK
    win = CHUNK + 2 * CONV_HALO
    nconv = xbc_ref.shape[2]
    nh = SSD_HEADS

    def conv_setup(src_ref, pad_ref, length, row0):
        zeros = jnp.zeros((CONV_HALO, nconv), F32)
        pad_ref[row0:row0 + CONV_HALO, :] = zeros
        pad_ref[row0 + CONV_HALO + length:row0 + 2 * CONV_HALO + length, :] = zeros
        pad_ref[row0 + CONV_HALO:row0 + CONV_HALO + length, :] = src_ref[0].astype(F32)

    def conv_chunk(pad_ref, dst_ref, c, row0):
        base = c * CHUNK if isinstance(c, int) else pl.multiple_of(c * CHUNK, CHUNK)
        for cb_i in range(nconv // LANES):
            cols = slice(cb_i * LANES, (cb_i + 1) * LANES)
            w = pad_ref[pl.ds(base + row0, win), cols]
            acc = cb_ref[:, cols] + w[CONV_HALO:CONV_HALO + CHUNK] * cw_ref[SSD_CONV // 2:SSD_CONV // 2 + 1, cols]
            for j in range(SSD_CONV):
                if j == SSD_CONV // 2:
                    continue
                shifted = pltpu.roll(w, (SSD_CONV // 2 - j) % win, 0)
                acc = acc + shifted[CONV_HALO:CONV_HALO + CHUNK] * cw_ref[j:j + 1, cols]
            dst_ref[pl.ds(base, CHUNK), cols] = _silu(acc).astype(BF16)

    conv_setup(xbcc_ref, xpadc, Lc, 0)
    for c in range(nchc):
        conv_chunk(xpadc, uc, c, 0)
    conv_setup(xbc_ref, xpad, L, CHUNK)

    a_neg = -jnp.exp(alog_ref[...])
    dtv[...] = jax.nn.softplus(dt_ref[0] + dtb_ref[...])
    dav[...] = dtv[...] * a_neg
    dtcv[...] = jax.nn.softplus(dtc_ref[0] + dtb_ref[...])
    dacv[...] = dtcv[...] * a_neg

    row_i = lax.broadcasted_iota(jnp.int32, (CHUNK, CHUNK), 0)
    col_i = lax.broadcasted_iota(jnp.int32, (CHUNK, CHUNK), 1)
    causal = col_i <= row_i
    lo_half = col_i < SSD_HEADDIM
    fwd_lane = col_i < nh
    head_of = lax.broadcasted_iota(jnp.int32, (CHUNK, SSD_WIDTH), 1) // SSD_HEADDIM
    src_col = lax.broadcasted_iota(jnp.int32, (CHUNK, SSD_WIDTH), 0)
    exp_f = (head_of == src_col).astype(BF16)
    exp_b = (head_of == src_col - nh).astype(BF16)
    exp_fb = jnp.concatenate([exp_f, exp_b], axis=1)

    def split3(v):
        hi = v.astype(BF16)
        r1 = v - hi.astype(F32)
        mid = r1.astype(BF16)
        return hi, mid, (r1 - mid.astype(F32)).astype(BF16)

    def times_onehot(v, m, passes=3):
        parts = split3(v)[:passes]
        acc = _dot(parts[0], m)
        for part in parts[1:]:
            acc = acc + _dot(part, m)
        return acc

    def colb(mat, r):
        return jnp.broadcast_to(mat[:, r:r + 1], (CHUNK, CHUNK))

    def pair_sel(a, b_):
        return jnp.where(lo_half, a, b_)

    gw = 2 * LANES

    def chunk_terms(u_ref, dt_s, da_s, base):
        dt = dt_s[pl.ds(base, CHUNK), :]
        da = da_s[pl.ds(base, CHUNK), :]
        acol = da
        for step in (1, 2, 4, 8, 16, 32, 64):
            acol = acol + jnp.where(row_i >= step, pltpu.roll(acol, step, 0), 0.0)
        ecol = acol - da
        last = acol[CHUNK - 1:CHUNK, :]
        wgt = jnp.where(fwd_lane, jnp.exp(last - acol), jnp.exp(ecol)) * dt
        scale = jnp.where(fwd_lane, jnp.exp(acol), jnp.exp(last - ecol))
        wide = times_onehot(jnp.concatenate([wgt, scale], axis=0), exp_fb, passes=1)
        dec = times_onehot(jnp.broadcast_to(jnp.exp(last), (SUBLANES, LANES)), exp_fb)[0:1]
        xs = u_ref[pl.ds(base, CHUNK), 0:SSD_WIDTH].astype(F32)
        kmats = []
        for g in range(SSD_GROUPS):
            xw = jnp.concatenate([xs[:, g * gw:(g + 1) * gw] * wide[:CHUNK, g * gw:(g + 1) * gw],
                                  xs[:, g * gw:(g + 1) * gw] * wide[:CHUNK, SSD_WIDTH + g * gw:SSD_WIDTH + (g + 1) * gw]],
                                 axis=1).astype(BF16)
            bm = u_ref[pl.ds(base, CHUNK), SSD_WIDTH + g * SSD_STATE:SSD_WIDTH + (g + 1) * SSD_STATE]
            kmats.append(_dot_tn(bm, xw))
        return dt, acol, ecol, wide[CHUNK:], dec, kmats

    def advance(s, dec, kmats, backward):
        off = SSD_WIDTH if backward else 0
        koff = gw if backward else 0
        return [dec[:, off + g * gw:off + (g + 1) * gw] * s[g] + kmats[g][:, koff:koff + gw]
                for g in range(SSD_GROUPS)]

    ctx_terms = [chunk_terms(uc, dtcv, dacv, c * CHUNK) for c in range(nchc)]
    s_f0 = [jnp.zeros((SSD_STATE, gw), F32) for _ in range(SSD_GROUPS)]
    for c in range(nchc):
        s_f0 = advance(s_f0, ctx_terms[c][4], ctx_terms[c][5], False)
    s_b0 = [jnp.zeros((SSD_STATE, gw), F32) for _ in range(SSD_GROUPS)]
    for c in reversed(range(nchc)):
        s_b0 = advance(s_b0, ctx_terms[c][4], ctx_terms[c][5], True)

    def prep(c, carry):
        base = pl.multiple_of(c * CHUNK, CHUNK)
        conv_chunk(xpad, u, c, CHUNK)
        dt, acol, ecol, scale, dec, kmats = chunk_terms(u, dtv, dav, base)
        acum[pl.ds(base, CHUNK), :] = acol
        ecum[pl.ds(base, CHUNK), :] = ecol
        hrow = pl.ds(pl.multiple_of(c * 2 * nh, 2 * nh), 2 * nh)
        arow_scr[hrow, :] = acol.T[:2 * nh]
        erow_scr[hrow, :] = ecol.T[:2 * nh]
        dtrow_scr[hrow, :] = dt.T[:2 * nh]
        xpad[pl.ds(base, CHUNK), :] = scale
        dec_scr[pl.ds(pl.multiple_of(c * SUBLANES, SUBLANES), SUBLANES), :] = jnp.broadcast_to(dec, (SUBLANES, 2 * SSD_WIDTH))
        for g in range(SSD_GROUPS):
            sf_scr[c, g] = kmats[g][:, :gw]
            kb_scr[c, g] = kmats[g][:, gw:]
        return carry

    lax.fori_loop(0, nch, prep, 0, unroll=SSD_PREP_UNROLL)

    def chunk_dec(c):
        return dec_scr[pl.ds(pl.multiple_of(c * SUBLANES, SUBLANES), 1), :]

    def fwd(c, s_old):
        dec = chunk_dec(c)
        new = []
        for g in range(SSD_GROUPS):
            new.append(dec[:, g * gw:(g + 1) * gw] * s_old[g] + sf_scr[c, g])
            sf_scr[c, g] = s_old[g]
        return tuple(new)

    lax.fori_loop(0, nch, fwd, tuple(s_f0))

    def bwd(i, s_b):
        c = nch - 1 - i
        base = pl.multiple_of(c * CHUNK, CHUNK)
        acol = acum[pl.ds(base, CHUNK), :]
        ecol = ecum[pl.ds(base, CHUNK), :]
        hrow = pl.ds(pl.multiple_of(c * 2 * nh, 2 * nh), 2 * nh)
        arow = arow_scr[hrow, :]
        erow = erow_scr[hrow, :]
        dt_t = dtrow_scr[hrow, :]
        scale = xpad[pl.ds(base, CHUNK), :]
        ys = []
        for g in range(SSD_GROUPS):
            bm = u[pl.ds(base, CHUNK), SSD_WIDTH + g * SSD_STATE:SSD_WIDTH + (g + 1) * SSD_STATE]
            cm = u[pl.ds(base, CHUNK), SSD_WIDTH + (SSD_GROUPS + g) * SSD_STATE:SSD_WIDTH + (SSD_GROUPS + g + 1) * SSD_STATE]
            cbm = _dot_nt(cm, bm)
            cs_f = _dot(cm, sf_scr[c, g].astype(BF16))
            cs_b = _dot(cm, s_b[g].astype(BF16))
            for pp in range(SSD_PAIRS // SSD_GROUPS):
                p = g * (SSD_PAIRS // SSD_GROUPS) + pp
                xs_b = u[pl.ds(base, CHUNK), p * LANES:(p + 1) * LANES]
                y_h = []
                for hh in range(2):
                    r = 2 * p + hh
                    arg = jnp.where(causal, colb(acol, r) - arow[r:r + 1, :],
                                    erow[nh + r:nh + r + 1, :] - colb(ecol, nh + r))
                    coef = jnp.where(causal, dt_t[r:r + 1, :], dt_t[nh + r:nh + r + 1, :])
                    gm = (cbm * (jnp.exp(arg) * coef)).astype(BF16)
                    y_h.append(_dot(gm, xs_b))
                sl = slice(pp * LANES, (pp + 1) * LANES)
                wl = slice(p * LANES, (p + 1) * LANES)
                wlb = slice(SSD_WIDTH + p * LANES, SSD_WIDTH + (p + 1) * LANES)
                ys.append(pair_sel(y_h[0], y_h[1]) + cs_f[:, sl] * scale[:, wl] + cs_b[:, sl] * scale[:, wlb]
                          + dsk_ref[:, wl] * xs_b.astype(F32))
        y = jnp.concatenate(ys, axis=1)
        y = y * _silu(z_ref[0, pl.ds(base, CHUNK), :].astype(F32))
        ms = jnp.mean(y * y, axis=-1, keepdims=True)
        y_ref[0, pl.ds(base, CHUNK), :] = ((y * lax.rsqrt(ms + EPS)) * nw_ref[...]).astype(BF16)
        dec = chunk_dec(c)
        return tuple(dec[:, SSD_WIDTH + g * gw:SSD_WIDTH + (g + 1) * gw] * s_b[g] + kb_scr[c, g]
                     for g in range(SSD_GROUPS))

    lax.fori_loop(0, nch, bwd, tuple(s_b0), unroll=SSD_OUT_UNROLL)


def _ssd(xbc, z, dt, xbcc, dtc, conv_w8, conv_b, dt_bias, a_log, d_skip, norm_w):
    b, L, nconv = xbc.shape
    Lc = xbcc.shape[1]
    nch = L // CHUNK
    per_b = lambda n, w: pl.BlockSpec((1, n, w), lambda i: (i, 0, 0))
    return pl.pallas_call(
        _ssd_kernel,
        grid=(b,),
        in_specs=[
            per_b(L, nconv), per_b(L, SSD_WIDTH), per_b(L, LANES), per_b(Lc, nconv), per_b(Lc, LANES),
            _const_spec(conv_w8.shape), _const_spec(conv_b.shape), _const_spec(dt_bias.shape),
            _const_spec(a_log.shape), _const_spec(d_skip.shape), _const_spec(norm_w.shape),
        ],
        out_specs=per_b(L, SSD_WIDTH),
        out_shape=jax.ShapeDtypeStruct((b, L, SSD_WIDTH), BF16),
        scratch_shapes=[
            pltpu.VMEM((CHUNK + L + 2 * CONV_HALO, nconv), F32),
            pltpu.VMEM((Lc + 2 * CONV_HALO, nconv), F32),
            pltpu.VMEM((L, nconv), BF16),
            pltpu.VMEM((Lc, nconv), BF16),
            pltpu.VMEM((L, LANES), F32), pltpu.VMEM((L, LANES), F32),
            pltpu.VMEM((Lc, LANES), F32), pltpu.VMEM((Lc, LANES), F32),
            pltpu.VMEM((nch, SSD_GROUPS, SSD_STATE, 2 * LANES), F32),
            pltpu.VMEM((nch, SSD_GROUPS, SSD_STATE, 2 * LANES), F32),
            pltpu.VMEM((L, LANES), F32), pltpu.VMEM((L, LANES), F32),
            pltpu.VMEM((nch * SUBLANES, 2 * SSD_WIDTH), F32),
            pltpu.VMEM((nch * 2 * SSD_HEADS, CHUNK), F32), pltpu.VMEM((nch * 2 * SSD_HEADS, CHUNK), F32),
            pltpu.VMEM((nch * 2 * SSD_HEADS, CHUNK), F32),
        ],
        compiler_params=pltpu.CompilerParams(vmem_limit_bytes=VMEM_LIMIT_SSD),
        name="ssd",
    )(xbc, z, dt, xbcc, dtc, conv_w8, conv_b, dt_bias, a_log, d_skip, norm_w)


def _ret_kernel(q_ref, k_ref, v_ref, g_ref, kc_ref, vc_ref, df_ref, db_ref, gn_ref, y_ref, sf_scr):
    L = q_ref.shape[1]
    Lc = kc_ref.shape[1]
    nch = L // CHUNK
    dk = RET_DK
    row_i = lax.broadcasted_iota(jnp.int32, (CHUNK, dk), 0).astype(F32)
    col_i = lax.broadcasted_iota(jnp.int32, (CHUNK, dk), 1).astype(F32)
    rel = row_i - col_i
    crow = lax.broadcasted_iota(jnp.int32, (Lc, dk), 0).astype(F32)

    heads = []
    s_f0 = []
    s_b0 = []
    for h in range(RET_HEADS):
        cols = slice(h * dk, (h + 1) * dk)
        lg_f = -jnp.exp(df_ref[:, cols])
        lg_b = -jnp.exp(db_ref[:, cols])
        heads.append(dict(
            cols=cols,
            dmat=jnp.where(rel >= 0, jnp.exp(jnp.maximum(rel, 0.0) * lg_f), jnp.exp(jnp.maximum(-rel, 0.0) * lg_b)),
            dq_f=jnp.exp((row_i + 1.0) * lg_f),
            dq_b=jnp.exp((CHUNK - row_i) * lg_b),
            dk_f=jnp.exp((CHUNK - 1.0 - row_i) * lg_f),
            dk_b=jnp.exp(row_i * lg_b),
            dc_f=jnp.exp(CHUNK * lg_f),
            dc_b=jnp.exp(CHUNK * lg_b),
        ))
        kc = kc_ref[0, :, cols].astype(F32)
        vc = vc_ref[0, :, cols]
        s_f0.append(_dot_tn((kc * jnp.exp((Lc - 1.0 - crow) * lg_f)).astype(BF16), vc))
        s_b0.append(_dot_tn((kc * jnp.exp(crow * lg_b)).astype(BF16), vc))

    def fwd(c, s_f):
        base = pl.multiple_of(c * CHUNK, CHUNK)
        new = []
        for h, hd in enumerate(heads):
            sf_scr[c, h] = s_f[h]
            kk = k_ref[0, pl.ds(base, CHUNK), hd["cols"]].astype(F32)
            vv = v_ref[0, pl.ds(base, CHUNK), hd["cols"]]
            new.append(hd["dc_f"] * s_f[h] + _dot_tn((kk * hd["dk_f"]).astype(BF16), vv))
        return tuple(new)

    lax.fori_loop(0, nch, fwd, tuple(s_f0), unroll=RET_UNROLL)

    def bwd(i, s_bs):
        c = nch - 1 - i
        base = pl.multiple_of(c * CHUNK, CHUNK)
        new = []
        for h, hd in enumerate(heads):
            qq = q_ref[0, pl.ds(base, CHUNK), hd["cols"]]
            kk = k_ref[0, pl.ds(base, CHUNK), hd["cols"]]
            vv = v_ref[0, pl.ds(base, CHUNK), hd["cols"]]
            s_b = s_bs[h]
            scores = (_dot_nt(qq, kk) * hd["dmat"]).astype(BF16)
            y = (_dot(scores, vv)
                 + _dot(qq, sf_scr[c, h].astype(BF16)) * hd["dq_f"]
                 + _dot(qq, s_b.astype(BF16)) * hd["dq_b"])
            mu = jnp.mean(y, axis=-1, keepdims=True)
            yc = y - mu
            var = jnp.mean(yc * yc, axis=-1, keepdims=True)
            yn = (yc * lax.rsqrt(var + EPS)) * gn_ref[:, hd["cols"]]
            gate = _silu(g_ref[0, pl.ds(base, CHUNK), hd["cols"]].astype(F32))
            y_ref[0, pl.ds(base, CHUNK), hd["cols"]] = (yn * gate).astype(BF16)
            new.append(hd["dc_b"] * s_b + _dot_tn((kk.astype(F32) * hd["dk_b"]).astype(BF16), vv))
        return tuple(new)

    lax.fori_loop(0, nch, bwd, tuple(s_b0), unroll=RET_UNROLL)


def _retention(q, k, v, g, kc, vc, decay_f, decay_b, gn_w):
    b, L, w = q.shape
    Lc = kc.shape[1]
    nch = L // CHUNK
    per_b = lambda n: pl.BlockSpec((1, n, w), lambda i: (i, 0, 0))
    return pl.pallas_call(
        _ret_kernel,
        grid=(b,),
        in_specs=[per_b(L), per_b(L), per_b(L), per_b(L), per_b(Lc), per_b(Lc),
                  _const_spec((1, w)), _const_spec((1, w)), _const_spec((1, w))],
        out_specs=per_b(L),
        out_shape=jax.ShapeDtypeStruct((b, L, w), BF16),
        scratch_shapes=[
            pltpu.VMEM((nch, RET_HEADS, RET_DK, RET_DK), F32),
        ],
        compiler_params=pltpu.CompilerParams(vmem_limit_bytes=VMEM_LIMIT),
        name="retention",
    )(q, k, v, g, kc, vc, decay_f, decay_b, gn_w)


def _outproj_router_kernel(yr_ref, ys_ref, x_ref, g1_ref, sh2_ref, sc2_ref, npost_ref, npre_ref,
                           wor_ref, wos_ref, wr_ref, br_ref, tri_ref,
                           x1_ref, h2_ref, route_ref, slots_ref, seg_ref,
                           wcat):
    i = pl.program_id(0)

    @pl.when(i == 0)
    def _():
        wr = wr_ref[...]
        hi = wr.astype(BF16)
        wcat[:, :LANES] = hi
        wcat[:, LANES:] = (wr - hi.astype(F32)).astype(BF16)

    _route_tile(yr_ref, ys_ref, x_ref, g1_ref, sh2_ref, sc2_ref, npost_ref, npre_ref, wor_ref, wos_ref, br_ref,
                tri_ref, x1_ref, h2_ref, route_ref, slots_ref, seg_ref, wcat)


def _route_tile(yr_ref, ys_ref, x_ref, g1_ref, sh2_ref, sc2_ref, npost_ref, npre_ref, wor_ref, wos_ref, br_ref,
                tri_ref, x1_ref, h2_ref, route_ref, slots_ref, seg_ref, wcat):
    tm = x_ref.shape[0]
    rows = slice(0, tm)
    y = _dot(yr_ref[rows, :], wor_ref[...]) + _dot(ys_ref[rows, :], wos_ref[...])
    ms = jnp.mean(y * y, axis=-1, keepdims=True)
    x1 = x_ref[rows, :] + (y * lax.rsqrt(ms + EPS)) * (g1_ref[0] * npost_ref[...])
    x1_ref[rows, :] = x1
    h2 = _norm_mod(x1, npre_ref[...], sc2_ref[0], sh2_ref[0])
    h2_ref[rows, :] = h2.astype(BF16)

    h_hi = h2.astype(BF16)
    h_lo = (h2 - h_hi.astype(F32)).astype(BF16)
    both = _dot(h_hi, wcat[...])
    lg = both[:, :LANES] + both[:, LANES:] + _dot(h_lo, wcat[:, :LANES]) + br_ref[...]

    lane = lax.broadcasted_iota(jnp.int32, (tm, LANES), 1)
    lane_f = lane.astype(F32)
    is_grp = (lane >= N_EXPERTS) & (lane < N_EXPERTS + MOE_GROUPS)
    gl = jnp.where(is_grp, lg, NEG_BIG)
    mg = jnp.max(gl, axis=-1, keepdims=True)
    grp_lane = jnp.min(jnp.where(gl == mg, lane_f, 1e9), axis=-1, keepdims=True)
    p_g = 1.0 / jnp.sum(jnp.where(is_grp, jnp.exp(gl - mg), 0.0), axis=-1, keepdims=True)
    first = (grp_lane - N_EXPERTS) * EXPERTS_PER_GROUP
    in_grp = (lane_f >= first) & (lane_f < first + EXPERTS_PER_GROUP)
    el = jnp.where(in_grp, lg, NEG_BIG)
    t1 = jnp.max(el, axis=-1, keepdims=True)
    i1 = jnp.min(jnp.where(el == t1, lane_f, 1e9), axis=-1, keepdims=True)
    el2 = jnp.where(lane_f == i1, NEG_BIG, el)
    t2 = jnp.max(el2, axis=-1, keepdims=True)
    i2 = jnp.min(jnp.where(el2 == t2, lane_f, 1e9), axis=-1, keepdims=True)
    s = jnp.exp(t2 - t1)
    w1 = p_g / (1.0 + s)
    w2 = p_g * s / (1.0 + s)

    oh1 = (lane_f == i1)
    oh2 = (lane_f == i2)
    ohf = jnp.where(oh1 | oh2, 1.0, 0.0)
    before = _dot(tri_ref[...], ohf.astype(BF16))
    cnt = jnp.sum(ohf, axis=0, keepdims=True)
    seg = jnp.floor((cnt + (SEG_ALIGN - 1.0)) * (1.0 / SEG_ALIGN)) * SEG_ALIGN
    e_row = lax.broadcasted_iota(jnp.int32, (LANES, LANES), 0)
    e_col = lax.broadcasted_iota(jnp.int32, (LANES, LANES), 1)
    earlier = (e_row < e_col).astype(BF16)
    seg_off = _dot(jnp.broadcast_to(seg, (SUBLANES, LANES)).astype(BF16), earlier)[0:1]
    where_to = before + seg_off
    lpos1 = jnp.sum(jnp.where(oh1, where_to, 0.0), axis=-1, keepdims=True)
    lpos2 = jnp.sum(jnp.where(oh2, where_to, 0.0), axis=-1, keepdims=True)

    cols = [w1, w2, lpos1, lpos2]
    for wk in (w1, w2):
        hi = wk.astype(BF16).astype(F32)
        cols += [hi, wk - hi]
    cols.append(jnp.ones_like(w1))
    packed = jnp.zeros((tm, LANES), F32)
    for k, col in enumerate(cols):
        packed = jnp.where(lane == k, col, packed)
    route_ref[rows, :] = packed

    row = lax.broadcasted_iota(jnp.int32, (tm, LANES), 0)
    on_diag = (row % LANES) == lane
    per = tm // LANES
    for qi, col in enumerate((lpos1, lpos2)):
        picked = jnp.where(on_diag, col, 0.0)
        dense = jnp.sum(picked.reshape(per, LANES, LANES), axis=1).astype(jnp.int32)
        slots_ref[0, qi * per:(qi + 1) * per, :] = dense
    tbl_row = lax.broadcasted_iota(jnp.int32, (SUBLANES, LANES), 0)
    seg_ref[0] = jnp.where(tbl_row == 0, seg, jnp.where(tbl_row == 1, seg_off, 0.0)).astype(jnp.int32)


def _outproj_router(yr, ys, x2, mod3, npost, npre, wo_r, wo_s, w_router, b_router, seq_len):
    T, d = x2.shape
    tm = TM_OUT
    per_seq = seq_len // tm
    rw = yr.shape[1]
    tri = (jnp.arange(tm)[:, None] > jnp.arange(tm)[None, :]).astype(BF16)
    tok = lambda w: pl.BlockSpec((tm, w), lambda i: (i, 0))
    modv = lambda k: pl.BlockSpec((1, 1, d), lambda i: (i // per_seq, 0, k))
    tile3 = lambda r: pl.BlockSpec((1, r, LANES), lambda i: (i, 0, 0))
    slot_rows = TOP_K * (tm // LANES)
    return pl.pallas_call(
        _outproj_router_kernel,
        grid=(T // tm,),
        in_specs=[
            tok(rw), tok(rw), tok(d), modv(2), modv(3), modv(4),
            _const_spec((1, d)), _const_spec((1, d)),
            _const_spec(wo_r.shape), _const_spec(wo_s.shape), _const_spec(w_router.shape), _const_spec((1, LANES)),
            _const_spec((tm, tm)),
        ],
        out_specs=[tok(d), tok(d), tok(LANES), tile3(slot_rows), tile3(SUBLANES)],
        out_shape=[jax.ShapeDtypeStruct((T, d), F32), jax.ShapeDtypeStruct((T, d), BF16),
                   jax.ShapeDtypeStruct((T, LANES), F32),
                   jax.ShapeDtypeStruct((T // tm, slot_rows, LANES), jnp.int32),
                   jax.ShapeDtypeStruct((T // tm, SUBLANES, LANES), jnp.int32)],
        scratch_shapes=[pltpu.VMEM((d, 2 * LANES), BF16)],
        compiler_params=pltpu.CompilerParams(dimension_semantics=("arbitrary",),
                                             vmem_limit_bytes=VMEM_LIMIT),
        name="outproj_router",
    )(yr, ys, x2, mod3, mod3, mod3, npost, npre, wo_r, wo_s, w_router, b_router, tri)


def _expert_kernel(be_ref, first_ref, slot_ref, next_ref, nused_ref, xs_ref, wg_hbm, wu_hbm, wd_hbm, y_ref,
                   wg_f, wu_f, wd_f, wg_b, wu_b, wd_b, sem):
    def fetch(e, s):
        return [pltpu.make_async_copy(src.at[e], dst.at[s], sem.at[s, k])
                for k, (src, dst) in enumerate(((wg_hbm, wg_f), (wu_hbm, wu_f), (wd_hbm, wd_f)))]

    @pl.when(pl.program_id(0) == 0)
    def _():
        for c in fetch(be_ref[0], slot_ref[0]):
            c.start()

    for sb in range(EXPERT_BLOCKS_PER_STEP):
        _expert_block(pl.program_id(0) * EXPERT_BLOCKS_PER_STEP + sb, slice(sb * MB_EXPERT, (sb + 1) * MB_EXPERT),
                      fetch, be_ref, first_ref, slot_ref, next_ref, nused_ref, xs_ref, y_ref,
                      wg_f, wu_f, wd_f, wg_b, wu_b, wd_b)


def _expert_block(i, rows, fetch, be_ref, first_ref, slot_ref, next_ref, nused_ref, xs_ref, y_ref,
                  wg_f, wu_f, wd_f, wg_b, wu_b, wd_b):
    @pl.when(first_ref[i] == 1)
    def _():
        s = slot_ref[i]

        @pl.when(next_ref[i] >= 0)
        def _():
            for c in fetch(next_ref[i], 1 - s):
                c.start()

        for c in fetch(be_ref[i], s):
            c.wait()
        wg_b[...] = wg_f[s].astype(BF16)
        wu_b[...] = wu_f[s].astype(BF16)
        wd_b[...] = wd_f[s].astype(BF16)

    @pl.when(i < nused_ref[0])
    def _():
        d = y_ref.shape[1]
        side = xs_ref[rows, d:d + LANES].astype(F32)
        second = side[:, 8:9] == 2.0
        unscale = jnp.where(second, 0.5, 1.0)
        weight = jnp.where(second, side[:, 6:7] + side[:, 7:8], side[:, 4:5] + side[:, 5:6]) * unscale
        xb = xs_ref[rows, 0:d] * unscale.astype(BF16)
        hid = (_silu(_dot(xb, wg_b[...])) * _dot(xb, wu_b[...])).astype(BF16)
        y_ref[rows, :] = (_dot(hid, wd_b[...]) * weight).astype(BF16)

    @pl.when(i >= nused_ref[0])
    def _():
        y_ref[rows, :] = jnp.zeros((rows.stop - rows.start, y_ref.shape[1]), y_ref.dtype)


def _expert_plan(padded, pad_end, n_blocks, mb):
    n_used = (pad_end[-1:] // mb).astype(jnp.int32)
    blk_start = jnp.arange(n_blocks, dtype=jnp.int32) * mb
    experts = jnp.arange(N_EXPERTS, dtype=jnp.int32)
    blk_expert = jnp.minimum(jnp.sum((pad_end[None, :] <= blk_start[:, None]).astype(jnp.int32), axis=1),
                             N_EXPERTS - 1)
    prev = jnp.concatenate([jnp.full((1,), -1, jnp.int32), blk_expert[:-1]])
    first = ((blk_start < pad_end[-1]) & (blk_expert != prev)).astype(jnp.int32)
    slot = (jnp.cumsum(first) - 1) % 2
    later = jnp.where((padded > 0)[None, :] & (experts[None, :] > experts[:, None]), experts[None, :], N_EXPERTS)
    next_nonempty = jnp.min(later, axis=1)
    next_nonempty = jnp.where(next_nonempty == N_EXPERTS, -1, next_nonempty)
    nxt = jnp.sum(jnp.where(blk_expert[:, None] == experts[None, :], next_nonempty[None, :], 0), axis=1)
    return [a.astype(jnp.int32) for a in (blk_expert, first, slot, nxt, n_used)]


def _experts(plan, xs, w_gate, w_up, w_down):
    cap, dp = xs.shape
    n_exp, d, de = w_gate.shape
    step_rows = MB_EXPERT * EXPERT_BLOCKS_PER_STEP
    assert dp == d + LANES and cap % step_rows == 0
    grid_spec = pltpu.PrefetchScalarGridSpec(
        num_scalar_prefetch=len(plan),
        grid=(cap // step_rows,),
        in_specs=[
            pl.BlockSpec((step_rows, dp), lambda i, be, fi, sl, nx, nu: (
                jnp.minimum(i, (nu[0] - 1) // EXPERT_BLOCKS_PER_STEP), 0)),
            pl.BlockSpec(memory_space=pl.ANY), pl.BlockSpec(memory_space=pl.ANY), pl.BlockSpec(memory_space=pl.ANY),
        ],
        out_specs=pl.BlockSpec((step_rows, d), lambda i, be, fi, sl, nx, nu: (i, 0)),
        scratch_shapes=[pltpu.VMEM((2, d, de), F32), pltpu.VMEM((2, d, de), F32), pltpu.VMEM((2, de, d), F32),
                        pltpu.VMEM((d, de), BF16), pltpu.VMEM((d, de), BF16), pltpu.VMEM((de, d), BF16),
                        pltpu.SemaphoreType.DMA((2, 3))],
    )
    return pl.pallas_call(
        _expert_kernel,
        grid_spec=grid_spec,
        out_shape=jax.ShapeDtypeStruct((cap, d), BF16),
        compiler_params=pltpu.CompilerParams(dimension_semantics=("arbitrary",),
                                             vmem_limit_bytes=VMEM_LIMIT),
        name="experts",
    )(*plan, xs, w_gate, w_up, w_down)


def _segment_pieces(tile, segrow_ref, seglen_ref, segoff_ref, act):
    def per_expert(e, carry):
        idx = tile * N_EXPERTS + e
        g0 = segrow_ref[idx]
        l0 = segoff_ref[idx]

        def piece(j, c2):
            act(pl.multiple_of(l0 + j * SEG_ALIGN, SEG_ALIGN), pl.multiple_of(g0 + j * SEG_ALIGN, SEG_ALIGN))
            return c2

        lax.fori_loop(0, seglen_ref[idx], piece, 0)
        return carry

    lax.fori_loop(0, N_EXPERTS, per_expert, 0)


def _wait_rows(total, row_copy):
    size = SEG_ALIGN
    while size <= _stage_rows(max(TD_DISPATCH, TF_COMBINE)):
        @pl.when((total & size) != 0)
        def _(size=size):
            row_copy(size).wait()
        size *= 2


def _for_tile_rows(tile_rows, max_rows, body):
    @pl.when(tile_rows <= COMMON_STAGE_ROWS)
    def _():
        body(COMMON_STAGE_ROWS)

    @pl.when(tile_rows > COMMON_STAGE_ROWS)
    def _():
        body(max_rows)


def _stage_rows(tile_tokens):
    return TOP_K * tile_tokens + N_EXPERTS * SEG_ALIGN


def _dispatch_seg_kernel(pad_end_ref, zero_from_ref, segrow_ref, seglen_ref, segoff_ref, tilerows_ref,
                         lpos_ref, route_ref, h_ref, xs_hbm, zbuf, stage, sem, zsem):
    i = pl.program_id(0)
    nt = pl.num_programs(0)
    td, d = h_ref.shape
    sr = stage.shape[1]
    per = td // LANES

    def zero_fills(act):
        def fill(row, n):
            act(pltpu.make_async_copy(zbuf.at[pl.ds(0, n)], xs_hbm.at[pl.ds(pl.multiple_of(row, SEG_ALIGN), n)], zsem))

        def region(e, carry):
            row = zero_from_ref[e]
            left = pad_end_ref[e] - row
            size = ZERO_PIECE
            while size >= SEG_ALIGN:
                take = (left & size) != 0

                @pl.when(take)
                def _(row=row, size=size):
                    fill(row, size)

                row = row + jnp.where(take, size, 0)
                size //= 2
            return carry

        def tail(p, carry):
            @pl.when(p * ZERO_PIECE >= pad_end_ref[N_EXPERTS - 1])
            def _():
                fill(p * ZERO_PIECE, ZERO_PIECE)
            return carry

        lax.fori_loop(0, N_EXPERTS, region, 0)
        lax.fori_loop(0, xs_hbm.shape[0] // ZERO_PIECE, tail, 0)

    @pl.when(i == 0)
    def _():
        zbuf[...] = jnp.zeros_like(zbuf)
        zero_fills(lambda c: c.start())

    def shipped(tile):
        _wait_rows(tilerows_ref[tile], lambda n: pltpu.make_async_copy(
            stage.at[tile % 2, pl.ds(0, n)], xs_hbm.at[pl.ds(0, n)], sem.at[tile % 2]))

    @pl.when(i >= 2)
    def _():
        shipped(i - 2)

    slot = i % 2

    def sort_rows(n_rows):
        srow = lax.broadcasted_iota(jnp.int32, (n_rows, LANES), 0)
        place = jnp.concatenate(
            [jnp.where(srow == lpos_ref[0, cb:cb + 1, :], 1.0,
                       jnp.where(srow == lpos_ref[0, per + cb:per + cb + 1, :], 2.0, 0.0)) for cb in range(per)],
            axis=1).astype(BF16)
        stage[slot, 0:n_rows, :] = _dot(place, jnp.concatenate([h_ref[...], route_ref[...].astype(BF16)],
                                                               axis=1)).astype(BF16)

    _for_tile_rows(tilerows_ref[i], sr, sort_rows)
    _segment_pieces(i, segrow_ref, seglen_ref, segoff_ref, lambda lr, gr: pltpu.make_async_copy(
        stage.at[slot, pl.ds(lr, SEG_ALIGN)], xs_hbm.at[pl.ds(gr, SEG_ALIGN)], sem.at[slot]).start())

    @pl.when(i == nt - 1)
    def _():
        @pl.when(i >= 1)
        def _():
            shipped(i - 1)
        shipped(i)
        zero_fills(lambda c: c.wait())


def _dispatch_seg(plan, lpos, route, h2, cap):
    T, d = h2.shape
    td = TD_DISPATCH
    nt = T // td
    width = d + LANES
    sr = _stage_rows(td)
    n_pre = len(plan)
    grid_spec = pltpu.PrefetchScalarGridSpec(
        num_scalar_prefetch=n_pre,
        grid=(nt,),
        in_specs=[
            pl.BlockSpec((1, lpos.shape[1], LANES), lambda i, *_: (i, 0, 0)),
            pl.BlockSpec((td, LANES), lambda i, *_: (i, 0)),
            pl.BlockSpec((td, d), lambda i, *_: (i, 0)),
        ],
        out_specs=pl.BlockSpec(memory_space=pl.ANY),
        scratch_shapes=[pltpu.VMEM((ZERO_PIECE, width), BF16), pltpu.VMEM((2, sr, width), BF16),
                        pltpu.SemaphoreType.DMA((2,)), pltpu.SemaphoreType.DMA(())],
    )
    return pl.pallas_call(
        _dispatch_seg_kernel,
        grid_spec=grid_spec,
        out_shape=jax.ShapeDtypeStruct((cap, width), BF16),
        compiler_params=pltpu.CompilerParams(dimension_semantics=("arbitrary",), vmem_limit_bytes=VMEM_LIMIT),
        name="dispatch",
    )(*plan, lpos, route, h2)


def _combine_seg_kernel(segrow_ref, seglen_ref, segoff_ref, tilerows_ref, route_ref, x1_ref, g2_ref, nw_ref,
                        yb_hbm, o_ref, stage, sem):
    i = pl.program_id(0)
    nt = pl.num_programs(0)
    tf = x1_ref.shape[0]
    sr = stage.shape[1]

    def fetch(tile, slot):
        _segment_pieces(tile, segrow_ref, seglen_ref, segoff_ref, lambda lr, gr: pltpu.make_async_copy(
            yb_hbm.at[pl.ds(gr, SEG_ALIGN)], stage.at[slot, pl.ds(lr, SEG_ALIGN)], sem.at[slot]).start())

    @pl.when(i == 0)
    def _():
        stage[...] = jnp.zeros_like(stage)
        fetch(i, 0)

    for slot in range(2):
        @pl.when((i + 1 < nt) & (i % 2 != slot))
        def _(slot=slot):
            fetch(i + 1, slot)

    _wait_rows(tilerows_ref[i], lambda n: pltpu.make_async_copy(
        yb_hbm.at[pl.ds(0, n)], stage.at[i % 2, pl.ds(0, n)], sem.at[i % 2]))

    local_row = route_ref[:, 2:2 + TOP_K].astype(jnp.int32)

    def unsort_rows(n_rows):
        scol = lax.broadcasted_iota(jnp.int32, (tf, n_rows), 1)
        pick = jnp.where((scol == local_row[:, 0:1]) | (scol == local_row[:, 1:2]), 1.0, 0.0).astype(BF16)
        out = _dot(pick, stage[i % 2, 0:n_rows, :])
        ms = jnp.mean(out * out, axis=-1, keepdims=True)
        o_ref[...] = x1_ref[...] + g2_ref[0] * ((out * lax.rsqrt(ms + EPS)) * nw_ref[...])

    _for_tile_rows(tilerows_ref[i], sr, unsort_rows)


def _combine_seg(seg_plan, route, x1, mod3, nw, yb, seq_len):
    T, d = x1.shape
    tf = TF_COMBINE
    nt = T // tf
    per_seq = seq_len // tf
    grid_spec = pltpu.PrefetchScalarGridSpec(
        num_scalar_prefetch=len(seg_plan),
        grid=(nt,),
        in_specs=[
            pl.BlockSpec((tf, LANES), lambda i, *_: (i, 0)),
            pl.BlockSpec((tf, d), lambda i, *_: (i, 0)),
            pl.BlockSpec((1, 1, d), lambda i, *_: (i // per_seq, 0, 5)),
            pl.BlockSpec((1, d), lambda i, *_: (0, 0)),
            pl.BlockSpec(memory_space=pl.ANY),
        ],
        out_specs=pl.BlockSpec((tf, d), lambda i, *_: (i, 0)),
        scratch_shapes=[pltpu.VMEM((2, _stage_rows(tf), d), yb.dtype), pltpu.SemaphoreType.DMA((2,))],
    )
    return pl.pallas_call(
        _combine_seg_kernel,
        grid_spec=grid_spec,
        out_shape=jax.ShapeDtypeStruct((T, d), F32),
        compiler_params=pltpu.CompilerParams(dimension_semantics=("arbitrary",), vmem_limit_bytes=VMEM_LIMIT),
        name="combine",
    )(*seg_plan, route, x1, mod3, nw, yb)


def _rope_tables(L, n_heads):
    quarter = RET_DK // 4
    freqs = ROPE_BASE ** (-jnp.arange(quarter, dtype=F32) / quarter)
    t = jnp.arange(L)
    ang_r = (t // GRID_W).astype(F32)[:, None] * freqs
    ang_c = (t % GRID_W).astype(F32)[:, None] * freqs
    cos = jnp.concatenate([jnp.cos(ang_r)] * 2 + [jnp.cos(ang_c)] * 2, axis=-1)
    sin = jnp.concatenate([-jnp.sin(ang_r), jnp.sin(ang_r), -jnp.sin(ang_c), jnp.sin(ang_c)], axis=-1)
    return jnp.tile(cos, (1, n_heads)), jnp.tile(sin, (1, n_heads))


def _lane_pad(v, width=LANES):
    return jnp.pad(v, [(0, 0)] * (v.ndim - 1) + [(0, width - v.shape[-1])])


def kernel(x, c, ctx, c_ctx, w_mod, b_mod, norm_pre_mix, norm_post_mix, norm_pre_ffn, norm_post_ffn, w_in, w_out, ret_decay_f, ret_decay_b, ret_gn_w, ssd_conv_w, ssd_conv_b, ssd_dt_bias_f, ssd_dt_bias_b, ssd_a_log_f, ssd_a_log_b, ssd_d, ssd_norm_w, moe_w_rg, moe_b_rg, moe_w_re, moe_b_re, moe_w_gate, moe_w_up, moe_w_down):
    b, L, d = x.shape
    assert w_mod.shape[0] == 1, "single layer: context outputs are never needed"
    assert TM_OUT == TD_DISPATCH == TF_COMBINE, "router, dispatch and combine share one slot-row layout"
    rw = RET_HEADS * RET_DK
    nconv = SSD_WIDTH + 2 * SSD_GROUPS * SSD_STATE
    T = b * L

    mod_rows = -(-(b + 1) // SUBLANES) * SUBLANES
    c_all = jnp.zeros((mod_rows, d), F32).at[:b].set(c).at[b].set(c_ctx)
    mod3 = _modulation(c_all, w_mod[0], b_mod[0]).reshape(mod_rows, 1, 6 * d)

    wi = w_in[0]
    o = 0
    wq = wi[:, o:o + rw]; o += rw
    wk = wi[:, o:o + rw]; o += rw
    wv = wi[:, o:o + rw]; o += rw
    wg = wi[:, o:o + rw]; o += rw
    wz = wi[:, o:o + SSD_WIDTH]; o += SSD_WIDTH
    wxbc = wi[:, o:o + nconv].astype(BF16); o += nconv
    wdt = _lane_pad(wi[:, o:o + 2 * SSD_HEADS]).astype(BF16)
    wqk = jnp.concatenate([wq, wk], axis=1).astype(BF16)
    wvgz = jnp.concatenate([wv, wg, wz], axis=1).astype(BF16)
    cos_t, sin_t = _rope_tables(L, RET_HEADS)
    nw1 = norm_pre_mix[0].reshape(1, d)

    q, k, v, g, z, xbc, dt = _inproj(x, mod3, nw1, wqk, wvgz, wxbc, wdt, cos_t, sin_t)
    kc, vc, xbcc, dtc = _inproj_ctx(ctx, mod3, b, nw1, wk.astype(BF16), wv.astype(BF16), wxbc, wdt)

    conv_w8 = jnp.pad(ssd_conv_w[0], ((0, SUBLANES - SSD_CONV), (0, 0)))
    dt_bias = _lane_pad(jnp.concatenate([ssd_dt_bias_f[0], ssd_dt_bias_b[0]])[None, :])
    a_log = _lane_pad(jnp.concatenate([ssd_a_log_f[0], ssd_a_log_b[0]])[None, :])
    d_skip = jnp.repeat(ssd_d[0], SSD_HEADDIM)[None, :]
    ys = _ssd(xbc, z, dt, xbcc, dtc, conv_w8, ssd_conv_b[0][None, :], dt_bias, a_log, d_skip,
              ssd_norm_w[0][None, :])

    yr = _retention(q, k, v, g, kc, vc,
                    jnp.repeat(ret_decay_f[0], RET_DK)[None, :], jnp.repeat(ret_decay_b[0], RET_DK)[None, :],
                    ret_gn_w[0][None, :])

    wo = w_out[0].astype(BF16)
    w_router = _lane_pad(jnp.concatenate(
        [jnp.transpose(moe_w_re[0], (1, 0, 2)).reshape(d, N_EXPERTS), moe_w_rg[0]], axis=1))
    b_router = _lane_pad(jnp.concatenate([moe_b_re[0].reshape(-1), moe_b_rg[0]])[None, :])
    x1, h2, route, lpos, seg = _outproj_router(
        yr.reshape(T, rw), ys.reshape(T, SSD_WIDTH), x.reshape(T, d), mod3,
        norm_post_mix[0][None, :], norm_pre_ffn[0][None, :], wo[:rw], wo[rw:], w_router, b_router, L)

    mb = MB_EXPERT
    nt = T // TM_OUT
    n_blocks = -(-(T * TOP_K + nt * N_EXPERTS * (SEG_ALIGN - 1) + N_EXPERTS * (mb - 1)) // mb)
    n_blocks = -(-n_blocks // EXPERT_BLOCKS_PER_STEP) * EXPERT_BLOCKS_PER_STEP
    seg_len = seg[:, 0, :N_EXPERTS]
    seg_off = seg[:, 1, :N_EXPERTS]
    used = jnp.sum(seg_len, axis=0)
    padded = (used + mb - 1) // mb * mb
    pad_end = jnp.cumsum(padded)
    pad_start = pad_end - padded
    seg_row = pad_start[None, :] + jnp.cumsum(seg_len, axis=0) - seg_len
    seg_plan = [a.reshape(-1).astype(jnp.int32)
                for a in (seg_row, seg_len // SEG_ALIGN, seg_off, jnp.sum(seg_len, axis=1))]
    zero_from = (pad_start + used).astype(jnp.int32)

    xs = _dispatch_seg([pad_end.astype(jnp.int32), zero_from] + seg_plan, lpos, route, h2, n_blocks * mb)
    yb = _experts(_expert_plan(padded, pad_end, n_blocks, mb), xs, moe_w_gate[0], moe_w_up[0], moe_w_down[0])
    out = _combine_seg(seg_plan, route, x1, mod3, norm_post_ffn[0][None, :], yb, L)
    return out.reshape(b, L, d)
```

```python
import jax
import jax.numpy as jnp
from jax import lax
from jax.experimental import pallas as pl
from jax.experimental.pallas import tpu as pltpu

F32 = jnp.float32
BF16 = jnp.bfloat16

LANES = 128
SUBLANES = 8
BF16_TILE_ROWS = 16
V7X_VMEM_BYTES = 64 * 1024 * 1024
VMEM_LIMIT = V7X_VMEM_BYTES * 3 // 4
VMEM_LIMIT_SSD = V7X_VMEM_BYTES * 7 // 8

EPS = 1e-6
CHUNK = 128
GRID_W = 64
RET_HEADS = 4
RET_DK = 128
ROPE_BASE = 10000.0
SSD_HEADS = 8
SSD_HEADDIM = 64
SSD_GROUPS = 2
SSD_STATE = 128
SSD_WIDTH = SSD_HEADS * SSD_HEADDIM
SSD_CONV = 5
SSD_PAIRS = SSD_WIDTH // LANES
MOE_GROUPS = 4
EXPERTS_PER_GROUP = 8
N_EXPERTS = MOE_GROUPS * EXPERTS_PER_GROUP
TOP_K = 2
CONV_HALO = SUBLANES

TM_PROJ = 512
TM_OUT = 512
TD_DISPATCH = TM_OUT
MB_EXPERT = 512
EXPERT_BLOCKS_PER_STEP = 4
ZERO_PIECE = MB_EXPERT // 2
TF_COMBINE = TM_OUT
SEG_ALIGN = BF16_TILE_ROWS
COMMON_STAGE_ROWS = TOP_K * TM_OUT + N_EXPERTS * 10
RET_UNROLL = 8
SSD_PREP_UNROLL = 8
SSD_OUT_UNROLL = 2
NEG_BIG = -1e30


def _silu(v):
    return v * jax.nn.sigmoid(v)


def _dot(a, b):
    return jnp.dot(a, b, preferred_element_type=F32)


def _dot_tn(a, b):
    return lax.dot_general(a, b, (((0,), (0,)), ((), ())), preferred_element_type=F32)


def _dot_nt(a, b):
    return lax.dot_general(a, b, (((1,), (1,)), ((), ())), preferred_element_type=F32)


def _mod_kernel(c_ref, w_ref, b_ref, o_ref):
    a = _silu(c_ref[...])
    w = w_ref[...]
    a_hi = a.astype(BF16)
    a_lo = (a - a_hi.astype(F32)).astype(BF16)
    w_hi = w.astype(BF16)
    w_lo = (w - w_hi.astype(F32)).astype(BF16)
    o_ref[...] = _dot(a_hi, w_hi) + _dot(a_lo, w_hi) + _dot(a_hi, w_lo) + b_ref[...]


def _modulation(c_all, w_mod, b_mod):
    rows, d = c_all.shape
    n = w_mod.shape[1]
    return pl.pallas_call(
        _mod_kernel,
        grid=(n // d,),
        in_specs=[
            pl.BlockSpec((rows, d), lambda j: (0, 0)),
            pl.BlockSpec((d, d), lambda j: (0, j)),
            pl.BlockSpec((1, d), lambda j: (0, j)),
        ],
        out_specs=pl.BlockSpec((rows, d), lambda j: (0, j)),
        out_shape=jax.ShapeDtypeStruct((rows, n), F32),
        name="modulation",
    )(c_all, w_mod, b_mod.reshape(1, n))


def _norm_mod(x, nw, sc, sh):
    ms = jnp.mean(x * x, axis=-1, keepdims=True)
    return (x * lax.rsqrt(ms + EPS)) * (nw * (1.0 + sc)) + sh


def _rope(t, cos, sin_signed, first_half):
    width = t.shape[-1]
    quarter = RET_DK // 4
    swapped = jnp.where(first_half, pltpu.roll(t, width - quarter, 1), pltpu.roll(t, quarter, 1))
    return t * cos + swapped * sin_signed


def _inproj_kernel(x_ref, sh_ref, sc_ref, nw_ref, wqk_ref, wvgz_ref, wxbc_ref, wdt_ref, cos_ref, sin_ref,
                   q_ref, k_ref, v_ref, g_ref, z_ref, xbc_ref, dt_ref):
    hb = _norm_mod(x_ref[0], nw_ref[...], sc_ref[0], sh_ref[0]).astype(BF16)
    rw = q_ref.shape[-1]
    qk = _dot(hb, wqk_ref[...])
    cos = cos_ref[...]
    sin = sin_ref[...]
    lane = lax.broadcasted_iota(jnp.int32, cos.shape, 1)
    first_half = (lane % (RET_DK // 2)) < (RET_DK // 4)
    q_ref[0] = _rope(qk[:, :rw], cos, sin, first_half).astype(BF16)
    k_ref[0] = (_rope(qk[:, rw:], cos, sin, first_half) * (RET_DK ** -0.5)).astype(BF16)
    vgz = _dot(hb, wvgz_ref[...])
    v_ref[0] = vgz[:, :rw].astype(BF16)
    g_ref[0] = vgz[:, rw:2 * rw].astype(BF16)
    z_ref[0] = vgz[:, 2 * rw:].astype(BF16)
    xbc_ref[0] = _dot(hb, wxbc_ref[...]).astype(BF16)
    dt_ref[0] = _dot(hb, wdt_ref[...])


def _inproj_ctx_kernel(x_ref, sh_ref, sc_ref, nw_ref, wk_ref, wv_ref, wxbc_ref, wdt_ref,
                       k_ref, v_ref, xbc_ref, dt_ref):
    hb = _norm_mod(x_ref[0], nw_ref[...], sc_ref[0], sh_ref[0]).astype(BF16)
    k_ref[0] = (_dot(hb, wk_ref[...]) * (RET_DK ** -0.5)).astype(BF16)
    v_ref[0] = _dot(hb, wv_ref[...]).astype(BF16)
    xbc_ref[0] = _dot(hb, wxbc_ref[...]).astype(BF16)
    dt_ref[0] = _dot(hb, wdt_ref[...])


def _const_spec(shape):
    nd = len(shape)
    return pl.BlockSpec(shape, lambda *_: (0,) * nd)


def _inproj(x, mod3, nw, wqk, wvgz, wxbc, wdt, cos_t, sin_t):
    b, L, d = x.shape
    tm = min(TM_PROJ, L)
    rw = wqk.shape[1] // 2
    tok = lambda w: pl.BlockSpec((1, tm, w), lambda i, j: (i, j, 0))
    out_bf = lambda w: jax.ShapeDtypeStruct((b, L, w), BF16)
    return pl.pallas_call(
        _inproj_kernel,
        grid=(b, L // tm),
        in_specs=[
            tok(d),
            pl.BlockSpec((1, 1, d), lambda i, j: (i, 0, 0)),
            pl.BlockSpec((1, 1, d), lambda i, j: (i, 0, 1)),
            _const_spec((1, d)),
            _const_spec(wqk.shape), _const_spec(wvgz.shape), _const_spec(wxbc.shape), _const_spec(wdt.shape),
            pl.BlockSpec((tm, rw), lambda i, j: (j, 0)),
            pl.BlockSpec((tm, rw), lambda i, j: (j, 0)),
        ],
        out_specs=[tok(rw), tok(rw), tok(rw), tok(rw), tok(rw), tok(wxbc.shape[1]), tok(LANES)],
        out_shape=[out_bf(rw), out_bf(rw), out_bf(rw), out_bf(rw), out_bf(rw), out_bf(wxbc.shape[1]),
                   jax.ShapeDtypeStruct((b, L, LANES), F32)],
        compiler_params=pltpu.CompilerParams(vmem_limit_bytes=VMEM_LIMIT),
        name="inproj",
    )(x, mod3, mod3, nw, wqk, wvgz, wxbc, wdt, cos_t, sin_t)


def _inproj_ctx(ctx, mod3, ctx_row, nw, wk, wv, wxbc, wdt):
    b, L, d = ctx.shape
    tm = min(TM_PROJ, L)
    rw = wk.shape[1]
    tok = lambda w: pl.BlockSpec((1, tm, w), lambda i, j: (i, j, 0))
    out_bf = lambda w: jax.ShapeDtypeStruct((b, L, w), BF16)
    return pl.pallas_call(
        _inproj_ctx_kernel,
        grid=(b, L // tm),
        in_specs=[
            tok(d),
            pl.BlockSpec((1, 1, d), lambda i, j: (ctx_row, 0, 0)),
            pl.BlockSpec((1, 1, d), lambda i, j: (ctx_row, 0, 1)),
            _const_spec((1, d)),
            _const_spec(wk.shape), _const_spec(wv.shape), _const_spec(wxbc.shape), _const_spec(wdt.shape),
        ],
        out_specs=[tok(rw), tok(rw), tok(wxbc.shape[1]), tok(LANES)],
        out_shape=[out_bf(rw), out_bf(rw), out_bf(wxbc.shape[1]), jax.ShapeDtypeStruct((b, L, LANES), F32)],
        compiler_params=pltpu.CompilerParams(vmem_limit_bytes=VMEM_LIMIT),
        name="inproj_ctx",
    )(ctx, mod3, mod3, nw, wk, wv, wxbc, wdt)


def _ssd_kernel(xbc_ref, z_ref, dt_ref, xbcc_ref, dtc_ref, cw_ref, cb_ref, dtb_ref, alog_ref, dsk_ref, nw_ref,
                y_ref,
                xpad, xpadc, u, uc, dtv, dav, dtcv, dacv, sf_scr, kb_scr, yin_scr, dec_scr):
    L = xbc_ref.shape[1]
    Lc = xbcc_ref.shape[1]
    nch = L // CHUNK
    nchc = Lc // CHUNK
    win = CHUNK + 2 * CONV_HALO
    nconv = xbc_ref.shape[2]
    nh = SSD_HEADS

    def conv_pass(src_ref, pad_ref, dst_ref, n_chunks, length):
        zeros = jnp.zeros((CONV_HALO, nconv), F32)
        pad_ref[0:CONV_HALO, :] = zeros
        pad_ref[CONV_HALO + length:2 * CONV_HALO + length, :] = zeros
        pad_ref[CONV_HALO:CONV_HALO + length, :] = src_ref[0].astype(F32)

        def chunk(c, carry):
            base = pl.multiple_of(c * CHUNK, CHUNK)
            for cb_i in range(nconv // LANES):
                cols = slice(cb_i * LANES, (cb_i + 1) * LANES)
                w = pad_ref[pl.ds(base, win), cols]
                acc = cb_ref[:, cols] + w[CONV_HALO:CONV_HALO + CHUNK] * cw_ref[SSD_CONV // 2:SSD_CONV // 2 + 1, cols]
                for j in range(SSD_CONV):
                    if j == SSD_CONV // 2:
                        continue
                    shifted = pltpu.roll(w, (SSD_CONV // 2 - j) % win, 0)
                    acc = acc + shifted[CONV_HALO:CONV_HALO + CHUNK] * cw_ref[j:j + 1, cols]
                dst_ref[pl.ds(base, CHUNK), cols] = _silu(acc).astype(BF16)
            return carry

        lax.fori_loop(0, n_chunks, chunk, 0)

    conv_pass(xbcc_ref, xpadc, uc, nchc, Lc)
    conv_pass(xbc_ref, xpad, u, nch, L)

    a_neg = -jnp.exp(alog_ref[...])
    dtv[...] = jax.nn.softplus(dt_ref[0] + dtb_ref[...])
    dav[...] = dtv[...] * a_neg
    dtcv[...] = jax.nn.softplus(dtc_ref[0] + dtb_ref[...])
    dacv[...] = dtcv[...] * a_neg

    row_i = lax.broadcasted_iota(jnp.int32, (CHUNK, CHUNK), 0)
    col_i = lax.broadcasted_iota(jnp.int32, (CHUNK, CHUNK), 1)
    causal = col_i <= row_i
    lo_half = col_i < SSD_HEADDIM
    fwd_lane = col_i < nh
    head_of = lax.broadcasted_iota(jnp.int32, (CHUNK, SSD_WIDTH), 1) // SSD_HEADDIM
    src_col = lax.broadcasted_iota(jnp.int32, (CHUNK, SSD_WIDTH), 0)
    exp_f = (head_of == src_col).astype(BF16)
    exp_b = (head_of == src_col - nh).astype(BF16)
    exp_fb = jnp.concatenate([exp_f, exp_b], axis=1)

    def split3(v):
        hi = v.astype(BF16)
        r1 = v - hi.astype(F32)
        mid = r1.astype(BF16)
        return hi, mid, (r1 - mid.astype(F32)).astype(BF16)

    def times_onehot(v, m, passes=3):
        parts = split3(v)[:passes]
        acc = _dot(parts[0], m)
        for part in parts[1:]:
            acc = acc + _dot(part, m)
        return acc

    def colb(mat, r):
        return jnp.broadcast_to(mat[:, r:r + 1], (CHUNK, CHUNK))

    def pair_sel(a, b_):
        return jnp.where(lo_half, a, b_)

    gw = 2 * LANES

    def chunk_terms(u_ref, dt_s, da_s, base):
        dt = dt_s[pl.ds(base, CHUNK), :]
        da = da_s[pl.ds(base, CHUNK), :]
        acol = da
        for step in (1, 2, 4, 8, 16, 32, 64):
            acol = acol + jnp.where(row_i >= step, pltpu.roll(acol, step, 0), 0.0)
        ecol = acol - da
        last = acol[CHUNK - 1:CHUNK, :]
        wgt = jnp.where(fwd_lane, jnp.exp(last - acol), jnp.exp(ecol)) * dt
        scale = jnp.where(fwd_lane, jnp.exp(acol), jnp.exp(last - ecol))
        wide = times_onehot(jnp.concatenate([wgt, scale], axis=0), exp_fb, passes=1)
        dec = times_onehot(jnp.broadcast_to(jnp.exp(last), (SUBLANES, LANES)), exp_fb)[0:1]
        xs = u_ref[pl.ds(base, CHUNK), 0:SSD_WIDTH].astype(F32)
        kmats = []
        for g in range(SSD_GROUPS):
            xw = jnp.concatenate([xs[:, g * gw:(g + 1) * gw] * wide[:CHUNK, g * gw:(g + 1) * gw],
                                  xs[:, g * gw:(g + 1) * gw] * wide[:CHUNK, SSD_WIDTH + g * gw:SSD_WIDTH + (g + 1) * gw]],
                                 axis=1).astype(BF16)
            bm = u_ref[pl.ds(base, CHUNK), SSD_WIDTH + g * SSD_STATE:SSD_WIDTH + (g + 1) * SSD_STATE]
            kmats.append(_dot_tn(bm, xw))
        return dt, acol, ecol, wide[CHUNK:], dec, kmats

    def advance(s, dec, kmats, backward):
        off = SSD_WIDTH if backward else 0
        koff = gw if backward else 0
        return [dec[:, off + g * gw:off + (g + 1) * gw] * s[g] + kmats[g][:, koff:koff + gw]
                for g in range(SSD_GROUPS)]

    ctx_terms = [chunk_terms(uc, dtcv, dacv, c * CHUNK) for c in range(nchc)]
    s_f0 = [jnp.zeros((SSD_STATE, gw), F32) for _ in range(SSD_GROUPS)]
    for c in range(nchc):
        s_f0 = advance(s_f0, ctx_terms[c][4], ctx_terms[c][5], False)
    s_b0 = [jnp.zeros((SSD_STATE, gw), F32) for _ in range(SSD_GROUPS)]
    for c in reversed(range(nchc)):
        s_b0 = advance(s_b0, ctx_terms[c][4], ctx_terms[c][5], True)

    def prep(c, carry):
        base = pl.multiple_of(c * CHUNK, CHUNK)
        dt, acol, ecol, scale, dec, kmats = chunk_terms(u, dtv, dav, base)
        arow = acol.T
        erow = ecol.T
        dt_t = dt.T
        ys = []
        for g in range(SSD_GROUPS):
            bm = u[pl.ds(base, CHUNK), SSD_WIDTH + g * SSD_STATE:SSD_WIDTH + (g + 1) * SSD_STATE]
            cm = u[pl.ds(base, CHUNK), SSD_WIDTH + (SSD_GROUPS + g) * SSD_STATE:SSD_WIDTH + (SSD_GROUPS + g + 1) * SSD_STATE]
            cbm = _dot_nt(cm, bm)
            for pp in range(SSD_PAIRS // SSD_GROUPS):
                p = g * (SSD_PAIRS // SSD_GROUPS) + pp
                xs_b = u[pl.ds(base, CHUNK), p * LANES:(p + 1) * LANES]
                y_h = []
                for hh in range(2):
                    r = 2 * p + hh
                    arg = jnp.where(causal, colb(acol, r) - arow[r:r + 1, :],
                                    erow[nh + r:nh + r + 1, :] - colb(ecol, nh + r))
                    coef = jnp.where(causal, dt_t[r:r + 1, :], dt_t[nh + r:nh + r + 1, :])
                    gm = (cbm * (jnp.exp(arg) * coef)).astype(BF16)
                    y_h.append(_dot(gm, xs_b))
                ys.append(pair_sel(y_h[0], y_h[1]) + dsk_ref[:, p * LANES:(p + 1) * LANES] * xs_b.astype(F32))
        yin_scr[pl.ds(base, CHUNK), :] = jnp.concatenate(ys, axis=1)
        xpad[pl.ds(base, CHUNK), :] = scale
        dec_scr[pl.ds(pl.multiple_of(c * SUBLANES, SUBLANES), SUBLANES), :] = jnp.broadcast_to(dec, (SUBLANES, 2 * SSD_WIDTH))
        for g in range(SSD_GROUPS):
            sf_scr[c, g] = kmats[g][:, :gw]
            kb_scr[c, g] = kmats[g][:, gw:]
        return carry

    lax.fori_loop(0, nch, prep, 0, unroll=SSD_PREP_UNROLL)

    def chunk_dec(c):
        return dec_scr[pl.ds(pl.multiple_of(c * SUBLANES, SUBLANES), 1), :]

    def fwd(c, s_old):
        dec = chunk_dec(c)
        new = []
        for g in range(SSD_GROUPS):
            new.append(dec[:, g * gw:(g + 1) * gw] * s_old[g] + sf_scr[c, g])
            sf_scr[c, g] = s_old[g]
        return tuple(new)

    lax.fori_loop(0, nch, fwd, tuple(s_f0))

    def bwd(i, s_b):
        c = nch - 1 - i
        base = pl.multiple_of(c * CHUNK, CHUNK)
        scale = xpad[pl.ds(base, CHUNK), :]
        cross = []
        for g in range(SSD_GROUPS):
            cm = u[pl.ds(base, CHUNK), SSD_WIDTH + (SSD_GROUPS + g) * SSD_STATE:SSD_WIDTH + (SSD_GROUPS + g + 1) * SSD_STATE]
            cs_f = _dot(cm, sf_scr[c, g].astype(BF16))
            cs_b = _dot(cm, s_b[g].astype(BF16))
            cross.append(cs_f * scale[:, g * gw:(g + 1) * gw]
                         + cs_b * scale[:, SSD_WIDTH + g * gw:SSD_WIDTH + (g + 1) * gw])
        y = yin_scr[pl.ds(base, CHUNK), :] + jnp.concatenate(cross, axis=1)
        y = y * _silu(z_ref[0, pl.ds(base, CHUNK), :].astype(F32))
        ms = jnp.mean(y * y, axis=-1, keepdims=True)
        y_ref[0, pl.ds(base, CHUNK), :] = ((y * lax.rsqrt(ms + EPS)) * nw_ref[...]).astype(BF16)
        dec = chunk_dec(c)
        return tuple(dec[:, SSD_WIDTH + g * gw:SSD_WIDTH + (g + 1) * gw] * s_b[g] + kb_scr[c, g]
                     for g in range(SSD_GROUPS))

    lax.fori_loop(0, nch, bwd, tuple(s_b0), unroll=SSD_OUT_UNROLL)


def _ssd(xbc, z, dt, xbcc, dtc, conv_w8, conv_b, dt_bias, a_log, d_skip, norm_w):
    b, L, nconv = xbc.shape
    Lc = xbcc.shape[1]
    nch = L // CHUNK
    per_b = lambda n, w: pl.BlockSpec((1, n, w), lambda i: (i, 0, 0))
    return pl.pallas_call(
        _ssd_kernel,
        grid=(b,),
        in_specs=[
            per_b(L, nconv), per_b(L, SSD_WIDTH), per_b(L, LANES), per_b(Lc, nconv), per_b(Lc, LANES),
            _const_spec(conv_w8.shape), _const_spec(conv_b.shape), _const_spec(dt_bias.shape),
            _const_spec(a_log.shape), _const_spec(d_skip.shape), _const_spec(norm_w.shape),
        ],
        out_specs=per_b(L, SSD_WIDTH),
        out_shape=jax.ShapeDtypeStruct((b, L, SSD_WIDTH), BF16),
        scratch_shapes=[
            pltpu.VMEM((L + 2 * CONV_HALO, nconv), F32),
            pltpu.VMEM((Lc + 2 * CONV_HALO, nconv), F32),
            pltpu.VMEM((L, nconv), BF16),
            pltpu.VMEM((Lc, nconv), BF16),
            pltpu.VMEM((L, LANES), F32), pltpu.VMEM((L, LANES), F32),
            pltpu.VMEM((Lc, LANES), F32), pltpu.VMEM((Lc, LANES), F32),
            pltpu.VMEM((nch, SSD_GROUPS, SSD_STATE, 2 * LANES), F32),
            pltpu.VMEM((nch, SSD_GROUPS, SSD_STATE, 2 * LANES), F32),
            pltpu.VMEM((L, SSD_WIDTH), F32),
            pltpu.VMEM((nch * SUBLANES, 2 * SSD_WIDTH), F32),
        ],
        compiler_params=pltpu.CompilerParams(vmem_limit_bytes=VMEM_LIMIT_SSD),
        name="ssd",
    )(xbc, z, dt, xbcc, dtc, conv_w8, conv_b, dt_bias, a_log, d_skip, norm_w)


def _ret_kernel(q_ref, k_ref, v_ref, g_ref, kc_ref, vc_ref, df_ref, db_ref, gn_ref, y_ref, sf_scr):
    L = q_ref.shape[1]
    Lc = kc_ref.shape[1]
    nch = L // CHUNK
    dk = RET_DK
    row_i = lax.broadcasted_iota(jnp.int32, (CHUNK, dk), 0).astype(F32)
    col_i = lax.broadcasted_iota(jnp.int32, (CHUNK, dk), 1).astype(F32)
    rel = row_i - col_i
    crow = lax.broadcasted_iota(jnp.int32, (Lc, dk), 0).astype(F32)

    heads = []
    s_f0 = []
    s_b0 = []
    for h in range(RET_HEADS):
        cols = slice(h * dk, (h + 1) * dk)
        lg_f = -jnp.exp(df_ref[:, cols])
        lg_b = -jnp.exp(db_ref[:, cols])
        heads.append(dict(
            cols=cols,
            dmat=jnp.where(rel >= 0, jnp.exp(jnp.maximum(rel, 0.0) * lg_f), jnp.exp(jnp.maximum(-rel, 0.0) * lg_b)),
            dq_f=jnp.exp((row_i + 1.0) * lg_f),
            dq_b=jnp.exp((CHUNK - row_i) * lg_b),
            dk_f=jnp.exp((CHUNK - 1.0 - row_i) * lg_f),
            dk_b=jnp.exp(row_i * lg_b),
            dc_f=jnp.exp(CHUNK * lg_f),
            dc_b=jnp.exp(CHUNK * lg_b),
        ))
        kc = kc_ref[0, :, cols].astype(F32)
        vc = vc_ref[0, :, cols]
        s_f0.append(_dot_tn((kc * jnp.exp((Lc - 1.0 - crow) * lg_f)).astype(BF16), vc))
        s_b0.append(_dot_tn((kc * jnp.exp(crow * lg_b)).astype(BF16), vc))

    def fwd(c, s_f):
        base = pl.multiple_of(c * CHUNK, CHUNK)
        new = []
        for h, hd in enumerate(heads):
            sf_scr[c, h] = s_f[h]
            kk = k_ref[0, pl.ds(base, CHUNK), hd["cols"]].astype(F32)
            vv = v_ref[0, pl.ds(base, CHUNK), hd["cols"]]
            new.append(hd["dc_f"] * s_f[h] + _dot_tn((kk * hd["dk_f"]).astype(BF16), vv))
        return tuple(new)

    lax.fori_loop(0, nch, fwd, tuple(s_f0), unroll=RET_UNROLL)

    def bwd(i, s_bs):
        c = nch - 1 - i
        base = pl.multiple_of(c * CHUNK, CHUNK)
        new = []
        for h, hd in enumerate(heads):
            qq = q_ref[0, pl.ds(base, CHUNK), hd["cols"]]
            kk = k_ref[0, pl.ds(base, CHUNK), hd["cols"]]
            vv = v_ref[0, pl.ds(base, CHUNK), hd["cols"]]
            s_b = s_bs[h]
            scores = (_dot_nt(qq, kk) * hd["dmat"]).astype(BF16)
            y = (_dot(scores, vv)
                 + _dot(qq, sf_scr[c, h].astype(BF16)) * hd["dq_f"]
                 + _dot(qq, s_b.astype(BF16)) * hd["dq_b"])
            mu = jnp.mean(y, axis=-1, keepdims=True)
            yc = y - mu
            var = jnp.mean(yc * yc, axis=-1, keepdims=True)
            yn = (yc * lax.rsqrt(var + EPS)) * gn_ref[:, hd["cols"]]
            gate = _silu(g_ref[0, pl.ds(base, CHUNK), hd["cols"]].astype(F32))
            y_ref[0, pl.ds(base, CHUNK), hd["cols"]] = (yn * gate).astype(BF16)
            new.append(hd["dc_b"] * s_b + _dot_tn((kk.astype(F32) * hd["dk_b"]).astype(BF16), vv))
        return tuple(new)

    lax.fori_loop(0, nch, bwd, tuple(s_b0), unroll=RET_UNROLL)


def _retention(q, k, v, g, kc, vc, decay_f, decay_b, gn_w):
    b, L, w = q.shape
    Lc = kc.shape[1]
    nch = L // CHUNK
    per_b = lambda n: pl.BlockSpec((1, n, w), lambda i: (i, 0, 0))
    return pl.pallas_call(
        _ret_kernel,
        grid=(b,),
        in_specs=[per_b(L), per_b(L), per_b(L), per_b(L), per_b(Lc), per_b(Lc),
                  _const_spec((1, w)), _const_spec((1, w)), _const_spec((1, w))],
        out_specs=per_b(L),
        out_shape=jax.ShapeDtypeStruct((b, L, w), BF16),
        scratch_shapes=[
            pltpu.VMEM((nch, RET_HEADS, RET_DK, RET_DK), F32),
        ],
        compiler_params=pltpu.CompilerParams(vmem_limit_bytes=VMEM_LIMIT),
        name="retention",
    )(q, k, v, g, kc, vc, decay_f, decay_b, gn_w)


def _outproj_router_kernel(yr_ref, ys_ref, x_ref, g1_ref, sh2_ref, sc2_ref, npost_ref, npre_ref,
                           wor_ref, wos_ref, wr_ref, br_ref, tri_ref,
                           x1_ref, h2_ref, route_ref, slots_ref, seg_ref,
                           wcat):
    i = pl.program_id(0)

    @pl.when(i == 0)
    def _():
        wr = wr_ref[...]
        hi = wr.astype(BF16)
        wcat[:, :LANES] = hi
        wcat[:, LANES:] = (wr - hi.astype(F32)).astype(BF16)

    _route_tile(yr_ref, ys_ref, x_ref, g1_ref, sh2_ref, sc2_ref, npost_ref, npre_ref, wor_ref, wos_ref, br_ref,
                tri_ref, x1_ref, h2_ref, route_ref, slots_ref, seg_ref, wcat)


def _route_tile(yr_ref, ys_ref, x_ref, g1_ref, sh2_ref, sc2_ref, npost_ref, npre_ref, wor_ref, wos_ref, br_ref,
                tri_ref, x1_ref, h2_ref, route_ref, slots_ref, seg_ref, wcat):
    tm = x_ref.shape[0]
    rows = slice(0, tm)
    y = _dot(yr_ref[rows, :], wor_ref[...]) + _dot(ys_ref[rows, :], wos_ref[...])
    ms = jnp.mean(y * y, axis=-1, keepdims=True)
    x1 = x_ref[rows, :] + (y * lax.rsqrt(ms + EPS)) * (g1_ref[0] * npost_ref[...])
    x1_ref[rows, :] = x1
    h2 = _norm_mod(x1, npre_ref[...], sc2_ref[0], sh2_ref[0])
    h2_ref[rows, :] = h2.astype(BF16)

    h_hi = h2.astype(BF16)
    h_lo = (h2 - h_hi.astype(F32)).astype(BF16)
    both = _dot(h_hi, wcat[...])
    lg = both[:, :LANES] + both[:, LANES:] + _dot(h_lo, wcat[:, :LANES]) + br_ref[...]

    lane = lax.broadcasted_iota(jnp.int32, (tm, LANES), 1)
    lane_f = lane.astype(F32)
    is_grp = (lane >= N_EXPERTS) & (lane < N_EXPERTS + MOE_GROUPS)
    gl = jnp.where(is_grp, lg, NEG_BIG)
    mg = jnp.max(gl, axis=-1, keepdims=True)
    grp_lane = jnp.min(jnp.where(gl == mg, lane_f, 1e9), axis=-1, keepdims=True)
    p_g = 1.0 / jnp.sum(jnp.where(is_grp, jnp.exp(gl - mg), 0.0), axis=-1, keepdims=True)
    first = (grp_lane - N_EXPERTS) * EXPERTS_PER_GROUP
    in_grp = (lane_f >= first) & (lane_f < first + EXPERTS_PER_GROUP)
    el = jnp.where(in_grp, lg, NEG_BIG)
    t1 = jnp.max(el, axis=-1, keepdims=True)
    i1 = jnp.min(jnp.where(el == t1, lane_f, 1e9), axis=-1, keepdims=True)
    el2 = jnp.where(lane_f == i1, NEG_BIG, el)
    t2 = jnp.max(el2, axis=-1, keepdims=True)
    i2 = jnp.min(jnp.where(el2 == t2, lane_f, 1e9), axis=-1, keepdims=True)
    s = jnp.exp(t2 - t1)
    w1 = p_g / (1.0 + s)
    w2 = p_g * s / (1.0 + s)

    oh1 = (lane_f == i1)
    oh2 = (lane_f == i2)
    ohf = jnp.where(oh1 | oh2, 1.0, 0.0)
    before = _dot(tri_ref[...], ohf.astype(BF16))
    cnt = jnp.sum(ohf, axis=0, keepdims=True)
    seg = jnp.floor((cnt + (SEG_ALIGN - 1.0)) * (1.0 / SEG_ALIGN)) * SEG_ALIGN
    e_row = lax.broadcasted_iota(jnp.int32, (LANES, LANES), 0)
    e_col = lax.broadcasted_iota(jnp.int32, (LANES, LANES), 1)
    earlier = (e_row < e_col).astype(BF16)
    seg_off = _dot(jnp.broadcast_to(seg, (SUBLANES, LANES)).astype(BF16), earlier)[0:1]
    where_to = before + seg_off
    lpos1 = jnp.sum(jnp.where(oh1, where_to, 0.0), axis=-1, keepdims=True)
    lpos2 = jnp.sum(jnp.where(oh2, where_to, 0.0), axis=-1, keepdims=True)

    cols = [w1, w2, lpos1, lpos2]
    for wk in (w1, w2):
        hi = wk.astype(BF16).astype(F32)
        cols += [hi, wk - hi]
    cols.append(jnp.ones_like(w1))
    packed = jnp.zeros((tm, LANES), F32)
    for k, col in enumerate(cols):
        packed = jnp.where(lane == k, col, packed)
    route_ref[rows, :] = packed

    row = lax.broadcasted_iota(jnp.int32, (tm, LANES), 0)
    on_diag = (row % LANES) == lane
    per = tm // LANES
    for qi, col in enumerate((lpos1, lpos2)):
        picked = jnp.where(on_diag, col, 0.0)
        dense = jnp.sum(picked.reshape(per, LANES, LANES), axis=1).astype(jnp.int32)
        slots_ref[0, qi * per:(qi + 1) * per, :] = dense
    tbl_row = lax.broadcasted_iota(jnp.int32, (SUBLANES, LANES), 0)
    seg_ref[0] = jnp.where(tbl_row == 0, seg, jnp.where(tbl_row == 1, seg_off, 0.0)).astype(jnp.int32)


def _outproj_router(yr, ys, x2, mod3, npost, npre, wo_r, wo_s, w_router, b_router, seq_len):
    T, d = x2.shape
    tm = TM_OUT
    per_seq = seq_len // tm
    rw = yr.shape[1]
    tri = (jnp.arange(tm)[:, None] > jnp.arange(tm)[None, :]).astype(BF16)
    tok = lambda w: pl.BlockSpec((tm, w), lambda i: (i, 0))
    modv = lambda k: pl.BlockSpec((1, 1, d), lambda i: (i // per_seq, 0, k))
    tile3 = lambda r: pl.BlockSpec((1, r, LANES), lambda i: (i, 0, 0))
    slot_rows = TOP_K * (tm // LANES)
    return pl.pallas_call(
        _outproj_router_kernel,
        grid=(T // tm,),
        in_specs=[
            tok(rw), tok(rw), tok(d), modv(2), modv(3), modv(4),
            _const_spec((1, d)), _const_spec((1, d)),
            _const_spec(wo_r.shape), _const_spec(wo_s.shape), _const_spec(w_router.shape), _const_spec((1, LANES)),
            _const_spec((tm, tm)),
        ],
        out_specs=[tok(d), tok(d), tok(LANES), tile3(slot_rows), tile3(SUBLANES)],
        out_shape=[jax.ShapeDtypeStruct((T, d), F32), jax.ShapeDtypeStruct((T, d), BF16),
                   jax.ShapeDtypeStruct((T, LANES), F32),
                   jax.ShapeDtypeStruct((T // tm, slot_rows, LANES), jnp.int32),
                   jax.ShapeDtypeStruct((T // tm, SUBLANES, LANES), jnp.int32)],
        scratch_shapes=[pltpu.VMEM((d, 2 * LANES), BF16)],
        compiler_params=pltpu.CompilerParams(dimension_semantics=("arbitrary",),
                                             vmem_limit_bytes=VMEM_LIMIT),
        name="outproj_router",
    )(yr, ys, x2, mod3, mod3, mod3, npost, npre, wo_r, wo_s, w_router, b_router, tri)


def _expert_kernel(be_ref, first_ref, slot_ref, next_ref, nused_ref, xs_ref, wg_hbm, wu_hbm, wd_hbm, y_ref,
                   wg_f, wu_f, wd_f, wg_b, wu_b, wd_b, sem):
    def fetch(e, s):
        return [pltpu.make_async_copy(src.at[e], dst.at[s], sem.at[s, k])
                for k, (src, dst) in enumerate(((wg_hbm, wg_f), (wu_hbm, wu_f), (wd_hbm, wd_f)))]

    @pl.when(pl.program_id(0) == 0)
    def _():
        for c in fetch(be_ref[0], slot_ref[0]):
            c.start()

    for sb in range(EXPERT_BLOCKS_PER_STEP):
        _expert_block(pl.program_id(0) * EXPERT_BLOCKS_PER_STEP + sb, slice(sb * MB_EXPERT, (sb + 1) * MB_EXPERT),
                      fetch, be_ref, first_ref, slot_ref, next_ref, nused_ref, xs_ref, y_ref,
                      wg_f, wu_f, wd_f, wg_b, wu_b, wd_b)


def _expert_block(i, rows, fetch, be_ref, first_ref, slot_ref, next_ref, nused_ref, xs_ref, y_ref,
                  wg_f, wu_f, wd_f, wg_b, wu_b, wd_b):
    @pl.when(first_ref[i] == 1)
    def _():
        s = slot_ref[i]

        @pl.when(next_ref[i] >= 0)
        def _():
            for c in fetch(next_ref[i], 1 - s):
                c.start()

        for c in fetch(be_ref[i], s):
            c.wait()
        wg_b[...] = wg_f[s].astype(BF16)
        wu_b[...] = wu_f[s].astype(BF16)
        wd_b[...] = wd_f[s].astype(BF16)

    @pl.when(i < nused_ref[0])
    def _():
        d = y_ref.shape[1]
        side = xs_ref[rows, d:d + LANES].astype(F32)
        second = side[:, 8:9] == 2.0
        unscale = jnp.where(second, 0.5, 1.0)
        weight = jnp.where(second, side[:, 6:7] + side[:, 7:8], side[:, 4:5] + side[:, 5:6]) * unscale
        xb = xs_ref[rows, 0:d] * unscale.astype(BF16)
        hid = (_silu(_dot(xb, wg_b[...])) * _dot(xb, wu_b[...])).astype(BF16)
        y_ref[rows, :] = (_dot(hid, wd_b[...]) * weight).astype(BF16)

    @pl.when(i >= nused_ref[0])
    def _():
        y_ref[rows, :] = jnp.zeros((rows.stop - rows.start, y_ref.shape[1]), y_ref.dtype)


def _expert_plan(padded, pad_end, n_blocks, mb):
    n_used = (pad_end[-1:] // mb).astype(jnp.int32)
    blk_start = jnp.arange(n_blocks, dtype=jnp.int32) * mb
    experts = jnp.arange(N_EXPERTS, dtype=jnp.int32)
    blk_expert = jnp.minimum(jnp.sum((pad_end[None, :] <= blk_start[:, None]).astype(jnp.int32), axis=1),
                             N_EXPERTS - 1)
    prev = jnp.concatenate([jnp.full((1,), -1, jnp.int32), blk_expert[:-1]])
    first = ((blk_start < pad_end[-1]) & (blk_expert != prev)).astype(jnp.int32)
    slot = (jnp.cumsum(first) - 1) % 2
    later = jnp.where((padded > 0)[None, :] & (experts[None, :] > experts[:, None]), experts[None, :], N_EXPERTS)
    next_nonempty = jnp.min(later, axis=1)
    next_nonempty = jnp.where(next_nonempty == N_EXPERTS, -1, next_nonempty)
    nxt = jnp.sum(jnp.where(blk_expert[:, None] == experts[None, :], next_nonempty[None, :], 0), axis=1)
    return [a.astype(jnp.int32) for a in (blk_expert, first, slot, nxt, n_used)]


def _experts(plan, xs, w_gate, w_up, w_down):
    cap, dp = xs.shape
    n_exp, d, de = w_gate.shape
    step_rows = MB_EXPERT * EXPERT_BLOCKS_PER_STEP
    assert dp == d + LANES and cap % step_rows == 0
    grid_spec = pltpu.PrefetchScalarGridSpec(
        num_scalar_prefetch=len(plan),
        grid=(cap // step_rows,),
        in_specs=[
            pl.BlockSpec((step_rows, dp), lambda i, be, fi, sl, nx, nu: (
                jnp.minimum(i, (nu[0] - 1) // EXPERT_BLOCKS_PER_STEP), 0)),
            pl.BlockSpec(memory_space=pl.ANY), pl.BlockSpec(memory_space=pl.ANY), pl.BlockSpec(memory_space=pl.ANY),
        ],
        out_specs=pl.BlockSpec((step_rows, d), lambda i, be, fi, sl, nx, nu: (i, 0)),
        scratch_shapes=[pltpu.VMEM((2, d, de), F32), pltpu.VMEM((2, d, de), F32), pltpu.VMEM((2, de, d), F32),
                        pltpu.VMEM((d, de), BF16), pltpu.VMEM((d, de), BF16), pltpu.VMEM((de, d), BF16),
                        pltpu.SemaphoreType.DMA((2, 3))],
    )
    return pl.pallas_call(
        _expert_kernel,
        grid_spec=grid_spec,
        out_shape=jax.ShapeDtypeStruct((cap, d), BF16),
        compiler_params=pltpu.CompilerParams(dimension_semantics=("arbitrary",),
                                             vmem_limit_bytes=VMEM_LIMIT),
        name="experts",
    )(*plan, xs, w_gate, w_up, w_down)


def _segment_pieces(tile, segrow_ref, seglen_ref, segoff_ref, act):
    def per_expert(e, carry):
        idx = tile * N_EXPERTS + e
        g0 = segrow_ref[idx]
        l0 = segoff_ref[idx]

        def piece(j, c2):
            act(pl.multiple_of(l0 + j * SEG_ALIGN, SEG_ALIGN), pl.multiple_of(g0 + j * SEG_ALIGN, SEG_ALIGN))
            return c2

        lax.fori_loop(0, seglen_ref[idx], piece, 0)
        return carry

    lax.fori_loop(0, N_EXPERTS, per_expert, 0)


def _wait_rows(total, row_copy):
    size = SEG_ALIGN
    while size <= _stage_rows(max(TD_DISPATCH, TF_COMBINE)):
        @pl.when((total & size) != 0)
        def _(size=size):
            row_copy(size).wait()
        size *= 2


def _for_tile_rows(tile_rows, max_rows, body):
    @pl.when(tile_rows <= COMMON_STAGE_ROWS)
    def _():
        body(COMMON_STAGE_ROWS)

    @pl.when(tile_rows > COMMON_STAGE_ROWS)
    def _():
        body(max_rows)


def _stage_rows(tile_tokens):
    return TOP_K * tile_tokens + N_EXPERTS * SEG_ALIGN


def _dispatch_seg_kernel(pad_end_ref, zero_from_ref, segrow_ref, seglen_ref, segoff_ref, tilerows_ref,
                         lpos_ref, route_ref, h_ref, xs_hbm, zbuf, stage, sem, zsem):
    i = pl.program_id(0)
    nt = pl.num_programs(0)
    td, d = h_ref.shape
    sr = stage.shape[1]
    per = td // LANES

    def zero_fills(act):
        def fill(row, n):
            act(pltpu.make_async_copy(zbuf.at[pl.ds(0, n)], xs_hbm.at[pl.ds(pl.multiple_of(row, SEG_ALIGN), n)], zsem))

        def region(e, carry):
            row = zero_from_ref[e]
            left = pad_end_ref[e] - row
            size = ZERO_PIECE
            while size >= SEG_ALIGN:
                take = (left & size) != 0

                @pl.when(take)
                def _(row=row, size=size):
                    fill(row, size)

                row = row + jnp.where(take, size, 0)
                size //= 2
            return carry

        def tail(p, carry):
            @pl.when(p * ZERO_PIECE >= pad_end_ref[N_EXPERTS - 1])
            def _():
                fill(p * ZERO_PIECE, ZERO_PIECE)
            return carry

        lax.fori_loop(0, N_EXPERTS, region, 0)
        lax.fori_loop(0, xs_hbm.shape[0] // ZERO_PIECE, tail, 0)

    @pl.when(i == 0)
    def _():
        zbuf[...] = jnp.zeros_like(zbuf)
        zero_fills(lambda c: c.start())

    def shipped(tile):
        _wait_rows(tilerows_ref[tile], lambda n: pltpu.make_async_copy(
            stage.at[tile % 2, pl.ds(0, n)], xs_hbm.at[pl.ds(0, n)], sem.at[tile % 2]))

    @pl.when(i >= 2)
    def _():
        shipped(i - 2)

    slot = i % 2

    def sort_rows(n_rows):
        srow = lax.broadcasted_iota(jnp.int32, (n_rows, LANES), 0)
        place = jnp.concatenate(
            [jnp.where(srow == lpos_ref[0, cb:cb + 1, :], 1.0,
                       jnp.where(srow == lpos_ref[0, per + cb:per + cb + 1, :], 2.0, 0.0)) for cb in range(per)],
            axis=1).astype(BF16)
        stage[slot, 0:n_rows, :] = _dot(place, jnp.concatenate([h_ref[...], route_ref[...].astype(BF16)],
                                                               axis=1)).astype(BF16)

    _for_tile_rows(tilerows_ref[i], sr, sort_rows)
    _segment_pieces(i, segrow_ref, seglen_ref, segoff_ref, lambda lr, gr: pltpu.make_async_copy(
        stage.at[slot, pl.ds(lr, SEG_ALIGN)], xs_hbm.at[pl.ds(gr, SEG_ALIGN)], sem.at[slot]).start())

    @pl.when(i == nt - 1)
    def _():
        @pl.when(i >= 1)
        def _():
            shipped(i - 1)
        shipped(i)
        zero_fills(lambda c: c.wait())


def _dispatch_seg(plan, lpos, route, h2, cap):
    T, d = h2.shape
    td = TD_DISPATCH
    nt = T // td
    width = d + LANES
    sr = _stage_rows(td)
    n_pre = len(plan)
    grid_spec = pltpu.PrefetchScalarGridSpec(
        num_scalar_prefetch=n_pre,
        grid=(nt,),
        in_specs=[
            pl.BlockSpec((1, lpos.shape[1], LANES), lambda i, *_: (i, 0, 0)),
            pl.BlockSpec((td, LANES), lambda i, *_: (i, 0)),
            pl.BlockSpec((td, d), lambda i, *_: (i, 0)),
        ],
        out_specs=pl.BlockSpec(memory_space=pl.ANY),
        scratch_shapes=[pltpu.VMEM((ZERO_PIECE, width), BF16), pltpu.VMEM((2, sr, width), BF16),
                        pltpu.SemaphoreType.DMA((2,)), pltpu.SemaphoreType.DMA(())],
    )
    return pl.pallas_call(
        _dispatch_seg_kernel,
        grid_spec=grid_spec,
        out_shape=jax.ShapeDtypeStruct((cap, width), BF16),
        compiler_params=pltpu.CompilerParams(dimension_semantics=("arbitrary",), vmem_limit_bytes=VMEM_LIMIT),
        name="dispatch",
    )(*plan, lpos, route, h2)


def _combine_seg_kernel(segrow_ref, seglen_ref, segoff_ref, tilerows_ref, route_ref, x1_ref, g2_ref, nw_ref,
                        yb_hbm, o_ref, stage, sem):
    i = pl.program_id(0)
    nt = pl.num_programs(0)
    tf = x1_ref.shape[0]
    sr = stage.shape[1]

    def fetch(tile, slot):
        _segment_pieces(tile, segrow_ref, seglen_ref, segoff_ref, lambda lr, gr: pltpu.make_async_copy(
            yb_hbm.at[pl.ds(gr, SEG_ALIGN)], stage.at[slot, pl.ds(lr, SEG_ALIGN)], sem.at[slot]).start())

    @pl.when(i == 0)
    def _():
        stage[...] = jnp.zeros_like(stage)
        fetch(i, 0)

    for slot in range(2):
        @pl.when((i + 1 < nt) & (i % 2 != slot))
        def _(slot=slot):
            fetch(i + 1, slot)

    _wait_rows(tilerows_ref[i], lambda n: pltpu.make_async_copy(
        yb_hbm.at[pl.ds(0, n)], stage.at[i % 2, pl.ds(0, n)], sem.at[i % 2]))

    local_row = route_ref[:, 2:2 + TOP_K].astype(jnp.int32)

    def unsort_rows(n_rows):
        scol = lax.broadcasted_iota(jnp.int32, (tf, n_rows), 1)
        pick = jnp.where((scol == local_row[:, 0:1]) | (scol == local_row[:, 1:2]), 1.0, 0.0).astype(BF16)
        out = _dot(pick, stage[i % 2, 0:n_rows, :])
        ms = jnp.mean(out * out, axis=-1, keepdims=True)
        o_ref[...] = x1_ref[...] + g2_ref[0] * ((out * lax.rsqrt(ms + EPS)) * nw_ref[...])

    _for_tile_rows(tilerows_ref[i], sr, unsort_rows)


def _combine_seg(seg_plan, route, x1, mod3, nw, yb, seq_len):
    T, d = x1.shape
    tf = TF_COMBINE
    nt = T // tf
    per_seq = seq_len // tf
    grid_spec = pltpu.PrefetchScalarGridSpec(
        num_scalar_prefetch=len(seg_plan),
        grid=(nt,),
        in_specs=[
            pl.BlockSpec((tf, LANES), lambda i, *_: (i, 0)),
            pl.BlockSpec((tf, d), lambda i, *_: (i, 0)),
            pl.BlockSpec((1, 1, d), lambda i, *_: (i // per_seq, 0, 5)),
            pl.BlockSpec((1, d), lambda i, *_: (0, 0)),
            pl.BlockSpec(memory_space=pl.ANY),
        ],
        out_specs=pl.BlockSpec((tf, d), lambda i, *_: (i, 0)),
        scratch_shapes=[pltpu.VMEM((2, _stage_rows(tf), d), yb.dtype), pltpu.SemaphoreType.DMA((2,))],
    )
    return pl.pallas_call(
        _combine_seg_kernel,
        grid_spec=grid_spec,
        out_shape=jax.ShapeDtypeStruct((T, d), F32),
        compiler_params=pltpu.CompilerParams(dimension_semantics=("arbitrary",), vmem_limit_bytes=VMEM_LIMIT),
        name="combine",
    )(*seg_plan, route, x1, mod3, nw, yb)


def _rope_tables(L, n_heads):
    quarter = RET_DK // 4
    freqs = ROPE_BASE ** (-jnp.arange(quarter, dtype=F32) / quarter)
    t = jnp.arange(L)
    ang_r = (t // GRID_W).astype(F32)[:, None] * freqs
    ang_c = (t % GRID_W).astype(F32)[:, None] * freqs
    cos = jnp.concatenate([jnp.cos(ang_r)] * 2 + [jnp.cos(ang_c)] * 2, axis=-1)
    sin = jnp.concatenate([-jnp.sin(ang_r), jnp.sin(ang_r), -jnp.sin(ang_c), jnp.sin(ang_c)], axis=-1)
    return jnp.tile(cos, (1, n_heads)), jnp.tile(sin, (1, n_heads))


def _lane_pad(v, width=LANES):
    return jnp.pad(v, [(0, 0)] * (v.ndim - 1) + [(0, width - v.shape[-1])])


def kernel(x, c, ctx, c_ctx, w_mod, b_mod, norm_pre_mix, norm_post_mix, norm_pre_ffn, norm_post_ffn, w_in, w_out, ret_decay_f, ret_decay_b, ret_gn_w, ssd_conv_w, ssd_conv_b, ssd_dt_bias_f, ssd_dt_bias_b, ssd_a_log_f, ssd_a_log_b, ssd_d, ssd_norm_w, moe_w_rg, moe_b_rg, moe_w_re, moe_b_re, moe_w_gate, moe_w_up, moe_w_down):
    b, L, d = x.shape
    assert w_mod.shape[0] == 1, "single layer: context outputs are never needed"
    assert TM_OUT == TD_DISPATCH == TF_COMBINE, "router, dispatch and combine share one slot-row layout"
    rw = RET_HEADS * RET_DK
    nconv = SSD_WIDTH + 2 * SSD_GROUPS * SSD_STATE
    T = b * L

    mod_rows = -(-(b + 1) // SUBLANES) * SUBLANES
    c_all = jnp.zeros((mod_rows, d), F32).at[:b].set(c).at[b].set(c_ctx)
    mod3 = _modulation(c_all, w_mod[0], b_mod[0]).reshape(mod_rows, 1, 6 * d)

    wi = w_in[0]
    o = 0
    wq = wi[:, o:o + rw]; o += rw
    wk = wi[:, o:o + rw]; o += rw
    wv = wi[:, o:o + rw]; o += rw
    wg = wi[:, o:o + rw]; o += rw
    wz = wi[:, o:o + SSD_WIDTH]; o += SSD_WIDTH
    wxbc = wi[:, o:o + nconv].astype(BF16); o += nconv
    wdt = _lane_pad(wi[:, o:o + 2 * SSD_HEADS]).astype(BF16)
    wqk = jnp.concatenate([wq, wk], axis=1).astype(BF16)
    wvgz = jnp.concatenate([wv, wg, wz], axis=1).astype(BF16)
    cos_t, sin_t = _rope_tables(L, RET_HEADS)
    nw1 = norm_pre_mix[0].reshape(1, d)

    q, k, v, g, z, xbc, dt = _inproj(x, mod3, nw1, wqk, wvgz, wxbc, wdt, cos_t, sin_t)
    kc, vc, xbcc, dtc = _inproj_ctx(ctx, mod3, b, nw1, wk.astype(BF16), wv.astype(BF16), wxbc, wdt)

    conv_w8 = jnp.pad(ssd_conv_w[0], ((0, SUBLANES - SSD_CONV), (0, 0)))
    dt_bias = _lane_pad(jnp.concatenate([ssd_dt_bias_f[0], ssd_dt_bias_b[0]])[None, :])
    a_log = _lane_pad(jnp.concatenate([ssd_a_log_f[0], ssd_a_log_b[0]])[None, :])
    d_skip = jnp.repeat(ssd_d[0], SSD_HEADDIM)[None, :]
    ys = _ssd(xbc, z, dt, xbcc, dtc, conv_w8, ssd_conv_b[0][None, :], dt_bias, a_log, d_skip,
              ssd_norm_w[0][None, :])

    yr = _retention(q, k, v, g, kc, vc,
                    jnp.repeat(ret_decay_f[0], RET_DK)[None, :], jnp.repeat(ret_decay_b[0], RET_DK)[None, :],
                    ret_gn_w[0][None, :])

    wo = w_out[0].astype(BF16)
    w_router = _lane_pad(jnp.concatenate(
        [jnp.transpose(moe_w_re[0], (1, 0, 2)).reshape(d, N_EXPERTS), moe_w_rg[0]], axis=1))
    b_router = _lane_pad(jnp.concatenate([moe_b_re[0].reshape(-1), moe_b_rg[0]])[None, :])
    x1, h2, route, lpos, seg = _outproj_router(
        yr.reshape(T, rw), ys.reshape(T, SSD_WIDTH), x.reshape(T, d), mod3,
        norm_post_mix[0][None, :], norm_pre_ffn[0][None, :], wo[:rw], wo[rw:], w_router, b_router, L)

    mb = MB_EXPERT
    nt = T // TM_OUT
    n_blocks = -(-(T * TOP_K + nt * N_EXPERTS * (SEG_ALIGN - 1) + N_EXPERTS * (mb - 1)) // mb)
    n_blocks = -(-n_blocks // EXPERT_BLOCKS_PER_STEP) * EXPERT_BLOCKS_PER_STEP
    seg_len = seg[:, 0, :N_EXPERTS]
    seg_off = seg[:, 1, :N_EXPERTS]
    used = jnp.sum(seg_len, axis=0)
    padded = (used + mb - 1) // mb * mb
    pad_end = jnp.cumsum(padded)
    pad_start = pad_end - padded
    seg_row = pad_start[None, :] + jnp.cumsum(seg_len, axis=0) - seg_len
    seg_plan = [a.reshape(-1).astype(jnp.int32)
                for a in (seg_row, seg_len // SEG_ALIGN, seg_off, jnp.sum(seg_len, axis=1))]
    zero_from = (pad_start + used).astype(jnp.int32)

    xs = _dispatch_seg([pad_end.astype(jnp.int32), zero_from] + seg_plan, lpos, route, h2, n_blocks * mb)
    yb = _experts(_expert_plan(padded, pad_end, n_blocks, mb), xs, moe_w_gate[0], moe_w_up[0], moe_w_down[0])
    out = _combine_seg(seg_plan, route, x1, mod3, norm_post_ffn[0][None, :], yb, L)
    return out.reshape(b, L, d)
```

```python
import jax
import jax.numpy as jnp
from jax import lax
from jax.experimental import pallas as pl
from jax.experimental.pallas import tpu as pltpu

F32 = jnp.float32
BF16 = jnp.bfloat16

LANES = 128
SUBLANES = 8
BF16_TILE_ROWS = 16
V7X_VMEM_BYTES = 64 * 1024 * 1024
VMEM_LIMIT = V7X_VMEM_BYTES * 3 // 4
VMEM_LIMIT_SSD = V7X_VMEM_BYTES * 7 // 8

EPS = 1e-6
CHUNK = 128
GRID_W = 64
RET_HEADS = 4
RET_DK = 128
ROPE_BASE = 10000.0
SSD_HEADS = 8
SSD_HEADDIM = 64
SSD_GROUPS = 2
SSD_STATE = 128
SSD_WIDTH = SSD_HEADS * SSD_HEADDIM
SSD_CONV = 5
SSD_PAIRS = SSD_WIDTH // LANES
MOE_GROUPS = 4
EXPERTS_PER_GROUP = 8
N_EXPERTS = MOE_GROUPS * EXPERTS_PER_GROUP
TOP_K = 2
CONV_HALO = SUBLANES

TM_PROJ = 512
TM_OUT = 512
TD_DISPATCH = TM_OUT
MB_EXPERT = 512
EXPERT_BLOCKS_PER_STEP = 4
ZERO_PIECE = MB_EXPERT // 2
TF_COMBINE = TM_OUT
SEG_ALIGN = BF16_TILE_ROWS
COMMON_STAGE_ROWS = TOP_K * TM_OUT + N_EXPERTS * 10
RET_UNROLL = 8
SSD_PREP_UNROLL = 4
SSD_OUT_UNROLL = 2
NEG_BIG = -1e30


def _silu(v):
    return v * jax.nn.sigmoid(v)


def _dot(a, b):
    return jnp.dot(a, b, preferred_element_type=F32)


def _dot_tn(a, b):
    return lax.dot_general(a, b, (((0,), (0,)), ((), ())), preferred_element_type=F32)


def _dot_nt(a, b):
    return lax.dot_general(a, b, (((1,), (1,)), ((), ())), preferred_element_type=F32)


def _mod_kernel(c_ref, w_ref, b_ref, o_ref):
    a = _silu(c_ref[...])
    w = w_ref[...]
    a_hi = a.astype(BF16)
    a_lo = (a - a_hi.astype(F32)).astype(BF16)
    w_hi = w.astype(BF16)
    w_lo = (w - w_hi.astype(F32)).astype(BF16)
    o_ref[...] = _dot(a_hi, w_hi) + _dot(a_lo, w_hi) + _dot(a_hi, w_lo) + b_ref[...]


def _modulation(c_all, w_mod, b_mod):
    rows, d = c_all.shape
    n = w_mod.shape[1]
    return pl.pallas_call(
        _mod_kernel,
        grid=(n // d,),
        in_specs=[
            pl.BlockSpec((rows, d), lambda j: (0, 0)),
            pl.BlockSpec((d, d), lambda j: (0, j)),
            pl.BlockSpec((1, d), lambda j: (0, j)),
        ],
        out_specs=pl.BlockSpec((rows, d), lambda j: (0, j)),
        out_shape=jax.ShapeDtypeStruct((rows, n), F32),
        name="modulation",
    )(c_all, w_mod, b_mod.reshape(1, n))


def _norm_mod(x, nw, sc, sh):
    ms = jnp.mean(x * x, axis=-1, keepdims=True)
    return (x * lax.rsqrt(ms + EPS)) * (nw * (1.0 + sc)) + sh


def _rope(t, cos, sin_signed, first_half):
    width = t.shape[-1]
    quarter = RET_DK // 4
    swapped = jnp.where(first_half, pltpu.roll(t, width - quarter, 1), pltpu.roll(t, quarter, 1))
    return t * cos + swapped * sin_signed


def _inproj_kernel(x_ref, sh_ref, sc_ref, nw_ref, wqk_ref, wvgz_ref, wxbc_ref, wdt_ref, cos_ref, sin_ref,
                   q_ref, k_ref, v_ref, g_ref, z_ref, xbc_ref, dt_ref):
    hb = _norm_mod(x_ref[0], nw_ref[...], sc_ref[0], sh_ref[0]).astype(BF16)
    rw = q_ref.shape[-1]
    qk = _dot(hb, wqk_ref[...])
    cos = cos_ref[...]
    sin = sin_ref[...]
    lane = lax.broadcasted_iota(jnp.int32, cos.shape, 1)
    first_half = (lane % (RET_DK // 2)) < (RET_DK // 4)
    q_ref[0] = _rope(qk[:, :rw], cos, sin, first_half).astype(BF16)
    k_ref[0] = (_rope(qk[:, rw:], cos, sin, first_half) * (RET_DK ** -0.5)).astype(BF16)
    vgz = _dot(hb, wvgz_ref[...])
    v_ref[0] = vgz[:, :rw].astype(BF16)
    g_ref[0] = vgz[:, rw:2 * rw].astype(BF16)
    z_ref[0] = vgz[:, 2 * rw:].astype(BF16)
    xbc_ref[0] = _dot(hb, wxbc_ref[...]).astype(BF16)
    dt_ref[0] = _dot(hb, wdt_ref[...])


def _inproj_ctx_kernel(x_ref, sh_ref, sc_ref, nw_ref, wk_ref, wv_ref, wxbc_ref, wdt_ref,
                       k_ref, v_ref, xbc_ref, dt_ref):
    hb = _norm_mod(x_ref[0], nw_ref[...], sc_ref[0], sh_ref[0]).astype(BF16)
    k_ref[0] = (_dot(hb, wk_ref[...]) * (RET_DK ** -0.5)).astype(BF16)
    v_ref[0] = _dot(hb, wv_ref[...]).astype(BF16)
    xbc_ref[0] = _dot(hb, wxbc_ref[...]).astype(BF16)
    dt_ref[0] = _dot(hb, wdt_ref[...])


def _const_spec(shape):
    nd = len(shape)
    return pl.BlockSpec(shape, lambda *_: (0,) * nd)


def _inproj(x, mod3, nw, wqk, wvgz, wxbc, wdt, cos_t, sin_t):
    b, L, d = x.shape
    tm = min(TM_PROJ, L)
    rw = wqk.shape[1] // 2
    tok = lambda w: pl.BlockSpec((1, tm, w), lambda i, j: (i, j, 0))
    out_bf = lambda w: jax.ShapeDtypeStruct((b, L, w), BF16)
    return pl.pallas_call(
        _inproj_kernel,
        grid=(b, L // tm),
        in_specs=[
            tok(d),
            pl.BlockSpec((1, 1, d), lambda i, j: (i, 0, 0)),
            pl.BlockSpec((1, 1, d), lambda i, j: (i, 0, 1)),
            _const_spec((1, d)),
            _const_spec(wqk.shape), _const_spec(wvgz.shape), _const_spec(wxbc.shape), _const_spec(wdt.shape),
            pl.BlockSpec((tm, rw), lambda i, j: (j, 0)),
            pl.BlockSpec((tm, rw), lambda i, j: (j, 0)),
        ],
        out_specs=[tok(rw), tok(rw), tok(rw), tok(rw), tok(rw), tok(wxbc.shape[1]), tok(LANES)],
        out_shape=[out_bf(rw), out_bf(rw), out_bf(rw), out_bf(rw), out_bf(rw), out_bf(wxbc.shape[1]),
                   jax.ShapeDtypeStruct((b, L, LANES), F32)],
        compiler_params=pltpu.CompilerParams(vmem_limit_bytes=VMEM_LIMIT),
        name="inproj",
    )(x, mod3, mod3, nw, wqk, wvgz, wxbc, wdt, cos_t, sin_t)


def _inproj_ctx(ctx, mod3, ctx_row, nw, wk, wv, wxbc, wdt):
    b, L, d = ctx.shape
    tm = min(TM_PROJ, L)
    rw = wk.shape[1]
    tok = lambda w: pl.BlockSpec((1, tm, w), lambda i, j: (i, j, 0))
    out_bf = lambda w: jax.ShapeDtypeStruct((b, L, w), BF16)
    return pl.pallas_call(
        _inproj_ctx_kernel,
        grid=(b, L // tm),
        in_specs=[
            tok(d),
            pl.BlockSpec((1, 1, d), lambda i, j: (ctx_row, 0, 0)),
            pl.BlockSpec((1, 1, d), lambda i, j: (ctx_row, 0, 1)),
            _const_spec((1, d)),
            _const_spec(wk.shape), _const_spec(wv.shape), _const_spec(wxbc.shape), _const_spec(wdt.shape),
        ],
        out_specs=[tok(rw), tok(rw), tok(wxbc.shape[1]), tok(LANES)],
        out_shape=[out_bf(rw), out_bf(rw), out_bf(wxbc.shape[1]), jax.ShapeDtypeStruct((b, L, LANES), F32)],
        compiler_params=pltpu.CompilerParams(vmem_limit_bytes=VMEM_LIMIT),
        name="inproj_ctx",
    )(ctx, mod3, mod3, nw, wk, wv, wxbc, wdt)


def _ssd_kernel(xbc_ref, z_ref, dt_ref, xbcc_ref, dtc_ref, cw_ref, cb_ref, dtb_ref, alog_ref, dsk_ref, nw_ref,
                y_ref,
                xpad, xpadc, u, uc, dtv, dav, dtcv, dacv, sf_scr, kb_scr, acum, ecum, dec_scr,
                arow_scr, erow_scr, dtrow_scr):
    L = xbc_ref.shape[1]
    Lc = xbcc_ref.shape[1]
    nch = L // CHUNK
    nchc = Lc // CHUNK
    win = CHUNK + 2 * CONV_HALO
    nconv = xbc_ref.shape[2]
    nh = SSD_HEADS

    def conv_pass(src_ref, pad_ref, dst_ref, n_chunks, length):
        zeros = jnp.zeros((CONV_HALO, nconv), F32)
        pad_ref[0:CONV_HALO, :] = zeros
        pad_ref[CONV_HALO + length:2 * CONV_HALO + length, :] = zeros
        pad_ref[CONV_HALO:CONV_HALO + length, :] = src_ref[0].astype(F32)

        def chunk(c, carry):
            base = pl.multiple_of(c * CHUNK, CHUNK)
            for cb_i in range(nconv // LANES):
                cols = slice(cb_i * LANES, (cb_i + 1) * LANES)
                w = pad_ref[pl.ds(base, win), cols]
                acc = cb_ref[:, cols] + w[CONV_HALO:CONV_HALO + CHUNK] * cw_ref[SSD_CONV // 2:SSD_CONV // 2 + 1, cols]
                for j in range(SSD_CONV):
                    if j == SSD_CONV // 2:
                        continue
                    shifted = pltpu.roll(w, (SSD_CONV // 2 - j) % win, 0)
                    acc = acc + shifted[CONV_HALO:CONV_HALO + CHUNK] * cw_ref[j:j + 1, cols]
                dst_ref[pl.ds(base, CHUNK), cols] = _silu(acc).astype(BF16)
            return carry

        lax.fori_loop(0, n_chunks, chunk, 0)

    conv_pass(xbcc_ref, xpadc, uc, nchc, Lc)
    conv_pass(xbc_ref, xpad, u, nch, L)

    a_neg = -jnp.exp(alog_ref[...])
    dtv[...] = jax.nn.softplus(dt_ref[0] + dtb_ref[...])
    dav[...] = dtv[...] * a_neg
    dtcv[...] = jax.nn.softplus(dtc_ref[0] + dtb_ref[...])
    dacv[...] = dtcv[...] * a_neg

    row_i = lax.broadcasted_iota(jnp.int32, (CHUNK, CHUNK), 0)
    col_i = lax.broadcasted_iota(jnp.int32, (CHUNK, CHUNK), 1)
    causal = col_i <= row_i
    lo_half = col_i < SSD_HEADDIM
    fwd_lane = col_i < nh
    head_of = lax.broadcasted_iota(jnp.int32, (CHUNK, SSD_WIDTH), 1) // SSD_HEADDIM
    src_col = lax.broadcasted_iota(jnp.int32, (CHUNK, SSD_WIDTH), 0)
    exp_f = (head_of == src_col).astype(BF16)
    exp_b = (head_of == src_col - nh).astype(BF16)
    exp_fb = jnp.concatenate([exp_f, exp_b], axis=1)

    def split3(v):
        hi = v.astype(BF16)
        r1 = v - hi.astype(F32)
        mid = r1.astype(BF16)
        return hi, mid, (r1 - mid.astype(F32)).astype(BF16)

    def times_onehot(v, m, passes=3):
        parts = split3(v)[:passes]
        acc = _dot(parts[0], m)
        for part in parts[1:]:
            acc = acc + _dot(part, m)
        return acc

    def colb(mat, r):
        return jnp.broadcast_to(mat[:, r:r + 1], (CHUNK, CHUNK))

    def pair_sel(a, b_):
        return jnp.where(lo_half, a, b_)

    gw = 2 * LANES

    def chunk_terms(u_ref, dt_s, da_s, base):
        dt = dt_s[pl.ds(base, CHUNK), :]
        da = da_s[pl.ds(base, CHUNK), :]
        acol = da
        for step in (1, 2, 4, 8, 16, 32, 64):
            acol = acol + jnp.where(row_i >= step, pltpu.roll(acol, step, 0), 0.0)
        ecol = acol - da
        last = acol[CHUNK - 1:CHUNK, :]
        wgt = jnp.where(fwd_lane, jnp.exp(last - acol), jnp.exp(ecol)) * dt
        scale = jnp.where(fwd_lane, jnp.exp(acol), jnp.exp(last - ecol))
        wide = times_onehot(jnp.concatenate([wgt, scale], axis=0), exp_fb, passes=1)
        dec = times_onehot(jnp.broadcast_to(jnp.exp(last), (SUBLANES, LANES)), exp_fb)[0:1]
        xs = u_ref[pl.ds(base, CHUNK), 0:SSD_WIDTH].astype(F32)
        kmats = []
        for g in range(SSD_GROUPS):
            xw = jnp.concatenate([xs[:, g * gw:(g + 1) * gw] * wide[:CHUNK, g * gw:(g + 1) * gw],
                                  xs[:, g * gw:(g + 1) * gw] * wide[:CHUNK, SSD_WIDTH + g * gw:SSD_WIDTH + (g + 1) * gw]],
                                 axis=1).astype(BF16)
            bm = u_ref[pl.ds(base, CHUNK), SSD_WIDTH + g * SSD_STATE:SSD_WIDTH + (g + 1) * SSD_STATE]
            kmats.append(_dot_tn(bm, xw))
        return dt, acol, ecol, wide[CHUNK:], dec, kmats

    def advance(s, dec, kmats, backward):
        off = SSD_WIDTH if backward else 0
        koff = gw if backward else 0
        return [dec[:, off + g * gw:off + (g + 1) * gw] * s[g] + kmats[g][:, koff:koff + gw]
                for g in range(SSD_GROUPS)]

    ctx_terms = [chunk_terms(uc, dtcv, dacv, c * CHUNK) for c in range(nchc)]
    s_f0 = [jnp.zeros((SSD_STATE, gw), F32) for _ in range(SSD_GROUPS)]
    for c in range(nchc):
        s_f0 = advance(s_f0, ctx_terms[c][4], ctx_terms[c][5], False)
    s_b0 = [jnp.zeros((SSD_STATE, gw), F32) for _ in range(SSD_GROUPS)]
    for c in reversed(range(nchc)):
        s_b0 = advance(s_b0, ctx_terms[c][4], ctx_terms[c][5], True)

    def prep(c, carry):
        base = pl.multiple_of(c * CHUNK, CHUNK)
        dt, acol, ecol, scale, dec, kmats = chunk_terms(u, dtv, dav, base)
        acum[pl.ds(base, CHUNK), :] = acol
        ecum[pl.ds(base, CHUNK), :] = ecol
        hrow = pl.ds(pl.multiple_of(c * 2 * nh, 2 * nh), 2 * nh)
        arow_scr[hrow, :] = acol.T[:2 * nh]
        erow_scr[hrow, :] = ecol.T[:2 * nh]
        dtrow_scr[hrow, :] = dt.T[:2 * nh]
        xpad[pl.ds(base, CHUNK), :] = scale
        dec_scr[pl.ds(pl.multiple_of(c * SUBLANES, SUBLANES), SUBLANES), :] = jnp.broadcast_to(dec, (SUBLANES, 2 * SSD_WIDTH))
        for g in range(SSD_GROUPS):
            sf_scr[c, g] = kmats[g][:, :gw]
            kb_scr[c, g] = kmats[g][:, gw:]
        return carry

    lax.fori_loop(0, nch, prep, 0, unroll=SSD_PREP_UNROLL)

    def chunk_dec(c):
        return dec_scr[pl.ds(pl.multiple_of(c * SUBLANES, SUBLANES), 1), :]

    def fwd(c, s_old):
        dec = chunk_dec(c)
        new = []
        for g in range(SSD_GROUPS):
            new.append(dec[:, g * gw:(g + 1) * gw] * s_old[g] + sf_scr[c, g])
            sf_scr[c, g] = s_old[g]
        return tuple(new)

    lax.fori_loop(0, nch, fwd, tuple(s_f0))

    def bwd(i, s_b):
        c = nch - 1 - i
        base = pl.multiple_of(c * CHUNK, CHUNK)
        acol = acum[pl.ds(base, CHUNK), :]
        ecol = ecum[pl.ds(base, CHUNK), :]
        hrow = pl.ds(pl.multiple_of(c * 2 * nh, 2 * nh), 2 * nh)
        arow = arow_scr[hrow, :]
        erow = erow_scr[hrow, :]
        dt_t = dtrow_scr[hrow, :]
        scale = xpad[pl.ds(base, CHUNK), :]
        ys = []
        for g in range(SSD_GROUPS):
            bm = u[pl.ds(base, CHUNK), SSD_WIDTH + g * SSD_STATE:SSD_WIDTH + (g + 1) * SSD_STATE]
            cm = u[pl.ds(base, CHUNK), SSD_WIDTH + (SSD_GROUPS + g) * SSD_STATE:SSD_WIDTH + (SSD_GROUPS + g + 1) * SSD_STATE]
            cbm = _dot_nt(cm, bm)
            cs_f = _dot(cm, sf_scr[c, g].astype(BF16))
            cs_b = _dot(cm, s_b[g].astype(BF16))
            for pp in range(SSD_PAIRS // SSD_GROUPS):
                p = g * (SSD_PAIRS // SSD_GROUPS) + pp
                xs_b = u[pl.ds(base, CHUNK), p * LANES:(p + 1) * LANES]
                y_h = []
                for hh in range(2):
                    r = 2 * p + hh
                    arg = jnp.where(causal, colb(acol, r) - arow[r:r + 1, :],
                                    erow[nh + r:nh + r + 1, :] - colb(ecol, nh + r))
                    coef = jnp.where(causal, dt_t[r:r + 1, :], dt_t[nh + r:nh + r + 1, :])
                    gm = (cbm * (jnp.exp(arg) * coef)).astype(BF16)
                    y_h.append(_dot(gm, xs_b))
                sl = slice(pp * LANES, (pp + 1) * LANES)
                wl = slice(p * LANES, (p + 1) * LANES)
                wlb = slice(SSD_WIDTH + p * LANES, SSD_WIDTH + (p + 1) * LANES)
                ys.append(pair_sel(y_h[0], y_h[1]) + cs_f[:, sl] * scale[:, wl] + cs_b[:, sl] * scale[:, wlb]
                          + dsk_ref[:, wl] * xs_b.astype(F32))
        y = jnp.concatenate(ys, axis=1)
        y = y * _silu(z_ref[0, pl.ds(base, CHUNK), :].astype(F32))
        ms = jnp.mean(y * y, axis=-1, keepdims=True)
        y_ref[0, pl.ds(base, CHUNK), :] = ((y * lax.rsqrt(ms + EPS)) * nw_ref[...]).astype(BF16)
        dec = chunk_dec(c)
        return tuple(dec[:, SSD_WIDTH + g * gw:SSD_WIDTH + (g + 1) * gw] * s_b[g] + kb_scr[c, g]
                     for g in range(SSD_GROUPS))

    lax.fori_loop(0, nch, bwd, tuple(s_b0), unroll=SSD_OUT_UNROLL)


def _ssd(xbc, z, dt, xbcc, dtc, conv_w8, conv_b, dt_bias, a_log, d_skip, norm_w):
    b, L, nconv = xbc.shape
    Lc = xbcc.shape[1]
    nch = L // CHUNK
    per_b = lambda n, w: pl.BlockSpec((1, n, w), lambda i: (i, 0, 0))
    return pl.pallas_call(
        _ssd_kernel,
        grid=(b,),
        in_specs=[
            per_b(L, nconv), per_b(L, SSD_WIDTH), per_b(L, LANES), per_b(Lc, nconv), per_b(Lc, LANES),
            _const_spec(conv_w8.shape), _const_spec(conv_b.shape), _const_spec(dt_bias.shape),
            _const_spec(a_log.shape), _const_spec(d_skip.shape), _const_spec(norm_w.shape),
        ],
        out_specs=per_b(L, SSD_WIDTH),
        out_shape=jax.ShapeDtypeStruct((b, L, SSD_WIDTH), BF16),
        scratch_shapes=[
            pltpu.VMEM((L + 2 * CONV_HALO, nconv), F32),
            pltpu.VMEM((Lc + 2 * CONV_HALO, nconv), F32),
            pltpu.VMEM((L, nconv), BF16),
            pltpu.VMEM((Lc, nconv), BF16),
            pltpu.VMEM((L, LANES), F32), pltpu.VMEM((L, LANES), F32),
            pltpu.VMEM((Lc, LANES), F32), pltpu.VMEM((Lc, LANES), F32),
            pltpu.VMEM((nch, SSD_GROUPS, SSD_STATE, 2 * LANES), F32),
            pltpu.VMEM((nch, SSD_GROUPS, SSD_STATE, 2 * LANES), F32),
            pltpu.VMEM((L, LANES), F32), pltpu.VMEM((L, LANES), F32),
            pltpu.VMEM((nch * SUBLANES, 2 * SSD_WIDTH), F32),
            pltpu.VMEM((nch * 2 * SSD_HEADS, CHUNK), F32), pltpu.VMEM((nch * 2 * SSD_HEADS, CHUNK), F32),
            pltpu.VMEM((nch * 2 * SSD_HEADS, CHUNK), F32),
        ],
        compiler_params=pltpu.CompilerParams(vmem_limit_bytes=VMEM_LIMIT_SSD),
        name="ssd",
    )(xbc, z, dt, xbcc, dtc, conv_w8, conv_b, dt_bias, a_log, d_skip, norm_w)


def _ret_kernel(q_ref, k_ref, v_ref, g_ref, kc_ref, vc_ref, df_ref, db_ref, gn_ref, y_ref, sf_scr):
    L = q_ref.shape[1]
    Lc = kc_ref.shape[1]
    nch = L // CHUNK
    dk = RET_DK
    row_i = lax.broadcasted_iota(jnp.int32, (CHUNK, dk), 0).astype(F32)
    col_i = lax.broadcasted_iota(jnp.int32, (CHUNK, dk), 1).astype(F32)
    rel = row_i - col_i
    crow = lax.broadcasted_iota(jnp.int32, (Lc, dk), 0).astype(F32)

    heads = []
    s_f0 = []
    s_b0 = []
    for h in range(RET_HEADS):
        cols = slice(h * dk, (h + 1) * dk)
        lg_f = -jnp.exp(df_ref[:, cols])
        lg_b = -jnp.exp(db_ref[:, cols])
        heads.append(dict(
            cols=cols,
            dmat=jnp.where(rel >= 0, jnp.exp(jnp.maximum(rel, 0.0) * lg_f), jnp.exp(jnp.maximum(-rel, 0.0) * lg_b)),
            dq_f=jnp.exp((row_i + 1.0) * lg_f),
            dq_b=jnp.exp((CHUNK - row_i) * lg_b),
            dk_f=jnp.exp((CHUNK - 1.0 - row_i) * lg_f),
            dk_b=jnp.exp(row_i * lg_b),
            dc_f=jnp.exp(CHUNK * lg_f),
            dc_b=jnp.exp(CHUNK * lg_b),
        ))
        kc = kc_ref[0, :, cols].astype(F32)
        vc = vc_ref[0, :, cols]
        s_f0.append(_dot_tn((kc * jnp.exp((Lc - 1.0 - crow) * lg_f)).astype(BF16), vc))
        s_b0.append(_dot_tn((kc * jnp.exp(crow * lg_b)).astype(BF16), vc))

    def fwd(c, s_f):
        base = pl.multiple_of(c * CHUNK, CHUNK)
        new = []
        for h, hd in enumerate(heads):
            sf_scr[c, h] = s_f[h]
            kk = k_ref[0, pl.ds(base, CHUNK), hd["cols"]].astype(F32)
            vv = v_ref[0, pl.ds(base, CHUNK), hd["cols"]]
            new.append(hd["dc_f"] * s_f[h] + _dot_tn((kk * hd["dk_f"]).astype(BF16), vv))
        return tuple(new)

    lax.fori_loop(0, nch, fwd, tuple(s_f0), unroll=RET_UNROLL)

    def bwd(i, s_bs):
        c = nch - 1 - i
        base = pl.multiple_of(c * CHUNK, CHUNK)
        new = []
        for h, hd in enumerate(heads):
            qq = q_ref[0, pl.ds(base, CHUNK), hd["cols"]]
            kk = k_ref[0, pl.ds(base, CHUNK), hd["cols"]]
            vv = v_ref[0, pl.ds(base, CHUNK), hd["cols"]]
            s_b = s_bs[h]
            scores = (_dot_nt(qq, kk) * hd["dmat"]).astype(BF16)
            y = (_dot(scores, vv)
                 + _dot(qq, sf_scr[c, h].astype(BF16)) * hd["dq_f"]
                 + _dot(qq, s_b.astype(BF16)) * hd["dq_b"])
            mu = jnp.mean(y, axis=-1, keepdims=True)
            yc = y - mu
            var = jnp.mean(yc * yc, axis=-1, keepdims=True)
            yn = (yc * lax.rsqrt(var + EPS)) * gn_ref[:, hd["cols"]]
            gate = _silu(g_ref[0, pl.ds(base, CHUNK), hd["cols"]].astype(F32))
            y_ref[0, pl.ds(base, CHUNK), hd["cols"]] = (yn * gate).astype(BF16)
            new.append(hd["dc_b"] * s_b + _dot_tn((kk.astype(F32) * hd["dk_b"]).astype(BF16), vv))
        return tuple(new)

    lax.fori_loop(0, nch, bwd, tuple(s_b0), unroll=RET_UNROLL)


def _retention(q, k, v, g, kc, vc, decay_f, decay_b, gn_w):
    b, L, w = q.shape
    Lc = kc.shape[1]
    nch = L // CHUNK
    per_b = lambda n: pl.BlockSpec((1, n, w), lambda i: (i, 0, 0))
    return pl.pallas_call(
        _ret_kernel,
        grid=(b,),
        in_specs=[per_b(L), per_b(L), per_b(L), per_b(L), per_b(Lc), per_b(Lc),
                  _const_spec((1, w)), _const_spec((1, w)), _const_spec((1, w))],
        out_specs=per_b(L),
        out_shape=jax.ShapeDtypeStruct((b, L, w), BF16),
        scratch_shapes=[
            pltpu.VMEM((nch, RET_HEADS, RET_DK, RET_DK), F32),
        ],
        compiler_params=pltpu.CompilerParams(vmem_limit_bytes=VMEM_LIMIT),
        name="retention",
    )(q, k, v, g, kc, vc, decay_f, decay_b, gn_w)


def _outproj_router_kernel(yr_ref, ys_ref, x_ref, g1_ref, sh2_ref, sc2_ref, npost_ref, npre_ref,
                           wor_ref, wos_ref, wr_ref, br_ref, tri_ref,
                           x1_ref, h2_ref, route_ref, slots_ref, seg_ref,
                           wcat):
    i = pl.program_id(0)

    @pl.when(i == 0)
    def _():
        wr = wr_ref[...]
        hi = wr.astype(BF16)
        wcat[:, :LANES] = hi
        wcat[:, LANES:] = (wr - hi.astype(F32)).astype(BF16)

    _route_tile(yr_ref, ys_ref, x_ref, g1_ref, sh2_ref, sc2_ref, npost_ref, npre_ref, wor_ref, wos_ref, br_ref,
                tri_ref, x1_ref, h2_ref, route_ref, slots_ref, seg_ref, wcat)


def _route_tile(yr_ref, ys_ref, x_ref, g1_ref, sh2_ref, sc2_ref, npost_ref, npre_ref, wor_ref, wos_ref, br_ref,
                tri_ref, x1_ref, h2_ref, route_ref, slots_ref, seg_ref, wcat):
    tm = x_ref.shape[0]
    rows = slice(0, tm)
    y = _dot(yr_ref[rows, :], wor_ref[...]) + _dot(ys_ref[rows, :], wos_ref[...])
    ms = jnp.mean(y * y, axis=-1, keepdims=True)
    x1 = x_ref[rows, :] + (y * lax.rsqrt(ms + EPS)) * (g1_ref[0] * npost_ref[...])
    x1_ref[rows, :] = x1
    h2 = _norm_mod(x1, npre_ref[...], sc2_ref[0], sh2_ref[0])
    h2_ref[rows, :] = h2.astype(BF16)

    h_hi = h2.astype(BF16)
    h_lo = (h2 - h_hi.astype(F32)).astype(BF16)
    both = _dot(h_hi, wcat[...])
    lg = both[:, :LANES] + both[:, LANES:] + _dot(h_lo, wcat[:, :LANES]) + br_ref[...]

    lane = lax.broadcasted_iota(jnp.int32, (tm, LANES), 1)
    lane_f = lane.astype(F32)
    is_grp = (lane >= N_EXPERTS) & (lane < N_EXPERTS + MOE_GROUPS)
    gl = jnp.where(is_grp, lg, NEG_BIG)
    mg = jnp.max(gl, axis=-1, keepdims=True)
    grp_lane = jnp.min(jnp.where(gl == mg, lane_f, 1e9), axis=-1, keepdims=True)
    p_g = 1.0 / jnp.sum(jnp.where(is_grp, jnp.exp(gl - mg), 0.0), axis=-1, keepdims=True)
    first = (grp_lane - N_EXPERTS) * EXPERTS_PER_GROUP
    in_grp = (lane_f >= first) & (lane_f < first + EXPERTS_PER_GROUP)
    el = jnp.where(in_grp, lg, NEG_BIG)
    t1 = jnp.max(el, axis=-1, keepdims=True)
    i1 = jnp.min(jnp.where(el == t1, lane_f, 1e9), axis=-1, keepdims=True)
    el2 = jnp.where(lane_f == i1, NEG_BIG, el)
    t2 = jnp.max(el2, axis=-1, keepdims=True)
    i2 = jnp.min(jnp.where(el2 == t2, lane_f, 1e9), axis=-1, keepdims=True)
    s = jnp.exp(t2 - t1)
    w1 = p_g / (1.0 + s)
    w2 = p_g * s / (1.0 + s)

    oh1 = (lane_f == i1)
    oh2 = (lane_f == i2)
    ohf = jnp.where(oh1 | oh2, 1.0, 0.0)
    before = _dot(tri_ref[...], ohf.astype(BF16))
    cnt = jnp.sum(ohf, axis=0, keepdims=True)
    seg = jnp.floor((cnt + (SEG_ALIGN - 1.0)) * (1.0 / SEG_ALIGN)) * SEG_ALIGN
    e_row = lax.broadcasted_iota(jnp.int32, (LANES, LANES), 0)
    e_col = lax.broadcasted_iota(jnp.int32, (LANES, LANES), 1)
    earlier = (e_row < e_col).astype(BF16)
    seg_off = _dot(jnp.broadcast_to(seg, (SUBLANES, LANES)).astype(BF16), earlier)[0:1]
    where_to = before + seg_off
    lpos1 = jnp.sum(jnp.where(oh1, where_to, 0.0), axis=-1, keepdims=True)
    lpos2 = jnp.sum(jnp.where(oh2, where_to, 0.0), axis=-1, keepdims=True)

    cols = [w1, w2, lpos1, lpos2]
    for wk in (w1, w2):
        hi = wk.astype(BF16).astype(F32)
        cols += [hi, wk - hi]
    cols.append(jnp.ones_like(w1))
    packed = jnp.zeros((tm, LANES), F32)
    for k, col in enumerate(cols):
        packed = jnp.where(lane == k, col, packed)
    route_ref[rows, :] = packed

    row = lax.broadcasted_iota(jnp.int32, (tm, LANES), 0)
    on_diag = (row % LANES) == lane
    per = tm // LANES
    for qi, col in enumerate((lpos1, lpos2)):
        picked = jnp.where(on_diag, col, 0.0)
        dense = jnp.sum(picked.reshape(per, LANES, LANES), axis=1).astype(jnp.int32)
        slots_ref[0, qi * per:(qi + 1) * per, :] = dense
    tbl_row = lax.broadcasted_iota(jnp.int32, (SUBLANES, LANES), 0)
    seg_ref[0] = jnp.where(tbl_row == 0, seg, jnp.where(tbl_row == 1, seg_off, 0.0)).astype(jnp.int32)


def _outproj_router(yr, ys, x2, mod3, npost, npre, wo_r, wo_s, w_router, b_router, seq_len):
    T, d = x2.shape
    tm = TM_OUT
    per_seq = seq_len // tm
    rw = yr.shape[1]
    tri = (jnp.arange(tm)[:, None] > jnp.arange(tm)[None, :]).astype(BF16)
    tok = lambda w: pl.BlockSpec((tm, w), lambda i: (i, 0))
    modv = lambda k: pl.BlockSpec((1, 1, d), lambda i: (i // per_seq, 0, k))
    tile3 = lambda r: pl.BlockSpec((1, r, LANES), lambda i: (i, 0, 0))
    slot_rows = TOP_K * (tm // LANES)
    return pl.pallas_call(
        _outproj_router_kernel,
        grid=(T // tm,),
        in_specs=[
            tok(rw), tok(rw), tok(d), modv(2), modv(3), modv(4),
            _const_spec((1, d)), _const_spec((1, d)),
            _const_spec(wo_r.shape), _const_spec(wo_s.shape), _const_spec(w_router.shape), _const_spec((1, LANES)),
            _const_spec((tm, tm)),
        ],
        out_specs=[tok(d), tok(d), tok(LANES), tile3(slot_rows), tile3(SUBLANES)],
        out_shape=[jax.ShapeDtypeStruct((T, d), F32), jax.ShapeDtypeStruct((T, d), BF16),
                   jax.ShapeDtypeStruct((T, LANES), F32),
                   jax.ShapeDtypeStruct((T // tm, slot_rows, LANES), jnp.int32),
                   jax.ShapeDtypeStruct((T // tm, SUBLANES, LANES), jnp.int32)],
        scratch_shapes=[pltpu.VMEM((d, 2 * LANES), BF16)],
        compiler_params=pltpu.CompilerParams(dimension_semantics=("arbitrary",),
                                             vmem_limit_bytes=VMEM_LIMIT),
        name="outproj_router",
    )(yr, ys, x2, mod3, mod3, mod3, npost, npre, wo_r, wo_s, w_router, b_router, tri)


def _expert_kernel(be_ref, first_ref, slot_ref, next_ref, nused_ref, xs_ref, wg_hbm, wu_hbm, wd_hbm, y_ref,
                   wg_f, wu_f, wd_f, wg_b, wu_b, wd_b, sem):
    def fetch(e, s):
        return [pltpu.make_async_copy(src.at[e], dst.at[s], sem.at[s, k])
                for k, (src, dst) in enumerate(((wg_hbm, wg_f), (wu_hbm, wu_f), (wd_hbm, wd_f)))]

    @pl.when(pl.program_id(0) == 0)
    def _():
        for c in fetch(be_ref[0], slot_ref[0]):
            c.start()

    for sb in range(EXPERT_BLOCKS_PER_STEP):
        _expert_block(pl.program_id(0) * EXPERT_BLOCKS_PER_STEP + sb, slice(sb * MB_EXPERT, (sb + 1) * MB_EXPERT),
                      fetch, be_ref, first_ref, slot_ref, next_ref, nused_ref, xs_ref, y_ref,
                      wg_f, wu_f, wd_f, wg_b, wu_b, wd_b)


def _expert_block(i, rows, fetch, be_ref, first_ref, slot_ref, next_ref, nused_ref, xs_ref, y_ref,
                  wg_f, wu_f, wd_f, wg_b, wu_b, wd_b):
    @pl.when(first_ref[i] == 1)
    def _():
        s = slot_ref[i]

        @pl.when(next_ref[i] >= 0)
        def _():
            for c in fetch(next_ref[i], 1 - s):
                c.start()

        for c in fetch(be_ref[i], s):
            c.wait()
        wg_b[...] = wg_f[s].astype(BF16)
        wu_b[...] = wu_f[s].astype(BF16)
        wd_b[...] = wd_f[s].astype(BF16)

    @pl.when(i < nused_ref[0])
    def _():
        d = y_ref.shape[1]
        side = xs_ref[rows, d:d + LANES].astype(F32)
        second = side[:, 8:9] == 2.0
        unscale = jnp.where(second, 0.5, 1.0)
        weight = jnp.where(second, side[:, 6:7] + side[:, 7:8], side[:, 4:5] + side[:, 5:6]) * unscale
        xb = xs_ref[rows, 0:d] * unscale.astype(BF16)
        hid = (_silu(_dot(xb, wg_b[...])) * _dot(xb, wu_b[...])).astype(BF16)
        y_ref[rows, :] = (_dot(hid, wd_b[...]) * weight).astype(BF16)

    @pl.when(i >= nused_ref[0])
    def _():
        y_ref[rows, :] = jnp.zeros((rows.stop - rows.start, y_ref.shape[1]), y_ref.dtype)


def _expert_plan(padded, pad_end, n_blocks, mb):
    n_used = (pad_end[-1:] // mb).astype(jnp.int32)
    blk_start = jnp.arange(n_blocks, dtype=jnp.int32) * mb
    experts = jnp.arange(N_EXPERTS, dtype=jnp.int32)
    blk_expert = jnp.minimum(jnp.sum((pad_end[None, :] <= blk_start[:, None]).astype(jnp.int32), axis=1),
                             N_EXPERTS - 1)
    prev = jnp.concatenate([jnp.full((1,), -1, jnp.int32), blk_expert[:-1]])
    first = ((blk_start < pad_end[-1]) & (blk_expert != prev)).astype(jnp.int32)
    slot = (jnp.cumsum(first) - 1) % 2
    later = jnp.where((padded > 0)[None, :] & (experts[None, :] > experts[:, None]), experts[None, :], N_EXPERTS)
    next_nonempty = jnp.min(later, axis=1)
    next_nonempty = jnp.where(next_nonempty == N_EXPERTS, -1, next_nonempty)
    nxt = jnp.sum(jnp.where(blk_expert[:, None] == experts[None, :], next_nonempty[None, :], 0), axis=1)
    return [a.astype(jnp.int32) for a in (blk_expert, first, slot, nxt, n_used)]


def _experts(plan, xs, w_gate, w_up, w_down):
    cap, dp = xs.shape
    n_exp, d, de = w_gate.shape
    step_rows = MB_EXPERT * EXPERT_BLOCKS_PER_STEP
    assert dp == d + LANES and cap % step_rows == 0
    grid_spec = pltpu.PrefetchScalarGridSpec(
        num_scalar_prefetch=len(plan),
        grid=(cap // step_rows,),
        in_specs=[
            pl.BlockSpec((step_rows, dp), lambda i, be, fi, sl, nx, nu: (
                jnp.minimum(i, (nu[0] - 1) // EXPERT_BLOCKS_PER_STEP), 0)),
            pl.BlockSpec(memory_space=pl.ANY), pl.BlockSpec(memory_space=pl.ANY), pl.BlockSpec(memory_space=pl.ANY),
        ],
        out_specs=pl.BlockSpec((step_rows, d), lambda i, be, fi, sl, nx, nu: (i, 0)),
        scratch_shapes=[pltpu.VMEM((2, d, de), F32), pltpu.VMEM((2, d, de), F32), pltpu.VMEM((2, de, d), F32),
                        pltpu.VMEM((d, de), BF16), pltpu.VMEM((d, de), BF16), pltpu.VMEM((de, d), BF16),
                        pltpu.SemaphoreType.DMA((2, 3))],
    )
    return pl.pallas_call(
        _expert_kernel,
        grid_spec=grid_spec,
        out_shape=jax.ShapeDtypeStruct((cap, d), BF16),
        compiler_params=pltpu.CompilerParams(dimension_semantics=("arbitrary",),
                                             vmem_limit_bytes=VMEM_LIMIT),
        name="experts",
    )(*plan, xs, w_gate, w_up, w_down)


def _segment_pieces(tile, segrow_ref, seglen_ref, segoff_ref, act):
    def per_expert(e, carry):
        idx = tile * N_EXPERTS + e
        g0 = segrow_ref[idx]
        l0 = segoff_ref[idx]

        def piece(j, c2):
            act(pl.multiple_of(l0 + j * SEG_ALIGN, SEG_ALIGN), pl.multiple_of(g0 + j * SEG_ALIGN, SEG_ALIGN))
            return c2

        lax.fori_loop(0, seglen_ref[idx], piece, 0)
        return carry

    lax.fori_loop(0, N_EXPERTS, per_expert, 0)


def _wait_rows(total, row_copy):
    size = SEG_ALIGN
    while size <= _stage_rows(max(TD_DISPATCH, TF_COMBINE)):
        @pl.when((total & size) != 0)
        def _(size=size):
            row_copy(size).wait()
        size *= 2


def _for_tile_rows(tile_rows, max_rows, body):
    @pl.when(tile_rows <= COMMON_STAGE_ROWS)
    def _():
        body(COMMON_STAGE_ROWS)

    @pl.when(tile_rows > COMMON_STAGE_ROWS)
    def _():
        body(max_rows)


def _stage_rows(tile_tokens):
    return TOP_K * tile_tokens + N_EXPERTS * SEG_ALIGN


def _dispatch_seg_kernel(pad_end_ref, zero_from_ref, segrow_ref, seglen_ref, segoff_ref, tilerows_ref,
                         lpos_ref, route_ref, h_ref, xs_hbm, zbuf, stage, sem, zsem):
    i = pl.program_id(0)
    nt = pl.num_programs(0)
    td, d = h_ref.shape
    sr = stage.shape[1]
    per = td // LANES

    def zero_fills(act):
        def fill(row, n):
            act(pltpu.make_async_copy(zbuf.at[pl.ds(0, n)], xs_hbm.at[pl.ds(pl.multiple_of(row, SEG_ALIGN), n)], zsem))

        def region(e, carry):
            row = zero_from_ref[e]
            left = pad_end_ref[e] - row
            size = ZERO_PIECE
            while size >= SEG_ALIGN:
                take = (left & size) != 0

                @pl.when(take)
                def _(row=row, size=size):
                    fill(row, size)

                row = row + jnp.where(take, size, 0)
                size //= 2
            return carry

        def tail(p, carry):
            @pl.when(p * ZERO_PIECE >= pad_end_ref[N_EXPERTS - 1])
            def _():
                fill(p * ZERO_PIECE, ZERO_PIECE)
            return carry

        lax.fori_loop(0, N_EXPERTS, region, 0)
        lax.fori_loop(0, xs_hbm.shape[0] // ZERO_PIECE, tail, 0)

    @pl.when(i == 0)
    def _():
        zbuf[...] = jnp.zeros_like(zbuf)
        zero_fills(lambda c: c.start())

    def shipped(tile):
        _wait_rows(tilerows_ref[tile], lambda n: pltpu.make_async_copy(
            stage.at[tile % 2, pl.ds(0, n)], xs_hbm.at[pl.ds(0, n)], sem.at[tile % 2]))

    @pl.when(i >= 2)
    def _():
        shipped(i - 2)

    slot = i % 2

    def sort_rows(n_rows):
        srow = lax.broadcasted_iota(jnp.int32, (n_rows, LANES), 0)
        place = jnp.concatenate(
            [jnp.where(srow == lpos_ref[0, cb:cb + 1, :], 1.0,
                       jnp.where(srow == lpos_ref[0, per + cb:per + cb + 1, :], 2.0, 0.0)) for cb in range(per)],
            axis=1).astype(BF16)
        stage[slot, 0:n_rows, :] = _dot(place, jnp.concatenate([h_ref[...], route_ref[...].astype(BF16)],
                                                               axis=1)).astype(BF16)

    _for_tile_rows(tilerows_ref[i], sr, sort_rows)
    _segment_pieces(i, segrow_ref, seglen_ref, segoff_ref, lambda lr, gr: pltpu.make_async_copy(
        stage.at[slot, pl.ds(lr, SEG_ALIGN)], xs_hbm.at[pl.ds(gr, SEG_ALIGN)], sem.at[slot]).start())

    @pl.when(i == nt - 1)
    def _():
        @pl.when(i >= 1)
        def _():
            shipped(i - 1)
        shipped(i)
        zero_fills(lambda c: c.wait())


def _dispatch_seg(plan, lpos, route, h2, cap):
    T, d = h2.shape
    td = TD_DISPATCH
    nt = T // td
    width = d + LANES
    sr = _stage_rows(td)
    n_pre = len(plan)
    grid_spec = pltpu.PrefetchScalarGridSpec(
        num_scalar_prefetch=n_pre,
        grid=(nt,),
        in_specs=[
            pl.BlockSpec((1, lpos.shape[1], LANES), lambda i, *_: (i, 0, 0)),
            pl.BlockSpec((td, LANES), lambda i, *_: (i, 0)),
            pl.BlockSpec((td, d), lambda i, *_: (i, 0)),
        ],
        out_specs=pl.BlockSpec(memory_space=pl.ANY),
        scratch_shapes=[pltpu.VMEM((ZERO_PIECE, width), BF16), pltpu.VMEM((2, sr, width), BF16),
                        pltpu.SemaphoreType.DMA((2,)), pltpu.SemaphoreType.DMA(())],
    )
    return pl.pallas_call(
        _dispatch_seg_kernel,
        grid_spec=grid_spec,
        out_shape=jax.ShapeDtypeStruct((cap, width), BF16),
        compiler_params=pltpu.CompilerParams(dimension_semantics=("arbitrary",), vmem_limit_bytes=VMEM_LIMIT),
        name="dispatch",
    )(*plan, lpos, route, h2)


def _combine_seg_kernel(segrow_ref, seglen_ref, segoff_ref, tilerows_ref, route_ref, x1_ref, g2_ref, nw_ref,
                        yb_hbm, o_ref, stage, sem):
    i = pl.program_id(0)
    nt = pl.num_programs(0)
    tf = x1_ref.shape[0]
    sr = stage.shape[1]

    def fetch(tile, slot):
        _segment_pieces(tile, segrow_ref, seglen_ref, segoff_ref, lambda lr, gr: pltpu.make_async_copy(
            yb_hbm.at[pl.ds(gr, SEG_ALIGN)], stage.at[slot, pl.ds(lr, SEG_ALIGN)], sem.at[slot]).start())

    @pl.when(i == 0)
    def _():
        stage[...] = jnp.zeros_like(stage)
        fetch(i, 0)

    for slot in range(2):
        @pl.when((i + 1 < nt) & (i % 2 != slot))
        def _(slot=slot):
            fetch(i + 1, slot)

    _wait_rows(tilerows_ref[i], lambda n: pltpu.make_async_copy(
        yb_hbm.at[pl.ds(0, n)], stage.at[i % 2, pl.ds(0, n)], sem.at[i % 2]))

    local_row = route_ref[:, 2:2 + TOP_K].astype(jnp.int32)

    def unsort_rows(n_rows):
        scol = lax.broadcasted_iota(jnp.int32, (tf, n_rows), 1)
        pick = jnp.where((scol == local_row[:, 0:1]) | (scol == local_row[:, 1:2]), 1.0, 0.0).astype(BF16)
        out = _dot(pick, stage[i % 2, 0:n_rows, :])
        ms = jnp.mean(out * out, axis=-1, keepdims=True)
        o_ref[...] = x1_ref[...] + g2_ref[0] * ((out * lax.rsqrt(ms + EPS)) * nw_ref[...])

    _for_tile_rows(tilerows_ref[i], sr, unsort_rows)


def _combine_seg(seg_plan, route, x1, mod3, nw, yb, seq_len):
    T, d = x1.shape
    tf = TF_COMBINE
    nt = T // tf
    per_seq = seq_len // tf
    grid_spec = pltpu.PrefetchScalarGridSpec(
        num_scalar_prefetch=len(seg_plan),
        grid=(nt,),
        in_specs=[
            pl.BlockSpec((tf, LANES), lambda i, *_: (i, 0)),
            pl.BlockSpec((tf, d), lambda i, *_: (i, 0)),
            pl.BlockSpec((1, 1, d), lambda i, *_: (i // per_seq, 0, 5)),
            pl.BlockSpec((1, d), lambda i, *_: (0, 0)),
            pl.BlockSpec(memory_space=pl.ANY),
        ],
        out_specs=pl.BlockSpec((tf, d), lambda i, *_: (i, 0)),
        scratch_shapes=[pltpu.VMEM((2, _stage_rows(tf), d), yb.dtype), pltpu.SemaphoreType.DMA((2,))],
    )
    return pl.pallas_call(
        _combine_seg_kernel,
        grid_spec=grid_spec,
        out_shape=jax.ShapeDtypeStruct((T, d), F32),
        compiler_params=pltpu.CompilerParams(dimension_semantics=("arbitrary",), vmem_limit_bytes=VMEM_LIMIT),
        name="combine",
    )(*seg_plan, route, x1, mod3, nw, yb)


def _rope_tables(L, n_heads):
    quarter = RET_DK // 4
    freqs = ROPE_BASE ** (-jnp.arange(quarter, dtype=F32) / quarter)
    t = jnp.arange(L)
    ang_r = (t // GRID_W).astype(F32)[:, None] * freqs
    ang_c = (t % GRID_W).astype(F32)[:, None] * freqs
    cos = jnp.concatenate([jnp.cos(ang_r)] * 2 + [jnp.cos(ang_c)] * 2, axis=-1)
    sin = jnp.concatenate([-jnp.sin(ang_r), jnp.sin(ang_r), -jnp.sin(ang_c), jnp.sin(ang_c)], axis=-1)
    return jnp.tile(cos, (1, n_heads)), jnp.tile(sin, (1, n_heads))


def _lane_pad(v, width=LANES):
    return jnp.pad(v, [(0, 0)] * (v.ndim - 1) + [(0, width - v.shape[-1])])


def kernel(x, c, ctx, c_ctx, w_mod, b_mod, norm_pre_mix, norm_post_mix, norm_pre_ffn, norm_post_ffn, w_in, w_out, ret_decay_f, ret_decay_b, ret_gn_w, ssd_conv_w, ssd_conv_b, ssd_dt_bias_f, ssd_dt_bias_b, ssd_a_log_f, ssd_a_log_b, ssd_d, ssd_norm_w, moe_w_rg, moe_b_rg, moe_w_re, moe_b_re, moe_w_gate, moe_w_up, moe_w_down):
    b, L, d = x.shape
    assert w_mod.shape[0] == 1, "single layer: context outputs are never needed"
    assert TM_OUT == TD_DISPATCH == TF_COMBINE, "router, dispatch and combine share one slot-row layout"
    rw = RET_HEADS * RET_DK
    nconv = SSD_WIDTH + 2 * SSD_GROUPS * SSD_STATE
    T = b * L

    mod_rows = -(-(b + 1) // SUBLANES) * SUBLANES
    c_all = jnp.zeros((mod_rows, d), F32).at[:b].set(c).at[b].set(c_ctx)
    mod3 = _modulation(c_all, w_mod[0], b_mod[0]).reshape(mod_rows, 1, 6 * d)

    wi = w_in[0]
    o = 0
    wq = wi[:, o:o + rw]; o += rw
    wk = wi[:, o:o + rw]; o += rw
    wv = wi[:, o:o + rw]; o += rw
    wg = wi[:, o:o + rw]; o += rw
    wz = wi[:, o:o + SSD_WIDTH]; o += SSD_WIDTH
    wxbc = wi[:, o:o + nconv].astype(BF16); o += nconv
    wdt = _lane_pad(wi[:, o:o + 2 * SSD_HEADS]).astype(BF16)
    wqk = jnp.concatenate([wq, wk], axis=1).astype(BF16)
    wvgz = jnp.concatenate([wv, wg, wz], axis=1).astype(BF16)
    cos_t, sin_t = _rope_tables(L, RET_HEADS)
    nw1 = norm_pre_mix[0].reshape(1, d)

    q, k, v, g, z, xbc, dt = _inproj(x, mod3, nw1, wqk, wvgz, wxbc, wdt, cos_t, sin_t)
    kc, vc, xbcc, dtc = _inproj_ctx(ctx, mod3, b, nw1, wk.astype(BF16), wv.astype(BF16), wxbc, wdt)

    conv_w8 = jnp.pad(ssd_conv_w[0], ((0, SUBLANES - SSD_CONV), (0, 0)))
    dt_bias = _lane_pad(jnp.concatenate([ssd_dt_bias_f[0], ssd_dt_bias_b[0]])[None, :])
    a_log = _lane_pad(jnp.concatenate([ssd_a_log_f[0], ssd_a_log_b[0]])[None, :])
    d_skip = jnp.repeat(ssd_d[0], SSD_HEADDIM)[None, :]
    ys = _ssd(xbc, z, dt, xbcc, dtc, conv_w8, ssd_conv_b[0][None, :], dt_bias, a_log, d_skip,
              ssd_norm_w[0][None, :])

    yr = _retention(q, k, v, g, kc, vc,
                    jnp.repeat(ret_decay_f[0], RET_DK)[None, :], jnp.repeat(ret_decay_b[0], RET_DK)[None, :],
                    ret_gn_w[0][None, :])

    wo = w_out[0].astype(BF16)
    w_router = _lane_pad(jnp.concatenate(
        [jnp.transpose(moe_w_re[0], (1, 0, 2)).reshape(d, N_EXPERTS), moe_w_rg[0]], axis=1))
    b_router = _lane_pad(jnp.concatenate([moe_b_re[0].reshape(-1), moe_b_rg[0]])[None, :])
    x1, h2, route, lpos, seg = _outproj_router(
        yr.reshape(T, rw), ys.reshape(T, SSD_WIDTH), x.reshape(T, d), mod3,
        norm_post_mix[0][None, :], norm_pre_ffn[0][None, :], wo[:rw], wo[rw:], w_router, b_router, L)

    mb = MB_EXPERT
    nt = T // TM_OUT
    n_blocks = -(-(T * TOP_K + nt * N_EXPERTS * (SEG_ALIGN - 1) + N_EXPERTS * (mb - 1)) // mb)
    n_blocks = -(-n_blocks // EXPERT_BLOCKS_PER_STEP) * EXPERT_BLOCKS_PER_STEP
    seg_len = seg[:, 0, :N_EXPERTS]
    seg_off = seg[:, 1, :N_EXPERTS]
    used = jnp.sum(seg_len, axis=0)
    padded = (used + mb - 1) // mb * mb
    pad_end = jnp.cumsum(padded)
    pad_start = pad_end - padded
    seg_row = pad_start[None, :] + jnp.cumsum(seg_len, axis=0) - seg_len
    seg_plan = [a.reshape(-1).astype(jnp.int32)
                for a in (seg_row, seg_len // SEG_ALIGN, seg_off, jnp.sum(seg_len, axis=1))]
    zero_from = (pad_start + used).astype(jnp.int32)

    xs = _dispatch_seg([pad_end.astype(jnp.int32), zero_from] + seg_plan, lpos, route, h2, n_blocks * mb)
    yb = _experts(_expert_plan(padded, pad_end, n_blocks, mb), xs, moe_w_gate[0], moe_w_up[0], moe_w_down[0])
    out = _combine_seg(seg_plan, route, x1, mod3, norm_post_ffn[0][None, :], yb, L)
    return out.reshape(b, L, d)
```

```python
import jax
import jax.numpy as jnp
from jax import lax
from jax.experimental import pallas as pl
from jax.experimental.pallas import tpu as pltpu

F32 = jnp.float32
BF16 = jnp.bfloat16

LANES = 128
SUBLANES = 8
BF16_TILE_ROWS = 16
V7X_VMEM_BYTES = 64 * 1024 * 1024
VMEM_LIMIT = V7X_VMEM_BYTES * 3 // 4
VMEM_LIMIT_SSD = V7X_VMEM_BYTES * 7 // 8

EPS = 1e-6
CHUNK = 128
GRID_W = 64
RET_HEADS = 4
RET_DK = 128
ROPE_BASE = 10000.0
SSD_HEADS = 8
SSD_HEADDIM = 64
SSD_GROUPS = 2
SSD_STATE = 128
SSD_WIDTH = SSD_HEADS * SSD_HEADDIM
SSD_CONV = 5
SSD_PAIRS = SSD_WIDTH // LANES
MOE_GROUPS = 4
EXPERTS_PER_GROUP = 8
N_EXPERTS = MOE_GROUPS * EXPERTS_PER_GROUP
TOP_K = 2
CONV_HALO = SUBLANES

TM_PROJ = 512
TM_OUT = 512
TD_DISPATCH = TM_OUT
MB_EXPERT = 512
EXPERT_BLOCKS_PER_STEP = 4
ZERO_PIECE = MB_EXPERT // 2
TF_COMBINE = TM_OUT
SEG_ALIGN = BF16_TILE_ROWS
COMMON_STAGE_ROWS = TOP_K * TM_OUT + N_EXPERTS * 10
RET_UNROLL = 8
SSD_PREP_UNROLL = 8
SSD_OUT_UNROLL = 2
NEG_BIG = -1e30


def _silu(v):
    return v * jax.nn.sigmoid(v)


def _dot(a, b):
    return jnp.dot(a, b, preferred_element_type=F32)


def _dot_tn(a, b):
    return lax.dot_general(a, b, (((0,), (0,)), ((), ())), preferred_element_type=F32)


def _dot_nt(a, b):
    return lax.dot_general(a, b, (((1,), (1,)), ((), ())), preferred_element_type=F32)


def _mod_kernel(c_ref, w_ref, b_ref, o_ref):
    a = _silu(c_ref[...])
    w = w_ref[...]
    a_hi = a.astype(BF16)
    a_lo = (a - a_hi.astype(F32)).astype(BF16)
    w_hi = w.astype(BF16)
    w_lo = (w - w_hi.astype(F32)).astype(BF16)
    o_ref[...] = _dot(a_hi, w_hi) + _dot(a_lo, w_hi) + _dot(a_hi, w_lo) + b_ref[...]


def _modulation(c_all, w_mod, b_mod):
    rows, d = c_all.shape
    n = w_mod.shape[1]
    return pl.pallas_call(
        _mod_kernel,
        grid=(n // d,),
        in_specs=[
            pl.BlockSpec((rows, d), lambda j: (0, 0)),
            pl.BlockSpec((d, d), lambda j: (0, j)),
            pl.BlockSpec((1, d), lambda j: (0, j)),
        ],
        out_specs=pl.BlockSpec((rows, d), lambda j: (0, j)),
        out_shape=jax.ShapeDtypeStruct((rows, n), F32),
        name="modulation",
    )(c_all, w_mod, b_mod.reshape(1, n))


def _norm_mod(x, nw, sc, sh):
    ms = jnp.mean(x * x, axis=-1, keepdims=True)
    return (x * lax.rsqrt(ms + EPS)) * (nw * (1.0 + sc)) + sh


def _rope(t, cos, sin_signed, first_half):
    width = t.shape[-1]
    quarter = RET_DK // 4
    swapped = jnp.where(first_half, pltpu.roll(t, width - quarter, 1), pltpu.roll(t, quarter, 1))
    return t * cos + swapped * sin_signed


def _inproj_kernel(x_ref, sh_ref, sc_ref, nw_ref, wqk_ref, wvgz_ref, wxbc_ref, wdt_ref, cos_ref, sin_ref,
                   q_ref, k_ref, v_ref, g_ref, z_ref, xbc_ref, dt_ref):
    hb = _norm_mod(x_ref[0], nw_ref[...], sc_ref[0], sh_ref[0]).astype(BF16)
    rw = q_ref.shape[-1]
    qk = _dot(hb, wqk_ref[...])
    cos = cos_ref[...]
    sin = sin_ref[...]
    lane = lax.broadcasted_iota(jnp.int32, cos.shape, 1)
    first_half = (lane % (RET_DK // 2)) < (RET_DK // 4)
    q_ref[0] = _rope(qk[:, :rw], cos, sin, first_half).astype(BF16)
    k_ref[0] = (_rope(qk[:, rw:], cos, sin, first_half) * (RET_DK ** -0.5)).astype(BF16)
    vgz = _dot(hb, wvgz_ref[...])
    v_ref[0] = vgz[:, :rw].astype(BF16)
    g_ref[0] = vgz[:, rw:2 * rw].astype(BF16)
    z_ref[0] = vgz[:, 2 * rw:].astype(BF16)
    xbc_ref[0] = _dot(hb, wxbc_ref[...]).astype(BF16)
    dt_ref[0] = _dot(hb, wdt_ref[...])


def _inproj_ctx_kernel(x_ref, sh_ref, sc_ref, nw_ref, wk_ref, wv_ref, wxbc_ref, wdt_ref,
                       k_ref, v_ref, xbc_ref, dt_ref):
    hb = _norm_mod(x_ref[0], nw_ref[...], sc_ref[0], sh_ref[0]).astype(BF16)
    k_ref[0] = (_dot(hb, wk_ref[...]) * (RET_DK ** -0.5)).astype(BF16)
    v_ref[0] = _dot(hb, wv_ref[...]).astype(BF16)
    xbc_ref[0] = _dot(hb, wxbc_ref[...]).astype(BF16)
    dt_ref[0] = _dot(hb, wdt_ref[...])


def _const_spec(shape):
    nd = len(shape)
    return pl.BlockSpec(shape, lambda *_: (0,) * nd)


def _inproj(x, mod3, nw, wqk, wvgz, wxbc, wdt, cos_t, sin_t):
    b, L, d = x.shape
    tm = min(TM_PROJ, L)
    rw = wqk.shape[1] // 2
    tok = lambda w: pl.BlockSpec((1, tm, w), lambda i, j: (i, j, 0))
    out_bf = lambda w: jax.ShapeDtypeStruct((b, L, w), BF16)
    return pl.pallas_call(
        _inproj_kernel,
        grid=(b, L // tm),
        in_specs=[
            tok(d),
            pl.BlockSpec((1, 1, d), lambda i, j: (i, 0, 0)),
            pl.BlockSpec((1, 1, d), lambda i, j: (i, 0, 1)),
            _const_spec((1, d)),
            _const_spec(wqk.shape), _const_spec(wvgz.shape), _const_spec(wxbc.shape), _const_spec(wdt.shape),
            pl.BlockSpec((tm, rw), lambda i, j: (j, 0)),
            pl.BlockSpec((tm, rw), lambda i, j: (j, 0)),
        ],
        out_specs=[tok(rw), tok(rw), tok(rw), tok(rw), tok(rw), tok(wxbc.shape[1]), tok(LANES)],
        out_shape=[out_bf(rw), out_bf(rw), out_bf(rw), out_bf(rw), out_bf(rw), out_bf(wxbc.shape[1]),
                   jax.ShapeDtypeStruct((b, L, LANES), F32)],
        compiler_params=pltpu.CompilerParams(vmem_limit_bytes=VMEM_LIMIT),
        name="inproj",
    )(x, mod3, mod3, nw, wqk, wvgz, wxbc, wdt, cos_t, sin_t)


def _inproj_ctx(ctx, mod3, ctx_row, nw, wk, wv, wxbc, wdt):
    b, L, d = ctx.shape
    tm = min(TM_PROJ, L)
    rw = wk.shape[1]
    tok = lambda w: pl.BlockSpec((1, tm, w), lambda i, j: (i, j, 0))
    out_bf = lambda w: jax.ShapeDtypeStruct((b, L, w), BF16)
    return pl.pallas_call(
        _inproj_ctx_kernel,
        grid=(b, L // tm),
        in_specs=[
            tok(d),
            pl.BlockSpec((1, 1, d), lambda i, j: (ctx_row, 0, 0)),
            pl.BlockSpec((1, 1, d), lambda i, j: (ctx_row, 0, 1)),
            _const_spec((1, d)),
            _const_spec(wk.shape), _const_spec(wv.shape), _const_spec(wxbc.shape), _const_spec(wdt.shape),
        ],
        out_specs=[tok(rw), tok(rw), tok(wxbc.shape[1]), tok(LANES)],
        out_shape=[out_bf(rw), out_bf(rw), out_bf(wxbc.shape[1]), jax.ShapeDtypeStruct((b, L, LANES), F32)],
        compiler_params=pltpu.CompilerParams(vmem_limit_bytes=VMEM_LIMIT),
        name="inproj_ctx",
    )(ctx, mod3, mod3, nw, wk, wv, wxbc, wdt)


def _ssd_kernel(xbc_ref, z_ref, dt_ref, xbcc_ref, dtc_ref, cw_ref, cb_ref, dtb_ref, alog_ref, dsk_ref, nw_ref,
                y_ref,
                xpad, xpadc, u, uc, sf_scr, kb_scr, acum, ecum, dec_scr,
                arow_scr, erow_scr, dtrow_scr):
    L = xbc_ref.shape[1]
    Lc = xbcc_ref.shape[1]
    nch = L // CHUNK
    nchc = Lc // CHUNK
    win = CHUNK + 2 * CONV_HALO
    nconv = xbc_ref.shape[2]
    nh = SSD_HEADS

    def conv_pass(src_ref, pad_ref, dst_ref, n_chunks, length):
        zeros = jnp.zeros((CONV_HALO, nconv), F32)
        pad_ref[0:CONV_HALO, :] = zeros
        pad_ref[CONV_HALO + length:2 * CONV_HALO + length, :] = zeros
        pad_ref[CONV_HALO:CONV_HALO + length, :] = src_ref[0].astype(F32)

        def chunk(c, carry):
            base = pl.multiple_of(c * CHUNK, CHUNK)
            for cb_i in range(nconv // LANES):
                cols = slice(cb_i * LANES, (cb_i + 1) * LANES)
                w = pad_ref[pl.ds(base, win), cols]
                acc = cb_ref[:, cols] + w[CONV_HALO:CONV_HALO + CHUNK] * cw_ref[SSD_CONV // 2:SSD_CONV // 2 + 1, cols]
                for j in range(SSD_CONV):
                    if j == SSD_CONV // 2:
                        continue
                    shifted = pltpu.roll(w, (SSD_CONV // 2 - j) % win, 0)
                    acc = acc + shifted[CONV_HALO:CONV_HALO + CHUNK] * cw_ref[j:j + 1, cols]
                dst_ref[pl.ds(base, CHUNK), cols] = _silu(acc).astype(BF16)
            return carry

        lax.fori_loop(0, n_chunks, chunk, 0)

    conv_pass(xbcc_ref, xpadc, uc, nchc, Lc)
    conv_pass(xbc_ref, xpad, u, nch, L)

    a_neg = -jnp.exp(alog_ref[...])

    row_i = lax.broadcasted_iota(jnp.int32, (CHUNK, CHUNK), 0)
    col_i = lax.broadcasted_iota(jnp.int32, (CHUNK, CHUNK), 1)
    causal = col_i <= row_i
    lo_half = col_i < SSD_HEADDIM
    fwd_lane = col_i < nh
    head_of = lax.broadcasted_iota(jnp.int32, (CHUNK, SSD_WIDTH), 1) // SSD_HEADDIM
    src_col = lax.broadcasted_iota(jnp.int32, (CHUNK, SSD_WIDTH), 0)
    exp_f = (head_of == src_col).astype(BF16)
    exp_b = (head_of == src_col - nh).astype(BF16)
    exp_fb = jnp.concatenate([exp_f, exp_b], axis=1)

    def split3(v):
        hi = v.astype(BF16)
        r1 = v - hi.astype(F32)
        mid = r1.astype(BF16)
        return hi, mid, (r1 - mid.astype(F32)).astype(BF16)

    def times_onehot(v, m, passes=3):
        parts = split3(v)[:passes]
        acc = _dot(parts[0], m)
        for part in parts[1:]:
            acc = acc + _dot(part, m)
        return acc

    def colb(mat, r):
        return jnp.broadcast_to(mat[:, r:r + 1], (CHUNK, CHUNK))

    def pair_sel(a, b_):
        return jnp.where(lo_half, a, b_)

    gw = 2 * LANES

    def chunk_terms(u_ref, raw_ref, base):
        raw_t = (raw_ref[0, pl.ds(base, CHUNK), :] + dtb_ref[...]).T
        dt_row = jax.nn.softplus(raw_t[:2 * nh])
        dt = jnp.concatenate([dt_row, jnp.zeros((CHUNK - 2 * nh, CHUNK), F32)], axis=0).T
        da = dt * a_neg
        acol = da
        for step in (1, 2, 4, 8, 16, 32, 64):
            acol = acol + jnp.where(row_i >= step, pltpu.roll(acol, step, 0), 0.0)
        ecol = acol - da
        last = acol[CHUNK - 1:CHUNK, :]
        wgt = jnp.where(fwd_lane, jnp.exp(last - acol), jnp.exp(ecol)) * dt
        scale = jnp.where(fwd_lane, jnp.exp(acol), jnp.exp(last - ecol))
        wide = times_onehot(jnp.concatenate([wgt, scale], axis=0), exp_fb, passes=1)
        dec = times_onehot(jnp.broadcast_to(jnp.exp(last), (SUBLANES, LANES)), exp_fb)[0:1]
        xs = u_ref[pl.ds(base, CHUNK), 0:SSD_WIDTH].astype(F32)
        kmats = []
        for g in range(SSD_GROUPS):
            xw = jnp.concatenate([xs[:, g * gw:(g + 1) * gw] * wide[:CHUNK, g * gw:(g + 1) * gw],
                                  xs[:, g * gw:(g + 1) * gw] * wide[:CHUNK, SSD_WIDTH + g * gw:SSD_WIDTH + (g + 1) * gw]],
                                 axis=1).astype(BF16)
            bm = u_ref[pl.ds(base, CHUNK), SSD_WIDTH + g * SSD_STATE:SSD_WIDTH + (g + 1) * SSD_STATE]
            kmats.append(_dot_tn(bm, xw))
        return dt_row, acol, ecol, wide[CHUNK:], dec, kmats

    def advance(s, dec, kmats, backward):
        off = SSD_WIDTH if backward else 0
        koff = gw if backward else 0
        return [dec[:, off + g * gw:off + (g + 1) * gw] * s[g] + kmats[g][:, koff:koff + gw]
                for g in range(SSD_GROUPS)]

    ctx_terms = [chunk_terms(uc, dtc_ref, c * CHUNK) for c in range(nchc)]
    s_f0 = [jnp.zeros((SSD_STATE, gw), F32) for _ in range(SSD_GROUPS)]
    for c in range(nchc):
        s_f0 = advance(s_f0, ctx_terms[c][4], ctx_terms[c][5], False)
    s_b0 = [jnp.zeros((SSD_STATE, gw), F32) for _ in range(SSD_GROUPS)]
    for c in reversed(range(nchc)):
        s_b0 = advance(s_b0, ctx_terms[c][4], ctx_terms[c][5], True)

    def prep(c, carry):
        base = pl.multiple_of(c * CHUNK, CHUNK)
        dt_row, acol, ecol, scale, dec, kmats = chunk_terms(u, dt_ref, base)
        acum[pl.ds(base, CHUNK), :] = acol
        ecum[pl.ds(base, CHUNK), :] = ecol
        hrow = pl.ds(pl.multiple_of(c * 2 * nh, 2 * nh), 2 * nh)
        arow_scr[hrow, :] = acol.T[:2 * nh]
        erow_scr[hrow, :] = ecol.T[:2 * nh]
        dtrow_scr[hrow, :] = dt_row
        xpad[pl.ds(base, CHUNK), :] = scale
        dec_scr[pl.ds(pl.multiple_of(c * SUBLANES, SUBLANES), SUBLANES), :] = jnp.broadcast_to(dec, (SUBLANES, 2 * SSD_WIDTH))
        for g in range(SSD_GROUPS):
            sf_scr[c, g] = kmats[g][:, :gw]
            kb_scr[c, g] = kmats[g][:, gw:]
        return carry

    lax.fori_loop(0, nch, prep, 0, unroll=SSD_PREP_UNROLL)

    def chunk_dec(c):
        return dec_scr[pl.ds(pl.multiple_of(c * SUBLANES, SUBLANES), 1), :]

    def fwd(c, s_old):
        dec = chunk_dec(c)
        new = []
        for g in range(SSD_GROUPS):
            new.append(dec[:, g * gw:(g + 1) * gw] * s_old[g] + sf_scr[c, g])
            sf_scr[c, g] = s_old[g]
        return tuple(new)

    lax.fori_loop(0, nch, fwd, tuple(s_f0))

    def bwd(i, s_b):
        c = nch - 1 - i
        base = pl.multiple_of(c * CHUNK, CHUNK)
        acol = acum[pl.ds(base, CHUNK), :]
        ecol = ecum[pl.ds(base, CHUNK), :]
        hrow = pl.ds(pl.multiple_of(c * 2 * nh, 2 * nh), 2 * nh)
        arow = arow_scr[hrow, :]
        erow = erow_scr[hrow, :]
        dt_t = dtrow_scr[hrow, :]
        scale = xpad[pl.ds(base, CHUNK), :]
        ys = []
        for g in range(SSD_GROUPS):
            bm = u[pl.ds(base, CHUNK), SSD_WIDTH + g * SSD_STATE:SSD_WIDTH + (g + 1) * SSD_STATE]
            cm = u[pl.ds(base, CHUNK), SSD_WIDTH + (SSD_GROUPS + g) * SSD_STATE:SSD_WIDTH + (SSD_GROUPS + g + 1) * SSD_STATE]
            cbm = _dot_nt(cm, bm)
            cs_f = _dot(cm, sf_scr[c, g].astype(BF16))
            cs_b = _dot(cm, s_b[g].astype(BF16))
            for pp in range(SSD_PAIRS // SSD_GROUPS):
                p = g * (SSD_PAIRS // SSD_GROUPS) + pp
                xs_b = u[pl.ds(base, CHUNK), p * LANES:(p + 1) * LANES]
                y_h = []
                for hh in range(2):
                    r = 2 * p + hh
                    arg = jnp.where(causal, colb(acol, r) - arow[r:r + 1, :],
                                    erow[nh + r:nh + r + 1, :] - colb(ecol, nh + r))
                    coef = jnp.where(causal, dt_t[r:r + 1, :], dt_t[nh + r:nh + r + 1, :])
                    gm = (cbm * (jnp.exp(arg) * coef)).astype(BF16)
                    y_h.append(_dot(gm, xs_b))
                sl = slice(pp * LANES, (pp + 1) * LANES)
                wl = slice(p * LANES, (p + 1) * LANES)
                wlb = slice(SSD_WIDTH + p * LANES, SSD_WIDTH + (p + 1) * LANES)
                ys.append(pair_sel(y_h[0], y_h[1]) + cs_f[:, sl] * scale[:, wl] + cs_b[:, sl] * scale[:, wlb]
                          + dsk_ref[:, wl] * xs_b.astype(F32))
        y = jnp.concatenate(ys, axis=1)
        y = y * _silu(z_ref[0, pl.ds(base, CHUNK), :].astype(F32))
        ms = jnp.mean(y * y, axis=-1, keepdims=True)
        y_ref[0, pl.ds(base, CHUNK), :] = ((y * lax.rsqrt(ms + EPS)) * nw_ref[...]).astype(BF16)
        dec = chunk_dec(c)
        return tuple(dec[:, SSD_WIDTH + g * gw:SSD_WIDTH + (g + 1) * gw] * s_b[g] + kb_scr[c, g]
                     for g in range(SSD_GROUPS))

    lax.fori_loop(0, nch, bwd, tuple(s_b0), unroll=SSD_OUT_UNROLL)


def _ssd(xbc, z, dt, xbcc, dtc, conv_w8, conv_b, dt_bias, a_log, d_skip, norm_w):
    b, L, nconv = xbc.shape
    Lc = xbcc.shape[1]
    nch = L // CHUNK
    per_b = lambda n, w: pl.BlockSpec((1, n, w), lambda i: (i, 0, 0))
    return pl.pallas_call(
        _ssd_kernel,
        grid=(b,),
        in_specs=[
            per_b(L, nconv), per_b(L, SSD_WIDTH), per_b(L, LANES), per_b(Lc, nconv), per_b(Lc, LANES),
            _const_spec(conv_w8.shape), _const_spec(conv_b.shape), _const_spec(dt_bias.shape),
            _const_spec(a_log.shape), _const_spec(d_skip.shape), _const_spec(norm_w.shape),
        ],
        out_specs=per_b(L, SSD_WIDTH),
        out_shape=jax.ShapeDtypeStruct((b, L, SSD_WIDTH), BF16),
        scratch_shapes=[
            pltpu.VMEM((L + 2 * CONV_HALO, nconv), F32),
            pltpu.VMEM((Lc + 2 * CONV_HALO, nconv), F32),
            pltpu.VMEM((L, nconv), BF16),
            pltpu.VMEM((Lc, nconv), BF16),
            pltpu.VMEM((nch, SSD_GROUPS, SSD_STATE, 2 * LANES), F32),
            pltpu.VMEM((nch, SSD_GROUPS, SSD_STATE, 2 * LANES), F32),
            pltpu.VMEM((L, LANES), F32), pltpu.VMEM((L, LANES), F32),
            pltpu.VMEM((nch * SUBLANES, 2 * SSD_WIDTH), F32),
            pltpu.VMEM((nch * 2 * SSD_HEADS, CHUNK), F32), pltpu.VMEM((nch * 2 * SSD_HEADS, CHUNK), F32),
            pltpu.VMEM((nch * 2 * SSD_HEADS, CHUNK), F32),
        ],
        compiler_params=pltpu.CompilerParams(vmem_limit_bytes=VMEM_LIMIT_SSD),
        name="ssd",
    )(xbc, z, dt, xbcc, dtc, conv_w8, conv_b, dt_bias, a_log, d_skip, norm_w)


def _ret_kernel(q_ref, k_ref, v_ref, g_ref, kc_ref, vc_ref, df_ref, db_ref, gn_ref, y_ref, sf_scr):
    L = q_ref.shape[1]
    Lc = kc_ref.shape[1]
    nch = L // CHUNK
    dk = RET_DK
    row_i = lax.broadcasted_iota(jnp.int32, (CHUNK, dk), 0).astype(F32)
    col_i = lax.broadcasted_iota(jnp.int32, (CHUNK, dk), 1).astype(F32)
    rel = row_i - col_i
    crow = lax.broadcasted_iota(jnp.int32, (Lc, dk), 0).astype(F32)

    heads = []
    s_f0 = []
    s_b0 = []
    for h in range(RET_HEADS):
        cols = slice(h * dk, (h + 1) * dk)
        lg_f = -jnp.exp(df_ref[:, cols])
        lg_b = -jnp.exp(db_ref[:, cols])
        heads.append(dict(
            cols=cols,
            dmat=jnp.where(rel >= 0, jnp.exp(jnp.maximum(rel, 0.0) * lg_f), jnp.exp(jnp.maximum(-rel, 0.0) * lg_b)),
            dq_f=jnp.exp((row_i + 1.0) * lg_f),
            dq_b=jnp.exp((CHUNK - row_i) * lg_b),
            dk_f=jnp.exp((CHUNK - 1.0 - row_i) * lg_f),
            dk_b=jnp.exp(row_i * lg_b),
            dc_f=jnp.exp(CHUNK * lg_f),
            dc_b=jnp.exp(CHUNK * lg_b),
        ))
        kc = kc_ref[0, :, cols].astype(F32)
        vc = vc_ref[0, :, cols]
        s_f0.append(_dot_tn((kc * jnp.exp((Lc - 1.0 - crow) * lg_f)).astype(BF16), vc))
        s_b0.append(_dot_tn((kc * jnp.exp(crow * lg_b)).astype(BF16), vc))

    def fwd(c, s_f):
        base = pl.multiple_of(c * CHUNK, CHUNK)
        new = []
        for h, hd in enumerate(heads):
            sf_scr[c, h] = s_f[h]
            kk = k_ref[0, pl.ds(base, CHUNK), hd["cols"]].astype(F32)
            vv = v_ref[0, pl.ds(base, CHUNK), hd["cols"]]
            new.append(hd["dc_f"] * s_f[h] + _dot_tn((kk * hd["dk_f"]).astype(BF16), vv))
        return tuple(new)

    lax.fori_loop(0, nch, fwd, tuple(s_f0), unroll=RET_UNROLL)

    def bwd(i, s_bs):
        c = nch - 1 - i
        base = pl.multiple_of(c * CHUNK, CHUNK)
        new = []
        for h, hd in enumerate(heads):
            qq = q_ref[0, pl.ds(base, CHUNK), hd["cols"]]
            kk = k_ref[0, pl.ds(base, CHUNK), hd["cols"]]
            vv = v_ref[0, pl.ds(base, CHUNK), hd["cols"]]
            s_b = s_bs[h]
            scores = (_dot_nt(qq, kk) * hd["dmat"]).astype(BF16)
            y = (_dot(scores, vv)
                 + _dot(qq, sf_scr[c, h].astype(BF16)) * hd["dq_f"]
                 + _dot(qq, s_b.astype(BF16)) * hd["dq_b"])
            mu = jnp.mean(y, axis=-1, keepdims=True)
            yc = y - mu
            var = jnp.mean(yc * yc, axis=-1, keepdims=True)
            yn = (yc * lax.rsqrt(var + EPS)) * gn_ref[:, hd["cols"]]
            gate = _silu(g_ref[0, pl.ds(base, CHUNK), hd["cols"]].astype(F32))
            y_ref[0, pl.ds(base, CHUNK), hd["cols"]] = (yn * gate).astype(BF16)
            new.append(hd["dc_b"] * s_b + _dot_tn((kk.astype(F32) * hd["dk_b"]).astype(BF16), vv))
        return tuple(new)

    lax.fori_loop(0, nch, bwd, tuple(s_b0), unroll=RET_UNROLL)


def _retention(q, k, v, g, kc, vc, decay_f, decay_b, gn_w):
    b, L, w = q.shape
    Lc = kc.shape[1]
    nch = L // CHUNK
    per_b = lambda n: pl.BlockSpec((1, n, w), lambda i: (i, 0, 0))
    return pl.pallas_call(
        _ret_kernel,
        grid=(b,),
        in_specs=[per_b(L), per_b(L), per_b(L), per_b(L), per_b(Lc), per_b(Lc),
                  _const_spec((1, w)), _const_spec((1, w)), _const_spec((1, w))],
        out_specs=per_b(L),
        out_shape=jax.ShapeDtypeStruct((b, L, w), BF16),
        scratch_shapes=[
            pltpu.VMEM((nch, RET_HEADS, RET_DK, RET_DK), F32),
        ],
        compiler_params=pltpu.CompilerParams(vmem_limit_bytes=VMEM_LIMIT),
        name="retention",
    )(q, k, v, g, kc, vc, decay_f, decay_b, gn_w)


def _outproj_router_kernel(yr_ref, ys_ref, x_ref, g1_ref, sh2_ref, sc2_ref, npost_ref, npre_ref,
                           wor_ref, wos_ref, wr_ref, br_ref, tri_ref,
                           x1_ref, h2_ref, route_ref, slots_ref, seg_ref,
                           wcat):
    i = pl.program_id(0)

    @pl.when(i == 0)
    def _():
        wr = wr_ref[...]
        hi = wr.astype(BF16)
        wcat[:, :LANES] = hi
        wcat[:, LANES:] = (wr - hi.astype(F32)).astype(BF16)

    _route_tile(yr_ref, ys_ref, x_ref, g1_ref, sh2_ref, sc2_ref, npost_ref, npre_ref, wor_ref, wos_ref, br_ref,
                tri_ref, x1_ref, h2_ref, route_ref, slots_ref, seg_ref, wcat)


def _route_tile(yr_ref, ys_ref, x_ref, g1_ref, sh2_ref, sc2_ref, npost_ref, npre_ref, wor_ref, wos_ref, br_ref,
                tri_ref, x1_ref, h2_ref, route_ref, slots_ref, seg_ref, wcat):
    tm = x_ref.shape[0]
    rows = slice(0, tm)
    y = _dot(yr_ref[rows, :], wor_ref[...]) + _dot(ys_ref[rows, :], wos_ref[...])
    ms = jnp.mean(y * y, axis=-1, keepdims=True)
    x1 = x_ref[rows, :] + (y * lax.rsqrt(ms + EPS)) * (g1_ref[0] * npost_ref[...])
    x1_ref[rows, :] = x1
    h2 = _norm_mod(x1, npre_ref[...], sc2_ref[0], sh2_ref[0])
    h2_ref[rows, :] = h2.astype(BF16)

    h_hi = h2.astype(BF16)
    h_lo = (h2 - h_hi.astype(F32)).astype(BF16)
    both = _dot(h_hi, wcat[...])
    lg = both[:, :LANES] + both[:, LANES:] + _dot(h_lo, wcat[:, :LANES]) + br_ref[...]

    lane = lax.broadcasted_iota(jnp.int32, (tm, LANES), 1)
    lane_f = lane.astype(F32)
    is_grp = (lane >= N_EXPERTS) & (lane < N_EXPERTS + MOE_GROUPS)
    gl = jnp.where(is_grp, lg, NEG_BIG)
    mg = jnp.max(gl, axis=-1, keepdims=True)
    grp_lane = jnp.min(jnp.where(gl == mg, lane_f, 1e9), axis=-1, keepdims=True)
    p_g = 1.0 / jnp.sum(jnp.where(is_grp, jnp.exp(gl - mg), 0.0), axis=-1, keepdims=True)
    first = (grp_lane - N_EXPERTS) * EXPERTS_PER_GROUP
    in_grp = (lane_f >= first) & (lane_f < first + EXPERTS_PER_GROUP)
    el = jnp.where(in_grp, lg, NEG_BIG)
    t1 = jnp.max(el, axis=-1, keepdims=True)
    i1 = jnp.min(jnp.where(el == t1, lane_f, 1e9), axis=-1, keepdims=True)
    el2 = jnp.where(lane_f == i1, NEG_BIG, el)
    t2 = jnp.max(el2, axis=-1, keepdims=True)
    i2 = jnp.min(jnp.where(el2 == t2, lane_f, 1e9), axis=-1, keepdims=True)
    s = jnp.exp(t2 - t1)
    w1 = p_g / (1.0 + s)
    w2 = p_g * s / (1.0 + s)

    oh1 = (lane_f == i1)
    oh2 = (lane_f == i2)
    ohf = jnp.where(oh1 | oh2, 1.0, 0.0)
    before = _dot(tri_ref[...], ohf.astype(BF16))
    cnt = jnp.sum(ohf, axis=0, keepdims=True)
    seg = jnp.floor((cnt + (SEG_ALIGN - 1.0)) * (1.0 / SEG_ALIGN)) * SEG_ALIGN
    e_row = lax.broadcasted_iota(jnp.int32, (LANES, LANES), 0)
    e_col = lax.broadcasted_iota(jnp.int32, (LANES, LANES), 1)
    earlier = (e_row < e_col).astype(BF16)
    seg_off = _dot(jnp.broadcast_to(seg, (SUBLANES, LANES)).astype(BF16), earlier)[0:1]
    where_to = before + seg_off
    lpos1 = jnp.sum(jnp.where(oh1, where_to, 0.0), axis=-1, keepdims=True)
    lpos2 = jnp.sum(jnp.where(oh2, where_to, 0.0), axis=-1, keepdims=True)

    cols = [w1, w2, lpos1, lpos2]
    for wk in (w1, w2):
        hi = wk.astype(BF16).astype(F32)
        cols += [hi, wk - hi]
    cols.append(jnp.ones_like(w1))
    packed = jnp.zeros((tm, LANES), F32)
    for k, col in enumerate(cols):
        packed = jnp.where(lane == k, col, packed)
    route_ref[rows, :] = packed

    row = lax.broadcasted_iota(jnp.int32, (tm, LANES), 0)
    on_diag = (row % LANES) == lane
    per = tm // LANES
    for qi, col in enumerate((lpos1, lpos2)):
        picked = jnp.where(on_diag, col, 0.0)
        dense = jnp.sum(picked.reshape(per, LANES, LANES), axis=1).astype(jnp.int32)
        slots_ref[0, qi * per:(qi + 1) * per, :] = dense
    tbl_row = lax.broadcasted_iota(jnp.int32, (SUBLANES, LANES), 0)
    seg_ref[0] = jnp.where(tbl_row == 0, seg, jnp.where(tbl_row == 1, seg_off, 0.0)).astype(jnp.int32)


def _outproj_router(yr, ys, x2, mod3, npost, npre, wo_r, wo_s, w_router, b_router, seq_len):
    T, d = x2.shape
    tm = TM_OUT
    per_seq = seq_len // tm
    rw = yr.shape[1]
    tri = (jnp.arange(tm)[:, None] > jnp.arange(tm)[None, :]).astype(BF16)
    tok = lambda w: pl.BlockSpec((tm, w), lambda i: (i, 0))
    modv = lambda k: pl.BlockSpec((1, 1, d), lambda i: (i // per_seq, 0, k))
    tile3 = lambda r: pl.BlockSpec((1, r, LANES), lambda i: (i, 0, 0))
    slot_rows = TOP_K * (tm // LANES)
    return pl.pallas_call(
        _outproj_router_kernel,
        grid=(T // tm,),
        in_specs=[
            tok(rw), tok(rw), tok(d), modv(2), modv(3), modv(4),
            _const_spec((1, d)), _const_spec((1, d)),
            _const_spec(wo_r.shape), _const_spec(wo_s.shape), _const_spec(w_router.shape), _const_spec((1, LANES)),
            _const_spec((tm, tm)),
        ],
        out_specs=[tok(d), tok(d), tok(LANES), tile3(slot_rows), tile3(SUBLANES)],
        out_shape=[jax.ShapeDtypeStruct((T, d), F32), jax.ShapeDtypeStruct((T, d), BF16),
                   jax.ShapeDtypeStruct((T, LANES), F32),
                   jax.ShapeDtypeStruct((T // tm, slot_rows, LANES), jnp.int32),
                   jax.ShapeDtypeStruct((T // tm, SUBLANES, LANES), jnp.int32)],
        scratch_shapes=[pltpu.VMEM((d, 2 * LANES), BF16)],
        compiler_params=pltpu.CompilerParams(dimension_semantics=("arbitrary",),
                                             vmem_limit_bytes=VMEM_LIMIT),
        name="outproj_router",
    )(yr, ys, x2, mod3, mod3, mod3, npost, npre, wo_r, wo_s, w_router, b_router, tri)


def _expert_kernel(be_ref, first_ref, slot_ref, next_ref, nused_ref, xs_ref, wg_hbm, wu_hbm, wd_hbm, y_ref,
                   wg_f, wu_f, wd_f, wg_b, wu_b, wd_b, sem):
    def fetch(e, s):
        return [pltpu.make_async_copy(src.at[e], dst.at[s], sem.at[s, k])
                for k, (src, dst) in enumerate(((wg_hbm, wg_f), (wu_hbm, wu_f), (wd_hbm, wd_f)))]

    @pl.when(pl.program_id(0) == 0)
    def _():
        for c in fetch(be_ref[0], slot_ref[0]):
            c.start()

    for sb in range(EXPERT_BLOCKS_PER_STEP):
        _expert_block(pl.program_id(0) * EXPERT_BLOCKS_PER_STEP + sb, slice(sb * MB_EXPERT, (sb + 1) * MB_EXPERT),
                      fetch, be_ref, first_ref, slot_ref, next_ref, nused_ref, xs_ref, y_ref,
                      wg_f, wu_f, wd_f, wg_b, wu_b, wd_b)


def _expert_block(i, rows, fetch, be_ref, first_ref, slot_ref, next_ref, nused_ref, xs_ref, y_ref,
                  wg_f, wu_f, wd_f, wg_b, wu_b, wd_b):
    @pl.when(first_ref[i] == 1)
    def _():
        s = slot_ref[i]

        @pl.when(next_ref[i] >= 0)
        def _():
            for c in fetch(next_ref[i], 1 - s):
                c.start()

        for c in fetch(be_ref[i], s):
            c.wait()
        wg_b[...] = wg_f[s].astype(BF16)
        wu_b[...] = wu_f[s].astype(BF16)
        wd_b[...] = wd_f[s].astype(BF16)

    @pl.when(i < nused_ref[0])
    def _():
        d = y_ref.shape[1]
        side = xs_ref[rows, d:d + LANES].astype(F32)
        second = side[:, 8:9] == 2.0
        unscale = jnp.where(second, 0.5, 1.0)
        weight = jnp.where(second, side[:, 6:7] + side[:, 7:8], side[:, 4:5] + side[:, 5:6]) * unscale
        xb = xs_ref[rows, 0:d] * unscale.astype(BF16)
        hid = (_silu(_dot(xb, wg_b[...])) * _dot(xb, wu_b[...])).astype(BF16)
        y_ref[rows, :] = (_dot(hid, wd_b[...]) * weight).astype(BF16)

    @pl.when(i >= nused_ref[0])
    def _():
        y_ref[rows, :] = jnp.zeros((rows.stop - rows.start, y_ref.shape[1]), y_ref.dtype)


def _expert_plan(padded, pad_end, n_blocks, mb):
    n_used = (pad_end[-1:] // mb).astype(jnp.int32)
    blk_start = jnp.arange(n_blocks, dtype=jnp.int32) * mb
    experts = jnp.arange(N_EXPERTS, dtype=jnp.int32)
    blk_expert = jnp.minimum(jnp.sum((pad_end[None, :] <= blk_start[:, None]).astype(jnp.int32), axis=1),
                             N_EXPERTS - 1)
    prev = jnp.concatenate([jnp.full((1,), -1, jnp.int32), blk_expert[:-1]])
    first = ((blk_start < pad_end[-1]) & (blk_expert != prev)).astype(jnp.int32)
    slot = (jnp.cumsum(first) - 1) % 2
    later = jnp.where((padded > 0)[None, :] & (experts[None, :] > experts[:, None]), experts[None, :], N_EXPERTS)
    next_nonempty = jnp.min(later, axis=1)
    next_nonempty = jnp.where(next_nonempty == N_EXPERTS, -1, next_nonempty)
    nxt = jnp.sum(jnp.where(blk_expert[:, None] == experts[None, :], next_nonempty[None, :], 0), axis=1)
    return [a.astype(jnp.int32) for a in (blk_expert, first, slot, nxt, n_used)]


def _experts(plan, xs, w_gate, w_up, w_down):
    cap, dp = xs.shape
    n_exp, d, de = w_gate.shape
    step_rows = MB_EXPERT * EXPERT_BLOCKS_PER_STEP
    assert dp == d + LANES and cap % step_rows == 0
    grid_spec = pltpu.PrefetchScalarGridSpec(
        num_scalar_prefetch=len(plan),
        grid=(cap // step_rows,),
        in_specs=[
            pl.BlockSpec((step_rows, dp), lambda i, be, fi, sl, nx, nu: (
                jnp.minimum(i, (nu[0] - 1) // EXPERT_BLOCKS_PER_STEP), 0)),
            pl.BlockSpec(memory_space=pl.ANY), pl.BlockSpec(memory_space=pl.ANY), pl.BlockSpec(memory_space=pl.ANY),
        ],
        out_specs=pl.BlockSpec((step_rows, d), lambda i, be, fi, sl, nx, nu: (i, 0)),
        scratch_shapes=[pltpu.VMEM((2, d, de), F32), pltpu.VMEM((2, d, de), F32), pltpu.VMEM((2, de, d), F32),
                        pltpu.VMEM((d, de), BF16), pltpu.VMEM((d, de), BF16), pltpu.VMEM((de, d), BF16),
                        pltpu.SemaphoreType.DMA((2, 3))],
    )
    return pl.pallas_call(
        _expert_kernel,
        grid_spec=grid_spec,
        out_shape=jax.ShapeDtypeStruct((cap, d), BF16),
        compiler_params=pltpu.CompilerParams(dimension_semantics=("arbitrary",),
                                             vmem_limit_bytes=VMEM_LIMIT),
        name="experts",
    )(*plan, xs, w_gate, w_up, w_down)


def _segment_pieces(tile, segrow_ref, seglen_ref, segoff_ref, act):
    def per_expert(e, carry):
        idx = tile * N_EXPERTS + e
        g0 = segrow_ref[idx]
        l0 = segoff_ref[idx]

        def piece(j, c2):
            act(pl.multiple_of(l0 + j * SEG_ALIGN, SEG_ALIGN), pl.multiple_of(g0 + j * SEG_ALIGN, SEG_ALIGN))
            return c2

        lax.fori_loop(0, seglen_ref[idx], piece, 0)
        return carry

    lax.fori_loop(0, N_EXPERTS, per_expert, 0)


def _wait_rows(total, row_copy):
    size = SEG_ALIGN
    while size <= _stage_rows(max(TD_DISPATCH, TF_COMBINE)):
        @pl.when((total & size) != 0)
        def _(size=size):
            row_copy(size).wait()
        size *= 2


def _for_tile_rows(tile_rows, max_rows, body):
    @pl.when(tile_rows <= COMMON_STAGE_ROWS)
    def _():
        body(COMMON_STAGE_ROWS)

    @pl.when(tile_rows > COMMON_STAGE_ROWS)
    def _():
        body(max_rows)


def _stage_rows(tile_tokens):
    return TOP_K * tile_tokens + N_EXPERTS * SEG_ALIGN


def _dispatch_seg_kernel(pad_end_ref, zero_from_ref, segrow_ref, seglen_ref, segoff_ref, tilerows_ref,
                         lpos_ref, route_ref, h_ref, xs_hbm, zbuf, stage, sem, zsem):
    i = pl.program_id(0)
    nt = pl.num_programs(0)
    td, d = h_ref.shape
    sr = stage.shape[1]
    per = td // LANES

    def zero_fills(act):
        def fill(row, n):
            act(pltpu.make_async_copy(zbuf.at[pl.ds(0, n)], xs_hbm.at[pl.ds(pl.multiple_of(row, SEG_ALIGN), n)], zsem))

        def region(e, carry):
            row = zero_from_ref[e]
            left = pad_end_ref[e] - row
            size = ZERO_PIECE
            while size >= SEG_ALIGN:
                take = (left & size) != 0

                @pl.when(take)
                def _(row=row, size=size):
                    fill(row, size)

                row = row + jnp.where(take, size, 0)
                size //= 2
            return carry

        def tail(p, carry):
            @pl.when(p * ZERO_PIECE >= pad_end_ref[N_EXPERTS - 1])
            def _():
                fill(p * ZERO_PIECE, ZERO_PIECE)
            return carry

        lax.fori_loop(0, N_EXPERTS, region, 0)
        lax.fori_loop(0, xs_hbm.shape[0] // ZERO_PIECE, tail, 0)

    @pl.when(i == 0)
    def _():
        zbuf[...] = jnp.zeros_like(zbuf)
        zero_fills(lambda c: c.start())

    def shipped(tile):
        _wait_rows(tilerows_ref[tile], lambda n: pltpu.make_async_copy(
            stage.at[tile % 2, pl.ds(0, n)], xs_hbm.at[pl.ds(0, n)], sem.at[tile % 2]))

    @pl.when(i >= 2)
    def _():
        shipped(i - 2)

    slot = i % 2

    def sort_rows(n_rows):
        srow = lax.broadcasted_iota(jnp.int32, (n_rows, LANES), 0)
        place = jnp.concatenate(
            [jnp.where(srow == lpos_ref[0, cb:cb + 1, :], 1.0,
                       jnp.where(srow == lpos_ref[0, per + cb:per + cb + 1, :], 2.0, 0.0)) for cb in range(per)],
            axis=1).astype(BF16)
        stage[slot, 0:n_rows, :] = _dot(place, jnp.concatenate([h_ref[...], route_ref[...].astype(BF16)],
                                                               axis=1)).astype(BF16)

    _for_tile_rows(tilerows_ref[i], sr, sort_rows)
    _segment_pieces(i, segrow_ref, seglen_ref, segoff_ref, lambda lr, gr: pltpu.make_async_copy(
        stage.at[slot, pl.ds(lr, SEG_ALIGN)], xs_hbm.at[pl.ds(gr, SEG_ALIGN)], sem.at[slot]).start())

    @pl.when(i == nt - 1)
    def _():
        @pl.when(i >= 1)
        def _():
            shipped(i - 1)
        shipped(i)
        zero_fills(lambda c: c.wait())


def _dispatch_seg(plan, lpos, route, h2, cap):
    T, d = h2.shape
    td = TD_DISPATCH
    nt = T // td
    width = d + LANES
    sr = _stage_rows(td)
    n_pre = len(plan)
    grid_spec = pltpu.PrefetchScalarGridSpec(
        num_scalar_prefetch=n_pre,
        grid=(nt,),
        in_specs=[
            pl.BlockSpec((1, lpos.shape[1], LANES), lambda i, *_: (i, 0, 0)),
            pl.BlockSpec((td, LANES), lambda i, *_: (i, 0)),
            pl.BlockSpec((td, d), lambda i, *_: (i, 0)),
        ],
        out_specs=pl.BlockSpec(memory_space=pl.ANY),
        scratch_shapes=[pltpu.VMEM((ZERO_PIECE, width), BF16), pltpu.VMEM((2, sr, width), BF16),
                        pltpu.SemaphoreType.DMA((2,)), pltpu.SemaphoreType.DMA(())],
    )
    return pl.pallas_call(
        _dispatch_seg_kernel,
        grid_spec=grid_spec,
        out_shape=jax.ShapeDtypeStruct((cap, width), BF16),
        compiler_params=pltpu.CompilerParams(dimension_semantics=("arbitrary",), vmem_limit_bytes=VMEM_LIMIT),
        name="dispatch",
    )(*plan, lpos, route, h2)


def _combine_seg_kernel(segrow_ref, seglen_ref, segoff_ref, tilerows_ref, route_ref, x1_ref, g2_ref, nw_ref,
                        yb_hbm, o_ref, stage, sem):
    i = pl.program_id(0)
    nt = pl.num_programs(0)
    tf = x1_ref.shape[0]
    sr = stage.shape[1]

    def fetch(tile, slot):
        _segment_pieces(tile, segrow_ref, seglen_ref, segoff_ref, lambda lr, gr: pltpu.make_async_copy(
            yb_hbm.at[pl.ds(gr, SEG_ALIGN)], stage.at[slot, pl.ds(lr, SEG_ALIGN)], sem.at[slot]).start())

    @pl.when(i == 0)
    def _():
        stage[...] = jnp.zeros_like(stage)
        fetch(i, 0)

    for slot in range(2):
        @pl.when((i + 1 < nt) & (i % 2 != slot))
        def _(slot=slot):
            fetch(i + 1, slot)

    _wait_rows(tilerows_ref[i], lambda n: pltpu.make_async_copy(
        yb_hbm.at[pl.ds(0, n)], stage.at[i % 2, pl.ds(0, n)], sem.at[i % 2]))

    local_row = route_ref[:, 2:2 + TOP_K].astype(jnp.int32)

    def unsort_rows(n_rows):
        scol = lax.broadcasted_iota(jnp.int32, (tf, n_rows), 1)
        pick = jnp.where((scol == local_row[:, 0:1]) | (scol == local_row[:, 1:2]), 1.0, 0.0).astype(BF16)
        out = _dot(pick, stage[i % 2, 0:n_rows, :])
        ms = jnp.mean(out * out, axis=-1, keepdims=True)
        o_ref[...] = x1_ref[...] + g2_ref[0] * ((out * lax.rsqrt(ms + EPS)) * nw_ref[...])

    _for_tile_rows(tilerows_ref[i], sr, unsort_rows)


def _combine_seg(seg_plan, route, x1, mod3, nw, yb, seq_len):
    T, d = x1.shape
    tf = TF_COMBINE
    nt = T // tf
    per_seq = seq_len // tf
    grid_spec = pltpu.PrefetchScalarGridSpec(
        num_scalar_prefetch=len(seg_plan),
        grid=(nt,),
        in_specs=[
            pl.BlockSpec((tf, LANES), lambda i, *_: (i, 0)),
            pl.BlockSpec((tf, d), lambda i, *_: (i, 0)),
            pl.BlockSpec((1, 1, d), lambda i, *_: (i // per_seq, 0, 5)),
            pl.BlockSpec((1, d), lambda i, *_: (0, 0)),
            pl.BlockSpec(memory_space=pl.ANY),
        ],
        out_specs=pl.BlockSpec((tf, d), lambda i, *_: (i, 0)),
        scratch_shapes=[pltpu.VMEM((2, _stage_rows(tf), d), yb.dtype), pltpu.SemaphoreType.DMA((2,))],
    )
    return pl.pallas_call(
        _combine_seg_kernel,
        grid_spec=grid_spec,
        out_shape=jax.ShapeDtypeStruct((T, d), F32),
        compiler_params=pltpu.CompilerParams(dimension_semantics=("arbitrary",), vmem_limit_bytes=VMEM_LIMIT),
        name="combine",
    )(*seg_plan, route, x1, mod3, nw, yb)


def _rope_tables(L, n_heads):
    quarter = RET_DK // 4
    freqs = ROPE_BASE ** (-jnp.arange(quarter, dtype=F32) / quarter)
    t = jnp.arange(L)
    ang_r = (t // GRID_W).astype(F32)[:, None] * freqs
    ang_c = (t % GRID_W).astype(F32)[:, None] * freqs
    cos = jnp.concatenate([jnp.cos(ang_r)] * 2 + [jnp.cos(ang_c)] * 2, axis=-1)
    sin = jnp.concatenate([-jnp.sin(ang_r), jnp.sin(ang_r), -jnp.sin(ang_c), jnp.sin(ang_c)], axis=-1)
    return jnp.tile(cos, (1, n_heads)), jnp.tile(sin, (1, n_heads))


def _lane_pad(v, width=LANES):
    return jnp.pad(v, [(0, 0)] * (v.ndim - 1) + [(0, width - v.shape[-1])])


def kernel(x, c, ctx, c_ctx, w_mod, b_mod, norm_pre_mix, norm_post_mix, norm_pre_ffn, norm_post_ffn, w_in, w_out, ret_decay_f, ret_decay_b, ret_gn_w, ssd_conv_w, ssd_conv_b, ssd_dt_bias_f, ssd_dt_bias_b, ssd_a_log_f, ssd_a_log_b, ssd_d, ssd_norm_w, moe_w_rg, moe_b_rg, moe_w_re, moe_b_re, moe_w_gate, moe_w_up, moe_w_down):
    b, L, d = x.shape
    assert w_mod.shape[0] == 1, "single layer: context outputs are never needed"
    assert TM_OUT == TD_DISPATCH == TF_COMBINE, "router, dispatch and combine share one slot-row layout"
    rw = RET_HEADS * RET_DK
    nconv = SSD_WIDTH + 2 * SSD_GROUPS * SSD_STATE
    T = b * L

    mod_rows = -(-(b + 1) // SUBLANES) * SUBLANES
    c_all = jnp.zeros((mod_rows, d), F32).at[:b].set(c).at[b].set(c_ctx)
    mod3 = _modulation(c_all, w_mod[0], b_mod[0]).reshape(mod_rows, 1, 6 * d)

    wi = w_in[0]
    o = 0
    wq = wi[:, o:o + rw]; o += rw
    wk = wi[:, o:o + rw]; o += rw
    wv = wi[:, o:o + rw]; o += rw
    wg = wi[:, o:o + rw]; o += rw
    wz = wi[:, o:o + SSD_WIDTH]; o += SSD_WIDTH
    wxbc = wi[:, o:o + nconv].astype(BF16); o += nconv
    wdt = _lane_pad(wi[:, o:o + 2 * SSD_HEADS]).astype(BF16)
    wqk = jnp.concatenate([wq, wk], axis=1).astype(BF16)
    wvgz = jnp.concatenate([wv, wg, wz], axis=1).astype(BF16)
    cos_t, sin_t = _rope_tables(L, RET_HEADS)
    nw1 = norm_pre_mix[0].reshape(1, d)

    q, k, v, g, z, xbc, dt = _inproj(x, mod3, nw1, wqk, wvgz, wxbc, wdt, cos_t, sin_t)
    kc, vc, xbcc, dtc = _inproj_ctx(ctx, mod3, b, nw1, wk.astype(BF16), wv.astype(BF16), wxbc, wdt)

    conv_w8 = jnp.pad(ssd_conv_w[0], ((0, SUBLANES - SSD_CONV), (0, 0)))
    dt_bias = _lane_pad(jnp.concatenate([ssd_dt_bias_f[0], ssd_dt_bias_b[0]])[None, :])
    a_log = _lane_pad(jnp.concatenate([ssd_a_log_f[0], ssd_a_log_b[0]])[None, :])
    d_skip = jnp.repeat(ssd_d[0], SSD_HEADDIM)[None, :]
    ys = _ssd(xbc, z, dt, xbcc, dtc, conv_w8, ssd_conv_b[0][None, :], dt_bias, a_log, d_skip,
              ssd_norm_w[0][None, :])

    yr = _retention(q, k, v, g, kc, vc,
                    jnp.repeat(ret_decay_f[0], RET_DK)[None, :], jnp.repeat(ret_decay_b[0], RET_DK)[None, :],
                    ret_gn_w[0][None, :])

    wo = w_out[0].astype(BF16)
    w_router = _lane_pad(jnp.concatenate(
        [jnp.transpose(moe_w_re[0], (1, 0, 2)).reshape(d, N_EXPERTS), moe_w_rg[0]], axis=1))
    b_router = _lane_pad(jnp.concatenate([moe_b_re[0].reshape(-1), moe_b_rg[0]])[None, :])
    x1, h2, route, lpos, seg = _outproj_router(
        yr.reshape(T, rw), ys.reshape(T, SSD_WIDTH), x.reshape(T, d), mod3,
        norm_post_mix[0][None, :], norm_pre_ffn[0][None, :], wo[:rw], wo[rw:], w_router, b_router, L)

    mb = MB_EXPERT
    nt = T // TM_OUT
    n_blocks = -(-(T * TOP_K + nt * N_EXPERTS * (SEG_ALIGN - 1) + N_EXPERTS * (mb - 1)) // mb)
    n_blocks = -(-n_blocks // EXPERT_BLOCKS_PER_STEP) * EXPERT_BLOCKS_PER_STEP
    seg_len = seg[:, 0, :N_EXPERTS]
    seg_off = seg[:, 1, :N_EXPERTS]
    used = jnp.sum(seg_len, axis=0)
    padded = (used + mb - 1) // mb * mb
    pad_end = jnp.cumsum(padded)
    pad_start = pad_end - padded
    seg_row = pad_start[None, :] + jnp.cumsum(seg_len, axis=0) - seg_len
    seg_plan = [a.reshape(-1).astype(jnp.int32)
                for a in (seg_row, seg_len // SEG_ALIGN, seg_off, jnp.sum(seg_len, axis=1))]
    zero_from = (pad_start + used).astype(jnp.int32)

    xs = _dispatch_seg([pad_end.astype(jnp.int32), zero_from] + seg_plan, lpos, route, h2, n_blocks * mb)
    yb = _experts(_expert_plan(padded, pad_end, n_blocks, mb), xs, moe_w_gate[0], moe_w_up[0], moe_w_down[0])
    out = _combine_seg(seg_plan, route, x1, mod3, norm_post_ffn[0][None, :], yb, L)
    return out.reshape(b, L, d)
```

```python
import jax
import jax.numpy as jnp
from jax import lax
from jax.experimental import pallas as pl
from jax.experimental.pallas import tpu as pltpu

F32 = jnp.float32
BF16 = jnp.bfloat16

LANES = 128
SUBLANES = 8
BF16_TILE_ROWS = 16
V7X_VMEM_BYTES = 64 * 1024 * 1024
VMEM_LIMIT = V7X_VMEM_BYTES * 3 // 4
VMEM_LIMIT_SSD = V7X_VMEM_BYTES * 7 // 8

EPS = 1e-6
CHUNK = 128
GRID_W = 64
RET_HEADS = 4
RET_DK = 128
ROPE_BASE = 10000.0
SSD_HEADS = 8
SSD_HEADDIM = 64
SSD_GROUPS = 2
SSD_STATE = 128
SSD_WIDTH = SSD_HEADS * SSD_HEADDIM
SSD_CONV = 5
SSD_PAIRS = SSD_WIDTH // LANES
MOE_GROUPS = 4
EXPERTS_PER_GROUP = 8
N_EXPERTS = MOE_GROUPS * EXPERTS_PER_GROUP
TOP_K = 2
CONV_HALO = SUBLANES

TM_PROJ = 512
TM_OUT = 512
TD_DISPATCH = TM_OUT
MB_EXPERT = 512
EXPERT_BLOCKS_PER_STEP = 4
ZERO_PIECE = MB_EXPERT // 2
TF_COMBINE = TM_OUT
SEG_ALIGN = BF16_TILE_ROWS
COMMON_STAGE_ROWS = TOP_K * TM_OUT + N_EXPERTS * 10
RET_UNROLL = 8
SSD_PREP_UNROLL = 8
SSD_OUT_UNROLL = 2
NEG_BIG = -1e30


def _silu(v):
    return v * jax.nn.sigmoid(v)


def _dot(a, b):
    return jnp.dot(a, b, preferred_element_type=F32)


def _dot_tn(a, b):
    return lax.dot_general(a, b, (((0,), (0,)), ((), ())), preferred_element_type=F32)


def _dot_nt(a, b):
    return lax.dot_general(a, b, (((1,), (1,)), ((), ())), preferred_element_type=F32)


def _mod_kernel(c_ref, w_ref, b_ref, o_ref):
    a = _silu(c_ref[...])
    w = w_ref[...]
    a_hi = a.astype(BF16)
    a_lo = (a - a_hi.astype(F32)).astype(BF16)
    w_hi = w.astype(BF16)
    w_lo = (w - w_hi.astype(F32)).astype(BF16)
    o_ref[...] = _dot(a_hi, w_hi) + _dot(a_lo, w_hi) + _dot(a_hi, w_lo) + b_ref[...]


def _modulation(c_all, w_mod, b_mod):
    rows, d = c_all.shape
    n = w_mod.shape[1]
    return pl.pallas_call(
        _mod_kernel,
        grid=(n // d,),
        in_specs=[
            pl.BlockSpec((rows, d), lambda j: (0, 0)),
            pl.BlockSpec((d, d), lambda j: (0, j)),
            pl.BlockSpec((1, d), lambda j: (0, j)),
        ],
        out_specs=pl.BlockSpec((rows, d), lambda j: (0, j)),
        out_shape=jax.ShapeDtypeStruct((rows, n), F32),
        name="modulation",
    )(c_all, w_mod, b_mod.reshape(1, n))


def _norm_mod(x, nw, sc, sh):
    ms = jnp.mean(x * x, axis=-1, keepdims=True)
    return (x * lax.rsqrt(ms + EPS)) * (nw * (1.0 + sc)) + sh


def _rope(t, cos, sin_signed, first_half):
    width = t.shape[-1]
    quarter = RET_DK // 4
    swapped = jnp.where(first_half, pltpu.roll(t, width - quarter, 1), pltpu.roll(t, quarter, 1))
    return t * cos + swapped * sin_signed


def _inproj_kernel(x_ref, sh_ref, sc_ref, nw_ref, wqk_ref, wvgz_ref, wxbc_ref, wdt_ref, cos_ref, sin_ref,
                   q_ref, k_ref, v_ref, g_ref, z_ref, xbc_ref, dt_ref):
    hb = _norm_mod(x_ref[0], nw_ref[...], sc_ref[0], sh_ref[0]).astype(BF16)
    rw = q_ref.shape[-1]
    qk = _dot(hb, wqk_ref[...])
    cos = cos_ref[...]
    sin = sin_ref[...]
    lane = lax.broadcasted_iota(jnp.int32, cos.shape, 1)
    first_half = (lane % (RET_DK // 2)) < (RET_DK // 4)
    q_ref[0] = _rope(qk[:, :rw], cos, sin, first_half).astype(BF16)
    k_ref[0] = (_rope(qk[:, rw:], cos, sin, first_half) * (RET_DK ** -0.5)).astype(BF16)
    vgz = _dot(hb, wvgz_ref[...])
    v_ref[0] = vgz[:, :rw].astype(BF16)
    g_ref[0] = vgz[:, rw:2 * rw].astype(BF16)
    z_ref[0] = vgz[:, 2 * rw:].astype(BF16)
    xbc_ref[0] = _dot(hb, wxbc_ref[...]).astype(BF16)
    dt_ref[0] = _dot(hb, wdt_ref[...])


def _inproj_ctx_kernel(x_ref, sh_ref, sc_ref, nw_ref, wk_ref, wv_ref, wxbc_ref, wdt_ref,
                       k_ref, v_ref, xbc_ref, dt_ref):
    hb = _norm_mod(x_ref[0], nw_ref[...], sc_ref[0], sh_ref[0]).astype(BF16)
    k_ref[0] = (_dot(hb, wk_ref[...]) * (RET_DK ** -0.5)).astype(BF16)
    v_ref[0] = _dot(hb, wv_ref[...]).astype(BF16)
    xbc_ref[0] = _dot(hb, wxbc_ref[...]).astype(BF16)
    dt_ref[0] = _dot(hb, wdt_ref[...])


def _const_spec(shape):
    nd = len(shape)
    return pl.BlockSpec(shape, lambda *_: (0,) * nd)


def _inproj(x, mod3, nw, wqk, wvgz, wxbc, wdt, cos_t, sin_t):
    b, L, d = x.shape
    tm = min(TM_PROJ, L)
    rw = wqk.shape[1] // 2
    tok = lambda w: pl.BlockSpec((1, tm, w), lambda i, j: (i, j, 0))
    out_bf = lambda w: jax.ShapeDtypeStruct((b, L, w), BF16)
    return pl.pallas_call(
        _inproj_kernel,
        grid=(b, L // tm),
        in_specs=[
            tok(d),
            pl.BlockSpec((1, 1, d), lambda i, j: (i, 0, 0)),
            pl.BlockSpec((1, 1, d), lambda i, j: (i, 0, 1)),
            _const_spec((1, d)),
            _const_spec(wqk.shape), _const_spec(wvgz.shape), _const_spec(wxbc.shape), _const_spec(wdt.shape),
            pl.BlockSpec((tm, rw), lambda i, j: (j, 0)),
            pl.BlockSpec((tm, rw), lambda i, j: (j, 0)),
        ],
        out_specs=[tok(rw), tok(rw), tok(rw), tok(rw), tok(rw), tok(wxbc.shape[1]), tok(LANES)],
        out_shape=[out_bf(rw), out_bf(rw), out_bf(rw), out_bf(rw), out_bf(rw), out_bf(wxbc.shape[1]),
                   jax.ShapeDtypeStruct((b, L, LANES), F32)],
        compiler_params=pltpu.CompilerParams(vmem_limit_bytes=VMEM_LIMIT),
        name="inproj",
    )(x, mod3, mod3, nw, wqk, wvgz, wxbc, wdt, cos_t, sin_t)


def _inproj_ctx(ctx, mod3, ctx_row, nw, wk, wv, wxbc, wdt):
    b, L, d = ctx.shape
    tm = min(TM_PROJ, L)
    rw = wk.shape[1]
    tok = lambda w: pl.BlockSpec((1, tm, w), lambda i, j: (i, j, 0))
    out_bf = lambda w: jax.ShapeDtypeStruct((b, L, w), BF16)
    return pl.pallas_call(
        _inproj_ctx_kernel,
        grid=(b, L // tm),
        in_specs=[
            tok(d),
            pl.BlockSpec((1, 1, d), lambda i, j: (ctx_row, 0, 0)),
            pl.BlockSpec((1, 1, d), lambda i, j: (ctx_row, 0, 1)),
            _const_spec((1, d)),
            _const_spec(wk.shape), _const_spec(wv.shape), _const_spec(wxbc.shape), _const_spec(wdt.shape),
        ],
        out_specs=[tok(rw), tok(rw), tok(wxbc.shape[1]), tok(LANES)],
        out_shape=[out_bf(rw), out_bf(rw), out_bf(wxbc.shape[1]), jax.ShapeDtypeStruct((b, L, LANES), F32)],
        compiler_params=pltpu.CompilerParams(vmem_limit_bytes=VMEM_LIMIT),
        name="inproj_ctx",
    )(ctx, mod3, mod3, nw, wk, wv, wxbc, wdt)


def _ssd_kernel(xbc_ref, z_ref, dt_ref, xbcc_ref, dtc_ref, cw_ref, cb_ref, dtb_ref, alog_ref, dsk_ref, nw_ref,
                y_ref,
                xpad, xpadc, u, uc, sf_scr, kb_scr, acum, ecum, dec_scr,
                arow_scr, erow_scr, dtrow_scr):
    L = xbc_ref.shape[1]
    Lc = xbcc_ref.shape[1]
    nch = L // CHUNK
    nchc = Lc // CHUNK
    win = CHUNK + 2 * CONV_HALO
    nconv = xbc_ref.shape[2]
    nh = SSD_HEADS

    def conv_pass(src_ref, pad_ref, dst_ref, n_chunks, length):
        zeros = jnp.zeros((CONV_HALO, nconv), F32)
        pad_ref[0:CONV_HALO, :] = zeros
        pad_ref[CONV_HALO + length:2 * CONV_HALO + length, :] = zeros
        pad_ref[CONV_HALO:CONV_HALO + length, :] = src_ref[0].astype(F32)

        def chunk(c, carry):
            base = pl.multiple_of(c * CHUNK, CHUNK)
            for cb_i in range(nconv // LANES):
                cols = slice(cb_i * LANES, (cb_i + 1) * LANES)
                w = pad_ref[pl.ds(base, win), cols]
                acc = cb_ref[:, cols] + w[CONV_HALO:CONV_HALO + CHUNK] * cw_ref[SSD_CONV // 2:SSD_CONV // 2 + 1, cols]
                for j in range(SSD_CONV):
                    if j == SSD_CONV // 2:
                        continue
                    shifted = pltpu.roll(w, (SSD_CONV // 2 - j) % win, 0)
                    acc = acc + shifted[CONV_HALO:CONV_HALO + CHUNK] * cw_ref[j:j + 1, cols]
                dst_ref[pl.ds(base, CHUNK), cols] = _silu(acc).astype(BF16)
            return carry

        lax.fori_loop(0, n_chunks, chunk, 0)

    conv_pass(xbcc_ref, xpadc, uc, nchc, Lc)
    conv_pass(xbc_ref, xpad, u, nch, L)

    a_neg = -jnp.exp(alog_ref[...])

    row_i = lax.broadcasted_iota(jnp.int32, (CHUNK, CHUNK), 0)
    col_i = lax.broadcasted_iota(jnp.int32, (CHUNK, CHUNK), 1)
    causal = col_i <= row_i
    lo_half = col_i < SSD_HEADDIM
    fwd_lane = col_i < nh
    head_of = lax.broadcasted_iota(jnp.int32, (CHUNK, SSD_WIDTH), 1) // SSD_HEADDIM
    src_col = lax.broadcasted_iota(jnp.int32, (CHUNK, SSD_WIDTH), 0)
    exp_f = (head_of == src_col).astype(BF16)
    exp_b = (head_of == src_col - nh).astype(BF16)
    exp_fb = jnp.concatenate([exp_f, exp_b], axis=1)

    def split3(v):
        hi = v.astype(BF16)
        r1 = v - hi.astype(F32)
        mid = r1.astype(BF16)
        return hi, mid, (r1 - mid.astype(F32)).astype(BF16)

    def times_onehot(v, m, passes=3):
        parts = split3(v)[:passes]
        acc = _dot(parts[0], m)
        for part in parts[1:]:
            acc = acc + _dot(part, m)
        return acc

    def colb(mat, r):
        return jnp.broadcast_to(mat[:, r:r + 1], (CHUNK, CHUNK))

    def pair_sel(a, b_):
        return jnp.where(lo_half, a, b_)

    gw = 2 * LANES

    def chunk_terms(u_ref, raw_ref, base):
        raw_t = (raw_ref[0, pl.ds(base, CHUNK), :] + dtb_ref[...]).T
        dt_row = jax.nn.softplus(raw_t[:2 * nh])
        dt = jnp.concatenate([dt_row, jnp.zeros((CHUNK - 2 * nh, CHUNK), F32)], axis=0).T
        da = dt * a_neg
        acol = da
        for step in (1, 2, 4, 8, 16, 32, 64):
            acol = acol + jnp.where(row_i >= step, pltpu.roll(acol, step, 0), 0.0)
        ecol = acol - da
        last = acol[CHUNK - 1:CHUNK, :]
        wgt = jnp.where(fwd_lane, jnp.exp(last - acol), jnp.exp(ecol)) * dt
        scale = jnp.where(fwd_lane, jnp.exp(acol), jnp.exp(last - ecol))
        wide = times_onehot(jnp.concatenate([wgt, scale], axis=0), exp_fb, passes=1)
        dec = times_onehot(jnp.broadcast_to(jnp.exp(last), (SUBLANES, LANES)), exp_fb)[0:1]
        xs = u_ref[pl.ds(base, CHUNK), 0:SSD_WIDTH].astype(F32)
        kmats = []
        for g in range(SSD_GROUPS):
            xw = jnp.concatenate([xs[:, g * gw:(g + 1) * gw] * wide[:CHUNK, g * gw:(g + 1) * gw],
                                  xs[:, g * gw:(g + 1) * gw] * wide[:CHUNK, SSD_WIDTH + g * gw:SSD_WIDTH + (g + 1) * gw]],
                                 axis=1).astype(BF16)
            bm = u_ref[pl.ds(base, CHUNK), SSD_WIDTH + g * SSD_STATE:SSD_WIDTH + (g + 1) * SSD_STATE]
            kmats.append(_dot_tn(bm, xw))
        return dt_row, acol, ecol, wide[CHUNK:], dec, kmats

    def advance(s, dec, kmats, backward):
        off = SSD_WIDTH if backward else 0
        koff = gw if backward else 0
        return [dec[:, off + g * gw:off + (g + 1) * gw] * s[g] + kmats[g][:, koff:koff + gw]
                for g in range(SSD_GROUPS)]

    ctx_terms = [chunk_terms(uc, dtc_ref, c * CHUNK) for c in range(nchc)]
    s_f0 = [jnp.zeros((SSD_STATE, gw), F32) for _ in range(SSD_GROUPS)]
    for c in range(nchc):
        s_f0 = advance(s_f0, ctx_terms[c][4], ctx_terms[c][5], False)
    s_b0 = [jnp.zeros((SSD_STATE, gw), F32) for _ in range(SSD_GROUPS)]
    for c in reversed(range(nchc)):
        s_b0 = advance(s_b0, ctx_terms[c][4], ctx_terms[c][5], True)

    def prep(c, carry):
        base = pl.multiple_of(c * CHUNK, CHUNK)
        dt_row, acol, ecol, scale, dec, kmats = chunk_terms(u, dt_ref, base)
        acum[pl.ds(base, CHUNK), :] = acol
        ecum[pl.ds(base, CHUNK), :] = ecol
        hrow = pl.ds(pl.multiple_of(c * 2 * nh, 2 * nh), 2 * nh)
        arow_scr[hrow, :] = acol.T[:2 * nh]
        erow_scr[hrow, :] = ecol.T[:2 * nh]
        dtrow_scr[hrow, :] = dt_row
        xpad[pl.ds(base, CHUNK), :] = scale
        dec_scr[pl.ds(pl.multiple_of(c * SUBLANES, SUBLANES), SUBLANES), :] = jnp.broadcast_to(dec, (SUBLANES, 2 * SSD_WIDTH))
        for g in range(SSD_GROUPS):
            sf_scr[c, g] = kmats[g][:, :gw]
            kb_scr[c, g] = kmats[g][:, gw:]
        return carry

    lax.fori_loop(0, nch, prep, 0, unroll=SSD_PREP_UNROLL)

    def chunk_dec(c):
        return dec_scr[pl.ds(pl.multiple_of(c * SUBLANES, SUBLANES), 1), :]

    def fwd(c, s_old):
        dec = chunk_dec(c)
        new = []
        for g in range(SSD_GROUPS):
            new.append(dec[:, g * gw:(g + 1) * gw] * s_old[g] + sf_scr[c, g])
            sf_scr[c, g] = s_old[g]
        return tuple(new)

    lax.fori_loop(0, nch, fwd, tuple(s_f0))

    def bwd(i, s_b):
        c = nch - 1 - i
        base = pl.multiple_of(c * CHUNK, CHUNK)
        acol = acum[pl.ds(base, CHUNK), :]
        ecol = ecum[pl.ds(base, CHUNK), :]
        hrow = pl.ds(pl.multiple_of(c * 2 * nh, 2 * nh), 2 * nh)
        arow = arow_scr[hrow, :]
        erow = erow_scr[hrow, :]
        dt_t = dtrow_scr[hrow, :]
        scale = xpad[pl.ds(base, CHUNK), :]
        ys = []
        for g in range(SSD_GROUPS):
            bm = u[pl.ds(base, CHUNK), SSD_WIDTH + g * SSD_STATE:SSD_WIDTH + (g + 1) * SSD_STATE]
            cm = u[pl.ds(base, CHUNK), SSD_WIDTH + (SSD_GROUPS + g) * SSD_STATE:SSD_WIDTH + (SSD_GROUPS + g + 1) * SSD_STATE]
            cbm = _dot_nt(cm, bm)
            cs_f = _dot(cm, sf_scr[c, g].astype(BF16))
            cs_b = _dot(cm, s_b[g].astype(BF16))
            for pp in range(SSD_PAIRS // SSD_GROUPS):
                p = g * (SSD_PAIRS // SSD_GROUPS) + pp
                xs_b = u[pl.ds(base, CHUNK), p * LANES:(p + 1) * LANES]
                y_h = []
                for hh in range(2):
                    r = 2 * p + hh
                    arg = jnp.where(causal, colb(acol, r) - arow[r:r + 1, :],
                                    erow[nh + r:nh + r + 1, :] - colb(ecol, nh + r))
                    coef = jnp.where(causal, dt_t[r:r + 1, :], dt_t[nh + r:nh + r + 1, :])
                    gm = (cbm * (jnp.exp(arg) * coef)).astype(BF16)
                    y_h.append(_dot(gm, xs_b))
                sl = slice(pp * LANES, (pp + 1) * LANES)
                wl = slice(p * LANES, (p + 1) * LANES)
                wlb = slice(SSD_WIDTH + p * LANES, SSD_WIDTH + (p + 1) * LANES)
                ys.append(pair_sel(y_h[0], y_h[1]) + cs_f[:, sl] * scale[:, wl] + cs_b[:, sl] * scale[:, wlb]
                          + dsk_ref[:, wl] * xs_b.astype(F32))
        y = jnp.concatenate(ys, axis=1)
        y = y * _silu(z_ref[0, pl.ds(base, CHUNK), :].astype(F32))
        ms = jnp.mean(y * y, axis=-1, keepdims=True)
        y_ref[0, pl.ds(base, CHUNK), :] = ((y * lax.rsqrt(ms + EPS)) * nw_ref[...]).astype(BF16)
        dec = chunk_dec(c)
        return tuple(dec[:, SSD_WIDTH + g * gw:SSD_WIDTH + (g + 1) * gw] * s_b[g] + kb_scr[c, g]
                     for g in range(SSD_GROUPS))

    lax.fori_loop(0, nch, bwd, tuple(s_b0), unroll=SSD_OUT_UNROLL)


def _ssd(xbc, z, dt, xbcc, dtc, conv_w8, conv_b, dt_bias, a_log, d_skip, norm_w):
    b, L, nconv = xbc.shape
    Lc = xbcc.shape[1]
    nch = L // CHUNK
    per_b = lambda n, w: pl.BlockSpec((1, n, w), lambda i: (i, 0, 0))
    return pl.pallas_call(
        _ssd_kernel,
        grid=(b,),
        in_specs=[
            per_b(L, nconv), per_b(L, SSD_WIDTH), per_b(L, LANES), per_b(Lc, nconv), per_b(Lc, LANES),
            _const_spec(conv_w8.shape), _const_spec(conv_b.shape), _const_spec(dt_bias.shape),
            _const_spec(a_log.shape), _const_spec(d_skip.shape), _const_spec(norm_w.shape),
        ],
        out_specs=per_b(L, SSD_WIDTH),
        out_shape=jax.ShapeDtypeStruct((b, L, SSD_WIDTH), BF16),
        scratch_shapes=[
            pltpu.VMEM((L + 2 * CONV_HALO, nconv), F32),
            pltpu.VMEM((Lc + 2 * CONV_HALO, nconv), F32),
            pltpu.VMEM((L, nconv), BF16),
            pltpu.VMEM((Lc, nconv), BF16),
            pltpu.VMEM((nch, SSD_GROUPS, SSD_STATE, 2 * LANES), F32),
            pltpu.VMEM((nch, SSD_GROUPS, SSD_STATE, 2 * LANES), F32),
            pltpu.VMEM((L, LANES), F32), pltpu.VMEM((L, LANES), F32),
            pltpu.VMEM((nch * SUBLANES, 2 * SSD_WIDTH), F32),
            pltpu.VMEM((nch * 2 * SSD_HEADS, CHUNK), F32), pltpu.VMEM((nch * 2 * SSD_HEADS, CHUNK), F32),
            pltpu.VMEM((nch * 2 * SSD_HEADS, CHUNK), F32),
        ],
        compiler_params=pltpu.CompilerParams(vmem_limit_bytes=VMEM_LIMIT_SSD),
        name="ssd",
    )(xbc, z, dt, xbcc, dtc, conv_w8, conv_b, dt_bias, a_log, d_skip, norm_w)


def _ret_kernel(q_ref, k_ref, v_ref, g_ref, kc_ref, vc_ref, df_ref, db_ref, gn_ref, y_ref, sf_scr):
    L = q_ref.shape[1]
    Lc = kc_ref.shape[1]
    nch = L // CHUNK
    dk = RET_DK
    row_i = lax.broadcasted_iota(jnp.int32, (CHUNK, dk), 0).astype(F32)
    col_i = lax.broadcasted_iota(jnp.int32, (CHUNK, dk), 1).astype(F32)
    rel = row_i - col_i
    crow = lax.broadcasted_iota(jnp.int32, (Lc, dk), 0).astype(F32)

    heads = []
    s_f0 = []
    s_b0 = []
    for h in range(RET_HEADS):
        cols = slice(h * dk, (h + 1) * dk)
        lg_f = -jnp.exp(df_ref[:, cols])
        lg_b = -jnp.exp(db_ref[:, cols])
        heads.append(dict(
            cols=cols,
            dmat=jnp.where(rel >= 0, jnp.exp(jnp.maximum(rel, 0.0) * lg_f), jnp.exp(jnp.maximum(-rel, 0.0) * lg_b)),
            dq_f=jnp.exp((row_i + 1.0) * lg_f),
            dq_b=jnp.exp((CHUNK - row_i) * lg_b),
            dk_f=jnp.exp((CHUNK - 1.0 - row_i) * lg_f),
            dk_b=jnp.exp(row_i * lg_b),
            dc_f=jnp.exp(CHUNK * lg_f),
            dc_b=jnp.exp(CHUNK * lg_b),
        ))
        kc = kc_ref[0, :, cols].astype(F32)
        vc = vc_ref[0, :, cols]
        s_f0.append(_dot_tn((kc * jnp.exp((Lc - 1.0 - crow) * lg_f)).astype(BF16), vc))
        s_b0.append(_dot_tn((kc * jnp.exp(crow * lg_b)).astype(BF16), vc))

    def fwd(c, s_f):
        base = pl.multiple_of(c * CHUNK, CHUNK)
        new = []
        for h, hd in enumerate(heads):
            sf_scr[c, h] = s_f[h]
            kk = k_ref[0, pl.ds(base, CHUNK), hd["cols"]].astype(F32)
            vv = v_ref[0, pl.ds(base, CHUNK), hd["cols"]]
            new.append(hd["dc_f"] * s_f[h] + _dot_tn((kk * hd["dk_f"]).astype(BF16), vv))
        return tuple(new)

    lax.fori_loop(0, nch, fwd, tuple(s_f0), unroll=RET_UNROLL)

    def bwd(i, s_bs):
        c = nch - 1 - i
        base = pl.multiple_of(c * CHUNK, CHUNK)
        new = []
        for h, hd in enumerate(heads):
            qq = q_ref[0, pl.ds(base, CHUNK), hd["cols"]]
            kk = k_ref[0, pl.ds(base, CHUNK), hd["cols"]]
            vv = v_ref[0, pl.ds(base, CHUNK), hd["cols"]]
            s_b = s_bs[h]
            scores = (_dot_nt(qq, kk) * hd["dmat"]).astype(BF16)
            y = (_dot(scores, vv)
                 + _dot(qq, sf_scr[c, h].astype(BF16)) * hd["dq_f"]
                 + _dot(qq, s_b.astype(BF16)) * hd["dq_b"])
            mu = jnp.mean(y, axis=-1, keepdims=True)
            yc = y - mu
            var = jnp.mean(yc * yc, axis=-1, keepdims=True)
            yn = (yc * lax.rsqrt(var + EPS)) * gn_ref[:, hd["cols"]]
            gate = _silu(g_ref[0, pl.ds(base, CHUNK), hd["cols"]].astype(F32))
            y_ref[0, pl.ds(base, CHUNK), hd["cols"]] = (yn * gate).astype(BF16)
            new.append(hd["dc_b"] * s_b + _dot_tn((kk.astype(F32) * hd["dk_b"]).astype(BF16), vv))
        return tuple(new)

    lax.fori_loop(0, nch, bwd, tuple(s_b0), unroll=RET_UNROLL)


def _retention(q, k, v, g, kc, vc, decay_f, decay_b, gn_w):
    b, L, w = q.shape
    Lc = kc.shape[1]
    nch = L // CHUNK
    per_b = lambda n: pl.BlockSpec((1, n, w), lambda i: (i, 0, 0))
    return pl.pallas_call(
        _ret_kernel,
        grid=(b,),
        in_specs=[per_b(L), per_b(L), per_b(L), per_b(L), per_b(Lc), per_b(Lc),
                  _const_spec((1, w)), _const_spec((1, w)), _const_spec((1, w))],
        out_specs=per_b(L),
        out_shape=jax.ShapeDtypeStruct((b, L, w), BF16),
        scratch_shapes=[
            pltpu.VMEM((nch, RET_HEADS, RET_DK, RET_DK), F32),
        ],
        compiler_params=pltpu.CompilerParams(vmem_limit_bytes=VMEM_LIMIT),
        name="retention",
    )(q, k, v, g, kc, vc, decay_f, decay_b, gn_w)


def _outproj_router_kernel(yr_ref, ys_ref, x_ref, g1_ref, sh2_ref, sc2_ref, npost_ref, npre_ref,
                           wor_ref, wos_ref, wr_ref, br_ref, tri_ref,
                           x1_ref, h2_ref, route_ref, slots_ref, seg_ref,
                           wcat):
    i = pl.program_id(0)

    @pl.when(i == 0)
    def _():
        wr = wr_ref[...]
        hi = wr.astype(BF16)
        wcat[:, :LANES] = hi
        wcat[:, LANES:] = (wr - hi.astype(F32)).astype(BF16)

    _route_tile(yr_ref, ys_ref, x_ref, g1_ref, sh2_ref, sc2_ref, npost_ref, npre_ref, wor_ref, wos_ref, br_ref,
                tri_ref, x1_ref, h2_ref, route_ref, slots_ref, seg_ref, wcat)


def _route_tile(yr_ref, ys_ref, x_ref, g1_ref, sh2_ref, sc2_ref, npost_ref, npre_ref, wor_ref, wos_ref, br_ref,
                tri_ref, x1_ref, h2_ref, route_ref, slots_ref, seg_ref, wcat):
    tm = x_ref.shape[0]
    rows = slice(0, tm)
    y = _dot(yr_ref[rows, :], wor_ref[...]) + _dot(ys_ref[rows, :], wos_ref[...])
    ms = jnp.mean(y * y, axis=-1, keepdims=True)
    x1 = x_ref[rows, :] + (y * lax.rsqrt(ms + EPS)) * (g1_ref[0] * npost_ref[...])
    x1_ref[rows, :] = x1
    h2 = _norm_mod(x1, npre_ref[...], sc2_ref[0], sh2_ref[0])
    h2_ref[rows, :] = h2.astype(BF16)

    h_hi = h2.astype(BF16)
    h_lo = (h2 - h_hi.astype(F32)).astype(BF16)
    both = _dot(h_hi, wcat[...])
    lg = both[:, :LANES] + both[:, LANES:] + _dot(h_lo, wcat[:, :LANES]) + br_ref[...]

    lane = lax.broadcasted_iota(jnp.int32, (tm, LANES), 1)
    lane_f = lane.astype(F32)
    is_grp = (lane >= N_EXPERTS) & (lane < N_EXPERTS + MOE_GROUPS)
    gl = jnp.where(is_grp, lg, NEG_BIG)
    mg = jnp.max(gl, axis=-1, keepdims=True)
    grp_lane = jnp.min(jnp.where(gl == mg, lane_f, 1e9), axis=-1, keepdims=True)
    p_g = 1.0 / jnp.sum(jnp.where(is_grp, jnp.exp(gl - mg), 0.0), axis=-1, keepdims=True)
    first = (grp_lane - N_EXPERTS) * EXPERTS_PER_GROUP
    in_grp = (lane_f >= first) & (lane_f < first + EXPERTS_PER_GROUP)
    el = jnp.where(in_grp, lg, NEG_BIG)
    t1 = jnp.max(el, axis=-1, keepdims=True)
    i1 = jnp.min(jnp.where(el == t1, lane_f, 1e9), axis=-1, keepdims=True)
    el2 = jnp.where(lane_f == i1, NEG_BIG, el)
    t2 = jnp.max(el2, axis=-1, keepdims=True)
    i2 = jnp.min(jnp.where(el2 == t2, lane_f, 1e9), axis=-1, keepdims=True)
    s = jnp.exp(t2 - t1)
    w1 = p_g / (1.0 + s)
    w2 = p_g * s / (1.0 + s)

    oh1 = (lane_f == i1)
    oh2 = (lane_f == i2)
    ohf = jnp.where(oh1 | oh2, 1.0, 0.0)
    before = _dot(tri_ref[...], ohf.astype(BF16))
    cnt = jnp.sum(ohf, axis=0, keepdims=True)
    seg = jnp.floor((cnt + (SEG_ALIGN - 1.0)) * (1.0 / SEG_ALIGN)) * SEG_ALIGN
    e_row = lax.broadcasted_iota(jnp.int32, (LANES, LANES), 0)
    e_col = lax.broadcasted_iota(jnp.int32, (LANES, LANES), 1)
    earlier = (e_row < e_col).astype(BF16)
    seg_off = _dot(jnp.broadcast_to(seg, (SUBLANES, LANES)).astype(BF16), earlier)[0:1]
    where_to = before + seg_off
    lpos1 = jnp.sum(jnp.where(oh1, where_to, 0.0), axis=-1, keepdims=True)
    lpos2 = jnp.sum(jnp.where(oh2, where_to, 0.0), axis=-1, keepdims=True)

    cols = [w1, w2, lpos1, lpos2]
    for wk in (w1, w2):
        hi = wk.astype(BF16).astype(F32)
        cols += [hi, wk - hi]
    cols.append(jnp.ones_like(w1))
    packed = jnp.zeros((tm, LANES), F32)
    for k, col in enumerate(cols):
        packed = jnp.where(lane == k, col, packed)
    route_ref[rows, :] = packed

    row = lax.broadcasted_iota(jnp.int32, (tm, LANES), 0)
    on_diag = (row % LANES) == lane
    per = tm // LANES
    for qi, col in enumerate((lpos1, lpos2)):
        picked = jnp.where(on_diag, col, 0.0)
        dense = jnp.sum(picked.reshape(per, LANES, LANES), axis=1).astype(jnp.int32)
        slots_ref[0, qi * per:(qi + 1) * per, :] = dense
    tbl_row = lax.broadcasted_iota(jnp.int32, (SUBLANES, LANES), 0)
    seg_ref[0] = jnp.where(tbl_row == 0, seg, jnp.where(tbl_row == 1, seg_off, 0.0)).astype(jnp.int32)


def _outproj_router(yr, ys, x2, mod3, npost, npre, wo_r, wo_s, w_router, b_router, seq_len):
    T, d = x2.shape
    tm = TM_OUT
    per_seq = seq_len // tm
    rw = yr.shape[1]
    tri = (jnp.arange(tm)[:, None] > jnp.arange(tm)[None, :]).astype(BF16)
    tok = lambda w: pl.BlockSpec((tm, w), lambda i: (i, 0))
    modv = lambda k: pl.BlockSpec((1, 1, d), lambda i: (i // per_seq, 0, k))
    tile3 = lambda r: pl.BlockSpec((1, r, LANES), lambda i: (i, 0, 0))
    slot_rows = TOP_K * (tm // LANES)
    return pl.pallas_call(
        _outproj_router_kernel,
        grid=(T // tm,),
        in_specs=[
            tok(rw), tok(rw), tok(d), modv(2), modv(3), modv(4),
            _const_spec((1, d)), _const_spec((1, d)),
            _const_spec(wo_r.shape), _const_spec(wo_s.shape), _const_spec(w_router.shape), _const_spec((1, LANES)),
            _const_spec((tm, tm)),
        ],
        out_specs=[tok(d), tok(d), tok(LANES), tile3(slot_rows), tile3(SUBLANES)],
        out_shape=[jax.ShapeDtypeStruct((T, d), F32), jax.ShapeDtypeStruct((T, d), BF16),
                   jax.ShapeDtypeStruct((T, LANES), F32),
                   jax.ShapeDtypeStruct((T // tm, slot_rows, LANES), jnp.int32),
                   jax.ShapeDtypeStruct((T // tm, SUBLANES, LANES), jnp.int32)],
        scratch_shapes=[pltpu.VMEM((d, 2 * LANES), BF16)],
        compiler_params=pltpu.CompilerParams(dimension_semantics=("arbitrary",),
                                             vmem_limit_bytes=VMEM_LIMIT),
        name="outproj_router",
    )(yr, ys, x2, mod3, mod3, mod3, npost, npre, wo_r, wo_s, w_router, b_router, tri)


def _expert_kernel(be_ref, first_ref, slot_ref, next_ref, nused_ref, xs_ref, wg_hbm, wu_hbm, wd_hbm, y_ref,
                   wg_f, wu_f, wd_f, wg_b, wu_b, wd_b, sem):
    def fetch(e, s):
        return [pltpu.make_async_copy(src.at[e], dst.at[s], sem.at[s, k])
                for k, (src, dst) in enumerate(((wg_hbm, wg_f), (wu_hbm, wu_f), (wd_hbm, wd_f)))]

    @pl.when(pl.program_id(0) == 0)
    def _():
        for c in fetch(be_ref[0], slot_ref[0]):
            c.start()

    for sb in range(EXPERT_BLOCKS_PER_STEP):
        _expert_block(pl.program_id(0) * EXPERT_BLOCKS_PER_STEP + sb, slice(sb * MB_EXPERT, (sb + 1) * MB_EXPERT),
                      fetch, be_ref, first_ref, slot_ref, next_ref, nused_ref, xs_ref, y_ref,
                      wg_f, wu_f, wd_f, wg_b, wu_b, wd_b)


def _expert_block(i, rows, fetch, be_ref, first_ref, slot_ref, next_ref, nused_ref, xs_ref, y_ref,
                  wg_f, wu_f, wd_f, wg_b, wu_b, wd_b):
    @pl.when(first_ref[i] == 1)
    def _():
        s = slot_ref[i]

        @pl.when(next_ref[i] >= 0)
        def _():
            for c in fetch(next_ref[i], 1 - s):
                c.start()

        for c in fetch(be_ref[i], s):
            c.wait()
        wg_b[...] = wg_f[s].astype(BF16)
        wu_b[...] = wu_f[s].astype(BF16)
        wd_b[...] = wd_f[s].astype(BF16)

    @pl.when(i < nused_ref[0])
    def _():
        d = y_ref.shape[1]
        side = xs_ref[rows, d:d + LANES].astype(F32)
        second = side[:, 8:9] == 2.0
        unscale = jnp.where(second, 0.5, 1.0)
        weight = jnp.where(second, side[:, 6:7] + side[:, 7:8], side[:, 4:5] + side[:, 5:6]) * unscale
        xb = xs_ref[rows, 0:d] * unscale.astype(BF16)
        hid = (_silu(_dot(xb, wg_b[...])) * _dot(xb, wu_b[...])).astype(BF16)
        y_ref[rows, :] = (_dot(hid, wd_b[...]) * weight).astype(BF16)

    @pl.when(i >= nused_ref[0])
    def _():
        y_ref[rows, :] = jnp.zeros((rows.stop - rows.start, y_ref.shape[1]), y_ref.dtype)


def _expert_plan(padded, pad_end, n_blocks, mb):
    n_used = (pad_end[-1:] // mb).astype(jnp.int32)
    blk_start = jnp.arange(n_blocks, dtype=jnp.int32) * mb
    experts = jnp.arange(N_EXPERTS, dtype=jnp.int32)
    blk_expert = jnp.minimum(jnp.sum((pad_end[None, :] <= blk_start[:, None]).astype(jnp.int32), axis=1),
                             N_EXPERTS - 1)
    prev = jnp.concatenate([jnp.full((1,), -1, jnp.int32), blk_expert[:-1]])
    first = ((blk_start < pad_end[-1]) & (blk_expert != prev)).astype(jnp.int32)
    slot = (jnp.cumsum(first) - 1) % 2
    later = jnp.where((padded > 0)[None, :] & (experts[None, :] > experts[:, None]), experts[None, :], N_EXPERTS)
    next_nonempty = jnp.min(later, axis=1)
    next_nonempty = jnp.where(next_nonempty == N_EXPERTS, -1, next_nonempty)
    nxt = jnp.sum(jnp.where(blk_expert[:, None] == experts[None, :], next_nonempty[None, :], 0), axis=1)
    return [a.astype(jnp.int32) for a in (blk_expert, first, slot, nxt, n_used)]


def _experts(plan, xs, w_gate, w_up, w_down):
    cap, dp = xs.shape
    n_exp, d, de = w_gate.shape
    step_rows = MB_EXPERT * EXPERT_BLOCKS_PER_STEP
    assert dp == d + LANES and cap % step_rows == 0
    grid_spec = pltpu.PrefetchScalarGridSpec(
        num_scalar_prefetch=len(plan),
        grid=(cap // step_rows,),
        in_specs=[
            pl.BlockSpec((step_rows, dp), lambda i, be, fi, sl, nx, nu: (
                jnp.minimum(i, (nu[0] - 1) // EXPERT_BLOCKS_PER_STEP), 0)),
            pl.BlockSpec(memory_space=pl.ANY), pl.BlockSpec(memory_space=pl.ANY), pl.BlockSpec(memory_space=pl.ANY),
        ],
        out_specs=pl.BlockSpec((step_rows, d), lambda i, be, fi, sl, nx, nu: (i, 0)),
        scratch_shapes=[pltpu.VMEM((2, d, de), F32), pltpu.VMEM((2, d, de), F32), pltpu.VMEM((2, de, d), F32),
                        pltpu.VMEM((d, de), BF16), pltpu.VMEM((d, de), BF16), pltpu.VMEM((de, d), BF16),
                        pltpu.SemaphoreType.DMA((2, 3))],
    )
    return pl.pallas_call(
        _expert_kernel,
        grid_spec=grid_spec,
        out_shape=jax.ShapeDtypeStruct((cap, d), BF16),
        compiler_params=pltpu.CompilerParams(dimension_semantics=("arbitrary",),
                                             vmem_limit_bytes=VMEM_LIMIT),
        name="experts",
    )(*plan, xs, w_gate, w_up, w_down)


def _segment_pieces(tile, segrow_ref, seglen_ref, segoff_ref, act):
    def per_pair(ep, carry):
        for priority in range(2):
            idx = tile * N_EXPERTS + 2 * ep + priority
            g0 = segrow_ref[idx]
            l0 = segoff_ref[idx]

            def piece(j, c2, g0=g0, l0=l0, priority=priority):
                act(pl.multiple_of(l0 + j * SEG_ALIGN, SEG_ALIGN), pl.multiple_of(g0 + j * SEG_ALIGN, SEG_ALIGN),
                    priority)
                return c2

            lax.fori_loop(0, seglen_ref[idx], piece, 0)
        return carry

    lax.fori_loop(0, N_EXPERTS // 2, per_pair, 0)


def _wait_rows(total, row_copy):
    size = SEG_ALIGN
    while size <= _stage_rows(max(TD_DISPATCH, TF_COMBINE)):
        @pl.when((total & size) != 0)
        def _(size=size):
            row_copy(size).wait()
        size *= 2


def _for_tile_rows(tile_rows, max_rows, body):
    @pl.when(tile_rows <= COMMON_STAGE_ROWS)
    def _():
        body(COMMON_STAGE_ROWS)

    @pl.when(tile_rows > COMMON_STAGE_ROWS)
    def _():
        body(max_rows)


def _stage_rows(tile_tokens):
    return TOP_K * tile_tokens + N_EXPERTS * SEG_ALIGN


def _dispatch_seg_kernel(pad_end_ref, zero_from_ref, segrow_ref, seglen_ref, segoff_ref, tilerows_ref,
                         lpos_ref, route_ref, h_ref, xs_hbm, zbuf, stage, sem, zsem):
    i = pl.program_id(0)
    nt = pl.num_programs(0)
    td, d = h_ref.shape
    sr = stage.shape[1]
    per = td // LANES

    def zero_fills(act):
        def fill(row, n):
            act(pltpu.make_async_copy(zbuf.at[pl.ds(0, n)], xs_hbm.at[pl.ds(pl.multiple_of(row, SEG_ALIGN), n)], zsem))

        def region(e, carry):
            row = zero_from_ref[e]
            left = pad_end_ref[e] - row
            size = ZERO_PIECE
            while size >= SEG_ALIGN:
                take = (left & size) != 0

                @pl.when(take)
                def _(row=row, size=size):
                    fill(row, size)

                row = row + jnp.where(take, size, 0)
                size //= 2
            return carry

        def tail(p, carry):
            @pl.when(p * ZERO_PIECE >= pad_end_ref[N_EXPERTS - 1])
            def _():
                fill(p * ZERO_PIECE, ZERO_PIECE)
            return carry

        lax.fori_loop(0, N_EXPERTS, region, 0)
        lax.fori_loop(0, xs_hbm.shape[0] // ZERO_PIECE, tail, 0)

    @pl.when(i == 0)
    def _():
        zbuf[...] = jnp.zeros_like(zbuf)
        zero_fills(lambda c: c.start())

    def shipped(tile):
        _wait_rows(tilerows_ref[tile], lambda n: pltpu.make_async_copy(
            stage.at[tile % 2, pl.ds(0, n)], xs_hbm.at[pl.ds(0, n)], sem.at[tile % 2]))

    @pl.when(i >= 2)
    def _():
        shipped(i - 2)

    slot = i % 2

    def sort_rows(n_rows):
        srow = lax.broadcasted_iota(jnp.int32, (n_rows, LANES), 0)
        place = jnp.concatenate(
            [jnp.where(srow == lpos_ref[0, cb:cb + 1, :], 1.0,
                       jnp.where(srow == lpos_ref[0, per + cb:per + cb + 1, :], 2.0, 0.0)) for cb in range(per)],
            axis=1).astype(BF16)
        stage[slot, 0:n_rows, :] = _dot(place, jnp.concatenate([h_ref[...], route_ref[...].astype(BF16)],
                                                               axis=1)).astype(BF16)

    _for_tile_rows(tilerows_ref[i], sr, sort_rows)
    _segment_pieces(i, segrow_ref, seglen_ref, segoff_ref, lambda lr, gr, prio: pltpu.make_async_copy(
        stage.at[slot, pl.ds(lr, SEG_ALIGN)], xs_hbm.at[pl.ds(gr, SEG_ALIGN)], sem.at[slot]).start(priority=prio))

    @pl.when(i == nt - 1)
    def _():
        @pl.when(i >= 1)
        def _():
            shipped(i - 1)
        shipped(i)
        zero_fills(lambda c: c.wait())


def _dispatch_seg(plan, lpos, route, h2, cap):
    T, d = h2.shape
    td = TD_DISPATCH
    nt = T // td
    width = d + LANES
    sr = _stage_rows(td)
    n_pre = len(plan)
    grid_spec = pltpu.PrefetchScalarGridSpec(
        num_scalar_prefetch=n_pre,
        grid=(nt,),
        in_specs=[
            pl.BlockSpec((1, lpos.shape[1], LANES), lambda i, *_: (i, 0, 0)),
            pl.BlockSpec((td, LANES), lambda i, *_: (i, 0)),
            pl.BlockSpec((td, d), lambda i, *_: (i, 0)),
        ],
        out_specs=pl.BlockSpec(memory_space=pl.ANY),
        scratch_shapes=[pltpu.VMEM((ZERO_PIECE, width), BF16), pltpu.VMEM((2, sr, width), BF16),
                        pltpu.SemaphoreType.DMA((2,)), pltpu.SemaphoreType.DMA(())],
    )
    return pl.pallas_call(
        _dispatch_seg_kernel,
        grid_spec=grid_spec,
        out_shape=jax.ShapeDtypeStruct((cap, width), BF16),
        compiler_params=pltpu.CompilerParams(dimension_semantics=("arbitrary",), vmem_limit_bytes=VMEM_LIMIT),
        name="dispatch",
    )(*plan, lpos, route, h2)


def _combine_seg_kernel(segrow_ref, seglen_ref, segoff_ref, tilerows_ref, route_ref, x1_ref, g2_ref, nw_ref,
                        yb_hbm, o_ref, stage, sem):
    i = pl.program_id(0)
    nt = pl.num_programs(0)
    tf = x1_ref.shape[0]
    sr = stage.shape[1]

    def fetch(tile, slot):
        _segment_pieces(tile, segrow_ref, seglen_ref, segoff_ref, lambda lr, gr, prio: pltpu.make_async_copy(
            yb_hbm.at[pl.ds(gr, SEG_ALIGN)], stage.at[slot, pl.ds(lr, SEG_ALIGN)],
            sem.at[slot]).start(priority=prio))

    @pl.when(i == 0)
    def _():
        stage[...] = jnp.zeros_like(stage)
        fetch(i, 0)

    for slot in range(2):
        @pl.when((i + 1 < nt) & (i % 2 != slot))
        def _(slot=slot):
            fetch(i + 1, slot)

    _wait_rows(tilerows_ref[i], lambda n: pltpu.make_async_copy(
        yb_hbm.at[pl.ds(0, n)], stage.at[i % 2, pl.ds(0, n)], sem.at[i % 2]))

    local_row = route_ref[:, 2:2 + TOP_K].astype(jnp.int32)

    def unsort_rows(n_rows):
        scol = lax.broadcasted_iota(jnp.int32, (tf, n_rows), 1)
        pick = jnp.where((scol == local_row[:, 0:1]) | (scol == local_row[:, 1:2]), 1.0, 0.0).astype(BF16)
        out = _dot(pick, stage[i % 2, 0:n_rows, :])
        ms = jnp.mean(out * out, axis=-1, keepdims=True)
        o_ref[...] = x1_ref[...] + g2_ref[0] * ((out * lax.rsqrt(ms + EPS)) * nw_ref[...])

    _for_tile_rows(tilerows_ref[i], sr, unsort_rows)


def _combine_seg(seg_plan, route, x1, mod3, nw, yb, seq_len):
    T, d = x1.shape
    tf = TF_COMBINE
    nt = T // tf
    per_seq = seq_len // tf
    grid_spec = pltpu.PrefetchScalarGridSpec(
        num_scalar_prefetch=len(seg_plan),
        grid=(nt,),
        in_specs=[
            pl.BlockSpec((tf, LANES), lambda i, *_: (i, 0)),
            pl.BlockSpec((tf, d), lambda i, *_: (i, 0)),
            pl.BlockSpec((1, 1, d), lambda i, *_: (i // per_seq, 0, 5)),
            pl.BlockSpec((1, d), lambda i, *_: (0, 0)),
            pl.BlockSpec(memory_space=pl.ANY),
        ],
        out_specs=pl.BlockSpec((tf, d), lambda i, *_: (i, 0)),
        scratch_shapes=[pltpu.VMEM((2, _stage_rows(tf), d), yb.dtype), pltpu.SemaphoreType.DMA((2,))],
    )
    return pl.pallas_call(
        _combine_seg_kernel,
        grid_spec=grid_spec,
        out_shape=jax.ShapeDtypeStruct((T, d), F32),
        compiler_params=pltpu.CompilerParams(dimension_semantics=("arbitrary",), vmem_limit_bytes=VMEM_LIMIT),
        name="combine",
    )(*seg_plan, route, x1, mod3, nw, yb)


def _rope_tables(L, n_heads):
    quarter = RET_DK // 4
    freqs = ROPE_BASE ** (-jnp.arange(quarter, dtype=F32) / quarter)
    t = jnp.arange(L)
    ang_r = (t // GRID_W).astype(F32)[:, None] * freqs
    ang_c = (t % GRID_W).astype(F32)[:, None] * freqs
    cos = jnp.concatenate([jnp.cos(ang_r)] * 2 + [jnp.cos(ang_c)] * 2, axis=-1)
    sin = jnp.concatenate([-jnp.sin(ang_r), jnp.sin(ang_r), -jnp.sin(ang_c), jnp.sin(ang_c)], axis=-1)
    return jnp.tile(cos, (1, n_heads)), jnp.tile(sin, (1, n_heads))


def _lane_pad(v, width=LANES):
    return jnp.pad(v, [(0, 0)] * (v.ndim - 1) + [(0, width - v.shape[-1])])


def kernel(x, c, ctx, c_ctx, w_mod, b_mod, norm_pre_mix, norm_post_mix, norm_pre_ffn, norm_post_ffn, w_in, w_out, ret_decay_f, ret_decay_b, ret_gn_w, ssd_conv_w, ssd_conv_b, ssd_dt_bias_f, ssd_dt_bias_b, ssd_a_log_f, ssd_a_log_b, ssd_d, ssd_norm_w, moe_w_rg, moe_b_rg, moe_w_re, moe_b_re, moe_w_gate, moe_w_up, moe_w_down):
    b, L, d = x.shape
    assert w_mod.shape[0] == 1, "single layer: context outputs are never needed"
    assert TM_OUT == TD_DISPATCH == TF_COMBINE, "router, dispatch and combine share one slot-row layout"
    rw = RET_HEADS * RET_DK
    nconv = SSD_WIDTH + 2 * SSD_GROUPS * SSD_STATE
    T = b * L

    mod_rows = -(-(b + 1) // SUBLANES) * SUBLANES
    c_all = jnp.zeros((mod_rows, d), F32).at[:b].set(c).at[b].set(c_ctx)
    mod3 = _modulation(c_all, w_mod[0], b_mod[0]).reshape(mod_rows, 1, 6 * d)

    wi = w_in[0]
    o = 0
    wq = wi[:, o:o + rw]; o += rw
    wk = wi[:, o:o + rw]; o += rw
    wv = wi[:, o:o + rw]; o += rw
    wg = wi[:, o:o + rw]; o += rw
    wz = wi[:, o:o + SSD_WIDTH]; o += SSD_WIDTH
    wxbc = wi[:, o:o + nconv].astype(BF16); o += nconv
    wdt = _lane_pad(wi[:, o:o + 2 * SSD_HEADS]).astype(BF16)
    wqk = jnp.concatenate([wq, wk], axis=1).astype(BF16)
    wvgz = jnp.concatenate([wv, wg, wz], axis=1).astype(BF16)
    cos_t, sin_t = _rope_tables(L, RET_HEADS)
    nw1 = norm_pre_mix[0].reshape(1, d)

    q, k, v, g, z, xbc, dt = _inproj(x, mod3, nw1, wqk, wvgz, wxbc, wdt, cos_t, sin_t)
    kc, vc, xbcc, dtc = _inproj_ctx(ctx, mod3, b, nw1, wk.astype(BF16), wv.astype(BF16), wxbc, wdt)

    conv_w8 = jnp.pad(ssd_conv_w[0], ((0, SUBLANES - SSD_CONV), (0, 0)))
    dt_bias = _lane_pad(jnp.concatenate([ssd_dt_bias_f[0], ssd_dt_bias_b[0]])[None, :])
    a_log = _lane_pad(jnp.concatenate([ssd_a_log_f[0], ssd_a_log_b[0]])[None, :])
    d_skip = jnp.repeat(ssd_d[0], SSD_HEADDIM)[None, :]
    ys = _ssd(xbc, z, dt, xbcc, dtc, conv_w8, ssd_conv_b[0][None, :], dt_bias, a_log, d_skip,
              ssd_norm_w[0][None, :])

    yr = _retention(q, k, v, g, kc, vc,
                    jnp.repeat(ret_decay_f[0], RET_DK)[None, :], jnp.repeat(ret_decay_b[0], RET_DK)[None, :],
                    ret_gn_w[0][None, :])

    wo = w_out[0].astype(BF16)
    w_router = _lane_pad(jnp.concatenate(
        [jnp.transpose(moe_w_re[0], (1, 0, 2)).reshape(d, N_EXPERTS), moe_w_rg[0]], axis=1))
    b_router = _lane_pad(jnp.concatenate([moe_b_re[0].reshape(-1), moe_b_rg[0]])[None, :])
    x1, h2, route, lpos, seg = _outproj_router(
        yr.reshape(T, rw), ys.reshape(T, SSD_WIDTH), x.reshape(T, d), mod3,
        norm_post_mix[0][None, :], norm_pre_ffn[0][None, :], wo[:rw], wo[rw:], w_router, b_router, L)

    mb = MB_EXPERT
    nt = T // TM_OUT
    n_blocks = -(-(T * TOP_K + nt * N_EXPERTS * (SEG_ALIGN - 1) + N_EXPERTS * (mb - 1)) // mb)
    n_blocks = -(-n_blocks // EXPERT_BLOCKS_PER_STEP) * EXPERT_BLOCKS_PER_STEP
    seg_len = seg[:, 0, :N_EXPERTS]
    seg_off = seg[:, 1, :N_EXPERTS]
    used = jnp.sum(seg_len, axis=0)
    padded = (used + mb - 1) // mb * mb
    pad_end = jnp.cumsum(padded)
    pad_start = pad_end - padded
    seg_row = pad_start[None, :] + jnp.cumsum(seg_len, axis=0) - seg_len
    seg_plan = [a.reshape(-1).astype(jnp.int32)
                for a in (seg_row, seg_len // SEG_ALIGN, seg_off, jnp.sum(seg_len, axis=1))]
    zero_from = (pad_start + used).astype(jnp.int32)

    xs = _dispatch_seg([pad_end.astype(jnp.int32), zero_from] + seg_plan, lpos, route, h2, n_blocks * mb)
    yb = _experts(_expert_plan(padded, pad_end, n_blocks, mb), xs, moe_w_gate[0], moe_w_up[0], moe_w_down[0])
    out = _combine_seg(seg_plan, route, x1, mod3, norm_post_ffn[0][None, :], yb, L)
    return out.reshape(b, L, d)
```
